```python
import jax, jax.numpy as jnp
from jax import lax
import numpy as np

D_MODEL = 1024
BATCH = 8
SEQ = 4096
DEPTH = 1

EXPAND = 2
D_MIX = EXPAND * D_MODEL
D_CONV = D_MIX // 2
D_GMLP = D_MIX - D_CONV
CONV_GROUPS = 8
CONV_GROUP_DIM = D_CONV // CONV_GROUPS
GMLP_HEADS = 8
GMLP_HEAD_DIM = D_GMLP // GMLP_HEADS
CONV_WIDTH = 31
CONV_HALF = CONV_WIDTH // 2
CHUNK = 128
EPS = 1e-6
D_IN = 3 * D_CONV + 3 * D_GMLP
SPLITS = (D_CONV, 2 * D_CONV, 3 * D_CONV, 3 * D_CONV + D_GMLP, 3 * D_CONV + 2 * D_GMLP)

kernel_name = "hybrid_conv_gmlp_adaln_encoder"


def rms_norm(x, g):
    xf = x.astype(jnp.float32)
    xf = xf * lax.rsqrt(jnp.mean(xf * xf, axis=-1, keepdims=True) + EPS)
    return (xf * g.astype(jnp.float32)).astype(x.dtype)


def layer_norm(x, g, b):
    xf = x.astype(jnp.float32)
    mu = jnp.mean(xf, axis=-1, keepdims=True)
    var = jnp.mean(jnp.square(xf - mu), axis=-1, keepdims=True)
    y = (xf - mu) * lax.rsqrt(var + EPS) * g.astype(jnp.float32) + b.astype(jnp.float32)
    return y.astype(x.dtype)


def _fwd_setup_inputs(seed: int = 0) -> dict:
    key = jax.random.key(seed)
    ks = jax.random.split(key, 16)
    L = DEPTH
    nrm = jax.random.normal
    return {
        "x": nrm(ks[0], (BATCH, SEQ, D_MODEL), jnp.float32),
        "c": nrm(ks[1], (BATCH, D_MODEL), jnp.float32),
        "w_ada": nrm(ks[2], (L, D_MODEL, 3 * D_MODEL), jnp.float32) * D_MODEL ** -0.5,
        "b_ada": nrm(ks[3], (L, 3 * D_MODEL), jnp.float32) * 0.02,
        "norm_g": 1.0 + 0.02 * nrm(ks[4], (L, D_MODEL), jnp.float32),
        "w_in": nrm(ks[5], (L, D_MODEL, D_IN), jnp.float32) * D_MODEL ** -0.5,
        "conv_w": nrm(ks[6], (L, CONV_WIDTH, 1, D_CONV), jnp.float32) * CONV_WIDTH ** -0.5,
        "conv_b": nrm(ks[7], (L, D_CONV), jnp.float32) * 0.02,
        "conv_ln_g": 1.0 + 0.02 * nrm(ks[8], (L, D_CONV), jnp.float32),
        "conv_ln_b": nrm(ks[9], (L, D_CONV), jnp.float32) * 0.02,
        "sg_ln_g": 1.0 + 0.02 * nrm(ks[10], (L, D_GMLP), jnp.float32),
        "sg_ln_b": nrm(ks[11], (L, D_GMLP), jnp.float32) * 0.02,
        "w_s": nrm(ks[12], (L, GMLP_HEADS, CHUNK, CHUNK), jnp.float32) * CHUNK ** -0.5,
        "b_s": 1.0 + 0.02 * nrm(ks[13], (L, GMLP_HEADS, CHUNK), jnp.float32),
        "w_out": nrm(ks[14], (L, D_MIX, D_MODEL), jnp.float32) * D_MIX ** -0.5,
        "final_g": 1.0 + 0.02 * nrm(ks[15], (D_MODEL,), jnp.float32),
    }


def conv_group(a, a_glu, a_gate, conv_w, conv_b, ln_g, ln_b):
    a = a * jax.nn.sigmoid(a_glu)
    a = lax.conv_general_dilated(
        a, conv_w.astype(a.dtype), window_strides=(1,),
        padding=[(CONV_HALF, CONV_HALF)],
        dimension_numbers=("NWC", "WIO", "NWC"),
        feature_group_count=D_CONV) + conv_b
    a = jax.nn.silu(layer_norm(a, ln_g, ln_b))
    return a * jax.nn.silu(a_gate)


def gmlp_group(u, v, b_gate, ln_g, ln_b, w_s, b_s):
    B, S, _ = v.shape
    v = layer_norm(v, ln_g, ln_b)
    v = v.reshape(B, S // CHUNK, CHUNK, GMLP_HEADS, GMLP_HEAD_DIM)
    v = jnp.einsum("hpq,bnqhd->bnphd", w_s.astype(v.dtype), v) \
        + jnp.transpose(b_s)[None, None, :, :, None]
    v = v.reshape(B, S, D_GMLP)
    return u * v * jax.nn.silu(b_gate)


def _fwd_reference(x, c, w_ada, b_ada, norm_g, w_in, conv_w, conv_b, conv_ln_g, conv_ln_b,
              sg_ln_g, sg_ln_b, w_s, b_s, w_out, final_g):
    c_act = jax.nn.silu(c)
    for l in range(DEPTH):
        mod = jnp.einsum("bd,de->be", c_act, w_ada[l]) + b_ada[l]
        shift, scale, gate = jnp.split(mod, 3, axis=-1)
        h = rms_norm(x, norm_g[l]) * (1.0 + scale[:, None, :]) + shift[:, None, :]
        z = jnp.einsum("bsd,de->bse", h, w_in[l])
        a, a_glu, a_gate, u, v, b_gate = jnp.split(z, SPLITS, axis=-1)
        y_a = conv_group(a, a_glu, a_gate, conv_w[l], conv_b[l], conv_ln_g[l], conv_ln_b[l])
        y_b = gmlp_group(u, v, b_gate, sg_ln_g[l], sg_ln_b[l], w_s[l], b_s[l])
        y = jnp.einsum("bse,ed->bsd", jnp.concatenate([y_a, y_b], axis=-1), w_out[l])
        x = x + gate[:, None, :] * y
    return rms_norm(x, final_g)


import jax as _jax
import jax.numpy as _jnp

TWIN_FORMAT = 'train_step'
FWD_PARAMS = ['x', 'c', 'w_ada', 'b_ada', 'norm_g', 'w_in', 'conv_w', 'conv_b', 'conv_ln_g', 'conv_ln_b', 'sg_ln_g', 'sg_ln_b', 'w_s', 'b_s', 'w_out', 'final_g']
TWIN_WEIGHTS = ['w_ada', 'b_ada', 'norm_g', 'w_in', 'conv_w', 'conv_b', 'conv_ln_g', 'conv_ln_b', 'sg_ln_g', 'sg_ln_b', 'w_s', 'b_s', 'w_out', 'final_g']
TWIN_DIFF_INPUT = 'x'
TWIN_INPUTS = ['x', 'c', 'w_ada', 'b_ada', 'norm_g', 'w_in', 'conv_w', 'conv_b', 'conv_ln_g', 'conv_ln_b', 'sg_ln_g', 'sg_ln_b', 'w_s', 'b_s', 'w_out', 'final_g', 'loss_target', 'm_w_ada', 'm_b_ada', 'm_norm_g', 'm_w_in', 'm_conv_w', 'm_conv_b', 'm_conv_ln_g', 'm_conv_ln_b', 'm_sg_ln_g', 'm_sg_ln_b', 'm_w_s', 'm_b_s', 'm_w_out', 'm_final_g', 'v_w_ada', 'v_b_ada', 'v_norm_g', 'v_w_in', 'v_conv_w', 'v_conv_b', 'v_conv_ln_g', 'v_conv_ln_b', 'v_sg_ln_g', 'v_sg_ln_b', 'v_w_s', 'v_b_s', 'v_w_out', 'v_final_g']
TWIN_OUTPUTS = ['loss', 'grad_x', 'grad_w_ada', 'grad_b_ada', 'grad_norm_g', 'grad_w_in', 'grad_conv_w', 'grad_conv_b', 'grad_conv_ln_g', 'grad_conv_ln_b', 'grad_sg_ln_g', 'grad_sg_ln_b', 'grad_w_s', 'grad_b_s', 'grad_w_out', 'grad_final_g', 'delta_w_ada', 'delta_b_ada', 'delta_norm_g', 'delta_w_in', 'delta_conv_w', 'delta_conv_b', 'delta_conv_ln_g', 'delta_conv_ln_b', 'delta_sg_ln_g', 'delta_sg_ln_b', 'delta_w_s', 'delta_b_s', 'delta_w_out', 'delta_final_g', 'new_m_w_ada', 'new_m_b_ada', 'new_m_norm_g', 'new_m_w_in', 'new_m_conv_w', 'new_m_conv_b', 'new_m_conv_ln_g', 'new_m_conv_ln_b', 'new_m_sg_ln_g', 'new_m_sg_ln_b', 'new_m_w_s', 'new_m_b_s', 'new_m_w_out', 'new_m_final_g', 'new_v_w_ada', 'new_v_b_ada', 'new_v_norm_g', 'new_v_w_in', 'new_v_conv_w', 'new_v_conv_b', 'new_v_conv_ln_g', 'new_v_conv_ln_b', 'new_v_sg_ln_g', 'new_v_sg_ln_b', 'new_v_w_s', 'new_v_b_s', 'new_v_w_out', 'new_v_final_g']
TWIN_LEAF_KINDS = {'loss': 'loss', 'grad_x': 'grad_x', 'grad_w_ada': 'grad_w', 'grad_b_ada': 'grad_w', 'grad_norm_g': 'grad_w', 'grad_w_in': 'grad_w', 'grad_conv_w': 'grad_w', 'grad_conv_b': 'grad_w', 'grad_conv_ln_g': 'grad_w', 'grad_conv_ln_b': 'grad_w', 'grad_sg_ln_g': 'grad_w', 'grad_sg_ln_b': 'grad_w', 'grad_w_s': 'grad_w', 'grad_b_s': 'grad_w', 'grad_w_out': 'grad_w', 'grad_final_g': 'grad_w', 'delta_w_ada': 'delta_w', 'delta_b_ada': 'delta_w', 'delta_norm_g': 'delta_w', 'delta_w_in': 'delta_w', 'delta_conv_w': 'delta_w', 'delta_conv_b': 'delta_w', 'delta_conv_ln_g': 'delta_w', 'delta_conv_ln_b': 'delta_w', 'delta_sg_ln_g': 'delta_w', 'delta_sg_ln_b': 'delta_w', 'delta_w_s': 'delta_w', 'delta_b_s': 'delta_w', 'delta_w_out': 'delta_w', 'delta_final_g': 'delta_w', 'new_m_w_ada': 'new_m', 'new_m_b_ada': 'new_m', 'new_m_norm_g': 'new_m', 'new_m_w_in': 'new_m', 'new_m_conv_w': 'new_m', 'new_m_conv_b': 'new_m', 'new_m_conv_ln_g': 'new_m', 'new_m_conv_ln_b': 'new_m', 'new_m_sg_ln_g': 'new_m', 'new_m_sg_ln_b': 'new_m', 'new_m_w_s': 'new_m', 'new_m_b_s': 'new_m', 'new_m_w_out': 'new_m', 'new_m_final_g': 'new_m', 'new_v_w_ada': 'new_v', 'new_v_b_ada': 'new_v', 'new_v_norm_g': 'new_v', 'new_v_w_in': 'new_v', 'new_v_conv_w': 'new_v', 'new_v_conv_b': 'new_v', 'new_v_conv_ln_g': 'new_v', 'new_v_conv_ln_b': 'new_v', 'new_v_sg_ln_g': 'new_v', 'new_v_sg_ln_b': 'new_v', 'new_v_w_s': 'new_v', 'new_v_b_s': 'new_v', 'new_v_w_out': 'new_v', 'new_v_final_g': 'new_v'}


def _forward(args):
    return _fwd_reference(*[args[k] for k in FWD_PARAMS])


def _output_shape():
    out = _jax.eval_shape(lambda: _forward(_fwd_setup_inputs(0)))
    return out.shape, out.dtype

N_MICROBATCH = 1
ADAM_LR = 0.001
ADAM_B1 = 0.9
ADAM_B2 = 0.999
ADAM_EPS = 1e-08
ADAM_WD = 0.01
ADAM_STEP = 10
PER_EXAMPLE_BATCH_AXIS = {'x': 0, 'c': 0, 'loss_target': 0}
SHARED_INPUTS = []
_WEIGHT_DTYPES = {'w_ada': _jnp.float32, 'b_ada': _jnp.float32, 'norm_g': _jnp.float32, 'w_in': _jnp.float32, 'conv_w': _jnp.float32, 'conv_b': _jnp.float32, 'conv_ln_g': _jnp.float32, 'conv_ln_b': _jnp.float32, 'sg_ln_g': _jnp.float32, 'sg_ln_b': _jnp.float32, 'w_s': _jnp.float32, 'b_s': _jnp.float32, 'w_out': _jnp.float32, 'final_g': _jnp.float32}
MOMENT_SCALE = {'w_ada': 1.143819e-01, 'b_ada': 2.078001e-01, 'norm_g': 1.467693e-01, 'w_in': 6.893904e-02, 'conv_w': 3.389688e-02, 'conv_b': 4.930119e-02, 'conv_ln_g': 3.947191e-02, 'conv_ln_b': 3.729092e-02, 'sg_ln_g': 8.076762e-02, 'sg_ln_b': 7.612168e-02, 'w_s': 7.151820e-02, 'b_s': 7.082810e-02, 'w_out': 1.047647e-01, 'final_g': 3.295085e+01}


def _to_microbatches(a, axis):
    t = _jnp.moveaxis(a, axis, 0)
    t = t.reshape((N_MICROBATCH, t.shape[0] // N_MICROBATCH) + t.shape[1:])
    return _jnp.moveaxis(t, 1, axis + 1)


def setup_inputs(seed: int = 0) -> dict:
    inp = _fwd_setup_inputs(seed)
    key = _jax.random.fold_in(_jax.random.key(seed), 7919)
    shape, _ = _output_shape()
    out = dict(inp)
    out["loss_target"] = _jax.random.normal(_jax.random.fold_in(key, 0), shape, _jnp.float32)
    for i, name in enumerate(TWIN_WEIGHTS):
        w = inp[name].astype(_jnp.float32)
        if MOMENT_SCALE is None:
            s = _jnp.sqrt(_jnp.mean(_jnp.square(w)) + 1e-30)
        else:
            s = MOMENT_SCALE[name]
        km, kv = _jax.random.split(_jax.random.fold_in(key, i + 1))
        out[name] = w
        out["m_" + name] = s * _jax.random.normal(km, w.shape, _jnp.float32)
        out["v_" + name] = (s * s) * _jax.random.uniform(kv, w.shape, _jnp.float32, 0.5, 1.5)
    if N_MICROBATCH > 1:
        for name, axis in PER_EXAMPLE_BATCH_AXIS.items():
            out[name] = _to_microbatches(out[name], axis)
    return {'x': out['x'], 'c': out['c'], 'w_ada': out['w_ada'], 'b_ada': out['b_ada'], 'norm_g': out['norm_g'], 'w_in': out['w_in'], 'conv_w': out['conv_w'], 'conv_b': out['conv_b'], 'conv_ln_g': out['conv_ln_g'], 'conv_ln_b': out['conv_ln_b'], 'sg_ln_g': out['sg_ln_g'], 'sg_ln_b': out['sg_ln_b'], 'w_s': out['w_s'], 'b_s': out['b_s'], 'w_out': out['w_out'], 'final_g': out['final_g'], 'loss_target': out['loss_target'], 'm_w_ada': out['m_w_ada'], 'm_b_ada': out['m_b_ada'], 'm_norm_g': out['m_norm_g'], 'm_w_in': out['m_w_in'], 'm_conv_w': out['m_conv_w'], 'm_conv_b': out['m_conv_b'], 'm_conv_ln_g': out['m_conv_ln_g'], 'm_conv_ln_b': out['m_conv_ln_b'], 'm_sg_ln_g': out['m_sg_ln_g'], 'm_sg_ln_b': out['m_sg_ln_b'], 'm_w_s': out['m_w_s'], 'm_b_s': out['m_b_s'], 'm_w_out': out['m_w_out'], 'm_final_g': out['m_final_g'], 'v_w_ada': out['v_w_ada'], 'v_b_ada': out['v_b_ada'], 'v_norm_g': out['v_norm_g'], 'v_w_in': out['v_w_in'], 'v_conv_w': out['v_conv_w'], 'v_conv_b': out['v_conv_b'], 'v_conv_ln_g': out['v_conv_ln_g'], 'v_conv_ln_b': out['v_conv_ln_b'], 'v_sg_ln_g': out['v_sg_ln_g'], 'v_sg_ln_b': out['v_sg_ln_b'], 'v_w_s': out['v_w_s'], 'v_b_s': out['v_b_s'], 'v_w_out': out['v_w_out'], 'v_final_g': out['v_final_g']}


def _loss(weights, diff, rest, loss_target):
    with _jax.named_scope("forward"):
        args = {**rest, TWIN_DIFF_INPUT: diff, **{k: w.astype(_WEIGHT_DTYPES[k]) for k, w in weights.items()}}
        y = _forward(args)
    with _jax.named_scope("loss_head"):
        err = _jnp.square(y.astype(_jnp.float32) - loss_target)
        return 0.5 * _jnp.sum(_jnp.mean(err, axis=-1)) if err.ndim else 0.5 * err


def _adamw(w, g, m, v):
    m = ADAM_B1 * m + (1.0 - ADAM_B1) * g
    v = ADAM_B2 * v + (1.0 - ADAM_B2) * _jnp.square(g)
    m_hat = m / (1.0 - ADAM_B1 ** ADAM_STEP)
    v_hat = v / (1.0 - ADAM_B2 ** ADAM_STEP)
    delta = -ADAM_LR * (m_hat / (_jnp.sqrt(v_hat) + ADAM_EPS) + ADAM_WD * w)
    return delta, m, v


def reference(x, c, w_ada, b_ada, norm_g, w_in, conv_w, conv_b, conv_ln_g, conv_ln_b, sg_ln_g, sg_ln_b, w_s, b_s, w_out, final_g, loss_target, m_w_ada, m_b_ada, m_norm_g, m_w_in, m_conv_w, m_conv_b, m_conv_ln_g, m_conv_ln_b, m_sg_ln_g, m_sg_ln_b, m_w_s, m_b_s, m_w_out, m_final_g, v_w_ada, v_b_ada, v_norm_g, v_w_in, v_conv_w, v_conv_b, v_conv_ln_g, v_conv_ln_b, v_sg_ln_g, v_sg_ln_b, v_w_s, v_b_s, v_w_out, v_final_g):
    given = dict(x=x, c=c, w_ada=w_ada, b_ada=b_ada, norm_g=norm_g, w_in=w_in, conv_w=conv_w, conv_b=conv_b, conv_ln_g=conv_ln_g, conv_ln_b=conv_ln_b, sg_ln_g=sg_ln_g, sg_ln_b=sg_ln_b, w_s=w_s, b_s=b_s, w_out=w_out, final_g=final_g, loss_target=loss_target, m_w_ada=m_w_ada, m_b_ada=m_b_ada, m_norm_g=m_norm_g, m_w_in=m_w_in, m_conv_w=m_conv_w, m_conv_b=m_conv_b, m_conv_ln_g=m_conv_ln_g, m_conv_ln_b=m_conv_ln_b, m_sg_ln_g=m_sg_ln_g, m_sg_ln_b=m_sg_ln_b, m_w_s=m_w_s, m_b_s=m_b_s, m_w_out=m_w_out, m_final_g=m_final_g, v_w_ada=v_w_ada, v_b_ada=v_b_ada, v_norm_g=v_norm_g, v_w_in=v_w_in, v_conv_w=v_conv_w, v_conv_b=v_conv_b, v_conv_ln_g=v_conv_ln_g, v_conv_ln_b=v_conv_ln_b, v_sg_ln_g=v_sg_ln_g, v_sg_ln_b=v_sg_ln_b, v_w_s=v_w_s, v_b_s=v_b_s, v_w_out=v_w_out, v_final_g=v_final_g)
    weights = {n: given[n] for n in TWIN_WEIGHTS}
    shared = {n: given[n] for n in SHARED_INPUTS}
    per_example = {n: given[n] for n in ['x', 'c']}
    grad_fn = _jax.value_and_grad(_loss, argnums=(0, 1))

    def one_microbatch(ex, loss_target):
        ex = dict(ex)
        diff = ex.pop(TWIN_DIFF_INPUT)
        return grad_fn(weights, diff, {**shared, **ex}, loss_target)

    if N_MICROBATCH == 1:
        loss, (grad_w, grad_x) = one_microbatch(per_example, given["loss_target"])
    else:
        def body(carry, xs):
            loss_sum, grad_sum = carry
            l_k, (gw_k, gx_k) = one_microbatch(xs[0], xs[1])
            with _jax.named_scope("update"):
                return (loss_sum + l_k, _jax.tree.map(_jnp.add, grad_sum, gw_k)), gx_k

        init = (_jnp.zeros((), _jnp.float32), _jax.tree.map(_jnp.zeros_like, weights))
        (loss, grad_w), grad_x = _jax.lax.scan(body, init, (per_example, given["loss_target"]))
    with _jax.named_scope("update"):
        delta_w, new_m, new_v = {}, {}, {}
        for n in TWIN_WEIGHTS:
            delta_w[n], new_m[n], new_v[n] = _adamw(weights[n], grad_w[n], given["m_" + n], given["v_" + n])
    return (loss, grad_x, *[grad_w[n] for n in TWIN_WEIGHTS], *[delta_w[n] for n in TWIN_WEIGHTS],
            *[new_m[n] for n in TWIN_WEIGHTS], *[new_v[n] for n in TWIN_WEIGHTS])
```

```python
import functools

import jax
import jax.numpy as jnp
from jax import lax
from jax.experimental import pallas as pl
from jax.experimental.pallas import tpu as pltpu

F32 = jnp.float32
BF16 = jnp.bfloat16
MESH = pl.DeviceIdType.MESH

D_MODEL = 1024
N_CHIPS = 4
HEADS = 8
CHUNK = 128
CONV_WIDTH = 31
CONV_HALF = CONV_WIDTH // 2
CONV_PAD = 16
EPS = 1e-6
ADAM_LR = 0.001
ADAM_B1 = 0.9
ADAM_B2 = 0.999
ADAM_EPS = 1e-08
ADAM_WD = 0.01
ADAM_STEP = 10

V7X_VMEM_BYTES = 64 * 1024 * 1024
VMEM_LIMIT = V7X_VMEM_BYTES - 8 * 1024 * 1024
ROWS = 16
TOKEN_TILE = 256
TIME_TILE = 128
K_TILE = 2048

N_GROUPS = 6


def _natural_group(j):
    return (j + 2) % N_GROUPS


def _pos():
    return lax.axis_index("x"), lax.axis_index("y"), lax.axis_index("c")


def _rcopy(src, dst, ssem, rsem, dev):
    return pltpu.make_async_remote_copy(src_ref=src, dst_ref=dst, send_sem=ssem, recv_sem=rsem,
                                        device_id=dev, device_id_type=MESH)


def _vmem():
    return pl.BlockSpec(memory_space=pltpu.VMEM)


def _params(**kw):
    return pltpu.CompilerParams(vmem_limit_bytes=VMEM_LIMIT, **kw)


def _sigmoid(v):
    return jax.nn.sigmoid(v)


def _row_loop(n_rows, body):
    def step(r, carry):
        body(pl.ds(pl.multiple_of(r * ROWS, ROWS), ROWS))
        return carry
    lax.fori_loop(0, n_rows // ROWS, step, 0)


def _colsum8(v):
    return v.reshape(v.shape[0] // 8, 8, v.shape[1]).sum(axis=0)


def _mean(v):
    return jnp.mean(v, axis=-1, keepdims=True)


def _dot_nn(a, b):
    return jnp.dot(a, b, preferred_element_type=F32)


def _dot_nt(a, b):
    return lax.dot_general(a, b, (((1,), (1,)), ((), ())), preferred_element_type=F32)


def _dot_tn(a, b):
    return lax.dot_general(a, b, (((0,), (0,)), ((), ())), preferred_element_type=F32)


def _gather_weights(w_in, w_out, cw):
    def body(win_ref, wout_ref, cw_ref, win12_ref, wout4_ref, cw4_ref, ssem, rsem):
        x, y, c = _pos()
        k = 2 * x + y
        sib = (x, y, 1 - c)
        others = [(1 - x, y), (x, 1 - y), (1 - x, 1 - y)]
        for j in range(3):
            for hf in range(2):
                win12_ref[3 * k + j, hf] = win_ref[hf * 512:(hf + 1) * 512, j * 512:(j + 1) * 512].astype(BF16)
        for hf in range(2):
            wout4_ref[k, hf] = wout_ref[hf * 256:(hf + 1) * 256, :].astype(BF16)
        cw4_ref[k] = cw_ref[...]

        def win_slab(kk, hf):
            return win12_ref.at[pl.ds(3 * kk, 3), hf]

        def wout_slab(kk, hf):
            return wout4_ref.at[kk, hf]

        first = []
        for j, (ox, oy) in enumerate(others):
            dev = (ox, oy, c)
            first.append(_rcopy(win_slab(k, c), win_slab(k, c), ssem.at[j], rsem.at[j], dev))
            first.append(_rcopy(wout_slab(k, c), wout_slab(k, c), ssem.at[3 + j], rsem.at[3 + j], dev))
            first.append(_rcopy(cw4_ref.at[k], cw4_ref.at[k], ssem.at[6 + j], rsem.at[6 + j], dev))
        for cp in first:
            cp.start()
        passed = []
        for j, (ox, oy) in enumerate(others):
            ko = 2 * ox + oy
            dev = (ox, oy, c)
            _rcopy(win_slab(ko, c), win_slab(ko, c), ssem.at[j], rsem.at[j], dev).wait_recv()
            f1 = _rcopy(win_slab(ko, c), win_slab(ko, c), ssem.at[9 + j], rsem.at[9 + j], sib)
            f1.start()
            _rcopy(wout_slab(ko, c), wout_slab(ko, c), ssem.at[3 + j], rsem.at[3 + j], dev).wait_recv()
            f2 = _rcopy(wout_slab(ko, c), wout_slab(ko, c), ssem.at[12 + j], rsem.at[12 + j], sib)
            f2.start()
            passed += [f1, f2]
        for j, (ox, oy) in enumerate(others):
            ko = 2 * ox + oy
            _rcopy(cw4_ref.at[ko], cw4_ref.at[ko], ssem.at[6 + j], rsem.at[6 + j], (ox, oy, c)).wait_recv()
            _rcopy(win_slab(ko, 1 - c), win_slab(ko, 1 - c), ssem.at[9 + j], rsem.at[9 + j], sib).wait_recv()
            _rcopy(wout_slab(ko, 1 - c), wout_slab(ko, 1 - c), ssem.at[12 + j], rsem.at[12 + j], sib).wait_recv()
        for cp in first + passed:
            cp.wait_send()

    return pl.pallas_call(
        body, name="gather_weights",
        out_shape=(jax.ShapeDtypeStruct((12, 2, 512, 512), BF16),
                   jax.ShapeDtypeStruct((N_CHIPS, 2, 256, D_MODEL), BF16),
                   jax.ShapeDtypeStruct((N_CHIPS, 32, 256), F32)),
        in_specs=[_vmem(), _vmem(), _vmem()],
        out_specs=(_vmem(), _vmem(), _vmem()),
        scratch_shapes=[pltpu.SemaphoreType.DMA((15,)), pltpu.SemaphoreType.DMA((15,))],
        compiler_params=_params(),
    )(w_in, w_out, cw)


def _butterfly(bufs, recvs, ssem, rsem, partners, first_sem=0):
    s = first_sem
    for step, dev in enumerate(partners):
        cps = []
        for buf, recv in zip(bufs, recvs):
            cp = _rcopy(buf, recv.at[step], ssem.at[s], rsem.at[s], dev)
            cp.start()
            cps.append(cp)
            s += 1
        for cp in cps:
            cp.wait()
        for buf, recv in zip(bufs, recvs):
            buf[...] = buf[...] + recv[step]
    return s


def _mod(c, w_ada, b_ada):
    def body(c_ref, wada_ref, bada_ref, mod_ref, cact_ref, crecv, mbuf, mrecv, ssem, rsem):
        x, y, c_ = _pos()
        k = 2 * x + y
        b = 4 * x + 2 * y + c_
        row = lax.broadcasted_iota(jnp.int32, (8, D_MODEL), 0)
        cact_ref[...] = jnp.where(row == b, jnp.broadcast_to(c_ref[...], (8, D_MODEL)), 0.0)
        n = _butterfly([cact_ref], [crecv], ssem, rsem, [(x, y, 1 - c_), (x, 1 - y, c_), (1 - x, y, c_)])
        call = cact_ref[...]
        cact = call * _sigmoid(call)
        cact_ref[...] = cact
        part = _dot_nn(cact.astype(BF16), wada_ref[...].astype(BF16))
        mbuf[...] = jnp.zeros_like(mbuf)
        mbuf[k] = part
        _butterfly([mbuf], [mrecv], ssem, rsem, [(x, 1 - y, c_), (1 - x, y, c_)], first_sem=n)
        row8 = lax.broadcasted_iota(jnp.int32, (8, 768), 0)
        for kk in range(N_CHIPS):
            piece = jnp.sum(jnp.where(row8 == b, mbuf[kk], 0.0), axis=0, keepdims=True)
            mod_ref[:, kk * 768:(kk + 1) * 768] = piece + bada_ref[:, kk * 768:(kk + 1) * 768]

    return pl.pallas_call(
        body, name="mod",
        out_shape=(jax.ShapeDtypeStruct((1, 3 * D_MODEL), F32), jax.ShapeDtypeStruct((8, D_MODEL), F32)),
        in_specs=[_vmem(), _vmem(), _vmem()],
        out_specs=(_vmem(), _vmem()),
        scratch_shapes=[pltpu.VMEM((3, 8, D_MODEL), F32), pltpu.VMEM((N_CHIPS, 8, 768), F32),
                        pltpu.VMEM((2, N_CHIPS, 8, 768), F32),
                        pltpu.SemaphoreType.DMA((5,)), pltpu.SemaphoreType.DMA((5,))],
        compiler_params=_params(),
    )(c, w_ada, b_ada)


def _reduce_small(sm1, sm2, dws, dbs, dcw):
    def body(sm1_ref, sm2_ref, dws_in, dbs_in, dcw_in, pack_ref, dws_ref, dbs_ref, dcw_ref, dcwsh_ref, dcb_ref, loss_ref,
             r_pack, r_dws, r_dbs, r_dcw, ssem, rsem):
        x, y, c = _pos()
        k = 2 * x + y
        b = 4 * x + 2 * y + c
        row = lax.broadcasted_iota(jnp.int32, (8, D_MODEL), 0)

        def onehot(v):
            return jnp.where(row == b, jnp.broadcast_to(v, (8, D_MODEL)), 0.0)

        pack_ref[0:8, :] = onehot(sm2_ref[0:1, :])
        pack_ref[8:16, :] = onehot(sm2_ref[1:2, :])
        pack_ref[16:24, :] = onehot(sm1_ref[1:2, :])
        order = [sm2_ref[2:3, :], sm1_ref[2:3, :], sm1_ref[3:4, :], sm1_ref[4:5, :], sm1_ref[5:6, :],
                 sm1_ref[0:1, :], sm1_ref[6:7, :], jnp.zeros((1, D_MODEL), F32)]
        for i, v in enumerate(order):
            pack_ref[24 + i:25 + i, :] = v
        dws_ref[...] = dws_in[...]
        dbs_ref[...] = dbs_in[...]
        dcw_ref[...] = dcw_in[...]
        _butterfly([pack_ref, dws_ref, dbs_ref, dcw_ref], [r_pack, r_dws, r_dbs, r_dcw], ssem, rsem,
                   [(x, y, 1 - c), (x, 1 - y, c), (1 - x, y, c)])
        sel = jnp.zeros((32, 256), F32)
        for kk in range(N_CHIPS):
            sel = jnp.where(k == kk, dcw_ref[:, kk * 256:(kk + 1) * 256], sel)
        dcwsh_ref[...] = sel
        dcb_ref[...] = dcw_ref[31:32, :]
        loss_ref[...] = jnp.broadcast_to(jnp.sum(pack_ref[30:31, :], axis=1, keepdims=True), (8, 128))

    shp = [(32, D_MODEL), (D_MODEL, 128), (128, 128), (32, D_MODEL)]
    return pl.pallas_call(
        body, name="reduce_small",
        out_shape=tuple(jax.ShapeDtypeStruct(s, F32) for s in shp) + (
            jax.ShapeDtypeStruct((32, 256), F32), jax.ShapeDtypeStruct((1, D_MODEL), F32),
            jax.ShapeDtypeStruct((8, 128), F32)),
        in_specs=[_vmem()] * 5,
        out_specs=tuple(_vmem() for _ in range(7)),
        scratch_shapes=[pltpu.VMEM((3,) + s, F32) for s in shp] + [
            pltpu.SemaphoreType.DMA((12,)), pltpu.SemaphoreType.DMA((12,))],
        compiler_params=_params(),
    )(sm1, sm2, dws, dbs, dcw)


def _reduce_scatter(g5, name):
    _, nj, _, nr, nc = g5.shape

    def body(g_ref, out_ref, own, recv_a, recv_b, lsem, ssem, rsem):
        x, y, c = _pos()
        k = 2 * x + y
        sib = (x, y, 1 - c)
        others = [(1 - x, y), (x, 1 - y), (1 - x, 1 - y)]
        loc = pltpu.make_async_copy(g_ref.at[:, :, c], own, lsem)
        loc.start()
        pair = _rcopy(g_ref.at[:, :, 1 - c], recv_a, ssem.at[0], rsem.at[0], sib)
        pair.start()
        loc.wait()
        pair.wait()

        def add_pair(i, carry):
            kk = i // nj
            j = i % nj
            own[kk, j] = own[kk, j] + recv_a[kk, j]
            return carry
        lax.fori_loop(0, N_CHIPS * nj, add_pair, 0)
        sends = []
        for r, (ox, oy) in enumerate(others):
            cp = _rcopy(own.at[2 * ox + oy], recv_b.at[r], ssem.at[1 + r], rsem.at[1 + r], (ox, oy, c))
            cp.start()
            sends.append(cp)
        for cp in sends:
            cp.wait_recv()
        for j in range(nj):
            out_ref[j, c] = ((own[k, j] + recv_b[0, j]) + recv_b[1, j]) + recv_b[2, j]
        swap = _rcopy(out_ref.at[:, c], out_ref.at[:, c], ssem.at[4], rsem.at[4], sib)
        swap.start()
        _rcopy(out_ref.at[:, 1 - c], out_ref.at[:, 1 - c], ssem.at[4], rsem.at[4], sib).wait_recv()
        swap.wait_send()
        for cp in sends:
            cp.wait_send()

    return pl.pallas_call(
        body, name=name,
        out_shape=jax.ShapeDtypeStruct((nj, 2, nr, nc), F32),
        in_specs=[pl.BlockSpec(memory_space=pl.ANY)],
        out_specs=_vmem(),
        scratch_shapes=[pltpu.VMEM((N_CHIPS, nj, nr, nc), F32), pltpu.VMEM((N_CHIPS, nj, nr, nc), F32),
                        pltpu.VMEM((3, nj, nr, nc), F32),
                        pltpu.SemaphoreType.DMA, pltpu.SemaphoreType.DMA((5,)), pltpu.SemaphoreType.DMA((5,))],
        compiler_params=_params(),
    )(g5)


def _fwd_in(x, mod, norm_g, w12):
    s_len = x.shape[0]
    tm = TOKEN_TILE

    def body(x_ref, mod_ref, g_ref, w_ref, h_ref, z_ref):
        shift = mod_ref[:, 0:D_MODEL]
        scale1 = 1.0 + mod_ref[:, D_MODEL:2 * D_MODEL]
        g = g_ref[...]

        def rows_fn(rows):
            xt = x_ref[rows, :]
            r = lax.rsqrt(_mean(xt * xt) + EPS)
            h_ref[rows, :] = ((xt * r * g) * scale1 + shift).astype(BF16)
        _row_loop(tm, rows_fn)
        hb = h_ref[...]
        for j in range(N_GROUPS):
            n = _natural_group(j)
            for hf in range(2):
                z_ref[j, :, hf * 512:(hf + 1) * 512] = _dot_nn(hb, w_ref[2 * n + hf])

    return pl.pallas_call(
        body, name="fwd_in",
        grid=(s_len // tm,),
        out_shape=(jax.ShapeDtypeStruct((s_len, D_MODEL), BF16), jax.ShapeDtypeStruct((N_GROUPS, s_len, D_MODEL), F32)),
        in_specs=[pl.BlockSpec((tm, D_MODEL), lambda i: (i, 0)),
                  pl.BlockSpec((1, 3 * D_MODEL), lambda i: (0, 0)),
                  pl.BlockSpec((1, D_MODEL), lambda i: (0, 0)),
                  pl.BlockSpec((12, D_MODEL, 512), lambda i: (0, 0, 0), pipeline_mode=pl.Buffered(1))],
        out_specs=(pl.BlockSpec((tm, D_MODEL), lambda i: (i, 0)),
                   pl.BlockSpec((N_GROUPS, tm, D_MODEL), lambda i: (0, i, 0))),
        compiler_params=_params(dimension_semantics=("arbitrary",)),
    )(x, mod, norm_g, w12)


def _shifted_windows(win, n_out):
    n = win.shape[0]
    for s in range(8):
        ws = win if s == 0 else pltpu.roll(win, n - s, axis=0)
        for a in range(4):
            o = 8 * a + s
            if 1 <= o <= CONV_WIDTH:
                yield o, ws[8 * a:8 * a + n_out, :]


def _conv_fwd(z6, cw4, conv_b):
    s_len = z6.shape[1]
    tt = TIME_TILE

    def body(z_ref, cw_ref, cb_ref, q_ref, ppad):
        zero = jnp.zeros((CONV_PAD, 128), F32)
        ppad[0:CONV_PAD, :] = zero
        ppad[s_len + CONV_PAD:s_len + 2 * CONV_PAD, :] = zero

        def fill(i, carry):
            t0 = pl.multiple_of(i * tt, tt)
            ppad[pl.ds(CONV_PAD + t0, tt), :] = z_ref[0, pl.ds(t0, tt), :] * _sigmoid(z_ref[1, pl.ds(t0, tt), :])
            return carry
        lax.fori_loop(0, s_len // tt, fill, 0)
        w = cw_ref[0]
        bias = cb_ref[...]

        def conv(i, carry):
            t0 = pl.multiple_of(i * tt, tt)
            win = ppad[pl.ds(t0, tt + 2 * CONV_PAD), :]
            acc = jnp.broadcast_to(bias, (tt, 128))
            for o, sl in _shifted_windows(win, tt):
                acc = acc + w[o - 1:o, :] * sl
            q_ref[pl.ds(t0, tt), :] = acc
            return carry
        lax.fori_loop(0, s_len // tt, conv, 0)

    return pl.pallas_call(
        body, name="conv_fwd",
        grid=(D_MODEL // 128,),
        out_shape=jax.ShapeDtypeStruct((s_len, D_MODEL), F32),
        in_specs=[pl.BlockSpec((2, s_len, 128), lambda j: (2, 0, j)),
                  pl.BlockSpec((1, 32, 128), lambda j: (j // 2, 0, j % 2)),
                  pl.BlockSpec((1, 128), lambda j: (0, j))],
        out_specs=pl.BlockSpec((s_len, 128), lambda j: (0, j)),
        scratch_shapes=[pltpu.VMEM((s_len + 2 * CONV_PAD, 128), F32)],
        compiler_params=_params(dimension_semantics=("arbitrary",)),
    )(z6, cw4, conv_b)


def _middle(x, z6, q, target, ln_rows, mod, w_s, bs_exp, w_out):
    s_len = x.shape[0]
    tm = TOKEN_TILE
    n_steps = s_len // tm
    n_chunks = tm // CHUNK
    inv_d = 1.0 / D_MODEL

    def body(x_ref, z_ref, q_ref, tgt_ref, cg_ref, cb_ref, sg_ref, sb_ref, fg_ref, mod_ref, ws_ref, bs_ref, wout_ref,
             dz_ref, dq_ref, dx2_ref, ycat_ref, dy_ref, dws_ref, sm_ref, dbs_ref,
             vl_scr, vm_scr, y_scr, dycat_scr, dvm_scr, dvl_scr, acc_scr, dbs_acc):
        i = pl.program_id(0)

        @pl.when(i == 0)
        def _():
            acc_scr[...] = jnp.zeros_like(acc_scr)
            dbs_acc[...] = jnp.zeros_like(dbs_acc)
            dws_ref[...] = jnp.zeros_like(dws_ref)

        cg, cb, sg, sb, fg = cg_ref[...], cb_ref[...], sg_ref[...], sb_ref[...], fg_ref[...]
        gm = mod_ref[:, 2 * D_MODEL:3 * D_MODEL]

        def norm_stats(t):
            c = t - _mean(t)
            rstd = lax.rsqrt(_mean(c * c) + EPS)
            return c * rstd, rstd

        def phase1(rows):
            qhat, _ = norm_stats(q_ref[rows, :])
            ln = qhat * cg + cb
            gz = z_ref[0, rows, :]
            ycat_ref[rows, 0:D_MODEL] = ((ln * _sigmoid(ln)) * (gz * _sigmoid(gz))).astype(BF16)
            vhat, _ = norm_stats(z_ref[2, rows, :])
            vl_scr[rows, :] = (vhat * sg + sb).astype(BF16)
        _row_loop(tm, phase1)

        for ch in range(n_chunks):
            r0 = ch * CHUNK
            for h in range(HEADS):
                c0 = h * CHUNK
                vm_scr[r0:r0 + CHUNK, c0:c0 + CHUNK] = (
                    _dot_nn(ws_ref[h].astype(BF16), vl_scr[r0:r0 + CHUNK, c0:c0 + CHUNK]) + bs_ref[:, c0:c0 + CHUNK])

        def phase3(rows):
            bg = z_ref[3, rows, :]
            ycat_ref[rows, D_MODEL:2 * D_MODEL] = (z_ref[1, rows, :] * vm_scr[rows, :] * (bg * _sigmoid(bg))).astype(BF16)
        _row_loop(tm, phase3)

        y_scr[...] = _dot_nn(ycat_ref[...], wout_ref[...])

        def phase5(rows):
            y = y_scr[rows, :]
            x2 = x_ref[rows, :] + gm * y
            r2 = lax.rsqrt(_mean(x2 * x2) + EPS)
            xn2 = x2 * r2
            diff = xn2 * fg - tgt_ref[rows, :]
            acc_scr[6] += _colsum8(diff * diff)
            dout = diff * inv_d
            acc_scr[0] += _colsum8(dout * xn2)
            dxn = dout * fg
            dx2 = r2 * (dxn - xn2 * _mean(dxn * xn2))
            dx2_ref[rows, :] = dx2
            acc_scr[1] += _colsum8(dx2 * y)
            dy_ref[rows, :] = (dx2 * gm).astype(BF16)
        _row_loop(tm, phase5)

        dycat_scr[...] = _dot_nt(dy_ref[...], wout_ref[...])

        def phase7(rows):
            dyb = dycat_scr[rows, D_MODEL:2 * D_MODEL]
            u = z_ref[1, rows, :]
            bg = z_ref[3, rows, :]
            vm = vm_scr[rows, :]
            sig = _sigmoid(bg)
            silu = bg * sig
            dz_ref[1, rows, :] = (dyb * vm * silu).astype(BF16)
            dvm = dyb * u * silu
            dz_ref[3, rows, :] = (dyb * u * vm * (sig * (1.0 + bg * (1.0 - sig)))).astype(BF16)
            dvm_scr[rows, :] = dvm.astype(BF16)
            pos = pl.ds(pl.multiple_of(rows.start % CHUNK, ROWS), ROWS)
            dbs_acc[pos, :] += dvm
        _row_loop(tm, phase7)

        for ch in range(n_chunks):
            r0 = ch * CHUNK
            for h in range(HEADS):
                c0 = h * CHUNK
                dvm_b = dvm_scr[r0:r0 + CHUNK, c0:c0 + CHUNK]
                dws_ref[h] += _dot_nt(dvm_b, vl_scr[r0:r0 + CHUNK, c0:c0 + CHUNK])
                dvl_scr[r0:r0 + CHUNK, c0:c0 + CHUNK] = _dot_tn(ws_ref[h].astype(BF16), dvm_b)

        def phase9(rows):
            vhat, rstd_v = norm_stats(z_ref[2, rows, :])
            dvl = dvl_scr[rows, :]
            acc_scr[4] += _colsum8(dvl * vhat)
            acc_scr[5] += _colsum8(dvl)
            dvh = dvl * sg
            dz_ref[2, rows, :] = (rstd_v * (dvh - _mean(dvh) - vhat * _mean(dvh * vhat))).astype(BF16)
            qhat, rstd_q = norm_stats(q_ref[rows, :])
            ln = qhat * cg + cb
            sig_ln = _sigmoid(ln)
            gz = z_ref[0, rows, :]
            sig_g = _sigmoid(gz)
            dya = dycat_scr[rows, 0:D_MODEL]
            dz_ref[0, rows, :] = (dya * (ln * sig_ln) * (sig_g * (1.0 + gz * (1.0 - sig_g)))).astype(BF16)
            dln = (dya * (gz * sig_g)) * (sig_ln * (1.0 + ln * (1.0 - sig_ln)))
            acc_scr[2] += _colsum8(dln * qhat)
            acc_scr[3] += _colsum8(dln)
            dqh = dln * cg
            dq_ref[rows, :] = rstd_q * (dqh - _mean(dqh) - qhat * _mean(dqh * qhat))
        _row_loop(tm, phase9)

        @pl.when(i == n_steps - 1)
        def _():
            for qi in range(8):
                scale = 0.5 * inv_d if qi == 6 else 1.0
                sm_ref[qi:qi + 1, :] = jnp.sum(acc_scr[qi], axis=0, keepdims=True) * scale
            lane = lax.broadcasted_iota(jnp.int32, (CHUNK, CHUNK), 1)
            tile = jnp.zeros((CHUNK, CHUNK), F32)
            for h in range(HEADS):
                col = jnp.sum(dbs_acc[:, h * CHUNK:(h + 1) * CHUNK], axis=1, keepdims=True)
                tile = jnp.where(lane == h, col, tile)
            dbs_ref[...] = tile

    tok = lambda i: (i, 0)
    const2 = lambda i: (0, 0)
    return pl.pallas_call(
        body, name="middle",
        grid=(n_steps,),
        out_shape=(jax.ShapeDtypeStruct((N_GROUPS, s_len, D_MODEL), BF16),
                   jax.ShapeDtypeStruct((s_len, D_MODEL), F32),
                   jax.ShapeDtypeStruct((s_len, D_MODEL), F32),
                   jax.ShapeDtypeStruct((s_len, 2 * D_MODEL), BF16),
                   jax.ShapeDtypeStruct((s_len, D_MODEL), BF16),
                   jax.ShapeDtypeStruct((HEADS, CHUNK, CHUNK), F32),
                   jax.ShapeDtypeStruct((8, D_MODEL), F32),
                   jax.ShapeDtypeStruct((CHUNK, CHUNK), F32)),
        in_specs=[pl.BlockSpec((tm, D_MODEL), tok),
                  pl.BlockSpec((4, tm, D_MODEL), lambda i: (0, i, 0)),
                  pl.BlockSpec((tm, D_MODEL), tok),
                  pl.BlockSpec((tm, D_MODEL), tok),
                  *[pl.BlockSpec((1, D_MODEL), const2) for _ in range(5)],
                  pl.BlockSpec((1, 3 * D_MODEL), const2),
                  pl.BlockSpec((HEADS, CHUNK, CHUNK), lambda i: (0, 0, 0)),
                  pl.BlockSpec((CHUNK, D_MODEL), const2),
                  pl.BlockSpec((2 * D_MODEL, D_MODEL), const2, pipeline_mode=pl.Buffered(1))],
        out_specs=(pl.BlockSpec((4, tm, D_MODEL), lambda i: (0, i, 0)),
                   pl.BlockSpec((tm, D_MODEL), tok),
                   pl.BlockSpec((tm, D_MODEL), tok),
                   pl.BlockSpec((tm, 2 * D_MODEL), tok),
                   pl.BlockSpec((tm, D_MODEL), tok),
                   pl.BlockSpec((HEADS, CHUNK, CHUNK), lambda i: (0, 0, 0)),
                   pl.BlockSpec((8, D_MODEL), const2),
                   pl.BlockSpec((CHUNK, CHUNK), const2)),
        scratch_shapes=[pltpu.VMEM((tm, D_MODEL), BF16),
                        pltpu.VMEM((tm, D_MODEL), F32),
                        pltpu.VMEM((tm, D_MODEL), F32),
                        pltpu.VMEM((tm, 2 * D_MODEL), F32),
                        pltpu.VMEM((tm, D_MODEL), BF16),
                        pltpu.VMEM((tm, D_MODEL), F32),
                        pltpu.VMEM((8, 8, D_MODEL), F32),
                        pltpu.VMEM((CHUNK, D_MODEL), F32)],
        compiler_params=_params(dimension_semantics=("arbitrary",)),
    )(x, z6, q, target, *ln_rows, mod, w_s, bs_exp, w_out)


def _conv_bwd(dq, z6, cw4, dz6):
    s_len = dq.shape[0]
    tt = TIME_TILE

    def body(dq_ref, z_ref, cw_ref, dz_in, dz_ref, dcw_ref, dqpad, ppad, wacc):
        del dz_in
        zero = jnp.zeros((CONV_PAD, 128), F32)
        for pad in (dqpad, ppad):
            pad[0:CONV_PAD, :] = zero
            pad[s_len + CONV_PAD:s_len + 2 * CONV_PAD, :] = zero
        wacc[...] = jnp.zeros_like(wacc)

        def fill(i, carry):
            t0 = pl.multiple_of(i * tt, tt)
            ppad[pl.ds(CONV_PAD + t0, tt), :] = z_ref[0, pl.ds(t0, tt), :] * _sigmoid(z_ref[1, pl.ds(t0, tt), :])
            dqpad[pl.ds(CONV_PAD + t0, tt), :] = dq_ref[pl.ds(t0, tt), :]
            return carry
        lax.fori_loop(0, s_len // tt, fill, 0)
        w = cw_ref[0]

        def bwd(i, carry):
            t0 = pl.multiple_of(i * tt, tt)
            dp = jnp.zeros((tt, 128), F32)
            for o, sl in _shifted_windows(dqpad[pl.ds(t0, tt + 2 * CONV_PAD), :], tt):
                dp = dp + w[CONV_WIDTH - o:CONV_WIDTH - o + 1, :] * sl
            a = z_ref[0, pl.ds(t0, tt), :]
            sig = _sigmoid(z_ref[1, pl.ds(t0, tt), :])
            dz_ref[0, pl.ds(t0, tt), :] = (dp * sig).astype(BF16)
            dz_ref[1, pl.ds(t0, tt), :] = (dp * a * (sig * (1.0 - sig))).astype(BF16)
            dqt = dq_ref[pl.ds(t0, tt), :]
            for o, sl in _shifted_windows(ppad[pl.ds(t0, tt + 2 * CONV_PAD), :], tt):
                wacc[o - 1] += _colsum8(dqt * sl)
            wacc[CONV_WIDTH] += _colsum8(dqt)
            return carry
        lax.fori_loop(0, s_len // tt, bwd, 0)
        for k in range(32):
            dcw_ref[k:k + 1, :] = jnp.sum(wacc[k], axis=0, keepdims=True)

    return pl.pallas_call(
        body, name="conv_bwd",
        grid=(D_MODEL // 128,),
        out_shape=(jax.ShapeDtypeStruct(dz6.shape, BF16), jax.ShapeDtypeStruct((32, D_MODEL), F32)),
        in_specs=[pl.BlockSpec((s_len, 128), lambda j: (0, j)),
                  pl.BlockSpec((2, s_len, 128), lambda j: (2, 0, j)),
                  pl.BlockSpec((1, 32, 128), lambda j: (j // 2, 0, j % 2)),
                  pl.BlockSpec(memory_space=pl.ANY)],
        out_specs=(pl.BlockSpec((2, s_len, 128), lambda j: (2, 0, j)),
                   pl.BlockSpec((32, 128), lambda j: (0, j))),
        scratch_shapes=[pltpu.VMEM((s_len + 2 * CONV_PAD, 128), F32), pltpu.VMEM((s_len + 2 * CONV_PAD, 128), F32),
                        pltpu.VMEM((32, 8, 128), F32)],
        input_output_aliases={3: 0},
        compiler_params=_params(dimension_semantics=("arbitrary",)),
    )(dq, z6, cw4, dz6)


def _bwd_in(dz6, w12, x, dx2, mod, norm_g):
    s_len = x.shape[0]
    tm = TOKEN_TILE
    n_steps = s_len // tm

    def body(dz_ref, w_ref, x_ref, dx2_ref, mod_ref, g_ref, gx_ref, sm_ref, dh_scr, acc_scr):
        i = pl.program_id(0)

        @pl.when(i == 0)
        def _():
            acc_scr[...] = jnp.zeros_like(acc_scr)

        dh = jnp.zeros((tm, D_MODEL), F32)
        for j in range(N_GROUPS):
            n = _natural_group(j)
            for hf in range(2):
                dh = dh + _dot_nt(dz_ref[j, :, hf * 512:(hf + 1) * 512], w_ref[2 * n + hf])
        dh_scr[...] = dh
        scale1 = 1.0 + mod_ref[:, D_MODEL:2 * D_MODEL]
        g = g_ref[...]

        def rows_fn(rows):
            xt = x_ref[rows, :]
            r = lax.rsqrt(_mean(xt * xt) + EPS)
            xn = xt * r
            dhr = dh_scr[rows, :]
            acc_scr[0] += _colsum8(dhr)
            acc_scr[1] += _colsum8(dhr * (xn * g))
            acc_scr[2] += _colsum8(dhr * scale1 * xn)
            dxn = dhr * (g * scale1)
            gx_ref[rows, :] = dx2_ref[rows, :] + r * (dxn - xn * _mean(dxn * xn))
        _row_loop(tm, rows_fn)

        @pl.when(i == n_steps - 1)
        def _():
            for qi in range(8):
                sm_ref[qi:qi + 1, :] = jnp.sum(acc_scr[qi], axis=0, keepdims=True)

    tok = lambda i: (i, 0)
    const2 = lambda i: (0, 0)
    return pl.pallas_call(
        body, name="bwd_in",
        grid=(n_steps,),
        out_shape=(jax.ShapeDtypeStruct((s_len, D_MODEL), F32), jax.ShapeDtypeStruct((8, D_MODEL), F32)),
        in_specs=[pl.BlockSpec((N_GROUPS, tm, D_MODEL), lambda i: (0, i, 0)),
                  pl.BlockSpec((12, D_MODEL, 512), lambda i: (0, 0, 0), pipeline_mode=pl.Buffered(1)),
                  pl.BlockSpec((tm, D_MODEL), tok),
                  pl.BlockSpec((tm, D_MODEL), tok),
                  pl.BlockSpec((1, 3 * D_MODEL), const2),
                  pl.BlockSpec((1, D_MODEL), const2)],
        out_specs=(pl.BlockSpec((tm, D_MODEL), tok), pl.BlockSpec((8, D_MODEL), const2)),
        scratch_shapes=[pltpu.VMEM((tm, D_MODEL), F32), pltpu.VMEM((8, 8, D_MODEL), F32)],
        compiler_params=_params(dimension_semantics=("arbitrary",)),
    )(dz6, w12, x, dx2, mod, norm_g)


def _grad_w_in(h, dz6):
    s_len = h.shape[0]
    tk = min(K_TILE, s_len)

    def body(h_ref, dz_ref, out_ref):
        @pl.when(pl.program_id(1) == 0)
        def _():
            out_ref[...] = jnp.zeros_like(out_ref)
        out_ref[0] += _dot_tn(h_ref[...], dz_ref[0])

    return pl.pallas_call(
        body, name="grad_w_in",
        grid=(12, s_len // tk),
        out_shape=jax.ShapeDtypeStruct((12, D_MODEL, 512), F32),
        in_specs=[pl.BlockSpec((tk, D_MODEL), lambda m, kk: (kk, 0)),
                  pl.BlockSpec((1, tk, 512), lambda m, kk: ((m // 2 + 4) % N_GROUPS, kk, m % 2))],
        out_specs=pl.BlockSpec((1, D_MODEL, 512), lambda m, kk: (m, 0, 0)),
        compiler_params=_params(dimension_semantics=("arbitrary", "arbitrary")),
    )(h, dz6)


def _grad_w_out(ycat, dy):
    s_len = dy.shape[0]
    tk = min(K_TILE, s_len)

    def body(y_ref, dy_ref, out_ref):
        @pl.when(pl.program_id(1) == 0)
        def _():
            out_ref[...] = jnp.zeros_like(out_ref)
        out_ref[...] += _dot_tn(y_ref[...], dy_ref[...])

    return pl.pallas_call(
        body, name="grad_w_out",
        grid=(4, s_len // tk),
        out_shape=jax.ShapeDtypeStruct((2 * D_MODEL, D_MODEL), F32),
        in_specs=[pl.BlockSpec((tk, 512), lambda m, kk: (kk, m)),
                  pl.BlockSpec((tk, D_MODEL), lambda m, kk: (kk, 0))],
        out_specs=pl.BlockSpec((512, D_MODEL), lambda m, kk: (m, 0)),
        compiler_params=_params(dimension_semantics=("arbitrary", "arbitrary")),
    )(ycat, dy)


def _adamw_math(w, g, m, v):
    m = ADAM_B1 * m + (1.0 - ADAM_B1) * g
    v = ADAM_B2 * v + (1.0 - ADAM_B2) * (g * g)
    m_hat = m / (1.0 - ADAM_B1 ** ADAM_STEP)
    v_hat = v / (1.0 - ADAM_B2 ** ADAM_STEP)
    delta = -ADAM_LR * (m_hat / (jnp.sqrt(v_hat) + ADAM_EPS) + ADAM_WD * w)
    return delta, m, v


def _adamw_blocked(w, m, v, g4, name):
    nj, _, nr, nc = g4.shape

    def body(w_ref, m_ref, v_ref, g_ref, go_ref, d_ref, mo_ref, vo_ref):
        g = g_ref[0, 0]
        d, mn, vn = _adamw_math(w_ref[...], g, m_ref[...], v_ref[...])
        go_ref[...] = g
        d_ref[...] = d
        mo_ref[...] = mn
        vo_ref[...] = vn

    blk = pl.BlockSpec((nr, nc), lambda j, hf: (hf, j))
    return pl.pallas_call(
        body, name=name,
        grid=(nj, 2),
        out_shape=tuple(jax.ShapeDtypeStruct(w.shape, F32) for _ in range(4)),
        in_specs=[blk, blk, blk, pl.BlockSpec((1, 1, nr, nc), lambda j, hf: (j, hf, 0, 0))],
        out_specs=(blk, blk, blk, blk),
        compiler_params=_params(dimension_semantics=("arbitrary", "arbitrary")),
    )(w, m, v, g4)


def _adamw_w_ada(cact, pack, w, m, v):
    def body(cact_ref, pack_ref, w_ref, m_ref, v_ref, g_ref, d_ref, mo_ref, vo_ref, gb_ref, dmod):
        x, y, _ = _pos()
        k = 2 * x + y
        for t in range(3):
            dmod[:, t * D_MODEL:(t + 1) * D_MODEL] = pack_ref[8 * t:8 * t + 8, :]
        gb_ref[...] = jnp.sum(dmod[...], axis=0, keepdims=True)
        sel = jnp.zeros((8, 768), F32)
        for kk in range(N_CHIPS):
            sel = jnp.where(k == kk, dmod[:, kk * 768:(kk + 1) * 768], sel)
        g = _dot_tn(cact_ref[...].astype(BF16), sel.astype(BF16))
        d, mn, vn = _adamw_math(w_ref[...], g, m_ref[...], v_ref[...])
        g_ref[...] = g
        d_ref[...] = d
        mo_ref[...] = mn
        vo_ref[...] = vn

    return pl.pallas_call(
        body, name="adamw_w_ada",
        out_shape=tuple(jax.ShapeDtypeStruct(w.shape, F32) for _ in range(4)) + (
            jax.ShapeDtypeStruct((1, 3 * D_MODEL), F32),),
        in_specs=[_vmem()] * 5,
        out_specs=tuple(_vmem() for _ in range(5)),
        scratch_shapes=[pltpu.VMEM((8, 3 * D_MODEL), F32)],
        compiler_params=_params(),
    )(cact, pack, w, m, v)


def _adamw_small(items):
    n = len(items)

    def body(*refs):
        ins, outs = refs[:4 * n], refs[4 * n:]
        for i in range(n):
            w_ref, g_ref, m_ref, v_ref = ins[4 * i:4 * i + 4]
            d, mn, vn = _adamw_math(w_ref[...], g_ref[...], m_ref[...], v_ref[...])
            outs[3 * i][...] = d
            outs[3 * i + 1][...] = mn
            outs[3 * i + 2][...] = vn

    flat = [a for it in items for a in it]
    outs = pl.pallas_call(
        body, name="adamw_small",
        out_shape=tuple(jax.ShapeDtypeStruct(it[0].shape, F32) for it in items for _ in range(3)),
        in_specs=[_vmem()] * (4 * n),
        out_specs=tuple(_vmem() for _ in range(3 * n)),
        compiler_params=_params(),
    )(*flat)
    return [tuple(outs[3 * i:3 * i + 3]) for i in range(n)]


def kernel(x, c, w_ada, b_ada, norm_g, w_in, conv_w, conv_b, conv_ln_g, conv_ln_b, sg_ln_g, sg_ln_b, w_s, b_s, w_out, final_g, loss_target, m_w_ada, m_b_ada, m_norm_g, m_w_in, m_conv_w, m_conv_b, m_conv_ln_g, m_conv_ln_b, m_sg_ln_g, m_sg_ln_b, m_w_s, m_b_s, m_w_out, m_final_g, v_w_ada, v_b_ada, v_norm_g, v_w_in, v_conv_w, v_conv_b, v_conv_ln_g, v_conv_ln_b, v_sg_ln_g, v_sg_ln_b, v_w_s, v_b_s, v_w_out, v_final_g):
    s_len = x.shape[1]
    x2d = x[0]
    tgt = loss_target[0]
    row = lambda a: a.reshape(1, -1)

    cw_sh = jnp.pad(conv_w.reshape(CONV_WIDTH, 256), ((0, 1), (0, 0)))
    w_in12, w_out4, cw4 = _gather_weights(w_in[0], w_out[0], cw_sh)
    w12 = w_in12.reshape(12, D_MODEL, 512)
    w_out_full = w_out4.reshape(2 * D_MODEL, D_MODEL)
    mod, cact = _mod(c, w_ada[0], b_ada)

    h, z6 = _fwd_in(x2d, mod, norm_g, w12)
    q = _conv_fwd(z6, cw4, conv_b)
    ln_rows = (conv_ln_g, conv_ln_b, sg_ln_g, sg_ln_b, row(final_g))
    bs_exp = jnp.repeat(b_s[0].T, CHUNK, axis=1)
    dz6, dq, dx2, ycat, dy, dws, sm1, dbs = _middle(x2d, z6, q, tgt, ln_rows, mod, w_s[0], bs_exp, w_out_full)
    dz6, dcw = _conv_bwd(dq, z6, cw4, dz6)
    grad_x, sm2 = _bwd_in(dz6, w12, x2d, dx2, mod, norm_g)
    gw12 = _grad_w_in(h, dz6)
    gwout = _grad_w_out(ycat, dy)

    pack, dws_r, dbs_r, _, dcw_sh, dcb, loss_t = _reduce_small(sm1, sm2, dws.reshape(D_MODEL, CHUNK), dbs, dcw)
    g_w_in4 = _reduce_scatter(gw12.reshape(N_CHIPS, 3, 2, 512, 512), "reduce_scatter_w_in")
    g_w_out4 = _reduce_scatter(gwout.reshape(N_CHIPS, 1, 2, 256, D_MODEL), "reduce_scatter_w_out")

    g_w_in, d_w_in, nm_w_in, nv_w_in = _adamw_blocked(w_in[0], m_w_in[0], v_w_in[0], g_w_in4, "adamw_w_in")
    g_w_out, d_w_out, nm_w_out, nv_w_out = _adamw_blocked(w_out[0], m_w_out[0], v_w_out[0], g_w_out4, "adamw_w_out")
    g_w_ada, d_w_ada, nm_w_ada, nv_w_ada, g_b_ada = _adamw_w_ada(cact, pack, w_ada[0], m_w_ada[0], v_w_ada[0])

    g_norm_g, g_cln_g, g_cln_b, g_sln_g, g_sln_b, g_final = (pack[24 + i:25 + i] for i in range(6))
    loss = loss_t[0, 0]
    g_conv_w = dcw_sh[:CONV_WIDTH]
    g_w_s = dws_r
    g_b_s = dbs_r[:, :HEADS].T
    small = [
        (b_ada, g_b_ada, m_b_ada, v_b_ada),
        (norm_g, g_norm_g, m_norm_g, v_norm_g),
        (conv_w.reshape(CONV_WIDTH, 256), g_conv_w, m_conv_w.reshape(CONV_WIDTH, 256), v_conv_w.reshape(CONV_WIDTH, 256)),
        (conv_b, dcb, m_conv_b, v_conv_b),
        (conv_ln_g, g_cln_g, m_conv_ln_g, v_conv_ln_g),
        (conv_ln_b, g_cln_b, m_conv_ln_b, v_conv_ln_b),
        (sg_ln_g, g_sln_g, m_sg_ln_g, v_sg_ln_g),
        (sg_ln_b, g_sln_b, m_sg_ln_b, v_sg_ln_b),
        (w_s.reshape(D_MODEL, CHUNK), g_w_s, m_w_s.reshape(D_MODEL, CHUNK), v_w_s.reshape(D_MODEL, CHUNK)),
        (b_s[0], g_b_s, m_b_s[0], v_b_s[0]),
        (row(final_g), g_final, row(m_final_g), row(v_final_g)),
    ]
    upd = _adamw_small(small)

    shapes = [w_ada.shape, b_ada.shape, norm_g.shape, w_in.shape, conv_w.shape, conv_b.shape, conv_ln_g.shape,
              conv_ln_b.shape, sg_ln_g.shape, sg_ln_b.shape, w_s.shape, b_s.shape, w_out.shape, final_g.shape]
    grads = [g_w_ada, g_b_ada, g_norm_g, g_w_in, g_conv_w, dcb, g_cln_g, g_cln_b, g_sln_g, g_sln_b, g_w_s, g_b_s,
             g_w_out, g_final]
    big = {0: (d_w_ada, nm_w_ada, nv_w_ada), 3: (d_w_in, nm_w_in, nv_w_in), 12: (d_w_out, nm_w_out, nv_w_out)}
    small_pos = [1, 2, 4, 5, 6, 7, 8, 9, 10, 11, 13]
    trip = [None] * 14
    for i, t in big.items():
        trip[i] = t
    for i, t in zip(small_pos, upd):
        trip[i] = t
    fit = lambda arrs: [a.reshape(s) for a, s in zip(arrs, shapes)]
    return (loss, grad_x.reshape(x.shape), *fit(grads), *fit([t[0] for t in trip]), *fit([t[1] for t in trip]),
            *fit([t[2] for t in trip]))
```

```python
import functools

import jax
import jax.numpy as jnp
from jax import lax
from jax.experimental import pallas as pl
from jax.experimental.pallas import tpu as pltpu

F32 = jnp.float32
BF16 = jnp.bfloat16
MESH = pl.DeviceIdType.MESH

D_MODEL = 1024
N_CHIPS = 4
HEADS = 8
CHUNK = 128
CONV_WIDTH = 31
CONV_HALF = CONV_WIDTH // 2
CONV_PAD = 16
EPS = 1e-6
ADAM_LR = 0.001
ADAM_B1 = 0.9
ADAM_B2 = 0.999
ADAM_EPS = 1e-08
ADAM_WD = 0.01
ADAM_STEP = 10

V7X_VMEM_BYTES = 64 * 1024 * 1024
VMEM_LIMIT = V7X_VMEM_BYTES - 8 * 1024 * 1024
ROWS = 16
UNROLL = 8
TOKEN_TILE = 256
TIME_TILE = 128
K_TILE = 2048

N_GROUPS = 6


def _natural_group(j):
    return (j + 2) % N_GROUPS


def _pos():
    return lax.axis_index("x"), lax.axis_index("y"), lax.axis_index("c")


def _rcopy(src, dst, ssem, rsem, dev):
    return pltpu.make_async_remote_copy(src_ref=src, dst_ref=dst, send_sem=ssem, recv_sem=rsem,
                                        device_id=dev, device_id_type=MESH)


def _vmem():
    return pl.BlockSpec(memory_space=pltpu.VMEM)


def _params(**kw):
    return pltpu.CompilerParams(vmem_limit_bytes=VMEM_LIMIT, **kw)


def _sigmoid(v):
    return jax.nn.sigmoid(v)


def _row_loop(n_rows, body, unroll=1):
    def step(r, carry):
        body(pl.ds(pl.multiple_of(r * ROWS, ROWS), ROWS))
        return carry
    lax.fori_loop(0, n_rows // ROWS, step, 0, unroll=unroll)


def _colsum8(v):
    return v.reshape(v.shape[0] // 8, 8, v.shape[1]).sum(axis=0)


def _mean(v):
    return jnp.mean(v, axis=-1, keepdims=True)


def _dot_nn(a, b):
    return jnp.dot(a, b, preferred_element_type=F32)


def _dot_nt(a, b):
    return lax.dot_general(a, b, (((1,), (1,)), ((), ())), preferred_element_type=F32)


def _dot_tn(a, b):
    return lax.dot_general(a, b, (((0,), (0,)), ((), ())), preferred_element_type=F32)


def _gather_weights(w_in, w_out, cw):
    def body(win_ref, wout_ref, cw_ref, win12_ref, wout4_ref, cw4_ref, ssem, rsem):
        x, y, c = _pos()
        k = 2 * x + y
        sib = (x, y, 1 - c)
        others = [(1 - x, y), (x, 1 - y), (1 - x, 1 - y)]
        for j in range(3):
            for hf in range(2):
                win12_ref[3 * k + j, hf] = win_ref[hf * 512:(hf + 1) * 512, j * 512:(j + 1) * 512].astype(BF16)
        for hf in range(2):
            wout4_ref[k, hf] = wout_ref[hf * 256:(hf + 1) * 256, :].astype(BF16)
        cw4_ref[k] = cw_ref[...]

        def win_slab(kk, hf):
            return win12_ref.at[pl.ds(3 * kk, 3), hf]

        def wout_slab(kk, hf):
            return wout4_ref.at[kk, hf]

        first = []
        for j, (ox, oy) in enumerate(others):
            dev = (ox, oy, c)
            first.append(_rcopy(win_slab(k, c), win_slab(k, c), ssem.at[j], rsem.at[j], dev))
            first.append(_rcopy(wout_slab(k, c), wout_slab(k, c), ssem.at[3 + j], rsem.at[3 + j], dev))
            first.append(_rcopy(cw4_ref.at[k], cw4_ref.at[k], ssem.at[6 + j], rsem.at[6 + j], dev))
        for cp in first:
            cp.start()
        passed = []
        for j, (ox, oy) in enumerate(others):
            ko = 2 * ox + oy
            dev = (ox, oy, c)
            _rcopy(win_slab(ko, c), win_slab(ko, c), ssem.at[j], rsem.at[j], dev).wait_recv()
            f1 = _rcopy(win_slab(ko, c), win_slab(ko, c), ssem.at[9 + j], rsem.at[9 + j], sib)
            f1.start()
            _rcopy(wout_slab(ko, c), wout_slab(ko, c), ssem.at[3 + j], rsem.at[3 + j], dev).wait_recv()
            f2 = _rcopy(wout_slab(ko, c), wout_slab(ko, c), ssem.at[12 + j], rsem.at[12 + j], sib)
            f2.start()
            passed += [f1, f2]
        for j, (ox, oy) in enumerate(others):
            ko = 2 * ox + oy
            _rcopy(cw4_ref.at[ko], cw4_ref.at[ko], ssem.at[6 + j], rsem.at[6 + j], (ox, oy, c)).wait_recv()
            _rcopy(win_slab(ko, 1 - c), win_slab(ko, 1 - c), ssem.at[9 + j], rsem.at[9 + j], sib).wait_recv()
            _rcopy(wout_slab(ko, 1 - c), wout_slab(ko, 1 - c), ssem.at[12 + j], rsem.at[12 + j], sib).wait_recv()
        for cp in first + passed:
            cp.wait_send()

    return pl.pallas_call(
        body, name="gather_weights",
        out_shape=(jax.ShapeDtypeStruct((12, 2, 512, 512), BF16),
                   jax.ShapeDtypeStruct((N_CHIPS, 2, 256, D_MODEL), BF16),
                   jax.ShapeDtypeStruct((N_CHIPS, 32, 256), F32)),
        in_specs=[_vmem(), _vmem(), _vmem()],
        out_specs=(_vmem(), _vmem(), _vmem()),
        scratch_shapes=[pltpu.SemaphoreType.DMA((15,)), pltpu.SemaphoreType.DMA((15,))],
        compiler_params=_params(),
    )(w_in, w_out, cw)


def _butterfly(bufs, recvs, ssem, rsem, partners, first_sem=0):
    s = first_sem
    for step, dev in enumerate(partners):
        cps = []
        for buf, recv in zip(bufs, recvs):
            cp = _rcopy(buf, recv.at[step], ssem.at[s], rsem.at[s], dev)
            cp.start()
            cps.append(cp)
            s += 1
        for cp in cps:
            cp.wait()
        for buf, recv in zip(bufs, recvs):
            buf[...] = buf[...] + recv[step]
    return s


def _mod(c, w_ada, b_ada):
    def body(c_ref, wada_ref, bada_ref, mod_ref, cact_ref, crecv, mbuf, mrecv, ssem, rsem):
        x, y, c_ = _pos()
        k = 2 * x + y
        b = 4 * x + 2 * y + c_
        row = lax.broadcasted_iota(jnp.int32, (8, D_MODEL), 0)
        cact_ref[...] = jnp.where(row == b, jnp.broadcast_to(c_ref[...], (8, D_MODEL)), 0.0)
        n = _butterfly([cact_ref], [crecv], ssem, rsem, [(x, y, 1 - c_), (x, 1 - y, c_), (1 - x, y, c_)])
        call = cact_ref[...]
        cact = call * _sigmoid(call)
        cact_ref[...] = cact
        part = _dot_nn(cact.astype(BF16), wada_ref[...].astype(BF16))
        mbuf[...] = jnp.zeros_like(mbuf)
        mbuf[k] = part
        _butterfly([mbuf], [mrecv], ssem, rsem, [(x, 1 - y, c_), (1 - x, y, c_)], first_sem=n)
        row8 = lax.broadcasted_iota(jnp.int32, (8, 768), 0)
        for kk in range(N_CHIPS):
            piece = jnp.sum(jnp.where(row8 == b, mbuf[kk], 0.0), axis=0, keepdims=True)
            mod_ref[:, kk * 768:(kk + 1) * 768] = piece + bada_ref[:, kk * 768:(kk + 1) * 768]

    return pl.pallas_call(
        body, name="mod",
        out_shape=(jax.ShapeDtypeStruct((1, 3 * D_MODEL), F32), jax.ShapeDtypeStruct((8, D_MODEL), F32)),
        in_specs=[_vmem(), _vmem(), _vmem()],
        out_specs=(_vmem(), _vmem()),
        scratch_shapes=[pltpu.VMEM((3, 8, D_MODEL), F32), pltpu.VMEM((N_CHIPS, 8, 768), F32),
                        pltpu.VMEM((2, N_CHIPS, 8, 768), F32),
                        pltpu.SemaphoreType.DMA((5,)), pltpu.SemaphoreType.DMA((5,))],
        compiler_params=_params(),
    )(c, w_ada, b_ada)


def _reduce_small(sm1, sm2, dws, dbs, dcw):
    def body(sm1_ref, sm2_ref, dws_in, dbs_in, dcw_in, pack_ref, dws_ref, dbs_ref, dcw_ref, dcwsh_ref, dcb_ref, loss_ref,
             r_pack, r_dws, r_dbs, r_dcw, ssem, rsem):
        x, y, c = _pos()
        k = 2 * x + y
        b = 4 * x + 2 * y + c
        row = lax.broadcasted_iota(jnp.int32, (8, D_MODEL), 0)

        def onehot(v):
            return jnp.where(row == b, jnp.broadcast_to(v, (8, D_MODEL)), 0.0)

        pack_ref[0:8, :] = onehot(sm2_ref[0:1, :])
        pack_ref[8:16, :] = onehot(sm2_ref[1:2, :])
        pack_ref[16:24, :] = onehot(sm1_ref[1:2, :])
        order = [sm2_ref[2:3, :], sm1_ref[2:3, :], sm1_ref[3:4, :], sm1_ref[4:5, :], sm1_ref[5:6, :],
                 sm1_ref[0:1, :], sm1_ref[6:7, :], jnp.zeros((1, D_MODEL), F32)]
        for i, v in enumerate(order):
            pack_ref[24 + i:25 + i, :] = v
        dws_ref[...] = dws_in[...]
        dbs_ref[...] = dbs_in[...]
        dcw_ref[...] = dcw_in[...]
        _butterfly([pack_ref, dws_ref, dbs_ref, dcw_ref], [r_pack, r_dws, r_dbs, r_dcw], ssem, rsem,
                   [(x, y, 1 - c), (x, 1 - y, c), (1 - x, y, c)])
        sel = jnp.zeros((32, 256), F32)
        for kk in range(N_CHIPS):
            sel = jnp.where(k == kk, dcw_ref[:, kk * 256:(kk + 1) * 256], sel)
        dcwsh_ref[...] = sel
        dcb_ref[...] = dcw_ref[31:32, :]
        loss_ref[...] = jnp.broadcast_to(jnp.sum(pack_ref[30:31, :], axis=1, keepdims=True), (8, 128))

    shp = [(32, D_MODEL), (D_MODEL, 128), (128, 128), (32, D_MODEL)]
    return pl.pallas_call(
        body, name="reduce_small",
        out_shape=tuple(jax.ShapeDtypeStruct(s, F32) for s in shp) + (
            jax.ShapeDtypeStruct((32, 256), F32), jax.ShapeDtypeStruct((1, D_MODEL), F32),
            jax.ShapeDtypeStruct((8, 128), F32)),
        in_specs=[_vmem()] * 5,
        out_specs=tuple(_vmem() for _ in range(7)),
        scratch_shapes=[pltpu.VMEM((3,) + s, F32) for s in shp] + [
            pltpu.SemaphoreType.DMA((12,)), pltpu.SemaphoreType.DMA((12,))],
        compiler_params=_params(),
    )(sm1, sm2, dws, dbs, dcw)


def _reduce_scatter(g5, name):
    _, nj, _, nr, nc = g5.shape

    def body(g_ref, out_ref, own, recv_a, send_b, recv_b, lsem, ssem, rsem):
        x, y, c = _pos()
        k = 2 * x + y
        sib = (x, y, 1 - c)
        others = [(1 - x, y), (x, 1 - y), (1 - x, 1 - y)]
        loc = pltpu.make_async_copy(g_ref.at[:, :, c], own, lsem)
        loc.start()
        pair = _rcopy(g_ref.at[:, :, 1 - c], recv_a, ssem.at[0], rsem.at[0], sib)
        pair.start()
        loc.wait()
        pair.wait()
        sends = []
        for r, (ox, oy) in enumerate(others):
            ko = 2 * ox + oy
            for j in range(nj):
                send_b[r, j] = (own[ko, j] + recv_a[ko, j]).astype(BF16)
            cp = _rcopy(send_b.at[r], recv_b.at[r], ssem.at[1 + r], rsem.at[1 + r], (ox, oy, c))
            cp.start()
            sends.append(cp)
        for j in range(nj):
            own[k, j] = own[k, j] + recv_a[k, j]
        for cp in sends:
            cp.wait_recv()
        for j in range(nj):
            out_ref[j, c] = ((own[k, j] + recv_b[0, j].astype(F32)) + recv_b[1, j].astype(F32)) + recv_b[2, j].astype(F32)
        swap = _rcopy(out_ref.at[:, c], out_ref.at[:, c], ssem.at[4], rsem.at[4], sib)
        swap.start()
        _rcopy(out_ref.at[:, 1 - c], out_ref.at[:, 1 - c], ssem.at[4], rsem.at[4], sib).wait_recv()
        swap.wait_send()
        for cp in sends:
            cp.wait_send()

    return pl.pallas_call(
        body, name=name,
        out_shape=jax.ShapeDtypeStruct((nj, 2, nr, nc), F32),
        in_specs=[pl.BlockSpec(memory_space=pl.ANY)],
        out_specs=_vmem(),
        scratch_shapes=[pltpu.VMEM((N_CHIPS, nj, nr, nc), F32), pltpu.VMEM((N_CHIPS, nj, nr, nc), F32),
                        pltpu.VMEM((3, nj, nr, nc), BF16), pltpu.VMEM((3, nj, nr, nc), BF16),
                        pltpu.SemaphoreType.DMA, pltpu.SemaphoreType.DMA((5,)), pltpu.SemaphoreType.DMA((5,))],
        compiler_params=_params(),
    )(g5)


def _fwd_in(x, mod, norm_g, w12):
    s_len = x.shape[0]
    tm = TOKEN_TILE

    def body(x_ref, mod_ref, g_ref, w_ref, h_ref, z_ref):
        shift = mod_ref[:, 0:D_MODEL]
        scale1 = 1.0 + mod_ref[:, D_MODEL:2 * D_MODEL]
        g = g_ref[...]

        def rows_fn(rows):
            xt = x_ref[rows, :]
            r = lax.rsqrt(_mean(xt * xt) + EPS)
            h_ref[rows, :] = ((xt * r * g) * scale1 + shift).astype(BF16)
        _row_loop(tm, rows_fn, unroll=UNROLL)
        hb = h_ref[...]
        for j in range(N_GROUPS):
            n = _natural_group(j)
            for hf in range(2):
                z_ref[j, :, hf * 512:(hf + 1) * 512] = _dot_nn(hb, w_ref[2 * n + hf])

    return pl.pallas_call(
        body, name="fwd_in",
        grid=(s_len // tm,),
        out_shape=(jax.ShapeDtypeStruct((s_len, D_MODEL), BF16), jax.ShapeDtypeStruct((N_GROUPS, s_len, D_MODEL), F32)),
        in_specs=[pl.BlockSpec((tm, D_MODEL), lambda i: (i, 0)),
                  pl.BlockSpec((1, 3 * D_MODEL), lambda i: (0, 0)),
                  pl.BlockSpec((1, D_MODEL), lambda i: (0, 0)),
                  pl.BlockSpec((12, D_MODEL, 512), lambda i: (0, 0, 0), pipeline_mode=pl.Buffered(1))],
        out_specs=(pl.BlockSpec((tm, D_MODEL), lambda i: (i, 0)),
                   pl.BlockSpec((N_GROUPS, tm, D_MODEL), lambda i: (0, i, 0))),
        compiler_params=_params(dimension_semantics=("arbitrary",)),
    )(x, mod, norm_g, w12)


def _shifted_windows(win, n_out):
    n = win.shape[0]
    for s in range(8):
        ws = win if s == 0 else pltpu.roll(win, n - s, axis=0)
        for a in range(4):
            o = 8 * a + s
            if 1 <= o <= CONV_WIDTH:
                yield o, ws[8 * a:8 * a + n_out, :]


def _conv_fwd(z6, cw4, conv_b):
    s_len = z6.shape[1]
    tt = TIME_TILE

    def body(z_ref, cw_ref, cb_ref, q_ref, ppad):
        zero = jnp.zeros((CONV_PAD, 128), F32)
        ppad[0:CONV_PAD, :] = zero
        ppad[s_len + CONV_PAD:s_len + 2 * CONV_PAD, :] = zero

        def fill(i, carry):
            t0 = pl.multiple_of(i * tt, tt)
            ppad[pl.ds(CONV_PAD + t0, tt), :] = z_ref[0, pl.ds(t0, tt), :] * _sigmoid(z_ref[1, pl.ds(t0, tt), :])
            return carry
        lax.fori_loop(0, s_len // tt, fill, 0)
        w = cw_ref[0]
        bias = cb_ref[...]

        def conv(i, carry):
            t0 = pl.multiple_of(i * tt, tt)
            win = ppad[pl.ds(t0, tt + 2 * CONV_PAD), :]
            acc = jnp.broadcast_to(bias, (tt, 128))
            for o, sl in _shifted_windows(win, tt):
                acc = acc + w[o - 1:o, :] * sl
            q_ref[pl.ds(t0, tt), :] = acc
            return carry
        lax.fori_loop(0, s_len // tt, conv, 0)

    return pl.pallas_call(
        body, name="conv_fwd",
        grid=(D_MODEL // 128,),
        out_shape=jax.ShapeDtypeStruct((s_len, D_MODEL), F32),
        in_specs=[pl.BlockSpec((2, s_len, 128), lambda j: (2, 0, j)),
                  pl.BlockSpec((1, 32, 128), lambda j: (j // 2, 0, j % 2)),
                  pl.BlockSpec((1, 128), lambda j: (0, j))],
        out_specs=pl.BlockSpec((s_len, 128), lambda j: (0, j)),
        scratch_shapes=[pltpu.VMEM((s_len + 2 * CONV_PAD, 128), F32)],
        compiler_params=_params(dimension_semantics=("arbitrary",)),
    )(z6, cw4, conv_b)


def _middle(x, z6, q, target, ln_rows, mod, w_s, bs_exp, w_out):
    s_len = x.shape[0]
    tm = TOKEN_TILE
    n_steps = s_len // tm
    n_chunks = tm // CHUNK
    inv_d = 1.0 / D_MODEL

    def body(x_ref, z_ref, q_ref, tgt_ref, cg_ref, cb_ref, sg_ref, sb_ref, fg_ref, mod_ref, ws_ref, bs_ref, wout_ref,
             dz_ref, dq_ref, dx2_ref, ycat_ref, dy_ref, dws_ref, sm_ref, dbs_ref,
             vl_scr, vm_scr, y_scr, dycat_scr, dvm_scr, dvl_scr, acc_scr, dbs_acc):
        i = pl.program_id(0)

        @pl.when(i == 0)
        def _():
            acc_scr[...] = jnp.zeros_like(acc_scr)
            dbs_acc[...] = jnp.zeros_like(dbs_acc)
            dws_ref[...] = jnp.zeros_like(dws_ref)

        cg, cb, sg, sb, fg = cg_ref[...], cb_ref[...], sg_ref[...], sb_ref[...], fg_ref[...]
        gm = mod_ref[:, 2 * D_MODEL:3 * D_MODEL]

        def norm_stats(t):
            c = t - _mean(t)
            rstd = lax.rsqrt(_mean(c * c) + EPS)
            return c * rstd, rstd

        def phase1(rows):
            qhat, _ = norm_stats(q_ref[rows, :])
            ln = qhat * cg + cb
            gz = z_ref[0, rows, :]
            ycat_ref[rows, 0:D_MODEL] = ((ln * _sigmoid(ln)) * (gz * _sigmoid(gz))).astype(BF16)
            vhat, _ = norm_stats(z_ref[2, rows, :])
            vl_scr[rows, :] = (vhat * sg + sb).astype(BF16)
        _row_loop(tm, phase1, unroll=UNROLL)

        for ch in range(n_chunks):
            r0 = ch * CHUNK
            for h in range(HEADS):
                c0 = h * CHUNK
                vm_scr[r0:r0 + CHUNK, c0:c0 + CHUNK] = (
                    _dot_nn(ws_ref[h].astype(BF16), vl_scr[r0:r0 + CHUNK, c0:c0 + CHUNK]) + bs_ref[:, c0:c0 + CHUNK])

        def phase3(rows):
            bg = z_ref[3, rows, :]
            ycat_ref[rows, D_MODEL:2 * D_MODEL] = (z_ref[1, rows, :] * vm_scr[rows, :] * (bg * _sigmoid(bg))).astype(BF16)
        _row_loop(tm, phase3)

        y_scr[...] = _dot_nn(ycat_ref[...], wout_ref[...])

        def phase5(rows):
            y = y_scr[rows, :]
            x2 = x_ref[rows, :] + gm * y
            r2 = lax.rsqrt(_mean(x2 * x2) + EPS)
            xn2 = x2 * r2
            diff = xn2 * fg - tgt_ref[rows, :]
            acc_scr[6] += _colsum8(diff * diff)
            dout = diff * inv_d
            acc_scr[0] += _colsum8(dout * xn2)
            dxn = dout * fg
            dx2 = r2 * (dxn - xn2 * _mean(dxn * xn2))
            dx2_ref[rows, :] = dx2
            acc_scr[1] += _colsum8(dx2 * y)
            dy_ref[rows, :] = (dx2 * gm).astype(BF16)
        _row_loop(tm, phase5, unroll=UNROLL)

        dycat_scr[...] = _dot_nt(dy_ref[...], wout_ref[...])

        def phase7(rows):
            dyb = dycat_scr[rows, D_MODEL:2 * D_MODEL]
            u = z_ref[1, rows, :]
            bg = z_ref[3, rows, :]
            vm = vm_scr[rows, :]
            sig = _sigmoid(bg)
            silu = bg * sig
            dz_ref[1, rows, :] = (dyb * vm * silu).astype(BF16)
            dvm = dyb * u * silu
            dz_ref[3, rows, :] = (dyb * u * vm * (sig * (1.0 + bg * (1.0 - sig)))).astype(BF16)
            dvm_scr[rows, :] = dvm.astype(BF16)
            pos = pl.ds(pl.multiple_of(rows.start % CHUNK, ROWS), ROWS)
            dbs_acc[pos, :] += dvm
        _row_loop(tm, phase7)

        for ch in range(n_chunks):
            r0 = ch * CHUNK
            for h in range(HEADS):
                c0 = h * CHUNK
                dvm_b = dvm_scr[r0:r0 + CHUNK, c0:c0 + CHUNK]
                dws_ref[h] += _dot_nt(dvm_b, vl_scr[r0:r0 + CHUNK, c0:c0 + CHUNK])
                dvl_scr[r0:r0 + CHUNK, c0:c0 + CHUNK] = _dot_tn(ws_ref[h].astype(BF16), dvm_b)

        def phase9(rows):
            vhat, rstd_v = norm_stats(z_ref[2, rows, :])
            dvl = dvl_scr[rows, :]
            acc_scr[4] += _colsum8(dvl * vhat)
            acc_scr[5] += _colsum8(dvl)
            dvh = dvl * sg
            dz_ref[2, rows, :] = (rstd_v * (dvh - _mean(dvh) - vhat * _mean(dvh * vhat))).astype(BF16)
            qhat, rstd_q = norm_stats(q_ref[rows, :])
            ln = qhat * cg + cb
            sig_ln = _sigmoid(ln)
            gz = z_ref[0, rows, :]
            sig_g = _sigmoid(gz)
            dya = dycat_scr[rows, 0:D_MODEL]
            dz_ref[0, rows, :] = (dya * (ln * sig_ln) * (sig_g * (1.0 + gz * (1.0 - sig_g)))).astype(BF16)
            dln = (dya * (gz * sig_g)) * (sig_ln * (1.0 + ln * (1.0 - sig_ln)))
            acc_scr[2] += _colsum8(dln * qhat)
            acc_scr[3] += _colsum8(dln)
            dqh = dln * cg
            dq_ref[rows, :] = rstd_q * (dqh - _mean(dqh) - qhat * _mean(dqh * qhat))
        _row_loop(tm, phase9, unroll=UNROLL)

        @pl.when(i == n_steps - 1)
        def _():
            for qi in range(8):
                scale = 0.5 * inv_d if qi == 6 else 1.0
                sm_ref[qi:qi + 1, :] = jnp.sum(acc_scr[qi], axis=0, keepdims=True) * scale
            lane = lax.broadcasted_iota(jnp.int32, (CHUNK, CHUNK), 1)
            tile = jnp.zeros((CHUNK, CHUNK), F32)
            for h in range(HEADS):
                col = jnp.sum(dbs_acc[:, h * CHUNK:(h + 1) * CHUNK], axis=1, keepdims=True)
                tile = jnp.where(lane == h, col, tile)
            dbs_ref[...] = tile

    tok = lambda i: (i, 0)
    const2 = lambda i: (0, 0)
    return pl.pallas_call(
        body, name="middle",
        grid=(n_steps,),
        out_shape=(jax.ShapeDtypeStruct((N_GROUPS, s_len, D_MODEL), BF16),
                   jax.ShapeDtypeStruct((s_len, D_MODEL), F32),
                   jax.ShapeDtypeStruct((s_len, D_MODEL), F32),
                   jax.ShapeDtypeStruct((s_len, 2 * D_MODEL), BF16),
                   jax.ShapeDtypeStruct((s_len, D_MODEL), BF16),
                   jax.ShapeDtypeStruct((HEADS, CHUNK, CHUNK), F32),
                   jax.ShapeDtypeStruct((8, D_MODEL), F32),
                   jax.ShapeDtypeStruct((CHUNK, CHUNK), F32)),
        in_specs=[pl.BlockSpec((tm, D_MODEL), tok),
                  pl.BlockSpec((4, tm, D_MODEL), lambda i: (0, i, 0)),
                  pl.BlockSpec((tm, D_MODEL), tok),
                  pl.BlockSpec((tm, D_MODEL), tok),
                  *[pl.BlockSpec((1, D_MODEL), const2) for _ in range(5)],
                  pl.BlockSpec((1, 3 * D_MODEL), const2),
                  pl.BlockSpec((HEADS, CHUNK, CHUNK), lambda i: (0, 0, 0)),
                  pl.BlockSpec((CHUNK, D_MODEL), const2),
                  pl.BlockSpec((2 * D_MODEL, D_MODEL), const2, pipeline_mode=pl.Buffered(1))],
        out_specs=(pl.BlockSpec((4, tm, D_MODEL), lambda i: (0, i, 0)),
                   pl.BlockSpec((tm, D_MODEL), tok),
                   pl.BlockSpec((tm, D_MODEL), tok),
                   pl.BlockSpec((tm, 2 * D_MODEL), tok),
                   pl.BlockSpec((tm, D_MODEL), tok),
                   pl.BlockSpec((HEADS, CHUNK, CHUNK), lambda i: (0, 0, 0)),
                   pl.BlockSpec((8, D_MODEL), const2),
                   pl.BlockSpec((CHUNK, CHUNK), const2)),
        scratch_shapes=[pltpu.VMEM((tm, D_MODEL), BF16),
                        pltpu.VMEM((tm, D_MODEL), F32),
                        pltpu.VMEM((tm, D_MODEL), F32),
                        pltpu.VMEM((tm, 2 * D_MODEL), F32),
                        pltpu.VMEM((tm, D_MODEL), BF16),
                        pltpu.VMEM((tm, D_MODEL), F32),
                        pltpu.VMEM((8, 8, D_MODEL), F32),
                        pltpu.VMEM((CHUNK, D_MODEL), F32)],
        compiler_params=_params(dimension_semantics=("arbitrary",)),
    )(x, z6, q, target, *ln_rows, mod, w_s, bs_exp, w_out)


def _conv_bwd(dq, z6, cw4, dz6):
    s_len = dq.shape[0]
    tt = TIME_TILE

    def body(dq_ref, z_ref, cw_ref, dz_in, dz_ref, dcw_ref, dqpad, ppad, wacc):
        del dz_in
        zero = jnp.zeros((CONV_PAD, 128), F32)
        for pad in (dqpad, ppad):
            pad[0:CONV_PAD, :] = zero
            pad[s_len + CONV_PAD:s_len + 2 * CONV_PAD, :] = zero
        wacc[...] = jnp.zeros_like(wacc)

        def fill(i, carry):
            t0 = pl.multiple_of(i * tt, tt)
            ppad[pl.ds(CONV_PAD + t0, tt), :] = z_ref[0, pl.ds(t0, tt), :] * _sigmoid(z_ref[1, pl.ds(t0, tt), :])
            dqpad[pl.ds(CONV_PAD + t0, tt), :] = dq_ref[pl.ds(t0, tt), :]
            return carry
        lax.fori_loop(0, s_len // tt, fill, 0)
        w = cw_ref[0]

        def bwd(i, carry):
            t0 = pl.multiple_of(i * tt, tt)
            dp = jnp.zeros((tt, 128), F32)
            for o, sl in _shifted_windows(dqpad[pl.ds(t0, tt + 2 * CONV_PAD), :], tt):
                dp = dp + w[CONV_WIDTH - o:CONV_WIDTH - o + 1, :] * sl
            a = z_ref[0, pl.ds(t0, tt), :]
            sig = _sigmoid(z_ref[1, pl.ds(t0, tt), :])
            dz_ref[0, pl.ds(t0, tt), :] = (dp * sig).astype(BF16)
            dz_ref[1, pl.ds(t0, tt), :] = (dp * a * (sig * (1.0 - sig))).astype(BF16)
            dqt = dq_ref[pl.ds(t0, tt), :]
            for o, sl in _shifted_windows(ppad[pl.ds(t0, tt + 2 * CONV_PAD), :], tt):
                wacc[o - 1] += _colsum8(dqt * sl)
            wacc[CONV_WIDTH] += _colsum8(dqt)
            return carry
        lax.fori_loop(0, s_len // tt, bwd, 0)
        for k in range(32):
            dcw_ref[k:k + 1, :] = jnp.sum(wacc[k], axis=0, keepdims=True)

    return pl.pallas_call(
        body, name="conv_bwd",
        grid=(D_MODEL // 128,),
        out_shape=(jax.ShapeDtypeStruct(dz6.shape, BF16), jax.ShapeDtypeStruct((32, D_MODEL), F32)),
        in_specs=[pl.BlockSpec((s_len, 128), lambda j: (0, j)),
                  pl.BlockSpec((2, s_len, 128), lambda j: (2, 0, j)),
                  pl.BlockSpec((1, 32, 128), lambda j: (j // 2, 0, j % 2)),
                  pl.BlockSpec(memory_space=pl.ANY)],
        out_specs=(pl.BlockSpec((2, s_len, 128), lambda j: (2, 0, j)),
                   pl.BlockSpec((32, 128), lambda j: (0, j))),
        scratch_shapes=[pltpu.VMEM((s_len + 2 * CONV_PAD, 128), F32), pltpu.VMEM((s_len + 2 * CONV_PAD, 128), F32),
                        pltpu.VMEM((32, 8, 128), F32)],
        input_output_aliases={3: 0},
        compiler_params=_params(dimension_semantics=("arbitrary",)),
    )(dq, z6, cw4, dz6)


def _bwd_in(dz6, w12, x, dx2, mod, norm_g):
    s_len = x.shape[0]
    tm = TOKEN_TILE
    n_steps = s_len // tm

    def body(dz_ref, w_ref, x_ref, dx2_ref, mod_ref, g_ref, gx_ref, sm_ref, dh_scr, acc_scr):
        i = pl.program_id(0)

        @pl.when(i == 0)
        def _():
            acc_scr[...] = jnp.zeros_like(acc_scr)

        dh = jnp.zeros((tm, D_MODEL), F32)
        for j in range(N_GROUPS):
            n = _natural_group(j)
            for hf in range(2):
                dh = dh + _dot_nt(dz_ref[j, :, hf * 512:(hf + 1) * 512], w_ref[2 * n + hf])
        dh_scr[...] = dh
        scale1 = 1.0 + mod_ref[:, D_MODEL:2 * D_MODEL]
        g = g_ref[...]

        def rows_fn(rows):
            xt = x_ref[rows, :]
            r = lax.rsqrt(_mean(xt * xt) + EPS)
            xn = xt * r
            dhr = dh_scr[rows, :]
            acc_scr[0] += _colsum8(dhr)
            acc_scr[1] += _colsum8(dhr * (xn * g))
            acc_scr[2] += _colsum8(dhr * scale1 * xn)
            dxn = dhr * (g * scale1)
            gx_ref[rows, :] = dx2_ref[rows, :] + r * (dxn - xn * _mean(dxn * xn))
        _row_loop(tm, rows_fn, unroll=UNROLL)

        @pl.when(i == n_steps - 1)
        def _():
            for qi in range(8):
                sm_ref[qi:qi + 1, :] = jnp.sum(acc_scr[qi], axis=0, keepdims=True)

    tok = lambda i: (i, 0)
    const2 = lambda i: (0, 0)
    return pl.pallas_call(
        body, name="bwd_in",
        grid=(n_steps,),
        out_shape=(jax.ShapeDtypeStruct((s_len, D_MODEL), F32), jax.ShapeDtypeStruct((8, D_MODEL), F32)),
        in_specs=[pl.BlockSpec((N_GROUPS, tm, D_MODEL), lambda i: (0, i, 0)),
                  pl.BlockSpec((12, D_MODEL, 512), lambda i: (0, 0, 0), pipeline_mode=pl.Buffered(1)),
                  pl.BlockSpec((tm, D_MODEL), tok),
                  pl.BlockSpec((tm, D_MODEL), tok),
                  pl.BlockSpec((1, 3 * D_MODEL), const2),
                  pl.BlockSpec((1, D_MODEL), const2)],
        out_specs=(pl.BlockSpec((tm, D_MODEL), tok), pl.BlockSpec((8, D_MODEL), const2)),
        scratch_shapes=[pltpu.VMEM((tm, D_MODEL), F32), pltpu.VMEM((8, 8, D_MODEL), F32)],
        compiler_params=_params(dimension_semantics=("arbitrary",)),
    )(dz6, w12, x, dx2, mod, norm_g)


def _grad_w_in(h, dz6):
    s_len = h.shape[0]
    tk = min(K_TILE, s_len)

    def body(h_ref, dz_ref, out_ref):
        @pl.when(pl.program_id(1) == 0)
        def _():
            out_ref[...] = jnp.zeros_like(out_ref)
        out_ref[0] += _dot_tn(h_ref[...], dz_ref[0])

    return pl.pallas_call(
        body, name="grad_w_in",
        grid=(12, s_len // tk),
        out_shape=jax.ShapeDtypeStruct((12, D_MODEL, 512), F32),
        in_specs=[pl.BlockSpec((tk, D_MODEL), lambda m, kk: (kk, 0)),
                  pl.BlockSpec((1, tk, 512), lambda m, kk: ((m // 2 + 4) % N_GROUPS, kk, m % 2))],
        out_specs=pl.BlockSpec((1, D_MODEL, 512), lambda m, kk: (m, 0, 0)),
        compiler_params=_params(dimension_semantics=("arbitrary", "arbitrary")),
    )(h, dz6)


def _grad_w_out(ycat, dy):
    s_len = dy.shape[0]
    tk = min(K_TILE, s_len)

    def body(y_ref, dy_ref, out_ref):
        @pl.when(pl.program_id(1) == 0)
        def _():
            out_ref[...] = jnp.zeros_like(out_ref)
        out_ref[...] += _dot_tn(y_ref[...], dy_ref[...])

    return pl.pallas_call(
        body, name="grad_w_out",
        grid=(4, s_len // tk),
        out_shape=jax.ShapeDtypeStruct((2 * D_MODEL, D_MODEL), F32),
        in_specs=[pl.BlockSpec((tk, 512), lambda m, kk: (kk, m)),
                  pl.BlockSpec((tk, D_MODEL), lambda m, kk: (kk, 0))],
        out_specs=pl.BlockSpec((512, D_MODEL), lambda m, kk: (m, 0)),
        compiler_params=_params(dimension_semantics=("arbitrary", "arbitrary")),
    )(ycat, dy)


def _adamw_math(w, g, m, v):
    m = ADAM_B1 * m + (1.0 - ADAM_B1) * g
    v = ADAM_B2 * v + (1.0 - ADAM_B2) * (g * g)
    m_hat = m / (1.0 - ADAM_B1 ** ADAM_STEP)
    v_hat = v / (1.0 - ADAM_B2 ** ADAM_STEP)
    delta = -ADAM_LR * (m_hat / (jnp.sqrt(v_hat) + ADAM_EPS) + ADAM_WD * w)
    return delta, m, v


def _adamw_blocked(w, m, v, g4, name):
    nj, _, nr, nc = g4.shape

    def body(w_ref, m_ref, v_ref, g_ref, go_ref, d_ref, mo_ref, vo_ref):
        g = g_ref[0, 0]
        d, mn, vn = _adamw_math(w_ref[...], g, m_ref[...], v_ref[...])
        go_ref[...] = g
        d_ref[...] = d
        mo_ref[...] = mn
        vo_ref[...] = vn

    blk = pl.BlockSpec((nr, nc), lambda j, hf: (hf, j))
    return pl.pallas_call(
        body, name=name,
        grid=(nj, 2),
        out_shape=tuple(jax.ShapeDtypeStruct(w.shape, F32) for _ in range(4)),
        in_specs=[blk, blk, blk, pl.BlockSpec((1, 1, nr, nc), lambda j, hf: (j, hf, 0, 0))],
        out_specs=(blk, blk, blk, blk),
        compiler_params=_params(dimension_semantics=("arbitrary", "arbitrary")),
    )(w, m, v, g4)


def _adamw_w_ada(cact, pack, w, m, v):
    def body(cact_ref, pack_ref, w_ref, m_ref, v_ref, g_ref, d_ref, mo_ref, vo_ref, gb_ref, dmod):
        x, y, _ = _pos()
        k = 2 * x + y
        for t in range(3):
            dmod[:, t * D_MODEL:(t + 1) * D_MODEL] = pack_ref[8 * t:8 * t + 8, :]
        gb_ref[...] = jnp.sum(dmod[...], axis=0, keepdims=True)
        sel = jnp.zeros((8, 768), F32)
        for kk in range(N_CHIPS):
            sel = jnp.where(k == kk, dmod[:, kk * 768:(kk + 1) * 768], sel)
        g = _dot_tn(cact_ref[...].astype(BF16), sel.astype(BF16))
        d, mn, vn = _adamw_math(w_ref[...], g, m_ref[...], v_ref[...])
        g_ref[...] = g
        d_ref[...] = d
        mo_ref[...] = mn
        vo_ref[...] = vn

    return pl.pallas_call(
        body, name="adamw_w_ada",
        out_shape=tuple(jax.ShapeDtypeStruct(w.shape, F32) for _ in range(4)) + (
            jax.ShapeDtypeStruct((1, 3 * D_MODEL), F32),),
        in_specs=[_vmem()] * 5,
        out_specs=tuple(_vmem() for _ in range(5)),
        scratch_shapes=[pltpu.VMEM((8, 3 * D_MODEL), F32)],
        compiler_params=_params(),
    )(cact, pack, w, m, v)


def _adamw_small(items):
    n = len(items)

    def body(*refs):
        ins, outs = refs[:4 * n], refs[4 * n:]
        for i in range(n):
            w_ref, g_ref, m_ref, v_ref = ins[4 * i:4 * i + 4]
            d, mn, vn = _adamw_math(w_ref[...], g_ref[...], m_ref[...], v_ref[...])
            outs[3 * i][...] = d
            outs[3 * i + 1][...] = mn
            outs[3 * i + 2][...] = vn

    flat = [a for it in items for a in it]
    outs = pl.pallas_call(
        body, name="adamw_small",
        out_shape=tuple(jax.ShapeDtypeStruct(it[0].shape, F32) for it in items for _ in range(3)),
        in_specs=[_vmem()] * (4 * n),
        out_specs=tuple(_vmem() for _ in range(3 * n)),
        compiler_params=_params(),
    )(*flat)
    return [tuple(outs[3 * i:3 * i + 3]) for i in range(n)]


def kernel(x, c, w_ada, b_ada, norm_g, w_in, conv_w, conv_b, conv_ln_g, conv_ln_b, sg_ln_g, sg_ln_b, w_s, b_s, w_out, final_g, loss_target, m_w_ada, m_b_ada, m_norm_g, m_w_in, m_conv_w, m_conv_b, m_conv_ln_g, m_conv_ln_b, m_sg_ln_g, m_sg_ln_b, m_w_s, m_b_s, m_w_out, m_final_g, v_w_ada, v_b_ada, v_norm_g, v_w_in, v_conv_w, v_conv_b, v_conv_ln_g, v_conv_ln_b, v_sg_ln_g, v_sg_ln_b, v_w_s, v_b_s, v_w_out, v_final_g):
    s_len = x.shape[1]
    x2d = x[0]
    tgt = loss_target[0]
    row = lambda a: a.reshape(1, -1)

    cw_sh = jnp.pad(conv_w.reshape(CONV_WIDTH, 256), ((0, 1), (0, 0)))
    w_in12, w_out4, cw4 = _gather_weights(w_in[0], w_out[0], cw_sh)
    w12 = w_in12.reshape(12, D_MODEL, 512)
    w_out_full = w_out4.reshape(2 * D_MODEL, D_MODEL)
    mod, cact = _mod(c, w_ada[0], b_ada)

    h, z6 = _fwd_in(x2d, mod, norm_g, w12)
    q = _conv_fwd(z6, cw4, conv_b)
    ln_rows = (conv_ln_g, conv_ln_b, sg_ln_g, sg_ln_b, row(final_g))
    bs_exp = jnp.repeat(b_s[0].T, CHUNK, axis=1)
    dz6, dq, dx2, ycat, dy, dws, sm1, dbs = _middle(x2d, z6, q, tgt, ln_rows, mod, w_s[0], bs_exp, w_out_full)
    dz6, dcw = _conv_bwd(dq, z6, cw4, dz6)
    grad_x, sm2 = _bwd_in(dz6, w12, x2d, dx2, mod, norm_g)
    gw12 = _grad_w_in(h, dz6)
    gwout = _grad_w_out(ycat, dy)

    pack, dws_r, dbs_r, _, dcw_sh, dcb, loss_t = _reduce_small(sm1, sm2, dws.reshape(D_MODEL, CHUNK), dbs, dcw)
    g_w_in4 = _reduce_scatter(gw12.reshape(N_CHIPS, 3, 2, 512, 512), "reduce_scatter_w_in")
    g_w_out4 = _reduce_scatter(gwout.reshape(N_CHIPS, 1, 2, 256, D_MODEL), "reduce_scatter_w_out")

    g_w_in, d_w_in, nm_w_in, nv_w_in = _adamw_blocked(w_in[0], m_w_in[0], v_w_in[0], g_w_in4, "adamw_w_in")
    g_w_out, d_w_out, nm_w_out, nv_w_out = _adamw_blocked(w_out[0], m_w_out[0], v_w_out[0], g_w_out4, "adamw_w_out")
    g_w_ada, d_w_ada, nm_w_ada, nv_w_ada, g_b_ada = _adamw_w_ada(cact, pack, w_ada[0], m_w_ada[0], v_w_ada[0])

    g_norm_g, g_cln_g, g_cln_b, g_sln_g, g_sln_b, g_final = (pack[24 + i:25 + i] for i in range(6))
    loss = loss_t[0, 0]
    g_conv_w = dcw_sh[:CONV_WIDTH]
    g_w_s = dws_r
    g_b_s = dbs_r[:, :HEADS].T
    small = [
        (b_ada, g_b_ada, m_b_ada, v_b_ada),
        (norm_g, g_norm_g, m_norm_g, v_norm_g),
        (conv_w.reshape(CONV_WIDTH, 256), g_conv_w, m_conv_w.reshape(CONV_WIDTH, 256), v_conv_w.reshape(CONV_WIDTH, 256)),
        (conv_b, dcb, m_conv_b, v_conv_b),
        (conv_ln_g, g_cln_g, m_conv_ln_g, v_conv_ln_g),
        (conv_ln_b, g_cln_b, m_conv_ln_b, v_conv_ln_b),
        (sg_ln_g, g_sln_g, m_sg_ln_g, v_sg_ln_g),
        (sg_ln_b, g_sln_b, m_sg_ln_b, v_sg_ln_b),
        (w_s.reshape(D_MODEL, CHUNK), g_w_s, m_w_s.reshape(D_MODEL, CHUNK), v_w_s.reshape(D_MODEL, CHUNK)),
        (b_s[0], g_b_s, m_b_s[0], v_b_s[0]),
        (row(final_g), g_final, row(m_final_g), row(v_final_g)),
    ]
    upd = _adamw_small(small)

    shapes = [w_ada.shape, b_ada.shape, norm_g.shape, w_in.shape, conv_w.shape, conv_b.shape, conv_ln_g.shape,
              conv_ln_b.shape, sg_ln_g.shape, sg_ln_b.shape, w_s.shape, b_s.shape, w_out.shape, final_g.shape]
    grads = [g_w_ada, g_b_ada, g_norm_g, g_w_in, g_conv_w, dcb, g_cln_g, g_cln_b, g_sln_g, g_sln_b, g_w_s, g_b_s,
             g_w_out, g_final]
    big = {0: (d_w_ada, nm_w_ada, nv_w_ada), 3: (d_w_in, nm_w_in, nv_w_in), 12: (d_w_out, nm_w_out, nv_w_out)}
    small_pos = [1, 2, 4, 5, 6, 7, 8, 9, 10, 11, 13]
    trip = [None] * 14
    for i, t in big.items():
        trip[i] = t
    for i, t in zip(small_pos, upd):
        trip[i] = t
    fit = lambda arrs: [a.reshape(s) for a, s in zip(arrs, shapes)]
    return (loss, grad_x.reshape(x.shape), *fit(grads), *fit([t[0] for t in trip]), *fit([t[1] for t in trip]),
            *fit([t[2] for t in trip]))
```

```python
import functools

import jax
import jax.numpy as jnp
from jax import lax
from jax.experimental import pallas as pl
from jax.experimental.pallas import tpu as pltpu

F32 = jnp.float32
BF16 = jnp.bfloat16
MESH = pl.DeviceIdType.MESH

D_MODEL = 1024
N_CHIPS = 4
HEADS = 8
CHUNK = 128
CONV_WIDTH = 31
CONV_HALF = CONV_WIDTH // 2
CONV_PAD = 16
EPS = 1e-6
ADAM_LR = 0.001
ADAM_B1 = 0.9
ADAM_B2 = 0.999
ADAM_EPS = 1e-08
ADAM_WD = 0.01
ADAM_STEP = 10

V7X_VMEM_BYTES = 64 * 1024 * 1024
VMEM_LIMIT = V7X_VMEM_BYTES - 8 * 1024 * 1024
ROWS = 16
UNROLL = 8
TOKEN_TILE = 256
TIME_TILE = 128
K_TILE = 2048

N_GROUPS = 6


def _natural_group(j):
    return (j + 2) % N_GROUPS


def _pos():
    return lax.axis_index("x"), lax.axis_index("y"), lax.axis_index("c")


def _rcopy(src, dst, ssem, rsem, dev):
    return pltpu.make_async_remote_copy(src_ref=src, dst_ref=dst, send_sem=ssem, recv_sem=rsem,
                                        device_id=dev, device_id_type=MESH)


def _vmem():
    return pl.BlockSpec(memory_space=pltpu.VMEM)


def _params(**kw):
    return pltpu.CompilerParams(vmem_limit_bytes=VMEM_LIMIT, **kw)


def _sigmoid(v):
    return jax.nn.sigmoid(v)


def _row_loop(n_rows, body, unroll=1):
    def step(r, carry):
        body(pl.ds(pl.multiple_of(r * ROWS, ROWS), ROWS))
        return carry
    lax.fori_loop(0, n_rows // ROWS, step, 0, unroll=unroll)


def _colsum8(v):
    return v.reshape(v.shape[0] // 8, 8, v.shape[1]).sum(axis=0)


def _mean(v):
    return jnp.mean(v, axis=-1, keepdims=True)


def _dot_nn(a, b):
    return jnp.dot(a, b, preferred_element_type=F32)


def _dot_nt(a, b):
    return lax.dot_general(a, b, (((1,), (1,)), ((), ())), preferred_element_type=F32)


def _dot_tn(a, b):
    return lax.dot_general(a, b, (((0,), (0,)), ((), ())), preferred_element_type=F32)


def _gather_weights(w_in, w_out, cw):
    def body(win_ref, wout_ref, cw_ref, win12_ref, wout4_ref, cw4_ref, ssem, rsem):
        x, y, c = _pos()
        k = 2 * x + y
        sib = (x, y, 1 - c)
        others = [(1 - x, y), (x, 1 - y), (1 - x, 1 - y)]
        for j in range(3):
            for hf in range(2):
                win12_ref[3 * k + j, hf] = win_ref[hf * 512:(hf + 1) * 512, j * 512:(j + 1) * 512].astype(BF16)
        for hf in range(2):
            wout4_ref[k, hf] = wout_ref[hf * 256:(hf + 1) * 256, :].astype(BF16)
        cw4_ref[k] = cw_ref[...]

        def win_slab(kk, hf):
            return win12_ref.at[pl.ds(3 * kk, 3), hf]

        def wout_slab(kk, hf):
            return wout4_ref.at[kk, hf]

        first = []
        for j, (ox, oy) in enumerate(others):
            dev = (ox, oy, c)
            first.append(_rcopy(win_slab(k, c), win_slab(k, c), ssem.at[j], rsem.at[j], dev))
            first.append(_rcopy(wout_slab(k, c), wout_slab(k, c), ssem.at[3 + j], rsem.at[3 + j], dev))
            first.append(_rcopy(cw4_ref.at[k], cw4_ref.at[k], ssem.at[6 + j], rsem.at[6 + j], dev))
        for cp in first:
            cp.start()
        passed = []
        for j, (ox, oy) in enumerate(others):
            ko = 2 * ox + oy
            dev = (ox, oy, c)
            _rcopy(win_slab(ko, c), win_slab(ko, c), ssem.at[j], rsem.at[j], dev).wait_recv()
            f1 = _rcopy(win_slab(ko, c), win_slab(ko, c), ssem.at[9 + j], rsem.at[9 + j], sib)
            f1.start()
            _rcopy(wout_slab(ko, c), wout_slab(ko, c), ssem.at[3 + j], rsem.at[3 + j], dev).wait_recv()
            f2 = _rcopy(wout_slab(ko, c), wout_slab(ko, c), ssem.at[12 + j], rsem.at[12 + j], sib)
            f2.start()
            passed += [f1, f2]
        for j, (ox, oy) in enumerate(others):
            ko = 2 * ox + oy
            _rcopy(cw4_ref.at[ko], cw4_ref.at[ko], ssem.at[6 + j], rsem.at[6 + j], (ox, oy, c)).wait_recv()
            _rcopy(win_slab(ko, 1 - c), win_slab(ko, 1 - c), ssem.at[9 + j], rsem.at[9 + j], sib).wait_recv()
            _rcopy(wout_slab(ko, 1 - c), wout_slab(ko, 1 - c), ssem.at[12 + j], rsem.at[12 + j], sib).wait_recv()
        for cp in first + passed:
            cp.wait_send()

    return pl.pallas_call(
        body, name="gather_weights",
        out_shape=(jax.ShapeDtypeStruct((12, 2, 512, 512), BF16),
                   jax.ShapeDtypeStruct((N_CHIPS, 2, 256, D_MODEL), BF16),
                   jax.ShapeDtypeStruct((N_CHIPS, 32, 256), F32)),
        in_specs=[_vmem(), _vmem(), _vmem()],
        out_specs=(_vmem(), _vmem(), _vmem()),
        scratch_shapes=[pltpu.SemaphoreType.DMA((15,)), pltpu.SemaphoreType.DMA((15,))],
        compiler_params=_params(),
    )(w_in, w_out, cw)


def _butterfly(bufs, recvs, ssem, rsem, partners, first_sem=0):
    s = first_sem
    for step, dev in enumerate(partners):
        cps = []
        for buf, recv in zip(bufs, recvs):
            cp = _rcopy(buf, recv.at[step], ssem.at[s], rsem.at[s], dev)
            cp.start()
            cps.append(cp)
            s += 1
        for cp in cps:
            cp.wait()
        for buf, recv in zip(bufs, recvs):
            buf[...] = buf[...] + recv[step]
    return s


def _mod(c, w_ada, b_ada):
    def body(c_ref, wada_ref, bada_ref, mod_ref, cact_ref, crecv, mbuf, mrecv, ssem, rsem):
        x, y, c_ = _pos()
        k = 2 * x + y
        b = 4 * x + 2 * y + c_
        row = lax.broadcasted_iota(jnp.int32, (8, D_MODEL), 0)
        cact_ref[...] = jnp.where(row == b, jnp.broadcast_to(c_ref[...], (8, D_MODEL)), 0.0)
        n = _butterfly([cact_ref], [crecv], ssem, rsem, [(x, y, 1 - c_), (x, 1 - y, c_), (1 - x, y, c_)])
        call = cact_ref[...]
        cact = call * _sigmoid(call)
        cact_ref[...] = cact
        part = _dot_nn(cact.astype(BF16), wada_ref[...].astype(BF16))
        mbuf[...] = jnp.zeros_like(mbuf)
        mbuf[k] = part
        _butterfly([mbuf], [mrecv], ssem, rsem, [(x, 1 - y, c_), (1 - x, y, c_)], first_sem=n)
        row8 = lax.broadcasted_iota(jnp.int32, (8, 768), 0)
        for kk in range(N_CHIPS):
            piece = jnp.sum(jnp.where(row8 == b, mbuf[kk], 0.0), axis=0, keepdims=True)
            mod_ref[:, kk * 768:(kk + 1) * 768] = piece + bada_ref[:, kk * 768:(kk + 1) * 768]

    return pl.pallas_call(
        body, name="mod",
        out_shape=(jax.ShapeDtypeStruct((1, 3 * D_MODEL), F32), jax.ShapeDtypeStruct((8, D_MODEL), F32)),
        in_specs=[_vmem(), _vmem(), _vmem()],
        out_specs=(_vmem(), _vmem()),
        scratch_shapes=[pltpu.VMEM((3, 8, D_MODEL), F32), pltpu.VMEM((N_CHIPS, 8, 768), F32),
                        pltpu.VMEM((2, N_CHIPS, 8, 768), F32),
                        pltpu.SemaphoreType.DMA((5,)), pltpu.SemaphoreType.DMA((5,))],
        compiler_params=_params(),
    )(c, w_ada, b_ada)


def _reduce_small(sm1, sm2, dws, dbs, dcw):
    def body(sm1_ref, sm2_ref, dws_in, dbs_in, dcw_in, pack_ref, dws_ref, dbs_ref, dcw_ref, dcwsh_ref, dcb_ref, loss_ref,
             r_pack, r_dws, r_dbs, r_dcw, ssem, rsem):
        x, y, c = _pos()
        k = 2 * x + y
        b = 4 * x + 2 * y + c
        row = lax.broadcasted_iota(jnp.int32, (8, D_MODEL), 0)

        def onehot(v):
            return jnp.where(row == b, jnp.broadcast_to(v, (8, D_MODEL)), 0.0)

        pack_ref[0:8, :] = onehot(sm2_ref[0:1, :])
        pack_ref[8:16, :] = onehot(sm2_ref[1:2, :])
        pack_ref[16:24, :] = onehot(sm1_ref[1:2, :])
        order = [sm2_ref[2:3, :], sm1_ref[2:3, :], sm1_ref[3:4, :], sm1_ref[4:5, :], sm1_ref[5:6, :],
                 sm1_ref[0:1, :], sm1_ref[6:7, :], jnp.zeros((1, D_MODEL), F32)]
        for i, v in enumerate(order):
            pack_ref[24 + i:25 + i, :] = v
        dws_ref[...] = dws_in[...]
        dbs_ref[...] = dbs_in[...]
        dcw_ref[...] = dcw_in[...]
        _butterfly([pack_ref, dws_ref, dbs_ref, dcw_ref], [r_pack, r_dws, r_dbs, r_dcw], ssem, rsem,
                   [(x, y, 1 - c), (x, 1 - y, c), (1 - x, y, c)])
        sel = jnp.zeros((32, 256), F32)
        for kk in range(N_CHIPS):
            sel = jnp.where(k == kk, dcw_ref[:, kk * 256:(kk + 1) * 256], sel)
        dcwsh_ref[...] = sel
        dcb_ref[...] = dcw_ref[31:32, :]
        loss_ref[...] = jnp.broadcast_to(jnp.sum(pack_ref[30:31, :], axis=1, keepdims=True), (8, 128))

    shp = [(32, D_MODEL), (D_MODEL, 128), (128, 128), (32, D_MODEL)]
    return pl.pallas_call(
        body, name="reduce_small",
        out_shape=tuple(jax.ShapeDtypeStruct(s, F32) for s in shp) + (
            jax.ShapeDtypeStruct((32, 256), F32), jax.ShapeDtypeStruct((1, D_MODEL), F32),
            jax.ShapeDtypeStruct((8, 128), F32)),
        in_specs=[_vmem()] * 5,
        out_specs=tuple(_vmem() for _ in range(7)),
        scratch_shapes=[pltpu.VMEM((3,) + s, F32) for s in shp] + [
            pltpu.SemaphoreType.DMA((12,)), pltpu.SemaphoreType.DMA((12,))],
        compiler_params=_params(),
    )(sm1, sm2, dws, dbs, dcw)


def _fwd_in(x, mod, norm_g, w12):
    s_len = x.shape[0]
    tm = TOKEN_TILE

    def body(x_ref, mod_ref, g_ref, w_ref, h_ref, z_ref):
        shift = mod_ref[:, 0:D_MODEL]
        scale1 = 1.0 + mod_ref[:, D_MODEL:2 * D_MODEL]
        g = g_ref[...]

        def rows_fn(rows):
            xt = x_ref[rows, :]
            r = lax.rsqrt(_mean(xt * xt) + EPS)
            h_ref[rows, :] = ((xt * r * g) * scale1 + shift).astype(BF16)
        _row_loop(tm, rows_fn, unroll=UNROLL)
        hb = h_ref[...]
        for j in range(N_GROUPS):
            n = _natural_group(j)
            for hf in range(2):
                z_ref[j, :, hf * 512:(hf + 1) * 512] = _dot_nn(hb, w_ref[2 * n + hf])

    return pl.pallas_call(
        body, name="fwd_in",
        grid=(s_len // tm,),
        out_shape=(jax.ShapeDtypeStruct((s_len, D_MODEL), BF16), jax.ShapeDtypeStruct((N_GROUPS, s_len, D_MODEL), F32)),
        in_specs=[pl.BlockSpec((tm, D_MODEL), lambda i: (i, 0)),
                  pl.BlockSpec((1, 3 * D_MODEL), lambda i: (0, 0)),
                  pl.BlockSpec((1, D_MODEL), lambda i: (0, 0)),
                  pl.BlockSpec((12, D_MODEL, 512), lambda i: (0, 0, 0), pipeline_mode=pl.Buffered(1))],
        out_specs=(pl.BlockSpec((tm, D_MODEL), lambda i: (i, 0)),
                   pl.BlockSpec((N_GROUPS, tm, D_MODEL), lambda i: (0, i, 0))),
        compiler_params=_params(dimension_semantics=("arbitrary",)),
    )(x, mod, norm_g, w12)


def _shifted_windows(win, n_out):
    n = win.shape[0]
    for s in range(8):
        ws = win if s == 0 else pltpu.roll(win, n - s, axis=0)
        for a in range(4):
            o = 8 * a + s
            if 1 <= o <= CONV_WIDTH:
                yield o, ws[8 * a:8 * a + n_out, :]


def _conv_fwd(z6, cw4, conv_b):
    s_len = z6.shape[1]
    tt = TIME_TILE

    def body(z_ref, cw_ref, cb_ref, q_ref, ppad):
        zero = jnp.zeros((CONV_PAD, 128), F32)
        ppad[0:CONV_PAD, :] = zero
        ppad[s_len + CONV_PAD:s_len + 2 * CONV_PAD, :] = zero

        def fill(i, carry):
            t0 = pl.multiple_of(i * tt, tt)
            ppad[pl.ds(CONV_PAD + t0, tt), :] = z_ref[0, pl.ds(t0, tt), :] * _sigmoid(z_ref[1, pl.ds(t0, tt), :])
            return carry
        lax.fori_loop(0, s_len // tt, fill, 0)
        w = cw_ref[0]
        bias = cb_ref[...]

        def conv(i, carry):
            t0 = pl.multiple_of(i * tt, tt)
            win = ppad[pl.ds(t0, tt + 2 * CONV_PAD), :]
            acc = jnp.broadcast_to(bias, (tt, 128))
            for o, sl in _shifted_windows(win, tt):
                acc = acc + w[o - 1:o, :] * sl
            q_ref[pl.ds(t0, tt), :] = acc
            return carry
        lax.fori_loop(0, s_len // tt, conv, 0)

    return pl.pallas_call(
        body, name="conv_fwd",
        grid=(D_MODEL // 128,),
        out_shape=jax.ShapeDtypeStruct((s_len, D_MODEL), F32),
        in_specs=[pl.BlockSpec((2, s_len, 128), lambda j: (2, 0, j)),
                  pl.BlockSpec((1, 32, 128), lambda j: (j // 2, 0, j % 2)),
                  pl.BlockSpec((1, 128), lambda j: (0, j))],
        out_specs=pl.BlockSpec((s_len, 128), lambda j: (0, j)),
        scratch_shapes=[pltpu.VMEM((s_len + 2 * CONV_PAD, 128), F32)],
        compiler_params=_params(dimension_semantics=("arbitrary",)),
    )(z6, cw4, conv_b)


def _middle(x, z6, q, target, ln_rows, mod, w_s, bs_exp, w_out):
    s_len = x.shape[0]
    tm = TOKEN_TILE
    n_steps = s_len // tm
    n_chunks = tm // CHUNK
    inv_d = 1.0 / D_MODEL

    def body(x_ref, z_ref, q_ref, tgt_ref, cg_ref, cb_ref, sg_ref, sb_ref, fg_ref, mod_ref, ws_ref, bs_ref, wout_ref,
             dz_ref, dq_ref, dx2_ref, ycat_ref, dy_ref, dws_ref, sm_ref, dbs_ref,
             vl_scr, vm_scr, y_scr, dycat_scr, dvm_scr, dvl_scr, acc_scr, dbs_acc):
        i = pl.program_id(0)

        @pl.when(i == 0)
        def _():
            acc_scr[...] = jnp.zeros_like(acc_scr)
            dbs_acc[...] = jnp.zeros_like(dbs_acc)
            dws_ref[...] = jnp.zeros_like(dws_ref)

        cg, cb, sg, sb, fg = cg_ref[...], cb_ref[...], sg_ref[...], sb_ref[...], fg_ref[...]
        gm = mod_ref[:, 2 * D_MODEL:3 * D_MODEL]

        def norm_stats(t):
            c = t - _mean(t)
            rstd = lax.rsqrt(_mean(c * c) + EPS)
            return c * rstd, rstd

        def phase1(rows):
            qhat, _ = norm_stats(q_ref[rows, :])
            ln = qhat * cg + cb
            gz = z_ref[0, rows, :]
            ycat_ref[rows, 0:D_MODEL] = ((ln * _sigmoid(ln)) * (gz * _sigmoid(gz))).astype(BF16)
            vhat, _ = norm_stats(z_ref[2, rows, :])
            vl_scr[rows, :] = (vhat * sg + sb).astype(BF16)
        _row_loop(tm, phase1, unroll=UNROLL)

        for ch in range(n_chunks):
            r0 = ch * CHUNK
            for h in range(HEADS):
                c0 = h * CHUNK
                vm_scr[r0:r0 + CHUNK, c0:c0 + CHUNK] = (
                    _dot_nn(ws_ref[h].astype(BF16), vl_scr[r0:r0 + CHUNK, c0:c0 + CHUNK]) + bs_ref[:, c0:c0 + CHUNK])

        def phase3(rows):
            bg = z_ref[3, rows, :]
            ycat_ref[rows, D_MODEL:2 * D_MODEL] = (z_ref[1, rows, :] * vm_scr[rows, :] * (bg * _sigmoid(bg))).astype(BF16)
        _row_loop(tm, phase3)

        y_scr[...] = _dot_nn(ycat_ref[...], wout_ref[...])

        def phase5(rows):
            y = y_scr[rows, :]
            x2 = x_ref[rows, :] + gm * y
            r2 = lax.rsqrt(_mean(x2 * x2) + EPS)
            xn2 = x2 * r2
            diff = xn2 * fg - tgt_ref[rows, :]
            acc_scr[6] += _colsum8(diff * diff)
            dout = diff * inv_d
            acc_scr[0] += _colsum8(dout * xn2)
            dxn = dout * fg
            dx2 = r2 * (dxn - xn2 * _mean(dxn * xn2))
            dx2_ref[rows, :] = dx2
            acc_scr[1] += _colsum8(dx2 * y)
            dy_ref[rows, :] = (dx2 * gm).astype(BF16)
        _row_loop(tm, phase5, unroll=UNROLL)

        dycat_scr[...] = _dot_nt(dy_ref[...], wout_ref[...])

        def phase7(rows):
            dyb = dycat_scr[rows, D_MODEL:2 * D_MODEL]
            u = z_ref[1, rows, :]
            bg = z_ref[3, rows, :]
            vm = vm_scr[rows, :]
            sig = _sigmoid(bg)
            silu = bg * sig
            dz_ref[1, rows, :] = (dyb * vm * silu).astype(BF16)
            dvm = dyb * u * silu
            dz_ref[3, rows, :] = (dyb * u * vm * (sig * (1.0 + bg * (1.0 - sig)))).astype(BF16)
            dvm_scr[rows, :] = dvm.astype(BF16)
            pos = pl.ds(pl.multiple_of(rows.start % CHUNK, ROWS), ROWS)
            dbs_acc[pos, :] += dvm
        _row_loop(tm, phase7)

        for ch in range(n_chunks):
            r0 = ch * CHUNK
            for h in range(HEADS):
                c0 = h * CHUNK
                dvm_b = dvm_scr[r0:r0 + CHUNK, c0:c0 + CHUNK]
                dws_ref[h] += _dot_nt(dvm_b, vl_scr[r0:r0 + CHUNK, c0:c0 + CHUNK])
                dvl_scr[r0:r0 + CHUNK, c0:c0 + CHUNK] = _dot_tn(ws_ref[h].astype(BF16), dvm_b)

        def phase9(rows):
            vhat, rstd_v = norm_stats(z_ref[2, rows, :])
            dvl = dvl_scr[rows, :]
            acc_scr[4] += _colsum8(dvl * vhat)
            acc_scr[5] += _colsum8(dvl)
            dvh = dvl * sg
            dz_ref[2, rows, :] = (rstd_v * (dvh - _mean(dvh) - vhat * _mean(dvh * vhat))).astype(BF16)
            qhat, rstd_q = norm_stats(q_ref[rows, :])
            ln = qhat * cg + cb
            sig_ln = _sigmoid(ln)
            gz = z_ref[0, rows, :]
            sig_g = _sigmoid(gz)
            dya = dycat_scr[rows, 0:D_MODEL]
            dz_ref[0, rows, :] = (dya * (ln * sig_ln) * (sig_g * (1.0 + gz * (1.0 - sig_g)))).astype(BF16)
            dln = (dya * (gz * sig_g)) * (sig_ln * (1.0 + ln * (1.0 - sig_ln)))
            acc_scr[2] += _colsum8(dln * qhat)
            acc_scr[3] += _colsum8(dln)
            dqh = dln * cg
            dq_ref[rows, :] = rstd_q * (dqh - _mean(dqh) - qhat * _mean(dqh * qhat))
        _row_loop(tm, phase9, unroll=UNROLL)

        @pl.when(i == n_steps - 1)
        def _():
            for qi in range(8):
                scale = 0.5 * inv_d if qi == 6 else 1.0
                sm_ref[qi:qi + 1, :] = jnp.sum(acc_scr[qi], axis=0, keepdims=True) * scale
            lane = lax.broadcasted_iota(jnp.int32, (CHUNK, CHUNK), 1)
            tile = jnp.zeros((CHUNK, CHUNK), F32)
            for h in range(HEADS):
                col = jnp.sum(dbs_acc[:, h * CHUNK:(h + 1) * CHUNK], axis=1, keepdims=True)
                tile = jnp.where(lane == h, col, tile)
            dbs_ref[...] = tile

    tok = lambda i: (i, 0)
    const2 = lambda i: (0, 0)
    return pl.pallas_call(
        body, name="middle",
        grid=(n_steps,),
        out_shape=(jax.ShapeDtypeStruct((N_GROUPS, s_len, D_MODEL), BF16),
                   jax.ShapeDtypeStruct((s_len, D_MODEL), F32),
                   jax.ShapeDtypeStruct((s_len, D_MODEL), F32),
                   jax.ShapeDtypeStruct((s_len, 2 * D_MODEL), BF16),
                   jax.ShapeDtypeStruct((s_len, D_MODEL), BF16),
                   jax.ShapeDtypeStruct((HEADS, CHUNK, CHUNK), F32),
                   jax.ShapeDtypeStruct((8, D_MODEL), F32),
                   jax.ShapeDtypeStruct((CHUNK, CHUNK), F32)),
        in_specs=[pl.BlockSpec((tm, D_MODEL), tok),
                  pl.BlockSpec((4, tm, D_MODEL), lambda i: (0, i, 0)),
                  pl.BlockSpec((tm, D_MODEL), tok),
                  pl.BlockSpec((tm, D_MODEL), tok),
                  *[pl.BlockSpec((1, D_MODEL), const2) for _ in range(5)],
                  pl.BlockSpec((1, 3 * D_MODEL), const2),
                  pl.BlockSpec((HEADS, CHUNK, CHUNK), lambda i: (0, 0, 0)),
                  pl.BlockSpec((CHUNK, D_MODEL), const2),
                  pl.BlockSpec((2 * D_MODEL, D_MODEL), const2, pipeline_mode=pl.Buffered(1))],
        out_specs=(pl.BlockSpec((4, tm, D_MODEL), lambda i: (0, i, 0)),
                   pl.BlockSpec((tm, D_MODEL), tok),
                   pl.BlockSpec((tm, D_MODEL), tok),
                   pl.BlockSpec((tm, 2 * D_MODEL), tok),
                   pl.BlockSpec((tm, D_MODEL), tok),
                   pl.BlockSpec((HEADS, CHUNK, CHUNK), lambda i: (0, 0, 0)),
                   pl.BlockSpec((8, D_MODEL), const2),
                   pl.BlockSpec((CHUNK, CHUNK), const2)),
        scratch_shapes=[pltpu.VMEM((tm, D_MODEL), BF16),
                        pltpu.VMEM((tm, D_MODEL), F32),
                        pltpu.VMEM((tm, D_MODEL), F32),
                        pltpu.VMEM((tm, 2 * D_MODEL), F32),
                        pltpu.VMEM((tm, D_MODEL), BF16),
                        pltpu.VMEM((tm, D_MODEL), F32),
                        pltpu.VMEM((8, 8, D_MODEL), F32),
                        pltpu.VMEM((CHUNK, D_MODEL), F32)],
        compiler_params=_params(dimension_semantics=("arbitrary",)),
    )(x, z6, q, target, *ln_rows, mod, w_s, bs_exp, w_out)


def _conv_bwd(dq, z6, cw4, dz6):
    s_len = dq.shape[0]
    tt = TIME_TILE

    def body(dq_ref, z_ref, cw_ref, dz_in, dz_ref, dcw_ref, dqpad, ppad, wacc):
        del dz_in
        zero = jnp.zeros((CONV_PAD, 128), F32)
        for pad in (dqpad, ppad):
            pad[0:CONV_PAD, :] = zero
            pad[s_len + CONV_PAD:s_len + 2 * CONV_PAD, :] = zero
        wacc[...] = jnp.zeros_like(wacc)

        def fill(i, carry):
            t0 = pl.multiple_of(i * tt, tt)
            ppad[pl.ds(CONV_PAD + t0, tt), :] = z_ref[0, pl.ds(t0, tt), :] * _sigmoid(z_ref[1, pl.ds(t0, tt), :])
            dqpad[pl.ds(CONV_PAD + t0, tt), :] = dq_ref[pl.ds(t0, tt), :]
            return carry
        lax.fori_loop(0, s_len // tt, fill, 0)
        w = cw_ref[0]

        def bwd(i, carry):
            t0 = pl.multiple_of(i * tt, tt)
            dp = jnp.zeros((tt, 128), F32)
            for o, sl in _shifted_windows(dqpad[pl.ds(t0, tt + 2 * CONV_PAD), :], tt):
                dp = dp + w[CONV_WIDTH - o:CONV_WIDTH - o + 1, :] * sl
            a = z_ref[0, pl.ds(t0, tt), :]
            sig = _sigmoid(z_ref[1, pl.ds(t0, tt), :])
            dz_ref[0, pl.ds(t0, tt), :] = (dp * sig).astype(BF16)
            dz_ref[1, pl.ds(t0, tt), :] = (dp * a * (sig * (1.0 - sig))).astype(BF16)
            dqt = dq_ref[pl.ds(t0, tt), :]
            for o, sl in _shifted_windows(ppad[pl.ds(t0, tt + 2 * CONV_PAD), :], tt):
                wacc[o - 1] += _colsum8(dqt * sl)
            wacc[CONV_WIDTH] += _colsum8(dqt)
            return carry
        lax.fori_loop(0, s_len // tt, bwd, 0)
        for k in range(32):
            dcw_ref[k:k + 1, :] = jnp.sum(wacc[k], axis=0, keepdims=True)

    return pl.pallas_call(
        body, name="conv_bwd",
        grid=(D_MODEL // 128,),
        out_shape=(jax.ShapeDtypeStruct(dz6.shape, BF16), jax.ShapeDtypeStruct((32, D_MODEL), F32)),
        in_specs=[pl.BlockSpec((s_len, 128), lambda j: (0, j)),
                  pl.BlockSpec((2, s_len, 128), lambda j: (2, 0, j)),
                  pl.BlockSpec((1, 32, 128), lambda j: (j // 2, 0, j % 2)),
                  pl.BlockSpec(memory_space=pl.ANY)],
        out_specs=(pl.BlockSpec((2, s_len, 128), lambda j: (2, 0, j)),
                   pl.BlockSpec((32, 128), lambda j: (0, j))),
        scratch_shapes=[pltpu.VMEM((s_len + 2 * CONV_PAD, 128), F32), pltpu.VMEM((s_len + 2 * CONV_PAD, 128), F32),
                        pltpu.VMEM((32, 8, 128), F32)],
        input_output_aliases={3: 0},
        compiler_params=_params(dimension_semantics=("arbitrary",)),
    )(dq, z6, cw4, dz6)


def _bwd_in(dz6, w12, x, dx2, mod, norm_g):
    s_len = x.shape[0]
    tm = TOKEN_TILE
    n_steps = s_len // tm

    def body(dz_ref, w_ref, x_ref, dx2_ref, mod_ref, g_ref, gx_ref, sm_ref, dh_scr, acc_scr):
        i = pl.program_id(0)

        @pl.when(i == 0)
        def _():
            acc_scr[...] = jnp.zeros_like(acc_scr)

        dh = jnp.zeros((tm, D_MODEL), F32)
        for j in range(N_GROUPS):
            n = _natural_group(j)
            for hf in range(2):
                dh = dh + _dot_nt(dz_ref[j, :, hf * 512:(hf + 1) * 512], w_ref[2 * n + hf])
        dh_scr[...] = dh
        scale1 = 1.0 + mod_ref[:, D_MODEL:2 * D_MODEL]
        g = g_ref[...]

        def rows_fn(rows):
            xt = x_ref[rows, :]
            r = lax.rsqrt(_mean(xt * xt) + EPS)
            xn = xt * r
            dhr = dh_scr[rows, :]
            acc_scr[0] += _colsum8(dhr)
            acc_scr[1] += _colsum8(dhr * (xn * g))
            acc_scr[2] += _colsum8(dhr * scale1 * xn)
            dxn = dhr * (g * scale1)
            gx_ref[rows, :] = dx2_ref[rows, :] + r * (dxn - xn * _mean(dxn * xn))
        _row_loop(tm, rows_fn, unroll=UNROLL)

        @pl.when(i == n_steps - 1)
        def _():
            for qi in range(8):
                sm_ref[qi:qi + 1, :] = jnp.sum(acc_scr[qi], axis=0, keepdims=True)

    tok = lambda i: (i, 0)
    const2 = lambda i: (0, 0)
    return pl.pallas_call(
        body, name="bwd_in",
        grid=(n_steps,),
        out_shape=(jax.ShapeDtypeStruct((s_len, D_MODEL), F32), jax.ShapeDtypeStruct((8, D_MODEL), F32)),
        in_specs=[pl.BlockSpec((N_GROUPS, tm, D_MODEL), lambda i: (0, i, 0)),
                  pl.BlockSpec((12, D_MODEL, 512), lambda i: (0, 0, 0), pipeline_mode=pl.Buffered(1)),
                  pl.BlockSpec((tm, D_MODEL), tok),
                  pl.BlockSpec((tm, D_MODEL), tok),
                  pl.BlockSpec((1, 3 * D_MODEL), const2),
                  pl.BlockSpec((1, D_MODEL), const2)],
        out_specs=(pl.BlockSpec((tm, D_MODEL), tok), pl.BlockSpec((8, D_MODEL), const2)),
        scratch_shapes=[pltpu.VMEM((tm, D_MODEL), F32), pltpu.VMEM((8, 8, D_MODEL), F32)],
        compiler_params=_params(dimension_semantics=("arbitrary",)),
    )(dz6, w12, x, dx2, mod, norm_g)


def _grad_reduce_scatter(kidx, a, b, *, name, nj, nr, nc, a_spec, b_spec, b_lead, kt):
    nblk = N_CHIPS * nj
    nfor = 3 * nj

    def body(k_ref, a_ref, b_ref, out_ref, acc, recv_a, send_b, recv_b, own_ps,
             lsem, psend, precv, isend, irecv, fsend, frecv):
        del k_ref
        i = pl.program_id(0)
        kk = pl.program_id(1)
        x, y, c = _pos()
        k = 2 * x + y
        sib = (x, y, 1 - c)
        slot = i % 2

        @pl.when(kk == 0)
        def _():
            acc[slot] = jnp.zeros((2, nr, nc), F32)

        res = _dot_tn(a_ref[...], b_ref[0] if b_lead else b_ref[...])
        acc[slot, 0] += res[0:nr]
        acc[slot, 1] += res[nr:2 * nr]

        def pair_copy(ii, sl):
            return _rcopy(acc.at[sl, 1 - c], recv_a.at[ii], psend.at[ii], precv.at[ii], sib)

        def pair_sum(ii, sl):
            cp = pair_copy(ii, sl)
            cp.wait_recv()
            cp.wait_send()
            return acc[sl, c] + recv_a[ii]

        def ici_copy(ii, kd, rslot):
            return _rcopy(send_b.at[ii], recv_b.at[rslot], isend.at[ii], irecv.at[rslot], (kd // 2, kd % 2, c))

        def finalize(ii, sl):
            ps = pair_sum(ii, sl)
            s = ii // nj
            j = ii % nj

            @pl.when(s < 3)
            def _():
                send_b[ii] = ps.astype(BF16)
                ici_copy(ii, (k + 1 + s) % N_CHIPS, (2 - s) * nj + j).start()

            @pl.when(s == 3)
            def _():
                own_ps[j] = ps

        @pl.when(kk == kt - 1)
        def _():
            pair_copy(i, slot).start()

            @pl.when(i >= 1)
            def _():
                finalize(i - 1, 1 - slot)

            @pl.when(i == nblk - 1)
            def _():
                own_ps[nj - 1] = pair_sum(nblk - 1, (nblk - 1) % 2)
                for r in range(nfor):
                    ici_copy(0, k, r).wait_recv()
                for j in range(nj):
                    tot = own_ps[j]
                    for s in range(3):
                        tot = tot + recv_b[s * nj + j].astype(F32)
                    own_ps[j] = tot
                loc = pltpu.make_async_copy(own_ps, out_ref.at[:, c], lsem)
                loc.start()
                swap = _rcopy(own_ps, out_ref.at[:, c], fsend, frecv, sib)
                swap.start()
                loc.wait()
                swap.wait_send()
                _rcopy(own_ps, out_ref.at[:, 1 - c], fsend, frecv, sib).wait_recv()
                for ii in range(nfor):
                    ici_copy(ii, k, 0).wait_send()

    def seq(i, k_ref):
        return (nj * (k_ref[0] + 1) + i) % nblk

    grid_spec = pltpu.PrefetchScalarGridSpec(
        num_scalar_prefetch=1,
        grid=(nblk, kt),
        in_specs=[a_spec(seq), b_spec(seq)],
        out_specs=pl.BlockSpec(memory_space=pl.ANY),
        scratch_shapes=[pltpu.VMEM((2, 2, nr, nc), F32),
                        pltpu.VMEM((nblk, nr, nc), F32),
                        pltpu.VMEM((nfor, nr, nc), BF16),
                        pltpu.VMEM((nfor, nr, nc), BF16),
                        pltpu.VMEM((nj, nr, nc), F32),
                        pltpu.SemaphoreType.DMA,
                        pltpu.SemaphoreType.DMA((nblk,)), pltpu.SemaphoreType.DMA((nblk,)),
                        pltpu.SemaphoreType.DMA((nfor,)), pltpu.SemaphoreType.DMA((nfor,)),
                        pltpu.SemaphoreType.DMA, pltpu.SemaphoreType.DMA])
    return pl.pallas_call(
        body, name=name,
        grid_spec=grid_spec,
        out_shape=jax.ShapeDtypeStruct((nj, 2, nr, nc), F32),
        compiler_params=_params(dimension_semantics=("arbitrary", "arbitrary")),
    )(kidx, a, b)


def _grad_w_in(kidx, h, dz6):
    s_len = h.shape[0]
    tk = min(K_TILE, s_len)
    return _grad_reduce_scatter(
        kidx, h, dz6, name="grad_w_in", nj=3, nr=512, nc=512, b_lead=True, kt=s_len // tk,
        a_spec=lambda seq: pl.BlockSpec((tk, D_MODEL), lambda i, kk, k_ref: (kk, 0)),
        b_spec=lambda seq: pl.BlockSpec(
            (1, tk, 512), lambda i, kk, k_ref: ((seq(i, k_ref) // 2 + 4) % N_GROUPS, kk, seq(i, k_ref) % 2)))


def _grad_w_out(kidx, ycat, dy):
    s_len = dy.shape[0]
    tk = min(K_TILE, s_len)
    return _grad_reduce_scatter(
        kidx, ycat, dy, name="grad_w_out", nj=1, nr=256, nc=D_MODEL, b_lead=False, kt=s_len // tk,
        a_spec=lambda seq: pl.BlockSpec((tk, 512), lambda i, kk, k_ref: (kk, seq(i, k_ref))),
        b_spec=lambda seq: pl.BlockSpec((tk, D_MODEL), lambda i, kk, k_ref: (kk, 0)))


def _adamw_math(w, g, m, v):
    m = ADAM_B1 * m + (1.0 - ADAM_B1) * g
    v = ADAM_B2 * v + (1.0 - ADAM_B2) * (g * g)
    m_hat = m / (1.0 - ADAM_B1 ** ADAM_STEP)
    v_hat = v / (1.0 - ADAM_B2 ** ADAM_STEP)
    delta = -ADAM_LR * (m_hat / (jnp.sqrt(v_hat) + ADAM_EPS) + ADAM_WD * w)
    return delta, m, v


def _adamw_blocked(w, m, v, g4, name):
    nj, _, nr, nc = g4.shape

    def body(w_ref, m_ref, v_ref, g_ref, go_ref, d_ref, mo_ref, vo_ref):
        g = g_ref[0, 0]
        d, mn, vn = _adamw_math(w_ref[...], g, m_ref[...], v_ref[...])
        go_ref[...] = g
        d_ref[...] = d
        mo_ref[...] = mn
        vo_ref[...] = vn

    blk = pl.BlockSpec((nr, nc), lambda j, hf: (hf, j))
    return pl.pallas_call(
        body, name=name,
        grid=(nj, 2),
        out_shape=tuple(jax.ShapeDtypeStruct(w.shape, F32) for _ in range(4)),
        in_specs=[blk, blk, blk, pl.BlockSpec((1, 1, nr, nc), lambda j, hf: (j, hf, 0, 0))],
        out_specs=(blk, blk, blk, blk),
        compiler_params=_params(dimension_semantics=("arbitrary", "arbitrary")),
    )(w, m, v, g4)


def _adamw_w_ada(cact, pack, w, m, v):
    def body(cact_ref, pack_ref, w_ref, m_ref, v_ref, g_ref, d_ref, mo_ref, vo_ref, gb_ref, dmod):
        x, y, _ = _pos()
        k = 2 * x + y
        for t in range(3):
            dmod[:, t * D_MODEL:(t + 1) * D_MODEL] = pack_ref[8 * t:8 * t + 8, :]
        gb_ref[...] = jnp.sum(dmod[...], axis=0, keepdims=True)
        sel = jnp.zeros((8, 768), F32)
        for kk in range(N_CHIPS):
            sel = jnp.where(k == kk, dmod[:, kk * 768:(kk + 1) * 768], sel)
        g = _dot_tn(cact_ref[...].astype(BF16), sel.astype(BF16))
        d, mn, vn = _adamw_math(w_ref[...], g, m_ref[...], v_ref[...])
        g_ref[...] = g
        d_ref[...] = d
        mo_ref[...] = mn
        vo_ref[...] = vn

    return pl.pallas_call(
        body, name="adamw_w_ada",
        out_shape=tuple(jax.ShapeDtypeStruct(w.shape, F32) for _ in range(4)) + (
            jax.ShapeDtypeStruct((1, 3 * D_MODEL), F32),),
        in_specs=[_vmem()] * 5,
        out_specs=tuple(_vmem() for _ in range(5)),
        scratch_shapes=[pltpu.VMEM((8, 3 * D_MODEL), F32)],
        compiler_params=_params(),
    )(cact, pack, w, m, v)


def _adamw_small(items):
    n = len(items)

    def body(*refs):
        ins, outs = refs[:4 * n], refs[4 * n:]
        for i in range(n):
            w_ref, g_ref, m_ref, v_ref = ins[4 * i:4 * i + 4]
            d, mn, vn = _adamw_math(w_ref[...], g_ref[...], m_ref[...], v_ref[...])
            outs[3 * i][...] = d
            outs[3 * i + 1][...] = mn
            outs[3 * i + 2][...] = vn

    flat = [a for it in items for a in it]
    outs = pl.pallas_call(
        body, name="adamw_small",
        out_shape=tuple(jax.ShapeDtypeStruct(it[0].shape, F32) for it in items for _ in range(3)),
        in_specs=[_vmem()] * (4 * n),
        out_specs=tuple(_vmem() for _ in range(3 * n)),
        compiler_params=_params(),
    )(*flat)
    return [tuple(outs[3 * i:3 * i + 3]) for i in range(n)]


def kernel(x, c, w_ada, b_ada, norm_g, w_in, conv_w, conv_b, conv_ln_g, conv_ln_b, sg_ln_g, sg_ln_b, w_s, b_s, w_out, final_g, loss_target, m_w_ada, m_b_ada, m_norm_g, m_w_in, m_conv_w, m_conv_b, m_conv_ln_g, m_conv_ln_b, m_sg_ln_g, m_sg_ln_b, m_w_s, m_b_s, m_w_out, m_final_g, v_w_ada, v_b_ada, v_norm_g, v_w_in, v_conv_w, v_conv_b, v_conv_ln_g, v_conv_ln_b, v_sg_ln_g, v_sg_ln_b, v_w_s, v_b_s, v_w_out, v_final_g):
    s_len = x.shape[1]
    x2d = x[0]
    tgt = loss_target[0]
    row = lambda a: a.reshape(1, -1)

    cw_sh = jnp.pad(conv_w.reshape(CONV_WIDTH, 256), ((0, 1), (0, 0)))
    w_in12, w_out4, cw4 = _gather_weights(w_in[0], w_out[0], cw_sh)
    w12 = w_in12.reshape(12, D_MODEL, 512)
    w_out_full = w_out4.reshape(2 * D_MODEL, D_MODEL)
    mod, cact = _mod(c, w_ada[0], b_ada)

    h, z6 = _fwd_in(x2d, mod, norm_g, w12)
    q = _conv_fwd(z6, cw4, conv_b)
    ln_rows = (conv_ln_g, conv_ln_b, sg_ln_g, sg_ln_b, row(final_g))
    bs_exp = jnp.repeat(b_s[0].T, CHUNK, axis=1)
    dz6, dq, dx2, ycat, dy, dws, sm1, dbs = _middle(x2d, z6, q, tgt, ln_rows, mod, w_s[0], bs_exp, w_out_full)
    kidx = (2 * lax.axis_index("x") + lax.axis_index("y")).astype(jnp.int32).reshape(1)
    g_w_out4 = _grad_w_out(kidx, ycat, dy)
    dz6, dcw = _conv_bwd(dq, z6, cw4, dz6)
    g_w_in4 = _grad_w_in(kidx, h, dz6)
    grad_x, sm2 = _bwd_in(dz6, w12, x2d, dx2, mod, norm_g)

    pack, dws_r, dbs_r, _, dcw_sh, dcb, loss_t = _reduce_small(sm1, sm2, dws.reshape(D_MODEL, CHUNK), dbs, dcw)

    g_w_in, d_w_in, nm_w_in, nv_w_in = _adamw_blocked(w_in[0], m_w_in[0], v_w_in[0], g_w_in4, "adamw_w_in")
    g_w_out, d_w_out, nm_w_out, nv_w_out = _adamw_blocked(w_out[0], m_w_out[0], v_w_out[0], g_w_out4, "adamw_w_out")
    g_w_ada, d_w_ada, nm_w_ada, nv_w_ada, g_b_ada = _adamw_w_ada(cact, pack, w_ada[0], m_w_ada[0], v_w_ada[0])

    g_norm_g, g_cln_g, g_cln_b, g_sln_g, g_sln_b, g_final = (pack[24 + i:25 + i] for i in range(6))
    loss = loss_t[0, 0]
    g_conv_w = dcw_sh[:CONV_WIDTH]
    g_w_s = dws_r
    g_b_s = dbs_r[:, :HEADS].T
    small = [
        (b_ada, g_b_ada, m_b_ada, v_b_ada),
        (norm_g, g_norm_g, m_norm_g, v_norm_g),
        (conv_w.reshape(CONV_WIDTH, 256), g_conv_w, m_conv_w.reshape(CONV_WIDTH, 256), v_conv_w.reshape(CONV_WIDTH, 256)),
        (conv_b, dcb, m_conv_b, v_conv_b),
        (conv_ln_g, g_cln_g, m_conv_ln_g, v_conv_ln_g),
        (conv_ln_b, g_cln_b, m_conv_ln_b, v_conv_ln_b),
        (sg_ln_g, g_sln_g, m_sg_ln_g, v_sg_ln_g),
        (sg_ln_b, g_sln_b, m_sg_ln_b, v_sg_ln_b),
        (w_s.reshape(D_MODEL, CHUNK), g_w_s, m_w_s.reshape(D_MODEL, CHUNK), v_w_s.reshape(D_MODEL, CHUNK)),
        (b_s[0], g_b_s, m_b_s[0], v_b_s[0]),
        (row(final_g), g_final, row(m_final_g), row(v_final_g)),
    ]
    upd = _adamw_small(small)

    shapes = [w_ada.shape, b_ada.shape, norm_g.shape, w_in.shape, conv_w.shape, conv_b.shape, conv_ln_g.shape,
              conv_ln_b.shape, sg_ln_g.shape, sg_ln_b.shape, w_s.shape, b_s.shape, w_out.shape, final_g.shape]
    grads = [g_w_ada, g_b_ada, g_norm_g, g_w_in, g_conv_w, dcb, g_cln_g, g_cln_b, g_sln_g, g_sln_b, g_w_s, g_b_s,
             g_w_out, g_final]
    big = {0: (d_w_ada, nm_w_ada, nv_w_ada), 3: (d_w_in, nm_w_in, nv_w_in), 12: (d_w_out, nm_w_out, nv_w_out)}
    small_pos = [1, 2, 4, 5, 6, 7, 8, 9, 10, 11, 13]
    trip = [None] * 14
    for i, t in big.items():
        trip[i] = t
    for i, t in zip(small_pos, upd):
        trip[i] = t
    fit = lambda arrs: [a.reshape(s) for a, s in zip(arrs, shapes)]
    return (loss, grad_x.reshape(x.shape), *fit(grads), *fit([t[0] for t in trip]), *fit([t[1] for t in trip]),
            *fit([t[2] for t in trip]))
```

```python
import functools

import jax
import jax.numpy as jnp
from jax import lax
from jax.experimental import pallas as pl
from jax.experimental.pallas import tpu as pltpu

F32 = jnp.float32
BF16 = jnp.bfloat16
MESH = pl.DeviceIdType.MESH

D_MODEL = 1024
N_CHIPS = 4
HEADS = 8
CHUNK = 128
CONV_WIDTH = 31
CONV_HALF = CONV_WIDTH // 2
CONV_PAD = 16
EPS = 1e-6
ADAM_LR = 0.001
ADAM_B1 = 0.9
ADAM_B2 = 0.999
ADAM_EPS = 1e-08
ADAM_WD = 0.01
ADAM_STEP = 10

V7X_VMEM_BYTES = 64 * 1024 * 1024
VMEM_LIMIT = V7X_VMEM_BYTES - 8 * 1024 * 1024
ROWS = 16
UNROLL = 8
TOKEN_TILE = 256
TIME_TILE = 128
K_TILE = 2048

N_GROUPS = 6


def _natural_group(j):
    return (j + 2) % N_GROUPS


def _pos():
    return lax.axis_index("x"), lax.axis_index("y"), lax.axis_index("c")


def _rcopy(src, dst, ssem, rsem, dev):
    return pltpu.make_async_remote_copy(src_ref=src, dst_ref=dst, send_sem=ssem, recv_sem=rsem,
                                        device_id=dev, device_id_type=MESH)


def _vmem():
    return pl.BlockSpec(memory_space=pltpu.VMEM)


def _params(**kw):
    return pltpu.CompilerParams(vmem_limit_bytes=VMEM_LIMIT, **kw)


def _sigmoid(v):
    return jax.nn.sigmoid(v)


def _row_loop(n_rows, body, unroll=1):
    def step(r, carry):
        body(pl.ds(pl.multiple_of(r * ROWS, ROWS), ROWS))
        return carry
    lax.fori_loop(0, n_rows // ROWS, step, 0, unroll=unroll)


def _colsum8(v):
    return v.reshape(v.shape[0] // 8, 8, v.shape[1]).sum(axis=0)


def _mean(v):
    return jnp.mean(v, axis=-1, keepdims=True)


def _dot_nn(a, b):
    return jnp.dot(a, b, preferred_element_type=F32)


def _dot_nt(a, b):
    return lax.dot_general(a, b, (((1,), (1,)), ((), ())), preferred_element_type=F32)


def _dot_tn(a, b):
    return lax.dot_general(a, b, (((0,), (0,)), ((), ())), preferred_element_type=F32)


def _remote_chip(k, r):
    return jnp.bitwise_xor(k, r + 1)


def _butterfly(bufs, recvs, ssem, rsem, partners, first_sem=0):
    s = first_sem
    for step, dev in enumerate(partners):
        cps = []
        for buf, recv in zip(bufs, recvs):
            cp = _rcopy(buf, recv.at[step], ssem.at[s], rsem.at[s], dev)
            cp.start()
            cps.append(cp)
            s += 1
        for cp in cps:
            cp.wait()
        for buf, recv in zip(bufs, recvs):
            buf[...] = buf[...] + recv[step]
    return s


def _front(kidx, x, c, w_ada, b_ada, norm_g, w_in, cw):
    s_len = x.shape[0]
    tmh = min(512, s_len)
    tmz = min(1024, s_len)
    nh = s_len // tmh
    nz = s_len // tmz
    n_steps = nh + 12 * nz

    def body(k_ref, x_ref, c_ref, bada_ref, g_ref, cw_ref, wada_hbm, win_hbm,
             h_ref, z_ref, mod_ref, cact_ref, cw4_ref, w12_hbm,
             h_all, wbuf, stage, wada_v, cbuf, crecv, mbuf, mrecv, cw4_s,
             lsem, bsend, brecv, isend, irecv, fsend, frecv, csend, crecvs, osem):
        del k_ref
        t = pl.program_id(0)
        x_, y_, c_ = _pos()
        k = 2 * x_ + y_
        b = 4 * x_ + 2 * y_ + c_
        sib = (x_, y_, 1 - c_)

        def dev_of(r):
            kk = _remote_chip(k, r)
            return (kk // 2, kk % 2, c_)

        def slab(kk, hf):
            return wbuf.at[pl.ds(3 * kk, 3), hf]

        def ici(r, kk):
            return _rcopy(slab(kk, c_), slab(kk, c_), isend.at[r], irecv.at[r], dev_of(r))

        def fwd(r, hf):
            kk = _remote_chip(k, r)
            return _rcopy(slab(kk, hf), slab(kk, hf), fsend.at[r], frecv.at[r], sib)

        def cw_copy(r, kk):
            return _rcopy(cw4_s.at[kk], cw4_s.at[kk], csend.at[r], crecvs.at[r], dev_of(r))

        def to_hbm(s):
            kk = jnp.bitwise_xor(k, s)
            return pltpu.make_async_copy(wbuf.at[pl.ds(3 * kk, 3)], w12_hbm.at[pl.ds(3 * kk, 3)], osem.at[s])

        @pl.when(t == 0)
        def _():
            ld_w = pltpu.make_async_copy(win_hbm, stage, lsem.at[0])
            ld_w.start()
            ld_a = pltpu.make_async_copy(wada_hbm, wada_v, lsem.at[1])
            ld_a.start()
            cw4_s[k] = cw_ref[...]
            for r in range(3):
                cw_copy(r, k).start()
            ld_w.wait()
            for j in range(3):
                for hf in range(2):
                    wbuf[3 * k + j, hf] = stage[hf * 512:(hf + 1) * 512, j * 512:(j + 1) * 512].astype(BF16)
            for r in range(3):
                ici(r, k).start()
            to_hbm(0).start()
            row = lax.broadcasted_iota(jnp.int32, (8, D_MODEL), 0)
            cbuf[...] = jnp.where(row == b, jnp.broadcast_to(c_ref[...], (8, D_MODEL)), 0.0)
            n = _butterfly([cbuf], [crecv], bsend, brecv, [(x_, y_, 1 - c_), (x_, 1 - y_, c_), (1 - x_, y_, c_)])
            call = cbuf[...]
            cact = call * _sigmoid(call)
            cact_ref[...] = cact
            ld_a.wait()
            mbuf[...] = jnp.zeros_like(mbuf)
            mbuf[k] = _dot_nn(cact.astype(BF16), wada_v[...].astype(BF16))
            _butterfly([mbuf], [mrecv], bsend, brecv, [(x_, 1 - y_, c_), (1 - x_, y_, c_)], first_sem=n)
            row8 = lax.broadcasted_iota(jnp.int32, (8, 768), 0)
            for kk in range(N_CHIPS):
                piece = jnp.sum(jnp.where(row8 == b, mbuf[kk], 0.0), axis=0, keepdims=True)
                mod_ref[:, kk * 768:(kk + 1) * 768] = piece + bada_ref[:, kk * 768:(kk + 1) * 768]
            for r in range(3):
                cw_copy(r, _remote_chip(k, r)).wait_recv()
            cw4_ref[...] = cw4_s[...]

        @pl.when(t < nh)
        def _():
            shift = mod_ref[:, 0:D_MODEL]
            scale1 = 1.0 + mod_ref[:, D_MODEL:2 * D_MODEL]
            g = g_ref[...]
            base = t * tmh

            def rows_fn(rows):
                xt = x_ref[rows, :]
                r = lax.rsqrt(_mean(xt * xt) + EPS)
                hv = ((xt * r * g) * scale1 + shift).astype(BF16)
                h_ref[rows, :] = hv
                h_all[pl.ds(pl.multiple_of(base + rows.start, ROWS), ROWS), :] = hv
            _row_loop(tmh, rows_fn, unroll=UNROLL)

        @pl.when(t >= nh)
        def _():
            u = t - nh
            i = u // nz
            rt = u % nz
            for r in range(3):
                @pl.when(jnp.logical_and(rt == 0, i == 3 * r + 2))
                def _():
                    ici(r, _remote_chip(k, r)).wait_recv()
                    fwd(r, c_).start()

                @pl.when(jnp.logical_and(rt == 0, i == 3 * r + 3))
                def _():
                    fwd(r, 1 - c_).wait_recv()
                    to_hbm(r + 1).start()
            m = 3 * jnp.bitwise_xor(k, i // 3) + i % 3
            hb = h_all[pl.ds(pl.multiple_of(rt * tmz, tmz), tmz), :]
            z_ref[0] = _dot_nn(hb[:, 0:512], wbuf[m, 0]) + _dot_nn(hb[:, 512:1024], wbuf[m, 1])

        @pl.when(t == n_steps - 1)
        def _():
            for r in range(3):
                ici(r, k).wait_send()
                fwd(r, c_).wait_send()
                cw_copy(r, k).wait_send()
            for s in range(4):
                to_hbm(s).wait()

    def z_index(t, k_ref):
        u = jnp.maximum(t - nh, 0)
        i = u // nz
        m = 3 * jnp.bitwise_xor(k_ref[0], i // 3) + i % 3
        return ((m // 2 + 4) % N_GROUPS, u % nz, m % 2)

    tok = lambda t, k_ref: (jnp.minimum(t, nh - 1), 0)
    const2 = lambda t, k_ref: (0, 0)
    grid_spec = pltpu.PrefetchScalarGridSpec(
        num_scalar_prefetch=1,
        grid=(n_steps,),
        in_specs=[pl.BlockSpec((tmh, D_MODEL), tok),
                  pl.BlockSpec((1, D_MODEL), const2),
                  pl.BlockSpec((1, 3 * D_MODEL), const2),
                  pl.BlockSpec((1, D_MODEL), const2),
                  pl.BlockSpec((32, 256), const2),
                  pl.BlockSpec(memory_space=pl.ANY),
                  pl.BlockSpec(memory_space=pl.ANY)],
        out_specs=(pl.BlockSpec((tmh, D_MODEL), tok),
                   pl.BlockSpec((1, tmz, 512), z_index),
                   pl.BlockSpec((1, 3 * D_MODEL), const2),
                   pl.BlockSpec((8, D_MODEL), const2),
                   pl.BlockSpec((N_CHIPS, 32, 256), lambda t, k_ref: (0, 0, 0)),
                   pl.BlockSpec(memory_space=pl.ANY)),
        scratch_shapes=[pltpu.VMEM((s_len, D_MODEL), BF16),
                        pltpu.VMEM((12, 2, 512, 512), BF16),
                        pltpu.VMEM((D_MODEL, 1536), F32),
                        pltpu.VMEM((D_MODEL, 768), F32),
                        pltpu.VMEM((8, D_MODEL), F32), pltpu.VMEM((3, 8, D_MODEL), F32),
                        pltpu.VMEM((N_CHIPS, 8, 768), F32), pltpu.VMEM((2, N_CHIPS, 8, 768), F32),
                        pltpu.VMEM((N_CHIPS, 32, 256), F32),
                        pltpu.SemaphoreType.DMA((2,)),
                        pltpu.SemaphoreType.DMA((5,)), pltpu.SemaphoreType.DMA((5,)),
                        pltpu.SemaphoreType.DMA((3,)), pltpu.SemaphoreType.DMA((3,)),
                        pltpu.SemaphoreType.DMA((3,)), pltpu.SemaphoreType.DMA((3,)),
                        pltpu.SemaphoreType.DMA((3,)), pltpu.SemaphoreType.DMA((3,)),
                        pltpu.SemaphoreType.DMA((4,))])
    return pl.pallas_call(
        body, name="front",
        grid_spec=grid_spec,
        out_shape=(jax.ShapeDtypeStruct((s_len, D_MODEL), BF16),
                   jax.ShapeDtypeStruct((N_GROUPS, s_len, D_MODEL), F32),
                   jax.ShapeDtypeStruct((1, 3 * D_MODEL), F32),
                   jax.ShapeDtypeStruct((8, D_MODEL), F32),
                   jax.ShapeDtypeStruct((N_CHIPS, 32, 256), F32),
                   jax.ShapeDtypeStruct((12, 2, 512, 512), BF16)),
        compiler_params=_params(dimension_semantics=("arbitrary",)),
    )(kidx, x, c, b_ada, norm_g, cw, w_ada, w_in)


def _reduce_small(sm1, sm2, dws, dbs, dcw):
    def body(sm1_ref, sm2_ref, dws_in, dbs_in, dcw_in, pack_ref, dws_ref, dbs_ref, dcw_ref, dcwsh_ref, dcb_ref, loss_ref,
             r_pack, r_dws, r_dbs, r_dcw, ssem, rsem):
        x, y, c = _pos()
        k = 2 * x + y
        b = 4 * x + 2 * y + c
        row = lax.broadcasted_iota(jnp.int32, (8, D_MODEL), 0)

        def onehot(v):
            return jnp.where(row == b, jnp.broadcast_to(v, (8, D_MODEL)), 0.0)

        pack_ref[0:8, :] = onehot(sm2_ref[0:1, :])
        pack_ref[8:16, :] = onehot(sm2_ref[1:2, :])
        pack_ref[16:24, :] = onehot(sm1_ref[1:2, :])
        order = [sm2_ref[2:3, :], sm1_ref[2:3, :], sm1_ref[3:4, :], sm1_ref[4:5, :], sm1_ref[5:6, :],
                 sm1_ref[0:1, :], sm1_ref[6:7, :], jnp.zeros((1, D_MODEL), F32)]
        for i, v in enumerate(order):
            pack_ref[24 + i:25 + i, :] = v
        dws_ref[...] = dws_in[...]
        dbs_ref[...] = dbs_in[...]
        dcw_ref[...] = dcw_in[...]
        _butterfly([pack_ref, dws_ref, dbs_ref, dcw_ref], [r_pack, r_dws, r_dbs, r_dcw], ssem, rsem,
                   [(x, y, 1 - c), (x, 1 - y, c), (1 - x, y, c)])
        sel = jnp.zeros((32, 256), F32)
        for kk in range(N_CHIPS):
            sel = jnp.where(k == kk, dcw_ref[:, kk * 256:(kk + 1) * 256], sel)
        dcwsh_ref[...] = sel
        dcb_ref[...] = dcw_ref[31:32, :]
        loss_ref[...] = jnp.broadcast_to(jnp.sum(pack_ref[30:31, :], axis=1, keepdims=True), (8, 128))

    shp = [(32, D_MODEL), (D_MODEL, 128), (128, 128), (32, D_MODEL)]
    return pl.pallas_call(
        body, name="reduce_small",
        out_shape=tuple(jax.ShapeDtypeStruct(s, F32) for s in shp) + (
            jax.ShapeDtypeStruct((32, 256), F32), jax.ShapeDtypeStruct((1, D_MODEL), F32),
            jax.ShapeDtypeStruct((8, 128), F32)),
        in_specs=[_vmem()] * 5,
        out_specs=tuple(_vmem() for _ in range(7)),
        scratch_shapes=[pltpu.VMEM((3,) + s, F32) for s in shp] + [
            pltpu.SemaphoreType.DMA((12,)), pltpu.SemaphoreType.DMA((12,))],
        compiler_params=_params(),
    )(sm1, sm2, dws, dbs, dcw)


def _shifted_windows(win, n_out):
    n = win.shape[0]
    for s in range(8):
        ws = win if s == 0 else pltpu.roll(win, n - s, axis=0)
        for a in range(4):
            o = 8 * a + s
            if 1 <= o <= CONV_WIDTH:
                yield o, ws[8 * a:8 * a + n_out, :]


def _conv_fwd(z6, cw4, conv_b, w_out):
    s_len = z6.shape[1]
    tt = TIME_TILE
    n_blocks = D_MODEL // 128

    def body(z_ref, cw_ref, cb_ref, wout_hbm, q_ref, wout4_hbm, ppad, stage, wbuf, lsem, isend, irecv, fsend, frecv, osem):
        jb = pl.program_id(0)
        x_, y_, c_ = _pos()
        k = 2 * x_ + y_
        sib = (x_, y_, 1 - c_)

        def ici(r, kk):
            rk = _remote_chip(k, r)
            return _rcopy(wbuf.at[kk, c_], wbuf.at[kk, c_], isend.at[r], irecv.at[r], (rk // 2, rk % 2, c_))

        def fwd(r, hf):
            kk = _remote_chip(k, r)
            return _rcopy(wbuf.at[kk, hf], wbuf.at[kk, hf], fsend.at[r], frecv.at[r], sib)

        @pl.when(jb == 0)
        def _():
            ld = pltpu.make_async_copy(wout_hbm, stage, lsem)
            ld.start()
            ld.wait()
            for hf in range(2):
                wbuf[k, hf] = stage[hf * 256:(hf + 1) * 256, :].astype(BF16)
            for r in range(3):
                ici(r, k).start()

        @pl.when(jb == n_blocks // 2)
        def _():
            for r in range(3):
                ici(r, _remote_chip(k, r)).wait_recv()
                fwd(r, c_).start()

        zero = jnp.zeros((CONV_PAD, 128), F32)
        ppad[0:CONV_PAD, :] = zero
        ppad[s_len + CONV_PAD:s_len + 2 * CONV_PAD, :] = zero

        def fill(i, carry):
            t0 = pl.multiple_of(i * tt, tt)
            ppad[pl.ds(CONV_PAD + t0, tt), :] = z_ref[0, pl.ds(t0, tt), :] * _sigmoid(z_ref[1, pl.ds(t0, tt), :])
            return carry
        lax.fori_loop(0, s_len // tt, fill, 0)
        w = cw_ref[0]
        bias = cb_ref[...]

        def conv(i, carry):
            t0 = pl.multiple_of(i * tt, tt)
            win = ppad[pl.ds(t0, tt + 2 * CONV_PAD), :]
            acc = jnp.broadcast_to(bias, (tt, 128))
            for o, sl in _shifted_windows(win, tt):
                acc = acc + w[o - 1:o, :] * sl
            q_ref[pl.ds(t0, tt), :] = acc
            return carry
        lax.fori_loop(0, s_len // tt, conv, 0)

        @pl.when(jb == n_blocks - 1)
        def _():
            for r in range(3):
                fwd(r, 1 - c_).wait_recv()
            out = pltpu.make_async_copy(wbuf, wout4_hbm, osem)
            out.start()
            for r in range(3):
                ici(r, k).wait_send()
                fwd(r, c_).wait_send()
            out.wait()

    return pl.pallas_call(
        body, name="conv_fwd",
        grid=(n_blocks,),
        out_shape=(jax.ShapeDtypeStruct((s_len, D_MODEL), F32), jax.ShapeDtypeStruct((N_CHIPS, 2, 256, D_MODEL), BF16)),
        in_specs=[pl.BlockSpec((2, s_len, 128), lambda j: (2, 0, j)),
                  pl.BlockSpec((1, 32, 128), lambda j: (j // 2, 0, j % 2)),
                  pl.BlockSpec((1, 128), lambda j: (0, j)),
                  pl.BlockSpec(memory_space=pl.ANY)],
        out_specs=(pl.BlockSpec((s_len, 128), lambda j: (0, j)), pl.BlockSpec(memory_space=pl.ANY)),
        scratch_shapes=[pltpu.VMEM((s_len + 2 * CONV_PAD, 128), F32),
                        pltpu.VMEM((512, D_MODEL), F32),
                        pltpu.VMEM((N_CHIPS, 2, 256, D_MODEL), BF16),
                        pltpu.SemaphoreType.DMA,
                        pltpu.SemaphoreType.DMA((3,)), pltpu.SemaphoreType.DMA((3,)),
                        pltpu.SemaphoreType.DMA((3,)), pltpu.SemaphoreType.DMA((3,)),
                        pltpu.SemaphoreType.DMA],
        compiler_params=_params(dimension_semantics=("arbitrary",)),
    )(z6, cw4, conv_b, w_out)


def _middle(x, z6, q, target, ln_rows, mod, w_s, bs_exp, w_out):
    s_len = x.shape[0]
    tm = TOKEN_TILE
    n_steps = s_len // tm
    n_chunks = tm // CHUNK
    inv_d = 1.0 / D_MODEL

    def body(x_ref, z_ref, q_ref, tgt_ref, cg_ref, cb_ref, sg_ref, sb_ref, fg_ref, mod_ref, ws_ref, bs_ref, wout_ref,
             dz_ref, dq_ref, dx2_ref, ycat_ref, dy_ref, dws_ref, sm_ref, dbs_ref,
             vl_scr, vm_scr, y_scr, dycat_scr, dvm_scr, dvl_scr, acc_scr, dbs_acc):
        i = pl.program_id(0)

        @pl.when(i == 0)
        def _():
            acc_scr[...] = jnp.zeros_like(acc_scr)
            dbs_acc[...] = jnp.zeros_like(dbs_acc)
            dws_ref[...] = jnp.zeros_like(dws_ref)

        cg, cb, sg, sb, fg = cg_ref[...], cb_ref[...], sg_ref[...], sb_ref[...], fg_ref[...]
        gm = mod_ref[:, 2 * D_MODEL:3 * D_MODEL]

        def norm_stats(t):
            c = t - _mean(t)
            rstd = lax.rsqrt(_mean(c * c) + EPS)
            return c * rstd, rstd

        def phase1(rows):
            qhat, _ = norm_stats(q_ref[rows, :])
            ln = qhat * cg + cb
            gz = z_ref[0, rows, :]
            ycat_ref[rows, 0:D_MODEL] = ((ln * _sigmoid(ln)) * (gz * _sigmoid(gz))).astype(BF16)
            vhat, _ = norm_stats(z_ref[2, rows, :])
            vl_scr[rows, :] = (vhat * sg + sb).astype(BF16)
        _row_loop(tm, phase1, unroll=UNROLL)

        for ch in range(n_chunks):
            r0 = ch * CHUNK
            for h in range(HEADS):
                c0 = h * CHUNK
                vm_scr[r0:r0 + CHUNK, c0:c0 + CHUNK] = (
                    _dot_nn(ws_ref[h].astype(BF16), vl_scr[r0:r0 + CHUNK, c0:c0 + CHUNK]) + bs_ref[:, c0:c0 + CHUNK])

        def phase3(rows):
            bg = z_ref[3, rows, :]
            ycat_ref[rows, D_MODEL:2 * D_MODEL] = (z_ref[1, rows, :] * vm_scr[rows, :] * (bg * _sigmoid(bg))).astype(BF16)
        _row_loop(tm, phase3)

        y_scr[...] = _dot_nn(ycat_ref[...], wout_ref[...])

        def phase5(rows):
            y = y_scr[rows, :]
            x2 = x_ref[rows, :] + gm * y
            r2 = lax.rsqrt(_mean(x2 * x2) + EPS)
            xn2 = x2 * r2
            diff = xn2 * fg - tgt_ref[rows, :]
            acc_scr[6] += _colsum8(diff * diff)
            dout = diff * inv_d
            acc_scr[0] += _colsum8(dout * xn2)
            dxn = dout * fg
            dx2 = r2 * (dxn - xn2 * _mean(dxn * xn2))
            dx2_ref[rows, :] = dx2
            acc_scr[1] += _colsum8(dx2 * y)
            dy_ref[rows, :] = (dx2 * gm).astype(BF16)
        _row_loop(tm, phase5, unroll=UNROLL)

        dycat_scr[...] = _dot_nt(dy_ref[...], wout_ref[...])

        def phase7(rows):
            dyb = dycat_scr[rows, D_MODEL:2 * D_MODEL]
            u = z_ref[1, rows, :]
            bg = z_ref[3, rows, :]
            vm = vm_scr[rows, :]
            sig = _sigmoid(bg)
            silu = bg * sig
            dz_ref[1, rows, :] = (dyb * vm * silu).astype(BF16)
            dvm = dyb * u * silu
            dz_ref[3, rows, :] = (dyb * u * vm * (sig * (1.0 + bg * (1.0 - sig)))).astype(BF16)
            dvm_scr[rows, :] = dvm.astype(BF16)
            pos = pl.ds(pl.multiple_of(rows.start % CHUNK, ROWS), ROWS)
            dbs_acc[pos, :] += dvm
        _row_loop(tm, phase7)

        for ch in range(n_chunks):
            r0 = ch * CHUNK
            for h in range(HEADS):
                c0 = h * CHUNK
                dvm_b = dvm_scr[r0:r0 + CHUNK, c0:c0 + CHUNK]
                dws_ref[h] += _dot_nt(dvm_b, vl_scr[r0:r0 + CHUNK, c0:c0 + CHUNK])
                dvl_scr[r0:r0 + CHUNK, c0:c0 + CHUNK] = _dot_tn(ws_ref[h].astype(BF16), dvm_b)

        def phase9(rows):
            vhat, rstd_v = norm_stats(z_ref[2, rows, :])
            dvl = dvl_scr[rows, :]
            acc_scr[4] += _colsum8(dvl * vhat)
            acc_scr[5] += _colsum8(dvl)
            dvh = dvl * sg
            dz_ref[2, rows, :] = (rstd_v * (dvh - _mean(dvh) - vhat * _mean(dvh * vhat))).astype(BF16)
            qhat, rstd_q = norm_stats(q_ref[rows, :])
            ln = qhat * cg + cb
            sig_ln = _sigmoid(ln)
            gz = z_ref[0, rows, :]
            sig_g = _sigmoid(gz)
            dya = dycat_scr[rows, 0:D_MODEL]
            dz_ref[0, rows, :] = (dya * (ln * sig_ln) * (sig_g * (1.0 + gz * (1.0 - sig_g)))).astype(BF16)
            dln = (dya * (gz * sig_g)) * (sig_ln * (1.0 + ln * (1.0 - sig_ln)))
            acc_scr[2] += _colsum8(dln * qhat)
            acc_scr[3] += _colsum8(dln)
            dqh = dln * cg
            dq_ref[rows, :] = rstd_q * (dqh - _mean(dqh) - qhat * _mean(dqh * qhat))
        _row_loop(tm, phase9, unroll=UNROLL)

        @pl.when(i == n_steps - 1)
        def _():
            for qi in range(8):
                scale = 0.5 * inv_d if qi == 6 else 1.0
                sm_ref[qi:qi + 1, :] = jnp.sum(acc_scr[qi], axis=0, keepdims=True) * scale
            lane = lax.broadcasted_iota(jnp.int32, (CHUNK, CHUNK), 1)
            tile = jnp.zeros((CHUNK, CHUNK), F32)
            for h in range(HEADS):
                col = jnp.sum(dbs_acc[:, h * CHUNK:(h + 1) * CHUNK], axis=1, keepdims=True)
                tile = jnp.where(lane == h, col, tile)
            dbs_ref[...] = tile

    tok = lambda i: (i, 0)
    const2 = lambda i: (0, 0)
    return pl.pallas_call(
        body, name="middle",
        grid=(n_steps,),
        out_shape=(jax.ShapeDtypeStruct((N_GROUPS, s_len, D_MODEL), BF16),
                   jax.ShapeDtypeStruct((s_len, D_MODEL), F32),
                   jax.ShapeDtypeStruct((s_len, D_MODEL), F32),
                   jax.ShapeDtypeStruct((s_len, 2 * D_MODEL), BF16),
                   jax.ShapeDtypeStruct((s_len, D_MODEL), BF16),
                   jax.ShapeDtypeStruct((HEADS, CHUNK, CHUNK), F32),
                   jax.ShapeDtypeStruct((8, D_MODEL), F32),
                   jax.ShapeDtypeStruct((CHUNK, CHUNK), F32)),
        in_specs=[pl.BlockSpec((tm, D_MODEL), tok),
                  pl.BlockSpec((4, tm, D_MODEL), lambda i: (0, i, 0)),
                  pl.BlockSpec((tm, D_MODEL), tok),
                  pl.BlockSpec((tm, D_MODEL), tok),
                  *[pl.BlockSpec((1, D_MODEL), const2) for _ in range(5)],
                  pl.BlockSpec((1, 3 * D_MODEL), const2),
                  pl.BlockSpec((HEADS, CHUNK, CHUNK), lambda i: (0, 0, 0)),
                  pl.BlockSpec((CHUNK, D_MODEL), const2),
                  pl.BlockSpec((2 * D_MODEL, D_MODEL), const2, pipeline_mode=pl.Buffered(1))],
        out_specs=(pl.BlockSpec((4, tm, D_MODEL), lambda i: (0, i, 0)),
                   pl.BlockSpec((tm, D_MODEL), tok),
                   pl.BlockSpec((tm, D_MODEL), tok),
                   pl.BlockSpec((tm, 2 * D_MODEL), tok),
                   pl.BlockSpec((tm, D_MODEL), tok),
                   pl.BlockSpec((HEADS, CHUNK, CHUNK), lambda i: (0, 0, 0)),
                   pl.BlockSpec((8, D_MODEL), const2),
                   pl.BlockSpec((CHUNK, CHUNK), const2)),
        scratch_shapes=[pltpu.VMEM((tm, D_MODEL), BF16),
                        pltpu.VMEM((tm, D_MODEL), F32),
                        pltpu.VMEM((tm, D_MODEL), F32),
                        pltpu.VMEM((tm, 2 * D_MODEL), F32),
                        pltpu.VMEM((tm, D_MODEL), BF16),
                        pltpu.VMEM((tm, D_MODEL), F32),
                        pltpu.VMEM((8, 8, D_MODEL), F32),
                        pltpu.VMEM((CHUNK, D_MODEL), F32)],
        compiler_params=_params(dimension_semantics=("arbitrary",)),
    )(x, z6, q, target, *ln_rows, mod, w_s, bs_exp, w_out)


def _conv_bwd(dq, z6, cw4, dz6):
    s_len = dq.shape[0]
    tt = TIME_TILE

    def body(dq_ref, z_ref, cw_ref, dz_in, dz_ref, dcw_ref, dqpad, ppad, wacc):
        del dz_in
        zero = jnp.zeros((CONV_PAD, 128), F32)
        for pad in (dqpad, ppad):
            pad[0:CONV_PAD, :] = zero
            pad[s_len + CONV_PAD:s_len + 2 * CONV_PAD, :] = zero
        wacc[...] = jnp.zeros_like(wacc)

        def fill(i, carry):
            t0 = pl.multiple_of(i * tt, tt)
            ppad[pl.ds(CONV_PAD + t0, tt), :] = z_ref[0, pl.ds(t0, tt), :] * _sigmoid(z_ref[1, pl.ds(t0, tt), :])
            dqpad[pl.ds(CONV_PAD + t0, tt), :] = dq_ref[pl.ds(t0, tt), :]
            return carry
        lax.fori_loop(0, s_len // tt, fill, 0)
        w = cw_ref[0]

        def bwd(i, carry):
            t0 = pl.multiple_of(i * tt, tt)
            dp = jnp.zeros((tt, 128), F32)
            for o, sl in _shifted_windows(dqpad[pl.ds(t0, tt + 2 * CONV_PAD), :], tt):
                dp = dp + w[CONV_WIDTH - o:CONV_WIDTH - o + 1, :] * sl
            a = z_ref[0, pl.ds(t0, tt), :]
            sig = _sigmoid(z_ref[1, pl.ds(t0, tt), :])
            dz_ref[0, pl.ds(t0, tt), :] = (dp * sig).astype(BF16)
            dz_ref[1, pl.ds(t0, tt), :] = (dp * a * (sig * (1.0 - sig))).astype(BF16)
            dqt = dq_ref[pl.ds(t0, tt), :]
            for o, sl in _shifted_windows(ppad[pl.ds(t0, tt + 2 * CONV_PAD), :], tt):
                wacc[o - 1] += _colsum8(dqt * sl)
            wacc[CONV_WIDTH] += _colsum8(dqt)
            return carry
        lax.fori_loop(0, s_len // tt, bwd, 0)
        for k in range(32):
            dcw_ref[k:k + 1, :] = jnp.sum(wacc[k], axis=0, keepdims=True)

    return pl.pallas_call(
        body, name="conv_bwd",
        grid=(D_MODEL // 128,),
        out_shape=(jax.ShapeDtypeStruct(dz6.shape, BF16), jax.ShapeDtypeStruct((32, D_MODEL), F32)),
        in_specs=[pl.BlockSpec((s_len, 128), lambda j: (0, j)),
                  pl.BlockSpec((2, s_len, 128), lambda j: (2, 0, j)),
                  pl.BlockSpec((1, 32, 128), lambda j: (j // 2, 0, j % 2)),
                  pl.BlockSpec(memory_space=pl.ANY)],
        out_specs=(pl.BlockSpec((2, s_len, 128), lambda j: (2, 0, j)),
                   pl.BlockSpec((32, 128), lambda j: (0, j))),
        scratch_shapes=[pltpu.VMEM((s_len + 2 * CONV_PAD, 128), F32), pltpu.VMEM((s_len + 2 * CONV_PAD, 128), F32),
                        pltpu.VMEM((32, 8, 128), F32)],
        input_output_aliases={3: 0},
        compiler_params=_params(dimension_semantics=("arbitrary",)),
    )(dq, z6, cw4, dz6)


def _bwd_in(dz6, w12, x, dx2, mod, norm_g):
    s_len = x.shape[0]
    tm = TOKEN_TILE
    n_steps = s_len // tm

    def body(dz_ref, w_ref, x_ref, dx2_ref, mod_ref, g_ref, gx_ref, sm_ref, dh_scr, acc_scr):
        i = pl.program_id(0)

        @pl.when(i == 0)
        def _():
            acc_scr[...] = jnp.zeros_like(acc_scr)

        dh = jnp.zeros((tm, D_MODEL), F32)
        for j in range(N_GROUPS):
            n = _natural_group(j)
            for hf in range(2):
                dh = dh + _dot_nt(dz_ref[j, :, hf * 512:(hf + 1) * 512], w_ref[2 * n + hf])
        dh_scr[...] = dh
        scale1 = 1.0 + mod_ref[:, D_MODEL:2 * D_MODEL]
        g = g_ref[...]

        def rows_fn(rows):
            xt = x_ref[rows, :]
            r = lax.rsqrt(_mean(xt * xt) + EPS)
            xn = xt * r
            dhr = dh_scr[rows, :]
            acc_scr[0] += _colsum8(dhr)
            acc_scr[1] += _colsum8(dhr * (xn * g))
            acc_scr[2] += _colsum8(dhr * scale1 * xn)
            dxn = dhr * (g * scale1)
            gx_ref[rows, :] = dx2_ref[rows, :] + r * (dxn - xn * _mean(dxn * xn))
        _row_loop(tm, rows_fn, unroll=UNROLL)

        @pl.when(i == n_steps - 1)
        def _():
            for qi in range(8):
                sm_ref[qi:qi + 1, :] = jnp.sum(acc_scr[qi], axis=0, keepdims=True)

    tok = lambda i: (i, 0)
    const2 = lambda i: (0, 0)
    return pl.pallas_call(
        body, name="bwd_in",
        grid=(n_steps,),
        out_shape=(jax.ShapeDtypeStruct((s_len, D_MODEL), F32), jax.ShapeDtypeStruct((8, D_MODEL), F32)),
        in_specs=[pl.BlockSpec((N_GROUPS, tm, D_MODEL), lambda i: (0, i, 0)),
                  pl.BlockSpec((12, D_MODEL, 512), lambda i: (0, 0, 0), pipeline_mode=pl.Buffered(1)),
                  pl.BlockSpec((tm, D_MODEL), tok),
                  pl.BlockSpec((tm, D_MODEL), tok),
                  pl.BlockSpec((1, 3 * D_MODEL), const2),
                  pl.BlockSpec((1, D_MODEL), const2)],
        out_specs=(pl.BlockSpec((tm, D_MODEL), tok), pl.BlockSpec((8, D_MODEL), const2)),
        scratch_shapes=[pltpu.VMEM((tm, D_MODEL), F32), pltpu.VMEM((8, 8, D_MODEL), F32)],
        compiler_params=_params(dimension_semantics=("arbitrary",)),
    )(dz6, w12, x, dx2, mod, norm_g)


def _grad_reduce_scatter(kidx, a, b, *, name, nj, nr, nc, a_spec, b_spec, b_lead, kt):
    nblk = N_CHIPS * nj
    nfor = 3 * nj

    def body(k_ref, a_ref, b_ref, out_ref, acc, recv_a, send_b, recv_b, own_ps,
             lsem, psend, precv, isend, irecv, fsend, frecv):
        del k_ref
        i = pl.program_id(0)
        kk = pl.program_id(1)
        x, y, c = _pos()
        k = 2 * x + y
        sib = (x, y, 1 - c)
        slot = i % 2

        @pl.when(kk == 0)
        def _():
            acc[slot] = jnp.zeros((2, nr, nc), F32)

        res = _dot_tn(a_ref[...], b_ref[0] if b_lead else b_ref[...])
        acc[slot, 0] += res[0:nr]
        acc[slot, 1] += res[nr:2 * nr]

        def pair_copy(ii, sl):
            return _rcopy(acc.at[sl, 1 - c], recv_a.at[ii], psend.at[ii], precv.at[ii], sib)

        def pair_sum(ii, sl):
            cp = pair_copy(ii, sl)
            cp.wait_recv()
            cp.wait_send()
            return acc[sl, c] + recv_a[ii]

        def ici_copy(ii, kd, rslot):
            return _rcopy(send_b.at[ii], recv_b.at[rslot], isend.at[ii], irecv.at[rslot], (kd // 2, kd % 2, c))

        def finalize(ii, sl):
            ps = pair_sum(ii, sl)
            s = ii // nj
            j = ii % nj

            @pl.when(s < 3)
            def _():
                send_b[ii] = ps.astype(BF16)
                ici_copy(ii, (k + 1 + s) % N_CHIPS, (2 - s) * nj + j).start()

            @pl.when(s == 3)
            def _():
                own_ps[j] = ps

        @pl.when(kk == kt - 1)
        def _():
            pair_copy(i, slot).start()

            @pl.when(i >= 1)
            def _():
                finalize(i - 1, 1 - slot)

            @pl.when(i == nblk - 1)
            def _():
                own_ps[nj - 1] = pair_sum(nblk - 1, (nblk - 1) % 2)
                for r in range(nfor):
                    ici_copy(0, k, r).wait_recv()
                for j in range(nj):
                    tot = own_ps[j]
                    for s in range(3):
                        tot = tot + recv_b[s * nj + j].astype(F32)
                    own_ps[j] = tot
                loc = pltpu.make_async_copy(own_ps, out_ref.at[:, c], lsem)
                loc.start()
                swap = _rcopy(own_ps, out_ref.at[:, c], fsend, frecv, sib)
                swap.start()
                loc.wait()
                swap.wait_send()
                _rcopy(own_ps, out_ref.at[:, 1 - c], fsend, frecv, sib).wait_recv()
                for ii in range(nfor):
                    ici_copy(ii, k, 0).wait_send()

    def seq(i, k_ref):
        return (nj * (k_ref[0] + 1) + i) % nblk

    grid_spec = pltpu.PrefetchScalarGridSpec(
        num_scalar_prefetch=1,
        grid=(nblk, kt),
        in_specs=[a_spec(seq), b_spec(seq)],
        out_specs=pl.BlockSpec(memory_space=pl.ANY),
        scratch_shapes=[pltpu.VMEM((2, 2, nr, nc), F32),
                        pltpu.VMEM((nblk, nr, nc), F32),
                        pltpu.VMEM((nfor, nr, nc), BF16),
                        pltpu.VMEM((nfor, nr, nc), BF16),
                        pltpu.VMEM((nj, nr, nc), F32),
                        pltpu.SemaphoreType.DMA,
                        pltpu.SemaphoreType.DMA((nblk,)), pltpu.SemaphoreType.DMA((nblk,)),
                        pltpu.SemaphoreType.DMA((nfor,)), pltpu.SemaphoreType.DMA((nfor,)),
                        pltpu.SemaphoreType.DMA, pltpu.SemaphoreType.DMA])
    return pl.pallas_call(
        body, name=name,
        grid_spec=grid_spec,
        out_shape=jax.ShapeDtypeStruct((nj, 2, nr, nc), F32),
        compiler_params=_params(dimension_semantics=("arbitrary", "arbitrary")),
    )(kidx, a, b)


def _grad_w_in(kidx, h, dz6):
    s_len = h.shape[0]
    tk = min(K_TILE, s_len)
    return _grad_reduce_scatter(
        kidx, h, dz6, name="grad_w_in", nj=3, nr=512, nc=512, b_lead=True, kt=s_len // tk,
        a_spec=lambda seq: pl.BlockSpec((tk, D_MODEL), lambda i, kk, k_ref: (kk, 0)),
        b_spec=lambda seq: pl.BlockSpec(
            (1, tk, 512), lambda i, kk, k_ref: ((seq(i, k_ref) // 2 + 4) % N_GROUPS, kk, seq(i, k_ref) % 2)))


def _grad_w_out(kidx, ycat, dy):
    s_len = dy.shape[0]
    tk = min(K_TILE, s_len)
    return _grad_reduce_scatter(
        kidx, ycat, dy, name="grad_w_out", nj=1, nr=256, nc=D_MODEL, b_lead=False, kt=s_len // tk,
        a_spec=lambda seq: pl.BlockSpec((tk, 512), lambda i, kk, k_ref: (kk, seq(i, k_ref))),
        b_spec=lambda seq: pl.BlockSpec((tk, D_MODEL), lambda i, kk, k_ref: (kk, 0)))


def _adamw_math(w, g, m, v):
    m = ADAM_B1 * m + (1.0 - ADAM_B1) * g
    v = ADAM_B2 * v + (1.0 - ADAM_B2) * (g * g)
    m_hat = m / (1.0 - ADAM_B1 ** ADAM_STEP)
    v_hat = v / (1.0 - ADAM_B2 ** ADAM_STEP)
    delta = -ADAM_LR * (m_hat / (jnp.sqrt(v_hat) + ADAM_EPS) + ADAM_WD * w)
    return delta, m, v


def _adamw_blocked(w, m, v, g4, name):
    nj, _, nr, nc = g4.shape

    def body(w_ref, m_ref, v_ref, g_ref, go_ref, d_ref, mo_ref, vo_ref):
        g = g_ref[0, 0]
        d, mn, vn = _adamw_math(w_ref[...], g, m_ref[...], v_ref[...])
        go_ref[...] = g
        d_ref[...] = d
        mo_ref[...] = mn
        vo_ref[...] = vn

    blk = pl.BlockSpec((nr, nc), lambda j, hf: (hf, j))
    return pl.pallas_call(
        body, name=name,
        grid=(nj, 2),
        out_shape=tuple(jax.ShapeDtypeStruct(w.shape, F32) for _ in range(4)),
        in_specs=[blk, blk, blk, pl.BlockSpec((1, 1, nr, nc), lambda j, hf: (j, hf, 0, 0))],
        out_specs=(blk, blk, blk, blk),
        compiler_params=_params(dimension_semantics=("arbitrary", "arbitrary")),
    )(w, m, v, g4)


def _adamw_w_ada(cact, pack, w, m, v):
    def body(cact_ref, pack_ref, w_ref, m_ref, v_ref, g_ref, d_ref, mo_ref, vo_ref, gb_ref, dmod):
        x, y, _ = _pos()
        k = 2 * x + y
        for t in range(3):
            dmod[:, t * D_MODEL:(t + 1) * D_MODEL] = pack_ref[8 * t:8 * t + 8, :]
        gb_ref[...] = jnp.sum(dmod[...], axis=0, keepdims=True)
        sel = jnp.zeros((8, 768), F32)
        for kk in range(N_CHIPS):
            sel = jnp.where(k == kk, dmod[:, kk * 768:(kk + 1) * 768], sel)
        g = _dot_tn(cact_ref[...].astype(BF16), sel.astype(BF16))
        d, mn, vn = _adamw_math(w_ref[...], g, m_ref[...], v_ref[...])
        g_ref[...] = g
        d_ref[...] = d
        mo_ref[...] = mn
        vo_ref[...] = vn

    return pl.pallas_call(
        body, name="adamw_w_ada",
        out_shape=tuple(jax.ShapeDtypeStruct(w.shape, F32) for _ in range(4)) + (
            jax.ShapeDtypeStruct((1, 3 * D_MODEL), F32),),
        in_specs=[_vmem()] * 5,
        out_specs=tuple(_vmem() for _ in range(5)),
        scratch_shapes=[pltpu.VMEM((8, 3 * D_MODEL), F32)],
        compiler_params=_params(),
    )(cact, pack, w, m, v)


def _adamw_small(items):
    n = len(items)

    def body(*refs):
        ins, outs = refs[:4 * n], refs[4 * n:]
        for i in range(n):
            w_ref, g_ref, m_ref, v_ref = ins[4 * i:4 * i + 4]
            d, mn, vn = _adamw_math(w_ref[...], g_ref[...], m_ref[...], v_ref[...])
            outs[3 * i][...] = d
            outs[3 * i + 1][...] = mn
            outs[3 * i + 2][...] = vn

    flat = [a for it in items for a in it]
    outs = pl.pallas_call(
        body, name="adamw_small",
        out_shape=tuple(jax.ShapeDtypeStruct(it[0].shape, F32) for it in items for _ in range(3)),
        in_specs=[_vmem()] * (4 * n),
        out_specs=tuple(_vmem() for _ in range(3 * n)),
        compiler_params=_params(),
    )(*flat)
    return [tuple(outs[3 * i:3 * i + 3]) for i in range(n)]


def kernel(x, c, w_ada, b_ada, norm_g, w_in, conv_w, conv_b, conv_ln_g, conv_ln_b, sg_ln_g, sg_ln_b, w_s, b_s, w_out, final_g, loss_target, m_w_ada, m_b_ada, m_norm_g, m_w_in, m_conv_w, m_conv_b, m_conv_ln_g, m_conv_ln_b, m_sg_ln_g, m_sg_ln_b, m_w_s, m_b_s, m_w_out, m_final_g, v_w_ada, v_b_ada, v_norm_g, v_w_in, v_conv_w, v_conv_b, v_conv_ln_g, v_conv_ln_b, v_sg_ln_g, v_sg_ln_b, v_w_s, v_b_s, v_w_out, v_final_g):
    s_len = x.shape[1]
    x2d = x[0]
    tgt = loss_target[0]
    row = lambda a: a.reshape(1, -1)

    kidx = (2 * lax.axis_index("x") + lax.axis_index("y")).astype(jnp.int32).reshape(1)
    cw_sh = jnp.pad(conv_w.reshape(CONV_WIDTH, 256), ((0, 1), (0, 0)))
    h, z6, mod, cact, cw4, w_in12 = _front(kidx, x2d, c, w_ada[0], b_ada, norm_g, w_in[0], cw_sh)
    w12 = w_in12.reshape(12, D_MODEL, 512)
    q, w_out4 = _conv_fwd(z6, cw4, conv_b, w_out[0])
    w_out_full = w_out4.reshape(2 * D_MODEL, D_MODEL)
    ln_rows = (conv_ln_g, conv_ln_b, sg_ln_g, sg_ln_b, row(final_g))
    bs_exp = jnp.repeat(b_s[0].T, CHUNK, axis=1)
    dz6, dq, dx2, ycat, dy, dws, sm1, dbs = _middle(x2d, z6, q, tgt, ln_rows, mod, w_s[0], bs_exp, w_out_full)
    g_w_out4 = _grad_w_out(kidx, ycat, dy)
    dz6, dcw = _conv_bwd(dq, z6, cw4, dz6)
    g_w_in4 = _grad_w_in(kidx, h, dz6)
    grad_x, sm2 = _bwd_in(dz6, w12, x2d, dx2, mod, norm_g)

    pack, dws_r, dbs_r, _, dcw_sh, dcb, loss_t = _reduce_small(sm1, sm2, dws.reshape(D_MODEL, CHUNK), dbs, dcw)

    g_w_in, d_w_in, nm_w_in, nv_w_in = _adamw_blocked(w_in[0], m_w_in[0], v_w_in[0], g_w_in4, "adamw_w_in")
    g_w_out, d_w_out, nm_w_out, nv_w_out = _adamw_blocked(w_out[0], m_w_out[0], v_w_out[0], g_w_out4, "adamw_w_out")
    g_w_ada, d_w_ada, nm_w_ada, nv_w_ada, g_b_ada = _adamw_w_ada(cact, pack, w_ada[0], m_w_ada[0], v_w_ada[0])

    g_norm_g, g_cln_g, g_cln_b, g_sln_g, g_sln_b, g_final = (pack[24 + i:25 + i] for i in range(6))
    loss = loss_t[0, 0]
    g_conv_w = dcw_sh[:CONV_WIDTH]
    g_w_s = dws_r
    g_b_s = dbs_r[:, :HEADS].T
    small = [
        (b_ada, g_b_ada, m_b_ada, v_b_ada),
        (norm_g, g_norm_g, m_norm_g, v_norm_g),
        (conv_w.reshape(CONV_WIDTH, 256), g_conv_w, m_conv_w.reshape(CONV_WIDTH, 256), v_conv_w.reshape(CONV_WIDTH, 256)),
        (conv_b, dcb, m_conv_b, v_conv_b),
        (conv_ln_g, g_cln_g, m_conv_ln_g, v_conv_ln_g),
        (conv_ln_b, g_cln_b, m_conv_ln_b, v_conv_ln_b),
        (sg_ln_g, g_sln_g, m_sg_ln_g, v_sg_ln_g),
        (sg_ln_b, g_sln_b, m_sg_ln_b, v_sg_ln_b),
        (w_s.reshape(D_MODEL, CHUNK), g_w_s, m_w_s.reshape(D_MODEL, CHUNK), v_w_s.reshape(D_MODEL, CHUNK)),
        (b_s[0], g_b_s, m_b_s[0], v_b_s[0]),
        (row(final_g), g_final, row(m_final_g), row(v_final_g)),
    ]
    upd = _adamw_small(small)

    shapes = [w_ada.shape, b_ada.shape, norm_g.shape, w_in.shape, conv_w.shape, conv_b.shape, conv_ln_g.shape,
              conv_ln_b.shape, sg_ln_g.shape, sg_ln_b.shape, w_s.shape, b_s.shape, w_out.shape, final_g.shape]
    grads = [g_w_ada, g_b_ada, g_norm_g, g_w_in, g_conv_w, dcb, g_cln_g, g_cln_b, g_sln_g, g_sln_b, g_w_s, g_b_s,
             g_w_out, g_final]
    big = {0: (d_w_ada, nm_w_ada, nv_w_ada), 3: (d_w_in, nm_w_in, nv_w_in), 12: (d_w_out, nm_w_out, nv_w_out)}
    small_pos = [1, 2, 4, 5, 6, 7, 8, 9, 10, 11, 13]
    trip = [None] * 14
    for i, t in big.items():
        trip[i] = t
    for i, t in zip(small_pos, upd):
        trip[i] = t
    fit = lambda arrs: [a.reshape(s) for a, s in zip(arrs, shapes)]
    return (loss, grad_x.reshape(x.shape), *fit(grads), *fit([t[0] for t in trip]), *fit([t[1] for t in trip]),
            *fit([t[2] for t in trip]))
```

```python
import functools

import jax
import jax.numpy as jnp
from jax import lax
from jax.experimental import pallas as pl
from jax.experimental.pallas import tpu as pltpu

F32 = jnp.float32
BF16 = jnp.bfloat16
MESH = pl.DeviceIdType.MESH

D_MODEL = 1024
N_CHIPS = 4
HEADS = 8
CHUNK = 128
CONV_WIDTH = 31
CONV_HALF = CONV_WIDTH // 2
CONV_PAD = 16
EPS = 1e-6
ADAM_LR = 0.001
ADAM_B1 = 0.9
ADAM_B2 = 0.999
ADAM_EPS = 1e-08
ADAM_WD = 0.01
ADAM_STEP = 10

V7X_VMEM_BYTES = 64 * 1024 * 1024
VMEM_LIMIT = V7X_VMEM_BYTES - 8 * 1024 * 1024
ROWS = 16
UNROLL = 8
TOKEN_TILE = 256
TIME_TILE = 128
K_TILE = 2048

N_GROUPS = 6


def _natural_group(j):
    return (j + 2) % N_GROUPS


def _pos():
    return lax.axis_index("x"), lax.axis_index("y"), lax.axis_index("c")


def _rcopy(src, dst, ssem, rsem, dev):
    return pltpu.make_async_remote_copy(src_ref=src, dst_ref=dst, send_sem=ssem, recv_sem=rsem,
                                        device_id=dev, device_id_type=MESH)


def _vmem():
    return pl.BlockSpec(memory_space=pltpu.VMEM)


def _params(**kw):
    return pltpu.CompilerParams(vmem_limit_bytes=VMEM_LIMIT, **kw)


def _sigmoid(v):
    return jax.nn.sigmoid(v)


def _row_loop(n_rows, body, unroll=1):
    def step(r, carry):
        body(pl.ds(pl.multiple_of(r * ROWS, ROWS), ROWS))
        return carry
    lax.fori_loop(0, n_rows // ROWS, step, 0, unroll=unroll)


def _colsum8(v):
    return v.reshape(v.shape[0] // 8, 8, v.shape[1]).sum(axis=0)


def _mean(v):
    return jnp.mean(v, axis=-1, keepdims=True)


def _dot_nn(a, b):
    return jnp.dot(a, b, preferred_element_type=F32)


def _dot_nt(a, b):
    return lax.dot_general(a, b, (((1,), (1,)), ((), ())), preferred_element_type=F32)


def _dot_tn(a, b):
    return lax.dot_general(a, b, (((0,), (0,)), ((), ())), preferred_element_type=F32)


def _remote_chip(k, r):
    return jnp.bitwise_xor(k, r + 1)


def _butterfly(bufs, recvs, ssem, rsem, partners, first_sem=0):
    s = first_sem
    for step, dev in enumerate(partners):
        cps = []
        for buf, recv in zip(bufs, recvs):
            cp = _rcopy(buf, recv.at[step], ssem.at[s], rsem.at[s], dev)
            cp.start()
            cps.append(cp)
            s += 1
        for cp in cps:
            cp.wait()
        for buf, recv in zip(bufs, recvs):
            buf[...] = buf[...] + recv[step]
    return s


def _front(kidx, x, c, w_ada, b_ada, norm_g, w_in, cw):
    s_len = x.shape[0]
    tmh = min(512, s_len)
    tmz = min(1024, s_len)
    nh = s_len // tmh
    nz = s_len // tmz
    n_steps = nh + 12 * nz

    def remote_block(q):
        return jnp.where(q < 6, q % 2, 2), jnp.where(q < 6, q // 2, q - 6)

    def block_of(i, k):
        r, j = remote_block(jnp.maximum(i - 3, 0))
        return jnp.where(i < 3, 3 * k + i, 3 * _remote_chip(k, r) + j)

    def body(k_ref, x_ref, c_ref, bada_ref, g_ref, cw_ref, wada_hbm, win_hbm,
             h_ref, z_ref, mod_ref, cact_ref, cw4_ref, w12_hbm,
             h_all, wbuf, stage, wada_v, cslab, mslab, cw4_s,
             lsem, csend, crecv, msend, mrecv, wsend, wrecv, isend, irecv, fsend, frecv, osem):
        del k_ref
        t = pl.program_id(0)
        x_, y_, c_ = _pos()
        k = 2 * x_ + y_
        b = 4 * x_ + 2 * y_ + c_
        sib = (x_, y_, 1 - c_)

        def dev_of(r):
            kk = _remote_chip(k, r)
            return (kk // 2, kk % 2, c_)

        def ici(q, kk):
            r, j = remote_block(q)
            return _rcopy(wbuf.at[3 * kk + j, c_], wbuf.at[3 * kk + j, c_], isend.at[q], irecv.at[q], dev_of(r))

        def fwd(q, hf):
            r, j = remote_block(q)
            blk = 3 * _remote_chip(k, r) + j
            return _rcopy(wbuf.at[blk, hf], wbuf.at[blk, hf], fsend.at[q], frecv.at[q], sib)

        def c_copy(q, src):
            d = jnp.bitwise_xor(b, q)
            return _rcopy(cslab.at[src], cslab.at[src], csend.at[q - 1], crecv.at[q - 1], (d // 4, (d // 2) % 2, d % 2))

        def m_copy(r, kk):
            return _rcopy(mslab.at[kk], mslab.at[kk], msend.at[r], mrecv.at[r], dev_of(r))

        def cw_copy(r, kk):
            return _rcopy(cw4_s.at[kk], cw4_s.at[kk], wsend.at[r], wrecv.at[r], dev_of(r))

        def to_hbm(i):
            m = block_of(i, k)
            return pltpu.make_async_copy(wbuf.at[m], w12_hbm.at[m], osem.at[i])

        @pl.when(t == 0)
        def _():
            ld_w = pltpu.make_async_copy(win_hbm, stage, lsem.at[0])
            ld_w.start()
            ld_a = pltpu.make_async_copy(wada_hbm, wada_v, lsem.at[1])
            ld_a.start()
            cslab[b] = jnp.broadcast_to(c_ref[...], (8, D_MODEL))
            for q in range(1, 8):
                c_copy(q, b).start()
            cw4_s[k] = cw_ref[...]
            for r in range(3):
                cw_copy(r, k).start()
            ld_w.wait()
            for j in range(3):
                for hf in range(2):
                    wbuf[3 * k + j, hf] = stage[hf * 512:(hf + 1) * 512, j * 512:(j + 1) * 512].astype(BF16)
            for i in range(3):
                to_hbm(i).start()
            for q in range(1, 8):
                c_copy(q, jnp.bitwise_xor(b, q)).wait_recv()
            row = lax.broadcasted_iota(jnp.int32, (8, D_MODEL), 0)
            call = jnp.zeros((8, D_MODEL), F32)
            for d in range(8):
                call = jnp.where(row == d, cslab[d], call)
            cact = call * _sigmoid(call)
            cact_ref[...] = cact
            ld_a.wait()
            mslab[k] = _dot_nn(cact.astype(BF16), wada_v[...].astype(BF16))
            for r in range(3):
                m_copy(r, k).start()
            for q in range(9):
                ici(q, k).start()
            for r in range(3):
                m_copy(r, _remote_chip(k, r)).wait_recv()
            row8 = lax.broadcasted_iota(jnp.int32, (8, 768), 0)
            for kk in range(N_CHIPS):
                piece = jnp.sum(jnp.where(row8 == b, mslab[kk], 0.0), axis=0, keepdims=True)
                mod_ref[:, kk * 768:(kk + 1) * 768] = piece + bada_ref[:, kk * 768:(kk + 1) * 768]
            for r in range(3):
                cw_copy(r, _remote_chip(k, r)).wait_recv()
            cw4_ref[...] = cw4_s[...]

        @pl.when(t < nh)
        def _():
            shift = mod_ref[:, 0:D_MODEL]
            scale1 = 1.0 + mod_ref[:, D_MODEL:2 * D_MODEL]
            g = g_ref[...]
            base = t * tmh

            def rows_fn(rows):
                xt = x_ref[rows, :]
                r = lax.rsqrt(_mean(xt * xt) + EPS)
                hv = ((xt * r * g) * scale1 + shift).astype(BF16)
                h_ref[rows, :] = hv
                h_all[pl.ds(pl.multiple_of(base + rows.start, ROWS), ROWS), :] = hv
            _row_loop(tmh, rows_fn, unroll=UNROLL)

        @pl.when(t >= nh)
        def _():
            u = t - nh
            i = u // nz
            rt = u % nz
            @pl.when(jnp.logical_and(rt == 0, jnp.logical_and(i >= 2, i <= 10)))
            def _():
                q = i - 2
                r, _ = remote_block(q)
                ici(q, _remote_chip(k, r)).wait_recv()
                fwd(q, c_).start()

            @pl.when(jnp.logical_and(rt == 0, i >= 3))
            def _():
                fwd(i - 3, 1 - c_).wait_recv()
                to_hbm(i).start()
            m = block_of(i, k)
            hb = h_all[pl.ds(pl.multiple_of(rt * tmz, tmz), tmz), :]
            z_ref[0] = _dot_nn(hb[:, 0:512], wbuf[m, 0]) + _dot_nn(hb[:, 512:1024], wbuf[m, 1])

        @pl.when(t == n_steps - 1)
        def _():
            for q in range(1, 8):
                c_copy(q, b).wait_send()
            for r in range(3):
                m_copy(r, k).wait_send()
                cw_copy(r, k).wait_send()
            for q in range(9):
                ici(q, k).wait_send()
                fwd(q, c_).wait_send()
            for i in range(12):
                to_hbm(i).wait()

    def z_index(t, k_ref):
        u = jnp.maximum(t - nh, 0)
        m = block_of(u // nz, k_ref[0])
        return ((m // 2 + 4) % N_GROUPS, u % nz, m % 2)

    tok = lambda t, k_ref: (jnp.minimum(t, nh - 1), 0)
    const2 = lambda t, k_ref: (0, 0)
    grid_spec = pltpu.PrefetchScalarGridSpec(
        num_scalar_prefetch=1,
        grid=(n_steps,),
        in_specs=[pl.BlockSpec((tmh, D_MODEL), tok),
                  pl.BlockSpec((1, D_MODEL), const2),
                  pl.BlockSpec((1, 3 * D_MODEL), const2),
                  pl.BlockSpec((1, D_MODEL), const2),
                  pl.BlockSpec((32, 256), const2),
                  pl.BlockSpec(memory_space=pl.ANY),
                  pl.BlockSpec(memory_space=pl.ANY)],
        out_specs=(pl.BlockSpec((tmh, D_MODEL), tok),
                   pl.BlockSpec((1, tmz, 512), z_index),
                   pl.BlockSpec((1, 3 * D_MODEL), const2),
                   pl.BlockSpec((8, D_MODEL), const2),
                   pl.BlockSpec((N_CHIPS, 32, 256), lambda t, k_ref: (0, 0, 0)),
                   pl.BlockSpec(memory_space=pl.ANY)),
        scratch_shapes=[pltpu.VMEM((s_len, D_MODEL), BF16),
                        pltpu.VMEM((12, 2, 512, 512), BF16),
                        pltpu.VMEM((D_MODEL, 1536), F32),
                        pltpu.VMEM((D_MODEL, 768), F32),
                        pltpu.VMEM((8, 8, D_MODEL), F32),
                        pltpu.VMEM((N_CHIPS, 8, 768), F32),
                        pltpu.VMEM((N_CHIPS, 32, 256), F32),
                        pltpu.SemaphoreType.DMA((2,)),
                        pltpu.SemaphoreType.DMA((7,)), pltpu.SemaphoreType.DMA((7,)),
                        pltpu.SemaphoreType.DMA((3,)), pltpu.SemaphoreType.DMA((3,)),
                        pltpu.SemaphoreType.DMA((3,)), pltpu.SemaphoreType.DMA((3,)),
                        pltpu.SemaphoreType.DMA((9,)), pltpu.SemaphoreType.DMA((9,)),
                        pltpu.SemaphoreType.DMA((9,)), pltpu.SemaphoreType.DMA((9,)),
                        pltpu.SemaphoreType.DMA((12,))])
    return pl.pallas_call(
        body, name="front",
        grid_spec=grid_spec,
        out_shape=(jax.ShapeDtypeStruct((s_len, D_MODEL), BF16),
                   jax.ShapeDtypeStruct((N_GROUPS, s_len, D_MODEL), F32),
                   jax.ShapeDtypeStruct((1, 3 * D_MODEL), F32),
                   jax.ShapeDtypeStruct((8, D_MODEL), F32),
                   jax.ShapeDtypeStruct((N_CHIPS, 32, 256), F32),
                   jax.ShapeDtypeStruct((12, 2, 512, 512), BF16)),
        compiler_params=_params(dimension_semantics=("arbitrary",)),
    )(kidx, x, c, b_ada, norm_g, cw, w_ada, w_in)


def _reduce_small(sm1, sm2, dws, dbs, dcw):
    def body(sm1_ref, sm2_ref, dws_in, dbs_in, dcw_in, pack_ref, dws_ref, dbs_ref, dcw_ref, dcwsh_ref, dcb_ref, loss_ref,
             r_pack, r_dws, r_dbs, r_dcw, ssem, rsem):
        x, y, c = _pos()
        k = 2 * x + y
        b = 4 * x + 2 * y + c
        row = lax.broadcasted_iota(jnp.int32, (8, D_MODEL), 0)

        def onehot(v):
            return jnp.where(row == b, jnp.broadcast_to(v, (8, D_MODEL)), 0.0)

        pack_ref[0:8, :] = onehot(sm2_ref[0:1, :])
        pack_ref[8:16, :] = onehot(sm2_ref[1:2, :])
        pack_ref[16:24, :] = onehot(sm1_ref[1:2, :])
        order = [sm2_ref[2:3, :], sm1_ref[2:3, :], sm1_ref[3:4, :], sm1_ref[4:5, :], sm1_ref[5:6, :],
                 sm1_ref[0:1, :], sm1_ref[6:7, :], jnp.zeros((1, D_MODEL), F32)]
        for i, v in enumerate(order):
            pack_ref[24 + i:25 + i, :] = v
        dws_ref[...] = dws_in[...]
        dbs_ref[...] = dbs_in[...]
        dcw_ref[...] = dcw_in[...]
        _butterfly([pack_ref, dws_ref, dbs_ref, dcw_ref], [r_pack, r_dws, r_dbs, r_dcw], ssem, rsem,
                   [(x, y, 1 - c), (x, 1 - y, c), (1 - x, y, c)])
        sel = jnp.zeros((32, 256), F32)
        for kk in range(N_CHIPS):
            sel = jnp.where(k == kk, dcw_ref[:, kk * 256:(kk + 1) * 256], sel)
        dcwsh_ref[...] = sel
        dcb_ref[...] = dcw_ref[31:32, :]
        loss_ref[...] = jnp.broadcast_to(jnp.sum(pack_ref[30:31, :], axis=1, keepdims=True), (8, 128))

    shp = [(32, D_MODEL), (D_MODEL, 128), (128, 128), (32, D_MODEL)]
    return pl.pallas_call(
        body, name="reduce_small",
        out_shape=tuple(jax.ShapeDtypeStruct(s, F32) for s in shp) + (
            jax.ShapeDtypeStruct((32, 256), F32), jax.ShapeDtypeStruct((1, D_MODEL), F32),
            jax.ShapeDtypeStruct((8, 128), F32)),
        in_specs=[_vmem()] * 5,
        out_specs=tuple(_vmem() for _ in range(7)),
        scratch_shapes=[pltpu.VMEM((3,) + s, F32) for s in shp] + [
            pltpu.SemaphoreType.DMA((12,)), pltpu.SemaphoreType.DMA((12,))],
        compiler_params=_params(),
    )(sm1, sm2, dws, dbs, dcw)


def _shifted_windows(win, n_out):
    n = win.shape[0]
    for s in range(8):
        ws = win if s == 0 else pltpu.roll(win, n - s, axis=0)
        for a in range(4):
            o = 8 * a + s
            if 1 <= o <= CONV_WIDTH:
                yield o, ws[8 * a:8 * a + n_out, :]


def _conv_fwd(z6, cw4, conv_b, w_out):
    s_len = z6.shape[1]
    tt = TIME_TILE
    n_blocks = D_MODEL // 128

    def body(z_ref, cw_ref, cb_ref, wout_hbm, q_ref, wout4_hbm, ppad, stage, wbuf, lsem, isend, irecv, fsend, frecv, osem):
        jb = pl.program_id(0)
        x_, y_, c_ = _pos()
        k = 2 * x_ + y_
        sib = (x_, y_, 1 - c_)

        def ici(r, kk):
            rk = _remote_chip(k, r)
            return _rcopy(wbuf.at[kk, c_], wbuf.at[kk, c_], isend.at[r], irecv.at[r], (rk // 2, rk % 2, c_))

        def fwd(r, hf):
            kk = _remote_chip(k, r)
            return _rcopy(wbuf.at[kk, hf], wbuf.at[kk, hf], fsend.at[r], frecv.at[r], sib)

        @pl.when(jb == 0)
        def _():
            ld = pltpu.make_async_copy(wout_hbm, stage, lsem)
            ld.start()
            ld.wait()
            for hf in range(2):
                wbuf[k, hf] = stage[hf * 256:(hf + 1) * 256, :].astype(BF16)
            for r in range(3):
                ici(r, k).start()

        @pl.when(jb == n_blocks // 2)
        def _():
            for r in range(3):
                ici(r, _remote_chip(k, r)).wait_recv()
                fwd(r, c_).start()

        zero = jnp.zeros((CONV_PAD, 128), F32)
        ppad[0:CONV_PAD, :] = zero
        ppad[s_len + CONV_PAD:s_len + 2 * CONV_PAD, :] = zero

        def fill(i, carry):
            t0 = pl.multiple_of(i * tt, tt)
            ppad[pl.ds(CONV_PAD + t0, tt), :] = z_ref[0, pl.ds(t0, tt), :] * _sigmoid(z_ref[1, pl.ds(t0, tt), :])
            return carry
        lax.fori_loop(0, s_len // tt, fill, 0)
        w = cw_ref[0]
        bias = cb_ref[...]

        def conv(i, carry):
            t0 = pl.multiple_of(i * tt, tt)
            win = ppad[pl.ds(t0, tt + 2 * CONV_PAD), :]
            acc = jnp.broadcast_to(bias, (tt, 128))
            for o, sl in _shifted_windows(win, tt):
                acc = acc + w[o - 1:o, :] * sl
            q_ref[pl.ds(t0, tt), :] = acc
            return carry
        lax.fori_loop(0, s_len // tt, conv, 0)

        @pl.when(jb == n_blocks - 1)
        def _():
            for r in range(3):
                fwd(r, 1 - c_).wait_recv()
            out = pltpu.make_async_copy(wbuf, wout4_hbm, osem)
            out.start()
            for r in range(3):
                ici(r, k).wait_send()
                fwd(r, c_).wait_send()
            out.wait()

    return pl.pallas_call(
        body, name="conv_fwd",
        grid=(n_blocks,),
        out_shape=(jax.ShapeDtypeStruct((s_len, D_MODEL), F32), jax.ShapeDtypeStruct((N_CHIPS, 2, 256, D_MODEL), BF16)),
        in_specs=[pl.BlockSpec((2, s_len, 128), lambda j: (2, 0, j)),
                  pl.BlockSpec((1, 32, 128), lambda j: (j // 2, 0, j % 2)),
                  pl.BlockSpec((1, 128), lambda j: (0, j)),
                  pl.BlockSpec(memory_space=pl.ANY)],
        out_specs=(pl.BlockSpec((s_len, 128), lambda j: (0, j)), pl.BlockSpec(memory_space=pl.ANY)),
        scratch_shapes=[pltpu.VMEM((s_len + 2 * CONV_PAD, 128), F32),
                        pltpu.VMEM((512, D_MODEL), F32),
                        pltpu.VMEM((N_CHIPS, 2, 256, D_MODEL), BF16),
                        pltpu.SemaphoreType.DMA,
                        pltpu.SemaphoreType.DMA((3,)), pltpu.SemaphoreType.DMA((3,)),
                        pltpu.SemaphoreType.DMA((3,)), pltpu.SemaphoreType.DMA((3,)),
                        pltpu.SemaphoreType.DMA],
        compiler_params=_params(dimension_semantics=("arbitrary",)),
    )(z6, cw4, conv_b, w_out)


def _middle(x, z6, q, target, ln_rows, mod, w_s, bs_exp, w_out):
    s_len = x.shape[0]
    tm = TOKEN_TILE
    n_steps = s_len // tm
    n_chunks = tm // CHUNK
    inv_d = 1.0 / D_MODEL

    def body(x_ref, z_ref, q_ref, tgt_ref, cg_ref, cb_ref, sg_ref, sb_ref, fg_ref, mod_ref, ws_ref, bs_ref, wout_ref,
             dz_ref, dq_ref, dx2_ref, ycat_ref, dy_ref, dws_ref, sm_ref, dbs_ref,
             vl_scr, vm_scr, y_scr, dycat_scr, dvm_scr, dvl_scr, acc_scr, dbs_acc):
        i = pl.program_id(0)

        @pl.when(i == 0)
        def _():
            acc_scr[...] = jnp.zeros_like(acc_scr)
            dbs_acc[...] = jnp.zeros_like(dbs_acc)
            dws_ref[...] = jnp.zeros_like(dws_ref)

        cg, cb, sg, sb, fg = cg_ref[...], cb_ref[...], sg_ref[...], sb_ref[...], fg_ref[...]
        gm = mod_ref[:, 2 * D_MODEL:3 * D_MODEL]

        def norm_stats(t):
            c = t - _mean(t)
            rstd = lax.rsqrt(_mean(c * c) + EPS)
            return c * rstd, rstd

        def phase1(rows):
            qhat, _ = norm_stats(q_ref[rows, :])
            ln = qhat * cg + cb
            gz = z_ref[0, rows, :]
            ycat_ref[rows, 0:D_MODEL] = ((ln * _sigmoid(ln)) * (gz * _sigmoid(gz))).astype(BF16)
            vhat, _ = norm_stats(z_ref[2, rows, :])
            vl_scr[rows, :] = (vhat * sg + sb).astype(BF16)
        _row_loop(tm, phase1, unroll=UNROLL)

        for ch in range(n_chunks):
            r0 = ch * CHUNK
            for h in range(HEADS):
                c0 = h * CHUNK
                vm_scr[r0:r0 + CHUNK, c0:c0 + CHUNK] = (
                    _dot_nn(ws_ref[h].astype(BF16), vl_scr[r0:r0 + CHUNK, c0:c0 + CHUNK]) + bs_ref[:, c0:c0 + CHUNK])

        def phase3(rows):
            bg = z_ref[3, rows, :]
            ycat_ref[rows, D_MODEL:2 * D_MODEL] = (z_ref[1, rows, :] * vm_scr[rows, :] * (bg * _sigmoid(bg))).astype(BF16)
        _row_loop(tm, phase3)

        y_scr[...] = _dot_nn(ycat_ref[...], wout_ref[...])

        def phase5(rows):
            y = y_scr[rows, :]
            x2 = x_ref[rows, :] + gm * y
            r2 = lax.rsqrt(_mean(x2 * x2) + EPS)
            xn2 = x2 * r2
            diff = xn2 * fg - tgt_ref[rows, :]
            acc_scr[6] += _colsum8(diff * diff)
            dout = diff * inv_d
            acc_scr[0] += _colsum8(dout * xn2)
            dxn = dout * fg
            dx2 = r2 * (dxn - xn2 * _mean(dxn * xn2))
            dx2_ref[rows, :] = dx2
            acc_scr[1] += _colsum8(dx2 * y)
            dy_ref[rows, :] = (dx2 * gm).astype(BF16)
        _row_loop(tm, phase5, unroll=UNROLL)

        dycat_scr[...] = _dot_nt(dy_ref[...], wout_ref[...])

        def phase7(rows):
            dyb = dycat_scr[rows, D_MODEL:2 * D_MODEL]
            u = z_ref[1, rows, :]
            bg = z_ref[3, rows, :]
            vm = vm_scr[rows, :]
            sig = _sigmoid(bg)
            silu = bg * sig
            dz_ref[1, rows, :] = (dyb * vm * silu).astype(BF16)
            dvm = dyb * u * silu
            dz_ref[3, rows, :] = (dyb * u * vm * (sig * (1.0 + bg * (1.0 - sig)))).astype(BF16)
            dvm_scr[rows, :] = dvm.astype(BF16)
            pos = pl.ds(pl.multiple_of(rows.start % CHUNK, ROWS), ROWS)
            dbs_acc[pos, :] += dvm
        _row_loop(tm, phase7)

        for ch in range(n_chunks):
            r0 = ch * CHUNK
            for h in range(HEADS):
                c0 = h * CHUNK
                dvm_b = dvm_scr[r0:r0 + CHUNK, c0:c0 + CHUNK]
                dws_ref[h] += _dot_nt(dvm_b, vl_scr[r0:r0 + CHUNK, c0:c0 + CHUNK])
                dvl_scr[r0:r0 + CHUNK, c0:c0 + CHUNK] = _dot_tn(ws_ref[h].astype(BF16), dvm_b)

        def phase9(rows):
            vhat, rstd_v = norm_stats(z_ref[2, rows, :])
            dvl = dvl_scr[rows, :]
            acc_scr[4] += _colsum8(dvl * vhat)
            acc_scr[5] += _colsum8(dvl)
            dvh = dvl * sg
            dz_ref[2, rows, :] = (rstd_v * (dvh - _mean(dvh) - vhat * _mean(dvh * vhat))).astype(BF16)
            qhat, rstd_q = norm_stats(q_ref[rows, :])
            ln = qhat * cg + cb
            sig_ln = _sigmoid(ln)
            gz = z_ref[0, rows, :]
            sig_g = _sigmoid(gz)
            dya = dycat_scr[rows, 0:D_MODEL]
            dz_ref[0, rows, :] = (dya * (ln * sig_ln) * (sig_g * (1.0 + gz * (1.0 - sig_g)))).astype(BF16)
            dln = (dya * (gz * sig_g)) * (sig_ln * (1.0 + ln * (1.0 - sig_ln)))
            acc_scr[2] += _colsum8(dln * qhat)
            acc_scr[3] += _colsum8(dln)
            dqh = dln * cg
            dq_ref[rows, :] = rstd_q * (dqh - _mean(dqh) - qhat * _mean(dqh * qhat))
        _row_loop(tm, phase9, unroll=UNROLL)

        @pl.when(i == n_steps - 1)
        def _():
            for qi in range(8):
                scale = 0.5 * inv_d if qi == 6 else 1.0
                sm_ref[qi:qi + 1, :] = jnp.sum(acc_scr[qi], axis=0, keepdims=True) * scale
            lane = lax.broadcasted_iota(jnp.int32, (CHUNK, CHUNK), 1)
            tile = jnp.zeros((CHUNK, CHUNK), F32)
            for h in range(HEADS):
                col = jnp.sum(dbs_acc[:, h * CHUNK:(h + 1) * CHUNK], axis=1, keepdims=True)
                tile = jnp.where(lane == h, col, tile)
            dbs_ref[...] = tile

    tok = lambda i: (i, 0)
    const2 = lambda i: (0, 0)
    return pl.pallas_call(
        body, name="middle",
        grid=(n_steps,),
        out_shape=(jax.ShapeDtypeStruct((N_GROUPS, s_len, D_MODEL), BF16),
                   jax.ShapeDtypeStruct((s_len, D_MODEL), F32),
                   jax.ShapeDtypeStruct((s_len, D_MODEL), F32),
                   jax.ShapeDtypeStruct((s_len, 2 * D_MODEL), BF16),
                   jax.ShapeDtypeStruct((s_len, D_MODEL), BF16),
                   jax.ShapeDtypeStruct((HEADS, CHUNK, CHUNK), F32),
                   jax.ShapeDtypeStruct((8, D_MODEL), F32),
                   jax.ShapeDtypeStruct((CHUNK, CHUNK), F32)),
        in_specs=[pl.BlockSpec((tm, D_MODEL), tok),
                  pl.BlockSpec((4, tm, D_MODEL), lambda i: (0, i, 0)),
                  pl.BlockSpec((tm, D_MODEL), tok),
                  pl.BlockSpec((tm, D_MODEL), tok),
                  *[pl.BlockSpec((1, D_MODEL), const2) for _ in range(5)],
                  pl.BlockSpec((1, 3 * D_MODEL), const2),
                  pl.BlockSpec((HEADS, CHUNK, CHUNK), lambda i: (0, 0, 0)),
                  pl.BlockSpec((CHUNK, D_MODEL), const2),
                  pl.BlockSpec((2 * D_MODEL, D_MODEL), const2, pipeline_mode=pl.Buffered(1))],
        out_specs=(pl.BlockSpec((4, tm, D_MODEL), lambda i: (0, i, 0)),
                   pl.BlockSpec((tm, D_MODEL), tok),
                   pl.BlockSpec((tm, D_MODEL), tok),
                   pl.BlockSpec((tm, 2 * D_MODEL), tok),
                   pl.BlockSpec((tm, D_MODEL), tok),
                   pl.BlockSpec((HEADS, CHUNK, CHUNK), lambda i: (0, 0, 0)),
                   pl.BlockSpec((8, D_MODEL), const2),
                   pl.BlockSpec((CHUNK, CHUNK), const2)),
        scratch_shapes=[pltpu.VMEM((tm, D_MODEL), BF16),
                        pltpu.VMEM((tm, D_MODEL), F32),
                        pltpu.VMEM((tm, D_MODEL), F32),
                        pltpu.VMEM((tm, 2 * D_MODEL), F32),
                        pltpu.VMEM((tm, D_MODEL), BF16),
                        pltpu.VMEM((tm, D_MODEL), F32),
                        pltpu.VMEM((8, 8, D_MODEL), F32),
                        pltpu.VMEM((CHUNK, D_MODEL), F32)],
        compiler_params=_params(dimension_semantics=("arbitrary",)),
    )(x, z6, q, target, *ln_rows, mod, w_s, bs_exp, w_out)


def _conv_bwd(dq, z6, cw4, dz6):
    s_len = dq.shape[0]
    tt = TIME_TILE

    def body(dq_ref, z_ref, cw_ref, dz_in, dz_ref, dcw_ref, dqpad, ppad, wacc):
        del dz_in
        zero = jnp.zeros((CONV_PAD, 128), F32)
        for pad in (dqpad, ppad):
            pad[0:CONV_PAD, :] = zero
            pad[s_len + CONV_PAD:s_len + 2 * CONV_PAD, :] = zero
        wacc[...] = jnp.zeros_like(wacc)

        def fill(i, carry):
            t0 = pl.multiple_of(i * tt, tt)
            ppad[pl.ds(CONV_PAD + t0, tt), :] = z_ref[0, pl.ds(t0, tt), :] * _sigmoid(z_ref[1, pl.ds(t0, tt), :])
            dqpad[pl.ds(CONV_PAD + t0, tt), :] = dq_ref[pl.ds(t0, tt), :]
            return carry
        lax.fori_loop(0, s_len // tt, fill, 0)
        w = cw_ref[0]

        def bwd(i, carry):
            t0 = pl.multiple_of(i * tt, tt)
            dp = jnp.zeros((tt, 128), F32)
            for o, sl in _shifted_windows(dqpad[pl.ds(t0, tt + 2 * CONV_PAD), :], tt):
                dp = dp + w[CONV_WIDTH - o:CONV_WIDTH - o + 1, :] * sl
            a = z_ref[0, pl.ds(t0, tt), :]
            sig = _sigmoid(z_ref[1, pl.ds(t0, tt), :])
            dz_ref[0, pl.ds(t0, tt), :] = (dp * sig).astype(BF16)
            dz_ref[1, pl.ds(t0, tt), :] = (dp * a * (sig * (1.0 - sig))).astype(BF16)
            dqt = dq_ref[pl.ds(t0, tt), :]
            for o, sl in _shifted_windows(ppad[pl.ds(t0, tt + 2 * CONV_PAD), :], tt):
                wacc[o - 1] += _colsum8(dqt * sl)
            wacc[CONV_WIDTH] += _colsum8(dqt)
            return carry
        lax.fori_loop(0, s_len // tt, bwd, 0)
        for k in range(32):
            dcw_ref[k:k + 1, :] = jnp.sum(wacc[k], axis=0, keepdims=True)

    return pl.pallas_call(
        body, name="conv_bwd",
        grid=(D_MODEL // 128,),
        out_shape=(jax.ShapeDtypeStruct(dz6.shape, BF16), jax.ShapeDtypeStruct((32, D_MODEL), F32)),
        in_specs=[pl.BlockSpec((s_len, 128), lambda j: (0, j)),
                  pl.BlockSpec((2, s_len, 128), lambda j: (2, 0, j)),
                  pl.BlockSpec((1, 32, 128), lambda j: (j // 2, 0, j % 2)),
                  pl.BlockSpec(memory_space=pl.ANY)],
        out_specs=(pl.BlockSpec((2, s_len, 128), lambda j: (2, 0, j)),
                   pl.BlockSpec((32, 128), lambda j: (0, j))),
        scratch_shapes=[pltpu.VMEM((s_len + 2 * CONV_PAD, 128), F32), pltpu.VMEM((s_len + 2 * CONV_PAD, 128), F32),
                        pltpu.VMEM((32, 8, 128), F32)],
        input_output_aliases={3: 0},
        compiler_params=_params(dimension_semantics=("arbitrary",)),
    )(dq, z6, cw4, dz6)


def _bwd_in(dz6, w12, x, dx2, mod, norm_g):
    s_len = x.shape[0]
    tm = TOKEN_TILE
    n_steps = s_len // tm

    def body(dz_ref, w_ref, x_ref, dx2_ref, mod_ref, g_ref, gx_ref, sm_ref, dh_scr, acc_scr):
        i = pl.program_id(0)

        @pl.when(i == 0)
        def _():
            acc_scr[...] = jnp.zeros_like(acc_scr)

        dh = jnp.zeros((tm, D_MODEL), F32)
        for j in range(N_GROUPS):
            n = _natural_group(j)
            for hf in range(2):
                dh = dh + _dot_nt(dz_ref[j, :, hf * 512:(hf + 1) * 512], w_ref[2 * n + hf])
        dh_scr[...] = dh
        scale1 = 1.0 + mod_ref[:, D_MODEL:2 * D_MODEL]
        g = g_ref[...]

        def rows_fn(rows):
            xt = x_ref[rows, :]
            r = lax.rsqrt(_mean(xt * xt) + EPS)
            xn = xt * r
            dhr = dh_scr[rows, :]
            acc_scr[0] += _colsum8(dhr)
            acc_scr[1] += _colsum8(dhr * (xn * g))
            acc_scr[2] += _colsum8(dhr * scale1 * xn)
            dxn = dhr * (g * scale1)
            gx_ref[rows, :] = dx2_ref[rows, :] + r * (dxn - xn * _mean(dxn * xn))
        _row_loop(tm, rows_fn, unroll=UNROLL)

        @pl.when(i == n_steps - 1)
        def _():
            for qi in range(8):
                sm_ref[qi:qi + 1, :] = jnp.sum(acc_scr[qi], axis=0, keepdims=True)

    tok = lambda i: (i, 0)
    const2 = lambda i: (0, 0)
    return pl.pallas_call(
        body, name="bwd_in",
        grid=(n_steps,),
        out_shape=(jax.ShapeDtypeStruct((s_len, D_MODEL), F32), jax.ShapeDtypeStruct((8, D_MODEL), F32)),
        in_specs=[pl.BlockSpec((N_GROUPS, tm, D_MODEL), lambda i: (0, i, 0)),
                  pl.BlockSpec((12, D_MODEL, 512), lambda i: (0, 0, 0), pipeline_mode=pl.Buffered(1)),
                  pl.BlockSpec((tm, D_MODEL), tok),
                  pl.BlockSpec((tm, D_MODEL), tok),
                  pl.BlockSpec((1, 3 * D_MODEL), const2),
                  pl.BlockSpec((1, D_MODEL), const2)],
        out_specs=(pl.BlockSpec((tm, D_MODEL), tok), pl.BlockSpec((8, D_MODEL), const2)),
        scratch_shapes=[pltpu.VMEM((tm, D_MODEL), F32), pltpu.VMEM((8, 8, D_MODEL), F32)],
        compiler_params=_params(dimension_semantics=("arbitrary",)),
    )(dz6, w12, x, dx2, mod, norm_g)


def _grad_reduce_scatter(kidx, a, b, *, name, nj, nr, nc, a_spec, b_spec, b_lead, kt):
    nblk = N_CHIPS * nj
    nfor = 3 * nj

    def body(k_ref, a_ref, b_ref, out_ref, acc, recv_a, send_b, recv_b, own_ps,
             lsem, psend, precv, isend, irecv, fsend, frecv):
        del k_ref
        i = pl.program_id(0)
        kk = pl.program_id(1)
        x, y, c = _pos()
        k = 2 * x + y
        sib = (x, y, 1 - c)
        slot = i % 2

        @pl.when(kk == 0)
        def _():
            acc[slot] = jnp.zeros((2, nr, nc), F32)

        def pair_copy(ii, sl):
            return _rcopy(acc.at[sl, 1 - c], recv_a.at[ii], psend.at[ii], precv.at[ii], sib)

        def pair_sum(ii, sl):
            cp = pair_copy(ii, sl)
            cp.wait_recv()
            cp.wait_send()
            return acc[sl, c] + recv_a[ii]

        def ici_copy(ii, kd, rslot):
            return _rcopy(send_b.at[ii], recv_b.at[rslot], isend.at[ii], irecv.at[rslot], (kd // 2, kd % 2, c))

        def finalize(ii, sl):
            ps = pair_sum(ii, sl)
            s = ii // nj
            j = ii % nj

            @pl.when(s < 3)
            def _():
                send_b[ii] = ps.astype(BF16)
                ici_copy(ii, (k + 1 + s) % N_CHIPS, (2 - s) * nj + j).start()

            @pl.when(s == 3)
            def _():
                own_ps[j] = ps

        @pl.when(jnp.logical_and(kk == kt - 1, i >= 1))
        def _():
            finalize(i - 1, 1 - slot)

        res = _dot_tn(a_ref[...], b_ref[0] if b_lead else b_ref[...])
        acc[slot, 0] += res[0:nr]
        acc[slot, 1] += res[nr:2 * nr]

        @pl.when(kk == kt - 1)
        def _():
            pair_copy(i, slot).start()

            @pl.when(i == nblk - 1)
            def _():
                own_ps[nj - 1] = pair_sum(nblk - 1, (nblk - 1) % 2)
                for r in range(nfor):
                    ici_copy(0, k, r).wait_recv()
                for j in range(nj):
                    tot = own_ps[j]
                    for s in range(3):
                        tot = tot + recv_b[s * nj + j].astype(F32)
                    own_ps[j] = tot
                loc = pltpu.make_async_copy(own_ps, out_ref.at[:, c], lsem)
                loc.start()
                swap = _rcopy(own_ps, out_ref.at[:, c], fsend, frecv, sib)
                swap.start()
                loc.wait()
                swap.wait_send()
                _rcopy(own_ps, out_ref.at[:, 1 - c], fsend, frecv, sib).wait_recv()
                for ii in range(nfor):
                    ici_copy(ii, k, 0).wait_send()

    def seq(i, k_ref):
        return (nj * (k_ref[0] + 1) + i) % nblk

    grid_spec = pltpu.PrefetchScalarGridSpec(
        num_scalar_prefetch=1,
        grid=(nblk, kt),
        in_specs=[a_spec(seq), b_spec(seq)],
        out_specs=pl.BlockSpec(memory_space=pl.ANY),
        scratch_shapes=[pltpu.VMEM((2, 2, nr, nc), F32),
                        pltpu.VMEM((nblk, nr, nc), F32),
                        pltpu.VMEM((nfor, nr, nc), BF16),
                        pltpu.VMEM((nfor, nr, nc), BF16),
                        pltpu.VMEM((nj, nr, nc), F32),
                        pltpu.SemaphoreType.DMA,
                        pltpu.SemaphoreType.DMA((nblk,)), pltpu.SemaphoreType.DMA((nblk,)),
                        pltpu.SemaphoreType.DMA((nfor,)), pltpu.SemaphoreType.DMA((nfor,)),
                        pltpu.SemaphoreType.DMA, pltpu.SemaphoreType.DMA])
    return pl.pallas_call(
        body, name=name,
        grid_spec=grid_spec,
        out_shape=jax.ShapeDtypeStruct((nj, 2, nr, nc), F32),
        compiler_params=_params(dimension_semantics=("arbitrary", "arbitrary")),
    )(kidx, a, b)


def _grad_w_in(kidx, h, dz6):
    s_len = h.shape[0]
    tk = min(K_TILE, s_len)
    return _grad_reduce_scatter(
        kidx, h, dz6, name="grad_w_in", nj=3, nr=512, nc=512, b_lead=True, kt=s_len // tk,
        a_spec=lambda seq: pl.BlockSpec((tk, D_MODEL), lambda i, kk, k_ref: (kk, 0)),
        b_spec=lambda seq: pl.BlockSpec(
            (1, tk, 512), lambda i, kk, k_ref: ((seq(i, k_ref) // 2 + 4) % N_GROUPS, kk, seq(i, k_ref) % 2)))


def _grad_w_out(kidx, ycat, dy):
    s_len = dy.shape[0]
    tk = min(K_TILE, s_len)
    return _grad_reduce_scatter(
        kidx, ycat, dy, name="grad_w_out", nj=1, nr=256, nc=D_MODEL, b_lead=False, kt=s_len // tk,
        a_spec=lambda seq: pl.BlockSpec((tk, 512), lambda i, kk, k_ref: (kk, seq(i, k_ref))),
        b_spec=lambda seq: pl.BlockSpec((tk, D_MODEL), lambda i, kk, k_ref: (kk, 0)))


def _adamw_math(w, g, m, v):
    m = ADAM_B1 * m + (1.0 - ADAM_B1) * g
    v = ADAM_B2 * v + (1.0 - ADAM_B2) * (g * g)
    m_hat = m / (1.0 - ADAM_B1 ** ADAM_STEP)
    v_hat = v / (1.0 - ADAM_B2 ** ADAM_STEP)
    delta = -ADAM_LR * (m_hat / (jnp.sqrt(v_hat) + ADAM_EPS) + ADAM_WD * w)
    return delta, m, v


def _adamw_blocked(w, m, v, g4, name):
    nj, _, nr, nc = g4.shape

    def body(w_ref, m_ref, v_ref, g_ref, go_ref, d_ref, mo_ref, vo_ref):
        g = g_ref[0, 0]
        d, mn, vn = _adamw_math(w_ref[...], g, m_ref[...], v_ref[...])
        go_ref[...] = g
        d_ref[...] = d
        mo_ref[...] = mn
        vo_ref[...] = vn

    blk = pl.BlockSpec((nr, nc), lambda j, hf: (hf, j))
    return pl.pallas_call(
        body, name=name,
        grid=(nj, 2),
        out_shape=tuple(jax.ShapeDtypeStruct(w.shape, F32) for _ in range(4)),
        in_specs=[blk, blk, blk, pl.BlockSpec((1, 1, nr, nc), lambda j, hf: (j, hf, 0, 0))],
        out_specs=(blk, blk, blk, blk),
        compiler_params=_params(dimension_semantics=("arbitrary", "arbitrary")),
    )(w, m, v, g4)


def _adamw_w_ada(cact, pack, w, m, v):
    def body(cact_ref, pack_ref, w_ref, m_ref, v_ref, g_ref, d_ref, mo_ref, vo_ref, gb_ref, dmod):
        x, y, _ = _pos()
        k = 2 * x + y
        for t in range(3):
            dmod[:, t * D_MODEL:(t + 1) * D_MODEL] = pack_ref[8 * t:8 * t + 8, :]
        gb_ref[...] = jnp.sum(dmod[...], axis=0, keepdims=True)
        sel = jnp.zeros((8, 768), F32)
        for kk in range(N_CHIPS):
            sel = jnp.where(k == kk, dmod[:, kk * 768:(kk + 1) * 768], sel)
        g = _dot_tn(cact_ref[...].astype(BF16), sel.astype(BF16))
        d, mn, vn = _adamw_math(w_ref[...], g, m_ref[...], v_ref[...])
        g_ref[...] = g
        d_ref[...] = d
        mo_ref[...] = mn
        vo_ref[...] = vn

    return pl.pallas_call(
        body, name="adamw_w_ada",
        out_shape=tuple(jax.ShapeDtypeStruct(w.shape, F32) for _ in range(4)) + (
            jax.ShapeDtypeStruct((1, 3 * D_MODEL), F32),),
        in_specs=[_vmem()] * 5,
        out_specs=tuple(_vmem() for _ in range(5)),
        scratch_shapes=[pltpu.VMEM((8, 3 * D_MODEL), F32)],
        compiler_params=_params(),
    )(cact, pack, w, m, v)


def _adamw_small(items):
    n = len(items)

    def body(*refs):
        ins, outs = refs[:4 * n], refs[4 * n:]
        for i in range(n):
            w_ref, g_ref, m_ref, v_ref = ins[4 * i:4 * i + 4]
            d, mn, vn = _adamw_math(w_ref[...], g_ref[...], m_ref[...], v_ref[...])
            outs[3 * i][...] = d
            outs[3 * i + 1][...] = mn
            outs[3 * i + 2][...] = vn

    flat = [a for it in items for a in it]
    outs = pl.pallas_call(
        body, name="adamw_small",
        out_shape=tuple(jax.ShapeDtypeStruct(it[0].shape, F32) for it in items for _ in range(3)),
        in_specs=[_vmem()] * (4 * n),
        out_specs=tuple(_vmem() for _ in range(3 * n)),
        compiler_params=_params(),
    )(*flat)
    return [tuple(outs[3 * i:3 * i + 3]) for i in range(n)]


def kernel(x, c, w_ada, b_ada, norm_g, w_in, conv_w, conv_b, conv_ln_g, conv_ln_b, sg_ln_g, sg_ln_b, w_s, b_s, w_out, final_g, loss_target, m_w_ada, m_b_ada, m_norm_g, m_w_in, m_conv_w, m_conv_b, m_conv_ln_g, m_conv_ln_b, m_sg_ln_g, m_sg_ln_b, m_w_s, m_b_s, m_w_out, m_final_g, v_w_ada, v_b_ada, v_norm_g, v_w_in, v_conv_w, v_conv_b, v_conv_ln_g, v_conv_ln_b, v_sg_ln_g, v_sg_ln_b, v_w_s, v_b_s, v_w_out, v_final_g):
    s_len = x.shape[1]
    x2d = x[0]
    tgt = loss_target[0]
    row = lambda a: a.reshape(1, -1)

    kidx = (2 * lax.axis_index("x") + lax.axis_index("y")).astype(jnp.int32).reshape(1)
    cw_sh = jnp.pad(conv_w.reshape(CONV_WIDTH, 256), ((0, 1), (0, 0)))
    h, z6, mod, cact, cw4, w_in12 = _front(kidx, x2d, c, w_ada[0], b_ada, norm_g, w_in[0], cw_sh)
    w12 = w_in12.reshape(12, D_MODEL, 512)
    q, w_out4 = _conv_fwd(z6, cw4, conv_b, w_out[0])
    w_out_full = w_out4.reshape(2 * D_MODEL, D_MODEL)
    ln_rows = (conv_ln_g, conv_ln_b, sg_ln_g, sg_ln_b, row(final_g))
    bs_exp = jnp.repeat(b_s[0].T, CHUNK, axis=1)
    dz6, dq, dx2, ycat, dy, dws, sm1, dbs = _middle(x2d, z6, q, tgt, ln_rows, mod, w_s[0], bs_exp, w_out_full)
    g_w_out4 = _grad_w_out(kidx, ycat, dy)
    dz6, dcw = _conv_bwd(dq, z6, cw4, dz6)
    g_w_in4 = _grad_w_in(kidx, h, dz6)
    grad_x, sm2 = _bwd_in(dz6, w12, x2d, dx2, mod, norm_g)

    pack, dws_r, dbs_r, _, dcw_sh, dcb, loss_t = _reduce_small(sm1, sm2, dws.reshape(D_MODEL, CHUNK), dbs, dcw)

    g_w_in, d_w_in, nm_w_in, nv_w_in = _adamw_blocked(w_in[0], m_w_in[0], v_w_in[0], g_w_in4, "adamw_w_in")
    g_w_out, d_w_out, nm_w_out, nv_w_out = _adamw_blocked(w_out[0], m_w_out[0], v_w_out[0], g_w_out4, "adamw_w_out")
    g_w_ada, d_w_ada, nm_w_ada, nv_w_ada, g_b_ada = _adamw_w_ada(cact, pack, w_ada[0], m_w_ada[0], v_w_ada[0])

    g_norm_g, g_cln_g, g_cln_b, g_sln_g, g_sln_b, g_final = (pack[24 + i:25 + i] for i in range(6))
    loss = loss_t[0, 0]
    g_conv_w = dcw_sh[:CONV_WIDTH]
    g_w_s = dws_r
    g_b_s = dbs_r[:, :HEADS].T
    small = [
        (b_ada, g_b_ada, m_b_ada, v_b_ada),
        (norm_g, g_norm_g, m_norm_g, v_norm_g),
        (conv_w.reshape(CONV_WIDTH, 256), g_conv_w, m_conv_w.reshape(CONV_WIDTH, 256), v_conv_w.reshape(CONV_WIDTH, 256)),
        (conv_b, dcb, m_conv_b, v_conv_b),
        (conv_ln_g, g_cln_g, m_conv_ln_g, v_conv_ln_g),
        (conv_ln_b, g_cln_b, m_conv_ln_b, v_conv_ln_b),
        (sg_ln_g, g_sln_g, m_sg_ln_g, v_sg_ln_g),
        (sg_ln_b, g_sln_b, m_sg_ln_b, v_sg_ln_b),
        (w_s.reshape(D_MODEL, CHUNK), g_w_s, m_w_s.reshape(D_MODEL, CHUNK), v_w_s.reshape(D_MODEL, CHUNK)),
        (b_s[0], g_b_s, m_b_s[0], v_b_s[0]),
        (row(final_g), g_final, row(m_final_g), row(v_final_g)),
    ]
    upd = _adamw_small(small)

    shapes = [w_ada.shape, b_ada.shape, norm_g.shape, w_in.shape, conv_w.shape, conv_b.shape, conv_ln_g.shape,
              conv_ln_b.shape, sg_ln_g.shape, sg_ln_b.shape, w_s.shape, b_s.shape, w_out.shape, final_g.shape]
    grads = [g_w_ada, g_b_ada, g_norm_g, g_w_in, g_conv_w, dcb, g_cln_g, g_cln_b, g_sln_g, g_sln_b, g_w_s, g_b_s,
             g_w_out, g_final]
    big = {0: (d_w_ada, nm_w_ada, nv_w_ada), 3: (d_w_in, nm_w_in, nv_w_in), 12: (d_w_out, nm_w_out, nv_w_out)}
    small_pos = [1, 2, 4, 5, 6, 7, 8, 9, 10, 11, 13]
    trip = [None] * 14
    for i, t in big.items():
        trip[i] = t
    for i, t in zip(small_pos, upd):
        trip[i] = t
    fit = lambda arrs: [a.reshape(s) for a, s in zip(arrs, shapes)]
    return (loss, grad_x.reshape(x.shape), *fit(grads), *fit([t[0] for t in trip]), *fit([t[1] for t in trip]),
            *fit([t[2] for t in trip]))
```

```python
import jax
import jax.numpy as jnp
from jax import lax
from jax.experimental import pallas as pl
from jax.experimental.pallas import tpu as pltpu

F32 = jnp.float32
BF16 = jnp.bfloat16
MESH = pl.DeviceIdType.MESH

D_MODEL = 1024
N_CHIPS = 4
HEADS = 8
CHUNK = 128
CONV_WIDTH = 31
CONV_HALF = CONV_WIDTH // 2
CONV_PAD = 16
EPS = 1e-6
ADAM_LR = 0.001
ADAM_B1 = 0.9
ADAM_B2 = 0.999
ADAM_EPS = 1e-08
ADAM_WD = 0.01
ADAM_STEP = 10

V7X_VMEM_BYTES = 64 * 1024 * 1024
VMEM_LIMIT = V7X_VMEM_BYTES - 8 * 1024 * 1024
ROWS = 16
UNROLL = 8
TOKEN_TILE = 256
TIME_TILE = 128
K_TILE = 2048

N_GROUPS = 6


def _natural_group(j):
    return (j + 2) % N_GROUPS


def _pos():
    return lax.axis_index("x"), lax.axis_index("y"), lax.axis_index("c")


def _rcopy(src, dst, ssem, rsem, dev):
    return pltpu.make_async_remote_copy(src_ref=src, dst_ref=dst, send_sem=ssem, recv_sem=rsem,
                                        device_id=dev, device_id_type=MESH)


def _vmem():
    return pl.BlockSpec(memory_space=pltpu.VMEM)


def _params(**kw):
    return pltpu.CompilerParams(vmem_limit_bytes=VMEM_LIMIT, **kw)


def _sigmoid(v):
    return jax.nn.sigmoid(v)


def _row_loop(n_rows, body, unroll=1):
    def step(r, carry):
        body(pl.ds(pl.multiple_of(r * ROWS, ROWS), ROWS))
        return carry
    lax.fori_loop(0, n_rows // ROWS, step, 0, unroll=unroll)


def _colsum8(v):
    return v.reshape(v.shape[0] // 8, 8, v.shape[1]).sum(axis=0)


def _mean(v):
    return jnp.mean(v, axis=-1, keepdims=True)


def _dot_nn(a, b):
    return jnp.dot(a, b, preferred_element_type=F32)


def _dot_nt(a, b):
    return lax.dot_general(a, b, (((1,), (1,)), ((), ())), preferred_element_type=F32)


def _dot_tn(a, b):
    return lax.dot_general(a, b, (((0,), (0,)), ((), ())), preferred_element_type=F32)


def _remote_chip(k, r):
    return jnp.bitwise_xor(k, r + 1)


def _front(kidx, x, c, w_ada, b_ada, norm_g, w_in, cw):
    s_len = x.shape[0]
    tmh = min(512, s_len)
    tmz = min(1024, s_len)
    nh = s_len // tmh
    nz = s_len // tmz
    n_steps = nh + 12 * nz

    def remote_block(q):
        return jnp.where(q < 6, q % 2, 2), jnp.where(q < 6, q // 2, q - 6)

    def block_of(i, k):
        r, j = remote_block(jnp.maximum(i - 3, 0))
        return jnp.where(i < 3, 3 * k + i, 3 * _remote_chip(k, r) + j)

    def body(k_ref, x_ref, c_ref, bada_ref, g_ref, cw_ref, wada_hbm, win_hbm,
             h_ref, z_ref, mod_ref, cact_ref, cw4_ref, w12_hbm,
             h_all, wbuf, stage, wada_v, cslab, mslab, cw4_s,
             lsem, csend, crecv, msend, mrecv, wsend, wrecv, isend, irecv, fsend, frecv, osem):
        del k_ref
        t = pl.program_id(0)
        x_, y_, c_ = _pos()
        k = 2 * x_ + y_
        b = 4 * x_ + 2 * y_ + c_
        sib = (x_, y_, 1 - c_)

        def dev_of(r):
            kk = _remote_chip(k, r)
            return (kk // 2, kk % 2, c_)

        def ici(q, kk):
            r, j = remote_block(q)
            return _rcopy(wbuf.at[3 * kk + j, c_], wbuf.at[3 * kk + j, c_], isend.at[q], irecv.at[q], dev_of(r))

        def fwd(q, hf):
            r, j = remote_block(q)
            blk = 3 * _remote_chip(k, r) + j
            return _rcopy(wbuf.at[blk, hf], wbuf.at[blk, hf], fsend.at[q], frecv.at[q], sib)

        def c_copy(q, src):
            d = jnp.bitwise_xor(b, q)
            return _rcopy(cslab.at[src], cslab.at[src], csend.at[q - 1], crecv.at[q - 1], (d // 4, (d // 2) % 2, d % 2))

        def m_copy(r, kk):
            return _rcopy(mslab.at[kk], mslab.at[kk], msend.at[r], mrecv.at[r], dev_of(r))

        def cw_copy(r, kk):
            return _rcopy(cw4_s.at[kk], cw4_s.at[kk], wsend.at[r], wrecv.at[r], dev_of(r))

        def to_hbm(i):
            m = block_of(i, k)
            return pltpu.make_async_copy(wbuf.at[m], w12_hbm.at[m], osem.at[i])

        @pl.when(t == 0)
        def _():
            ld_w = pltpu.make_async_copy(win_hbm, stage, lsem.at[0])
            ld_w.start()
            ld_a = pltpu.make_async_copy(wada_hbm, wada_v, lsem.at[1])
            ld_a.start()
            cslab[b] = jnp.broadcast_to(c_ref[...], (8, D_MODEL))
            for q in range(1, 8):
                c_copy(q, b).start()
            cw4_s[k] = cw_ref[...]
            for r in range(3):
                cw_copy(r, k).start()
            ld_w.wait()
            for j in range(3):
                for hf in range(2):
                    wbuf[3 * k + j, hf] = stage[hf * 512:(hf + 1) * 512, j * 512:(j + 1) * 512].astype(BF16)
            for i in range(3):
                to_hbm(i).start()
            for q in range(1, 8):
                c_copy(q, jnp.bitwise_xor(b, q)).wait_recv()
            row = lax.broadcasted_iota(jnp.int32, (8, D_MODEL), 0)
            call = jnp.zeros((8, D_MODEL), F32)
            for d in range(8):
                call = jnp.where(row == d, cslab[d], call)
            cact = call * _sigmoid(call)
            cact_ref[...] = cact
            ld_a.wait()
            mslab[k] = _dot_nn(cact.astype(BF16), wada_v[...].astype(BF16))
            for r in range(3):
                m_copy(r, k).start()
            for q in range(9):
                ici(q, k).start()
            for r in range(3):
                m_copy(r, _remote_chip(k, r)).wait_recv()
            row8 = lax.broadcasted_iota(jnp.int32, (8, 768), 0)
            for kk in range(N_CHIPS):
                piece = jnp.sum(jnp.where(row8 == b, mslab[kk], 0.0), axis=0, keepdims=True)
                mod_ref[:, kk * 768:(kk + 1) * 768] = piece + bada_ref[:, kk * 768:(kk + 1) * 768]
            for r in range(3):
                cw_copy(r, _remote_chip(k, r)).wait_recv()
            cw4_ref[...] = cw4_s[...]

        @pl.when(t < nh)
        def _():
            shift = mod_ref[:, 0:D_MODEL]
            scale1 = 1.0 + mod_ref[:, D_MODEL:2 * D_MODEL]
            g = g_ref[...]
            base = t * tmh

            def rows_fn(rows):
                xt = x_ref[rows, :]
                r = lax.rsqrt(_mean(xt * xt) + EPS)
                hv = ((xt * r * g) * scale1 + shift).astype(BF16)
                h_ref[rows, :] = hv
                h_all[pl.ds(pl.multiple_of(base + rows.start, ROWS), ROWS), :] = hv
            _row_loop(tmh, rows_fn, unroll=UNROLL)

        @pl.when(t >= nh)
        def _():
            u = t - nh
            i = u // nz
            rt = u % nz
            @pl.when(jnp.logical_and(rt == 0, jnp.logical_and(i >= 2, i <= 10)))
            def _():
                q = i - 2
                r, _ = remote_block(q)
                ici(q, _remote_chip(k, r)).wait_recv()
                fwd(q, c_).start()

            @pl.when(jnp.logical_and(rt == 0, i >= 3))
            def _():
                fwd(i - 3, 1 - c_).wait_recv()
                to_hbm(i).start()
            m = block_of(i, k)
            hb = h_all[pl.ds(pl.multiple_of(rt * tmz, tmz), tmz), :]
            z_ref[0] = _dot_nn(hb[:, 0:512], wbuf[m, 0]) + _dot_nn(hb[:, 512:1024], wbuf[m, 1])

        @pl.when(t == n_steps - 1)
        def _():
            for q in range(1, 8):
                c_copy(q, b).wait_send()
            for r in range(3):
                m_copy(r, k).wait_send()
                cw_copy(r, k).wait_send()
            for q in range(9):
                ici(q, k).wait_send()
                fwd(q, c_).wait_send()
            for i in range(12):
                to_hbm(i).wait()

    def z_index(t, k_ref):
        u = jnp.maximum(t - nh, 0)
        m = block_of(u // nz, k_ref[0])
        return ((m // 2 + 4) % N_GROUPS, u % nz, m % 2)

    tok = lambda t, k_ref: (jnp.minimum(t, nh - 1), 0)
    const2 = lambda t, k_ref: (0, 0)
    grid_spec = pltpu.PrefetchScalarGridSpec(
        num_scalar_prefetch=1,
        grid=(n_steps,),
        in_specs=[pl.BlockSpec((tmh, D_MODEL), tok),
                  pl.BlockSpec((1, D_MODEL), const2),
                  pl.BlockSpec((1, 3 * D_MODEL), const2),
                  pl.BlockSpec((1, D_MODEL), const2),
                  pl.BlockSpec((32, 256), const2),
                  pl.BlockSpec(memory_space=pl.ANY),
                  pl.BlockSpec(memory_space=pl.ANY)],
        out_specs=(pl.BlockSpec((tmh, D_MODEL), tok),
                   pl.BlockSpec((1, tmz, 512), z_index),
                   pl.BlockSpec((1, 3 * D_MODEL), const2),
                   pl.BlockSpec((8, D_MODEL), const2),
                   pl.BlockSpec((N_CHIPS, 32, 256), lambda t, k_ref: (0, 0, 0)),
                   pl.BlockSpec(memory_space=pl.ANY)),
        scratch_shapes=[pltpu.VMEM((s_len, D_MODEL), BF16),
                        pltpu.VMEM((12, 2, 512, 512), BF16),
                        pltpu.VMEM((D_MODEL, 1536), F32),
                        pltpu.VMEM((D_MODEL, 768), F32),
                        pltpu.VMEM((8, 8, D_MODEL), F32),
                        pltpu.VMEM((N_CHIPS, 8, 768), F32),
                        pltpu.VMEM((N_CHIPS, 32, 256), F32),
                        pltpu.SemaphoreType.DMA((2,)),
                        pltpu.SemaphoreType.DMA((7,)), pltpu.SemaphoreType.DMA((7,)),
                        pltpu.SemaphoreType.DMA((3,)), pltpu.SemaphoreType.DMA((3,)),
                        pltpu.SemaphoreType.DMA((3,)), pltpu.SemaphoreType.DMA((3,)),
                        pltpu.SemaphoreType.DMA((9,)), pltpu.SemaphoreType.DMA((9,)),
                        pltpu.SemaphoreType.DMA((9,)), pltpu.SemaphoreType.DMA((9,)),
                        pltpu.SemaphoreType.DMA((12,))])
    return pl.pallas_call(
        body, name="front",
        grid_spec=grid_spec,
        out_shape=(jax.ShapeDtypeStruct((s_len, D_MODEL), BF16),
                   jax.ShapeDtypeStruct((N_GROUPS, s_len, D_MODEL), F32),
                   jax.ShapeDtypeStruct((1, 3 * D_MODEL), F32),
                   jax.ShapeDtypeStruct((8, D_MODEL), F32),
                   jax.ShapeDtypeStruct((N_CHIPS, 32, 256), F32),
                   jax.ShapeDtypeStruct((12, 2, 512, 512), BF16)),
        compiler_params=_params(dimension_semantics=("arbitrary",)),
    )(kidx, x, c, b_ada, norm_g, cw, w_ada, w_in)


def _all_reduce_partners():
    x, y, c = _pos()
    return [(x, y, 1 - c), (x, 1 - y, c), (1 - x, y, c)]


def _butterfly_event(e, bufs, recvs, ssem, rsem, partners):
    n = len(bufs)

    def copies(s):
        return [_rcopy(buf, recv.at[s], ssem.at[s * n + i], rsem.at[s * n + i], partners[s])
                for i, (buf, recv) in enumerate(zip(bufs, recvs))]

    if e > 0:
        for cp in copies(e - 1):
            cp.wait()
        for buf, recv in zip(bufs, recvs):
            buf[...] = buf[...] + recv[e - 1]
    if e < len(partners):
        for cp in copies(e):
            cp.start()


def _onehot_rows(v, b):
    row = lax.broadcasted_iota(jnp.int32, (8, v.shape[1]), 0)
    return jnp.where(row == b, jnp.broadcast_to(v, (8, v.shape[1])), 0.0)


def _shifted_windows(win, n_out):
    n = win.shape[0]
    for s in range(8):
        ws = win if s == 0 else pltpu.roll(win, n - s, axis=0)
        for a in range(4):
            o = 8 * a + s
            if 1 <= o <= CONV_WIDTH:
                yield o, ws[8 * a:8 * a + n_out, :]


def _conv_fwd(z6, cw4, conv_b, w_out):
    s_len = z6.shape[1]
    tt = TIME_TILE
    n_blocks = D_MODEL // 128

    def body(z_ref, cw_ref, cb_ref, wout_hbm, q_ref, wout4_hbm, ppad, stage, wbuf, lsem, isend, irecv, fsend, frecv, osem):
        jb = pl.program_id(0)
        x_, y_, c_ = _pos()
        k = 2 * x_ + y_
        sib = (x_, y_, 1 - c_)

        def ici(r, kk):
            rk = _remote_chip(k, r)
            return _rcopy(wbuf.at[kk, c_], wbuf.at[kk, c_], isend.at[r], irecv.at[r], (rk // 2, rk % 2, c_))

        def fwd(r, hf):
            kk = _remote_chip(k, r)
            return _rcopy(wbuf.at[kk, hf], wbuf.at[kk, hf], fsend.at[r], frecv.at[r], sib)

        @pl.when(jb == 0)
        def _():
            ld = pltpu.make_async_copy(wout_hbm, stage, lsem)
            ld.start()
            ld.wait()
            for hf in range(2):
                wbuf[k, hf] = stage[hf * 256:(hf + 1) * 256, :].astype(BF16)
            for r in range(3):
                ici(r, k).start()

        @pl.when(jb == n_blocks // 2)
        def _():
            for r in range(3):
                ici(r, _remote_chip(k, r)).wait_recv()
                fwd(r, c_).start()

        zero = jnp.zeros((CONV_PAD, 128), F32)
        ppad[0:CONV_PAD, :] = zero
        ppad[s_len + CONV_PAD:s_len + 2 * CONV_PAD, :] = zero

        def fill(i, carry):
            t0 = pl.multiple_of(i * tt, tt)
            ppad[pl.ds(CONV_PAD + t0, tt), :] = z_ref[0, pl.ds(t0, tt), :] * _sigmoid(z_ref[1, pl.ds(t0, tt), :])
            return carry
        lax.fori_loop(0, s_len // tt, fill, 0)
        w = cw_ref[0]
        bias = cb_ref[...]

        def conv(i, carry):
            t0 = pl.multiple_of(i * tt, tt)
            win = ppad[pl.ds(t0, tt + 2 * CONV_PAD), :]
            acc = jnp.broadcast_to(bias, (tt, 128))
            for o, sl in _shifted_windows(win, tt):
                acc = acc + w[o - 1:o, :] * sl
            q_ref[pl.ds(t0, tt), :] = acc
            return carry
        lax.fori_loop(0, s_len // tt, conv, 0)

        @pl.when(jb == n_blocks - 1)
        def _():
            for r in range(3):
                fwd(r, 1 - c_).wait_recv()
            out = pltpu.make_async_copy(wbuf, wout4_hbm, osem)
            out.start()
            for r in range(3):
                ici(r, k).wait_send()
                fwd(r, c_).wait_send()
            out.wait()

    return pl.pallas_call(
        body, name="conv_fwd",
        grid=(n_blocks,),
        out_shape=(jax.ShapeDtypeStruct((s_len, D_MODEL), F32), jax.ShapeDtypeStruct((N_CHIPS, 2, 256, D_MODEL), BF16)),
        in_specs=[pl.BlockSpec((2, s_len, 128), lambda j: (2, 0, j)),
                  pl.BlockSpec((1, 32, 128), lambda j: (j // 2, 0, j % 2)),
                  pl.BlockSpec((1, 128), lambda j: (0, j)),
                  pl.BlockSpec(memory_space=pl.ANY)],
        out_specs=(pl.BlockSpec((s_len, 128), lambda j: (0, j)), pl.BlockSpec(memory_space=pl.ANY)),
        scratch_shapes=[pltpu.VMEM((s_len + 2 * CONV_PAD, 128), F32),
                        pltpu.VMEM((512, D_MODEL), F32),
                        pltpu.VMEM((N_CHIPS, 2, 256, D_MODEL), BF16),
                        pltpu.SemaphoreType.DMA,
                        pltpu.SemaphoreType.DMA((3,)), pltpu.SemaphoreType.DMA((3,)),
                        pltpu.SemaphoreType.DMA((3,)), pltpu.SemaphoreType.DMA((3,)),
                        pltpu.SemaphoreType.DMA],
        compiler_params=_params(dimension_semantics=("arbitrary",)),
    )(z6, cw4, conv_b, w_out)


def _middle(x, z6, q, target, ln_rows, mod, w_s, bs_exp, w_out):
    s_len = x.shape[0]
    tm = TOKEN_TILE
    n_steps = s_len // tm
    n_chunks = tm // CHUNK
    inv_d = 1.0 / D_MODEL

    def body(x_ref, z_ref, q_ref, tgt_ref, cg_ref, cb_ref, sg_ref, sb_ref, fg_ref, mod_ref, ws_ref, bs_ref, wout_ref,
             dz_ref, dq_ref, dx2_ref, ycat_ref, dy_ref, dws_ref, sm_ref, dbs_ref,
             vl_scr, vm_scr, y_scr, dycat_scr, dvm_scr, dvl_scr, acc_scr, dbs_acc):
        i = pl.program_id(0)

        @pl.when(i == 0)
        def _():
            acc_scr[...] = jnp.zeros_like(acc_scr)
            dbs_acc[...] = jnp.zeros_like(dbs_acc)
            dws_ref[...] = jnp.zeros_like(dws_ref)

        cg, cb, sg, sb, fg = cg_ref[...], cb_ref[...], sg_ref[...], sb_ref[...], fg_ref[...]
        gm = mod_ref[:, 2 * D_MODEL:3 * D_MODEL]

        def norm_stats(t):
            c = t - _mean(t)
            rstd = lax.rsqrt(_mean(c * c) + EPS)
            return c * rstd, rstd

        def phase1(rows):
            qhat, _ = norm_stats(q_ref[rows, :])
            ln = qhat * cg + cb
            gz = z_ref[0, rows, :]
            ycat_ref[rows, 0:D_MODEL] = ((ln * _sigmoid(ln)) * (gz * _sigmoid(gz))).astype(BF16)
            vhat, _ = norm_stats(z_ref[2, rows, :])
            vl_scr[rows, :] = (vhat * sg + sb).astype(BF16)
        _row_loop(tm, phase1, unroll=UNROLL)

        for ch in range(n_chunks):
            r0 = ch * CHUNK
            for h in range(HEADS):
                c0 = h * CHUNK
                vm_scr[r0:r0 + CHUNK, c0:c0 + CHUNK] = (
                    _dot_nn(ws_ref[h].astype(BF16), vl_scr[r0:r0 + CHUNK, c0:c0 + CHUNK]) + bs_ref[:, c0:c0 + CHUNK])

        def phase3(rows):
            bg = z_ref[3, rows, :]
            ycat_ref[rows, D_MODEL:2 * D_MODEL] = (z_ref[1, rows, :] * vm_scr[rows, :] * (bg * _sigmoid(bg))).astype(BF16)
        _row_loop(tm, phase3)

        y_scr[...] = _dot_nn(ycat_ref[...], wout_ref[...])

        def phase5(rows):
            y = y_scr[rows, :]
            x2 = x_ref[rows, :] + gm * y
            r2 = lax.rsqrt(_mean(x2 * x2) + EPS)
            xn2 = x2 * r2
            diff = xn2 * fg - tgt_ref[rows, :]
            acc_scr[6] += _colsum8(diff * diff)
            dout = diff * inv_d
            acc_scr[0] += _colsum8(dout * xn2)
            dxn = dout * fg
            dx2 = r2 * (dxn - xn2 * _mean(dxn * xn2))
            dx2_ref[rows, :] = dx2
            acc_scr[1] += _colsum8(dx2 * y)
            dy_ref[rows, :] = (dx2 * gm).astype(BF16)
        _row_loop(tm, phase5, unroll=UNROLL)

        dycat_scr[...] = _dot_nt(dy_ref[...], wout_ref[...])

        def phase7(rows):
            dyb = dycat_scr[rows, D_MODEL:2 * D_MODEL]
            u = z_ref[1, rows, :]
            bg = z_ref[3, rows, :]
            vm = vm_scr[rows, :]
            sig = _sigmoid(bg)
            silu = bg * sig
            dz_ref[1, rows, :] = (dyb * vm * silu).astype(BF16)
            dvm = dyb * u * silu
            dz_ref[3, rows, :] = (dyb * u * vm * (sig * (1.0 + bg * (1.0 - sig)))).astype(BF16)
            dvm_scr[rows, :] = dvm.astype(BF16)
            pos = pl.ds(pl.multiple_of(rows.start % CHUNK, ROWS), ROWS)
            dbs_acc[pos, :] += dvm
        _row_loop(tm, phase7)

        for ch in range(n_chunks):
            r0 = ch * CHUNK
            for h in range(HEADS):
                c0 = h * CHUNK
                dvm_b = dvm_scr[r0:r0 + CHUNK, c0:c0 + CHUNK]
                dws_ref[h] += _dot_nt(dvm_b, vl_scr[r0:r0 + CHUNK, c0:c0 + CHUNK])
                dvl_scr[r0:r0 + CHUNK, c0:c0 + CHUNK] = _dot_tn(ws_ref[h].astype(BF16), dvm_b)

        def phase9(rows):
            vhat, rstd_v = norm_stats(z_ref[2, rows, :])
            dvl = dvl_scr[rows, :]
            acc_scr[4] += _colsum8(dvl * vhat)
            acc_scr[5] += _colsum8(dvl)
            dvh = dvl * sg
            dz_ref[2, rows, :] = (rstd_v * (dvh - _mean(dvh) - vhat * _mean(dvh * vhat))).astype(BF16)
            qhat, rstd_q = norm_stats(q_ref[rows, :])
            ln = qhat * cg + cb
            sig_ln = _sigmoid(ln)
            gz = z_ref[0, rows, :]
            sig_g = _sigmoid(gz)
            dya = dycat_scr[rows, 0:D_MODEL]
            dz_ref[0, rows, :] = (dya * (ln * sig_ln) * (sig_g * (1.0 + gz * (1.0 - sig_g)))).astype(BF16)
            dln = (dya * (gz * sig_g)) * (sig_ln * (1.0 + ln * (1.0 - sig_ln)))
            acc_scr[2] += _colsum8(dln * qhat)
            acc_scr[3] += _colsum8(dln)
            dqh = dln * cg
            dq_ref[rows, :] = rstd_q * (dqh - _mean(dqh) - qhat * _mean(dqh * qhat))
        _row_loop(tm, phase9, unroll=UNROLL)

        @pl.when(i == n_steps - 1)
        def _():
            for qi in range(8):
                scale = 0.5 * inv_d if qi == 6 else 1.0
                sm_ref[qi:qi + 1, :] = jnp.sum(acc_scr[qi], axis=0, keepdims=True) * scale
            lane = lax.broadcasted_iota(jnp.int32, (CHUNK, CHUNK), 1)
            tile = jnp.zeros((CHUNK, CHUNK), F32)
            for h in range(HEADS):
                col = jnp.sum(dbs_acc[:, h * CHUNK:(h + 1) * CHUNK], axis=1, keepdims=True)
                tile = jnp.where(lane == h, col, tile)
            dbs_ref[...] = tile

    tok = lambda i: (i, 0)
    const2 = lambda i: (0, 0)
    return pl.pallas_call(
        body, name="middle",
        grid=(n_steps,),
        out_shape=(jax.ShapeDtypeStruct((N_GROUPS, s_len, D_MODEL), BF16),
                   jax.ShapeDtypeStruct((s_len, D_MODEL), F32),
                   jax.ShapeDtypeStruct((s_len, D_MODEL), F32),
                   jax.ShapeDtypeStruct((s_len, 2 * D_MODEL), BF16),
                   jax.ShapeDtypeStruct((s_len, D_MODEL), BF16),
                   jax.ShapeDtypeStruct((HEADS, CHUNK, CHUNK), F32),
                   jax.ShapeDtypeStruct((8, D_MODEL), F32),
                   jax.ShapeDtypeStruct((CHUNK, CHUNK), F32)),
        in_specs=[pl.BlockSpec((tm, D_MODEL), tok),
                  pl.BlockSpec((4, tm, D_MODEL), lambda i: (0, i, 0)),
                  pl.BlockSpec((tm, D_MODEL), tok),
                  pl.BlockSpec((tm, D_MODEL), tok),
                  *[pl.BlockSpec((1, D_MODEL), const2) for _ in range(5)],
                  pl.BlockSpec((1, 3 * D_MODEL), const2),
                  pl.BlockSpec((HEADS, CHUNK, CHUNK), lambda i: (0, 0, 0)),
                  pl.BlockSpec((CHUNK, D_MODEL), const2),
                  pl.BlockSpec((2 * D_MODEL, D_MODEL), const2, pipeline_mode=pl.Buffered(1))],
        out_specs=(pl.BlockSpec((4, tm, D_MODEL), lambda i: (0, i, 0)),
                   pl.BlockSpec((tm, D_MODEL), tok),
                   pl.BlockSpec((tm, D_MODEL), tok),
                   pl.BlockSpec((tm, 2 * D_MODEL), tok),
                   pl.BlockSpec((tm, D_MODEL), tok),
                   pl.BlockSpec((HEADS, CHUNK, CHUNK), lambda i: (0, 0, 0)),
                   pl.BlockSpec((8, D_MODEL), const2),
                   pl.BlockSpec((CHUNK, CHUNK), const2)),
        scratch_shapes=[pltpu.VMEM((tm, D_MODEL), BF16),
                        pltpu.VMEM((tm, D_MODEL), F32),
                        pltpu.VMEM((tm, D_MODEL), F32),
                        pltpu.VMEM((tm, 2 * D_MODEL), F32),
                        pltpu.VMEM((tm, D_MODEL), BF16),
                        pltpu.VMEM((tm, D_MODEL), F32),
                        pltpu.VMEM((8, 8, D_MODEL), F32),
                        pltpu.VMEM((CHUNK, D_MODEL), F32)],
        compiler_params=_params(dimension_semantics=("arbitrary",)),
    )(x, z6, q, target, *ln_rows, mod, w_s, bs_exp, w_out)


def _conv_bwd(dq, z6, cw4, dz6):
    s_len = dq.shape[0]
    tt = TIME_TILE

    def body(dq_ref, z_ref, cw_ref, dz_in, dz_ref, dcw_ref, dqpad, ppad, wacc):
        del dz_in
        zero = jnp.zeros((CONV_PAD, 128), F32)
        for pad in (dqpad, ppad):
            pad[0:CONV_PAD, :] = zero
            pad[s_len + CONV_PAD:s_len + 2 * CONV_PAD, :] = zero
        wacc[...] = jnp.zeros_like(wacc)

        def fill(i, carry):
            t0 = pl.multiple_of(i * tt, tt)
            ppad[pl.ds(CONV_PAD + t0, tt), :] = z_ref[0, pl.ds(t0, tt), :] * _sigmoid(z_ref[1, pl.ds(t0, tt), :])
            dqpad[pl.ds(CONV_PAD + t0, tt), :] = dq_ref[pl.ds(t0, tt), :]
            return carry
        lax.fori_loop(0, s_len // tt, fill, 0)
        w = cw_ref[0]

        def bwd(i, carry):
            t0 = pl.multiple_of(i * tt, tt)
            dp = jnp.zeros((tt, 128), F32)
            for o, sl in _shifted_windows(dqpad[pl.ds(t0, tt + 2 * CONV_PAD), :], tt):
                dp = dp + w[CONV_WIDTH - o:CONV_WIDTH - o + 1, :] * sl
            a = z_ref[0, pl.ds(t0, tt), :]
            sig = _sigmoid(z_ref[1, pl.ds(t0, tt), :])
            dz_ref[0, pl.ds(t0, tt), :] = (dp * sig).astype(BF16)
            dz_ref[1, pl.ds(t0, tt), :] = (dp * a * (sig * (1.0 - sig))).astype(BF16)
            dqt = dq_ref[pl.ds(t0, tt), :]
            for o, sl in _shifted_windows(ppad[pl.ds(t0, tt + 2 * CONV_PAD), :], tt):
                wacc[o - 1] += _colsum8(dqt * sl)
            wacc[CONV_WIDTH] += _colsum8(dqt)
            return carry
        lax.fori_loop(0, s_len // tt, bwd, 0)
        for k in range(32):
            dcw_ref[k:k + 1, :] = jnp.sum(wacc[k], axis=0, keepdims=True)

    return pl.pallas_call(
        body, name="conv_bwd",
        grid=(D_MODEL // 128,),
        out_shape=(jax.ShapeDtypeStruct(dz6.shape, BF16), jax.ShapeDtypeStruct((32, D_MODEL), F32)),
        in_specs=[pl.BlockSpec((s_len, 128), lambda j: (0, j)),
                  pl.BlockSpec((2, s_len, 128), lambda j: (2, 0, j)),
                  pl.BlockSpec((1, 32, 128), lambda j: (j // 2, 0, j % 2)),
                  pl.BlockSpec(memory_space=pl.ANY)],
        out_specs=(pl.BlockSpec((2, s_len, 128), lambda j: (2, 0, j)),
                   pl.BlockSpec((32, 128), lambda j: (0, j))),
        scratch_shapes=[pltpu.VMEM((s_len + 2 * CONV_PAD, 128), F32), pltpu.VMEM((s_len + 2 * CONV_PAD, 128), F32),
                        pltpu.VMEM((32, 8, 128), F32)],
        input_output_aliases={3: 0},
        compiler_params=_params(dimension_semantics=("arbitrary",)),
    )(dq, z6, cw4, dz6)


def _bwd_in(dz6, w12, x, dx2, mod, norm_g, sm1, dws, dbs, dcw):
    s_len = x.shape[0]
    tm = TOKEN_TILE
    n_steps = s_len // tm
    shp = [(16, D_MODEL), (D_MODEL, 128), (128, 128), (32, D_MODEL)]

    def body(dz_ref, w_ref, x_ref, dx2_ref, mod_ref, g_ref, sm1_ref, dws_in, dbs_in, dcw_in,
             gx_ref, sm_ref, pk1_ref, dws_ref, dbs_ref, dcwsh_ref, dcb_ref, loss_ref,
             dh_scr, acc_scr, b_pk, b_dws, b_dbs, b_dcw, r_pk, r_dws, r_dbs, r_dcw, ssem, rsem):
        i = pl.program_id(0)
        x_, y_, c_ = _pos()
        bufs = [b_pk, b_dws, b_dbs, b_dcw]
        recvs = [r_pk, r_dws, r_dbs, r_dcw]

        @pl.when(i == 0)
        def _():
            acc_scr[...] = jnp.zeros_like(acc_scr)
            b_pk[0:8, :] = _onehot_rows(sm1_ref[1:2, :], 4 * x_ + 2 * y_ + c_)
            for r, src in enumerate([2, 3, 4, 5, 0, 6]):
                b_pk[8 + r:9 + r, :] = sm1_ref[src:src + 1, :]
            b_pk[14:16, :] = jnp.zeros((2, D_MODEL), F32)
            b_dws[...] = dws_in[...]
            b_dbs[...] = dbs_in[...]
            b_dcw[...] = dcw_in[...]

        for e in range(4):
            @pl.when(i == (e * n_steps) // 4)
            def _():
                _butterfly_event(e, bufs, recvs, ssem, rsem, _all_reduce_partners())

        @pl.when(i == (3 * n_steps) // 4)
        def _():
            k = 2 * x_ + y_
            pk1_ref[...] = b_pk[...]
            dws_ref[...] = b_dws[...]
            dbs_ref[...] = b_dbs[...]
            sel = jnp.zeros((32, 256), F32)
            for kk in range(N_CHIPS):
                sel = jnp.where(k == kk, b_dcw[:, kk * 256:(kk + 1) * 256], sel)
            dcwsh_ref[...] = sel
            dcb_ref[...] = b_dcw[31:32, :]
            loss_ref[...] = jnp.broadcast_to(jnp.sum(b_pk[13:14, :], axis=1, keepdims=True), (8, 128))

        dh = jnp.zeros((tm, D_MODEL), F32)
        for j in range(N_GROUPS):
            n = _natural_group(j)
            for hf in range(2):
                dh = dh + _dot_nt(dz_ref[j, :, hf * 512:(hf + 1) * 512], w_ref[2 * n + hf])
        dh_scr[...] = dh
        scale1 = 1.0 + mod_ref[:, D_MODEL:2 * D_MODEL]
        g = g_ref[...]

        def rows_fn(rows):
            xt = x_ref[rows, :]
            r = lax.rsqrt(_mean(xt * xt) + EPS)
            xn = xt * r
            dhr = dh_scr[rows, :]
            acc_scr[0] += _colsum8(dhr)
            acc_scr[1] += _colsum8(dhr * (xn * g))
            acc_scr[2] += _colsum8(dhr * scale1 * xn)
            dxn = dhr * (g * scale1)
            gx_ref[rows, :] = dx2_ref[rows, :] + r * (dxn - xn * _mean(dxn * xn))
        _row_loop(tm, rows_fn, unroll=UNROLL)

        @pl.when(i == n_steps - 1)
        def _():
            for qi in range(8):
                sm_ref[qi:qi + 1, :] = jnp.sum(acc_scr[qi], axis=0, keepdims=True)

    tok = lambda i: (i, 0)
    const2 = lambda i: (0, 0)
    return pl.pallas_call(
        body, name="bwd_in",
        grid=(n_steps,),
        out_shape=(jax.ShapeDtypeStruct((s_len, D_MODEL), F32), jax.ShapeDtypeStruct((8, D_MODEL), F32),
                   jax.ShapeDtypeStruct((16, D_MODEL), F32), jax.ShapeDtypeStruct((D_MODEL, 128), F32),
                   jax.ShapeDtypeStruct((128, 128), F32), jax.ShapeDtypeStruct((32, 256), F32),
                   jax.ShapeDtypeStruct((1, D_MODEL), F32), jax.ShapeDtypeStruct((8, 128), F32)),
        in_specs=[pl.BlockSpec((N_GROUPS, tm, D_MODEL), lambda i: (0, i, 0)),
                  pl.BlockSpec((12, D_MODEL, 512), lambda i: (0, 0, 0), pipeline_mode=pl.Buffered(1)),
                  pl.BlockSpec((tm, D_MODEL), tok),
                  pl.BlockSpec((tm, D_MODEL), tok),
                  pl.BlockSpec((1, 3 * D_MODEL), const2),
                  pl.BlockSpec((1, D_MODEL), const2),
                  pl.BlockSpec((8, D_MODEL), const2),
                  pl.BlockSpec((D_MODEL, 128), const2),
                  pl.BlockSpec((128, 128), const2),
                  pl.BlockSpec((32, D_MODEL), const2)],
        out_specs=(pl.BlockSpec((tm, D_MODEL), tok), pl.BlockSpec((8, D_MODEL), const2),
                   pl.BlockSpec((16, D_MODEL), const2), pl.BlockSpec((D_MODEL, 128), const2),
                   pl.BlockSpec((128, 128), const2), pl.BlockSpec((32, 256), const2),
                   pl.BlockSpec((1, D_MODEL), const2), pl.BlockSpec((8, 128), const2)),
        scratch_shapes=[pltpu.VMEM((tm, D_MODEL), F32), pltpu.VMEM((8, 8, D_MODEL), F32)]
        + [pltpu.VMEM(s, F32) for s in shp] + [pltpu.VMEM((3,) + s, F32) for s in shp]
        + [pltpu.SemaphoreType.DMA((12,)), pltpu.SemaphoreType.DMA((12,))],
        compiler_params=_params(dimension_semantics=("arbitrary",)),
    )(dz6, w12, x, dx2, mod, norm_g, sm1, dws, dbs, dcw)


def _grad_reduce_scatter(kidx, a, b, *, name, nj, nr, nc, a_spec, b_spec, b_lead, kt, sm2=None):
    nblk = N_CHIPS * nj
    nfor = 3 * nj

    def body(k_ref, a_ref, b_ref, *rest):
        del k_ref
        if sm2 is None:
            out_ref, rest = rest[0], rest[1:]
        else:
            sm2_ref, out_ref, pk2_ref, rest = rest[0], rest[1], rest[2], rest[3:]
            b_pk2, r_pk2, bsend, brecv = rest[-4:]
            rest = rest[:-4]
        acc, recv_a, send_b, recv_b, own_ps, lsem, psend, precv, isend, irecv, fsend, frecv = rest
        i = pl.program_id(0)
        kk = pl.program_id(1)
        x, y, c = _pos()
        k = 2 * x + y
        sib = (x, y, 1 - c)
        slot = i % 2

        @pl.when(kk == 0)
        def _():
            acc[slot] = jnp.zeros((2, nr, nc), F32)

        if sm2 is not None:
            @pl.when(jnp.logical_and(kk == 0, i == 0))
            def _():
                b = 4 * x + 2 * y + c
                b_pk2[0:8, :] = _onehot_rows(sm2_ref[0:1, :], b)
                b_pk2[8:16, :] = _onehot_rows(sm2_ref[1:2, :], b)
                b_pk2[16:17, :] = sm2_ref[2:3, :]
                b_pk2[17:24, :] = jnp.zeros((7, D_MODEL), F32)

            for e in range(4):
                @pl.when(jnp.logical_and(kk == 0, i == (e * nblk) // 4))
                def _():
                    _butterfly_event(e, [b_pk2], [r_pk2], bsend, brecv, _all_reduce_partners())

            @pl.when(jnp.logical_and(kk == 0, i == (3 * nblk) // 4))
            def _():
                pk2_ref[...] = b_pk2[...]

        def pair_copy(ii, sl):
            return _rcopy(acc.at[sl, 1 - c], recv_a.at[ii], psend.at[ii], precv.at[ii], sib)

        def pair_sum(ii, sl):
            cp = pair_copy(ii, sl)
            cp.wait_recv()
            cp.wait_send()
            return acc[sl, c] + recv_a[ii]

        def ici_copy(ii, kd, rslot):
            return _rcopy(send_b.at[ii], recv_b.at[rslot], isend.at[ii], irecv.at[rslot], (kd // 2, kd % 2, c))

        def finalize(ii, sl):
            ps = pair_sum(ii, sl)
            s = ii // nj
            j = ii % nj

            @pl.when(s < 3)
            def _():
                send_b[ii] = ps.astype(BF16)
                ici_copy(ii, (k + 1 + s) % N_CHIPS, (2 - s) * nj + j).start()

            @pl.when(s == 3)
            def _():
                own_ps[j] = ps

        @pl.when(jnp.logical_and(kk == kt - 1, i >= 1))
        def _():
            finalize(i - 1, 1 - slot)

        res = _dot_tn(a_ref[...], b_ref[0] if b_lead else b_ref[...])
        acc[slot, 0] += res[0:nr]
        acc[slot, 1] += res[nr:2 * nr]

        @pl.when(kk == kt - 1)
        def _():
            pair_copy(i, slot).start()

            @pl.when(i == nblk - 1)
            def _():
                own_ps[nj - 1] = pair_sum(nblk - 1, (nblk - 1) % 2)
                for r in range(nfor):
                    ici_copy(0, k, r).wait_recv()
                for j in range(nj):
                    tot = own_ps[j]
                    for s in range(3):
                        tot = tot + recv_b[s * nj + j].astype(F32)
                    own_ps[j] = tot
                loc = pltpu.make_async_copy(own_ps, out_ref.at[:, c], lsem)
                loc.start()
                swap = _rcopy(own_ps, out_ref.at[:, c], fsend, frecv, sib)
                swap.start()
                loc.wait()
                swap.wait_send()
                _rcopy(own_ps, out_ref.at[:, 1 - c], fsend, frecv, sib).wait_recv()
                for ii in range(nfor):
                    ici_copy(ii, k, 0).wait_send()

    def seq(i, k_ref):
        return (nj * (k_ref[0] + 1) + i) % nblk

    small = sm2 is not None
    const2 = lambda i, kk, k_ref: (0, 0)
    grid_spec = pltpu.PrefetchScalarGridSpec(
        num_scalar_prefetch=1,
        grid=(nblk, kt),
        in_specs=[a_spec(seq), b_spec(seq)] + ([pl.BlockSpec((8, D_MODEL), const2)] if small else []),
        out_specs=((pl.BlockSpec(memory_space=pl.ANY), pl.BlockSpec((24, D_MODEL), const2)) if small
                   else pl.BlockSpec(memory_space=pl.ANY)),
        scratch_shapes=[pltpu.VMEM((2, 2, nr, nc), F32),
                        pltpu.VMEM((nblk, nr, nc), F32),
                        pltpu.VMEM((nfor, nr, nc), BF16),
                        pltpu.VMEM((nfor, nr, nc), BF16),
                        pltpu.VMEM((nj, nr, nc), F32),
                        pltpu.SemaphoreType.DMA,
                        pltpu.SemaphoreType.DMA((nblk,)), pltpu.SemaphoreType.DMA((nblk,)),
                        pltpu.SemaphoreType.DMA((nfor,)), pltpu.SemaphoreType.DMA((nfor,)),
                        pltpu.SemaphoreType.DMA, pltpu.SemaphoreType.DMA]
        + ([pltpu.VMEM((24, D_MODEL), F32), pltpu.VMEM((3, 24, D_MODEL), F32),
            pltpu.SemaphoreType.DMA((3,)), pltpu.SemaphoreType.DMA((3,))] if small else []))
    grad_shape = jax.ShapeDtypeStruct((nj, 2, nr, nc), F32)
    return pl.pallas_call(
        body, name=name,
        grid_spec=grid_spec,
        out_shape=(grad_shape, jax.ShapeDtypeStruct((24, D_MODEL), F32)) if small else grad_shape,
        compiler_params=_params(dimension_semantics=("arbitrary", "arbitrary")),
    )(*([kidx, a, b] + ([sm2] if small else [])))


def _grad_w_in(kidx, h, dz6, sm2):
    s_len = h.shape[0]
    tk = min(K_TILE, s_len)
    return _grad_reduce_scatter(
        kidx, h, dz6, sm2=sm2, name="grad_w_in", nj=3, nr=512, nc=512, b_lead=True, kt=s_len // tk,
        a_spec=lambda seq: pl.BlockSpec((tk, D_MODEL), lambda i, kk, k_ref: (kk, 0)),
        b_spec=lambda seq: pl.BlockSpec(
            (1, tk, 512), lambda i, kk, k_ref: ((seq(i, k_ref) // 2 + 4) % N_GROUPS, kk, seq(i, k_ref) % 2)))


def _grad_w_out(kidx, ycat, dy):
    s_len = dy.shape[0]
    tk = min(K_TILE, s_len)
    return _grad_reduce_scatter(
        kidx, ycat, dy, name="grad_w_out", nj=1, nr=256, nc=D_MODEL, b_lead=False, kt=s_len // tk,
        a_spec=lambda seq: pl.BlockSpec((tk, 512), lambda i, kk, k_ref: (kk, seq(i, k_ref))),
        b_spec=lambda seq: pl.BlockSpec((tk, D_MODEL), lambda i, kk, k_ref: (kk, 0)))


def _adamw_math(w, g, m, v):
    m = ADAM_B1 * m + (1.0 - ADAM_B1) * g
    v = ADAM_B2 * v + (1.0 - ADAM_B2) * (g * g)
    m_hat = m / (1.0 - ADAM_B1 ** ADAM_STEP)
    v_hat = v / (1.0 - ADAM_B2 ** ADAM_STEP)
    delta = -ADAM_LR * (m_hat / (jnp.sqrt(v_hat) + ADAM_EPS) + ADAM_WD * w)
    return delta, m, v


def _adamw_blocked(w, m, v, g4, name):
    nj, _, nr, nc = g4.shape

    def body(w_ref, m_ref, v_ref, g_ref, go_ref, d_ref, mo_ref, vo_ref):
        g = g_ref[0, 0]
        d, mn, vn = _adamw_math(w_ref[...], g, m_ref[...], v_ref[...])
        go_ref[...] = g
        d_ref[...] = d
        mo_ref[...] = mn
        vo_ref[...] = vn

    blk = pl.BlockSpec((nr, nc), lambda j, hf: (hf, j))
    return pl.pallas_call(
        body, name=name,
        grid=(nj, 2),
        out_shape=tuple(jax.ShapeDtypeStruct(w.shape, F32) for _ in range(4)),
        in_specs=[blk, blk, blk, pl.BlockSpec((1, 1, nr, nc), lambda j, hf: (j, hf, 0, 0))],
        out_specs=(blk, blk, blk, blk),
        compiler_params=_params(dimension_semantics=("arbitrary", "arbitrary")),
    )(w, m, v, g4)


def _adamw_w_ada(cact, pk1, pk2, w, m, v):
    def body(cact_ref, pk1_ref, pk2_ref, w_ref, m_ref, v_ref, g_ref, d_ref, mo_ref, vo_ref, gb_ref, dmod):
        x, y, _ = _pos()
        k = 2 * x + y
        dmod[:, 0:D_MODEL] = pk2_ref[0:8, :]
        dmod[:, D_MODEL:2 * D_MODEL] = pk2_ref[8:16, :]
        dmod[:, 2 * D_MODEL:3 * D_MODEL] = pk1_ref[0:8, :]
        gb_ref[...] = jnp.sum(dmod[...], axis=0, keepdims=True)
        sel = jnp.zeros((8, 768), F32)
        for kk in range(N_CHIPS):
            sel = jnp.where(k == kk, dmod[:, kk * 768:(kk + 1) * 768], sel)
        g = _dot_tn(cact_ref[...].astype(BF16), sel.astype(BF16))
        d, mn, vn = _adamw_math(w_ref[...], g, m_ref[...], v_ref[...])
        g_ref[...] = g
        d_ref[...] = d
        mo_ref[...] = mn
        vo_ref[...] = vn

    return pl.pallas_call(
        body, name="adamw_w_ada",
        out_shape=tuple(jax.ShapeDtypeStruct(w.shape, F32) for _ in range(4)) + (
            jax.ShapeDtypeStruct((1, 3 * D_MODEL), F32),),
        in_specs=[_vmem()] * 6,
        out_specs=tuple(_vmem() for _ in range(5)),
        scratch_shapes=[pltpu.VMEM((8, 3 * D_MODEL), F32)],
        compiler_params=_params(),
    )(cact, pk1, pk2, w, m, v)


def _adamw_small(items):
    n = len(items)

    def body(*refs):
        ins, outs = refs[:4 * n], refs[4 * n:]
        for i in range(n):
            w_ref, g_ref, m_ref, v_ref = ins[4 * i:4 * i + 4]
            d, mn, vn = _adamw_math(w_ref[...], g_ref[...], m_ref[...], v_ref[...])
            outs[3 * i][...] = d
            outs[3 * i + 1][...] = mn
            outs[3 * i + 2][...] = vn

    flat = [a for it in items for a in it]
    outs = pl.pallas_call(
        body, name="adamw_small",
        out_shape=tuple(jax.ShapeDtypeStruct(it[0].shape, F32) for it in items for _ in range(3)),
        in_specs=[_vmem()] * (4 * n),
        out_specs=tuple(_vmem() for _ in range(3 * n)),
        compiler_params=_params(),
    )(*flat)
    return [tuple(outs[3 * i:3 * i + 3]) for i in range(n)]


def kernel(x, c, w_ada, b_ada, norm_g, w_in, conv_w, conv_b, conv_ln_g, conv_ln_b, sg_ln_g, sg_ln_b, w_s, b_s, w_out, final_g, loss_target, m_w_ada, m_b_ada, m_norm_g, m_w_in, m_conv_w, m_conv_b, m_conv_ln_g, m_conv_ln_b, m_sg_ln_g, m_sg_ln_b, m_w_s, m_b_s, m_w_out, m_final_g, v_w_ada, v_b_ada, v_norm_g, v_w_in, v_conv_w, v_conv_b, v_conv_ln_g, v_conv_ln_b, v_sg_ln_g, v_sg_ln_b, v_w_s, v_b_s, v_w_out, v_final_g):
    s_len = x.shape[1]
    x2d = x[0]
    tgt = loss_target[0]
    row = lambda a: a.reshape(1, -1)

    kidx = (2 * lax.axis_index("x") + lax.axis_index("y")).astype(jnp.int32).reshape(1)
    cw_sh = jnp.pad(conv_w.reshape(CONV_WIDTH, 256), ((0, 1), (0, 0)))
    h, z6, mod, cact, cw4, w_in12 = _front(kidx, x2d, c, w_ada[0], b_ada, norm_g, w_in[0], cw_sh)
    w12 = w_in12.reshape(12, D_MODEL, 512)
    q, w_out4 = _conv_fwd(z6, cw4, conv_b, w_out[0])
    w_out_full = w_out4.reshape(2 * D_MODEL, D_MODEL)
    ln_rows = (conv_ln_g, conv_ln_b, sg_ln_g, sg_ln_b, row(final_g))
    bs_exp = jnp.repeat(b_s[0].T, CHUNK, axis=1)
    dz6, dq, dx2, ycat, dy, dws, sm1, dbs = _middle(x2d, z6, q, tgt, ln_rows, mod, w_s[0], bs_exp, w_out_full)
    g_w_out4 = _grad_w_out(kidx, ycat, dy)
    dz6, dcw = _conv_bwd(dq, z6, cw4, dz6)
    grad_x, sm2, pk1, dws_r, dbs_r, dcw_sh, dcb, loss_t = _bwd_in(
        dz6, w12, x2d, dx2, mod, norm_g, sm1, dws.reshape(D_MODEL, CHUNK), dbs, dcw)
    g_w_in4, pk2 = _grad_w_in(kidx, h, dz6, sm2)

    g_w_in, d_w_in, nm_w_in, nv_w_in = _adamw_blocked(w_in[0], m_w_in[0], v_w_in[0], g_w_in4, "adamw_w_in")
    g_w_out, d_w_out, nm_w_out, nv_w_out = _adamw_blocked(w_out[0], m_w_out[0], v_w_out[0], g_w_out4, "adamw_w_out")
    g_w_ada, d_w_ada, nm_w_ada, nv_w_ada, g_b_ada = _adamw_w_ada(cact, pk1, pk2, w_ada[0], m_w_ada[0], v_w_ada[0])

    g_norm_g = pk2[16:17]
    g_cln_g, g_cln_b, g_sln_g, g_sln_b, g_final = (pk1[8 + i:9 + i] for i in range(5))
    loss = loss_t[0, 0]
    g_conv_w = dcw_sh[:CONV_WIDTH]
    g_w_s = dws_r
    g_b_s = dbs_r[:, :HEADS].T
    small = [
        (b_ada, g_b_ada, m_b_ada, v_b_ada),
        (norm_g, g_norm_g, m_norm_g, v_norm_g),
        (conv_w.reshape(CONV_WIDTH, 256), g_conv_w, m_conv_w.reshape(CONV_WIDTH, 256), v_conv_w.reshape(CONV_WIDTH, 256)),
        (conv_b, dcb, m_conv_b, v_conv_b),
        (conv_ln_g, g_cln_g, m_conv_ln_g, v_conv_ln_g),
        (conv_ln_b, g_cln_b, m_conv_ln_b, v_conv_ln_b),
        (sg_ln_g, g_sln_g, m_sg_ln_g, v_sg_ln_g),
        (sg_ln_b, g_sln_b, m_sg_ln_b, v_sg_ln_b),
        (w_s.reshape(D_MODEL, CHUNK), g_w_s, m_w_s.reshape(D_MODEL, CHUNK), v_w_s.reshape(D_MODEL, CHUNK)),
        (b_s[0], g_b_s, m_b_s[0], v_b_s[0]),
        (row(final_g), g_final, row(m_final_g), row(v_final_g)),
    ]
    upd = _adamw_small(small)

    shapes = [w_ada.shape, b_ada.shape, norm_g.shape, w_in.shape, conv_w.shape, conv_b.shape, conv_ln_g.shape,
              conv_ln_b.shape, sg_ln_g.shape, sg_ln_b.shape, w_s.shape, b_s.shape, w_out.shape, final_g.shape]
    grads = [g_w_ada, g_b_ada, g_norm_g, g_w_in, g_conv_w, dcb, g_cln_g, g_cln_b, g_sln_g, g_sln_b, g_w_s, g_b_s,
             g_w_out, g_final]
    big = {0: (d_w_ada, nm_w_ada, nv_w_ada), 3: (d_w_in, nm_w_in, nv_w_in), 12: (d_w_out, nm_w_out, nv_w_out)}
    small_pos = [1, 2, 4, 5, 6, 7, 8, 9, 10, 11, 13]
    trip = [None] * 14
    for i, t in big.items():
        trip[i] = t
    for i, t in zip(small_pos, upd):
        trip[i] = t
    fit = lambda arrs: [a.reshape(s) for a, s in zip(arrs, shapes)]
    return (loss, grad_x.reshape(x.shape), *fit(grads), *fit([t[0] for t in trip]), *fit([t[1] for t in trip]),
            *fit([t[2] for t in trip]))
```

```python
import jax
import jax.numpy as jnp
from jax import lax
from jax.experimental import pallas as pl
from jax.experimental.pallas import tpu as pltpu

F32 = jnp.float32
BF16 = jnp.bfloat16
MESH = pl.DeviceIdType.MESH

D_MODEL = 1024
N_CHIPS = 4
HEADS = 8
CHUNK = 128
CONV_WIDTH = 31
CONV_HALF = CONV_WIDTH // 2
CONV_PAD = 16
EPS = 1e-6
ADAM_LR = 0.001
ADAM_B1 = 0.9
ADAM_B2 = 0.999
ADAM_EPS = 1e-08
ADAM_WD = 0.01
ADAM_STEP = 10

V7X_VMEM_BYTES = 64 * 1024 * 1024
VMEM_LIMIT = V7X_VMEM_BYTES - 8 * 1024 * 1024
ROWS = 16
UNROLL = 8
TOKEN_TILE = 256
TIME_TILE = 128
K_TILE = 2048

N_GROUPS = 6


def _natural_group(j):
    return (j + 2) % N_GROUPS


def _pos():
    return lax.axis_index("x"), lax.axis_index("y"), lax.axis_index("c")


def _rcopy(src, dst, ssem, rsem, dev):
    return pltpu.make_async_remote_copy(src_ref=src, dst_ref=dst, send_sem=ssem, recv_sem=rsem,
                                        device_id=dev, device_id_type=MESH)


def _vmem():
    return pl.BlockSpec(memory_space=pltpu.VMEM)


def _params(**kw):
    return pltpu.CompilerParams(vmem_limit_bytes=VMEM_LIMIT, **kw)


def _sigmoid(v):
    return 0.5 * jnp.tanh(0.5 * v) + 0.5


def _row_loop(n_rows, body, unroll=1):
    def step(r, carry):
        body(pl.ds(pl.multiple_of(r * ROWS, ROWS), ROWS))
        return carry
    lax.fori_loop(0, n_rows // ROWS, step, 0, unroll=unroll)


def _colsum8(v):
    return v.reshape(v.shape[0] // 8, 8, v.shape[1]).sum(axis=0)


def _mean(v):
    return jnp.mean(v, axis=-1, keepdims=True)


def _dot_nn(a, b):
    return jnp.dot(a, b, preferred_element_type=F32)


def _dot_nt(a, b):
    return lax.dot_general(a, b, (((1,), (1,)), ((), ())), preferred_element_type=F32)


def _dot_tn(a, b):
    return lax.dot_general(a, b, (((0,), (0,)), ((), ())), preferred_element_type=F32)


def _remote_chip(k, r):
    return jnp.bitwise_xor(k, r + 1)


def _front(kidx, x, c, w_ada, b_ada, norm_g, w_in, cw):
    s_len = x.shape[0]
    tmh = min(512, s_len)
    tmz = min(1024, s_len)
    nh = s_len // tmh
    nz = s_len // tmz
    n_steps = nh + 12 * nz

    def remote_block(q):
        return jnp.where(q < 6, q % 2, 2), jnp.where(q < 6, q // 2, q - 6)

    def block_of(i, k):
        r, j = remote_block(jnp.maximum(i - 3, 0))
        return jnp.where(i < 3, 3 * k + i, 3 * _remote_chip(k, r) + j)

    def body(k_ref, x_ref, c_ref, bada_ref, g_ref, cw_ref, wada_hbm, win_hbm,
             h_ref, z_ref, mod_ref, cact_ref, cw4_ref, w12_hbm,
             h_all, wbuf, stage, wada_v, cslab, mslab, cw4_s,
             lsem, csend, crecv, msend, mrecv, wsend, wrecv, isend, irecv, fsend, frecv, osem):
        del k_ref
        t = pl.program_id(0)
        x_, y_, c_ = _pos()
        k = 2 * x_ + y_
        b = 4 * x_ + 2 * y_ + c_
        sib = (x_, y_, 1 - c_)

        def dev_of(r):
            kk = _remote_chip(k, r)
            return (kk // 2, kk % 2, c_)

        def ici(q, kk):
            r, j = remote_block(q)
            return _rcopy(wbuf.at[3 * kk + j, c_], wbuf.at[3 * kk + j, c_], isend.at[q], irecv.at[q], dev_of(r))

        def fwd(q, hf):
            r, j = remote_block(q)
            blk = 3 * _remote_chip(k, r) + j
            return _rcopy(wbuf.at[blk, hf], wbuf.at[blk, hf], fsend.at[q], frecv.at[q], sib)

        def c_copy(q, src):
            d = jnp.bitwise_xor(b, q)
            return _rcopy(cslab.at[src], cslab.at[src], csend.at[q - 1], crecv.at[q - 1], (d // 4, (d // 2) % 2, d % 2))

        def m_copy(r, kk):
            return _rcopy(mslab.at[kk], mslab.at[kk], msend.at[r], mrecv.at[r], dev_of(r))

        def cw_copy(r, kk):
            return _rcopy(cw4_s.at[kk], cw4_s.at[kk], wsend.at[r], wrecv.at[r], dev_of(r))

        def to_hbm(i):
            m = block_of(i, k)
            return pltpu.make_async_copy(wbuf.at[m], w12_hbm.at[m], osem.at[i])

        @pl.when(t == 0)
        def _():
            ld_w = pltpu.make_async_copy(win_hbm, stage, lsem.at[0])
            ld_w.start()
            ld_a = pltpu.make_async_copy(wada_hbm, wada_v, lsem.at[1])
            ld_a.start()
            cslab[b] = jnp.broadcast_to(c_ref[...], (8, D_MODEL))
            for q in range(1, 8):
                c_copy(q, b).start()
            cw4_s[k] = cw_ref[...]
            for r in range(3):
                cw_copy(r, k).start()
            ld_w.wait()
            for j in range(3):
                for hf in range(2):
                    wbuf[3 * k + j, hf] = stage[hf * 512:(hf + 1) * 512, j * 512:(j + 1) * 512].astype(BF16)
            for i in range(3):
                to_hbm(i).start()
            for q in range(1, 8):
                c_copy(q, jnp.bitwise_xor(b, q)).wait_recv()
            row = lax.broadcasted_iota(jnp.int32, (8, D_MODEL), 0)
            call = jnp.zeros((8, D_MODEL), F32)
            for d in range(8):
                call = jnp.where(row == d, cslab[d], call)
            cact = call * _sigmoid(call)
            cact_ref[...] = cact
            ld_a.wait()
            mslab[k] = _dot_nn(cact.astype(BF16), wada_v[...].astype(BF16))
            for r in range(3):
                m_copy(r, k).start()
            for q in range(9):
                ici(q, k).start()
            for r in range(3):
                m_copy(r, _remote_chip(k, r)).wait_recv()
            row8 = lax.broadcasted_iota(jnp.int32, (8, 768), 0)
            for kk in range(N_CHIPS):
                piece = jnp.sum(jnp.where(row8 == b, mslab[kk], 0.0), axis=0, keepdims=True)
                mod_ref[:, kk * 768:(kk + 1) * 768] = piece + bada_ref[:, kk * 768:(kk + 1) * 768]
            for r in range(3):
                cw_copy(r, _remote_chip(k, r)).wait_recv()
            cw4_ref[...] = cw4_s[...]

        @pl.when(t < nh)
        def _():
            shift = mod_ref[:, 0:D_MODEL]
            scale1 = 1.0 + mod_ref[:, D_MODEL:2 * D_MODEL]
            g = g_ref[...]
            base = t * tmh

            def rows_fn(rows):
                xt = x_ref[rows, :]
                r = lax.rsqrt(_mean(xt * xt) + EPS)
                hv = ((xt * r * g) * scale1 + shift).astype(BF16)
                h_ref[rows, :] = hv
                h_all[pl.ds(pl.multiple_of(base + rows.start, ROWS), ROWS), :] = hv
            _row_loop(tmh, rows_fn, unroll=UNROLL)

        @pl.when(t >= nh)
        def _():
            u = t - nh
            i = u // nz
            rt = u % nz
            @pl.when(jnp.logical_and(rt == 0, jnp.logical_and(i >= 2, i <= 10)))
            def _():
                q = i - 2
                r, _ = remote_block(q)
                ici(q, _remote_chip(k, r)).wait_recv()
                fwd(q, c_).start()

            @pl.when(jnp.logical_and(rt == 0, i >= 3))
            def _():
                fwd(i - 3, 1 - c_).wait_recv()
                to_hbm(i).start()
            m = block_of(i, k)
            hb = h_all[pl.ds(pl.multiple_of(rt * tmz, tmz), tmz), :]
            z_ref[0] = _dot_nn(hb[:, 0:512], wbuf[m, 0]) + _dot_nn(hb[:, 512:1024], wbuf[m, 1])

        @pl.when(t == n_steps - 1)
        def _():
            for q in range(1, 8):
                c_copy(q, b).wait_send()
            for r in range(3):
                m_copy(r, k).wait_send()
                cw_copy(r, k).wait_send()
            for q in range(9):
                ici(q, k).wait_send()
                fwd(q, c_).wait_send()
            for i in range(12):
                to_hbm(i).wait()

    def z_index(t, k_ref):
        u = jnp.maximum(t - nh, 0)
        m = block_of(u // nz, k_ref[0])
        return ((m // 2 + 4) % N_GROUPS, u % nz, m % 2)

    tok = lambda t, k_ref: (jnp.minimum(t, nh - 1), 0)
    const2 = lambda t, k_ref: (0, 0)
    grid_spec = pltpu.PrefetchScalarGridSpec(
        num_scalar_prefetch=1,
        grid=(n_steps,),
        in_specs=[pl.BlockSpec((tmh, D_MODEL), tok),
                  pl.BlockSpec((1, D_MODEL), const2),
                  pl.BlockSpec((1, 3 * D_MODEL), const2),
                  pl.BlockSpec((1, D_MODEL), const2),
                  pl.BlockSpec((32, 256), const2),
                  pl.BlockSpec(memory_space=pl.ANY),
                  pl.BlockSpec(memory_space=pl.ANY)],
        out_specs=(pl.BlockSpec((tmh, D_MODEL), tok),
                   pl.BlockSpec((1, tmz, 512), z_index),
                   pl.BlockSpec((1, 3 * D_MODEL), const2),
                   pl.BlockSpec((8, D_MODEL), const2),
                   pl.BlockSpec((N_CHIPS, 32, 256), lambda t, k_ref: (0, 0, 0)),
                   pl.BlockSpec(memory_space=pl.ANY)),
        scratch_shapes=[pltpu.VMEM((s_len, D_MODEL), BF16),
                        pltpu.VMEM((12, 2, 512, 512), BF16),
                        pltpu.VMEM((D_MODEL, 1536), F32),
                        pltpu.VMEM((D_MODEL, 768), F32),
                        pltpu.VMEM((8, 8, D_MODEL), F32),
                        pltpu.VMEM((N_CHIPS, 8, 768), F32),
                        pltpu.VMEM((N_CHIPS, 32, 256), F32),
                        pltpu.SemaphoreType.DMA((2,)),
                        pltpu.SemaphoreType.DMA((7,)), pltpu.SemaphoreType.DMA((7,)),
                        pltpu.SemaphoreType.DMA((3,)), pltpu.SemaphoreType.DMA((3,)),
                        pltpu.SemaphoreType.DMA((3,)), pltpu.SemaphoreType.DMA((3,)),
                        pltpu.SemaphoreType.DMA((9,)), pltpu.SemaphoreType.DMA((9,)),
                        pltpu.SemaphoreType.DMA((9,)), pltpu.SemaphoreType.DMA((9,)),
                        pltpu.SemaphoreType.DMA((12,))])
    return pl.pallas_call(
        body, name="front",
        grid_spec=grid_spec,
        out_shape=(jax.ShapeDtypeStruct((s_len, D_MODEL), BF16),
                   jax.ShapeDtypeStruct((N_GROUPS, s_len, D_MODEL), F32),
                   jax.ShapeDtypeStruct((1, 3 * D_MODEL), F32),
                   jax.ShapeDtypeStruct((8, D_MODEL), F32),
                   jax.ShapeDtypeStruct((N_CHIPS, 32, 256), F32),
                   jax.ShapeDtypeStruct((12, 2, 512, 512), BF16)),
        compiler_params=_params(dimension_semantics=("arbitrary",)),
    )(kidx, x, c, b_ada, norm_g, cw, w_ada, w_in)


def _all_reduce_partners():
    x, y, c = _pos()
    return [(x, y, 1 - c), (x, 1 - y, c), (1 - x, y, c)]


def _butterfly_event(e, bufs, recvs, ssem, rsem, partners):
    n = len(bufs)

    def copies(s):
        return [_rcopy(buf, recv.at[s], ssem.at[s * n + i], rsem.at[s * n + i], partners[s])
                for i, (buf, recv) in enumerate(zip(bufs, recvs))]

    if e > 0:
        for cp in copies(e - 1):
            cp.wait()
        for buf, recv in zip(bufs, recvs):
            buf[...] = buf[...] + recv[e - 1]
    if e < len(partners):
        for cp in copies(e):
            cp.start()


def _onehot_rows(v, b):
    row = lax.broadcasted_iota(jnp.int32, (8, v.shape[1]), 0)
    return jnp.where(row == b, jnp.broadcast_to(v, (8, v.shape[1])), 0.0)


def _conv_fwd(z6, cw4, conv_b, w_out):
    s_len = z6.shape[1]
    tt = TIME_TILE
    n_blocks = D_MODEL // 128

    def body(z_ref, cw_ref, cb_ref, wout_hbm, q_ref, wout4_hbm, ppad, stage, wbuf, lsem, isend, irecv, fsend, frecv, osem):
        jb = pl.program_id(0)
        x_, y_, c_ = _pos()
        k = 2 * x_ + y_
        sib = (x_, y_, 1 - c_)

        def ici(r, kk):
            rk = _remote_chip(k, r)
            return _rcopy(wbuf.at[kk, c_], wbuf.at[kk, c_], isend.at[r], irecv.at[r], (rk // 2, rk % 2, c_))

        def fwd(r, hf):
            kk = _remote_chip(k, r)
            return _rcopy(wbuf.at[kk, hf], wbuf.at[kk, hf], fsend.at[r], frecv.at[r], sib)

        @pl.when(jb == 0)
        def _():
            ld = pltpu.make_async_copy(wout_hbm, stage, lsem)
            ld.start()
            ld.wait()
            for hf in range(2):
                wbuf[k, hf] = stage[hf * 256:(hf + 1) * 256, :].astype(BF16)
            for r in range(3):
                ici(r, k).start()

        @pl.when(jb == n_blocks // 2)
        def _():
            for r in range(3):
                ici(r, _remote_chip(k, r)).wait_recv()
                fwd(r, c_).start()

        zero = jnp.zeros((CONV_PAD, 128), F32)
        ppad[0:CONV_PAD, :] = zero
        ppad[s_len + CONV_PAD:s_len + 2 * CONV_PAD, :] = zero

        def fill(i, carry):
            t0 = pl.multiple_of(i * tt, tt)
            ppad[pl.ds(CONV_PAD + t0, tt), :] = z_ref[0, pl.ds(t0, tt), :] * _sigmoid(z_ref[1, pl.ds(t0, tt), :])
            return carry
        lax.fori_loop(0, s_len // tt, fill, 0)
        w = cw_ref[0]
        bias = cb_ref[...]

        def conv(i, carry):
            t0 = pl.multiple_of(i * tt, tt)
            acc = jnp.broadcast_to(bias, (tt, 128))
            for o in range(1, CONV_WIDTH + 1):
                acc = acc + w[o - 1:o, :] * ppad[pl.ds(t0 + o, tt), :]
            q_ref[pl.ds(t0, tt), :] = acc
            return carry
        lax.fori_loop(0, s_len // tt, conv, 0)

        @pl.when(jb == n_blocks - 1)
        def _():
            for r in range(3):
                fwd(r, 1 - c_).wait_recv()
            out = pltpu.make_async_copy(wbuf, wout4_hbm, osem)
            out.start()
            for r in range(3):
                ici(r, k).wait_send()
                fwd(r, c_).wait_send()
            out.wait()

    return pl.pallas_call(
        body, name="conv_fwd",
        grid=(n_blocks,),
        out_shape=(jax.ShapeDtypeStruct((s_len, D_MODEL), F32), jax.ShapeDtypeStruct((N_CHIPS, 2, 256, D_MODEL), BF16)),
        in_specs=[pl.BlockSpec((2, s_len, 128), lambda j: (2, 0, j)),
                  pl.BlockSpec((1, 32, 128), lambda j: (j // 2, 0, j % 2)),
                  pl.BlockSpec((1, 128), lambda j: (0, j)),
                  pl.BlockSpec(memory_space=pl.ANY)],
        out_specs=(pl.BlockSpec((s_len, 128), lambda j: (0, j)), pl.BlockSpec(memory_space=pl.ANY)),
        scratch_shapes=[pltpu.VMEM((s_len + 2 * CONV_PAD, 128), F32),
                        pltpu.VMEM((512, D_MODEL), F32),
                        pltpu.VMEM((N_CHIPS, 2, 256, D_MODEL), BF16),
                        pltpu.SemaphoreType.DMA,
                        pltpu.SemaphoreType.DMA((3,)), pltpu.SemaphoreType.DMA((3,)),
                        pltpu.SemaphoreType.DMA((3,)), pltpu.SemaphoreType.DMA((3,)),
                        pltpu.SemaphoreType.DMA],
        compiler_params=_params(dimension_semantics=("arbitrary",)),
    )(z6, cw4, conv_b, w_out)


def _middle(x, z6, q, target, ln_rows, mod, w_s, bs_exp, w_out):
    s_len = x.shape[0]
    tm = TOKEN_TILE
    n_steps = s_len // tm
    n_chunks = tm // CHUNK
    inv_d = 1.0 / D_MODEL

    def body(x_ref, z_ref, q_ref, tgt_ref, cg_ref, cb_ref, sg_ref, sb_ref, fg_ref, mod_ref, ws_ref, bs_ref, wout_ref,
             dz_ref, dq_ref, dx2_ref, ycat_ref, dy_ref, dws_ref, sm_ref, dbs_ref,
             vl_scr, vm_scr, y_scr, dycat_scr, dvm_scr, dvl_scr, acc_scr, dbs_acc):
        i = pl.program_id(0)

        @pl.when(i == 0)
        def _():
            acc_scr[...] = jnp.zeros_like(acc_scr)
            dbs_acc[...] = jnp.zeros_like(dbs_acc)
            dws_ref[...] = jnp.zeros_like(dws_ref)

        cg, cb, sg, sb, fg = cg_ref[...], cb_ref[...], sg_ref[...], sb_ref[...], fg_ref[...]
        gm = mod_ref[:, 2 * D_MODEL:3 * D_MODEL]

        def norm_stats(t):
            c = t - _mean(t)
            rstd = lax.rsqrt(_mean(c * c) + EPS)
            return c * rstd, rstd

        def phase1(rows):
            qhat, _ = norm_stats(q_ref[rows, :])
            ln = qhat * cg + cb
            gz = z_ref[0, rows, :]
            ycat_ref[rows, 0:D_MODEL] = ((ln * _sigmoid(ln)) * (gz * _sigmoid(gz))).astype(BF16)
            vhat, _ = norm_stats(z_ref[2, rows, :])
            vl_scr[rows, :] = (vhat * sg + sb).astype(BF16)
        _row_loop(tm, phase1, unroll=UNROLL)

        for ch in range(n_chunks):
            r0 = ch * CHUNK
            for h in range(HEADS):
                c0 = h * CHUNK
                vm_scr[r0:r0 + CHUNK, c0:c0 + CHUNK] = (
                    _dot_nn(ws_ref[h].astype(BF16), vl_scr[r0:r0 + CHUNK, c0:c0 + CHUNK]) + bs_ref[:, c0:c0 + CHUNK])

        def phase3(rows):
            bg = z_ref[3, rows, :]
            ycat_ref[rows, D_MODEL:2 * D_MODEL] = (z_ref[1, rows, :] * vm_scr[rows, :] * (bg * _sigmoid(bg))).astype(BF16)
        _row_loop(tm, phase3)

        y_scr[...] = _dot_nn(ycat_ref[...], wout_ref[...])

        def phase5(rows):
            y = y_scr[rows, :]
            x2 = x_ref[rows, :] + gm * y
            r2 = lax.rsqrt(_mean(x2 * x2) + EPS)
            xn2 = x2 * r2
            diff = xn2 * fg - tgt_ref[rows, :]
            acc_scr[6] += _colsum8(diff * diff)
            dout = diff * inv_d
            acc_scr[0] += _colsum8(dout * xn2)
            dxn = dout * fg
            dx2 = r2 * (dxn - xn2 * _mean(dxn * xn2))
            dx2_ref[rows, :] = dx2
            acc_scr[1] += _colsum8(dx2 * y)
            dy_ref[rows, :] = (dx2 * gm).astype(BF16)
        _row_loop(tm, phase5, unroll=UNROLL)

        dycat_scr[...] = _dot_nt(dy_ref[...], wout_ref[...])

        def phase7(rows):
            dyb = dycat_scr[rows, D_MODEL:2 * D_MODEL]
            u = z_ref[1, rows, :]
            bg = z_ref[3, rows, :]
            vm = vm_scr[rows, :]
            sig = _sigmoid(bg)
            silu = bg * sig
            dz_ref[1, rows, :] = (dyb * vm * silu).astype(BF16)
            dvm = dyb * u * silu
            dz_ref[3, rows, :] = (dyb * u * vm * (sig * (1.0 + bg * (1.0 - sig)))).astype(BF16)
            dvm_scr[rows, :] = dvm.astype(BF16)
            pos = pl.ds(pl.multiple_of(rows.start % CHUNK, ROWS), ROWS)
            dbs_acc[pos, :] += dvm
        _row_loop(tm, phase7)

        for ch in range(n_chunks):
            r0 = ch * CHUNK
            for h in range(HEADS):
                c0 = h * CHUNK
                dvm_b = dvm_scr[r0:r0 + CHUNK, c0:c0 + CHUNK]
                dws_ref[h] += _dot_nt(dvm_b, vl_scr[r0:r0 + CHUNK, c0:c0 + CHUNK])
                dvl_scr[r0:r0 + CHUNK, c0:c0 + CHUNK] = _dot_tn(ws_ref[h].astype(BF16), dvm_b)

        def phase9(rows):
            vhat, rstd_v = norm_stats(z_ref[2, rows, :])
            dvl = dvl_scr[rows, :]
            acc_scr[4] += _colsum8(dvl * vhat)
            acc_scr[5] += _colsum8(dvl)
            dvh = dvl * sg
            dz_ref[2, rows, :] = (rstd_v * (dvh - _mean(dvh) - vhat * _mean(dvh * vhat))).astype(BF16)
            qhat, rstd_q = norm_stats(q_ref[rows, :])
            ln = qhat * cg + cb
            sig_ln = _sigmoid(ln)
            gz = z_ref[0, rows, :]
            sig_g = _sigmoid(gz)
            dya = dycat_scr[rows, 0:D_MODEL]
            dz_ref[0, rows, :] = (dya * (ln * sig_ln) * (sig_g * (1.0 + gz * (1.0 - sig_g)))).astype(BF16)
            dln = (dya * (gz * sig_g)) * (sig_ln * (1.0 + ln * (1.0 - sig_ln)))
            acc_scr[2] += _colsum8(dln * qhat)
            acc_scr[3] += _colsum8(dln)
            dqh = dln * cg
            dq_ref[rows, :] = rstd_q * (dqh - _mean(dqh) - qhat * _mean(dqh * qhat))
        _row_loop(tm, phase9, unroll=UNROLL)

        @pl.when(i == n_steps - 1)
        def _():
            for qi in range(8):
                scale = 0.5 * inv_d if qi == 6 else 1.0
                sm_ref[qi:qi + 1, :] = jnp.sum(acc_scr[qi], axis=0, keepdims=True) * scale
            lane = lax.broadcasted_iota(jnp.int32, (CHUNK, CHUNK), 1)
            tile = jnp.zeros((CHUNK, CHUNK), F32)
            for h in range(HEADS):
                col = jnp.sum(dbs_acc[:, h * CHUNK:(h + 1) * CHUNK], axis=1, keepdims=True)
                tile = jnp.where(lane == h, col, tile)
            dbs_ref[...] = tile

    tok = lambda i: (i, 0)
    const2 = lambda i: (0, 0)
    return pl.pallas_call(
        body, name="middle",
        grid=(n_steps,),
        out_shape=(jax.ShapeDtypeStruct((N_GROUPS, s_len, D_MODEL), BF16),
                   jax.ShapeDtypeStruct((s_len, D_MODEL), F32),
                   jax.ShapeDtypeStruct((s_len, D_MODEL), F32),
                   jax.ShapeDtypeStruct((s_len, 2 * D_MODEL), BF16),
                   jax.ShapeDtypeStruct((s_len, D_MODEL), BF16),
                   jax.ShapeDtypeStruct((HEADS, CHUNK, CHUNK), F32),
                   jax.ShapeDtypeStruct((8, D_MODEL), F32),
                   jax.ShapeDtypeStruct((CHUNK, CHUNK), F32)),
        in_specs=[pl.BlockSpec((tm, D_MODEL), tok),
                  pl.BlockSpec((4, tm, D_MODEL), lambda i: (0, i, 0)),
                  pl.BlockSpec((tm, D_MODEL), tok),
                  pl.BlockSpec((tm, D_MODEL), tok),
                  *[pl.BlockSpec((1, D_MODEL), const2) for _ in range(5)],
                  pl.BlockSpec((1, 3 * D_MODEL), const2),
                  pl.BlockSpec((HEADS, CHUNK, CHUNK), lambda i: (0, 0, 0)),
                  pl.BlockSpec((CHUNK, D_MODEL), const2),
                  pl.BlockSpec((2 * D_MODEL, D_MODEL), const2, pipeline_mode=pl.Buffered(1))],
        out_specs=(pl.BlockSpec((4, tm, D_MODEL), lambda i: (0, i, 0)),
                   pl.BlockSpec((tm, D_MODEL), tok),
                   pl.BlockSpec((tm, D_MODEL), tok),
                   pl.BlockSpec((tm, 2 * D_MODEL), tok),
                   pl.BlockSpec((tm, D_MODEL), tok),
                   pl.BlockSpec((HEADS, CHUNK, CHUNK), lambda i: (0, 0, 0)),
                   pl.BlockSpec((8, D_MODEL), const2),
                   pl.BlockSpec((CHUNK, CHUNK), const2)),
        scratch_shapes=[pltpu.VMEM((tm, D_MODEL), BF16),
                        pltpu.VMEM((tm, D_MODEL), F32),
                        pltpu.VMEM((tm, D_MODEL), F32),
                        pltpu.VMEM((tm, 2 * D_MODEL), F32),
                        pltpu.VMEM((tm, D_MODEL), BF16),
                        pltpu.VMEM((tm, D_MODEL), F32),
                        pltpu.VMEM((8, 8, D_MODEL), F32),
                        pltpu.VMEM((CHUNK, D_MODEL), F32)],
        compiler_params=_params(dimension_semantics=("arbitrary",)),
    )(x, z6, q, target, *ln_rows, mod, w_s, bs_exp, w_out)


def _conv_bwd(dq, z6, cw4, dz6, sm1, dws, dbs):
    s_len = dq.shape[0]
    tt = TIME_TILE
    n_blocks = D_MODEL // 128
    shp = [(16, D_MODEL), (D_MODEL, 128), (128, 128)]

    def body(dq_ref, z_ref, cw_ref, dz_in, sm1_ref, dws_in, dbs_in,
             dz_ref, dcw_ref, pk1_ref, dws_ref, dbs_ref, loss_ref,
             dqpad, ppad, wacc, b_pk, b_dws, b_dbs, r_pk, r_dws, r_dbs, ssem, rsem):
        del dz_in
        jb = pl.program_id(0)
        x_, y_, c_ = _pos()

        @pl.when(jb == 0)
        def _():
            b_pk[0:8, :] = _onehot_rows(sm1_ref[1:2, :], 4 * x_ + 2 * y_ + c_)
            for r, src in enumerate([2, 3, 4, 5, 0, 6]):
                b_pk[8 + r:9 + r, :] = sm1_ref[src:src + 1, :]
            b_pk[14:16, :] = jnp.zeros((2, D_MODEL), F32)
            b_dws[...] = dws_in[...]
            b_dbs[...] = dbs_in[...]

        for e in range(4):
            @pl.when(jb == (e * n_blocks) // 4)
            def _():
                _butterfly_event(e, [b_pk, b_dws, b_dbs], [r_pk, r_dws, r_dbs], ssem, rsem, _all_reduce_partners())

        @pl.when(jb == (3 * n_blocks) // 4)
        def _():
            pk1_ref[...] = b_pk[...]
            dws_ref[...] = b_dws[...]
            dbs_ref[...] = b_dbs[...]
            loss_ref[...] = jnp.broadcast_to(jnp.sum(b_pk[13:14, :], axis=1, keepdims=True), (8, 128))

        zero = jnp.zeros((CONV_PAD, 128), F32)
        for pad in (dqpad, ppad):
            pad[0:CONV_PAD, :] = zero
            pad[s_len + CONV_PAD:s_len + 2 * CONV_PAD, :] = zero
        wacc[...] = jnp.zeros_like(wacc)

        def fill(i, carry):
            t0 = pl.multiple_of(i * tt, tt)
            ppad[pl.ds(CONV_PAD + t0, tt), :] = z_ref[0, pl.ds(t0, tt), :] * _sigmoid(z_ref[1, pl.ds(t0, tt), :])
            dqpad[pl.ds(CONV_PAD + t0, tt), :] = dq_ref[pl.ds(t0, tt), :]
            return carry
        lax.fori_loop(0, s_len // tt, fill, 0)
        w = cw_ref[0]

        def bwd(i, carry):
            t0 = pl.multiple_of(i * tt, tt)
            dp = jnp.zeros((tt, 128), F32)
            for o in range(1, CONV_WIDTH + 1):
                dp = dp + w[CONV_WIDTH - o:CONV_WIDTH - o + 1, :] * dqpad[pl.ds(t0 + o, tt), :]
            a = z_ref[0, pl.ds(t0, tt), :]
            sig = _sigmoid(z_ref[1, pl.ds(t0, tt), :])
            dz_ref[0, pl.ds(t0, tt), :] = (dp * sig).astype(BF16)
            dz_ref[1, pl.ds(t0, tt), :] = (dp * a * (sig * (1.0 - sig))).astype(BF16)
            dqt = dq_ref[pl.ds(t0, tt), :]
            for o in range(1, CONV_WIDTH + 1):
                wacc[o - 1] += _colsum8(dqt * ppad[pl.ds(t0 + o, tt), :])
            wacc[CONV_WIDTH] += _colsum8(dqt)
            return carry
        lax.fori_loop(0, s_len // tt, bwd, 0)
        for k in range(32):
            dcw_ref[k:k + 1, :] = jnp.sum(wacc[k], axis=0, keepdims=True)

    const2 = lambda j: (0, 0)
    return pl.pallas_call(
        body, name="conv_bwd",
        grid=(n_blocks,),
        out_shape=(jax.ShapeDtypeStruct(dz6.shape, BF16), jax.ShapeDtypeStruct((32, D_MODEL), F32))
        + tuple(jax.ShapeDtypeStruct(s, F32) for s in shp) + (jax.ShapeDtypeStruct((8, 128), F32),),
        in_specs=[pl.BlockSpec((s_len, 128), lambda j: (0, j)),
                  pl.BlockSpec((2, s_len, 128), lambda j: (2, 0, j)),
                  pl.BlockSpec((1, 32, 128), lambda j: (j // 2, 0, j % 2)),
                  pl.BlockSpec(memory_space=pl.ANY),
                  pl.BlockSpec((8, D_MODEL), const2),
                  pl.BlockSpec((D_MODEL, 128), const2),
                  pl.BlockSpec((128, 128), const2)],
        out_specs=(pl.BlockSpec((2, s_len, 128), lambda j: (2, 0, j)),
                   pl.BlockSpec((32, 128), lambda j: (0, j)))
        + tuple(pl.BlockSpec(s, const2) for s in shp) + (pl.BlockSpec((8, 128), const2),),
        scratch_shapes=[pltpu.VMEM((s_len + 2 * CONV_PAD, 128), F32), pltpu.VMEM((s_len + 2 * CONV_PAD, 128), F32),
                        pltpu.VMEM((32, 8, 128), F32)]
        + [pltpu.VMEM(s, F32) for s in shp] + [pltpu.VMEM((3,) + s, F32) for s in shp]
        + [pltpu.SemaphoreType.DMA((9,)), pltpu.SemaphoreType.DMA((9,))],
        input_output_aliases={3: 0},
        compiler_params=_params(dimension_semantics=("arbitrary",)),
    )(dq, z6, cw4, dz6, sm1, dws, dbs)


def _bwd_in(dz6, w12, x, dx2, mod, norm_g):
    s_len = x.shape[0]
    tm = TOKEN_TILE
    n_steps = s_len // tm

    def body(dz_ref, w_ref, x_ref, dx2_ref, mod_ref, g_ref, gx_ref, sm_ref, dh_scr, acc_scr):
        i = pl.program_id(0)

        @pl.when(i == 0)
        def _():
            acc_scr[...] = jnp.zeros_like(acc_scr)

        dh = jnp.zeros((tm, D_MODEL), F32)
        for j in range(N_GROUPS):
            n = _natural_group(j)
            for hf in range(2):
                dh = dh + _dot_nt(dz_ref[j, :, hf * 512:(hf + 1) * 512], w_ref[2 * n + hf])
        dh_scr[...] = dh
        scale1 = 1.0 + mod_ref[:, D_MODEL:2 * D_MODEL]
        g = g_ref[...]

        def rows_fn(rows):
            xt = x_ref[rows, :]
            r = lax.rsqrt(_mean(xt * xt) + EPS)
            xn = xt * r
            dhr = dh_scr[rows, :]
            acc_scr[0] += _colsum8(dhr)
            acc_scr[1] += _colsum8(dhr * (xn * g))
            acc_scr[2] += _colsum8(dhr * scale1 * xn)
            dxn = dhr * (g * scale1)
            gx_ref[rows, :] = dx2_ref[rows, :] + r * (dxn - xn * _mean(dxn * xn))
        _row_loop(tm, rows_fn, unroll=UNROLL)

        @pl.when(i == n_steps - 1)
        def _():
            for qi in range(8):
                sm_ref[qi:qi + 1, :] = jnp.sum(acc_scr[qi], axis=0, keepdims=True)

    tok = lambda i: (i, 0)
    const2 = lambda i: (0, 0)
    return pl.pallas_call(
        body, name="bwd_in",
        grid=(n_steps,),
        out_shape=(jax.ShapeDtypeStruct((s_len, D_MODEL), F32), jax.ShapeDtypeStruct((8, D_MODEL), F32)),
        in_specs=[pl.BlockSpec((N_GROUPS, tm, D_MODEL), lambda i: (0, i, 0)),
                  pl.BlockSpec((12, D_MODEL, 512), lambda i: (0, 0, 0), pipeline_mode=pl.Buffered(1)),
                  pl.BlockSpec((tm, D_MODEL), tok),
                  pl.BlockSpec((tm, D_MODEL), tok),
                  pl.BlockSpec((1, 3 * D_MODEL), const2),
                  pl.BlockSpec((1, D_MODEL), const2)],
        out_specs=(pl.BlockSpec((tm, D_MODEL), tok), pl.BlockSpec((8, D_MODEL), const2)),
        scratch_shapes=[pltpu.VMEM((tm, D_MODEL), F32), pltpu.VMEM((8, 8, D_MODEL), F32)],
        compiler_params=_params(dimension_semantics=("arbitrary",)),
    )(dz6, w12, x, dx2, mod, norm_g)


def _grad_reduce_scatter(kidx, a, b, *, name, nj, nr, nc, a_spec, b_spec, b_lead, kt, small=None):
    nblk = N_CHIPS * nj
    nfor = 3 * nj

    def body(k_ref, a_ref, b_ref, *rest):
        del k_ref
        if small is None:
            out_ref, rest = rest[0], rest[1:]
        else:
            sm2_ref, dcw_in, out_ref, pk2_ref, dcwsh_ref, dcb_ref = rest[:6]
            b_pk2, b_dcw, r_pk2, r_dcw, bsend, brecv = rest[-6:]
            rest = rest[6:-6]
        acc, recv_a, send_b, recv_b, own_ps, lsem, psend, precv, isend, irecv, fsend, frecv = rest
        i = pl.program_id(0)
        kk = pl.program_id(1)
        x, y, c = _pos()
        k = 2 * x + y
        sib = (x, y, 1 - c)
        slot = i % 2

        @pl.when(kk == 0)
        def _():
            acc[slot] = jnp.zeros((2, nr, nc), F32)

        if small is not None:
            @pl.when(jnp.logical_and(kk == 0, i == 0))
            def _():
                b = 4 * x + 2 * y + c
                b_pk2[0:8, :] = _onehot_rows(sm2_ref[0:1, :], b)
                b_pk2[8:16, :] = _onehot_rows(sm2_ref[1:2, :], b)
                b_pk2[16:17, :] = sm2_ref[2:3, :]
                b_pk2[17:24, :] = jnp.zeros((7, D_MODEL), F32)
                b_dcw[...] = dcw_in[...]

            for e in range(4):
                @pl.when(jnp.logical_and(kk == 0, i == (e * nblk) // 4))
                def _():
                    _butterfly_event(e, [b_pk2, b_dcw], [r_pk2, r_dcw], bsend, brecv, _all_reduce_partners())

            @pl.when(jnp.logical_and(kk == 0, i == (3 * nblk) // 4))
            def _():
                pk2_ref[...] = b_pk2[...]
                sel = jnp.zeros((32, 256), F32)
                for kc in range(N_CHIPS):
                    sel = jnp.where(k == kc, b_dcw[:, kc * 256:(kc + 1) * 256], sel)
                dcwsh_ref[...] = sel
                dcb_ref[...] = b_dcw[31:32, :]

        def pair_copy(ii, sl):
            return _rcopy(acc.at[sl, 1 - c], recv_a.at[ii], psend.at[ii], precv.at[ii], sib)

        def pair_sum(ii, sl):
            cp = pair_copy(ii, sl)
            cp.wait_recv()
            cp.wait_send()
            return acc[sl, c] + recv_a[ii]

        def ici_copy(ii, kd, rslot):
            return _rcopy(send_b.at[ii], recv_b.at[rslot], isend.at[ii], irecv.at[rslot], (kd // 2, kd % 2, c))

        def finalize(ii, sl):
            ps = pair_sum(ii, sl)
            s = ii // nj
            j = ii % nj

            @pl.when(s < 3)
            def _():
                send_b[ii] = ps.astype(BF16)
                ici_copy(ii, (k + 1 + s) % N_CHIPS, (2 - s) * nj + j).start()

            @pl.when(s == 3)
            def _():
                own_ps[j] = ps

        @pl.when(jnp.logical_and(kk == kt - 1, i >= 1))
        def _():
            finalize(i - 1, 1 - slot)

        res = _dot_tn(a_ref[...], b_ref[0] if b_lead else b_ref[...])
        acc[slot, 0] += res[0:nr]
        acc[slot, 1] += res[nr:2 * nr]

        @pl.when(kk == kt - 1)
        def _():
            pair_copy(i, slot).start()

            @pl.when(i == nblk - 1)
            def _():
                own_ps[nj - 1] = pair_sum(nblk - 1, (nblk - 1) % 2)
                for r in range(nfor):
                    ici_copy(0, k, r).wait_recv()
                for j in range(nj):
                    tot = own_ps[j]
                    for s in range(3):
                        tot = tot + recv_b[s * nj + j].astype(F32)
                    own_ps[j] = tot
                loc = pltpu.make_async_copy(own_ps, out_ref.at[:, c], lsem)
                loc.start()
                swap = _rcopy(own_ps, out_ref.at[:, c], fsend, frecv, sib)
                swap.start()
                loc.wait()
                swap.wait_send()
                _rcopy(own_ps, out_ref.at[:, 1 - c], fsend, frecv, sib).wait_recv()
                for ii in range(nfor):
                    ici_copy(ii, k, 0).wait_send()

    def seq(i, k_ref):
        return (nj * (k_ref[0] + 1) + i) % nblk

    const2 = lambda i, kk, k_ref: (0, 0)
    small_in = [(8, D_MODEL), (32, D_MODEL)]
    small_out = [(24, D_MODEL), (32, 256), (1, D_MODEL)]
    small_buf = [(24, D_MODEL), (32, D_MODEL)]
    grid_spec = pltpu.PrefetchScalarGridSpec(
        num_scalar_prefetch=1,
        grid=(nblk, kt),
        in_specs=[a_spec(seq), b_spec(seq)] + ([pl.BlockSpec(s, const2) for s in small_in] if small else []),
        out_specs=((pl.BlockSpec(memory_space=pl.ANY),) + tuple(pl.BlockSpec(s, const2) for s in small_out) if small
                   else pl.BlockSpec(memory_space=pl.ANY)),
        scratch_shapes=[pltpu.VMEM((2, 2, nr, nc), F32),
                        pltpu.VMEM((nblk, nr, nc), F32),
                        pltpu.VMEM((nfor, nr, nc), BF16),
                        pltpu.VMEM((nfor, nr, nc), BF16),
                        pltpu.VMEM((nj, nr, nc), F32),
                        pltpu.SemaphoreType.DMA,
                        pltpu.SemaphoreType.DMA((nblk,)), pltpu.SemaphoreType.DMA((nblk,)),
                        pltpu.SemaphoreType.DMA((nfor,)), pltpu.SemaphoreType.DMA((nfor,)),
                        pltpu.SemaphoreType.DMA, pltpu.SemaphoreType.DMA]
        + ([pltpu.VMEM(s, F32) for s in small_buf] + [pltpu.VMEM((3,) + s, F32) for s in small_buf]
           + [pltpu.SemaphoreType.DMA((6,)), pltpu.SemaphoreType.DMA((6,))] if small else []))
    grad_shape = jax.ShapeDtypeStruct((nj, 2, nr, nc), F32)
    return pl.pallas_call(
        body, name=name,
        grid_spec=grid_spec,
        out_shape=((grad_shape,) + tuple(jax.ShapeDtypeStruct(s, F32) for s in small_out)) if small else grad_shape,
        compiler_params=_params(dimension_semantics=("arbitrary", "arbitrary")),
    )(*([kidx, a, b] + (list(small) if small else [])))


def _grad_w_in(kidx, h, dz6, sm2, dcw):
    s_len = h.shape[0]
    tk = min(K_TILE, s_len)
    return _grad_reduce_scatter(
        kidx, h, dz6, small=(sm2, dcw), name="grad_w_in", nj=3, nr=512, nc=512, b_lead=True, kt=s_len // tk,
        a_spec=lambda seq: pl.BlockSpec((tk, D_MODEL), lambda i, kk, k_ref: (kk, 0)),
        b_spec=lambda seq: pl.BlockSpec(
            (1, tk, 512), lambda i, kk, k_ref: ((seq(i, k_ref) // 2 + 4) % N_GROUPS, kk, seq(i, k_ref) % 2)))


def _grad_w_out(kidx, ycat, dy):
    s_len = dy.shape[0]
    tk = min(K_TILE, s_len)
    return _grad_reduce_scatter(
        kidx, ycat, dy, name="grad_w_out", nj=1, nr=256, nc=D_MODEL, b_lead=False, kt=s_len // tk,
        a_spec=lambda seq: pl.BlockSpec((tk, 512), lambda i, kk, k_ref: (kk, seq(i, k_ref))),
        b_spec=lambda seq: pl.BlockSpec((tk, D_MODEL), lambda i, kk, k_ref: (kk, 0)))


def _adamw_math(w, g, m, v):
    m = ADAM_B1 * m + (1.0 - ADAM_B1) * g
    v = ADAM_B2 * v + (1.0 - ADAM_B2) * (g * g)
    m_hat = m / (1.0 - ADAM_B1 ** ADAM_STEP)
    v_hat = v / (1.0 - ADAM_B2 ** ADAM_STEP)
    delta = -ADAM_LR * (m_hat / (jnp.sqrt(v_hat) + ADAM_EPS) + ADAM_WD * w)
    return delta, m, v


def _adamw_blocked(w, m, v, g4, name):
    nj, _, nr, nc = g4.shape

    def body(w_ref, m_ref, v_ref, g_ref, go_ref, d_ref, mo_ref, vo_ref):
        g = g_ref[0, 0]
        d, mn, vn = _adamw_math(w_ref[...], g, m_ref[...], v_ref[...])
        go_ref[...] = g
        d_ref[...] = d
        mo_ref[...] = mn
        vo_ref[...] = vn

    blk = pl.BlockSpec((nr, nc), lambda j, hf: (hf, j))
    return pl.pallas_call(
        body, name=name,
        grid=(nj, 2),
        out_shape=tuple(jax.ShapeDtypeStruct(w.shape, F32) for _ in range(4)),
        in_specs=[blk, blk, blk, pl.BlockSpec((1, 1, nr, nc), lambda j, hf: (j, hf, 0, 0))],
        out_specs=(blk, blk, blk, blk),
        compiler_params=_params(dimension_semantics=("arbitrary", "arbitrary")),
    )(w, m, v, g4)


def _adamw_w_ada(cact, pk1, pk2, w, m, v):
    def body(cact_ref, pk1_ref, pk2_ref, w_ref, m_ref, v_ref, g_ref, d_ref, mo_ref, vo_ref, gb_ref, dmod):
        x, y, _ = _pos()
        k = 2 * x + y
        dmod[:, 0:D_MODEL] = pk2_ref[0:8, :]
        dmod[:, D_MODEL:2 * D_MODEL] = pk2_ref[8:16, :]
        dmod[:, 2 * D_MODEL:3 * D_MODEL] = pk1_ref[0:8, :]
        gb_ref[...] = jnp.sum(dmod[...], axis=0, keepdims=True)
        sel = jnp.zeros((8, 768), F32)
        for kk in range(N_CHIPS):
            sel = jnp.where(k == kk, dmod[:, kk * 768:(kk + 1) * 768], sel)
        g = _dot_tn(cact_ref[...].astype(BF16), sel.astype(BF16))
        d, mn, vn = _adamw_math(w_ref[...], g, m_ref[...], v_ref[...])
        g_ref[...] = g
        d_ref[...] = d
        mo_ref[...] = mn
        vo_ref[...] = vn

    return pl.pallas_call(
        body, name="adamw_w_ada",
        out_shape=tuple(jax.ShapeDtypeStruct(w.shape, F32) for _ in range(4)) + (
            jax.ShapeDtypeStruct((1, 3 * D_MODEL), F32),),
        in_specs=[_vmem()] * 6,
        out_specs=tuple(_vmem() for _ in range(5)),
        scratch_shapes=[pltpu.VMEM((8, 3 * D_MODEL), F32)],
        compiler_params=_params(),
    )(cact, pk1, pk2, w, m, v)


def _adamw_small(items):
    n = len(items)

    def body(*refs):
        ins, outs = refs[:4 * n], refs[4 * n:]
        for i in range(n):
            w_ref, g_ref, m_ref, v_ref = ins[4 * i:4 * i + 4]
            d, mn, vn = _adamw_math(w_ref[...], g_ref[...], m_ref[...], v_ref[...])
            outs[3 * i][...] = d
            outs[3 * i + 1][...] = mn
            outs[3 * i + 2][...] = vn

    flat = [a for it in items for a in it]
    outs = pl.pallas_call(
        body, name="adamw_small",
        out_shape=tuple(jax.ShapeDtypeStruct(it[0].shape, F32) for it in items for _ in range(3)),
        in_specs=[_vmem()] * (4 * n),
        out_specs=tuple(_vmem() for _ in range(3 * n)),
        compiler_params=_params(),
    )(*flat)
    return [tuple(outs[3 * i:3 * i + 3]) for i in range(n)]


def kernel(x, c, w_ada, b_ada, norm_g, w_in, conv_w, conv_b, conv_ln_g, conv_ln_b, sg_ln_g, sg_ln_b, w_s, b_s, w_out, final_g, loss_target, m_w_ada, m_b_ada, m_norm_g, m_w_in, m_conv_w, m_conv_b, m_conv_ln_g, m_conv_ln_b, m_sg_ln_g, m_sg_ln_b, m_w_s, m_b_s, m_w_out, m_final_g, v_w_ada, v_b_ada, v_norm_g, v_w_in, v_conv_w, v_conv_b, v_conv_ln_g, v_conv_ln_b, v_sg_ln_g, v_sg_ln_b, v_w_s, v_b_s, v_w_out, v_final_g):
    s_len = x.shape[1]
    x2d = x[0]
    tgt = loss_target[0]
    row = lambda a: a.reshape(1, -1)

    kidx = (2 * lax.axis_index("x") + lax.axis_index("y")).astype(jnp.int32).reshape(1)
    cw_sh = jnp.pad(conv_w.reshape(CONV_WIDTH, 256), ((0, 1), (0, 0)))
    h, z6, mod, cact, cw4, w_in12 = _front(kidx, x2d, c, w_ada[0], b_ada, norm_g, w_in[0], cw_sh)
    w12 = w_in12.reshape(12, D_MODEL, 512)
    q, w_out4 = _conv_fwd(z6, cw4, conv_b, w_out[0])
    w_out_full = w_out4.reshape(2 * D_MODEL, D_MODEL)
    ln_rows = (conv_ln_g, conv_ln_b, sg_ln_g, sg_ln_b, row(final_g))
    bs_exp = jnp.repeat(b_s[0].T, CHUNK, axis=1)
    dz6, dq, dx2, ycat, dy, dws, sm1, dbs = _middle(x2d, z6, q, tgt, ln_rows, mod, w_s[0], bs_exp, w_out_full)
    g_w_out4 = _grad_w_out(kidx, ycat, dy)
    dz6, dcw, pk1, dws_r, dbs_r, loss_t = _conv_bwd(dq, z6, cw4, dz6, sm1, dws.reshape(D_MODEL, CHUNK), dbs)
    grad_x, sm2 = _bwd_in(dz6, w12, x2d, dx2, mod, norm_g)
    g_w_in4, pk2, dcw_sh, dcb = _grad_w_in(kidx, h, dz6, sm2, dcw)

    g_w_in, d_w_in, nm_w_in, nv_w_in = _adamw_blocked(w_in[0], m_w_in[0], v_w_in[0], g_w_in4, "adamw_w_in")
    g_w_out, d_w_out, nm_w_out, nv_w_out = _adamw_blocked(w_out[0], m_w_out[0], v_w_out[0], g_w_out4, "adamw_w_out")
    g_w_ada, d_w_ada, nm_w_ada, nv_w_ada, g_b_ada = _adamw_w_ada(cact, pk1, pk2, w_ada[0], m_w_ada[0], v_w_ada[0])

    g_norm_g = pk2[16:17]
    g_cln_g, g_cln_b, g_sln_g, g_sln_b, g_final = (pk1[8 + i:9 + i] for i in range(5))
    loss = loss_t[0, 0]
    g_conv_w = dcw_sh[:CONV_WIDTH]
    g_w_s = dws_r
    g_b_s = dbs_r[:, :HEADS].T
    small = [
        (b_ada, g_b_ada, m_b_ada, v_b_ada),
        (norm_g, g_norm_g, m_norm_g, v_norm_g),
        (conv_w.reshape(CONV_WIDTH, 256), g_conv_w, m_conv_w.reshape(CONV_WIDTH, 256), v_conv_w.reshape(CONV_WIDTH, 256)),
        (conv_b, dcb, m_conv_b, v_conv_b),
        (conv_ln_g, g_cln_g, m_conv_ln_g, v_conv_ln_g),
        (conv_ln_b, g_cln_b, m_conv_ln_b, v_conv_ln_b),
        (sg_ln_g, g_sln_g, m_sg_ln_g, v_sg_ln_g),
        (sg_ln_b, g_sln_b, m_sg_ln_b, v_sg_ln_b),
        (w_s.reshape(D_MODEL, CHUNK), g_w_s, m_w_s.reshape(D_MODEL, CHUNK), v_w_s.reshape(D_MODEL, CHUNK)),
        (b_s[0], g_b_s, m_b_s[0], v_b_s[0]),
        (row(final_g), g_final, row(m_final_g), row(v_final_g)),
    ]
    upd = _adamw_small(small)

    shapes = [w_ada.shape, b_ada.shape, norm_g.shape, w_in.shape, conv_w.shape, conv_b.shape, conv_ln_g.shape,
              conv_ln_b.shape, sg_ln_g.shape, sg_ln_b.shape, w_s.shape, b_s.shape, w_out.shape, final_g.shape]
    grads = [g_w_ada, g_b_ada, g_norm_g, g_w_in, g_conv_w, dcb, g_cln_g, g_cln_b, g_sln_g, g_sln_b, g_w_s, g_b_s,
             g_w_out, g_final]
    big = {0: (d_w_ada, nm_w_ada, nv_w_ada), 3: (d_w_in, nm_w_in, nv_w_in), 12: (d_w_out, nm_w_out, nv_w_out)}
    small_pos = [1, 2, 4, 5, 6, 7, 8, 9, 10, 11, 13]
    trip = [None] * 14
    for i, t in big.items():
        trip[i] = t
    for i, t in zip(small_pos, upd):
        trip[i] = t
    fit = lambda arrs: [a.reshape(s) for a, s in zip(arrs, shapes)]
    return (loss, grad_x.reshape(x.shape), *fit(grads), *fit([t[0] for t in trip]), *fit([t[1] for t in trip]),
            *fit([t[2] for t in trip]))
```

```python
import jax
import jax.numpy as jnp
from jax import lax
from jax.experimental import pallas as pl
from jax.experimental.pallas import tpu as pltpu

F32 = jnp.float32
BF16 = jnp.bfloat16
MESH = pl.DeviceIdType.MESH

D_MODEL = 1024
N_CHIPS = 4
HEADS = 8
CHUNK = 128
CONV_WIDTH = 31
CONV_HALF = CONV_WIDTH // 2
CONV_PAD = 16
EPS = 1e-6
ADAM_LR = 0.001
ADAM_B1 = 0.9
ADAM_B2 = 0.999
ADAM_EPS = 1e-08
ADAM_WD = 0.01
ADAM_STEP = 10

V7X_VMEM_BYTES = 64 * 1024 * 1024
VMEM_LIMIT = V7X_VMEM_BYTES - 8 * 1024 * 1024
ROWS = 16
UNROLL = 8
TOKEN_TILE = 256
FULL_UNROLL = TOKEN_TILE // ROWS
TIME_TILE = 128
K_TILE = 2048

N_GROUPS = 6


def _natural_group(j):
    return (j + 2) % N_GROUPS


def _pos():
    return lax.axis_index("x"), lax.axis_index("y"), lax.axis_index("c")


def _rcopy(src, dst, ssem, rsem, dev):
    return pltpu.make_async_remote_copy(src_ref=src, dst_ref=dst, send_sem=ssem, recv_sem=rsem,
                                        device_id=dev, device_id_type=MESH)


def _vmem():
    return pl.BlockSpec(memory_space=pltpu.VMEM)


def _params(**kw):
    return pltpu.CompilerParams(vmem_limit_bytes=VMEM_LIMIT, **kw)


def _sigmoid(v):
    return 0.5 * jnp.tanh(0.5 * v) + 0.5


def _row_loop(n_rows, body, unroll=1):
    def step(r, carry):
        body(pl.ds(pl.multiple_of(r * ROWS, ROWS), ROWS))
        return carry
    lax.fori_loop(0, n_rows // ROWS, step, 0, unroll=unroll)


def _colsum8(v):
    return v.reshape(v.shape[0] // 8, 8, v.shape[1]).sum(axis=0)


def _mean(v):
    return jnp.mean(v, axis=-1, keepdims=True)


def _dot_nn(a, b):
    return jnp.dot(a, b, preferred_element_type=F32)


def _dot_nt(a, b):
    return lax.dot_general(a, b, (((1,), (1,)), ((), ())), preferred_element_type=F32)


def _dot_tn(a, b):
    return lax.dot_general(a, b, (((0,), (0,)), ((), ())), preferred_element_type=F32)


def _remote_chip(k, r):
    return jnp.bitwise_xor(k, r + 1)


def _front(kidx, x, c, w_ada, b_ada, norm_g, w_in, cw):
    s_len = x.shape[0]
    tmh = min(512, s_len)
    tmz = min(1024, s_len)
    nh = s_len // tmh
    nz = s_len // tmz
    n_steps = nh + 12 * nz

    def remote_block(q):
        return jnp.where(q < 6, q % 2, 2), jnp.where(q < 6, q // 2, q - 6)

    def block_of(i, k):
        r, j = remote_block(jnp.maximum(i - 3, 0))
        return jnp.where(i < 3, 3 * k + i, 3 * _remote_chip(k, r) + j)

    def body(k_ref, x_ref, c_ref, bada_ref, g_ref, cw_ref, wada_hbm, win_hbm,
             h_ref, z_ref, mod_ref, cact_ref, cw4_ref, w12_hbm,
             h_all, wbuf, stage, wada_v, cslab, mslab, cw4_s,
             lsem, csend, crecv, msend, mrecv, wsend, wrecv, isend, irecv, fsend, frecv, osem):
        del k_ref
        t = pl.program_id(0)
        x_, y_, c_ = _pos()
        k = 2 * x_ + y_
        b = 4 * x_ + 2 * y_ + c_
        sib = (x_, y_, 1 - c_)

        def dev_of(r):
            kk = _remote_chip(k, r)
            return (kk // 2, kk % 2, c_)

        def ici(q, kk):
            r, j = remote_block(q)
            return _rcopy(wbuf.at[3 * kk + j, c_], wbuf.at[3 * kk + j, c_], isend.at[q], irecv.at[q], dev_of(r))

        def fwd(q, hf):
            r, j = remote_block(q)
            blk = 3 * _remote_chip(k, r) + j
            return _rcopy(wbuf.at[blk, hf], wbuf.at[blk, hf], fsend.at[q], frecv.at[q], sib)

        def c_copy(q, src):
            d = jnp.bitwise_xor(b, q)
            return _rcopy(cslab.at[src], cslab.at[src], csend.at[q - 1], crecv.at[q - 1], (d // 4, (d // 2) % 2, d % 2))

        def m_copy(r, kk):
            return _rcopy(mslab.at[kk], mslab.at[kk], msend.at[r], mrecv.at[r], dev_of(r))

        def cw_copy(r, kk):
            return _rcopy(cw4_s.at[kk], cw4_s.at[kk], wsend.at[r], wrecv.at[r], dev_of(r))

        def to_hbm(i):
            m = block_of(i, k)
            return pltpu.make_async_copy(wbuf.at[m], w12_hbm.at[m], osem.at[i])

        @pl.when(t == 0)
        def _():
            ld_w = pltpu.make_async_copy(win_hbm, stage, lsem.at[0])
            ld_w.start()
            ld_a = pltpu.make_async_copy(wada_hbm, wada_v, lsem.at[1])
            ld_a.start()
            cslab[b] = jnp.broadcast_to(c_ref[...], (8, D_MODEL))
            for q in range(1, 8):
                c_copy(q, b).start()
            cw4_s[k] = cw_ref[...]
            for r in range(3):
                cw_copy(r, k).start()
            ld_w.wait()
            for j in range(3):
                for hf in range(2):
                    wbuf[3 * k + j, hf] = stage[hf * 512:(hf + 1) * 512, j * 512:(j + 1) * 512].astype(BF16)
            for i in range(3):
                to_hbm(i).start()
            for q in range(1, 8):
                c_copy(q, jnp.bitwise_xor(b, q)).wait_recv()
            row = lax.broadcasted_iota(jnp.int32, (8, D_MODEL), 0)
            call = jnp.zeros((8, D_MODEL), F32)
            for d in range(8):
                call = jnp.where(row == d, cslab[d], call)
            cact = call * _sigmoid(call)
            cact_ref[...] = cact
            ld_a.wait()
            mslab[k] = _dot_nn(cact.astype(BF16), wada_v[...].astype(BF16))
            for r in range(3):
                m_copy(r, k).start()
            for q in range(9):
                ici(q, k).start()
            for r in range(3):
                m_copy(r, _remote_chip(k, r)).wait_recv()
            row8 = lax.broadcasted_iota(jnp.int32, (8, 768), 0)
            for kk in range(N_CHIPS):
                piece = jnp.sum(jnp.where(row8 == b, mslab[kk], 0.0), axis=0, keepdims=True)
                mod_ref[:, kk * 768:(kk + 1) * 768] = piece + bada_ref[:, kk * 768:(kk + 1) * 768]
            for r in range(3):
                cw_copy(r, _remote_chip(k, r)).wait_recv()
            cw4_ref[...] = cw4_s[...]

        @pl.when(t < nh)
        def _():
            shift = mod_ref[:, 0:D_MODEL]
            scale1 = 1.0 + mod_ref[:, D_MODEL:2 * D_MODEL]
            g = g_ref[...]
            base = t * tmh

            def rows_fn(rows):
                xt = x_ref[rows, :]
                r = lax.rsqrt(_mean(xt * xt) + EPS)
                hv = ((xt * r * g) * scale1 + shift).astype(BF16)
                h_ref[rows, :] = hv
                h_all[pl.ds(pl.multiple_of(base + rows.start, ROWS), ROWS), :] = hv
            _row_loop(tmh, rows_fn, unroll=UNROLL)

        @pl.when(t >= nh)
        def _():
            u = t - nh
            i = u // nz
            rt = u % nz
            @pl.when(jnp.logical_and(rt == 0, jnp.logical_and(i >= 2, i <= 10)))
            def _():
                q = i - 2
                r, _ = remote_block(q)
                ici(q, _remote_chip(k, r)).wait_recv()
                fwd(q, c_).start()

            @pl.when(jnp.logical_and(rt == 0, i >= 3))
            def _():
                fwd(i - 3, 1 - c_).wait_recv()
                to_hbm(i).start()
            m = block_of(i, k)
            hb = h_all[pl.ds(pl.multiple_of(rt * tmz, tmz), tmz), :]
            z_ref[0] = _dot_nn(hb[:, 0:512], wbuf[m, 0]) + _dot_nn(hb[:, 512:1024], wbuf[m, 1])

        @pl.when(t == n_steps - 1)
        def _():
            for q in range(1, 8):
                c_copy(q, b).wait_send()
            for r in range(3):
                m_copy(r, k).wait_send()
                cw_copy(r, k).wait_send()
            for q in range(9):
                ici(q, k).wait_send()
                fwd(q, c_).wait_send()
            for i in range(12):
                to_hbm(i).wait()

    def z_index(t, k_ref):
        u = jnp.maximum(t - nh, 0)
        m = block_of(u // nz, k_ref[0])
        return ((m // 2 + 4) % N_GROUPS, u % nz, m % 2)

    tok = lambda t, k_ref: (jnp.minimum(t, nh - 1), 0)
    const2 = lambda t, k_ref: (0, 0)
    grid_spec = pltpu.PrefetchScalarGridSpec(
        num_scalar_prefetch=1,
        grid=(n_steps,),
        in_specs=[pl.BlockSpec((tmh, D_MODEL), tok),
                  pl.BlockSpec((1, D_MODEL), const2),
                  pl.BlockSpec((1, 3 * D_MODEL), const2),
                  pl.BlockSpec((1, D_MODEL), const2),
                  pl.BlockSpec((32, 256), const2),
                  pl.BlockSpec(memory_space=pl.ANY),
                  pl.BlockSpec(memory_space=pl.ANY)],
        out_specs=(pl.BlockSpec((tmh, D_MODEL), tok),
                   pl.BlockSpec((1, tmz, 512), z_index),
                   pl.BlockSpec((1, 3 * D_MODEL), const2),
                   pl.BlockSpec((8, D_MODEL), const2),
                   pl.BlockSpec((N_CHIPS, 32, 256), lambda t, k_ref: (0, 0, 0)),
                   pl.BlockSpec(memory_space=pl.ANY)),
        scratch_shapes=[pltpu.VMEM((s_len, D_MODEL), BF16),
                        pltpu.VMEM((12, 2, 512, 512), BF16),
                        pltpu.VMEM((D_MODEL, 1536), F32),
                        pltpu.VMEM((D_MODEL, 768), F32),
                        pltpu.VMEM((8, 8, D_MODEL), F32),
                        pltpu.VMEM((N_CHIPS, 8, 768), F32),
                        pltpu.VMEM((N_CHIPS, 32, 256), F32),
                        pltpu.SemaphoreType.DMA((2,)),
                        pltpu.SemaphoreType.DMA((7,)), pltpu.SemaphoreType.DMA((7,)),
                        pltpu.SemaphoreType.DMA((3,)), pltpu.SemaphoreType.DMA((3,)),
                        pltpu.SemaphoreType.DMA((3,)), pltpu.SemaphoreType.DMA((3,)),
                        pltpu.SemaphoreType.DMA((9,)), pltpu.SemaphoreType.DMA((9,)),
                        pltpu.SemaphoreType.DMA((9,)), pltpu.SemaphoreType.DMA((9,)),
                        pltpu.SemaphoreType.DMA((12,))])
    return pl.pallas_call(
        body, name="front",
        grid_spec=grid_spec,
        out_shape=(jax.ShapeDtypeStruct((s_len, D_MODEL), BF16),
                   jax.ShapeDtypeStruct((N_GROUPS, s_len, D_MODEL), F32),
                   jax.ShapeDtypeStruct((1, 3 * D_MODEL), F32),
                   jax.ShapeDtypeStruct((8, D_MODEL), F32),
                   jax.ShapeDtypeStruct((N_CHIPS, 32, 256), F32),
                   jax.ShapeDtypeStruct((12, 2, 512, 512), BF16)),
        compiler_params=_params(dimension_semantics=("arbitrary",)),
    )(kidx, x, c, b_ada, norm_g, cw, w_ada, w_in)


def _all_reduce_partners():
    x, y, c = _pos()
    return [(x, y, 1 - c), (x, 1 - y, c), (1 - x, y, c)]


def _butterfly_event(e, bufs, recvs, ssem, rsem, partners):
    n = len(bufs)

    def copies(s):
        return [_rcopy(buf, recv.at[s], ssem.at[s * n + i], rsem.at[s * n + i], partners[s])
                for i, (buf, recv) in enumerate(zip(bufs, recvs))]

    if e > 0:
        for cp in copies(e - 1):
            cp.wait()
        for buf, recv in zip(bufs, recvs):
            buf[...] = buf[...] + recv[e - 1]
    if e < len(partners):
        for cp in copies(e):
            cp.start()


def _onehot_rows(v, b):
    row = lax.broadcasted_iota(jnp.int32, (8, v.shape[1]), 0)
    return jnp.where(row == b, jnp.broadcast_to(v, (8, v.shape[1])), 0.0)


def _conv_fwd(z6, cw4, conv_b, w_out):
    s_len = z6.shape[1]
    tt = TIME_TILE
    n_blocks = D_MODEL // 128

    def body(z_ref, cw_ref, cb_ref, wout_hbm, q_ref, wout4_hbm, ppad, stage, wbuf, lsem, isend, irecv, fsend, frecv, osem):
        jb = pl.program_id(0)
        x_, y_, c_ = _pos()
        k = 2 * x_ + y_
        sib = (x_, y_, 1 - c_)

        def ici(r, kk):
            rk = _remote_chip(k, r)
            return _rcopy(wbuf.at[kk, c_], wbuf.at[kk, c_], isend.at[r], irecv.at[r], (rk // 2, rk % 2, c_))

        def fwd(r, hf):
            kk = _remote_chip(k, r)
            return _rcopy(wbuf.at[kk, hf], wbuf.at[kk, hf], fsend.at[r], frecv.at[r], sib)

        @pl.when(jb == 0)
        def _():
            ld = pltpu.make_async_copy(wout_hbm, stage, lsem)
            ld.start()
            ld.wait()
            for hf in range(2):
                wbuf[k, hf] = stage[hf * 256:(hf + 1) * 256, :].astype(BF16)
            for r in range(3):
                ici(r, k).start()

        @pl.when(jb == n_blocks // 2)
        def _():
            for r in range(3):
                ici(r, _remote_chip(k, r)).wait_recv()
                fwd(r, c_).start()

        zero = jnp.zeros((CONV_PAD, 128), F32)
        ppad[0:CONV_PAD, :] = zero
        ppad[s_len + CONV_PAD:s_len + 2 * CONV_PAD, :] = zero

        def fill(i, carry):
            t0 = pl.multiple_of(i * tt, tt)
            ppad[pl.ds(CONV_PAD + t0, tt), :] = z_ref[0, pl.ds(t0, tt), :] * _sigmoid(z_ref[1, pl.ds(t0, tt), :])
            return carry
        lax.fori_loop(0, s_len // tt, fill, 0)
        w = cw_ref[0]
        bias = cb_ref[...]

        def conv(i, carry):
            t0 = pl.multiple_of(i * tt, tt)
            acc = jnp.broadcast_to(bias, (tt, 128))
            for o in range(1, CONV_WIDTH + 1):
                acc = acc + w[o - 1:o, :] * ppad[pl.ds(t0 + o, tt), :]
            q_ref[pl.ds(t0, tt), :] = acc
            return carry
        lax.fori_loop(0, s_len // tt, conv, 0)

        @pl.when(jb == n_blocks - 1)
        def _():
            for r in range(3):
                fwd(r, 1 - c_).wait_recv()
            out = pltpu.make_async_copy(wbuf, wout4_hbm, osem)
            out.start()
            for r in range(3):
                ici(r, k).wait_send()
                fwd(r, c_).wait_send()
            out.wait()

    return pl.pallas_call(
        body, name="conv_fwd",
        grid=(n_blocks,),
        out_shape=(jax.ShapeDtypeStruct((s_len, D_MODEL), F32), jax.ShapeDtypeStruct((N_CHIPS, 2, 256, D_MODEL), BF16)),
        in_specs=[pl.BlockSpec((2, s_len, 128), lambda j: (2, 0, j)),
                  pl.BlockSpec((1, 32, 128), lambda j: (j // 2, 0, j % 2)),
                  pl.BlockSpec((1, 128), lambda j: (0, j)),
                  pl.BlockSpec(memory_space=pl.ANY)],
        out_specs=(pl.BlockSpec((s_len, 128), lambda j: (0, j)), pl.BlockSpec(memory_space=pl.ANY)),
        scratch_shapes=[pltpu.VMEM((s_len + 2 * CONV_PAD, 128), F32),
                        pltpu.VMEM((512, D_MODEL), F32),
                        pltpu.VMEM((N_CHIPS, 2, 256, D_MODEL), BF16),
                        pltpu.SemaphoreType.DMA,
                        pltpu.SemaphoreType.DMA((3,)), pltpu.SemaphoreType.DMA((3,)),
                        pltpu.SemaphoreType.DMA((3,)), pltpu.SemaphoreType.DMA((3,)),
                        pltpu.SemaphoreType.DMA],
        compiler_params=_params(dimension_semantics=("arbitrary",)),
    )(z6, cw4, conv_b, w_out)


def _middle(x, z6, q, target, ln_rows, mod, w_s, bs_exp, w_out):
    s_len = x.shape[0]
    tm = TOKEN_TILE
    n_steps = s_len // tm
    n_chunks = tm // CHUNK
    inv_d = 1.0 / D_MODEL

    def body(x_ref, z_ref, q_ref, tgt_ref, cg_ref, cb_ref, sg_ref, sb_ref, fg_ref, mod_ref, ws_ref, bs_ref, wout_ref,
             dz_ref, dq_ref, dx2_ref, ycat_ref, dy_ref, dws_ref, sm_ref, dbs_ref,
             vl_scr, vm_scr, y_scr, dycat_scr, dvm_scr, dvl_scr, acc_scr, dbs_acc):
        i = pl.program_id(0)

        @pl.when(i == 0)
        def _():
            acc_scr[...] = jnp.zeros_like(acc_scr)
            dbs_acc[...] = jnp.zeros_like(dbs_acc)
            dws_ref[...] = jnp.zeros_like(dws_ref)

        cg, cb, sg, sb, fg = cg_ref[...], cb_ref[...], sg_ref[...], sb_ref[...], fg_ref[...]
        gm = mod_ref[:, 2 * D_MODEL:3 * D_MODEL]

        def norm_stats(t):
            c = t - _mean(t)
            rstd = lax.rsqrt(_mean(c * c) + EPS)
            return c * rstd, rstd

        def phase1(rows):
            qhat, _ = norm_stats(q_ref[rows, :])
            ln = qhat * cg + cb
            gz = z_ref[0, rows, :]
            ycat_ref[rows, 0:D_MODEL] = ((ln * _sigmoid(ln)) * (gz * _sigmoid(gz))).astype(BF16)
            vhat, _ = norm_stats(z_ref[2, rows, :])
            vl_scr[rows, :] = (vhat * sg + sb).astype(BF16)
        _row_loop(tm, phase1, unroll=FULL_UNROLL)

        for ch in range(n_chunks):
            r0 = ch * CHUNK
            for h in range(HEADS):
                c0 = h * CHUNK
                vm_scr[r0:r0 + CHUNK, c0:c0 + CHUNK] = (
                    _dot_nn(ws_ref[h].astype(BF16), vl_scr[r0:r0 + CHUNK, c0:c0 + CHUNK]) + bs_ref[:, c0:c0 + CHUNK])

        def phase3(rows):
            bg = z_ref[3, rows, :]
            ycat_ref[rows, D_MODEL:2 * D_MODEL] = (z_ref[1, rows, :] * vm_scr[rows, :] * (bg * _sigmoid(bg))).astype(BF16)
        _row_loop(tm, phase3, unroll=FULL_UNROLL)

        y_scr[...] = _dot_nn(ycat_ref[...], wout_ref[...])

        def phase5(rows):
            y = y_scr[rows, :]
            x2 = x_ref[rows, :] + gm * y
            r2 = lax.rsqrt(_mean(x2 * x2) + EPS)
            xn2 = x2 * r2
            diff = xn2 * fg - tgt_ref[rows, :]
            acc_scr[6] += _colsum8(diff * diff)
            dout = diff * inv_d
            acc_scr[0] += _colsum8(dout * xn2)
            dxn = dout * fg
            dx2 = r2 * (dxn - xn2 * _mean(dxn * xn2))
            dx2_ref[rows, :] = dx2
            acc_scr[1] += _colsum8(dx2 * y)
            dy_ref[rows, :] = (dx2 * gm).astype(BF16)
        _row_loop(tm, phase5, unroll=FULL_UNROLL)

        dycat_scr[...] = _dot_nt(dy_ref[...], wout_ref[...])

        def phase7(rows):
            dyb = dycat_scr[rows, D_MODEL:2 * D_MODEL]
            u = z_ref[1, rows, :]
            bg = z_ref[3, rows, :]
            vm = vm_scr[rows, :]
            sig = _sigmoid(bg)
            silu = bg * sig
            dz_ref[1, rows, :] = (dyb * vm * silu).astype(BF16)
            dvm = dyb * u * silu
            dz_ref[3, rows, :] = (dyb * u * vm * (sig * (1.0 + bg * (1.0 - sig)))).astype(BF16)
            dvm_scr[rows, :] = dvm.astype(BF16)
            pos = pl.ds(pl.multiple_of(rows.start % CHUNK, ROWS), ROWS)
            dbs_acc[pos, :] += dvm
        _row_loop(tm, phase7, unroll=FULL_UNROLL)

        for ch in range(n_chunks):
            r0 = ch * CHUNK
            for h in range(HEADS):
                c0 = h * CHUNK
                dvm_b = dvm_scr[r0:r0 + CHUNK, c0:c0 + CHUNK]
                dws_ref[h] += _dot_nt(dvm_b, vl_scr[r0:r0 + CHUNK, c0:c0 + CHUNK])
                dvl_scr[r0:r0 + CHUNK, c0:c0 + CHUNK] = _dot_tn(ws_ref[h].astype(BF16), dvm_b)

        def phase9(rows):
            vhat, rstd_v = norm_stats(z_ref[2, rows, :])
            dvl = dvl_scr[rows, :]
            acc_scr[4] += _colsum8(dvl * vhat)
            acc_scr[5] += _colsum8(dvl)
            dvh = dvl * sg
            dz_ref[2, rows, :] = (rstd_v * (dvh - _mean(dvh) - vhat * _mean(dvh * vhat))).astype(BF16)
            qhat, rstd_q = norm_stats(q_ref[rows, :])
            ln = qhat * cg + cb
            sig_ln = _sigmoid(ln)
            gz = z_ref[0, rows, :]
            sig_g = _sigmoid(gz)
            dya = dycat_scr[rows, 0:D_MODEL]
            dz_ref[0, rows, :] = (dya * (ln * sig_ln) * (sig_g * (1.0 + gz * (1.0 - sig_g)))).astype(BF16)
            dln = (dya * (gz * sig_g)) * (sig_ln * (1.0 + ln * (1.0 - sig_ln)))
            acc_scr[2] += _colsum8(dln * qhat)
            acc_scr[3] += _colsum8(dln)
            dqh = dln * cg
            dq_ref[rows, :] = rstd_q * (dqh - _mean(dqh) - qhat * _mean(dqh * qhat))
        _row_loop(tm, phase9, unroll=FULL_UNROLL)

        @pl.when(i == n_steps - 1)
        def _():
            for qi in range(8):
                scale = 0.5 * inv_d if qi == 6 else 1.0
                sm_ref[qi:qi + 1, :] = jnp.sum(acc_scr[qi], axis=0, keepdims=True) * scale
            lane = lax.broadcasted_iota(jnp.int32, (CHUNK, CHUNK), 1)
            tile = jnp.zeros((CHUNK, CHUNK), F32)
            for h in range(HEADS):
                col = jnp.sum(dbs_acc[:, h * CHUNK:(h + 1) * CHUNK], axis=1, keepdims=True)
                tile = jnp.where(lane == h, col, tile)
            dbs_ref[...] = tile

    tok = lambda i: (i, 0)
    const2 = lambda i: (0, 0)
    return pl.pallas_call(
        body, name="middle",
        grid=(n_steps,),
        out_shape=(jax.ShapeDtypeStruct((N_GROUPS, s_len, D_MODEL), BF16),
                   jax.ShapeDtypeStruct((s_len, D_MODEL), F32),
                   jax.ShapeDtypeStruct((s_len, D_MODEL), F32),
                   jax.ShapeDtypeStruct((s_len, 2 * D_MODEL), BF16),
                   jax.ShapeDtypeStruct((s_len, D_MODEL), BF16),
                   jax.ShapeDtypeStruct((HEADS, CHUNK, CHUNK), F32),
                   jax.ShapeDtypeStruct((8, D_MODEL), F32),
                   jax.ShapeDtypeStruct((CHUNK, CHUNK), F32)),
        in_specs=[pl.BlockSpec((tm, D_MODEL), tok),
                  pl.BlockSpec((4, tm, D_MODEL), lambda i: (0, i, 0)),
                  pl.BlockSpec((tm, D_MODEL), tok),
                  pl.BlockSpec((tm, D_MODEL), tok),
                  *[pl.BlockSpec((1, D_MODEL), const2) for _ in range(5)],
                  pl.BlockSpec((1, 3 * D_MODEL), const2),
                  pl.BlockSpec((HEADS, CHUNK, CHUNK), lambda i: (0, 0, 0)),
                  pl.BlockSpec((CHUNK, D_MODEL), const2),
                  pl.BlockSpec((2 * D_MODEL, D_MODEL), const2, pipeline_mode=pl.Buffered(1))],
        out_specs=(pl.BlockSpec((4, tm, D_MODEL), lambda i: (0, i, 0)),
                   pl.BlockSpec((tm, D_MODEL), tok),
                   pl.BlockSpec((tm, D_MODEL), tok),
                   pl.BlockSpec((tm, 2 * D_MODEL), tok),
                   pl.BlockSpec((tm, D_MODEL), tok),
                   pl.BlockSpec((HEADS, CHUNK, CHUNK), lambda i: (0, 0, 0)),
                   pl.BlockSpec((8, D_MODEL), const2),
                   pl.BlockSpec((CHUNK, CHUNK), const2)),
        scratch_shapes=[pltpu.VMEM((tm, D_MODEL), BF16),
                        pltpu.VMEM((tm, D_MODEL), F32),
                        pltpu.VMEM((tm, D_MODEL), F32),
                        pltpu.VMEM((tm, 2 * D_MODEL), F32),
                        pltpu.VMEM((tm, D_MODEL), BF16),
                        pltpu.VMEM((tm, D_MODEL), F32),
                        pltpu.VMEM((8, 8, D_MODEL), F32),
                        pltpu.VMEM((CHUNK, D_MODEL), F32)],
        compiler_params=_params(dimension_semantics=("arbitrary",)),
    )(x, z6, q, target, *ln_rows, mod, w_s, bs_exp, w_out)


def _conv_bwd(dq, z6, cw4, dz6, sm1, dws, dbs):
    s_len = dq.shape[0]
    tt = TIME_TILE
    n_blocks = D_MODEL // 128
    shp = [(16, D_MODEL), (D_MODEL, 128), (128, 128)]

    def body(dq_ref, z_ref, cw_ref, dz_in, sm1_ref, dws_in, dbs_in,
             dz_ref, dcw_ref, pk1_ref, dws_ref, dbs_ref, loss_ref,
             dqpad, ppad, wacc, b_pk, b_dws, b_dbs, r_pk, r_dws, r_dbs, ssem, rsem):
        del dz_in
        jb = pl.program_id(0)
        x_, y_, c_ = _pos()

        @pl.when(jb == 0)
        def _():
            b_pk[0:8, :] = _onehot_rows(sm1_ref[1:2, :], 4 * x_ + 2 * y_ + c_)
            for r, src in enumerate([2, 3, 4, 5, 0, 6]):
                b_pk[8 + r:9 + r, :] = sm1_ref[src:src + 1, :]
            b_pk[14:16, :] = jnp.zeros((2, D_MODEL), F32)
            b_dws[...] = dws_in[...]
            b_dbs[...] = dbs_in[...]

        for e in range(4):
            @pl.when(jb == (e * n_blocks) // 4)
            def _():
                _butterfly_event(e, [b_pk, b_dws, b_dbs], [r_pk, r_dws, r_dbs], ssem, rsem, _all_reduce_partners())

        @pl.when(jb == (3 * n_blocks) // 4)
        def _():
            pk1_ref[...] = b_pk[...]
            dws_ref[...] = b_dws[...]
            dbs_ref[...] = b_dbs[...]
            loss_ref[...] = jnp.broadcast_to(jnp.sum(b_pk[13:14, :], axis=1, keepdims=True), (8, 128))

        zero = jnp.zeros((CONV_PAD, 128), F32)
        for pad in (dqpad, ppad):
            pad[0:CONV_PAD, :] = zero
            pad[s_len + CONV_PAD:s_len + 2 * CONV_PAD, :] = zero
        wacc[...] = jnp.zeros_like(wacc)

        def fill(i, carry):
            t0 = pl.multiple_of(i * tt, tt)
            ppad[pl.ds(CONV_PAD + t0, tt), :] = z_ref[0, pl.ds(t0, tt), :] * _sigmoid(z_ref[1, pl.ds(t0, tt), :])
            dqpad[pl.ds(CONV_PAD + t0, tt), :] = dq_ref[pl.ds(t0, tt), :]
            return carry
        lax.fori_loop(0, s_len // tt, fill, 0)
        w = cw_ref[0]

        def bwd(i, carry):
            t0 = pl.multiple_of(i * tt, tt)
            dp = jnp.zeros((tt, 128), F32)
            for o in range(1, CONV_WIDTH + 1):
                dp = dp + w[CONV_WIDTH - o:CONV_WIDTH - o + 1, :] * dqpad[pl.ds(t0 + o, tt), :]
            a = z_ref[0, pl.ds(t0, tt), :]
            sig = _sigmoid(z_ref[1, pl.ds(t0, tt), :])
            dz_ref[0, pl.ds(t0, tt), :] = (dp * sig).astype(BF16)
            dz_ref[1, pl.ds(t0, tt), :] = (dp * a * (sig * (1.0 - sig))).astype(BF16)
            dqt = dq_ref[pl.ds(t0, tt), :]
            for o in range(1, CONV_WIDTH + 1):
                wacc[o - 1] += _colsum8(dqt * ppad[pl.ds(t0 + o, tt), :])
            wacc[CONV_WIDTH] += _colsum8(dqt)
            return carry
        lax.fori_loop(0, s_len // tt, bwd, 0)
        for k in range(32):
            dcw_ref[k:k + 1, :] = jnp.sum(wacc[k], axis=0, keepdims=True)

    const2 = lambda j: (0, 0)
    return pl.pallas_call(
        body, name="conv_bwd",
        grid=(n_blocks,),
        out_shape=(jax.ShapeDtypeStruct(dz6.shape, BF16), jax.ShapeDtypeStruct((32, D_MODEL), F32))
        + tuple(jax.ShapeDtypeStruct(s, F32) for s in shp) + (jax.ShapeDtypeStruct((8, 128), F32),),
        in_specs=[pl.BlockSpec((s_len, 128), lambda j: (0, j)),
                  pl.BlockSpec((2, s_len, 128), lambda j: (2, 0, j)),
                  pl.BlockSpec((1, 32, 128), lambda j: (j // 2, 0, j % 2)),
                  pl.BlockSpec(memory_space=pl.ANY),
                  pl.BlockSpec((8, D_MODEL), const2),
                  pl.BlockSpec((D_MODEL, 128), const2),
                  pl.BlockSpec((128, 128), const2)],
        out_specs=(pl.BlockSpec((2, s_len, 128), lambda j: (2, 0, j)),
                   pl.BlockSpec((32, 128), lambda j: (0, j)))
        + tuple(pl.BlockSpec(s, const2) for s in shp) + (pl.BlockSpec((8, 128), const2),),
        scratch_shapes=[pltpu.VMEM((s_len + 2 * CONV_PAD, 128), F32), pltpu.VMEM((s_len + 2 * CONV_PAD, 128), F32),
                        pltpu.VMEM((32, 8, 128), F32)]
        + [pltpu.VMEM(s, F32) for s in shp] + [pltpu.VMEM((3,) + s, F32) for s in shp]
        + [pltpu.SemaphoreType.DMA((9,)), pltpu.SemaphoreType.DMA((9,))],
        input_output_aliases={3: 0},
        compiler_params=_params(dimension_semantics=("arbitrary",)),
    )(dq, z6, cw4, dz6, sm1, dws, dbs)


def _bwd_in(dz6, w12, x, dx2, mod, norm_g):
    s_len = x.shape[0]
    tm = TOKEN_TILE
    n_steps = s_len // tm

    def body(dz_ref, w_ref, x_ref, dx2_ref, mod_ref, g_ref, gx_ref, sm_ref, dh_scr, acc_scr):
        i = pl.program_id(0)

        @pl.when(i == 0)
        def _():
            acc_scr[...] = jnp.zeros_like(acc_scr)

        dh = jnp.zeros((tm, D_MODEL), F32)
        for j in range(N_GROUPS):
            n = _natural_group(j)
            for hf in range(2):
                dh = dh + _dot_nt(dz_ref[j, :, hf * 512:(hf + 1) * 512], w_ref[2 * n + hf])
        dh_scr[...] = dh
        scale1 = 1.0 + mod_ref[:, D_MODEL:2 * D_MODEL]
        g = g_ref[...]

        def rows_fn(rows):
            xt = x_ref[rows, :]
            r = lax.rsqrt(_mean(xt * xt) + EPS)
            xn = xt * r
            dhr = dh_scr[rows, :]
            acc_scr[0] += _colsum8(dhr)
            acc_scr[1] += _colsum8(dhr * (xn * g))
            acc_scr[2] += _colsum8(dhr * scale1 * xn)
            dxn = dhr * (g * scale1)
            gx_ref[rows, :] = dx2_ref[rows, :] + r * (dxn - xn * _mean(dxn * xn))
        _row_loop(tm, rows_fn, unroll=FULL_UNROLL)

        @pl.when(i == n_steps - 1)
        def _():
            for qi in range(8):
                sm_ref[qi:qi + 1, :] = jnp.sum(acc_scr[qi], axis=0, keepdims=True)

    tok = lambda i: (i, 0)
    const2 = lambda i: (0, 0)
    return pl.pallas_call(
        body, name="bwd_in",
        grid=(n_steps,),
        out_shape=(jax.ShapeDtypeStruct((s_len, D_MODEL), F32), jax.ShapeDtypeStruct((8, D_MODEL), F32)),
        in_specs=[pl.BlockSpec((N_GROUPS, tm, D_MODEL), lambda i: (0, i, 0)),
                  pl.BlockSpec((12, D_MODEL, 512), lambda i: (0, 0, 0), pipeline_mode=pl.Buffered(1)),
                  pl.BlockSpec((tm, D_MODEL), tok),
                  pl.BlockSpec((tm, D_MODEL), tok),
                  pl.BlockSpec((1, 3 * D_MODEL), const2),
                  pl.BlockSpec((1, D_MODEL), const2)],
        out_specs=(pl.BlockSpec((tm, D_MODEL), tok), pl.BlockSpec((8, D_MODEL), const2)),
        scratch_shapes=[pltpu.VMEM((tm, D_MODEL), F32), pltpu.VMEM((8, 8, D_MODEL), F32)],
        compiler_params=_params(dimension_semantics=("arbitrary",)),
    )(dz6, w12, x, dx2, mod, norm_g)


def _grad_reduce_scatter(kidx, a, b, *, name, nj, nr, nc, a_spec, b_spec, b_lead, kt, small=None):
    nblk = N_CHIPS * nj
    nfor = 3 * nj

    def body(k_ref, a_ref, b_ref, *rest):
        del k_ref
        if small is None:
            out_ref, rest = rest[0], rest[1:]
        else:
            sm2_ref, dcw_in, out_ref, pk2_ref, dcwsh_ref, dcb_ref = rest[:6]
            b_pk2, b_dcw, r_pk2, r_dcw, bsend, brecv = rest[-6:]
            rest = rest[6:-6]
        acc, recv_a, send_b, recv_b, own_ps, lsem, psend, precv, isend, irecv, fsend, frecv = rest
        i = pl.program_id(0)
        kk = pl.program_id(1)
        x, y, c = _pos()
        k = 2 * x + y
        sib = (x, y, 1 - c)
        slot = i % 2

        @pl.when(kk == 0)
        def _():
            acc[slot] = jnp.zeros((2, nr, nc), F32)

        if small is not None:
            @pl.when(jnp.logical_and(kk == 0, i == 0))
            def _():
                b = 4 * x + 2 * y + c
                b_pk2[0:8, :] = _onehot_rows(sm2_ref[0:1, :], b)
                b_pk2[8:16, :] = _onehot_rows(sm2_ref[1:2, :], b)
                b_pk2[16:17, :] = sm2_ref[2:3, :]
                b_pk2[17:24, :] = jnp.zeros((7, D_MODEL), F32)
                b_dcw[...] = dcw_in[...]

            for e in range(4):
                @pl.when(jnp.logical_and(kk == 0, i == (e * nblk) // 4))
                def _():
                    _butterfly_event(e, [b_pk2, b_dcw], [r_pk2, r_dcw], bsend, brecv, _all_reduce_partners())

            @pl.when(jnp.logical_and(kk == 0, i == (3 * nblk) // 4))
            def _():
                pk2_ref[...] = b_pk2[...]
                sel = jnp.zeros((32, 256), F32)
                for kc in range(N_CHIPS):
                    sel = jnp.where(k == kc, b_dcw[:, kc * 256:(kc + 1) * 256], sel)
                dcwsh_ref[...] = sel
                dcb_ref[...] = b_dcw[31:32, :]

        def pair_copy(ii, sl):
            return _rcopy(acc.at[sl, 1 - c], recv_a.at[ii], psend.at[ii], precv.at[ii], sib)

        def pair_sum(ii, sl):
            cp = pair_copy(ii, sl)
            cp.wait_recv()
            cp.wait_send()
            return acc[sl, c] + recv_a[ii]

        def ici_copy(ii, kd, rslot):
            return _rcopy(send_b.at[ii], recv_b.at[rslot], isend.at[ii], irecv.at[rslot], (kd // 2, kd % 2, c))

        def finalize(ii, sl):
            ps = pair_sum(ii, sl)
            s = ii // nj
            j = ii % nj

            @pl.when(s < 3)
            def _():
                send_b[ii] = ps.astype(BF16)
                ici_copy(ii, (k + 1 + s) % N_CHIPS, (2 - s) * nj + j).start()

            @pl.when(s == 3)
            def _():
                own_ps[j] = ps

        @pl.when(jnp.logical_and(kk == kt - 1, i >= 1))
        def _():
            finalize(i - 1, 1 - slot)

        res = _dot_tn(a_ref[...], b_ref[0] if b_lead else b_ref[...])
        acc[slot, 0] += res[0:nr]
        acc[slot, 1] += res[nr:2 * nr]

        @pl.when(kk == kt - 1)
        def _():
            pair_copy(i, slot).start()

            @pl.when(i == nblk - 1)
            def _():
                own_ps[nj - 1] = pair_sum(nblk - 1, (nblk - 1) % 2)
                for r in range(nfor):
                    ici_copy(0, k, r).wait_recv()
                for j in range(nj):
                    tot = own_ps[j]
                    for s in range(3):
                        tot = tot + recv_b[s * nj + j].astype(F32)
                    own_ps[j] = tot
                loc = pltpu.make_async_copy(own_ps, out_ref.at[:, c], lsem)
                loc.start()
                swap = _rcopy(own_ps, out_ref.at[:, c], fsend, frecv, sib)
                swap.start()
                loc.wait()
                swap.wait_send()
                _rcopy(own_ps, out_ref.at[:, 1 - c], fsend, frecv, sib).wait_recv()
                for ii in range(nfor):
                    ici_copy(ii, k, 0).wait_send()

    def seq(i, k_ref):
        return (nj * (k_ref[0] + 1) + i) % nblk

    const2 = lambda i, kk, k_ref: (0, 0)
    small_in = [(8, D_MODEL), (32, D_MODEL)]
    small_out = [(24, D_MODEL), (32, 256), (1, D_MODEL)]
    small_buf = [(24, D_MODEL), (32, D_MODEL)]
    grid_spec = pltpu.PrefetchScalarGridSpec(
        num_scalar_prefetch=1,
        grid=(nblk, kt),
        in_specs=[a_spec(seq), b_spec(seq)] + ([pl.BlockSpec(s, const2) for s in small_in] if small else []),
        out_specs=((pl.BlockSpec(memory_space=pl.ANY),) + tuple(pl.BlockSpec(s, const2) for s in small_out) if small
                   else pl.BlockSpec(memory_space=pl.ANY)),
        scratch_shapes=[pltpu.VMEM((2, 2, nr, nc), F32),
                        pltpu.VMEM((nblk, nr, nc), F32),
                        pltpu.VMEM((nfor, nr, nc), BF16),
                        pltpu.VMEM((nfor, nr, nc), BF16),
                        pltpu.VMEM((nj, nr, nc), F32),
                        pltpu.SemaphoreType.DMA,
                        pltpu.SemaphoreType.DMA((nblk,)), pltpu.SemaphoreType.DMA((nblk,)),
                        pltpu.SemaphoreType.DMA((nfor,)), pltpu.SemaphoreType.DMA((nfor,)),
                        pltpu.SemaphoreType.DMA, pltpu.SemaphoreType.DMA]
        + ([pltpu.VMEM(s, F32) for s in small_buf] + [pltpu.VMEM((3,) + s, F32) for s in small_buf]
           + [pltpu.SemaphoreType.DMA((6,)), pltpu.SemaphoreType.DMA((6,))] if small else []))
    grad_shape = jax.ShapeDtypeStruct((nj, 2, nr, nc), F32)
    return pl.pallas_call(
        body, name=name,
        grid_spec=grid_spec,
        out_shape=((grad_shape,) + tuple(jax.ShapeDtypeStruct(s, F32) for s in small_out)) if small else grad_shape,
        compiler_params=_params(dimension_semantics=("arbitrary", "arbitrary")),
    )(*([kidx, a, b] + (list(small) if small else [])))


def _grad_w_in(kidx, h, dz6, sm2, dcw):
    s_len = h.shape[0]
    tk = min(K_TILE, s_len)
    return _grad_reduce_scatter(
        kidx, h, dz6, small=(sm2, dcw), name="grad_w_in", nj=3, nr=512, nc=512, b_lead=True, kt=s_len // tk,
        a_spec=lambda seq: pl.BlockSpec((tk, D_MODEL), lambda i, kk, k_ref: (kk, 0)),
        b_spec=lambda seq: pl.BlockSpec(
            (1, tk, 512), lambda i, kk, k_ref: ((seq(i, k_ref) // 2 + 4) % N_GROUPS, kk, seq(i, k_ref) % 2)))


def _grad_w_out(kidx, ycat, dy):
    s_len = dy.shape[0]
    tk = min(K_TILE, s_len)
    return _grad_reduce_scatter(
        kidx, ycat, dy, name="grad_w_out", nj=1, nr=256, nc=D_MODEL, b_lead=False, kt=s_len // tk,
        a_spec=lambda seq: pl.BlockSpec((tk, 512), lambda i, kk, k_ref: (kk, seq(i, k_ref))),
        b_spec=lambda seq: pl.BlockSpec((tk, D_MODEL), lambda i, kk, k_ref: (kk, 0)))


def _adamw_math(w, g, m, v):
    m = ADAM_B1 * m + (1.0 - ADAM_B1) * g
    v = ADAM_B2 * v + (1.0 - ADAM_B2) * (g * g)
    m_hat = m / (1.0 - ADAM_B1 ** ADAM_STEP)
    v_hat = v / (1.0 - ADAM_B2 ** ADAM_STEP)
    delta = -ADAM_LR * (m_hat / (jnp.sqrt(v_hat) + ADAM_EPS) + ADAM_WD * w)
    return delta, m, v


def _adamw_blocked(w, m, v, g4, name):
    nj, _, nr, nc = g4.shape

    def body(w_ref, m_ref, v_ref, g_ref, go_ref, d_ref, mo_ref, vo_ref):
        g = g_ref[0, 0]
        d, mn, vn = _adamw_math(w_ref[...], g, m_ref[...], v_ref[...])
        go_ref[...] = g
        d_ref[...] = d
        mo_ref[...] = mn
        vo_ref[...] = vn

    blk = pl.BlockSpec((nr, nc), lambda j, hf: (hf, j))
    return pl.pallas_call(
        body, name=name,
        grid=(nj, 2),
        out_shape=tuple(jax.ShapeDtypeStruct(w.shape, F32) for _ in range(4)),
        in_specs=[blk, blk, blk, pl.BlockSpec((1, 1, nr, nc), lambda j, hf: (j, hf, 0, 0))],
        out_specs=(blk, blk, blk, blk),
        compiler_params=_params(dimension_semantics=("arbitrary", "arbitrary")),
    )(w, m, v, g4)


def _adamw_w_ada(cact, pk1, pk2, w, m, v):
    def body(cact_ref, pk1_ref, pk2_ref, w_ref, m_ref, v_ref, g_ref, d_ref, mo_ref, vo_ref, gb_ref, dmod):
        x, y, _ = _pos()
        k = 2 * x + y
        dmod[:, 0:D_MODEL] = pk2_ref[0:8, :]
        dmod[:, D_MODEL:2 * D_MODEL] = pk2_ref[8:16, :]
        dmod[:, 2 * D_MODEL:3 * D_MODEL] = pk1_ref[0:8, :]
        gb_ref[...] = jnp.sum(dmod[...], axis=0, keepdims=True)
        sel = jnp.zeros((8, 768), F32)
        for kk in range(N_CHIPS):
            sel = jnp.where(k == kk, dmod[:, kk * 768:(kk + 1) * 768], sel)
        g = _dot_tn(cact_ref[...].astype(BF16), sel.astype(BF16))
        d, mn, vn = _adamw_math(w_ref[...], g, m_ref[...], v_ref[...])
        g_ref[...] = g
        d_ref[...] = d
        mo_ref[...] = mn
        vo_ref[...] = vn

    return pl.pallas_call(
        body, name="adamw_w_ada",
        out_shape=tuple(jax.ShapeDtypeStruct(w.shape, F32) for _ in range(4)) + (
            jax.ShapeDtypeStruct((1, 3 * D_MODEL), F32),),
        in_specs=[_vmem()] * 6,
        out_specs=tuple(_vmem() for _ in range(5)),
        scratch_shapes=[pltpu.VMEM((8, 3 * D_MODEL), F32)],
        compiler_params=_params(),
    )(cact, pk1, pk2, w, m, v)


def _adamw_small(items):
    n = len(items)

    def body(*refs):
        ins, outs = refs[:4 * n], refs[4 * n:]
        for i in range(n):
            w_ref, g_ref, m_ref, v_ref = ins[4 * i:4 * i + 4]
            d, mn, vn = _adamw_math(w_ref[...], g_ref[...], m_ref[...], v_ref[...])
            outs[3 * i][...] = d
            outs[3 * i + 1][...] = mn
            outs[3 * i + 2][...] = vn

    flat = [a for it in items for a in it]
    outs = pl.pallas_call(
        body, name="adamw_small",
        out_shape=tuple(jax.ShapeDtypeStruct(it[0].shape, F32) for it in items for _ in range(3)),
        in_specs=[_vmem()] * (4 * n),
        out_specs=tuple(_vmem() for _ in range(3 * n)),
        compiler_params=_params(),
    )(*flat)
    return [tuple(outs[3 * i:3 * i + 3]) for i in range(n)]


def kernel(x, c, w_ada, b_ada, norm_g, w_in, conv_w, conv_b, conv_ln_g, conv_ln_b, sg_ln_g, sg_ln_b, w_s, b_s, w_out, final_g, loss_target, m_w_ada, m_b_ada, m_norm_g, m_w_in, m_conv_w, m_conv_b, m_conv_ln_g, m_conv_ln_b, m_sg_ln_g, m_sg_ln_b, m_w_s, m_b_s, m_w_out, m_final_g, v_w_ada, v_b_ada, v_norm_g, v_w_in, v_conv_w, v_conv_b, v_conv_ln_g, v_conv_ln_b, v_sg_ln_g, v_sg_ln_b, v_w_s, v_b_s, v_w_out, v_final_g):
    s_len = x.shape[1]
    x2d = x[0]
    tgt = loss_target[0]
    row = lambda a: a.reshape(1, -1)

    kidx = (2 * lax.axis_index("x") + lax.axis_index("y")).astype(jnp.int32).reshape(1)
    cw_sh = jnp.pad(conv_w.reshape(CONV_WIDTH, 256), ((0, 1), (0, 0)))
    h, z6, mod, cact, cw4, w_in12 = _front(kidx, x2d, c, w_ada[0], b_ada, norm_g, w_in[0], cw_sh)
    w12 = w_in12.reshape(12, D_MODEL, 512)
    q, w_out4 = _conv_fwd(z6, cw4, conv_b, w_out[0])
    w_out_full = w_out4.reshape(2 * D_MODEL, D_MODEL)
    ln_rows = (conv_ln_g, conv_ln_b, sg_ln_g, sg_ln_b, row(final_g))
    bs_exp = jnp.repeat(b_s[0].T, CHUNK, axis=1)
    dz6, dq, dx2, ycat, dy, dws, sm1, dbs = _middle(x2d, z6, q, tgt, ln_rows, mod, w_s[0], bs_exp, w_out_full)
    g_w_out4 = _grad_w_out(kidx, ycat, dy)
    dz6, dcw, pk1, dws_r, dbs_r, loss_t = _conv_bwd(dq, z6, cw4, dz6, sm1, dws.reshape(D_MODEL, CHUNK), dbs)
    grad_x, sm2 = _bwd_in(dz6, w12, x2d, dx2, mod, norm_g)
    g_w_in4, pk2, dcw_sh, dcb = _grad_w_in(kidx, h, dz6, sm2, dcw)

    g_w_in, d_w_in, nm_w_in, nv_w_in = _adamw_blocked(w_in[0], m_w_in[0], v_w_in[0], g_w_in4, "adamw_w_in")
    g_w_out, d_w_out, nm_w_out, nv_w_out = _adamw_blocked(w_out[0], m_w_out[0], v_w_out[0], g_w_out4, "adamw_w_out")
    g_w_ada, d_w_ada, nm_w_ada, nv_w_ada, g_b_ada = _adamw_w_ada(cact, pk1, pk2, w_ada[0], m_w_ada[0], v_w_ada[0])

    g_norm_g = pk2[16:17]
    g_cln_g, g_cln_b, g_sln_g, g_sln_b, g_final = (pk1[8 + i:9 + i] for i in range(5))
    loss = loss_t[0, 0]
    g_conv_w = dcw_sh[:CONV_WIDTH]
    g_w_s = dws_r
    g_b_s = dbs_r[:, :HEADS].T
    small = [
        (b_ada, g_b_ada, m_b_ada, v_b_ada),
        (norm_g, g_norm_g, m_norm_g, v_norm_g),
        (conv_w.reshape(CONV_WIDTH, 256), g_conv_w, m_conv_w.reshape(CONV_WIDTH, 256), v_conv_w.reshape(CONV_WIDTH, 256)),
        (conv_b, dcb, m_conv_b, v_conv_b),
        (conv_ln_g, g_cln_g, m_conv_ln_g, v_conv_ln_g),
        (conv_ln_b, g_cln_b, m_conv_ln_b, v_conv_ln_b),
        (sg_ln_g, g_sln_g, m_sg_ln_g, v_sg_ln_g),
        (sg_ln_b, g_sln_b, m_sg_ln_b, v_sg_ln_b),
        (w_s.reshape(D_MODEL, CHUNK), g_w_s, m_w_s.reshape(D_MODEL, CHUNK), v_w_s.reshape(D_MODEL, CHUNK)),
        (b_s[0], g_b_s, m_b_s[0], v_b_s[0]),
        (row(final_g), g_final, row(m_final_g), row(v_final_g)),
    ]
    upd = _adamw_small(small)

    shapes = [w_ada.shape, b_ada.shape, norm_g.shape, w_in.shape, conv_w.shape, conv_b.shape, conv_ln_g.shape,
              conv_ln_b.shape, sg_ln_g.shape, sg_ln_b.shape, w_s.shape, b_s.shape, w_out.shape, final_g.shape]
    grads = [g_w_ada, g_b_ada, g_norm_g, g_w_in, g_conv_w, dcb, g_cln_g, g_cln_b, g_sln_g, g_sln_b, g_w_s, g_b_s,
             g_w_out, g_final]
    big = {0: (d_w_ada, nm_w_ada, nv_w_ada), 3: (d_w_in, nm_w_in, nv_w_in), 12: (d_w_out, nm_w_out, nv_w_out)}
    small_pos = [1, 2, 4, 5, 6, 7, 8, 9, 10, 11, 13]
    trip = [None] * 14
    for i, t in big.items():
        trip[i] = t
    for i, t in zip(small_pos, upd):
        trip[i] = t
    fit = lambda arrs: [a.reshape(s) for a, s in zip(arrs, shapes)]
    return (loss, grad_x.reshape(x.shape), *fit(grads), *fit([t[0] for t in trip]), *fit([t[1] for t in trip]),
            *fit([t[2] for t in trip]))
```

```python
import jax
import jax.numpy as jnp
from jax import lax
from jax.experimental import pallas as pl
from jax.experimental.pallas import tpu as pltpu

F32 = jnp.float32
BF16 = jnp.bfloat16
MESH = pl.DeviceIdType.MESH

D_MODEL = 1024
N_CHIPS = 4
HEADS = 8
CHUNK = 128
CONV_WIDTH = 31
CONV_HALF = CONV_WIDTH // 2
CONV_PAD = 16
EPS = 1e-6
ADAM_LR = 0.001
ADAM_B1 = 0.9
ADAM_B2 = 0.999
ADAM_EPS = 1e-08
ADAM_WD = 0.01
ADAM_STEP = 10

V7X_VMEM_BYTES = 64 * 1024 * 1024
VMEM_LIMIT = V7X_VMEM_BYTES - 8 * 1024 * 1024
ROWS = 16
UNROLL = 8
TOKEN_TILE = 256
FULL_UNROLL = TOKEN_TILE // ROWS
TIME_TILE = 128
K_TILE = 2048

N_GROUPS = 6


def _natural_group(j):
    return (j + 2) % N_GROUPS


def _pos():
    return lax.axis_index("x"), lax.axis_index("y"), lax.axis_index("c")


def _rcopy(src, dst, ssem, rsem, dev):
    return pltpu.make_async_remote_copy(src_ref=src, dst_ref=dst, send_sem=ssem, recv_sem=rsem,
                                        device_id=dev, device_id_type=MESH)


def _vmem():
    return pl.BlockSpec(memory_space=pltpu.VMEM)


def _params(**kw):
    return pltpu.CompilerParams(vmem_limit_bytes=VMEM_LIMIT, **kw)


def _sigmoid(v):
    return 0.5 * jnp.tanh(0.5 * v) + 0.5


def _row_loop(n_rows, body, unroll=1):
    def step(r, carry):
        body(pl.ds(pl.multiple_of(r * ROWS, ROWS), ROWS))
        return carry
    lax.fori_loop(0, n_rows // ROWS, step, 0, unroll=unroll)


def _colsum8(v):
    return v.reshape(v.shape[0] // 8, 8, v.shape[1]).sum(axis=0)


def _mean(v):
    return jnp.mean(v, axis=-1, keepdims=True)


def _dot_nn(a, b):
    return jnp.dot(a, b, preferred_element_type=F32)


def _dot_nt(a, b):
    return lax.dot_general(a, b, (((1,), (1,)), ((), ())), preferred_element_type=F32)


def _dot_tn(a, b):
    return lax.dot_general(a, b, (((0,), (0,)), ((), ())), preferred_element_type=F32)


def _remote_chip(k, r):
    return jnp.bitwise_xor(k, r + 1)


def _front(kidx, x, c, w_ada, b_ada, norm_g, w_in, cw):
    s_len = x.shape[0]
    tmh = min(512, s_len)
    tmz = min(1024, s_len)
    nh = s_len // tmh
    nz = s_len // tmz
    n_steps = nh + 12 * nz

    def remote_block(q):
        return jnp.where(q < 6, q % 2, 2), jnp.where(q < 6, q // 2, q - 6)

    def block_of(i, k):
        r, j = remote_block(jnp.maximum(i - 3, 0))
        return jnp.where(i < 3, 3 * k + i, 3 * _remote_chip(k, r) + j)

    def body(k_ref, x_ref, c_ref, bada_ref, g_ref, cw_ref, wada_hbm, win_hbm,
             h_ref, z_ref, mod_ref, cact_ref, cw4_ref, w12_hbm,
             h_all, wbuf, stage, wada_v, cslab, mslab, cw4_s,
             lsem, csend, crecv, msend, mrecv, wsend, wrecv, isend, irecv, fsend, frecv, osem):
        del k_ref
        t = pl.program_id(0)
        x_, y_, c_ = _pos()
        k = 2 * x_ + y_
        b = 4 * x_ + 2 * y_ + c_
        sib = (x_, y_, 1 - c_)

        def dev_of(r):
            kk = _remote_chip(k, r)
            return (kk // 2, kk % 2, c_)

        def ici(q, kk):
            r, j = remote_block(q)
            return _rcopy(wbuf.at[3 * kk + j, c_], wbuf.at[3 * kk + j, c_], isend.at[q], irecv.at[q], dev_of(r))

        def fwd(q, hf):
            r, j = remote_block(q)
            blk = 3 * _remote_chip(k, r) + j
            return _rcopy(wbuf.at[blk, hf], wbuf.at[blk, hf], fsend.at[q], frecv.at[q], sib)

        def c_copy(q, src):
            d = jnp.bitwise_xor(b, q)
            return _rcopy(cslab.at[src], cslab.at[src], csend.at[q - 1], crecv.at[q - 1], (d // 4, (d // 2) % 2, d % 2))

        def m_copy(r, kk):
            return _rcopy(mslab.at[kk], mslab.at[kk], msend.at[r], mrecv.at[r], dev_of(r))

        def cw_copy(r, kk):
            return _rcopy(cw4_s.at[kk], cw4_s.at[kk], wsend.at[r], wrecv.at[r], dev_of(r))

        def to_hbm(i):
            m = block_of(i, k)
            return pltpu.make_async_copy(wbuf.at[m], w12_hbm.at[m], osem.at[i])

        @pl.when(t == 0)
        def _():
            ld_w = pltpu.make_async_copy(win_hbm, stage, lsem.at[0])
            ld_w.start()
            ld_a = pltpu.make_async_copy(wada_hbm, wada_v, lsem.at[1])
            ld_a.start()
            cslab[b] = jnp.broadcast_to(c_ref[...], (8, D_MODEL))
            for q in range(1, 8):
                c_copy(q, b).start()
            cw4_s[k] = cw_ref[...]
            for r in range(3):
                cw_copy(r, k).start()
            ld_w.wait()
            for j in range(3):
                for hf in range(2):
                    wbuf[3 * k + j, hf] = stage[hf * 512:(hf + 1) * 512, j * 512:(j + 1) * 512].astype(BF16)
            for i in range(3):
                to_hbm(i).start()
            for q in range(1, 8):
                c_copy(q, jnp.bitwise_xor(b, q)).wait_recv()
            row = lax.broadcasted_iota(jnp.int32, (8, D_MODEL), 0)
            call = jnp.zeros((8, D_MODEL), F32)
            for d in range(8):
                call = jnp.where(row == d, cslab[d], call)
            cact = call * _sigmoid(call)
            cact_ref[...] = cact
            ld_a.wait()
            mslab[k] = _dot_nn(cact.astype(BF16), wada_v[...].astype(BF16))
            for r in range(3):
                m_copy(r, k).start()
            for q in range(9):
                ici(q, k).start()
            for r in range(3):
                m_copy(r, _remote_chip(k, r)).wait_recv()
            row8 = lax.broadcasted_iota(jnp.int32, (8, 768), 0)
            for kk in range(N_CHIPS):
                piece = jnp.sum(jnp.where(row8 == b, mslab[kk], 0.0), axis=0, keepdims=True)
                mod_ref[:, kk * 768:(kk + 1) * 768] = piece + bada_ref[:, kk * 768:(kk + 1) * 768]
            for r in range(3):
                cw_copy(r, _remote_chip(k, r)).wait_recv()
            cw4_ref[...] = cw4_s[...]

        @pl.when(t < nh)
        def _():
            shift = mod_ref[:, 0:D_MODEL]
            scale1 = 1.0 + mod_ref[:, D_MODEL:2 * D_MODEL]
            g = g_ref[...]
            base = t * tmh

            def rows_fn(rows):
                xt = x_ref[rows, :]
                r = lax.rsqrt(_mean(xt * xt) + EPS)
                hv = ((xt * r * g) * scale1 + shift).astype(BF16)
                h_ref[rows, :] = hv
                h_all[pl.ds(pl.multiple_of(base + rows.start, ROWS), ROWS), :] = hv
            _row_loop(tmh, rows_fn, unroll=UNROLL)

        @pl.when(t >= nh)
        def _():
            u = t - nh
            i = u // nz
            rt = u % nz
            @pl.when(jnp.logical_and(rt == 0, jnp.logical_and(i >= 2, i <= 10)))
            def _():
                q = i - 2
                r, _ = remote_block(q)
                ici(q, _remote_chip(k, r)).wait_recv()
                fwd(q, c_).start()

            @pl.when(jnp.logical_and(rt == 0, i >= 3))
            def _():
                fwd(i - 3, 1 - c_).wait_recv()
                to_hbm(i).start()
            m = block_of(i, k)
            hb = h_all[pl.ds(pl.multiple_of(rt * tmz, tmz), tmz), :]
            z_ref[0] = _dot_nn(hb[:, 0:512], wbuf[m, 0]) + _dot_nn(hb[:, 512:1024], wbuf[m, 1])

        @pl.when(t == n_steps - 1)
        def _():
            for q in range(1, 8):
                c_copy(q, b).wait_send()
            for r in range(3):
                m_copy(r, k).wait_send()
                cw_copy(r, k).wait_send()
            for q in range(9):
                ici(q, k).wait_send()
                fwd(q, c_).wait_send()
            for i in range(12):
                to_hbm(i).wait()

    def z_index(t, k_ref):
        u = jnp.maximum(t - nh, 0)
        m = block_of(u // nz, k_ref[0])
        return ((m // 2 + 4) % N_GROUPS, u % nz, m % 2)

    tok = lambda t, k_ref: (jnp.minimum(t, nh - 1), 0)
    const2 = lambda t, k_ref: (0, 0)
    grid_spec = pltpu.PrefetchScalarGridSpec(
        num_scalar_prefetch=1,
        grid=(n_steps,),
        in_specs=[pl.BlockSpec((tmh, D_MODEL), tok),
                  pl.BlockSpec((1, D_MODEL), const2),
                  pl.BlockSpec((1, 3 * D_MODEL), const2),
                  pl.BlockSpec((1, D_MODEL), const2),
                  pl.BlockSpec((32, 256), const2),
                  pl.BlockSpec(memory_space=pl.ANY),
                  pl.BlockSpec(memory_space=pl.ANY)],
        out_specs=(pl.BlockSpec((tmh, D_MODEL), tok),
                   pl.BlockSpec((1, tmz, 512), z_index),
                   pl.BlockSpec((1, 3 * D_MODEL), const2),
                   pl.BlockSpec((8, D_MODEL), const2),
                   pl.BlockSpec((N_CHIPS, 32, 256), lambda t, k_ref: (0, 0, 0)),
                   pl.BlockSpec(memory_space=pl.ANY)),
        scratch_shapes=[pltpu.VMEM((s_len, D_MODEL), BF16),
                        pltpu.VMEM((12, 2, 512, 512), BF16),
                        pltpu.VMEM((D_MODEL, 1536), F32),
                        pltpu.VMEM((D_MODEL, 768), F32),
                        pltpu.VMEM((8, 8, D_MODEL), F32),
                        pltpu.VMEM((N_CHIPS, 8, 768), F32),
                        pltpu.VMEM((N_CHIPS, 32, 256), F32),
                        pltpu.SemaphoreType.DMA((2,)),
                        pltpu.SemaphoreType.DMA((7,)), pltpu.SemaphoreType.DMA((7,)),
                        pltpu.SemaphoreType.DMA((3,)), pltpu.SemaphoreType.DMA((3,)),
                        pltpu.SemaphoreType.DMA((3,)), pltpu.SemaphoreType.DMA((3,)),
                        pltpu.SemaphoreType.DMA((9,)), pltpu.SemaphoreType.DMA((9,)),
                        pltpu.SemaphoreType.DMA((9,)), pltpu.SemaphoreType.DMA((9,)),
                        pltpu.SemaphoreType.DMA((12,))])
    return pl.pallas_call(
        body, name="front",
        grid_spec=grid_spec,
        out_shape=(jax.ShapeDtypeStruct((s_len, D_MODEL), BF16),
                   jax.ShapeDtypeStruct((N_GROUPS, s_len, D_MODEL), F32),
                   jax.ShapeDtypeStruct((1, 3 * D_MODEL), F32),
                   jax.ShapeDtypeStruct((8, D_MODEL), F32),
                   jax.ShapeDtypeStruct((N_CHIPS, 32, 256), F32),
                   jax.ShapeDtypeStruct((12, 2, 512, 512), BF16)),
        compiler_params=_params(dimension_semantics=("arbitrary",)),
    )(kidx, x, c, b_ada, norm_g, cw, w_ada, w_in)


def _all_reduce_partners():
    x, y, c = _pos()
    return [(x, y, 1 - c), (x, 1 - y, c), (1 - x, y, c)]


def _butterfly_event(e, bufs, recvs, ssem, rsem, partners):
    n = len(bufs)

    def copies(s):
        return [_rcopy(buf, recv.at[s], ssem.at[s * n + i], rsem.at[s * n + i], partners[s])
                for i, (buf, recv) in enumerate(zip(bufs, recvs))]

    if e > 0:
        for cp in copies(e - 1):
            cp.wait()
        for buf, recv in zip(bufs, recvs):
            buf[...] = buf[...] + recv[e - 1]
    if e < len(partners):
        for cp in copies(e):
            cp.start()


def _onehot_rows(v, b):
    row = lax.broadcasted_iota(jnp.int32, (8, v.shape[1]), 0)
    return jnp.where(row == b, jnp.broadcast_to(v, (8, v.shape[1])), 0.0)


def _conv_fwd(z6, cw4, conv_b, w_out):
    s_len = z6.shape[1]
    tt = TIME_TILE
    n_blocks = D_MODEL // 128

    def body(z_ref, cw_ref, cb_ref, wout_hbm, q_ref, wout4_hbm, ppad, stage, wbuf, lsem, isend, irecv, fsend, frecv, osem):
        jb = pl.program_id(0)
        x_, y_, c_ = _pos()
        k = 2 * x_ + y_
        sib = (x_, y_, 1 - c_)

        def ici(r, kk):
            rk = _remote_chip(k, r)
            return _rcopy(wbuf.at[kk, c_], wbuf.at[kk, c_], isend.at[r], irecv.at[r], (rk // 2, rk % 2, c_))

        def fwd(r, hf):
            kk = _remote_chip(k, r)
            return _rcopy(wbuf.at[kk, hf], wbuf.at[kk, hf], fsend.at[r], frecv.at[r], sib)

        @pl.when(jb == 0)
        def _():
            ld = pltpu.make_async_copy(wout_hbm, stage, lsem)
            ld.start()
            ld.wait()
            for hf in range(2):
                wbuf[k, hf] = stage[hf * 256:(hf + 1) * 256, :].astype(BF16)
            for r in range(3):
                ici(r, k).start()

        @pl.when(jb == (5 * n_blocks) // 8)
        def _():
            for r in range(3):
                ici(r, _remote_chip(k, r)).wait_recv()
                fwd(r, c_).start()

        zero = jnp.zeros((CONV_PAD, 128), F32)
        ppad[0:CONV_PAD, :] = zero
        ppad[s_len + CONV_PAD:s_len + 2 * CONV_PAD, :] = zero

        def fill(i, carry):
            t0 = pl.multiple_of(i * tt, tt)
            ppad[pl.ds(CONV_PAD + t0, tt), :] = z_ref[0, pl.ds(t0, tt), :] * _sigmoid(z_ref[1, pl.ds(t0, tt), :])
            return carry
        lax.fori_loop(0, s_len // tt, fill, 0)
        w = cw_ref[0]
        bias = cb_ref[...]

        def conv(i, carry):
            t0 = pl.multiple_of(i * tt, tt)
            acc = jnp.broadcast_to(bias, (tt, 128))
            for o in range(1, CONV_WIDTH + 1):
                acc = acc + w[o - 1:o, :] * ppad[pl.ds(t0 + o, tt), :]
            q_ref[pl.ds(t0, tt), :] = acc
            return carry
        lax.fori_loop(0, s_len // tt, conv, 0)

        @pl.when(jb == n_blocks - 1)
        def _():
            for r in range(3):
                fwd(r, 1 - c_).wait_recv()
            out = pltpu.make_async_copy(wbuf, wout4_hbm, osem)
            out.start()
            for r in range(3):
                ici(r, k).wait_send()
                fwd(r, c_).wait_send()
            out.wait()

    return pl.pallas_call(
        body, name="conv_fwd",
        grid=(n_blocks,),
        out_shape=(jax.ShapeDtypeStruct((s_len, D_MODEL), F32), jax.ShapeDtypeStruct((N_CHIPS, 2, 256, D_MODEL), BF16)),
        in_specs=[pl.BlockSpec((2, s_len, 128), lambda j: (2, 0, j)),
                  pl.BlockSpec((1, 32, 128), lambda j: (j // 2, 0, j % 2)),
                  pl.BlockSpec((1, 128), lambda j: (0, j)),
                  pl.BlockSpec(memory_space=pl.ANY)],
        out_specs=(pl.BlockSpec((s_len, 128), lambda j: (0, j)), pl.BlockSpec(memory_space=pl.ANY)),
        scratch_shapes=[pltpu.VMEM((s_len + 2 * CONV_PAD, 128), F32),
                        pltpu.VMEM((512, D_MODEL), F32),
                        pltpu.VMEM((N_CHIPS, 2, 256, D_MODEL), BF16),
                        pltpu.SemaphoreType.DMA,
                        pltpu.SemaphoreType.DMA((3,)), pltpu.SemaphoreType.DMA((3,)),
                        pltpu.SemaphoreType.DMA((3,)), pltpu.SemaphoreType.DMA((3,)),
                        pltpu.SemaphoreType.DMA],
        compiler_params=_params(dimension_semantics=("arbitrary",)),
    )(z6, cw4, conv_b, w_out)


def _middle(x, z6, q, target, ln_rows, mod, w_s, bs_exp, w_out):
    s_len = x.shape[0]
    tm = TOKEN_TILE
    n_steps = s_len // tm
    n_chunks = tm // CHUNK
    inv_d = 1.0 / D_MODEL

    def body(x_ref, z_ref, q_ref, tgt_ref, cg_ref, cb_ref, sg_ref, sb_ref, fg_ref, mod_ref, ws_ref, bs_ref, wout_ref,
             dz_ref, dq_ref, dx2_ref, ycat_ref, dy_ref, dws_ref, sm_ref, dbs_ref,
             vl_scr, vm_scr, y_scr, dycat_scr, dvm_scr, dvl_scr, acc_scr, dbs_acc):
        i = pl.program_id(0)

        @pl.when(i == 0)
        def _():
            acc_scr[...] = jnp.zeros_like(acc_scr)
            dbs_acc[...] = jnp.zeros_like(dbs_acc)
            dws_ref[...] = jnp.zeros_like(dws_ref)

        cg, cb, sg, sb, fg = cg_ref[...], cb_ref[...], sg_ref[...], sb_ref[...], fg_ref[...]
        gm = mod_ref[:, 2 * D_MODEL:3 * D_MODEL]

        def norm_stats(t):
            c = t - _mean(t)
            rstd = lax.rsqrt(_mean(c * c) + EPS)
            return c * rstd, rstd

        def phase1(rows):
            qhat, _ = norm_stats(q_ref[rows, :])
            ln = qhat * cg + cb
            gz = z_ref[0, rows, :]
            ycat_ref[rows, 0:D_MODEL] = ((ln * _sigmoid(ln)) * (gz * _sigmoid(gz))).astype(BF16)
            vhat, _ = norm_stats(z_ref[2, rows, :])
            vl_scr[rows, :] = (vhat * sg + sb).astype(BF16)
        _row_loop(tm, phase1, unroll=FULL_UNROLL)

        for ch in range(n_chunks):
            r0 = ch * CHUNK
            for h in range(HEADS):
                c0 = h * CHUNK
                vm_scr[r0:r0 + CHUNK, c0:c0 + CHUNK] = (
                    _dot_nn(ws_ref[h].astype(BF16), vl_scr[r0:r0 + CHUNK, c0:c0 + CHUNK]) + bs_ref[:, c0:c0 + CHUNK])

        def phase3(rows):
            bg = z_ref[3, rows, :]
            ycat_ref[rows, D_MODEL:2 * D_MODEL] = (z_ref[1, rows, :] * vm_scr[rows, :] * (bg * _sigmoid(bg))).astype(BF16)
        _row_loop(tm, phase3, unroll=FULL_UNROLL)

        y_scr[...] = _dot_nn(ycat_ref[...], wout_ref[...])

        def phase5(rows):
            y = y_scr[rows, :]
            x2 = x_ref[rows, :] + gm * y
            r2 = lax.rsqrt(_mean(x2 * x2) + EPS)
            xn2 = x2 * r2
            diff = xn2 * fg - tgt_ref[rows, :]
            acc_scr[6] += _colsum8(diff * diff)
            dout = diff * inv_d
            acc_scr[0] += _colsum8(dout * xn2)
            dxn = dout * fg
            dx2 = r2 * (dxn - xn2 * _mean(dxn * xn2))
            dx2_ref[rows, :] = dx2
            acc_scr[1] += _colsum8(dx2 * y)
            dy_ref[rows, :] = (dx2 * gm).astype(BF16)
        _row_loop(tm, phase5, unroll=FULL_UNROLL)

        dycat_scr[...] = _dot_nt(dy_ref[...], wout_ref[...])

        def phase7(rows):
            dyb = dycat_scr[rows, D_MODEL:2 * D_MODEL]
            u = z_ref[1, rows, :]
            bg = z_ref[3, rows, :]
            vm = vm_scr[rows, :]
            sig = _sigmoid(bg)
            silu = bg * sig
            dz_ref[1, rows, :] = (dyb * vm * silu).astype(BF16)
            dvm = dyb * u * silu
            dz_ref[3, rows, :] = (dyb * u * vm * (sig * (1.0 + bg * (1.0 - sig)))).astype(BF16)
            dvm_scr[rows, :] = dvm.astype(BF16)
            pos = pl.ds(pl.multiple_of(rows.start % CHUNK, ROWS), ROWS)
            dbs_acc[pos, :] += dvm
        _row_loop(tm, phase7, unroll=FULL_UNROLL)

        for ch in range(n_chunks):
            r0 = ch * CHUNK
            for h in range(HEADS):
                c0 = h * CHUNK
                dvm_b = dvm_scr[r0:r0 + CHUNK, c0:c0 + CHUNK]
                dws_ref[h] += _dot_nt(dvm_b, vl_scr[r0:r0 + CHUNK, c0:c0 + CHUNK])
                dvl_scr[r0:r0 + CHUNK, c0:c0 + CHUNK] = _dot_tn(ws_ref[h].astype(BF16), dvm_b)

        def phase9(rows):
            vhat, rstd_v = norm_stats(z_ref[2, rows, :])
            dvl = dvl_scr[rows, :]
            acc_scr[4] += _colsum8(dvl * vhat)
            acc_scr[5] += _colsum8(dvl)
            dvh = dvl * sg
            dz_ref[2, rows, :] = (rstd_v * (dvh - _mean(dvh) - vhat * _mean(dvh * vhat))).astype(BF16)
            qhat, rstd_q = norm_stats(q_ref[rows, :])
            ln = qhat * cg + cb
            sig_ln = _sigmoid(ln)
            gz = z_ref[0, rows, :]
            sig_g = _sigmoid(gz)
            dya = dycat_scr[rows, 0:D_MODEL]
            dz_ref[0, rows, :] = (dya * (ln * sig_ln) * (sig_g * (1.0 + gz * (1.0 - sig_g)))).astype(BF16)
            dln = (dya * (gz * sig_g)) * (sig_ln * (1.0 + ln * (1.0 - sig_ln)))
            acc_scr[2] += _colsum8(dln * qhat)
            acc_scr[3] += _colsum8(dln)
            dqh = dln * cg
            dq_ref[rows, :] = rstd_q * (dqh - _mean(dqh) - qhat * _mean(dqh * qhat))
        _row_loop(tm, phase9, unroll=FULL_UNROLL)

        @pl.when(i == n_steps - 1)
        def _():
            for qi in range(8):
                scale = 0.5 * inv_d if qi == 6 else 1.0
                sm_ref[qi:qi + 1, :] = jnp.sum(acc_scr[qi], axis=0, keepdims=True) * scale
            lane = lax.broadcasted_iota(jnp.int32, (CHUNK, CHUNK), 1)
            tile = jnp.zeros((CHUNK, CHUNK), F32)
            for h in range(HEADS):
                col = jnp.sum(dbs_acc[:, h * CHUNK:(h + 1) * CHUNK], axis=1, keepdims=True)
                tile = jnp.where(lane == h, col, tile)
            dbs_ref[...] = tile

    tok = lambda i: (i, 0)
    const2 = lambda i: (0, 0)
    return pl.pallas_call(
        body, name="middle",
        grid=(n_steps,),
        out_shape=(jax.ShapeDtypeStruct((N_GROUPS, s_len, D_MODEL), BF16),
                   jax.ShapeDtypeStruct((s_len, D_MODEL), F32),
                   jax.ShapeDtypeStruct((s_len, D_MODEL), F32),
                   jax.ShapeDtypeStruct((s_len, 2 * D_MODEL), BF16),
                   jax.ShapeDtypeStruct((s_len, D_MODEL), BF16),
                   jax.ShapeDtypeStruct((HEADS, CHUNK, CHUNK), F32),
                   jax.ShapeDtypeStruct((8, D_MODEL), F32),
                   jax.ShapeDtypeStruct((CHUNK, CHUNK), F32)),
        in_specs=[pl.BlockSpec((tm, D_MODEL), tok),
                  pl.BlockSpec((4, tm, D_MODEL), lambda i: (0, i, 0)),
                  pl.BlockSpec((tm, D_MODEL), tok),
                  pl.BlockSpec((tm, D_MODEL), tok),
                  *[pl.BlockSpec((1, D_MODEL), const2) for _ in range(5)],
                  pl.BlockSpec((1, 3 * D_MODEL), const2),
                  pl.BlockSpec((HEADS, CHUNK, CHUNK), lambda i: (0, 0, 0)),
                  pl.BlockSpec((CHUNK, D_MODEL), const2),
                  pl.BlockSpec((2 * D_MODEL, D_MODEL), const2, pipeline_mode=pl.Buffered(1))],
        out_specs=(pl.BlockSpec((4, tm, D_MODEL), lambda i: (0, i, 0)),
                   pl.BlockSpec((tm, D_MODEL), tok),
                   pl.BlockSpec((tm, D_MODEL), tok),
                   pl.BlockSpec((tm, 2 * D_MODEL), tok),
                   pl.BlockSpec((tm, D_MODEL), tok),
                   pl.BlockSpec((HEADS, CHUNK, CHUNK), lambda i: (0, 0, 0)),
                   pl.BlockSpec((8, D_MODEL), const2),
                   pl.BlockSpec((CHUNK, CHUNK), const2)),
        scratch_shapes=[pltpu.VMEM((tm, D_MODEL), BF16),
                        pltpu.VMEM((tm, D_MODEL), F32),
                        pltpu.VMEM((tm, D_MODEL), F32),
                        pltpu.VMEM((tm, 2 * D_MODEL), F32),
                        pltpu.VMEM((tm, D_MODEL), BF16),
                        pltpu.VMEM((tm, D_MODEL), F32),
                        pltpu.VMEM((8, 8, D_MODEL), F32),
                        pltpu.VMEM((CHUNK, D_MODEL), F32)],
        compiler_params=_params(dimension_semantics=("arbitrary",)),
    )(x, z6, q, target, *ln_rows, mod, w_s, bs_exp, w_out)


def _conv_bwd(dq, z6, cw4, dz6, sm1, dws, dbs):
    s_len = dq.shape[0]
    tt = TIME_TILE
    n_blocks = D_MODEL // 128
    shp = [(16, D_MODEL), (D_MODEL, 128), (128, 128)]

    def body(dq_ref, z_ref, cw_ref, dz_in, sm1_ref, dws_in, dbs_in,
             dz_ref, dcw_ref, pk1_ref, dws_ref, dbs_ref, loss_ref,
             dqpad, ppad, wacc, b_pk, b_dws, b_dbs, r_pk, r_dws, r_dbs, ssem, rsem):
        del dz_in
        jb = pl.program_id(0)
        x_, y_, c_ = _pos()

        @pl.when(jb == 0)
        def _():
            b_pk[0:8, :] = _onehot_rows(sm1_ref[1:2, :], 4 * x_ + 2 * y_ + c_)
            for r, src in enumerate([2, 3, 4, 5, 0, 6]):
                b_pk[8 + r:9 + r, :] = sm1_ref[src:src + 1, :]
            b_pk[14:16, :] = jnp.zeros((2, D_MODEL), F32)
            b_dws[...] = dws_in[...]
            b_dbs[...] = dbs_in[...]

        for e in range(4):
            @pl.when(jb == (e * n_blocks) // 4)
            def _():
                _butterfly_event(e, [b_pk, b_dws, b_dbs], [r_pk, r_dws, r_dbs], ssem, rsem, _all_reduce_partners())

        @pl.when(jb == (3 * n_blocks) // 4)
        def _():
            pk1_ref[...] = b_pk[...]
            dws_ref[...] = b_dws[...]
            dbs_ref[...] = b_dbs[...]
            loss_ref[...] = jnp.broadcast_to(jnp.sum(b_pk[13:14, :], axis=1, keepdims=True), (8, 128))

        zero = jnp.zeros((CONV_PAD, 128), F32)
        for pad in (dqpad, ppad):
            pad[0:CONV_PAD, :] = zero
            pad[s_len + CONV_PAD:s_len + 2 * CONV_PAD, :] = zero
        wacc[...] = jnp.zeros_like(wacc)

        def fill(i, carry):
            t0 = pl.multiple_of(i * tt, tt)
            ppad[pl.ds(CONV_PAD + t0, tt), :] = z_ref[0, pl.ds(t0, tt), :] * _sigmoid(z_ref[1, pl.ds(t0, tt), :])
            dqpad[pl.ds(CONV_PAD + t0, tt), :] = dq_ref[pl.ds(t0, tt), :]
            return carry
        lax.fori_loop(0, s_len // tt, fill, 0)
        w = cw_ref[0]

        def bwd(i, carry):
            t0 = pl.multiple_of(i * tt, tt)
            dp = jnp.zeros((tt, 128), F32)
            for o in range(1, CONV_WIDTH + 1):
                dp = dp + w[CONV_WIDTH - o:CONV_WIDTH - o + 1, :] * dqpad[pl.ds(t0 + o, tt), :]
            a = z_ref[0, pl.ds(t0, tt), :]
            sig = _sigmoid(z_ref[1, pl.ds(t0, tt), :])
            dz_ref[0, pl.ds(t0, tt), :] = (dp * sig).astype(BF16)
            dz_ref[1, pl.ds(t0, tt), :] = (dp * a * (sig * (1.0 - sig))).astype(BF16)
            dqt = dq_ref[pl.ds(t0, tt), :]
            for o in range(1, CONV_WIDTH + 1):
                wacc[o - 1] += _colsum8(dqt * ppad[pl.ds(t0 + o, tt), :])
            wacc[CONV_WIDTH] += _colsum8(dqt)
            return carry
        lax.fori_loop(0, s_len // tt, bwd, 0)
        for k in range(32):
            dcw_ref[k:k + 1, :] = jnp.sum(wacc[k], axis=0, keepdims=True)

    const2 = lambda j: (0, 0)
    return pl.pallas_call(
        body, name="conv_bwd",
        grid=(n_blocks,),
        out_shape=(jax.ShapeDtypeStruct(dz6.shape, BF16), jax.ShapeDtypeStruct((32, D_MODEL), F32))
        + tuple(jax.ShapeDtypeStruct(s, F32) for s in shp) + (jax.ShapeDtypeStruct((8, 128), F32),),
        in_specs=[pl.BlockSpec((s_len, 128), lambda j: (0, j)),
                  pl.BlockSpec((2, s_len, 128), lambda j: (2, 0, j)),
                  pl.BlockSpec((1, 32, 128), lambda j: (j // 2, 0, j % 2)),
                  pl.BlockSpec(memory_space=pl.ANY),
                  pl.BlockSpec((8, D_MODEL), const2),
                  pl.BlockSpec((D_MODEL, 128), const2),
                  pl.BlockSpec((128, 128), const2)],
        out_specs=(pl.BlockSpec((2, s_len, 128), lambda j: (2, 0, j)),
                   pl.BlockSpec((32, 128), lambda j: (0, j)))
        + tuple(pl.BlockSpec(s, const2) for s in shp) + (pl.BlockSpec((8, 128), const2),),
        scratch_shapes=[pltpu.VMEM((s_len + 2 * CONV_PAD, 128), F32), pltpu.VMEM((s_len + 2 * CONV_PAD, 128), F32),
                        pltpu.VMEM((32, 8, 128), F32)]
        + [pltpu.VMEM(s, F32) for s in shp] + [pltpu.VMEM((3,) + s, F32) for s in shp]
        + [pltpu.SemaphoreType.DMA((9,)), pltpu.SemaphoreType.DMA((9,))],
        input_output_aliases={3: 0},
        compiler_params=_params(dimension_semantics=("arbitrary",)),
    )(dq, z6, cw4, dz6, sm1, dws, dbs)


def _bwd_in(dz6, w12, x, dx2, mod, norm_g):
    s_len = x.shape[0]
    tm = TOKEN_TILE
    n_steps = s_len // tm

    def body(dz_ref, w_ref, x_ref, dx2_ref, mod_ref, g_ref, gx_ref, sm_ref, dh_scr, acc_scr):
        i = pl.program_id(0)

        @pl.when(i == 0)
        def _():
            acc_scr[...] = jnp.zeros_like(acc_scr)

        dh = jnp.zeros((tm, D_MODEL), F32)
        for j in range(N_GROUPS):
            n = _natural_group(j)
            for hf in range(2):
                dh = dh + _dot_nt(dz_ref[j, :, hf * 512:(hf + 1) * 512], w_ref[2 * n + hf])
        dh_scr[...] = dh
        scale1 = 1.0 + mod_ref[:, D_MODEL:2 * D_MODEL]
        g = g_ref[...]

        def rows_fn(rows):
            xt = x_ref[rows, :]
            r = lax.rsqrt(_mean(xt * xt) + EPS)
            xn = xt * r
            dhr = dh_scr[rows, :]
            acc_scr[0] += _colsum8(dhr)
            acc_scr[1] += _colsum8(dhr * (xn * g))
            acc_scr[2] += _colsum8(dhr * scale1 * xn)
            dxn = dhr * (g * scale1)
            gx_ref[rows, :] = dx2_ref[rows, :] + r * (dxn - xn * _mean(dxn * xn))
        _row_loop(tm, rows_fn, unroll=FULL_UNROLL)

        @pl.when(i == n_steps - 1)
        def _():
            for qi in range(8):
                sm_ref[qi:qi + 1, :] = jnp.sum(acc_scr[qi], axis=0, keepdims=True)

    tok = lambda i: (i, 0)
    const2 = lambda i: (0, 0)
    return pl.pallas_call(
        body, name="bwd_in",
        grid=(n_steps,),
        out_shape=(jax.ShapeDtypeStruct((s_len, D_MODEL), F32), jax.ShapeDtypeStruct((8, D_MODEL), F32)),
        in_specs=[pl.BlockSpec((N_GROUPS, tm, D_MODEL), lambda i: (0, i, 0)),
                  pl.BlockSpec((12, D_MODEL, 512), lambda i: (0, 0, 0), pipeline_mode=pl.Buffered(1)),
                  pl.BlockSpec((tm, D_MODEL), tok),
                  pl.BlockSpec((tm, D_MODEL), tok),
                  pl.BlockSpec((1, 3 * D_MODEL), const2),
                  pl.BlockSpec((1, D_MODEL), const2)],
        out_specs=(pl.BlockSpec((tm, D_MODEL), tok), pl.BlockSpec((8, D_MODEL), const2)),
        scratch_shapes=[pltpu.VMEM((tm, D_MODEL), F32), pltpu.VMEM((8, 8, D_MODEL), F32)],
        compiler_params=_params(dimension_semantics=("arbitrary",)),
    )(dz6, w12, x, dx2, mod, norm_g)


def _grad_reduce_scatter(kidx, a, b, *, name, nj, nr, nc, a_spec, b_spec, b_lead, kt, small=None):
    nblk = N_CHIPS * nj
    nfor = 3 * nj

    def shard_block(i):
        return jnp.where(i < nfor, i % 3, 3), jnp.where(i < nfor, i // 3, i - nfor)

    def body(k_ref, a_ref, b_ref, *rest):
        del k_ref
        if small is None:
            out_ref, rest = rest[0], rest[1:]
        else:
            sm2_ref, dcw_in, out_ref, pk2_ref, dcwsh_ref, dcb_ref = rest[:6]
            b_pk2, b_dcw, r_pk2, r_dcw, bsend, brecv = rest[-6:]
            rest = rest[6:-6]
        acc, recv_a, send_b, recv_b, own_ps, lsem, psend, precv, isend, irecv, fsend, frecv = rest
        i = pl.program_id(0)
        kk = pl.program_id(1)
        x, y, c = _pos()
        k = 2 * x + y
        sib = (x, y, 1 - c)
        slot = i % 2

        @pl.when(kk == 0)
        def _():
            acc[slot] = jnp.zeros((2, nr, nc), F32)

        if small is not None:
            @pl.when(jnp.logical_and(kk == 0, i == 0))
            def _():
                b = 4 * x + 2 * y + c
                b_pk2[0:8, :] = _onehot_rows(sm2_ref[0:1, :], b)
                b_pk2[8:16, :] = _onehot_rows(sm2_ref[1:2, :], b)
                b_pk2[16:17, :] = sm2_ref[2:3, :]
                b_pk2[17:24, :] = jnp.zeros((7, D_MODEL), F32)
                b_dcw[...] = dcw_in[...]

            for e in range(4):
                @pl.when(jnp.logical_and(kk == 0, i == (e * nblk) // 4))
                def _():
                    _butterfly_event(e, [b_pk2, b_dcw], [r_pk2, r_dcw], bsend, brecv, _all_reduce_partners())

            @pl.when(jnp.logical_and(kk == 0, i == (3 * nblk) // 4))
            def _():
                pk2_ref[...] = b_pk2[...]
                sel = jnp.zeros((32, 256), F32)
                for kc in range(N_CHIPS):
                    sel = jnp.where(k == kc, b_dcw[:, kc * 256:(kc + 1) * 256], sel)
                dcwsh_ref[...] = sel
                dcb_ref[...] = b_dcw[31:32, :]

        def pair_copy(ii, sl):
            return _rcopy(acc.at[sl, 1 - c], recv_a.at[ii], psend.at[ii], precv.at[ii], sib)

        def pair_sum(ii, sl):
            cp = pair_copy(ii, sl)
            cp.wait_recv()
            cp.wait_send()
            return acc[sl, c] + recv_a[ii]

        def ici_copy(ii, kd, rslot):
            return _rcopy(send_b.at[ii], recv_b.at[rslot], isend.at[ii], irecv.at[rslot], (kd // 2, kd % 2, c))

        def finalize(ii, sl):
            ps = pair_sum(ii, sl)
            s, j = shard_block(ii)

            @pl.when(s < 3)
            def _():
                send_b[ii] = ps.astype(BF16)
                ici_copy(ii, (k + 1 + s) % N_CHIPS, (2 - s) * nj + j).start()

            @pl.when(s == 3)
            def _():
                own_ps[j] = ps

        @pl.when(jnp.logical_and(kk == kt - 1, i >= 1))
        def _():
            finalize(i - 1, 1 - slot)

        res = _dot_tn(a_ref[...], b_ref[0] if b_lead else b_ref[...])
        acc[slot, 0] += res[0:nr]
        acc[slot, 1] += res[nr:2 * nr]

        @pl.when(kk == kt - 1)
        def _():
            pair_copy(i, slot).start()

            @pl.when(i == nblk - 1)
            def _():
                own_ps[nj - 1] = pair_sum(nblk - 1, (nblk - 1) % 2)
                for r in range(nfor):
                    ici_copy(0, k, r).wait_recv()
                for j in range(nj):
                    tot = own_ps[j]
                    for s in range(3):
                        tot = tot + recv_b[s * nj + j].astype(F32)
                    own_ps[j] = tot
                loc = pltpu.make_async_copy(own_ps, out_ref.at[:, c], lsem)
                loc.start()
                swap = _rcopy(own_ps, out_ref.at[:, c], fsend, frecv, sib)
                swap.start()
                loc.wait()
                swap.wait_send()
                _rcopy(own_ps, out_ref.at[:, 1 - c], fsend, frecv, sib).wait_recv()
                for ii in range(nfor):
                    ici_copy(ii, k, 0).wait_send()

    def seq(i, k_ref):
        s, j = shard_block(i)
        return nj * ((k_ref[0] + 1 + s) % N_CHIPS) + j

    const2 = lambda i, kk, k_ref: (0, 0)
    small_in = [(8, D_MODEL), (32, D_MODEL)]
    small_out = [(24, D_MODEL), (32, 256), (1, D_MODEL)]
    small_buf = [(24, D_MODEL), (32, D_MODEL)]
    grid_spec = pltpu.PrefetchScalarGridSpec(
        num_scalar_prefetch=1,
        grid=(nblk, kt),
        in_specs=[a_spec(seq), b_spec(seq)] + ([pl.BlockSpec(s, const2) for s in small_in] if small else []),
        out_specs=((pl.BlockSpec(memory_space=pl.ANY),) + tuple(pl.BlockSpec(s, const2) for s in small_out) if small
                   else pl.BlockSpec(memory_space=pl.ANY)),
        scratch_shapes=[pltpu.VMEM((2, 2, nr, nc), F32),
                        pltpu.VMEM((nblk, nr, nc), F32),
                        pltpu.VMEM((nfor, nr, nc), BF16),
                        pltpu.VMEM((nfor, nr, nc), BF16),
                        pltpu.VMEM((nj, nr, nc), F32),
                        pltpu.SemaphoreType.DMA,
                        pltpu.SemaphoreType.DMA((nblk,)), pltpu.SemaphoreType.DMA((nblk,)),
                        pltpu.SemaphoreType.DMA((nfor,)), pltpu.SemaphoreType.DMA((nfor,)),
                        pltpu.SemaphoreType.DMA, pltpu.SemaphoreType.DMA]
        + ([pltpu.VMEM(s, F32) for s in small_buf] + [pltpu.VMEM((3,) + s, F32) for s in small_buf]
           + [pltpu.SemaphoreType.DMA((6,)), pltpu.SemaphoreType.DMA((6,))] if small else []))
    grad_shape = jax.ShapeDtypeStruct((nj, 2, nr, nc), F32)
    return pl.pallas_call(
        body, name=name,
        grid_spec=grid_spec,
        out_shape=((grad_shape,) + tuple(jax.ShapeDtypeStruct(s, F32) for s in small_out)) if small else grad_shape,
        compiler_params=_params(dimension_semantics=("arbitrary", "arbitrary")),
    )(*([kidx, a, b] + (list(small) if small else [])))


def _grad_w_in(kidx, h, dz6, sm2, dcw):
    s_len = h.shape[0]
    tk = min(K_TILE, s_len)
    return _grad_reduce_scatter(
        kidx, h, dz6, small=(sm2, dcw), name="grad_w_in", nj=3, nr=512, nc=512, b_lead=True, kt=s_len // tk,
        a_spec=lambda seq: pl.BlockSpec((tk, D_MODEL), lambda i, kk, k_ref: (kk, 0)),
        b_spec=lambda seq: pl.BlockSpec(
            (1, tk, 512), lambda i, kk, k_ref: ((seq(i, k_ref) // 2 + 4) % N_GROUPS, kk, seq(i, k_ref) % 2)))


def _grad_w_out(kidx, ycat, dy):
    s_len = dy.shape[0]
    tk = min(K_TILE, s_len)
    return _grad_reduce_scatter(
        kidx, ycat, dy, name="grad_w_out", nj=1, nr=256, nc=D_MODEL, b_lead=False, kt=s_len // tk,
        a_spec=lambda seq: pl.BlockSpec((tk, 512), lambda i, kk, k_ref: (kk, seq(i, k_ref))),
        b_spec=lambda seq: pl.BlockSpec((tk, D_MODEL), lambda i, kk, k_ref: (kk, 0)))


def _adamw_math(w, g, m, v):
    m = ADAM_B1 * m + (1.0 - ADAM_B1) * g
    v = ADAM_B2 * v + (1.0 - ADAM_B2) * (g * g)
    m_hat = m / (1.0 - ADAM_B1 ** ADAM_STEP)
    v_hat = v / (1.0 - ADAM_B2 ** ADAM_STEP)
    delta = -ADAM_LR * (m_hat / (jnp.sqrt(v_hat) + ADAM_EPS) + ADAM_WD * w)
    return delta, m, v


def _adamw_blocked(w, m, v, g4, name):
    nj, _, nr, nc = g4.shape

    def body(w_ref, m_ref, v_ref, g_ref, go_ref, d_ref, mo_ref, vo_ref):
        g = g_ref[0, 0]
        d, mn, vn = _adamw_math(w_ref[...], g, m_ref[...], v_ref[...])
        go_ref[...] = g
        d_ref[...] = d
        mo_ref[...] = mn
        vo_ref[...] = vn

    blk = pl.BlockSpec((nr, nc), lambda j, hf: (hf, j))
    return pl.pallas_call(
        body, name=name,
        grid=(nj, 2),
        out_shape=tuple(jax.ShapeDtypeStruct(w.shape, F32) for _ in range(4)),
        in_specs=[blk, blk, blk, pl.BlockSpec((1, 1, nr, nc), lambda j, hf: (j, hf, 0, 0))],
        out_specs=(blk, blk, blk, blk),
        compiler_params=_params(dimension_semantics=("arbitrary", "arbitrary")),
    )(w, m, v, g4)


def _adamw_w_ada(cact, pk1, pk2, w, m, v):
    def body(cact_ref, pk1_ref, pk2_ref, w_ref, m_ref, v_ref, g_ref, d_ref, mo_ref, vo_ref, gb_ref, dmod):
        x, y, _ = _pos()
        k = 2 * x + y
        dmod[:, 0:D_MODEL] = pk2_ref[0:8, :]
        dmod[:, D_MODEL:2 * D_MODEL] = pk2_ref[8:16, :]
        dmod[:, 2 * D_MODEL:3 * D_MODEL] = pk1_ref[0:8, :]
        gb_ref[...] = jnp.sum(dmod[...], axis=0, keepdims=True)
        sel = jnp.zeros((8, 768), F32)
        for kk in range(N_CHIPS):
            sel = jnp.where(k == kk, dmod[:, kk * 768:(kk + 1) * 768], sel)
        g = _dot_tn(cact_ref[...].astype(BF16), sel.astype(BF16))
        d, mn, vn = _adamw_math(w_ref[...], g, m_ref[...], v_ref[...])
        g_ref[...] = g
        d_ref[...] = d
        mo_ref[...] = mn
        vo_ref[...] = vn

    return pl.pallas_call(
        body, name="adamw_w_ada",
        out_shape=tuple(jax.ShapeDtypeStruct(w.shape, F32) for _ in range(4)) + (
            jax.ShapeDtypeStruct((1, 3 * D_MODEL), F32),),
        in_specs=[_vmem()] * 6,
        out_specs=tuple(_vmem() for _ in range(5)),
        scratch_shapes=[pltpu.VMEM((8, 3 * D_MODEL), F32)],
        compiler_params=_params(),
    )(cact, pk1, pk2, w, m, v)


def _adamw_small(items):
    n = len(items)

    def body(*refs):
        ins, outs = refs[:4 * n], refs[4 * n:]
        for i in range(n):
            w_ref, g_ref, m_ref, v_ref = ins[4 * i:4 * i + 4]
            d, mn, vn = _adamw_math(w_ref[...], g_ref[...], m_ref[...], v_ref[...])
            outs[3 * i][...] = d
            outs[3 * i + 1][...] = mn
            outs[3 * i + 2][...] = vn

    flat = [a for it in items for a in it]
    outs = pl.pallas_call(
        body, name="adamw_small",
        out_shape=tuple(jax.ShapeDtypeStruct(it[0].shape, F32) for it in items for _ in range(3)),
        in_specs=[_vmem()] * (4 * n),
        out_specs=tuple(_vmem() for _ in range(3 * n)),
        compiler_params=_params(),
    )(*flat)
    return [tuple(outs[3 * i:3 * i + 3]) for i in range(n)]


def kernel(x, c, w_ada, b_ada, norm_g, w_in, conv_w, conv_b, conv_ln_g, conv_ln_b, sg_ln_g, sg_ln_b, w_s, b_s, w_out, final_g, loss_target, m_w_ada, m_b_ada, m_norm_g, m_w_in, m_conv_w, m_conv_b, m_conv_ln_g, m_conv_ln_b, m_sg_ln_g, m_sg_ln_b, m_w_s, m_b_s, m_w_out, m_final_g, v_w_ada, v_b_ada, v_norm_g, v_w_in, v_conv_w, v_conv_b, v_conv_ln_g, v_conv_ln_b, v_sg_ln_g, v_sg_ln_b, v_w_s, v_b_s, v_w_out, v_final_g):
    s_len = x.shape[1]
    x2d = x[0]
    tgt = loss_target[0]
    row = lambda a: a.reshape(1, -1)

    kidx = (2 * lax.axis_index("x") + lax.axis_index("y")).astype(jnp.int32).reshape(1)
    cw_sh = jnp.pad(conv_w.reshape(CONV_WIDTH, 256), ((0, 1), (0, 0)))
    h, z6, mod, cact, cw4, w_in12 = _front(kidx, x2d, c, w_ada[0], b_ada, norm_g, w_in[0], cw_sh)
    w12 = w_in12.reshape(12, D_MODEL, 512)
    q, w_out4 = _conv_fwd(z6, cw4, conv_b, w_out[0])
    w_out_full = w_out4.reshape(2 * D_MODEL, D_MODEL)
    ln_rows = (conv_ln_g, conv_ln_b, sg_ln_g, sg_ln_b, row(final_g))
    bs_exp = jnp.repeat(b_s[0].T, CHUNK, axis=1)
    dz6, dq, dx2, ycat, dy, dws, sm1, dbs = _middle(x2d, z6, q, tgt, ln_rows, mod, w_s[0], bs_exp, w_out_full)
    g_w_out4 = _grad_w_out(kidx, ycat, dy)
    dz6, dcw, pk1, dws_r, dbs_r, loss_t = _conv_bwd(dq, z6, cw4, dz6, sm1, dws.reshape(D_MODEL, CHUNK), dbs)
    grad_x, sm2 = _bwd_in(dz6, w12, x2d, dx2, mod, norm_g)
    g_w_in4, pk2, dcw_sh, dcb = _grad_w_in(kidx, h, dz6, sm2, dcw)

    g_w_in, d_w_in, nm_w_in, nv_w_in = _adamw_blocked(w_in[0], m_w_in[0], v_w_in[0], g_w_in4, "adamw_w_in")
    g_w_out, d_w_out, nm_w_out, nv_w_out = _adamw_blocked(w_out[0], m_w_out[0], v_w_out[0], g_w_out4, "adamw_w_out")
    g_w_ada, d_w_ada, nm_w_ada, nv_w_ada, g_b_ada = _adamw_w_ada(cact, pk1, pk2, w_ada[0], m_w_ada[0], v_w_ada[0])

    g_norm_g = pk2[16:17]
    g_cln_g, g_cln_b, g_sln_g, g_sln_b, g_final = (pk1[8 + i:9 + i] for i in range(5))
    loss = loss_t[0, 0]
    g_conv_w = dcw_sh[:CONV_WIDTH]
    g_w_s = dws_r
    g_b_s = dbs_r[:, :HEADS].T
    small = [
        (b_ada, g_b_ada, m_b_ada, v_b_ada),
        (norm_g, g_norm_g, m_norm_g, v_norm_g),
        (conv_w.reshape(CONV_WIDTH, 256), g_conv_w, m_conv_w.reshape(CONV_WIDTH, 256), v_conv_w.reshape(CONV_WIDTH, 256)),
        (conv_b, dcb, m_conv_b, v_conv_b),
        (conv_ln_g, g_cln_g, m_conv_ln_g, v_conv_ln_g),
        (conv_ln_b, g_cln_b, m_conv_ln_b, v_conv_ln_b),
        (sg_ln_g, g_sln_g, m_sg_ln_g, v_sg_ln_g),
        (sg_ln_b, g_sln_b, m_sg_ln_b, v_sg_ln_b),
        (w_s.reshape(D_MODEL, CHUNK), g_w_s, m_w_s.reshape(D_MODEL, CHUNK), v_w_s.reshape(D_MODEL, CHUNK)),
        (b_s[0], g_b_s, m_b_s[0], v_b_s[0]),
        (row(final_g), g_final, row(m_final_g), row(v_final_g)),
    ]
    upd = _adamw_small(small)

    shapes = [w_ada.shape, b_ada.shape, norm_g.shape, w_in.shape, conv_w.shape, conv_b.shape, conv_ln_g.shape,
              conv_ln_b.shape, sg_ln_g.shape, sg_ln_b.shape, w_s.shape, b_s.shape, w_out.shape, final_g.shape]
    grads = [g_w_ada, g_b_ada, g_norm_g, g_w_in, g_conv_w, dcb, g_cln_g, g_cln_b, g_sln_g, g_sln_b, g_w_s, g_b_s,
             g_w_out, g_final]
    big = {0: (d_w_ada, nm_w_ada, nv_w_ada), 3: (d_w_in, nm_w_in, nv_w_in), 12: (d_w_out, nm_w_out, nv_w_out)}
    small_pos = [1, 2, 4, 5, 6, 7, 8, 9, 10, 11, 13]
    trip = [None] * 14
    for i, t in big.items():
        trip[i] = t
    for i, t in zip(small_pos, upd):
        trip[i] = t
    fit = lambda arrs: [a.reshape(s) for a, s in zip(arrs, shapes)]
    return (loss, grad_x.reshape(x.shape), *fit(grads), *fit([t[0] for t in trip]), *fit([t[1] for t in trip]),
            *fit([t[2] for t in trip]))
```

```python
import jax
import jax.numpy as jnp
from jax import lax
from jax.experimental import pallas as pl
from jax.experimental.pallas import tpu as pltpu

F32 = jnp.float32
BF16 = jnp.bfloat16
MESH = pl.DeviceIdType.MESH

D_MODEL = 1024
N_CHIPS = 4
HEADS = 8
CHUNK = 128
CONV_WIDTH = 31
CONV_HALF = CONV_WIDTH // 2
CONV_PAD = 16
EPS = 1e-6
ADAM_LR = 0.001
ADAM_B1 = 0.9
ADAM_B2 = 0.999
ADAM_EPS = 1e-08
ADAM_WD = 0.01
ADAM_STEP = 10

V7X_VMEM_BYTES = 64 * 1024 * 1024
VMEM_LIMIT = V7X_VMEM_BYTES - 8 * 1024 * 1024
ROWS = 16
UNROLL = 8
TOKEN_TILE = 256
FULL_UNROLL = TOKEN_TILE // ROWS
TIME_TILE = 128
K_TILE = 2048

N_GROUPS = 6


def _natural_group(j):
    return (j + 2) % N_GROUPS


def _pos():
    return lax.axis_index("x"), lax.axis_index("y"), lax.axis_index("c")


def _rcopy(src, dst, ssem, rsem, dev):
    return pltpu.make_async_remote_copy(src_ref=src, dst_ref=dst, send_sem=ssem, recv_sem=rsem,
                                        device_id=dev, device_id_type=MESH)


def _vmem():
    return pl.BlockSpec(memory_space=pltpu.VMEM)


def _params(**kw):
    return pltpu.CompilerParams(vmem_limit_bytes=VMEM_LIMIT, **kw)


def _sigmoid(v):
    return 0.5 * jnp.tanh(0.5 * v) + 0.5


def _row_loop(n_rows, body, unroll=1):
    def step(r, carry):
        body(pl.ds(pl.multiple_of(r * ROWS, ROWS), ROWS))
        return carry
    lax.fori_loop(0, n_rows // ROWS, step, 0, unroll=unroll)


def _colsum8(v):
    return v.reshape(v.shape[0] // 8, 8, v.shape[1]).sum(axis=0)


def _mean(v):
    return jnp.mean(v, axis=-1, keepdims=True)


def _dot_nn(a, b):
    return jnp.dot(a, b, preferred_element_type=F32)


def _dot_nt(a, b):
    return lax.dot_general(a, b, (((1,), (1,)), ((), ())), preferred_element_type=F32)


def _dot_tn(a, b):
    return lax.dot_general(a, b, (((0,), (0,)), ((), ())), preferred_element_type=F32)


def _remote_chip(k, r):
    return jnp.bitwise_xor(k, r + 1)


def _front(kidx, x, c, w_ada, b_ada, norm_g, w_in, cw):
    s_len = x.shape[0]
    tmh = min(512, s_len)
    tmz = min(1024, s_len)
    nh = s_len // tmh
    nz = s_len // tmz
    n_steps = nh + 12 * nz

    def remote_block(q):
        return jnp.where(q < 6, q % 2, 2), jnp.where(q < 6, q // 2, q - 6)

    def block_of(i, k):
        r, j = remote_block(jnp.maximum(i - 3, 0))
        return jnp.where(i < 3, 3 * k + i, 3 * _remote_chip(k, r) + j)

    def body(k_ref, x_ref, c_ref, bada_ref, g_ref, cw_ref, wada_hbm, win_hbm,
             h_ref, z_ref, mod_ref, cact_ref, cw4_ref, w12_hbm,
             h_all, wbuf, stage, wada_v, cslab, mslab, cw4_s,
             lsem, csend, crecv, msend, mrecv, wsend, wrecv, isend, irecv, fsend, frecv, osem):
        del k_ref
        t = pl.program_id(0)
        x_, y_, c_ = _pos()
        k = 2 * x_ + y_
        b = 4 * x_ + 2 * y_ + c_
        sib = (x_, y_, 1 - c_)

        def dev_of(r):
            kk = _remote_chip(k, r)
            return (kk // 2, kk % 2, c_)

        def ici(q, kk):
            r, j = remote_block(q)
            return _rcopy(wbuf.at[3 * kk + j, c_], wbuf.at[3 * kk + j, c_], isend.at[q], irecv.at[q], dev_of(r))

        def fwd(q, hf):
            r, j = remote_block(q)
            blk = 3 * _remote_chip(k, r) + j
            return _rcopy(wbuf.at[blk, hf], wbuf.at[blk, hf], fsend.at[q], frecv.at[q], sib)

        def c_copy(q, src):
            d = jnp.bitwise_xor(b, q)
            return _rcopy(cslab.at[src], cslab.at[src], csend.at[q - 1], crecv.at[q - 1], (d // 4, (d // 2) % 2, d % 2))

        def m_copy(r, kk):
            return _rcopy(mslab.at[kk], mslab.at[kk], msend.at[r], mrecv.at[r], dev_of(r))

        def cw_copy(r, kk):
            return _rcopy(cw4_s.at[kk], cw4_s.at[kk], wsend.at[r], wrecv.at[r], dev_of(r))

        def to_hbm(i):
            m = block_of(i, k)
            return pltpu.make_async_copy(wbuf.at[m], w12_hbm.at[m], osem.at[i])

        @pl.when(t == 0)
        def _():
            ld_w = pltpu.make_async_copy(win_hbm, stage, lsem.at[0])
            ld_w.start()
            ld_a = pltpu.make_async_copy(wada_hbm, wada_v, lsem.at[1])
            ld_a.start()
            cslab[b] = jnp.broadcast_to(c_ref[...], (8, D_MODEL))
            for q in range(1, 8):
                c_copy(q, b).start()
            cw4_s[k] = cw_ref[...]
            for r in range(3):
                cw_copy(r, k).start()
            ld_w.wait()
            for j in range(3):
                for hf in range(2):
                    wbuf[3 * k + j, hf] = stage[hf * 512:(hf + 1) * 512, j * 512:(j + 1) * 512].astype(BF16)
            for i in range(3):
                to_hbm(i).start()
            for q in range(1, 8):
                c_copy(q, jnp.bitwise_xor(b, q)).wait_recv()
            row = lax.broadcasted_iota(jnp.int32, (8, D_MODEL), 0)
            call = jnp.zeros((8, D_MODEL), F32)
            for d in range(8):
                call = jnp.where(row == d, cslab[d], call)
            cact = call * _sigmoid(call)
            cact_ref[...] = cact
            ld_a.wait()
            mslab[k] = _dot_nn(cact.astype(BF16), wada_v[...].astype(BF16))
            for r in range(3):
                m_copy(r, k).start()
            for q in range(9):
                ici(q, k).start()
            for r in range(3):
                m_copy(r, _remote_chip(k, r)).wait_recv()
            row8 = lax.broadcasted_iota(jnp.int32, (8, 768), 0)
            for kk in range(N_CHIPS):
                piece = jnp.sum(jnp.where(row8 == b, mslab[kk], 0.0), axis=0, keepdims=True)
                mod_ref[:, kk * 768:(kk + 1) * 768] = piece + bada_ref[:, kk * 768:(kk + 1) * 768]
            for r in range(3):
                cw_copy(r, _remote_chip(k, r)).wait_recv()
            cw4_ref[...] = cw4_s[...]

        @pl.when(t < nh)
        def _():
            shift = mod_ref[:, 0:D_MODEL]
            scale1 = 1.0 + mod_ref[:, D_MODEL:2 * D_MODEL]
            g = g_ref[...]
            base = t * tmh

            def rows_fn(rows):
                xt = x_ref[rows, :]
                r = lax.rsqrt(_mean(xt * xt) + EPS)
                hv = ((xt * r * g) * scale1 + shift).astype(BF16)
                h_ref[rows, :] = hv
                h_all[pl.ds(pl.multiple_of(base + rows.start, ROWS), ROWS), :] = hv
            _row_loop(tmh, rows_fn, unroll=UNROLL)

        @pl.when(t >= nh)
        def _():
            u = t - nh
            i = u // nz
            rt = u % nz
            @pl.when(jnp.logical_and(rt == 0, jnp.logical_and(i >= 2, i <= 10)))
            def _():
                q = i - 2
                r, _ = remote_block(q)
                ici(q, _remote_chip(k, r)).wait_recv()
                fwd(q, c_).start()

            @pl.when(jnp.logical_and(rt == 0, i >= 3))
            def _():
                fwd(i - 3, 1 - c_).wait_recv()
                to_hbm(i).start()
            m = block_of(i, k)
            hb = h_all[pl.ds(pl.multiple_of(rt * tmz, tmz), tmz), :]
            z_ref[0] = _dot_nn(hb[:, 0:512], wbuf[m, 0]) + _dot_nn(hb[:, 512:1024], wbuf[m, 1])

        @pl.when(t == n_steps - 1)
        def _():
            for q in range(1, 8):
                c_copy(q, b).wait_send()
            for r in range(3):
                m_copy(r, k).wait_send()
                cw_copy(r, k).wait_send()
            for q in range(9):
                ici(q, k).wait_send()
                fwd(q, c_).wait_send()
            for i in range(12):
                to_hbm(i).wait()

    def z_index(t, k_ref):
        u = jnp.maximum(t - nh, 0)
        m = block_of(u // nz, k_ref[0])
        return ((m // 2 + 4) % N_GROUPS, u % nz, m % 2)

    tok = lambda t, k_ref: (jnp.minimum(t, nh - 1), 0)
    const2 = lambda t, k_ref: (0, 0)
    grid_spec = pltpu.PrefetchScalarGridSpec(
        num_scalar_prefetch=1,
        grid=(n_steps,),
        in_specs=[pl.BlockSpec((tmh, D_MODEL), tok),
                  pl.BlockSpec((1, D_MODEL), const2),
                  pl.BlockSpec((1, 3 * D_MODEL), const2),
                  pl.BlockSpec((1, D_MODEL), const2),
                  pl.BlockSpec((32, 256), const2),
                  pl.BlockSpec(memory_space=pl.ANY),
                  pl.BlockSpec(memory_space=pl.ANY)],
        out_specs=(pl.BlockSpec((tmh, D_MODEL), tok),
                   pl.BlockSpec((1, tmz, 512), z_index),
                   pl.BlockSpec((1, 3 * D_MODEL), const2),
                   pl.BlockSpec((8, D_MODEL), const2),
                   pl.BlockSpec((N_CHIPS, 32, 256), lambda t, k_ref: (0, 0, 0)),
                   pl.BlockSpec(memory_space=pl.ANY)),
        scratch_shapes=[pltpu.VMEM((s_len, D_MODEL), BF16),
                        pltpu.VMEM((12, 2, 512, 512), BF16),
                        pltpu.VMEM((D_MODEL, 1536), F32),
                        pltpu.VMEM((D_MODEL, 768), F32),
                        pltpu.VMEM((8, 8, D_MODEL), F32),
                        pltpu.VMEM((N_CHIPS, 8, 768), F32),
                        pltpu.VMEM((N_CHIPS, 32, 256), F32),
                        pltpu.SemaphoreType.DMA((2,)),
                        pltpu.SemaphoreType.DMA((7,)), pltpu.SemaphoreType.DMA((7,)),
                        pltpu.SemaphoreType.DMA((3,)), pltpu.SemaphoreType.DMA((3,)),
                        pltpu.SemaphoreType.DMA((3,)), pltpu.SemaphoreType.DMA((3,)),
                        pltpu.SemaphoreType.DMA((9,)), pltpu.SemaphoreType.DMA((9,)),
                        pltpu.SemaphoreType.DMA((9,)), pltpu.SemaphoreType.DMA((9,)),
                        pltpu.SemaphoreType.DMA((12,))])
    return pl.pallas_call(
        body, name="front",
        grid_spec=grid_spec,
        out_shape=(jax.ShapeDtypeStruct((s_len, D_MODEL), BF16),
                   jax.ShapeDtypeStruct((N_GROUPS, s_len, D_MODEL), F32),
                   jax.ShapeDtypeStruct((1, 3 * D_MODEL), F32),
                   jax.ShapeDtypeStruct((8, D_MODEL), F32),
                   jax.ShapeDtypeStruct((N_CHIPS, 32, 256), F32),
                   jax.ShapeDtypeStruct((12, 2, 512, 512), BF16)),
        compiler_params=_params(dimension_semantics=("arbitrary",)),
    )(kidx, x, c, b_ada, norm_g, cw, w_ada, w_in)


def _all_reduce_partners():
    x, y, c = _pos()
    return [(x, y, 1 - c), (x, 1 - y, c), (1 - x, y, c)]


def _butterfly_event(e, bufs, recvs, ssem, rsem, partners):
    n = len(bufs)

    def copies(s):
        return [_rcopy(buf, recv.at[s], ssem.at[s * n + i], rsem.at[s * n + i], partners[s])
                for i, (buf, recv) in enumerate(zip(bufs, recvs))]

    if e > 0:
        for cp in copies(e - 1):
            cp.wait()
        for buf, recv in zip(bufs, recvs):
            buf[...] = buf[...] + recv[e - 1]
    if e < len(partners):
        for cp in copies(e):
            cp.start()


def _onehot_rows(v, b):
    row = lax.broadcasted_iota(jnp.int32, (8, v.shape[1]), 0)
    return jnp.where(row == b, jnp.broadcast_to(v, (8, v.shape[1])), 0.0)


def _conv_fwd(z6, cw4, conv_b, w_out):
    s_len = z6.shape[1]
    tt = TIME_TILE
    n_blocks = D_MODEL // 128

    def body(z_ref, cw_ref, cb_ref, wout_hbm, q_ref, wout4_hbm, ppad, stage, wbuf, lsem, isend, irecv, fsend, frecv, osem):
        jb = pl.program_id(0)
        x_, y_, c_ = _pos()
        k = 2 * x_ + y_
        sib = (x_, y_, 1 - c_)

        def ici(r, kk):
            rk = _remote_chip(k, r)
            return _rcopy(wbuf.at[kk, c_], wbuf.at[kk, c_], isend.at[r], irecv.at[r], (rk // 2, rk % 2, c_))

        def fwd(r, hf):
            kk = _remote_chip(k, r)
            return _rcopy(wbuf.at[kk, hf], wbuf.at[kk, hf], fsend.at[r], frecv.at[r], sib)

        @pl.when(jb == 0)
        def _():
            ld = pltpu.make_async_copy(wout_hbm, stage, lsem)
            ld.start()
            ld.wait()
            for hf in range(2):
                wbuf[k, hf] = stage[hf * 256:(hf + 1) * 256, :].astype(BF16)
            for r in range(3):
                ici(r, k).start()

        @pl.when(jb == (5 * n_blocks) // 8)
        def _():
            for r in range(3):
                ici(r, _remote_chip(k, r)).wait_recv()
                fwd(r, c_).start()

        zero = jnp.zeros((CONV_PAD, 128), F32)
        ppad[0:CONV_PAD, :] = zero
        ppad[s_len + CONV_PAD:s_len + 2 * CONV_PAD, :] = zero

        def fill(i, carry):
            t0 = pl.multiple_of(i * tt, tt)
            ppad[pl.ds(CONV_PAD + t0, tt), :] = z_ref[0, pl.ds(t0, tt), :] * _sigmoid(z_ref[1, pl.ds(t0, tt), :])
            return carry
        lax.fori_loop(0, s_len // tt, fill, 0)
        w = cw_ref[0]
        bias = cb_ref[...]

        def conv(i, carry):
            t0 = pl.multiple_of(i * tt, tt)
            acc = jnp.broadcast_to(bias, (tt, 128))
            for o in range(1, CONV_WIDTH + 1):
                acc = acc + w[o - 1:o, :] * ppad[pl.ds(t0 + o, tt), :]
            q_ref[pl.ds(t0, tt), :] = acc
            return carry
        lax.fori_loop(0, s_len // tt, conv, 0)

        @pl.when(jb == n_blocks - 1)
        def _():
            for r in range(3):
                fwd(r, 1 - c_).wait_recv()
            out = pltpu.make_async_copy(wbuf, wout4_hbm, osem)
            out.start()
            for r in range(3):
                ici(r, k).wait_send()
                fwd(r, c_).wait_send()
            out.wait()

    return pl.pallas_call(
        body, name="conv_fwd",
        grid=(n_blocks,),
        out_shape=(jax.ShapeDtypeStruct((s_len, D_MODEL), F32), jax.ShapeDtypeStruct((N_CHIPS, 2, 256, D_MODEL), BF16)),
        in_specs=[pl.BlockSpec((2, s_len, 128), lambda j: (2, 0, j)),
                  pl.BlockSpec((1, 32, 128), lambda j: (j // 2, 0, j % 2)),
                  pl.BlockSpec((1, 128), lambda j: (0, j)),
                  pl.BlockSpec(memory_space=pl.ANY)],
        out_specs=(pl.BlockSpec((s_len, 128), lambda j: (0, j)), pl.BlockSpec(memory_space=pl.ANY)),
        scratch_shapes=[pltpu.VMEM((s_len + 2 * CONV_PAD, 128), F32),
                        pltpu.VMEM((512, D_MODEL), F32),
                        pltpu.VMEM((N_CHIPS, 2, 256, D_MODEL), BF16),
                        pltpu.SemaphoreType.DMA,
                        pltpu.SemaphoreType.DMA((3,)), pltpu.SemaphoreType.DMA((3,)),
                        pltpu.SemaphoreType.DMA((3,)), pltpu.SemaphoreType.DMA((3,)),
                        pltpu.SemaphoreType.DMA],
        compiler_params=_params(dimension_semantics=("arbitrary",)),
    )(z6, cw4, conv_b, w_out)


def _middle(x, z6, q, target, ln_rows, mod, w_s, bs_exp, w_out):
    s_len = x.shape[0]
    tm = TOKEN_TILE
    n_steps = s_len // tm
    n_chunks = tm // CHUNK
    inv_d = 1.0 / D_MODEL

    def body(x_ref, z_ref, q_ref, tgt_ref, cg_ref, cb_ref, sg_ref, sb_ref, fg_ref, mod_ref, ws_ref, bs_ref, wout_ref,
             dz_ref, dq_ref, dx2_ref, ycat_ref, dy_ref, dws_ref, sm_ref, dbs_ref,
             vl_scr, vm_scr, y_scr, dycat_scr, dvm_scr, dvl_scr, acc_scr, dbs_acc):
        i = pl.program_id(0)

        @pl.when(i == 0)
        def _():
            acc_scr[...] = jnp.zeros_like(acc_scr)
            dbs_acc[...] = jnp.zeros_like(dbs_acc)
            dws_ref[...] = jnp.zeros_like(dws_ref)

        cg, cb, sg, sb, fg = cg_ref[...], cb_ref[...], sg_ref[...], sb_ref[...], fg_ref[...]
        gm = mod_ref[:, 2 * D_MODEL:3 * D_MODEL]

        def norm_stats(t):
            c = t - _mean(t)
            rstd = lax.rsqrt(_mean(c * c) + EPS)
            return c * rstd, rstd

        def phase1(rows):
            qhat, _ = norm_stats(q_ref[rows, :])
            ln = qhat * cg + cb
            gz = z_ref[0, rows, :]
            ycat_ref[rows, 0:D_MODEL] = ((ln * _sigmoid(ln)) * (gz * _sigmoid(gz))).astype(BF16)
            vhat, _ = norm_stats(z_ref[2, rows, :])
            vl_scr[rows, :] = (vhat * sg + sb).astype(BF16)
        _row_loop(tm, phase1, unroll=FULL_UNROLL)

        for ch in range(n_chunks):
            r0 = ch * CHUNK
            for h in range(HEADS):
                c0 = h * CHUNK
                vm_scr[r0:r0 + CHUNK, c0:c0 + CHUNK] = (
                    _dot_nn(ws_ref[h].astype(BF16), vl_scr[r0:r0 + CHUNK, c0:c0 + CHUNK]) + bs_ref[:, c0:c0 + CHUNK])

        def phase3(rows):
            bg = z_ref[3, rows, :]
            ycat_ref[rows, D_MODEL:2 * D_MODEL] = (z_ref[1, rows, :] * vm_scr[rows, :] * (bg * _sigmoid(bg))).astype(BF16)
        _row_loop(tm, phase3, unroll=FULL_UNROLL)

        y_scr[...] = _dot_nn(ycat_ref[...], wout_ref[...])

        def phase5(rows):
            y = y_scr[rows, :]
            x2 = x_ref[rows, :] + gm * y
            r2 = lax.rsqrt(_mean(x2 * x2) + EPS)
            xn2 = x2 * r2
            diff = xn2 * fg - tgt_ref[rows, :]
            acc_scr[6] += _colsum8(diff * diff)
            dout = diff * inv_d
            acc_scr[0] += _colsum8(dout * xn2)
            dxn = dout * fg
            dx2 = r2 * (dxn - xn2 * _mean(dxn * xn2))
            dx2_ref[rows, :] = dx2
            acc_scr[1] += _colsum8(dx2 * y)
            dy_ref[rows, :] = (dx2 * gm).astype(BF16)
        _row_loop(tm, phase5, unroll=FULL_UNROLL)

        dycat_scr[...] = _dot_nt(dy_ref[...], wout_ref[...])

        def phase7(rows):
            dyb = dycat_scr[rows, D_MODEL:2 * D_MODEL]
            u = z_ref[1, rows, :]
            bg = z_ref[3, rows, :]
            vm = vm_scr[rows, :]
            sig = _sigmoid(bg)
            silu = bg * sig
            dz_ref[1, rows, :] = (dyb * vm * silu).astype(BF16)
            dvm = dyb * u * silu
            dz_ref[3, rows, :] = (dyb * u * vm * (sig * (1.0 + bg * (1.0 - sig)))).astype(BF16)
            dvm_scr[rows, :] = dvm.astype(BF16)
            pos = pl.ds(pl.multiple_of(rows.start % CHUNK, ROWS), ROWS)
            dbs_acc[pos, :] += dvm
        _row_loop(tm, phase7, unroll=FULL_UNROLL)

        for ch in range(n_chunks):
            r0 = ch * CHUNK
            for h in range(HEADS):
                c0 = h * CHUNK
                dvm_b = dvm_scr[r0:r0 + CHUNK, c0:c0 + CHUNK]
                dws_ref[h] += _dot_nt(dvm_b, vl_scr[r0:r0 + CHUNK, c0:c0 + CHUNK])
                dvl_scr[r0:r0 + CHUNK, c0:c0 + CHUNK] = _dot_tn(ws_ref[h].astype(BF16), dvm_b)

        def phase9(rows):
            vhat, rstd_v = norm_stats(z_ref[2, rows, :])
            dvl = dvl_scr[rows, :]
            acc_scr[4] += _colsum8(dvl * vhat)
            acc_scr[5] += _colsum8(dvl)
            dvh = dvl * sg
            dz_ref[2, rows, :] = (rstd_v * (dvh - _mean(dvh) - vhat * _mean(dvh * vhat))).astype(BF16)
            qhat, rstd_q = norm_stats(q_ref[rows, :])
            ln = qhat * cg + cb
            sig_ln = _sigmoid(ln)
            gz = z_ref[0, rows, :]
            sig_g = _sigmoid(gz)
            dya = dycat_scr[rows, 0:D_MODEL]
            dz_ref[0, rows, :] = (dya * (ln * sig_ln) * (sig_g * (1.0 + gz * (1.0 - sig_g)))).astype(BF16)
            dln = (dya * (gz * sig_g)) * (sig_ln * (1.0 + ln * (1.0 - sig_ln)))
            acc_scr[2] += _colsum8(dln * qhat)
            acc_scr[3] += _colsum8(dln)
            dqh = dln * cg
            dq_ref[rows, :] = rstd_q * (dqh - _mean(dqh) - qhat * _mean(dqh * qhat))
        _row_loop(tm, phase9, unroll=FULL_UNROLL)

        @pl.when(i == n_steps - 1)
        def _():
            for qi in range(8):
                scale = 0.5 * inv_d if qi == 6 else 1.0
                sm_ref[qi:qi + 1, :] = jnp.sum(acc_scr[qi], axis=0, keepdims=True) * scale
            lane = lax.broadcasted_iota(jnp.int32, (CHUNK, CHUNK), 1)
            tile = jnp.zeros((CHUNK, CHUNK), F32)
            for h in range(HEADS):
                col = jnp.sum(dbs_acc[:, h * CHUNK:(h + 1) * CHUNK], axis=1, keepdims=True)
                tile = jnp.where(lane == h, col, tile)
            dbs_ref[...] = tile

    tok = lambda i: (i, 0)
    const2 = lambda i: (0, 0)
    return pl.pallas_call(
        body, name="middle",
        grid=(n_steps,),
        out_shape=(jax.ShapeDtypeStruct((N_GROUPS, s_len, D_MODEL), BF16),
                   jax.ShapeDtypeStruct((s_len, D_MODEL), F32),
                   jax.ShapeDtypeStruct((s_len, D_MODEL), F32),
                   jax.ShapeDtypeStruct((s_len, 2 * D_MODEL), BF16),
                   jax.ShapeDtypeStruct((s_len, D_MODEL), BF16),
                   jax.ShapeDtypeStruct((HEADS, CHUNK, CHUNK), F32),
                   jax.ShapeDtypeStruct((8, D_MODEL), F32),
                   jax.ShapeDtypeStruct((CHUNK, CHUNK), F32)),
        in_specs=[pl.BlockSpec((tm, D_MODEL), tok),
                  pl.BlockSpec((4, tm, D_MODEL), lambda i: (0, i, 0)),
                  pl.BlockSpec((tm, D_MODEL), tok),
                  pl.BlockSpec((tm, D_MODEL), tok),
                  *[pl.BlockSpec((1, D_MODEL), const2) for _ in range(5)],
                  pl.BlockSpec((1, 3 * D_MODEL), const2),
                  pl.BlockSpec((HEADS, CHUNK, CHUNK), lambda i: (0, 0, 0)),
                  pl.BlockSpec((CHUNK, D_MODEL), const2),
                  pl.BlockSpec((2 * D_MODEL, D_MODEL), const2, pipeline_mode=pl.Buffered(1))],
        out_specs=(pl.BlockSpec((4, tm, D_MODEL), lambda i: (0, i, 0)),
                   pl.BlockSpec((tm, D_MODEL), tok),
                   pl.BlockSpec((tm, D_MODEL), tok),
                   pl.BlockSpec((tm, 2 * D_MODEL), tok),
                   pl.BlockSpec((tm, D_MODEL), tok),
                   pl.BlockSpec((HEADS, CHUNK, CHUNK), lambda i: (0, 0, 0)),
                   pl.BlockSpec((8, D_MODEL), const2),
                   pl.BlockSpec((CHUNK, CHUNK), const2)),
        scratch_shapes=[pltpu.VMEM((tm, D_MODEL), BF16),
                        pltpu.VMEM((tm, D_MODEL), F32),
                        pltpu.VMEM((tm, D_MODEL), F32),
                        pltpu.VMEM((tm, 2 * D_MODEL), F32),
                        pltpu.VMEM((tm, D_MODEL), BF16),
                        pltpu.VMEM((tm, D_MODEL), F32),
                        pltpu.VMEM((8, 8, D_MODEL), F32),
                        pltpu.VMEM((CHUNK, D_MODEL), F32)],
        compiler_params=_params(dimension_semantics=("arbitrary",)),
    )(x, z6, q, target, *ln_rows, mod, w_s, bs_exp, w_out)


def _conv_bwd(kidx, dq, z6, cw4, dz6, sm1, dws, dbs, ycat, dy):
    s_len = dq.shape[0]
    tt = TIME_TILE
    n_blocks = D_MODEL // 128
    shp = [(16, D_MODEL), (D_MODEL, 128), (128, 128)]
    nj, nr, nc = 1, 256, D_MODEL
    kt = n_blocks // N_CHIPS
    tk = s_len // kt

    def body(k_ref, dq_ref, z_ref, cw_ref, dz_in, sm1_ref, dws_in, dbs_in, ycat_ref, dy_ref,
             dz_ref, dcw_ref, pk1_ref, dws_ref, dbs_ref, loss_ref, gout_ref,
             dqpad, ppad, wacc, b_pk, b_dws, b_dbs, r_pk, r_dws, r_dbs, ssem, rsem, *rs_scr):
        del k_ref, dz_in
        jb = pl.program_id(0)
        x_, y_, c_ = _pos()

        @pl.when(jb == 0)
        def _():
            b_pk[0:8, :] = _onehot_rows(sm1_ref[1:2, :], 4 * x_ + 2 * y_ + c_)
            for r, src in enumerate([2, 3, 4, 5, 0, 6]):
                b_pk[8 + r:9 + r, :] = sm1_ref[src:src + 1, :]
            b_pk[14:16, :] = jnp.zeros((2, D_MODEL), F32)
            b_dws[...] = dws_in[...]
            b_dbs[...] = dbs_in[...]

        for e in range(4):
            @pl.when(jb == (e * n_blocks) // 4)
            def _():
                _butterfly_event(e, [b_pk, b_dws, b_dbs], [r_pk, r_dws, r_dbs], ssem, rsem, _all_reduce_partners())

        @pl.when(jb == (3 * n_blocks) // 4)
        def _():
            pk1_ref[...] = b_pk[...]
            dws_ref[...] = b_dws[...]
            dbs_ref[...] = b_dbs[...]
            loss_ref[...] = jnp.broadcast_to(jnp.sum(b_pk[13:14, :], axis=1, keepdims=True), (8, 128))

        zero = jnp.zeros((CONV_PAD, 128), F32)
        for pad in (dqpad, ppad):
            pad[0:CONV_PAD, :] = zero
            pad[s_len + CONV_PAD:s_len + 2 * CONV_PAD, :] = zero
        wacc[...] = jnp.zeros_like(wacc)

        def fill(i, carry):
            t0 = pl.multiple_of(i * tt, tt)
            ppad[pl.ds(CONV_PAD + t0, tt), :] = z_ref[0, pl.ds(t0, tt), :] * _sigmoid(z_ref[1, pl.ds(t0, tt), :])
            dqpad[pl.ds(CONV_PAD + t0, tt), :] = dq_ref[pl.ds(t0, tt), :]
            return carry
        lax.fori_loop(0, s_len // tt, fill, 0)
        w = cw_ref[0]

        def bwd(i, carry):
            t0 = i * tt
            dp = jnp.zeros((tt, 128), F32)
            for o in range(1, CONV_WIDTH + 1):
                dp = dp + w[CONV_WIDTH - o:CONV_WIDTH - o + 1, :] * dqpad[pl.ds(t0 + o, tt), :]
            a = z_ref[0, pl.ds(t0, tt), :]
            sig = _sigmoid(z_ref[1, pl.ds(t0, tt), :])
            dz_ref[0, pl.ds(t0, tt), :] = (dp * sig).astype(BF16)
            dz_ref[1, pl.ds(t0, tt), :] = (dp * a * (sig * (1.0 - sig))).astype(BF16)
            dqt = dq_ref[pl.ds(t0, tt), :]
            for o in range(1, CONV_WIDTH + 1):
                wacc[o - 1] += _colsum8(dqt * ppad[pl.ds(t0 + o, tt), :])
            wacc[CONV_WIDTH] += _colsum8(dqt)
            return carry

        n_iter = s_len // tt
        n_groups = min(8, n_iter)
        rows = tk // n_groups

        def conv_and_matmul(add):
            for g in range(n_groups):
                for it in range(g * n_iter // n_groups, (g + 1) * n_iter // n_groups):
                    bwd(it, 0)
                add(_dot_tn(ycat_ref[g * rows:(g + 1) * rows, :], dy_ref[g * rows:(g + 1) * rows, :]))
            for k in range(32):
                dcw_ref[k:k + 1, :] = jnp.sum(wacc[k], axis=0, keepdims=True)
        _rs_step(jb // kt, jb % kt, kt, nj, nr, nc, conv_and_matmul, gout_ref, rs_scr)

    const2 = lambda j, k_ref: (0, 0)
    grid_spec = pltpu.PrefetchScalarGridSpec(
        num_scalar_prefetch=1,
        grid=(n_blocks,),
        in_specs=[pl.BlockSpec((s_len, 128), lambda j, k_ref: (0, j)),
                  pl.BlockSpec((2, s_len, 128), lambda j, k_ref: (2, 0, j)),
                  pl.BlockSpec((1, 32, 128), lambda j, k_ref: (j // 2, 0, j % 2)),
                  pl.BlockSpec(memory_space=pl.ANY),
                  pl.BlockSpec((8, D_MODEL), const2),
                  pl.BlockSpec((D_MODEL, 128), const2),
                  pl.BlockSpec((128, 128), const2),
                  pl.BlockSpec((tk, 512), lambda j, k_ref: (j % kt, _rs_block(j // kt, k_ref[0], nj))),
                  pl.BlockSpec((tk, D_MODEL), lambda j, k_ref: (j % kt, 0))],
        out_specs=(pl.BlockSpec((2, s_len, 128), lambda j, k_ref: (2, 0, j)),
                   pl.BlockSpec((32, 128), lambda j, k_ref: (0, j)))
        + tuple(pl.BlockSpec(s, const2) for s in shp) + (pl.BlockSpec((8, 128), const2),
                                                         pl.BlockSpec(memory_space=pl.ANY)),
        scratch_shapes=[pltpu.VMEM((s_len + 2 * CONV_PAD, 128), F32), pltpu.VMEM((s_len + 2 * CONV_PAD, 128), F32),
                        pltpu.VMEM((32, 8, 128), F32)]
        + [pltpu.VMEM(s, F32) for s in shp] + [pltpu.VMEM((3,) + s, F32) for s in shp]
        + [pltpu.SemaphoreType.DMA((9,)), pltpu.SemaphoreType.DMA((9,))] + _rs_scratch(nj, nr, nc))
    return pl.pallas_call(
        body, name="conv_bwd",
        grid_spec=grid_spec,
        out_shape=(jax.ShapeDtypeStruct(dz6.shape, BF16), jax.ShapeDtypeStruct((32, D_MODEL), F32))
        + tuple(jax.ShapeDtypeStruct(s, F32) for s in shp) + (jax.ShapeDtypeStruct((8, 128), F32),
                                                              jax.ShapeDtypeStruct((nj, 2, nr, nc), F32)),
        input_output_aliases={4: 0},
        compiler_params=_params(dimension_semantics=("arbitrary",)),
    )(kidx, dq, z6, cw4, dz6, sm1, dws, dbs, ycat, dy)


def _bwd_in(dz6, w12, x, dx2, mod, norm_g):
    s_len = x.shape[0]
    tm = TOKEN_TILE
    n_steps = s_len // tm

    def body(dz_ref, w_ref, x_ref, dx2_ref, mod_ref, g_ref, gx_ref, sm_ref, dh_scr, acc_scr):
        i = pl.program_id(0)

        @pl.when(i == 0)
        def _():
            acc_scr[...] = jnp.zeros_like(acc_scr)

        dh = jnp.zeros((tm, D_MODEL), F32)
        for j in range(N_GROUPS):
            n = _natural_group(j)
            for hf in range(2):
                dh = dh + _dot_nt(dz_ref[j, :, hf * 512:(hf + 1) * 512], w_ref[2 * n + hf])
        dh_scr[...] = dh
        scale1 = 1.0 + mod_ref[:, D_MODEL:2 * D_MODEL]
        g = g_ref[...]

        def rows_fn(rows):
            xt = x_ref[rows, :]
            r = lax.rsqrt(_mean(xt * xt) + EPS)
            xn = xt * r
            dhr = dh_scr[rows, :]
            acc_scr[0] += _colsum8(dhr)
            acc_scr[1] += _colsum8(dhr * (xn * g))
            acc_scr[2] += _colsum8(dhr * scale1 * xn)
            dxn = dhr * (g * scale1)
            gx_ref[rows, :] = dx2_ref[rows, :] + r * (dxn - xn * _mean(dxn * xn))
        _row_loop(tm, rows_fn, unroll=FULL_UNROLL)

        @pl.when(i == n_steps - 1)
        def _():
            for qi in range(8):
                sm_ref[qi:qi + 1, :] = jnp.sum(acc_scr[qi], axis=0, keepdims=True)

    tok = lambda i: (i, 0)
    const2 = lambda i: (0, 0)
    return pl.pallas_call(
        body, name="bwd_in",
        grid=(n_steps,),
        out_shape=(jax.ShapeDtypeStruct((s_len, D_MODEL), F32), jax.ShapeDtypeStruct((8, D_MODEL), F32)),
        in_specs=[pl.BlockSpec((N_GROUPS, tm, D_MODEL), lambda i: (0, i, 0)),
                  pl.BlockSpec((12, D_MODEL, 512), lambda i: (0, 0, 0), pipeline_mode=pl.Buffered(1)),
                  pl.BlockSpec((tm, D_MODEL), tok),
                  pl.BlockSpec((tm, D_MODEL), tok),
                  pl.BlockSpec((1, 3 * D_MODEL), const2),
                  pl.BlockSpec((1, D_MODEL), const2)],
        out_specs=(pl.BlockSpec((tm, D_MODEL), tok), pl.BlockSpec((8, D_MODEL), const2)),
        scratch_shapes=[pltpu.VMEM((tm, D_MODEL), F32), pltpu.VMEM((8, 8, D_MODEL), F32)],
        compiler_params=_params(dimension_semantics=("arbitrary",)),
    )(dz6, w12, x, dx2, mod, norm_g)


def _rs_shard_block(i, nj):
    return jnp.where(i < 3 * nj, i % 3, 3), jnp.where(i < 3 * nj, i // 3, i - 3 * nj)


def _rs_block(i, k, nj):
    s, j = _rs_shard_block(i, nj)
    return nj * ((k + 1 + s) % N_CHIPS) + j


def _rs_scratch(nj, nr, nc):
    nblk, nfor = N_CHIPS * nj, 3 * nj
    return [pltpu.VMEM((2, 2, nr, nc), F32),
            pltpu.VMEM((nblk, nr, nc), F32),
            pltpu.VMEM((nfor, nr, nc), BF16),
            pltpu.VMEM((nfor, nr, nc), BF16),
            pltpu.VMEM((nj, nr, nc), F32),
            pltpu.SemaphoreType.DMA,
            pltpu.SemaphoreType.DMA((nblk,)), pltpu.SemaphoreType.DMA((nblk,)),
            pltpu.SemaphoreType.DMA((nfor,)), pltpu.SemaphoreType.DMA((nfor,)),
            pltpu.SemaphoreType.DMA, pltpu.SemaphoreType.DMA]


def _rs_step(i, kk, kt, nj, nr, nc, partial_fn, out_ref, scr):
    nblk, nfor = N_CHIPS * nj, 3 * nj
    acc, recv_a, send_b, recv_b, own_ps, lsem, psend, precv, isend, irecv, fsend, frecv = scr
    x, y, c = _pos()
    k = 2 * x + y
    sib = (x, y, 1 - c)
    slot = i % 2

    @pl.when(kk == 0)
    def _():
        acc[slot] = jnp.zeros((2, nr, nc), F32)

    def pair_copy(ii, sl):
        return _rcopy(acc.at[sl, 1 - c], recv_a.at[ii], psend.at[ii], precv.at[ii], sib)

    def pair_sum(ii, sl):
        cp = pair_copy(ii, sl)
        cp.wait_recv()
        cp.wait_send()
        return acc[sl, c] + recv_a[ii]

    def ici_copy(ii, kd, rslot):
        return _rcopy(send_b.at[ii], recv_b.at[rslot], isend.at[ii], irecv.at[rslot], (kd // 2, kd % 2, c))

    def finalize(ii, sl):
        ps = pair_sum(ii, sl)
        s, j = _rs_shard_block(ii, nj)

        @pl.when(s < 3)
        def _():
            send_b[ii] = ps.astype(BF16)
            ici_copy(ii, (k + 1 + s) % N_CHIPS, (2 - s) * nj + j).start()

        @pl.when(s == 3)
        def _():
            own_ps[j] = ps

    @pl.when(jnp.logical_and(kk == kt - 1, i >= 1))
    def _():
        finalize(i - 1, 1 - slot)

    def add_partial(res):
        acc[slot, 0] += res[0:nr]
        acc[slot, 1] += res[nr:2 * nr]
    partial_fn(add_partial)

    @pl.when(kk == kt - 1)
    def _():
        pair_copy(i, slot).start()

        @pl.when(i == nblk - 1)
        def _():
            own_ps[nj - 1] = pair_sum(nblk - 1, (nblk - 1) % 2)
            for r in range(nfor):
                ici_copy(0, k, r).wait_recv()
            for j in range(nj):
                tot = own_ps[j]
                for s in range(3):
                    tot = tot + recv_b[s * nj + j].astype(F32)
                own_ps[j] = tot
            loc = pltpu.make_async_copy(own_ps, out_ref.at[:, c], lsem)
            loc.start()
            swap = _rcopy(own_ps, out_ref.at[:, c], fsend, frecv, sib)
            swap.start()
            loc.wait()
            swap.wait_send()
            _rcopy(own_ps, out_ref.at[:, 1 - c], fsend, frecv, sib).wait_recv()
            for ii in range(nfor):
                ici_copy(ii, k, 0).wait_send()


def _grad_w_in(kidx, h, dz6, sm2, dcw):
    s_len = h.shape[0]
    tk = min(K_TILE, s_len)
    kt = s_len // tk
    nj, nr, nc = 3, 512, 512
    nblk = N_CHIPS * nj

    def body(k_ref, a_ref, b_ref, sm2_ref, dcw_in, out_ref, pk2_ref, dcwsh_ref, dcb_ref, *scr):
        del k_ref
        b_pk2, b_dcw, r_pk2, r_dcw, bsend, brecv = scr[-6:]
        i = pl.program_id(0)
        kk = pl.program_id(1)
        x, y, c = _pos()
        k = 2 * x + y

        @pl.when(jnp.logical_and(kk == 0, i == 0))
        def _():
            b = 4 * x + 2 * y + c
            b_pk2[0:8, :] = _onehot_rows(sm2_ref[0:1, :], b)
            b_pk2[8:16, :] = _onehot_rows(sm2_ref[1:2, :], b)
            b_pk2[16:17, :] = sm2_ref[2:3, :]
            b_pk2[17:24, :] = jnp.zeros((7, D_MODEL), F32)
            b_dcw[...] = dcw_in[...]

        for e in range(4):
            @pl.when(jnp.logical_and(kk == 0, i == (e * nblk) // 4))
            def _():
                _butterfly_event(e, [b_pk2, b_dcw], [r_pk2, r_dcw], bsend, brecv, _all_reduce_partners())

        @pl.when(jnp.logical_and(kk == 0, i == (3 * nblk) // 4))
        def _():
            pk2_ref[...] = b_pk2[...]
            sel = jnp.zeros((32, 256), F32)
            for kc in range(N_CHIPS):
                sel = jnp.where(k == kc, b_dcw[:, kc * 256:(kc + 1) * 256], sel)
            dcwsh_ref[...] = sel
            dcb_ref[...] = b_dcw[31:32, :]

        _rs_step(i, kk, kt, nj, nr, nc, lambda add: add(_dot_tn(a_ref[...], b_ref[0])), out_ref, scr[:-6])

    def dz_index(i, kk, k_ref):
        m = _rs_block(i, k_ref[0], nj)
        return ((m // 2 + 4) % N_GROUPS, kk, m % 2)

    const2 = lambda i, kk, k_ref: (0, 0)
    small_out = [(24, D_MODEL), (32, 256), (1, D_MODEL)]
    small_buf = [(24, D_MODEL), (32, D_MODEL)]
    grid_spec = pltpu.PrefetchScalarGridSpec(
        num_scalar_prefetch=1,
        grid=(nblk, kt),
        in_specs=[pl.BlockSpec((tk, D_MODEL), lambda i, kk, k_ref: (kk, 0)),
                  pl.BlockSpec((1, tk, 512), dz_index),
                  pl.BlockSpec((8, D_MODEL), const2),
                  pl.BlockSpec((32, D_MODEL), const2)],
        out_specs=(pl.BlockSpec(memory_space=pl.ANY),) + tuple(pl.BlockSpec(s, const2) for s in small_out),
        scratch_shapes=_rs_scratch(nj, nr, nc)
        + [pltpu.VMEM(s, F32) for s in small_buf] + [pltpu.VMEM((3,) + s, F32) for s in small_buf]
        + [pltpu.SemaphoreType.DMA((6,)), pltpu.SemaphoreType.DMA((6,))])
    return pl.pallas_call(
        body, name="grad_w_in",
        grid_spec=grid_spec,
        out_shape=(jax.ShapeDtypeStruct((nj, 2, nr, nc), F32),) + tuple(jax.ShapeDtypeStruct(s, F32) for s in small_out),
        compiler_params=_params(dimension_semantics=("arbitrary", "arbitrary")),
    )(kidx, h, dz6, sm2, dcw)


def _grad_w_out(kidx, ycat, dy):
    s_len = dy.shape[0]
    tk = min(K_TILE, s_len)
    kt = s_len // tk
    nj, nr, nc = 1, 256, D_MODEL

    def body(k_ref, a_ref, b_ref, out_ref, *scr):
        del k_ref
        _rs_step(pl.program_id(0), pl.program_id(1), kt, nj, nr, nc, lambda: _dot_tn(a_ref[...], b_ref[...]), out_ref, scr)

    grid_spec = pltpu.PrefetchScalarGridSpec(
        num_scalar_prefetch=1,
        grid=(N_CHIPS * nj, kt),
        in_specs=[pl.BlockSpec((tk, 512), lambda i, kk, k_ref: (kk, _rs_block(i, k_ref[0], nj))),
                  pl.BlockSpec((tk, D_MODEL), lambda i, kk, k_ref: (kk, 0))],
        out_specs=pl.BlockSpec(memory_space=pl.ANY),
        scratch_shapes=_rs_scratch(nj, nr, nc))
    return pl.pallas_call(
        body, name="grad_w_out",
        grid_spec=grid_spec,
        out_shape=jax.ShapeDtypeStruct((nj, 2, nr, nc), F32),
        compiler_params=_params(dimension_semantics=("arbitrary", "arbitrary")),
    )(kidx, ycat, dy)


def _adamw_math(w, g, m, v):
    m = ADAM_B1 * m + (1.0 - ADAM_B1) * g
    v = ADAM_B2 * v + (1.0 - ADAM_B2) * (g * g)
    m_hat = m / (1.0 - ADAM_B1 ** ADAM_STEP)
    v_hat = v / (1.0 - ADAM_B2 ** ADAM_STEP)
    delta = -ADAM_LR * (m_hat / (jnp.sqrt(v_hat) + ADAM_EPS) + ADAM_WD * w)
    return delta, m, v


def _adamw_blocked(w, m, v, g4, name):
    nj, _, nr, nc = g4.shape

    def body(w_ref, m_ref, v_ref, g_ref, go_ref, d_ref, mo_ref, vo_ref):
        g = g_ref[0, 0]
        d, mn, vn = _adamw_math(w_ref[...], g, m_ref[...], v_ref[...])
        go_ref[...] = g
        d_ref[...] = d
        mo_ref[...] = mn
        vo_ref[...] = vn

    blk = pl.BlockSpec((nr, nc), lambda j, hf: (hf, j))
    return pl.pallas_call(
        body, name=name,
        grid=(nj, 2),
        out_shape=tuple(jax.ShapeDtypeStruct(w.shape, F32) for _ in range(4)),
        in_specs=[blk, blk, blk, pl.BlockSpec((1, 1, nr, nc), lambda j, hf: (j, hf, 0, 0))],
        out_specs=(blk, blk, blk, blk),
        compiler_params=_params(dimension_semantics=("arbitrary", "arbitrary")),
    )(w, m, v, g4)


def _adamw_w_ada(cact, pk1, pk2, w, m, v):
    def body(cact_ref, pk1_ref, pk2_ref, w_ref, m_ref, v_ref, g_ref, d_ref, mo_ref, vo_ref, gb_ref, dmod):
        x, y, _ = _pos()
        k = 2 * x + y
        dmod[:, 0:D_MODEL] = pk2_ref[0:8, :]
        dmod[:, D_MODEL:2 * D_MODEL] = pk2_ref[8:16, :]
        dmod[:, 2 * D_MODEL:3 * D_MODEL] = pk1_ref[0:8, :]
        gb_ref[...] = jnp.sum(dmod[...], axis=0, keepdims=True)
        sel = jnp.zeros((8, 768), F32)
        for kk in range(N_CHIPS):
            sel = jnp.where(k == kk, dmod[:, kk * 768:(kk + 1) * 768], sel)
        g = _dot_tn(cact_ref[...].astype(BF16), sel.astype(BF16))
        d, mn, vn = _adamw_math(w_ref[...], g, m_ref[...], v_ref[...])
        g_ref[...] = g
        d_ref[...] = d
        mo_ref[...] = mn
        vo_ref[...] = vn

    return pl.pallas_call(
        body, name="adamw_w_ada",
        out_shape=tuple(jax.ShapeDtypeStruct(w.shape, F32) for _ in range(4)) + (
            jax.ShapeDtypeStruct((1, 3 * D_MODEL), F32),),
        in_specs=[_vmem()] * 6,
        out_specs=tuple(_vmem() for _ in range(5)),
        scratch_shapes=[pltpu.VMEM((8, 3 * D_MODEL), F32)],
        compiler_params=_params(),
    )(cact, pk1, pk2, w, m, v)


def _adamw_small(items):
    n = len(items)

    def body(*refs):
        ins, outs = refs[:4 * n], refs[4 * n:]
        for i in range(n):
            w_ref, g_ref, m_ref, v_ref = ins[4 * i:4 * i + 4]
            d, mn, vn = _adamw_math(w_ref[...], g_ref[...], m_ref[...], v_ref[...])
            outs[3 * i][...] = d
            outs[3 * i + 1][...] = mn
            outs[3 * i + 2][...] = vn

    flat = [a for it in items for a in it]
    outs = pl.pallas_call(
        body, name="adamw_small",
        out_shape=tuple(jax.ShapeDtypeStruct(it[0].shape, F32) for it in items for _ in range(3)),
        in_specs=[_vmem()] * (4 * n),
        out_specs=tuple(_vmem() for _ in range(3 * n)),
        compiler_params=_params(),
    )(*flat)
    return [tuple(outs[3 * i:3 * i + 3]) for i in range(n)]


def kernel(x, c, w_ada, b_ada, norm_g, w_in, conv_w, conv_b, conv_ln_g, conv_ln_b, sg_ln_g, sg_ln_b, w_s, b_s, w_out, final_g, loss_target, m_w_ada, m_b_ada, m_norm_g, m_w_in, m_conv_w, m_conv_b, m_conv_ln_g, m_conv_ln_b, m_sg_ln_g, m_sg_ln_b, m_w_s, m_b_s, m_w_out, m_final_g, v_w_ada, v_b_ada, v_norm_g, v_w_in, v_conv_w, v_conv_b, v_conv_ln_g, v_conv_ln_b, v_sg_ln_g, v_sg_ln_b, v_w_s, v_b_s, v_w_out, v_final_g):
    s_len = x.shape[1]
    x2d = x[0]
    tgt = loss_target[0]
    row = lambda a: a.reshape(1, -1)

    kidx = (2 * lax.axis_index("x") + lax.axis_index("y")).astype(jnp.int32).reshape(1)
    cw_sh = jnp.pad(conv_w.reshape(CONV_WIDTH, 256), ((0, 1), (0, 0)))
    h, z6, mod, cact, cw4, w_in12 = _front(kidx, x2d, c, w_ada[0], b_ada, norm_g, w_in[0], cw_sh)
    w12 = w_in12.reshape(12, D_MODEL, 512)
    q, w_out4 = _conv_fwd(z6, cw4, conv_b, w_out[0])
    w_out_full = w_out4.reshape(2 * D_MODEL, D_MODEL)
    ln_rows = (conv_ln_g, conv_ln_b, sg_ln_g, sg_ln_b, row(final_g))
    bs_exp = jnp.repeat(b_s[0].T, CHUNK, axis=1)
    dz6, dq, dx2, ycat, dy, dws, sm1, dbs = _middle(x2d, z6, q, tgt, ln_rows, mod, w_s[0], bs_exp, w_out_full)
    dz6, dcw, pk1, dws_r, dbs_r, loss_t, g_w_out4 = _conv_bwd(
        kidx, dq, z6, cw4, dz6, sm1, dws.reshape(D_MODEL, CHUNK), dbs, ycat, dy)
    grad_x, sm2 = _bwd_in(dz6, w12, x2d, dx2, mod, norm_g)
    g_w_in4, pk2, dcw_sh, dcb = _grad_w_in(kidx, h, dz6, sm2, dcw)

    g_w_in, d_w_in, nm_w_in, nv_w_in = _adamw_blocked(w_in[0], m_w_in[0], v_w_in[0], g_w_in4, "adamw_w_in")
    g_w_out, d_w_out, nm_w_out, nv_w_out = _adamw_blocked(w_out[0], m_w_out[0], v_w_out[0], g_w_out4, "adamw_w_out")
    g_w_ada, d_w_ada, nm_w_ada, nv_w_ada, g_b_ada = _adamw_w_ada(cact, pk1, pk2, w_ada[0], m_w_ada[0], v_w_ada[0])

    g_norm_g = pk2[16:17]
    g_cln_g, g_cln_b, g_sln_g, g_sln_b, g_final = (pk1[8 + i:9 + i] for i in range(5))
    loss = loss_t[0, 0]
    g_conv_w = dcw_sh[:CONV_WIDTH]
    g_w_s = dws_r
    g_b_s = dbs_r[:, :HEADS].T
    small = [
        (b_ada, g_b_ada, m_b_ada, v_b_ada),
        (norm_g, g_norm_g, m_norm_g, v_norm_g),
        (conv_w.reshape(CONV_WIDTH, 256), g_conv_w, m_conv_w.reshape(CONV_WIDTH, 256), v_conv_w.reshape(CONV_WIDTH, 256)),
        (conv_b, dcb, m_conv_b, v_conv_b),
        (conv_ln_g, g_cln_g, m_conv_ln_g, v_conv_ln_g),
        (conv_ln_b, g_cln_b, m_conv_ln_b, v_conv_ln_b),
        (sg_ln_g, g_sln_g, m_sg_ln_g, v_sg_ln_g),
        (sg_ln_b, g_sln_b, m_sg_ln_b, v_sg_ln_b),
        (w_s.reshape(D_MODEL, CHUNK), g_w_s, m_w_s.reshape(D_MODEL, CHUNK), v_w_s.reshape(D_MODEL, CHUNK)),
        (b_s[0], g_b_s, m_b_s[0], v_b_s[0]),
        (row(final_g), g_final, row(m_final_g), row(v_final_g)),
    ]
    upd = _adamw_small(small)

    shapes = [w_ada.shape, b_ada.shape, norm_g.shape, w_in.shape, conv_w.shape, conv_b.shape, conv_ln_g.shape,
              conv_ln_b.shape, sg_ln_g.shape, sg_ln_b.shape, w_s.shape, b_s.shape, w_out.shape, final_g.shape]
    grads = [g_w_ada, g_b_ada, g_norm_g, g_w_in, g_conv_w, dcb, g_cln_g, g_cln_b, g_sln_g, g_sln_b, g_w_s, g_b_s,
             g_w_out, g_final]
    big = {0: (d_w_ada, nm_w_ada, nv_w_ada), 3: (d_w_in, nm_w_in, nv_w_in), 12: (d_w_out, nm_w_out, nv_w_out)}
    small_pos = [1, 2, 4, 5, 6, 7, 8, 9, 10, 11, 13]
    trip = [None] * 14
    for i, t in big.items():
        trip[i] = t
    for i, t in zip(small_pos, upd):
        trip[i] = t
    fit = lambda arrs: [a.reshape(s) for a, s in zip(arrs, shapes)]
    return (loss, grad_x.reshape(x.shape), *fit(grads), *fit([t[0] for t in trip]), *fit([t[1] for t in trip]),
            *fit([t[2] for t in trip]))
```

```python
import jax
import jax.numpy as jnp
from jax import lax
from jax.experimental import pallas as pl
from jax.experimental.pallas import tpu as pltpu

F32 = jnp.float32
BF16 = jnp.bfloat16
MESH = pl.DeviceIdType.MESH

D_MODEL = 1024
N_CHIPS = 4
HEADS = 8
CHUNK = 128
CONV_WIDTH = 31
CONV_HALF = CONV_WIDTH // 2
CONV_PAD = 16
EPS = 1e-6
ADAM_LR = 0.001
ADAM_B1 = 0.9
ADAM_B2 = 0.999
ADAM_EPS = 1e-08
ADAM_WD = 0.01
ADAM_STEP = 10

V7X_VMEM_BYTES = 64 * 1024 * 1024
VMEM_LIMIT = V7X_VMEM_BYTES - 8 * 1024 * 1024
ROWS = 16
UNROLL = 8
TOKEN_TILE = 256
FULL_UNROLL = TOKEN_TILE // ROWS
TIME_TILE = 128
K_TILE = 2048

N_GROUPS = 6


def _natural_group(j):
    return (j + 2) % N_GROUPS


def _pos():
    return lax.axis_index("x"), lax.axis_index("y"), lax.axis_index("c")


def _rcopy(src, dst, ssem, rsem, dev):
    return pltpu.make_async_remote_copy(src_ref=src, dst_ref=dst, send_sem=ssem, recv_sem=rsem,
                                        device_id=dev, device_id_type=MESH)


def _vmem():
    return pl.BlockSpec(memory_space=pltpu.VMEM)


def _params(**kw):
    return pltpu.CompilerParams(vmem_limit_bytes=VMEM_LIMIT, **kw)


def _sigmoid(v):
    return 0.5 * jnp.tanh(0.5 * v) + 0.5


def _row_loop(n_rows, body, unroll=1):
    def step(r, carry):
        body(pl.ds(pl.multiple_of(r * ROWS, ROWS), ROWS))
        return carry
    lax.fori_loop(0, n_rows // ROWS, step, 0, unroll=unroll)


def _colsum8(v):
    return v.reshape(v.shape[0] // 8, 8, v.shape[1]).sum(axis=0)


def _mean(v):
    return jnp.mean(v, axis=-1, keepdims=True)


def _dot_nn(a, b):
    return jnp.dot(a, b, preferred_element_type=F32)


def _dot_nt(a, b):
    return lax.dot_general(a, b, (((1,), (1,)), ((), ())), preferred_element_type=F32)


def _dot_tn(a, b):
    return lax.dot_general(a, b, (((0,), (0,)), ((), ())), preferred_element_type=F32)


def _remote_chip(k, r):
    return jnp.bitwise_xor(k, r + 1)


def _front(kidx, x, c, w_ada, b_ada, norm_g, w_in, cw):
    s_len = x.shape[0]
    tmh = min(512, s_len)
    tmz = min(1024, s_len)
    nh = s_len // tmh
    nz = s_len // tmz
    n_steps = nh + 12 * nz

    def remote_block(q):
        return jnp.where(q < 6, q % 2, 2), jnp.where(q < 6, q // 2, q - 6)

    def block_of(i, k):
        r, j = remote_block(jnp.maximum(i - 3, 0))
        return jnp.where(i < 3, 3 * k + i, 3 * _remote_chip(k, r) + j)

    def body(k_ref, x_ref, c_ref, bada_ref, g_ref, cw_ref, wada_hbm, win_hbm,
             h_ref, z_ref, mod_ref, cact_ref, cw4_ref, w12_hbm,
             h_all, wbuf, stage, wada_v, cslab, mslab, cw4_s,
             lsem, csend, crecv, msend, mrecv, wsend, wrecv, isend, irecv, fsend, frecv, osem):
        del k_ref
        t = pl.program_id(0)
        x_, y_, c_ = _pos()
        k = 2 * x_ + y_
        b = 4 * x_ + 2 * y_ + c_
        sib = (x_, y_, 1 - c_)

        def dev_of(r):
            kk = _remote_chip(k, r)
            return (kk // 2, kk % 2, c_)

        def ici(q, kk):
            r, j = remote_block(q)
            return _rcopy(wbuf.at[3 * kk + j, c_], wbuf.at[3 * kk + j, c_], isend.at[q], irecv.at[q], dev_of(r))

        def fwd(q, hf):
            r, j = remote_block(q)
            blk = 3 * _remote_chip(k, r) + j
            return _rcopy(wbuf.at[blk, hf], wbuf.at[blk, hf], fsend.at[q], frecv.at[q], sib)

        def c_copy(q, src):
            d = jnp.bitwise_xor(b, q)
            return _rcopy(cslab.at[src], cslab.at[src], csend.at[q - 1], crecv.at[q - 1], (d // 4, (d // 2) % 2, d % 2))

        def m_copy(r, kk):
            return _rcopy(mslab.at[kk], mslab.at[kk], msend.at[r], mrecv.at[r], dev_of(r))

        def cw_copy(r, kk):
            return _rcopy(cw4_s.at[kk], cw4_s.at[kk], wsend.at[r], wrecv.at[r], dev_of(r))

        def to_hbm(i):
            m = block_of(i, k)
            return pltpu.make_async_copy(wbuf.at[m], w12_hbm.at[m], osem.at[i])

        @pl.when(t == 0)
        def _():
            ld_w = pltpu.make_async_copy(win_hbm, stage, lsem.at[0])
            ld_w.start()
            ld_a = pltpu.make_async_copy(wada_hbm, wada_v, lsem.at[1])
            ld_a.start()
            cslab[b] = jnp.broadcast_to(c_ref[...], (8, D_MODEL))
            for q in range(1, 8):
                c_copy(q, b).start()
            cw4_s[k] = cw_ref[...]
            for r in range(3):
                cw_copy(r, k).start()
            ld_w.wait()
            for j in range(3):
                for hf in range(2):
                    wbuf[3 * k + j, hf] = stage[hf * 512:(hf + 1) * 512, j * 512:(j + 1) * 512].astype(BF16)
            for i in range(3):
                to_hbm(i).start()
            for q in range(1, 8):
                c_copy(q, jnp.bitwise_xor(b, q)).wait_recv()
            row = lax.broadcasted_iota(jnp.int32, (8, D_MODEL), 0)
            call = jnp.zeros((8, D_MODEL), F32)
            for d in range(8):
                call = jnp.where(row == d, cslab[d], call)
            cact = call * _sigmoid(call)
            cact_ref[...] = cact
            ld_a.wait()
            mslab[k] = _dot_nn(cact.astype(BF16), wada_v[...].astype(BF16))
            for r in range(3):
                m_copy(r, k).start()
            for q in range(9):
                ici(q, k).start()
            for r in range(3):
                m_copy(r, _remote_chip(k, r)).wait_recv()
            row8 = lax.broadcasted_iota(jnp.int32, (8, 768), 0)
            for kk in range(N_CHIPS):
                piece = jnp.sum(jnp.where(row8 == b, mslab[kk], 0.0), axis=0, keepdims=True)
                mod_ref[:, kk * 768:(kk + 1) * 768] = piece + bada_ref[:, kk * 768:(kk + 1) * 768]
            for r in range(3):
                cw_copy(r, _remote_chip(k, r)).wait_recv()
            cw4_ref[...] = cw4_s[...]

        @pl.when(t < nh)
        def _():
            shift = mod_ref[:, 0:D_MODEL]
            scale1 = 1.0 + mod_ref[:, D_MODEL:2 * D_MODEL]
            g = g_ref[...]
            base = t * tmh

            def rows_fn(rows):
                xt = x_ref[rows, :]
                r = lax.rsqrt(_mean(xt * xt) + EPS)
                hv = ((xt * r * g) * scale1 + shift).astype(BF16)
                h_ref[rows, :] = hv
                h_all[pl.ds(pl.multiple_of(base + rows.start, ROWS), ROWS), :] = hv
            _row_loop(tmh, rows_fn, unroll=UNROLL)

        @pl.when(t >= nh)
        def _():
            u = t - nh
            i = u // nz
            rt = u % nz
            @pl.when(jnp.logical_and(rt == 0, jnp.logical_and(i >= 2, i <= 10)))
            def _():
                q = i - 2
                r, _ = remote_block(q)
                ici(q, _remote_chip(k, r)).wait_recv()
                fwd(q, c_).start()

            @pl.when(jnp.logical_and(rt == 0, i >= 3))
            def _():
                fwd(i - 3, 1 - c_).wait_recv()
                to_hbm(i).start()
            m = block_of(i, k)
            hb = h_all[pl.ds(pl.multiple_of(rt * tmz, tmz), tmz), :]
            z_ref[0] = _dot_nn(hb, wbuf[m].reshape(D_MODEL, 512))

        @pl.when(t == n_steps - 1)
        def _():
            for q in range(1, 8):
                c_copy(q, b).wait_send()
            for r in range(3):
                m_copy(r, k).wait_send()
                cw_copy(r, k).wait_send()
            for q in range(9):
                ici(q, k).wait_send()
                fwd(q, c_).wait_send()
            for i in range(12):
                to_hbm(i).wait()

    def z_index(t, k_ref):
        u = jnp.maximum(t - nh, 0)
        m = block_of(u // nz, k_ref[0])
        return ((m // 2 + 4) % N_GROUPS, u % nz, m % 2)

    tok = lambda t, k_ref: (jnp.minimum(t, nh - 1), 0)
    const2 = lambda t, k_ref: (0, 0)
    grid_spec = pltpu.PrefetchScalarGridSpec(
        num_scalar_prefetch=1,
        grid=(n_steps,),
        in_specs=[pl.BlockSpec((tmh, D_MODEL), tok),
                  pl.BlockSpec((1, D_MODEL), const2),
                  pl.BlockSpec((1, 3 * D_MODEL), const2),
                  pl.BlockSpec((1, D_MODEL), const2),
                  pl.BlockSpec((32, 256), const2),
                  pl.BlockSpec(memory_space=pl.ANY),
                  pl.BlockSpec(memory_space=pl.ANY)],
        out_specs=(pl.BlockSpec((tmh, D_MODEL), tok),
                   pl.BlockSpec((1, tmz, 512), z_index),
                   pl.BlockSpec((1, 3 * D_MODEL), const2),
                   pl.BlockSpec((8, D_MODEL), const2),
                   pl.BlockSpec((N_CHIPS, 32, 256), lambda t, k_ref: (0, 0, 0)),
                   pl.BlockSpec(memory_space=pl.ANY)),
        scratch_shapes=[pltpu.VMEM((s_len, D_MODEL), BF16),
                        pltpu.VMEM((12, 2, 512, 512), BF16),
                        pltpu.VMEM((D_MODEL, 1536), F32),
                        pltpu.VMEM((D_MODEL, 768), F32),
                        pltpu.VMEM((8, 8, D_MODEL), F32),
                        pltpu.VMEM((N_CHIPS, 8, 768), F32),
                        pltpu.VMEM((N_CHIPS, 32, 256), F32),
                        pltpu.SemaphoreType.DMA((2,)),
                        pltpu.SemaphoreType.DMA((7,)), pltpu.SemaphoreType.DMA((7,)),
                        pltpu.SemaphoreType.DMA((3,)), pltpu.SemaphoreType.DMA((3,)),
                        pltpu.SemaphoreType.DMA((3,)), pltpu.SemaphoreType.DMA((3,)),
                        pltpu.SemaphoreType.DMA((9,)), pltpu.SemaphoreType.DMA((9,)),
                        pltpu.SemaphoreType.DMA((9,)), pltpu.SemaphoreType.DMA((9,)),
                        pltpu.SemaphoreType.DMA((12,))])
    return pl.pallas_call(
        body, name="front",
        grid_spec=grid_spec,
        out_shape=(jax.ShapeDtypeStruct((s_len, D_MODEL), BF16),
                   jax.ShapeDtypeStruct((N_GROUPS, s_len, D_MODEL), F32),
                   jax.ShapeDtypeStruct((1, 3 * D_MODEL), F32),
                   jax.ShapeDtypeStruct((8, D_MODEL), F32),
                   jax.ShapeDtypeStruct((N_CHIPS, 32, 256), F32),
                   jax.ShapeDtypeStruct((12, 2, 512, 512), BF16)),
        compiler_params=_params(dimension_semantics=("arbitrary",)),
    )(kidx, x, c, b_ada, norm_g, cw, w_ada, w_in)


def _all_reduce_partners():
    x, y, c = _pos()
    return [(x, y, 1 - c), (x, 1 - y, c), (1 - x, y, c)]


def _butterfly_event(e, bufs, recvs, ssem, rsem, partners):
    n = len(bufs)

    def copies(s):
        return [_rcopy(buf, recv.at[s], ssem.at[s * n + i], rsem.at[s * n + i], partners[s])
                for i, (buf, recv) in enumerate(zip(bufs, recvs))]

    if e > 0:
        for cp in copies(e - 1):
            cp.wait()
        for buf, recv in zip(bufs, recvs):
            buf[...] = buf[...] + recv[e - 1]
    if e < len(partners):
        for cp in copies(e):
            cp.start()


def _onehot_rows(v, b):
    row = lax.broadcasted_iota(jnp.int32, (8, v.shape[1]), 0)
    return jnp.where(row == b, jnp.broadcast_to(v, (8, v.shape[1])), 0.0)


def _conv_fwd(z6, cw4, conv_b, w_out):
    s_len = z6.shape[1]
    tt = TIME_TILE
    n_blocks = D_MODEL // 128

    def body(z_ref, cw_ref, cb_ref, wout_hbm, q_ref, wout4_hbm, ppad, stage, wbuf, lsem, isend, irecv, fsend, frecv, osem):
        jb = pl.program_id(0)
        x_, y_, c_ = _pos()
        k = 2 * x_ + y_
        sib = (x_, y_, 1 - c_)

        def ici(r, kk):
            rk = _remote_chip(k, r)
            return _rcopy(wbuf.at[kk, c_], wbuf.at[kk, c_], isend.at[r], irecv.at[r], (rk // 2, rk % 2, c_))

        def fwd(r, hf):
            kk = _remote_chip(k, r)
            return _rcopy(wbuf.at[kk, hf], wbuf.at[kk, hf], fsend.at[r], frecv.at[r], sib)

        @pl.when(jb == 0)
        def _():
            ld = pltpu.make_async_copy(wout_hbm, stage, lsem)
            ld.start()
            ld.wait()
            for hf in range(2):
                wbuf[k, hf] = stage[hf * 256:(hf + 1) * 256, :].astype(BF16)
            for r in range(3):
                ici(r, k).start()

        @pl.when(jb == (5 * n_blocks) // 8)
        def _():
            for r in range(3):
                ici(r, _remote_chip(k, r)).wait_recv()
                fwd(r, c_).start()

        zero = jnp.zeros((CONV_PAD, 128), F32)
        ppad[0:CONV_PAD, :] = zero
        ppad[s_len + CONV_PAD:s_len + 2 * CONV_PAD, :] = zero

        def fill(i, carry):
            t0 = pl.multiple_of(i * tt, tt)
            ppad[pl.ds(CONV_PAD + t0, tt), :] = z_ref[0, pl.ds(t0, tt), :] * _sigmoid(z_ref[1, pl.ds(t0, tt), :])
            return carry
        lax.fori_loop(0, s_len // tt, fill, 0)
        w = cw_ref[0]
        bias = cb_ref[...]

        def conv(i, carry):
            t0 = pl.multiple_of(i * tt, tt)
            acc = jnp.broadcast_to(bias, (tt, 128))
            for o in range(1, CONV_WIDTH + 1):
                acc = acc + w[o - 1:o, :] * ppad[pl.ds(t0 + o, tt), :]
            q_ref[pl.ds(t0, tt), :] = acc
            return carry
        lax.fori_loop(0, s_len // tt, conv, 0)

        @pl.when(jb == n_blocks - 1)
        def _():
            for r in range(3):
                fwd(r, 1 - c_).wait_recv()
            out = pltpu.make_async_copy(wbuf, wout4_hbm, osem)
            out.start()
            for r in range(3):
                ici(r, k).wait_send()
                fwd(r, c_).wait_send()
            out.wait()

    return pl.pallas_call(
        body, name="conv_fwd",
        grid=(n_blocks,),
        out_shape=(jax.ShapeDtypeStruct((s_len, D_MODEL), F32), jax.ShapeDtypeStruct((N_CHIPS, 2, 256, D_MODEL), BF16)),
        in_specs=[pl.BlockSpec((2, s_len, 128), lambda j: (2, 0, j)),
                  pl.BlockSpec((1, 32, 128), lambda j: (j // 2, 0, j % 2)),
                  pl.BlockSpec((1, 128), lambda j: (0, j)),
                  pl.BlockSpec(memory_space=pl.ANY)],
        out_specs=(pl.BlockSpec((s_len, 128), lambda j: (0, j)), pl.BlockSpec(memory_space=pl.ANY)),
        scratch_shapes=[pltpu.VMEM((s_len + 2 * CONV_PAD, 128), F32),
                        pltpu.VMEM((512, D_MODEL), F32),
                        pltpu.VMEM((N_CHIPS, 2, 256, D_MODEL), BF16),
                        pltpu.SemaphoreType.DMA,
                        pltpu.SemaphoreType.DMA((3,)), pltpu.SemaphoreType.DMA((3,)),
                        pltpu.SemaphoreType.DMA((3,)), pltpu.SemaphoreType.DMA((3,)),
                        pltpu.SemaphoreType.DMA],
        compiler_params=_params(dimension_semantics=("arbitrary",)),
    )(z6, cw4, conv_b, w_out)


def _middle(x, z6, q, target, ln_rows, mod, w_s, bs_exp, w_out):
    s_len = x.shape[0]
    tm = TOKEN_TILE
    n_steps = s_len // tm
    n_chunks = tm // CHUNK
    inv_d = 1.0 / D_MODEL

    def body(x_ref, z_ref, q_ref, tgt_ref, cg_ref, cb_ref, sg_ref, sb_ref, fg_ref, mod_ref, ws_ref, bs_ref, wout_ref,
             dz_ref, dq_ref, dx2_ref, ycat_ref, dy_ref, dws_ref, sm_ref, dbs_ref,
             vl_scr, vm_scr, y_scr, dycat_scr, dvm_scr, dvl_scr, acc_scr, dbs_acc, keep, rstd_scr):
        i = pl.program_id(0)

        @pl.when(i == 0)
        def _():
            acc_scr[...] = jnp.zeros_like(acc_scr)
            dbs_acc[...] = jnp.zeros_like(dbs_acc)
            dws_ref[...] = jnp.zeros_like(dws_ref)

        cg, cb, sg, sb, fg = cg_ref[...], cb_ref[...], sg_ref[...], sb_ref[...], fg_ref[...]
        gm = mod_ref[:, 2 * D_MODEL:3 * D_MODEL]

        def norm_stats(t):
            c = t - _mean(t)
            rstd = lax.rsqrt(_mean(c * c) + EPS)
            return c * rstd, rstd

        def phase1(rows):
            qhat, rstd_q = norm_stats(q_ref[rows, :])
            ln = qhat * cg + cb
            gz = z_ref[0, rows, :]
            sig_ln = _sigmoid(ln)
            sig_g = _sigmoid(gz)
            ycat_ref[rows, 0:D_MODEL] = ((ln * sig_ln) * (gz * sig_g)).astype(BF16)
            vhat, rstd_v = norm_stats(z_ref[2, rows, :])
            vl_scr[rows, :] = (vhat * sg + sb).astype(BF16)
            keep[0, rows, :] = qhat
            keep[1, rows, :] = vhat
            keep[2, rows, :] = sig_ln
            keep[3, rows, :] = sig_g
            rstd_scr[0, rows, :] = rstd_q
            rstd_scr[1, rows, :] = rstd_v
        _row_loop(tm, phase1, unroll=FULL_UNROLL)

        for ch in range(n_chunks):
            r0 = ch * CHUNK
            for h in range(HEADS):
                c0 = h * CHUNK
                vm_scr[r0:r0 + CHUNK, c0:c0 + CHUNK] = (
                    _dot_nn(ws_ref[h].astype(BF16), vl_scr[r0:r0 + CHUNK, c0:c0 + CHUNK]) + bs_ref[:, c0:c0 + CHUNK])

        def phase3(rows):
            bg = z_ref[3, rows, :]
            sig_b = _sigmoid(bg)
            keep[4, rows, :] = sig_b
            ycat_ref[rows, D_MODEL:2 * D_MODEL] = (z_ref[1, rows, :] * vm_scr[rows, :] * (bg * sig_b)).astype(BF16)
        _row_loop(tm, phase3, unroll=FULL_UNROLL)

        y_scr[...] = _dot_nn(ycat_ref[...], wout_ref[...])

        def phase5(rows):
            y = y_scr[rows, :]
            x2 = x_ref[rows, :] + gm * y
            r2 = lax.rsqrt(_mean(x2 * x2) + EPS)
            xn2 = x2 * r2
            diff = xn2 * fg - tgt_ref[rows, :]
            acc_scr[6] += _colsum8(diff * diff)
            dout = diff * inv_d
            acc_scr[0] += _colsum8(dout * xn2)
            dxn = dout * fg
            dx2 = r2 * (dxn - xn2 * _mean(dxn * xn2))
            dx2_ref[rows, :] = dx2
            acc_scr[1] += _colsum8(dx2 * y)
            dy_ref[rows, :] = (dx2 * gm).astype(BF16)
        _row_loop(tm, phase5, unroll=FULL_UNROLL)

        dycat_scr[...] = _dot_nt(dy_ref[...], wout_ref[...])

        def phase7(rows):
            dyb = dycat_scr[rows, D_MODEL:2 * D_MODEL]
            u = z_ref[1, rows, :]
            bg = z_ref[3, rows, :]
            vm = vm_scr[rows, :]
            sig = keep[4, rows, :]
            silu = bg * sig
            dz_ref[1, rows, :] = (dyb * vm * silu).astype(BF16)
            dvm = dyb * u * silu
            dz_ref[3, rows, :] = (dyb * u * vm * (sig * (1.0 + bg * (1.0 - sig)))).astype(BF16)
            dvm_scr[rows, :] = dvm.astype(BF16)
            pos = pl.ds(pl.multiple_of(rows.start % CHUNK, ROWS), ROWS)
            dbs_acc[pos, :] += dvm
        _row_loop(tm, phase7, unroll=FULL_UNROLL)

        for ch in range(n_chunks):
            r0 = ch * CHUNK
            for h in range(HEADS):
                c0 = h * CHUNK
                dvm_b = dvm_scr[r0:r0 + CHUNK, c0:c0 + CHUNK]
                dws_ref[h] += _dot_nt(dvm_b, vl_scr[r0:r0 + CHUNK, c0:c0 + CHUNK])
                dvl_scr[r0:r0 + CHUNK, c0:c0 + CHUNK] = _dot_tn(ws_ref[h].astype(BF16), dvm_b)

        def phase9(rows):
            vhat, rstd_v = keep[1, rows, :], rstd_scr[1, rows, :]
            dvl = dvl_scr[rows, :]
            acc_scr[4] += _colsum8(dvl * vhat)
            acc_scr[5] += _colsum8(dvl)
            dvh = dvl * sg
            dz_ref[2, rows, :] = (rstd_v * (dvh - _mean(dvh) - vhat * _mean(dvh * vhat))).astype(BF16)
            qhat, rstd_q = keep[0, rows, :], rstd_scr[0, rows, :]
            ln = qhat * cg + cb
            sig_ln = keep[2, rows, :]
            gz = z_ref[0, rows, :]
            sig_g = keep[3, rows, :]
            dya = dycat_scr[rows, 0:D_MODEL]
            dz_ref[0, rows, :] = (dya * (ln * sig_ln) * (sig_g * (1.0 + gz * (1.0 - sig_g)))).astype(BF16)
            dln = (dya * (gz * sig_g)) * (sig_ln * (1.0 + ln * (1.0 - sig_ln)))
            acc_scr[2] += _colsum8(dln * qhat)
            acc_scr[3] += _colsum8(dln)
            dqh = dln * cg
            dq_ref[rows, :] = rstd_q * (dqh - _mean(dqh) - qhat * _mean(dqh * qhat))
        _row_loop(tm, phase9, unroll=FULL_UNROLL)

        @pl.when(i == n_steps - 1)
        def _():
            for qi in range(8):
                scale = 0.5 * inv_d if qi == 6 else 1.0
                sm_ref[qi:qi + 1, :] = jnp.sum(acc_scr[qi], axis=0, keepdims=True) * scale
            lane = lax.broadcasted_iota(jnp.int32, (CHUNK, CHUNK), 1)
            tile = jnp.zeros((CHUNK, CHUNK), F32)
            for h in range(HEADS):
                col = jnp.sum(dbs_acc[:, h * CHUNK:(h + 1) * CHUNK], axis=1, keepdims=True)
                tile = jnp.where(lane == h, col, tile)
            dbs_ref[...] = tile

    tok = lambda i: (i, 0)
    const2 = lambda i: (0, 0)
    return pl.pallas_call(
        body, name="middle",
        grid=(n_steps,),
        out_shape=(jax.ShapeDtypeStruct((N_GROUPS, s_len, D_MODEL), BF16),
                   jax.ShapeDtypeStruct((s_len, D_MODEL), F32),
                   jax.ShapeDtypeStruct((s_len, D_MODEL), F32),
                   jax.ShapeDtypeStruct((s_len, 2 * D_MODEL), BF16),
                   jax.ShapeDtypeStruct((s_len, D_MODEL), BF16),
                   jax.ShapeDtypeStruct((HEADS, CHUNK, CHUNK), F32),
                   jax.ShapeDtypeStruct((8, D_MODEL), F32),
                   jax.ShapeDtypeStruct((CHUNK, CHUNK), F32)),
        in_specs=[pl.BlockSpec((tm, D_MODEL), tok),
                  pl.BlockSpec((4, tm, D_MODEL), lambda i: (0, i, 0)),
                  pl.BlockSpec((tm, D_MODEL), tok),
                  pl.BlockSpec((tm, D_MODEL), tok),
                  *[pl.BlockSpec((1, D_MODEL), const2) for _ in range(5)],
                  pl.BlockSpec((1, 3 * D_MODEL), const2),
                  pl.BlockSpec((HEADS, CHUNK, CHUNK), lambda i: (0, 0, 0)),
                  pl.BlockSpec((CHUNK, D_MODEL), const2),
                  pl.BlockSpec((2 * D_MODEL, D_MODEL), const2, pipeline_mode=pl.Buffered(1))],
        out_specs=(pl.BlockSpec((4, tm, D_MODEL), lambda i: (0, i, 0)),
                   pl.BlockSpec((tm, D_MODEL), tok),
                   pl.BlockSpec((tm, D_MODEL), tok),
                   pl.BlockSpec((tm, 2 * D_MODEL), tok),
                   pl.BlockSpec((tm, D_MODEL), tok),
                   pl.BlockSpec((HEADS, CHUNK, CHUNK), lambda i: (0, 0, 0)),
                   pl.BlockSpec((8, D_MODEL), const2),
                   pl.BlockSpec((CHUNK, CHUNK), const2)),
        scratch_shapes=[pltpu.VMEM((tm, D_MODEL), BF16),
                        pltpu.VMEM((tm, D_MODEL), F32),
                        pltpu.VMEM((tm, D_MODEL), F32),
                        pltpu.VMEM((tm, 2 * D_MODEL), F32),
                        pltpu.VMEM((tm, D_MODEL), BF16),
                        pltpu.VMEM((tm, D_MODEL), F32),
                        pltpu.VMEM((8, 8, D_MODEL), F32),
                        pltpu.VMEM((CHUNK, D_MODEL), F32),
                        pltpu.VMEM((5, tm, D_MODEL), F32),
                        pltpu.VMEM((2, tm, 1), F32)],
        compiler_params=_params(dimension_semantics=("arbitrary",)),
    )(x, z6, q, target, *ln_rows, mod, w_s, bs_exp, w_out)


def _conv_bwd(kidx, dq, z6, cw4, dz6, sm1, dws, dbs, ycat, dy):
    s_len = dq.shape[0]
    tt = TIME_TILE
    n_blocks = D_MODEL // 128
    shp = [(16, D_MODEL), (D_MODEL, 128), (128, 128)]
    nj, nr, nc = 1, 256, D_MODEL
    kt = n_blocks // N_CHIPS
    tk = s_len // kt

    def body(k_ref, dq_ref, z_ref, cw_ref, dz_in, sm1_ref, dws_in, dbs_in, ycat_ref, dy_ref,
             dz_ref, dcw_ref, pk1_ref, dws_ref, dbs_ref, loss_ref, gout_ref,
             dqpad, ppad, wacc, b_pk, b_dws, b_dbs, r_pk, r_dws, r_dbs, ssem, rsem, *rs_scr):
        del k_ref, dz_in
        jb = pl.program_id(0)
        x_, y_, c_ = _pos()

        @pl.when(jb == 0)
        def _():
            b_pk[0:8, :] = _onehot_rows(sm1_ref[1:2, :], 4 * x_ + 2 * y_ + c_)
            for r, src in enumerate([2, 3, 4, 5, 0, 6]):
                b_pk[8 + r:9 + r, :] = sm1_ref[src:src + 1, :]
            b_pk[14:16, :] = jnp.zeros((2, D_MODEL), F32)
            b_dws[...] = dws_in[...]
            b_dbs[...] = dbs_in[...]

        for e in range(4):
            @pl.when(jb == (e * n_blocks) // 4)
            def _():
                _butterfly_event(e, [b_pk, b_dws, b_dbs], [r_pk, r_dws, r_dbs], ssem, rsem, _all_reduce_partners())

        @pl.when(jb == (3 * n_blocks) // 4)
        def _():
            pk1_ref[...] = b_pk[...]
            dws_ref[...] = b_dws[...]
            dbs_ref[...] = b_dbs[...]
            loss_ref[...] = jnp.broadcast_to(jnp.sum(b_pk[13:14, :], axis=1, keepdims=True), (8, 128))

        zero = jnp.zeros((CONV_PAD, 128), F32)
        for pad in (dqpad, ppad):
            pad[0:CONV_PAD, :] = zero
            pad[s_len + CONV_PAD:s_len + 2 * CONV_PAD, :] = zero
        wacc[...] = jnp.zeros_like(wacc)

        def fill(i, carry):
            t0 = pl.multiple_of(i * tt, tt)
            ppad[pl.ds(CONV_PAD + t0, tt), :] = z_ref[0, pl.ds(t0, tt), :] * _sigmoid(z_ref[1, pl.ds(t0, tt), :])
            dqpad[pl.ds(CONV_PAD + t0, tt), :] = dq_ref[pl.ds(t0, tt), :]
            return carry
        lax.fori_loop(0, s_len // tt, fill, 0)
        w = cw_ref[0]

        def bwd(i, carry):
            t0 = i * tt
            dp = jnp.zeros((tt, 128), F32)
            for o in range(1, CONV_WIDTH + 1):
                dp = dp + w[CONV_WIDTH - o:CONV_WIDTH - o + 1, :] * dqpad[pl.ds(t0 + o, tt), :]
            a = z_ref[0, pl.ds(t0, tt), :]
            sig = _sigmoid(z_ref[1, pl.ds(t0, tt), :])
            dz_ref[0, pl.ds(t0, tt), :] = (dp * sig).astype(BF16)
            dz_ref[1, pl.ds(t0, tt), :] = (dp * a * (sig * (1.0 - sig))).astype(BF16)
            dqt = dq_ref[pl.ds(t0, tt), :]
            for o in range(1, CONV_WIDTH + 1):
                wacc[o - 1] += _colsum8(dqt * ppad[pl.ds(t0 + o, tt), :])
            wacc[CONV_WIDTH] += _colsum8(dqt)
            return carry

        n_iter = s_len // tt
        n_groups = min(8, n_iter)
        rows = tk // n_groups

        def conv_and_matmul(add):
            for g in range(n_groups):
                for it in range(g * n_iter // n_groups, (g + 1) * n_iter // n_groups):
                    bwd(it, 0)
                add(_dot_tn(ycat_ref[g * rows:(g + 1) * rows, :], dy_ref[g * rows:(g + 1) * rows, :]))
            for k in range(32):
                dcw_ref[k:k + 1, :] = jnp.sum(wacc[k], axis=0, keepdims=True)
        _rs_step(jb // kt, jb % kt, kt, nj, nr, nc, conv_and_matmul, gout_ref, rs_scr)

    const2 = lambda j, k_ref: (0, 0)
    grid_spec = pltpu.PrefetchScalarGridSpec(
        num_scalar_prefetch=1,
        grid=(n_blocks,),
        in_specs=[pl.BlockSpec((s_len, 128), lambda j, k_ref: (0, j)),
                  pl.BlockSpec((2, s_len, 128), lambda j, k_ref: (2, 0, j)),
                  pl.BlockSpec((1, 32, 128), lambda j, k_ref: (j // 2, 0, j % 2)),
                  pl.BlockSpec(memory_space=pl.ANY),
                  pl.BlockSpec((8, D_MODEL), const2),
                  pl.BlockSpec((D_MODEL, 128), const2),
                  pl.BlockSpec((128, 128), const2),
                  pl.BlockSpec((tk, 512), lambda j, k_ref: (j % kt, _rs_block(j // kt, k_ref[0], nj))),
                  pl.BlockSpec((tk, D_MODEL), lambda j, k_ref: (j % kt, 0))],
        out_specs=(pl.BlockSpec((2, s_len, 128), lambda j, k_ref: (2, 0, j)),
                   pl.BlockSpec((32, 128), lambda j, k_ref: (0, j)))
        + tuple(pl.BlockSpec(s, const2) for s in shp) + (pl.BlockSpec((8, 128), const2),
                                                         pl.BlockSpec(memory_space=pl.ANY)),
        scratch_shapes=[pltpu.VMEM((s_len + 2 * CONV_PAD, 128), F32), pltpu.VMEM((s_len + 2 * CONV_PAD, 128), F32),
                        pltpu.VMEM((32, 8, 128), F32)]
        + [pltpu.VMEM(s, F32) for s in shp] + [pltpu.VMEM((3,) + s, F32) for s in shp]
        + [pltpu.SemaphoreType.DMA((9,)), pltpu.SemaphoreType.DMA((9,))] + _rs_scratch(nj, nr, nc))
    return pl.pallas_call(
        body, name="conv_bwd",
        grid_spec=grid_spec,
        out_shape=(jax.ShapeDtypeStruct(dz6.shape, BF16), jax.ShapeDtypeStruct((32, D_MODEL), F32))
        + tuple(jax.ShapeDtypeStruct(s, F32) for s in shp) + (jax.ShapeDtypeStruct((8, 128), F32),
                                                              jax.ShapeDtypeStruct((nj, 2, nr, nc), F32)),
        input_output_aliases={4: 0},
        compiler_params=_params(dimension_semantics=("arbitrary",)),
    )(kidx, dq, z6, cw4, dz6, sm1, dws, dbs, ycat, dy)


def _bwd_in(dz6, w12, x, dx2, mod, norm_g):
    s_len = x.shape[0]
    tm = TOKEN_TILE
    n_steps = s_len // tm

    def body(dz_ref, w_ref, x_ref, dx2_ref, mod_ref, g_ref, gx_ref, sm_ref, dh_scr, acc_scr):
        i = pl.program_id(0)

        @pl.when(i == 0)
        def _():
            acc_scr[...] = jnp.zeros_like(acc_scr)

        dh = jnp.zeros((tm, D_MODEL), F32)
        for j in range(N_GROUPS):
            n = _natural_group(j)
            for hf in range(2):
                dh = dh + _dot_nt(dz_ref[j, :, hf * 512:(hf + 1) * 512], w_ref[2 * n + hf])
        dh_scr[...] = dh
        scale1 = 1.0 + mod_ref[:, D_MODEL:2 * D_MODEL]
        g = g_ref[...]

        def rows_fn(rows):
            xt = x_ref[rows, :]
            r = lax.rsqrt(_mean(xt * xt) + EPS)
            xn = xt * r
            dhr = dh_scr[rows, :]
            acc_scr[0] += _colsum8(dhr)
            acc_scr[1] += _colsum8(dhr * (xn * g))
            acc_scr[2] += _colsum8(dhr * scale1 * xn)
            dxn = dhr * (g * scale1)
            gx_ref[rows, :] = dx2_ref[rows, :] + r * (dxn - xn * _mean(dxn * xn))
        _row_loop(tm, rows_fn, unroll=FULL_UNROLL)

        @pl.when(i == n_steps - 1)
        def _():
            for qi in range(8):
                sm_ref[qi:qi + 1, :] = jnp.sum(acc_scr[qi], axis=0, keepdims=True)

    tok = lambda i: (i, 0)
    const2 = lambda i: (0, 0)
    return pl.pallas_call(
        body, name="bwd_in",
        grid=(n_steps,),
        out_shape=(jax.ShapeDtypeStruct((s_len, D_MODEL), F32), jax.ShapeDtypeStruct((8, D_MODEL), F32)),
        in_specs=[pl.BlockSpec((N_GROUPS, tm, D_MODEL), lambda i: (0, i, 0)),
                  pl.BlockSpec((12, D_MODEL, 512), lambda i: (0, 0, 0), pipeline_mode=pl.Buffered(1)),
                  pl.BlockSpec((tm, D_MODEL), tok),
                  pl.BlockSpec((tm, D_MODEL), tok),
                  pl.BlockSpec((1, 3 * D_MODEL), const2),
                  pl.BlockSpec((1, D_MODEL), const2)],
        out_specs=(pl.BlockSpec((tm, D_MODEL), tok), pl.BlockSpec((8, D_MODEL), const2)),
        scratch_shapes=[pltpu.VMEM((tm, D_MODEL), F32), pltpu.VMEM((8, 8, D_MODEL), F32)],
        compiler_params=_params(dimension_semantics=("arbitrary",)),
    )(dz6, w12, x, dx2, mod, norm_g)


def _rs_shard_block(i, nj):
    return jnp.where(i < 3 * nj, i % 3, 3), jnp.where(i < 3 * nj, i // 3, i - 3 * nj)


def _rs_block(i, k, nj):
    s, j = _rs_shard_block(i, nj)
    return nj * ((k + 1 + s) % N_CHIPS) + j


def _rs_scratch(nj, nr, nc):
    nblk, nfor = N_CHIPS * nj, 3 * nj
    return [pltpu.VMEM((2, 2, nr, nc), F32),
            pltpu.VMEM((nblk, nr, nc), F32),
            pltpu.VMEM((nfor, nr, nc), BF16),
            pltpu.VMEM((nfor, nr, nc), BF16),
            pltpu.VMEM((nj, nr, nc), F32),
            pltpu.SemaphoreType.DMA,
            pltpu.SemaphoreType.DMA((nblk,)), pltpu.SemaphoreType.DMA((nblk,)),
            pltpu.SemaphoreType.DMA((nfor,)), pltpu.SemaphoreType.DMA((nfor,)),
            pltpu.SemaphoreType.DMA, pltpu.SemaphoreType.DMA]


def _rs_step(i, kk, kt, nj, nr, nc, partial_fn, out_ref, scr):
    nblk, nfor = N_CHIPS * nj, 3 * nj
    acc, recv_a, send_b, recv_b, own_ps, lsem, psend, precv, isend, irecv, fsend, frecv = scr
    x, y, c = _pos()
    k = 2 * x + y
    sib = (x, y, 1 - c)
    slot = i % 2

    @pl.when(kk == 0)
    def _():
        acc[slot] = jnp.zeros((2, nr, nc), F32)

    def pair_copy(ii, sl):
        return _rcopy(acc.at[sl, 1 - c], recv_a.at[ii], psend.at[ii], precv.at[ii], sib)

    def pair_sum(ii, sl):
        cp = pair_copy(ii, sl)
        cp.wait_recv()
        cp.wait_send()
        return acc[sl, c] + recv_a[ii]

    def ici_copy(ii, kd, rslot):
        return _rcopy(send_b.at[ii], recv_b.at[rslot], isend.at[ii], irecv.at[rslot], (kd // 2, kd % 2, c))

    def finalize(ii, sl):
        ps = pair_sum(ii, sl)
        s, j = _rs_shard_block(ii, nj)

        @pl.when(s < 3)
        def _():
            send_b[ii] = ps.astype(BF16)
            ici_copy(ii, (k + 1 + s) % N_CHIPS, (2 - s) * nj + j).start()

        @pl.when(s == 3)
        def _():
            own_ps[j] = ps

    @pl.when(jnp.logical_and(kk == kt - 1, i >= 1))
    def _():
        finalize(i - 1, 1 - slot)

    def add_partial(res):
        acc[slot, 0] += res[0:nr]
        acc[slot, 1] += res[nr:2 * nr]
    partial_fn(add_partial)

    @pl.when(kk == kt - 1)
    def _():
        pair_copy(i, slot).start()

        @pl.when(i == nblk - 1)
        def _():
            own_ps[nj - 1] = pair_sum(nblk - 1, (nblk - 1) % 2)
            for r in range(nfor):
                ici_copy(0, k, r).wait_recv()
            for j in range(nj):
                tot = own_ps[j]
                for s in range(3):
                    tot = tot + recv_b[s * nj + j].astype(F32)
                own_ps[j] = tot
            loc = pltpu.make_async_copy(own_ps, out_ref.at[:, c], lsem)
            loc.start()
            swap = _rcopy(own_ps, out_ref.at[:, c], fsend, frecv, sib)
            swap.start()
            loc.wait()
            swap.wait_send()
            _rcopy(own_ps, out_ref.at[:, 1 - c], fsend, frecv, sib).wait_recv()
            for ii in range(nfor):
                ici_copy(ii, k, 0).wait_send()


def _grad_w_in(kidx, h, dz6, sm2, dcw):
    s_len = h.shape[0]
    tk = min(K_TILE, s_len)
    kt = s_len // tk
    nj, nr, nc = 3, 512, 512
    nblk = N_CHIPS * nj

    def body(k_ref, a_ref, b_ref, sm2_ref, dcw_in, out_ref, pk2_ref, dcwsh_ref, dcb_ref, *scr):
        del k_ref
        b_pk2, b_dcw, r_pk2, r_dcw, bsend, brecv = scr[-6:]
        i = pl.program_id(0)
        kk = pl.program_id(1)
        x, y, c = _pos()
        k = 2 * x + y

        @pl.when(jnp.logical_and(kk == 0, i == 0))
        def _():
            b = 4 * x + 2 * y + c
            b_pk2[0:8, :] = _onehot_rows(sm2_ref[0:1, :], b)
            b_pk2[8:16, :] = _onehot_rows(sm2_ref[1:2, :], b)
            b_pk2[16:17, :] = sm2_ref[2:3, :]
            b_pk2[17:24, :] = jnp.zeros((7, D_MODEL), F32)
            b_dcw[...] = dcw_in[...]

        for e in range(4):
            @pl.when(jnp.logical_and(kk == 0, i == (e * nblk) // 4))
            def _():
                _butterfly_event(e, [b_pk2, b_dcw], [r_pk2, r_dcw], bsend, brecv, _all_reduce_partners())

        @pl.when(jnp.logical_and(kk == 0, i == (3 * nblk) // 4))
        def _():
            pk2_ref[...] = b_pk2[...]
            sel = jnp.zeros((32, 256), F32)
            for kc in range(N_CHIPS):
                sel = jnp.where(k == kc, b_dcw[:, kc * 256:(kc + 1) * 256], sel)
            dcwsh_ref[...] = sel
            dcb_ref[...] = b_dcw[31:32, :]

        _rs_step(i, kk, kt, nj, nr, nc, lambda add: add(_dot_tn(a_ref[...], b_ref[0])), out_ref, scr[:-6])

    def dz_index(i, kk, k_ref):
        m = _rs_block(i, k_ref[0], nj)
        return ((m // 2 + 4) % N_GROUPS, kk, m % 2)

    const2 = lambda i, kk, k_ref: (0, 0)
    small_out = [(24, D_MODEL), (32, 256), (1, D_MODEL)]
    small_buf = [(24, D_MODEL), (32, D_MODEL)]
    grid_spec = pltpu.PrefetchScalarGridSpec(
        num_scalar_prefetch=1,
        grid=(nblk, kt),
        in_specs=[pl.BlockSpec((tk, D_MODEL), lambda i, kk, k_ref: (kk, 0)),
                  pl.BlockSpec((1, tk, 512), dz_index),
                  pl.BlockSpec((8, D_MODEL), const2),
                  pl.BlockSpec((32, D_MODEL), const2)],
        out_specs=(pl.BlockSpec(memory_space=pl.ANY),) + tuple(pl.BlockSpec(s, const2) for s in small_out),
        scratch_shapes=_rs_scratch(nj, nr, nc)
        + [pltpu.VMEM(s, F32) for s in small_buf] + [pltpu.VMEM((3,) + s, F32) for s in small_buf]
        + [pltpu.SemaphoreType.DMA((6,)), pltpu.SemaphoreType.DMA((6,))])
    return pl.pallas_call(
        body, name="grad_w_in",
        grid_spec=grid_spec,
        out_shape=(jax.ShapeDtypeStruct((nj, 2, nr, nc), F32),) + tuple(jax.ShapeDtypeStruct(s, F32) for s in small_out),
        compiler_params=_params(dimension_semantics=("arbitrary", "arbitrary")),
    )(kidx, h, dz6, sm2, dcw)


def _adamw_math(w, g, m, v):
    m = ADAM_B1 * m + (1.0 - ADAM_B1) * g
    v = ADAM_B2 * v + (1.0 - ADAM_B2) * (g * g)
    m_hat = m / (1.0 - ADAM_B1 ** ADAM_STEP)
    v_hat = v / (1.0 - ADAM_B2 ** ADAM_STEP)
    delta = -ADAM_LR * (m_hat / (jnp.sqrt(v_hat) + ADAM_EPS) + ADAM_WD * w)
    return delta, m, v


def _adamw_blocked(w, m, v, g4, name):
    nj, _, nr, nc = g4.shape

    def body(w_ref, m_ref, v_ref, g_ref, go_ref, d_ref, mo_ref, vo_ref):
        g = g_ref[0, 0]
        d, mn, vn = _adamw_math(w_ref[...], g, m_ref[...], v_ref[...])
        go_ref[...] = g
        d_ref[...] = d
        mo_ref[...] = mn
        vo_ref[...] = vn

    blk = pl.BlockSpec((nr, nc), lambda j, hf: (hf, j))
    return pl.pallas_call(
        body, name=name,
        grid=(nj, 2),
        out_shape=tuple(jax.ShapeDtypeStruct(w.shape, F32) for _ in range(4)),
        in_specs=[blk, blk, blk, pl.BlockSpec((1, 1, nr, nc), lambda j, hf: (j, hf, 0, 0))],
        out_specs=(blk, blk, blk, blk),
        compiler_params=_params(dimension_semantics=("arbitrary", "arbitrary")),
    )(w, m, v, g4)


def _adamw_w_ada(cact, pk1, pk2, w, m, v):
    rb = 256

    def body(cact_ref, pk1_ref, pk2_ref, w_ref, m_ref, v_ref, g_ref, d_ref, mo_ref, vo_ref, gb_ref, dmod, sel_scr):
        @pl.when(pl.program_id(0) == 0)
        def _():
            x, y, _ = _pos()
            k = 2 * x + y
            dmod[:, 0:D_MODEL] = pk2_ref[0:8, :]
            dmod[:, D_MODEL:2 * D_MODEL] = pk2_ref[8:16, :]
            dmod[:, 2 * D_MODEL:3 * D_MODEL] = pk1_ref[0:8, :]
            gb_ref[...] = jnp.sum(dmod[...], axis=0, keepdims=True)
            sel = jnp.zeros((8, 768), F32)
            for kk in range(N_CHIPS):
                sel = jnp.where(k == kk, dmod[:, kk * 768:(kk + 1) * 768], sel)
            sel_scr[...] = sel

        g = _dot_tn(cact_ref[...].astype(BF16), sel_scr[...].astype(BF16))
        d, mn, vn = _adamw_math(w_ref[...], g, m_ref[...], v_ref[...])
        g_ref[...] = g
        d_ref[...] = d
        mo_ref[...] = mn
        vo_ref[...] = vn

    blk = pl.BlockSpec((rb, 768), lambda i: (i, 0))
    const2 = lambda i: (0, 0)
    return pl.pallas_call(
        body, name="adamw_w_ada",
        grid=(D_MODEL // rb,),
        out_shape=tuple(jax.ShapeDtypeStruct(w.shape, F32) for _ in range(4)) + (
            jax.ShapeDtypeStruct((1, 3 * D_MODEL), F32),),
        in_specs=[pl.BlockSpec((8, rb), lambda i: (0, i)), pl.BlockSpec((16, D_MODEL), const2),
                  pl.BlockSpec((24, D_MODEL), const2), blk, blk, blk],
        out_specs=(blk, blk, blk, blk, pl.BlockSpec((1, 3 * D_MODEL), const2)),
        scratch_shapes=[pltpu.VMEM((8, 3 * D_MODEL), F32), pltpu.VMEM((8, 768), F32)],
        compiler_params=_params(dimension_semantics=("arbitrary",)),
    )(cact, pk1, pk2, w, m, v)


def _adamw_small(items):
    n = len(items)

    def body(*refs):
        ins, outs = refs[:4 * n], refs[4 * n:]
        for i in range(n):
            w_ref, g_ref, m_ref, v_ref = ins[4 * i:4 * i + 4]
            d, mn, vn = _adamw_math(w_ref[...], g_ref[...], m_ref[...], v_ref[...])
            outs[3 * i][...] = d
            outs[3 * i + 1][...] = mn
            outs[3 * i + 2][...] = vn

    flat = [a for it in items for a in it]
    outs = pl.pallas_call(
        body, name="adamw_small",
        out_shape=tuple(jax.ShapeDtypeStruct(it[0].shape, F32) for it in items for _ in range(3)),
        in_specs=[_vmem()] * (4 * n),
        out_specs=tuple(_vmem() for _ in range(3 * n)),
        compiler_params=_params(),
    )(*flat)
    return [tuple(outs[3 * i:3 * i + 3]) for i in range(n)]


def kernel(x, c, w_ada, b_ada, norm_g, w_in, conv_w, conv_b, conv_ln_g, conv_ln_b, sg_ln_g, sg_ln_b, w_s, b_s, w_out, final_g, loss_target, m_w_ada, m_b_ada, m_norm_g, m_w_in, m_conv_w, m_conv_b, m_conv_ln_g, m_conv_ln_b, m_sg_ln_g, m_sg_ln_b, m_w_s, m_b_s, m_w_out, m_final_g, v_w_ada, v_b_ada, v_norm_g, v_w_in, v_conv_w, v_conv_b, v_conv_ln_g, v_conv_ln_b, v_sg_ln_g, v_sg_ln_b, v_w_s, v_b_s, v_w_out, v_final_g):
    s_len = x.shape[1]
    x2d = x[0]
    tgt = loss_target[0]
    row = lambda a: a.reshape(1, -1)

    kidx = (2 * lax.axis_index("x") + lax.axis_index("y")).astype(jnp.int32).reshape(1)
    cw_sh = jnp.pad(conv_w.reshape(CONV_WIDTH, 256), ((0, 1), (0, 0)))
    h, z6, mod, cact, cw4, w_in12 = _front(kidx, x2d, c, w_ada[0], b_ada, norm_g, w_in[0], cw_sh)
    w12 = w_in12.reshape(12, D_MODEL, 512)
    q, w_out4 = _conv_fwd(z6, cw4, conv_b, w_out[0])
    w_out_full = w_out4.reshape(2 * D_MODEL, D_MODEL)
    ln_rows = (conv_ln_g, conv_ln_b, sg_ln_g, sg_ln_b, row(final_g))
    bs_exp = jnp.repeat(b_s[0].T, CHUNK, axis=1)
    dz6, dq, dx2, ycat, dy, dws, sm1, dbs = _middle(x2d, z6, q, tgt, ln_rows, mod, w_s[0], bs_exp, w_out_full)
    dz6, dcw, pk1, dws_r, dbs_r, loss_t, g_w_out4 = _conv_bwd(
        kidx, dq, z6, cw4, dz6, sm1, dws.reshape(D_MODEL, CHUNK), dbs, ycat, dy)
    grad_x, sm2 = _bwd_in(dz6, w12, x2d, dx2, mod, norm_g)
    g_w_in4, pk2, dcw_sh, dcb = _grad_w_in(kidx, h, dz6, sm2, dcw)

    g_w_in, d_w_in, nm_w_in, nv_w_in = _adamw_blocked(w_in[0], m_w_in[0], v_w_in[0], g_w_in4, "adamw_w_in")
    g_w_out, d_w_out, nm_w_out, nv_w_out = _adamw_blocked(w_out[0], m_w_out[0], v_w_out[0], g_w_out4, "adamw_w_out")
    g_w_ada, d_w_ada, nm_w_ada, nv_w_ada, g_b_ada = _adamw_w_ada(cact, pk1, pk2, w_ada[0], m_w_ada[0], v_w_ada[0])

    g_norm_g = pk2[16:17]
    g_cln_g, g_cln_b, g_sln_g, g_sln_b, g_final = (pk1[8 + i:9 + i] for i in range(5))
    loss = loss_t[0, 0]
    g_conv_w = dcw_sh[:CONV_WIDTH]
    g_w_s = dws_r
    g_b_s = dbs_r[:, :HEADS].T
    small = [
        (b_ada, g_b_ada, m_b_ada, v_b_ada),
        (norm_g, g_norm_g, m_norm_g, v_norm_g),
        (conv_w.reshape(CONV_WIDTH, 256), g_conv_w, m_conv_w.reshape(CONV_WIDTH, 256), v_conv_w.reshape(CONV_WIDTH, 256)),
        (conv_b, dcb, m_conv_b, v_conv_b),
        (conv_ln_g, g_cln_g, m_conv_ln_g, v_conv_ln_g),
        (conv_ln_b, g_cln_b, m_conv_ln_b, v_conv_ln_b),
        (sg_ln_g, g_sln_g, m_sg_ln_g, v_sg_ln_g),
        (sg_ln_b, g_sln_b, m_sg_ln_b, v_sg_ln_b),
        (w_s.reshape(D_MODEL, CHUNK), g_w_s, m_w_s.reshape(D_MODEL, CHUNK), v_w_s.reshape(D_MODEL, CHUNK)),
        (b_s[0], g_b_s, m_b_s[0], v_b_s[0]),
        (row(final_g), g_final, row(m_final_g), row(v_final_g)),
    ]
    upd = _adamw_small(small)

    shapes = [w_ada.shape, b_ada.shape, norm_g.shape, w_in.shape, conv_w.shape, conv_b.shape, conv_ln_g.shape,
              conv_ln_b.shape, sg_ln_g.shape, sg_ln_b.shape, w_s.shape, b_s.shape, w_out.shape, final_g.shape]
    grads = [g_w_ada, g_b_ada, g_norm_g, g_w_in, g_conv_w, dcb, g_cln_g, g_cln_b, g_sln_g, g_sln_b, g_w_s, g_b_s,
             g_w_out, g_final]
    big = {0: (d_w_ada, nm_w_ada, nv_w_ada), 3: (d_w_in, nm_w_in, nv_w_in), 12: (d_w_out, nm_w_out, nv_w_out)}
    small_pos = [1, 2, 4, 5, 6, 7, 8, 9, 10, 11, 13]
    trip = [None] * 14
    for i, t in big.items():
        trip[i] = t
    for i, t in zip(small_pos, upd):
        trip[i] = t
    fit = lambda arrs: [a.reshape(s) for a, s in zip(arrs, shapes)]
    return (loss, grad_x.reshape(x.shape), *fit(grads), *fit([t[0] for t in trip]), *fit([t[1] for t in trip]),
            *fit([t[2] for t in trip]))
```

```python
import jax
import jax.numpy as jnp
from jax import lax
from jax.experimental import pallas as pl
from jax.experimental.pallas import tpu as pltpu

F32 = jnp.float32
BF16 = jnp.bfloat16
MESH = pl.DeviceIdType.MESH

D_MODEL = 1024
N_CHIPS = 4
HEADS = 8
CHUNK = 128
CONV_WIDTH = 31
CONV_HALF = CONV_WIDTH // 2
CONV_PAD = 16
EPS = 1e-6
ADAM_LR = 0.001
ADAM_B1 = 0.9
ADAM_B2 = 0.999
ADAM_EPS = 1e-08
ADAM_WD = 0.01
ADAM_STEP = 10

V7X_VMEM_BYTES = 64 * 1024 * 1024
VMEM_LIMIT = V7X_VMEM_BYTES - 8 * 1024 * 1024
ROWS = 16
UNROLL = 8
TOKEN_TILE = 256
FULL_UNROLL = TOKEN_TILE // ROWS
TIME_TILE = 128
K_TILE = 2048

N_GROUPS = 6


def _natural_group(j):
    return (j + 2) % N_GROUPS


def _pos():
    return lax.axis_index("x"), lax.axis_index("y"), lax.axis_index("c")


def _rcopy(src, dst, ssem, rsem, dev):
    return pltpu.make_async_remote_copy(src_ref=src, dst_ref=dst, send_sem=ssem, recv_sem=rsem,
                                        device_id=dev, device_id_type=MESH)


def _vmem():
    return pl.BlockSpec(memory_space=pltpu.VMEM)


def _params(**kw):
    return pltpu.CompilerParams(vmem_limit_bytes=VMEM_LIMIT, **kw)


def _sigmoid(v):
    return 0.5 * jnp.tanh(0.5 * v) + 0.5


def _row_loop(n_rows, body, unroll=1):
    def step(r, carry):
        body(pl.ds(pl.multiple_of(r * ROWS, ROWS), ROWS))
        return carry
    lax.fori_loop(0, n_rows // ROWS, step, 0, unroll=unroll)


def _colsum8(v):
    return v.reshape(v.shape[0] // 8, 8, v.shape[1]).sum(axis=0)


def _mean(v):
    return jnp.mean(v, axis=-1, keepdims=True)


def _dot_nn(a, b):
    return jnp.dot(a, b, preferred_element_type=F32)


def _dot_nt(a, b):
    return lax.dot_general(a, b, (((1,), (1,)), ((), ())), preferred_element_type=F32)


def _dot_tn(a, b):
    return lax.dot_general(a, b, (((0,), (0,)), ((), ())), preferred_element_type=F32)


def _remote_chip(k, r):
    return jnp.bitwise_xor(k, r + 1)


def _front(kidx, x, c, w_ada, b_ada, norm_g, w_in, cw):
    s_len = x.shape[0]
    tmh = min(512, s_len)
    tmz = min(1024, s_len)
    nh = s_len // tmh
    nz = s_len // tmz
    n_steps = nh + 12 * nz

    def remote_block(q):
        return jnp.where(q < 6, q % 2, 2), jnp.where(q < 6, q // 2, q - 6)

    def block_of(i, k):
        r, j = remote_block(jnp.maximum(i - 3, 0))
        return jnp.where(i < 3, 3 * k + i, 3 * _remote_chip(k, r) + j)

    def body(k_ref, x_ref, c_ref, bada_ref, g_ref, cw_ref, wada_hbm, win_hbm,
             h_ref, z_ref, mod_ref, cact_ref, cw4_ref, w12_hbm,
             h_all, wbuf, stage, wada_v, cslab, mslab, cw4_s,
             lsem, csend, crecv, msend, mrecv, wsend, wrecv, isend, irecv, fsend, frecv, osem):
        del k_ref
        t = pl.program_id(0)
        x_, y_, c_ = _pos()
        k = 2 * x_ + y_
        b = 4 * x_ + 2 * y_ + c_
        sib = (x_, y_, 1 - c_)

        def dev_of(r):
            kk = _remote_chip(k, r)
            return (kk // 2, kk % 2, c_)

        def ici(q, kk):
            r, j = remote_block(q)
            return _rcopy(wbuf.at[3 * kk + j, c_], wbuf.at[3 * kk + j, c_], isend.at[q], irecv.at[q], dev_of(r))

        def fwd(q, hf):
            r, j = remote_block(q)
            blk = 3 * _remote_chip(k, r) + j
            return _rcopy(wbuf.at[blk, hf], wbuf.at[blk, hf], fsend.at[q], frecv.at[q], sib)

        def c_copy(q, src):
            d = jnp.bitwise_xor(b, q)
            return _rcopy(cslab.at[src], cslab.at[src], csend.at[q - 1], crecv.at[q - 1], (d // 4, (d // 2) % 2, d % 2))

        def m_copy(r, kk):
            return _rcopy(mslab.at[kk], mslab.at[kk], msend.at[r], mrecv.at[r], dev_of(r))

        def cw_copy(r, kk):
            return _rcopy(cw4_s.at[kk], cw4_s.at[kk], wsend.at[r], wrecv.at[r], dev_of(r))

        def to_hbm(i):
            m = block_of(i, k)
            return pltpu.make_async_copy(wbuf.at[m], w12_hbm.at[m], osem.at[i])

        @pl.when(t == 0)
        def _():
            ld_w = pltpu.make_async_copy(win_hbm, stage, lsem.at[0])
            ld_w.start()
            ld_a = pltpu.make_async_copy(wada_hbm, wada_v, lsem.at[1])
            ld_a.start()
            cslab[b] = jnp.broadcast_to(c_ref[...], (8, D_MODEL))
            for q in range(1, 8):
                c_copy(q, b).start()
            cw4_s[k] = cw_ref[...]
            for r in range(3):
                cw_copy(r, k).start()
            ld_w.wait()
            for j in range(3):
                for hf in range(2):
                    wbuf[3 * k + j, hf] = stage[hf * 512:(hf + 1) * 512, j * 512:(j + 1) * 512].astype(BF16)
            for i in range(3):
                to_hbm(i).start()
            for q in range(1, 8):
                c_copy(q, jnp.bitwise_xor(b, q)).wait_recv()
            row = lax.broadcasted_iota(jnp.int32, (8, D_MODEL), 0)
            call = jnp.zeros((8, D_MODEL), F32)
            for d in range(8):
                call = jnp.where(row == d, cslab[d], call)
            cact = call * _sigmoid(call)
            cact_ref[...] = cact
            ld_a.wait()
            mslab[k] = _dot_nn(cact.astype(BF16), wada_v[...].astype(BF16))
            for r in range(3):
                m_copy(r, k).start()
            for q in range(9):
                ici(q, k).start()
            for r in range(3):
                m_copy(r, _remote_chip(k, r)).wait_recv()
            row8 = lax.broadcasted_iota(jnp.int32, (8, 768), 0)
            for kk in range(N_CHIPS):
                piece = jnp.sum(jnp.where(row8 == b, mslab[kk], 0.0), axis=0, keepdims=True)
                mod_ref[:, kk * 768:(kk + 1) * 768] = piece + bada_ref[:, kk * 768:(kk + 1) * 768]
            for r in range(3):
                cw_copy(r, _remote_chip(k, r)).wait_recv()
            cw4_ref[...] = cw4_s[...]

        @pl.when(t < nh)
        def _():
            shift = mod_ref[:, 0:D_MODEL]
            scale1 = 1.0 + mod_ref[:, D_MODEL:2 * D_MODEL]
            g = g_ref[...]
            base = t * tmh

            def rows_fn(rows):
                xt = x_ref[rows, :]
                r = lax.rsqrt(_mean(xt * xt) + EPS)
                hv = ((xt * r * g) * scale1 + shift).astype(BF16)
                h_ref[rows, :] = hv
                h_all[pl.ds(pl.multiple_of(base + rows.start, ROWS), ROWS), :] = hv
            _row_loop(tmh, rows_fn, unroll=UNROLL)

        @pl.when(t >= nh)
        def _():
            u = t - nh
            i = u // nz
            rt = u % nz
            @pl.when(jnp.logical_and(rt == 0, jnp.logical_and(i >= 2, i <= 10)))
            def _():
                q = i - 2
                r, _ = remote_block(q)
                ici(q, _remote_chip(k, r)).wait_recv()
                fwd(q, c_).start()

            @pl.when(jnp.logical_and(rt == 0, i >= 3))
            def _():
                fwd(i - 3, 1 - c_).wait_recv()
                to_hbm(i).start()
            m = block_of(i, k)
            hb = h_all[pl.ds(pl.multiple_of(rt * tmz, tmz), tmz), :]
            z_ref[0] = _dot_nn(hb, wbuf[m].reshape(D_MODEL, 512))

        @pl.when(t == n_steps - 1)
        def _():
            for q in range(1, 8):
                c_copy(q, b).wait_send()
            for r in range(3):
                m_copy(r, k).wait_send()
                cw_copy(r, k).wait_send()
            for q in range(9):
                ici(q, k).wait_send()
                fwd(q, c_).wait_send()
            for i in range(12):
                to_hbm(i).wait()

    def z_index(t, k_ref):
        u = jnp.maximum(t - nh, 0)
        m = block_of(u // nz, k_ref[0])
        return ((m // 2 + 4) % N_GROUPS, u % nz, m % 2)

    tok = lambda t, k_ref: (jnp.minimum(t, nh - 1), 0)
    const2 = lambda t, k_ref: (0, 0)
    grid_spec = pltpu.PrefetchScalarGridSpec(
        num_scalar_prefetch=1,
        grid=(n_steps,),
        in_specs=[pl.BlockSpec((tmh, D_MODEL), tok),
                  pl.BlockSpec((1, D_MODEL), const2),
                  pl.BlockSpec((1, 3 * D_MODEL), const2),
                  pl.BlockSpec((1, D_MODEL), const2),
                  pl.BlockSpec((32, 256), const2),
                  pl.BlockSpec(memory_space=pl.ANY),
                  pl.BlockSpec(memory_space=pl.ANY)],
        out_specs=(pl.BlockSpec((tmh, D_MODEL), tok),
                   pl.BlockSpec((1, tmz, 512), z_index),
                   pl.BlockSpec((1, 3 * D_MODEL), const2),
                   pl.BlockSpec((8, D_MODEL), const2),
                   pl.BlockSpec((N_CHIPS, 32, 256), lambda t, k_ref: (0, 0, 0)),
                   pl.BlockSpec(memory_space=pl.ANY)),
        scratch_shapes=[pltpu.VMEM((s_len, D_MODEL), BF16),
                        pltpu.VMEM((12, 2, 512, 512), BF16),
                        pltpu.VMEM((D_MODEL, 1536), F32),
                        pltpu.VMEM((D_MODEL, 768), F32),
                        pltpu.VMEM((8, 8, D_MODEL), F32),
                        pltpu.VMEM((N_CHIPS, 8, 768), F32),
                        pltpu.VMEM((N_CHIPS, 32, 256), F32),
                        pltpu.SemaphoreType.DMA((2,)),
                        pltpu.SemaphoreType.DMA((7,)), pltpu.SemaphoreType.DMA((7,)),
                        pltpu.SemaphoreType.DMA((3,)), pltpu.SemaphoreType.DMA((3,)),
                        pltpu.SemaphoreType.DMA((3,)), pltpu.SemaphoreType.DMA((3,)),
                        pltpu.SemaphoreType.DMA((9,)), pltpu.SemaphoreType.DMA((9,)),
                        pltpu.SemaphoreType.DMA((9,)), pltpu.SemaphoreType.DMA((9,)),
                        pltpu.SemaphoreType.DMA((12,))])
    return pl.pallas_call(
        body, name="front",
        grid_spec=grid_spec,
        out_shape=(jax.ShapeDtypeStruct((s_len, D_MODEL), BF16),
                   jax.ShapeDtypeStruct((N_GROUPS, s_len, D_MODEL), F32),
                   jax.ShapeDtypeStruct((1, 3 * D_MODEL), F32),
                   jax.ShapeDtypeStruct((8, D_MODEL), F32),
                   jax.ShapeDtypeStruct((N_CHIPS, 32, 256), F32),
                   jax.ShapeDtypeStruct((12, 2, 512, 512), BF16)),
        compiler_params=_params(dimension_semantics=("arbitrary",)),
    )(kidx, x, c, b_ada, norm_g, cw, w_ada, w_in)


def _all_reduce_partners():
    x, y, c = _pos()
    return [(x, y, 1 - c), (x, 1 - y, c), (1 - x, y, c)]


def _butterfly_event(e, bufs, recvs, ssem, rsem, partners):
    n = len(bufs)

    def copies(s):
        return [_rcopy(buf, recv.at[s], ssem.at[s * n + i], rsem.at[s * n + i], partners[s])
                for i, (buf, recv) in enumerate(zip(bufs, recvs))]

    if e > 0:
        for cp in copies(e - 1):
            cp.wait()
        for buf, recv in zip(bufs, recvs):
            buf[...] = buf[...] + recv[e - 1]
    if e < len(partners):
        for cp in copies(e):
            cp.start()


def _onehot_rows(v, b):
    row = lax.broadcasted_iota(jnp.int32, (8, v.shape[1]), 0)
    return jnp.where(row == b, jnp.broadcast_to(v, (8, v.shape[1])), 0.0)


def _conv_fwd(z6, cw4, conv_b, w_out):
    s_len = z6.shape[1]
    tt = TIME_TILE
    n_blocks = D_MODEL // 128

    def body(z_ref, cw_ref, cb_ref, wout_hbm, q_ref, wout4_hbm, ppad, stage, wbuf, lsem, isend, irecv, fsend, frecv, osem):
        jb = pl.program_id(0)
        x_, y_, c_ = _pos()
        k = 2 * x_ + y_
        sib = (x_, y_, 1 - c_)

        def ici(r, kk):
            rk = _remote_chip(k, r)
            return _rcopy(wbuf.at[kk, c_], wbuf.at[kk, c_], isend.at[r], irecv.at[r], (rk // 2, rk % 2, c_))

        def fwd(r, hf):
            kk = _remote_chip(k, r)
            return _rcopy(wbuf.at[kk, hf], wbuf.at[kk, hf], fsend.at[r], frecv.at[r], sib)

        @pl.when(jb == 0)
        def _():
            ld = pltpu.make_async_copy(wout_hbm, stage, lsem)
            ld.start()
            ld.wait()
            for hf in range(2):
                wbuf[k, hf] = stage[hf * 256:(hf + 1) * 256, :].astype(BF16)
            for r in range(3):
                ici(r, k).start()

        @pl.when(jb == (5 * n_blocks) // 8)
        def _():
            for r in range(3):
                ici(r, _remote_chip(k, r)).wait_recv()
                fwd(r, c_).start()

        zero = jnp.zeros((CONV_PAD, 128), F32)
        ppad[0:CONV_PAD, :] = zero
        ppad[s_len + CONV_PAD:s_len + 2 * CONV_PAD, :] = zero

        def fill(i, carry):
            t0 = pl.multiple_of(i * tt, tt)
            ppad[pl.ds(CONV_PAD + t0, tt), :] = z_ref[0, pl.ds(t0, tt), :] * _sigmoid(z_ref[1, pl.ds(t0, tt), :])
            return carry
        lax.fori_loop(0, s_len // tt, fill, 0)
        w = cw_ref[0]
        bias = cb_ref[...]

        def conv(i, carry):
            t0 = pl.multiple_of(i * tt, tt)
            acc = jnp.broadcast_to(bias, (tt, 128))
            for o in range(1, CONV_WIDTH + 1):
                acc = acc + w[o - 1:o, :] * ppad[pl.ds(t0 + o, tt), :]
            q_ref[pl.ds(t0, tt), :] = acc
            return carry
        lax.fori_loop(0, s_len // tt, conv, 0)

        @pl.when(jb == n_blocks - 1)
        def _():
            for r in range(3):
                fwd(r, 1 - c_).wait_recv()
            out = pltpu.make_async_copy(wbuf, wout4_hbm, osem)
            out.start()
            for r in range(3):
                ici(r, k).wait_send()
                fwd(r, c_).wait_send()
            out.wait()

    return pl.pallas_call(
        body, name="conv_fwd",
        grid=(n_blocks,),
        out_shape=(jax.ShapeDtypeStruct((s_len, D_MODEL), F32), jax.ShapeDtypeStruct((N_CHIPS, 2, 256, D_MODEL), BF16)),
        in_specs=[pl.BlockSpec((2, s_len, 128), lambda j: (2, 0, j)),
                  pl.BlockSpec((1, 32, 128), lambda j: (j // 2, 0, j % 2)),
                  pl.BlockSpec((1, 128), lambda j: (0, j)),
                  pl.BlockSpec(memory_space=pl.ANY)],
        out_specs=(pl.BlockSpec((s_len, 128), lambda j: (0, j)), pl.BlockSpec(memory_space=pl.ANY)),
        scratch_shapes=[pltpu.VMEM((s_len + 2 * CONV_PAD, 128), F32),
                        pltpu.VMEM((512, D_MODEL), F32),
                        pltpu.VMEM((N_CHIPS, 2, 256, D_MODEL), BF16),
                        pltpu.SemaphoreType.DMA,
                        pltpu.SemaphoreType.DMA((3,)), pltpu.SemaphoreType.DMA((3,)),
                        pltpu.SemaphoreType.DMA((3,)), pltpu.SemaphoreType.DMA((3,)),
                        pltpu.SemaphoreType.DMA],
        compiler_params=_params(dimension_semantics=("arbitrary",)),
    )(z6, cw4, conv_b, w_out)


def _middle(x, z6, q, target, ln_rows, mod, w_s, bs_exp, w_out):
    s_len = x.shape[0]
    tm = TOKEN_TILE
    n_steps = s_len // tm
    n_chunks = tm // CHUNK
    inv_d = 1.0 / D_MODEL

    def body(x_ref, z_ref, q_ref, tgt_ref, cg_ref, cb_ref, sg_ref, sb_ref, fg_ref, mod_ref, ws_ref, bs_ref, wout_ref,
             dz_ref, dq_ref, dx2_ref, ycat_ref, dy_ref, dws_ref, sm_ref, dbs_ref,
             vl_scr, vm_scr, y_scr, dycat_scr, dvm_scr, dvl_scr, acc_scr, dbs_acc, keep, rstd_scr):
        i = pl.program_id(0)

        @pl.when(i == 0)
        def _():
            acc_scr[...] = jnp.zeros_like(acc_scr)
            dbs_acc[...] = jnp.zeros_like(dbs_acc)
            dws_ref[...] = jnp.zeros_like(dws_ref)

        cg, cb, sg, sb, fg = cg_ref[...], cb_ref[...], sg_ref[...], sb_ref[...], fg_ref[...]
        gm = mod_ref[:, 2 * D_MODEL:3 * D_MODEL]

        def norm_stats(t):
            c = t - _mean(t)
            rstd = lax.rsqrt(_mean(c * c) + EPS)
            return c * rstd, rstd

        def phase1(rows):
            qhat, rstd_q = norm_stats(q_ref[rows, :])
            ln = qhat * cg + cb
            gz = z_ref[0, rows, :]
            sig_ln = _sigmoid(ln)
            sig_g = _sigmoid(gz)
            ycat_ref[rows, 0:D_MODEL] = ((ln * sig_ln) * (gz * sig_g)).astype(BF16)
            vhat, rstd_v = norm_stats(z_ref[2, rows, :])
            vl_scr[rows, :] = (vhat * sg + sb).astype(BF16)
            keep[0, rows, :] = qhat
            keep[1, rows, :] = vhat
            keep[2, rows, :] = sig_ln
            keep[3, rows, :] = sig_g
            rstd_scr[0, rows, :] = rstd_q
            rstd_scr[1, rows, :] = rstd_v
        _row_loop(tm, phase1, unroll=FULL_UNROLL)

        for ch in range(n_chunks):
            r0 = ch * CHUNK
            for h in range(HEADS):
                c0 = h * CHUNK
                vm_scr[r0:r0 + CHUNK, c0:c0 + CHUNK] = (
                    _dot_nn(ws_ref[h].astype(BF16), vl_scr[r0:r0 + CHUNK, c0:c0 + CHUNK]) + bs_ref[:, c0:c0 + CHUNK])

        def phase3(rows):
            bg = z_ref[3, rows, :]
            sig_b = _sigmoid(bg)
            keep[4, rows, :] = sig_b
            ycat_ref[rows, D_MODEL:2 * D_MODEL] = (z_ref[1, rows, :] * vm_scr[rows, :] * (bg * sig_b)).astype(BF16)
        _row_loop(tm, phase3, unroll=FULL_UNROLL)

        y_scr[...] = _dot_nn(ycat_ref[...], wout_ref[...])

        def phase5(rows):
            y = y_scr[rows, :]
            x2 = x_ref[rows, :] + gm * y
            r2 = lax.rsqrt(_mean(x2 * x2) + EPS)
            xn2 = x2 * r2
            diff = xn2 * fg - tgt_ref[rows, :]
            acc_scr[6] += _colsum8(diff * diff)
            dout = diff * inv_d
            acc_scr[0] += _colsum8(dout * xn2)
            dxn = dout * fg
            dx2 = r2 * (dxn - xn2 * _mean(dxn * xn2))
            dx2_ref[rows, :] = dx2
            acc_scr[1] += _colsum8(dx2 * y)
            dy_ref[rows, :] = (dx2 * gm).astype(BF16)
        _row_loop(tm, phase5, unroll=FULL_UNROLL)

        dycat_scr[...] = _dot_nt(dy_ref[...], wout_ref[...])

        def phase7(rows):
            dyb = dycat_scr[rows, D_MODEL:2 * D_MODEL]
            u = z_ref[1, rows, :]
            bg = z_ref[3, rows, :]
            vm = vm_scr[rows, :]
            sig = keep[4, rows, :]
            silu = bg * sig
            dz_ref[1, rows, :] = (dyb * vm * silu).astype(BF16)
            dvm = dyb * u * silu
            dz_ref[3, rows, :] = (dyb * u * vm * (sig * (1.0 + bg * (1.0 - sig)))).astype(BF16)
            dvm_scr[rows, :] = dvm.astype(BF16)
            pos = pl.ds(pl.multiple_of(rows.start % CHUNK, ROWS), ROWS)
            dbs_acc[pos, :] += dvm
        _row_loop(tm, phase7, unroll=FULL_UNROLL)

        for ch in range(n_chunks):
            r0 = ch * CHUNK
            for h in range(HEADS):
                c0 = h * CHUNK
                dvm_b = dvm_scr[r0:r0 + CHUNK, c0:c0 + CHUNK]
                dws_ref[h] += _dot_nt(dvm_b, vl_scr[r0:r0 + CHUNK, c0:c0 + CHUNK])
                dvl_scr[r0:r0 + CHUNK, c0:c0 + CHUNK] = _dot_tn(ws_ref[h].astype(BF16), dvm_b)

        def phase9(rows):
            vhat, rstd_v = keep[1, rows, :], rstd_scr[1, rows, :]
            dvl = dvl_scr[rows, :]
            acc_scr[4] += _colsum8(dvl * vhat)
            acc_scr[5] += _colsum8(dvl)
            dvh = dvl * sg
            dz_ref[2, rows, :] = (rstd_v * (dvh - _mean(dvh) - vhat * _mean(dvh * vhat))).astype(BF16)
            qhat, rstd_q = keep[0, rows, :], rstd_scr[0, rows, :]
            ln = qhat * cg + cb
            sig_ln = keep[2, rows, :]
            gz = z_ref[0, rows, :]
            sig_g = keep[3, rows, :]
            dya = dycat_scr[rows, 0:D_MODEL]
            dz_ref[0, rows, :] = (dya * (ln * sig_ln) * (sig_g * (1.0 + gz * (1.0 - sig_g)))).astype(BF16)
            dln = (dya * (gz * sig_g)) * (sig_ln * (1.0 + ln * (1.0 - sig_ln)))
            acc_scr[2] += _colsum8(dln * qhat)
            acc_scr[3] += _colsum8(dln)
            dqh = dln * cg
            dq_ref[rows, :] = rstd_q * (dqh - _mean(dqh) - qhat * _mean(dqh * qhat))
        _row_loop(tm, phase9, unroll=FULL_UNROLL)

        @pl.when(i == n_steps - 1)
        def _():
            for qi in range(8):
                scale = 0.5 * inv_d if qi == 6 else 1.0
                sm_ref[qi:qi + 1, :] = jnp.sum(acc_scr[qi], axis=0, keepdims=True) * scale
            lane = lax.broadcasted_iota(jnp.int32, (CHUNK, CHUNK), 1)
            tile = jnp.zeros((CHUNK, CHUNK), F32)
            for h in range(HEADS):
                col = jnp.sum(dbs_acc[:, h * CHUNK:(h + 1) * CHUNK], axis=1, keepdims=True)
                tile = jnp.where(lane == h, col, tile)
            dbs_ref[...] = tile

    tok = lambda i: (i, 0)
    const2 = lambda i: (0, 0)
    return pl.pallas_call(
        body, name="middle",
        grid=(n_steps,),
        out_shape=(jax.ShapeDtypeStruct((N_GROUPS, s_len, D_MODEL), BF16),
                   jax.ShapeDtypeStruct((s_len, D_MODEL), F32),
                   jax.ShapeDtypeStruct((s_len, D_MODEL), F32),
                   jax.ShapeDtypeStruct((s_len, 2 * D_MODEL), BF16),
                   jax.ShapeDtypeStruct((s_len, D_MODEL), BF16),
                   jax.ShapeDtypeStruct((HEADS, CHUNK, CHUNK), F32),
                   jax.ShapeDtypeStruct((8, D_MODEL), F32),
                   jax.ShapeDtypeStruct((CHUNK, CHUNK), F32)),
        in_specs=[pl.BlockSpec((tm, D_MODEL), tok),
                  pl.BlockSpec((4, tm, D_MODEL), lambda i: (0, i, 0)),
                  pl.BlockSpec((tm, D_MODEL), tok),
                  pl.BlockSpec((tm, D_MODEL), tok),
                  *[pl.BlockSpec((1, D_MODEL), const2) for _ in range(5)],
                  pl.BlockSpec((1, 3 * D_MODEL), const2),
                  pl.BlockSpec((HEADS, CHUNK, CHUNK), lambda i: (0, 0, 0)),
                  pl.BlockSpec((CHUNK, D_MODEL), const2),
                  pl.BlockSpec((2 * D_MODEL, D_MODEL), const2, pipeline_mode=pl.Buffered(1))],
        out_specs=(pl.BlockSpec((4, tm, D_MODEL), lambda i: (0, i, 0)),
                   pl.BlockSpec((tm, D_MODEL), tok),
                   pl.BlockSpec((tm, D_MODEL), tok),
                   pl.BlockSpec((tm, 2 * D_MODEL), tok),
                   pl.BlockSpec((tm, D_MODEL), tok),
                   pl.BlockSpec((HEADS, CHUNK, CHUNK), lambda i: (0, 0, 0)),
                   pl.BlockSpec((8, D_MODEL), const2),
                   pl.BlockSpec((CHUNK, CHUNK), const2)),
        scratch_shapes=[pltpu.VMEM((tm, D_MODEL), BF16),
                        pltpu.VMEM((tm, D_MODEL), F32),
                        pltpu.VMEM((tm, D_MODEL), F32),
                        pltpu.VMEM((tm, 2 * D_MODEL), F32),
                        pltpu.VMEM((tm, D_MODEL), BF16),
                        pltpu.VMEM((tm, D_MODEL), F32),
                        pltpu.VMEM((8, 8, D_MODEL), F32),
                        pltpu.VMEM((CHUNK, D_MODEL), F32),
                        pltpu.VMEM((5, tm, D_MODEL), F32),
                        pltpu.VMEM((2, tm, 1), F32)],
        compiler_params=_params(dimension_semantics=("arbitrary",)),
    )(x, z6, q, target, *ln_rows, mod, w_s, bs_exp, w_out)


def _conv_bwd(kidx, dq, z6, cw4, dz6, sm1, dws, dbs, ycat, dy):
    s_len = dq.shape[0]
    tt = TIME_TILE
    n_blocks = D_MODEL // 128
    shp = [(16, D_MODEL), (D_MODEL, 128), (128, 128)]
    nj, nr, nc = 1, 256, D_MODEL
    kt = n_blocks // N_CHIPS
    tk = s_len // kt

    def body(k_ref, dq_ref, z_ref, cw_ref, dz_in, sm1_ref, dws_in, dbs_in, ycat_ref, dy_ref,
             dz_ref, pk1_ref, dws_ref, dbs_ref, loss_ref, gout_ref,
             dqpad, b_pk, b_dws, b_dbs, r_pk, r_dws, r_dbs, ssem, rsem, *rs_scr):
        del k_ref, dz_in
        jb = pl.program_id(0)
        x_, y_, c_ = _pos()

        @pl.when(jb == 0)
        def _():
            b_pk[0:8, :] = _onehot_rows(sm1_ref[1:2, :], 4 * x_ + 2 * y_ + c_)
            for r, src in enumerate([2, 3, 4, 5, 0, 6]):
                b_pk[8 + r:9 + r, :] = sm1_ref[src:src + 1, :]
            b_pk[14:16, :] = jnp.zeros((2, D_MODEL), F32)
            b_dws[...] = dws_in[...]
            b_dbs[...] = dbs_in[...]

        for e in range(4):
            @pl.when(jb == (e * n_blocks) // 4)
            def _():
                _butterfly_event(e, [b_pk, b_dws, b_dbs], [r_pk, r_dws, r_dbs], ssem, rsem, _all_reduce_partners())

        @pl.when(jb == (3 * n_blocks) // 4)
        def _():
            pk1_ref[...] = b_pk[...]
            dws_ref[...] = b_dws[...]
            dbs_ref[...] = b_dbs[...]
            loss_ref[...] = jnp.broadcast_to(jnp.sum(b_pk[13:14, :], axis=1, keepdims=True), (8, 128))

        zero = jnp.zeros((CONV_PAD, 128), F32)
        dqpad[0:CONV_PAD, :] = zero
        dqpad[s_len + CONV_PAD:s_len + 2 * CONV_PAD, :] = zero
        dqpad[CONV_PAD:s_len + CONV_PAD, :] = dq_ref[...]
        w = cw_ref[0]

        def bwd(i, carry):
            t0 = i * tt
            dp = jnp.zeros((tt, 128), F32)
            for o in range(1, CONV_WIDTH + 1):
                dp = dp + w[CONV_WIDTH - o:CONV_WIDTH - o + 1, :] * dqpad[pl.ds(t0 + o, tt), :]
            a = z_ref[0, pl.ds(t0, tt), :]
            sig = _sigmoid(z_ref[1, pl.ds(t0, tt), :])
            dz_ref[0, pl.ds(t0, tt), :] = (dp * sig).astype(BF16)
            dz_ref[1, pl.ds(t0, tt), :] = (dp * a * (sig * (1.0 - sig))).astype(BF16)
            return carry

        n_iter = s_len // tt
        n_groups = min(8, n_iter)
        rows = tk // n_groups

        def conv_and_matmul(add):
            for g in range(n_groups):
                for it in range(g * n_iter // n_groups, (g + 1) * n_iter // n_groups):
                    bwd(it, 0)
                add(_dot_tn(ycat_ref[g * rows:(g + 1) * rows, :], dy_ref[g * rows:(g + 1) * rows, :]))
        _rs_step(jb // kt, jb % kt, kt, nj, nr, nc, conv_and_matmul, gout_ref, rs_scr)

    const2 = lambda j, k_ref: (0, 0)
    grid_spec = pltpu.PrefetchScalarGridSpec(
        num_scalar_prefetch=1,
        grid=(n_blocks,),
        in_specs=[pl.BlockSpec((s_len, 128), lambda j, k_ref: (0, j)),
                  pl.BlockSpec((2, s_len, 128), lambda j, k_ref: (2, 0, j)),
                  pl.BlockSpec((1, 32, 128), lambda j, k_ref: (j // 2, 0, j % 2)),
                  pl.BlockSpec(memory_space=pl.ANY),
                  pl.BlockSpec((8, D_MODEL), const2),
                  pl.BlockSpec((D_MODEL, 128), const2),
                  pl.BlockSpec((128, 128), const2),
                  pl.BlockSpec((tk, 512), lambda j, k_ref: (j % kt, _rs_block(j // kt, k_ref[0], nj))),
                  pl.BlockSpec((tk, D_MODEL), lambda j, k_ref: (j % kt, 0))],
        out_specs=(pl.BlockSpec((2, s_len, 128), lambda j, k_ref: (2, 0, j)),)
        + tuple(pl.BlockSpec(s, const2) for s in shp) + (pl.BlockSpec((8, 128), const2),
                                                         pl.BlockSpec(memory_space=pl.ANY)),
        scratch_shapes=[pltpu.VMEM((s_len + 2 * CONV_PAD, 128), F32)]
        + [pltpu.VMEM(s, F32) for s in shp] + [pltpu.VMEM((3,) + s, F32) for s in shp]
        + [pltpu.SemaphoreType.DMA((9,)), pltpu.SemaphoreType.DMA((9,))] + _rs_scratch(nj, nr, nc))
    return pl.pallas_call(
        body, name="conv_bwd",
        grid_spec=grid_spec,
        out_shape=(jax.ShapeDtypeStruct(dz6.shape, BF16),)
        + tuple(jax.ShapeDtypeStruct(s, F32) for s in shp) + (jax.ShapeDtypeStruct((8, 128), F32),
                                                              jax.ShapeDtypeStruct((nj, 2, nr, nc), F32)),
        input_output_aliases={4: 0},
        compiler_params=_params(dimension_semantics=("arbitrary",)),
    )(kidx, dq, z6, cw4, dz6, sm1, dws, dbs, ycat, dy)


def _bwd_in(dz6, w12, x, dx2, mod, norm_g, dq, z6):
    s_len = x.shape[0]
    tm = TOKEN_TILE
    n_steps = s_len // tm
    tt = TIME_TILE
    n_cblocks = D_MODEL // 128
    parts = max(1, n_steps // n_cblocks)
    nsub = max(1, n_cblocks // n_steps)
    cw = 128 * nsub
    tiles = (s_len // tt) // parts

    def body(dz_ref, w_ref, x_ref, dx2_ref, mod_ref, g_ref, dq_ref, z_ref, gx_ref, sm_ref, dcw_ref,
             dh_scr, acc_scr, ppad, wacc):
        i = pl.program_id(0)
        part = i % parts

        @pl.when(i == 0)
        def _():
            acc_scr[...] = jnp.zeros_like(acc_scr)

        @pl.when(part == 0)
        def _():
            zero = jnp.zeros((CONV_PAD, cw), F32)
            ppad[0:CONV_PAD, :] = zero
            ppad[s_len + CONV_PAD:s_len + 2 * CONV_PAD, :] = zero
            wacc[...] = jnp.zeros_like(wacc)

            def fill(t, carry):
                t0 = pl.multiple_of(t * tt, tt)
                ppad[pl.ds(CONV_PAD + t0, tt), :] = z_ref[0, pl.ds(t0, tt), :] * _sigmoid(z_ref[1, pl.ds(t0, tt), :])
                return carry
            lax.fori_loop(0, s_len // tt, fill, 0)

        def dw_tile(tile):
            t0 = pl.multiple_of((part * tiles + tile) * tt, tt)
            for sub in range(nsub):
                lanes = slice(sub * 128, (sub + 1) * 128)
                dqt = dq_ref[pl.ds(t0, tt), lanes]
                for o in range(1, CONV_WIDTH + 1):
                    wacc[sub, o - 1] += _colsum8(dqt * ppad[pl.ds(t0 + o, tt), lanes])
                wacc[sub, CONV_WIDTH] += _colsum8(dqt)

        dots = [(j, hf) for j in range(N_GROUPS) for hf in range(2)]
        dh = jnp.zeros((tm, D_MODEL), F32)
        for d, (j, hf) in enumerate(dots):
            dh = dh + _dot_nt(dz_ref[j, :, hf * 512:(hf + 1) * 512], w_ref[2 * _natural_group(j) + hf])
            for tile in range(d * tiles // len(dots), (d + 1) * tiles // len(dots)):
                dw_tile(tile)
        dh_scr[...] = dh

        @pl.when(part == parts - 1)
        def _():
            for sub in range(nsub):
                for k in range(32):
                    dcw_ref[k:k + 1, sub * 128:(sub + 1) * 128] = jnp.sum(wacc[sub, k], axis=0, keepdims=True)

        scale1 = 1.0 + mod_ref[:, D_MODEL:2 * D_MODEL]
        g = g_ref[...]

        def rows_fn(rows):
            xt = x_ref[rows, :]
            r = lax.rsqrt(_mean(xt * xt) + EPS)
            xn = xt * r
            dhr = dh_scr[rows, :]
            acc_scr[0] += _colsum8(dhr)
            acc_scr[1] += _colsum8(dhr * (xn * g))
            acc_scr[2] += _colsum8(dhr * scale1 * xn)
            dxn = dhr * (g * scale1)
            gx_ref[rows, :] = dx2_ref[rows, :] + r * (dxn - xn * _mean(dxn * xn))
        _row_loop(tm, rows_fn, unroll=FULL_UNROLL)

        @pl.when(i == n_steps - 1)
        def _():
            for qi in range(8):
                sm_ref[qi:qi + 1, :] = jnp.sum(acc_scr[qi], axis=0, keepdims=True)

    tok = lambda i: (i, 0)
    const2 = lambda i: (0, 0)
    return pl.pallas_call(
        body, name="bwd_in",
        grid=(n_steps,),
        out_shape=(jax.ShapeDtypeStruct((s_len, D_MODEL), F32), jax.ShapeDtypeStruct((8, D_MODEL), F32),
                   jax.ShapeDtypeStruct((32, D_MODEL), F32)),
        in_specs=[pl.BlockSpec((N_GROUPS, tm, D_MODEL), lambda i: (0, i, 0)),
                  pl.BlockSpec((12, D_MODEL, 512), lambda i: (0, 0, 0), pipeline_mode=pl.Buffered(1)),
                  pl.BlockSpec((tm, D_MODEL), tok),
                  pl.BlockSpec((tm, D_MODEL), tok),
                  pl.BlockSpec((1, 3 * D_MODEL), const2),
                  pl.BlockSpec((1, D_MODEL), const2),
                  pl.BlockSpec((s_len, cw), lambda i: (0, i // parts)),
                  pl.BlockSpec((2, s_len, cw), lambda i: (2, 0, i // parts))],
        out_specs=(pl.BlockSpec((tm, D_MODEL), tok), pl.BlockSpec((8, D_MODEL), const2),
                   pl.BlockSpec((32, cw), lambda i: (0, i // parts))),
        scratch_shapes=[pltpu.VMEM((tm, D_MODEL), F32), pltpu.VMEM((8, 8, D_MODEL), F32),
                        pltpu.VMEM((s_len + 2 * CONV_PAD, cw), F32), pltpu.VMEM((nsub, 32, 8, 128), F32)],
        compiler_params=_params(dimension_semantics=("arbitrary",)),
    )(dz6, w12, x, dx2, mod, norm_g, dq, z6)


def _rs_shard_block(i, nj):
    return jnp.where(i < 3 * nj, i % 3, 3), jnp.where(i < 3 * nj, i // 3, i - 3 * nj)


def _rs_block(i, k, nj):
    s, j = _rs_shard_block(i, nj)
    return nj * ((k + 1 + s) % N_CHIPS) + j


def _rs_scratch(nj, nr, nc):
    nblk, nfor = N_CHIPS * nj, 3 * nj
    return [pltpu.VMEM((2, 2, nr, nc), F32),
            pltpu.VMEM((nblk, nr, nc), F32),
            pltpu.VMEM((nfor, nr, nc), BF16),
            pltpu.VMEM((nfor, nr, nc), BF16),
            pltpu.VMEM((nj, nr, nc), F32),
            pltpu.SemaphoreType.DMA,
            pltpu.SemaphoreType.DMA((nblk,)), pltpu.SemaphoreType.DMA((nblk,)),
            pltpu.SemaphoreType.DMA((nfor,)), pltpu.SemaphoreType.DMA((nfor,)),
            pltpu.SemaphoreType.DMA, pltpu.SemaphoreType.DMA]


def _rs_step(i, kk, kt, nj, nr, nc, partial_fn, out_ref, scr):
    nblk, nfor = N_CHIPS * nj, 3 * nj
    acc, recv_a, send_b, recv_b, own_ps, lsem, psend, precv, isend, irecv, fsend, frecv = scr
    x, y, c = _pos()
    k = 2 * x + y
    sib = (x, y, 1 - c)
    slot = i % 2

    @pl.when(kk == 0)
    def _():
        acc[slot] = jnp.zeros((2, nr, nc), F32)

    def pair_copy(ii, sl):
        return _rcopy(acc.at[sl, 1 - c], recv_a.at[ii], psend.at[ii], precv.at[ii], sib)

    def pair_sum(ii, sl):
        cp = pair_copy(ii, sl)
        cp.wait_recv()
        cp.wait_send()
        return acc[sl, c] + recv_a[ii]

    def ici_copy(ii, kd, rslot):
        return _rcopy(send_b.at[ii], recv_b.at[rslot], isend.at[ii], irecv.at[rslot], (kd // 2, kd % 2, c))

    def finalize(ii, sl):
        ps = pair_sum(ii, sl)
        s, j = _rs_shard_block(ii, nj)

        @pl.when(s < 3)
        def _():
            send_b[ii] = ps.astype(BF16)
            ici_copy(ii, (k + 1 + s) % N_CHIPS, (2 - s) * nj + j).start()

        @pl.when(s == 3)
        def _():
            own_ps[j] = ps

    @pl.when(jnp.logical_and(kk == kt - 1, i >= 1))
    def _():
        finalize(i - 1, 1 - slot)

    def add_partial(res):
        acc[slot, 0] += res[0:nr]
        acc[slot, 1] += res[nr:2 * nr]
    partial_fn(add_partial)

    @pl.when(kk == kt - 1)
    def _():
        pair_copy(i, slot).start()

        @pl.when(i == nblk - 1)
        def _():
            own_ps[nj - 1] = pair_sum(nblk - 1, (nblk - 1) % 2)
            for r in range(nfor):
                ici_copy(0, k, r).wait_recv()
            for j in range(nj):
                tot = own_ps[j]
                for s in range(3):
                    tot = tot + recv_b[s * nj + j].astype(F32)
                own_ps[j] = tot
            loc = pltpu.make_async_copy(own_ps, out_ref.at[:, c], lsem)
            loc.start()
            swap = _rcopy(own_ps, out_ref.at[:, c], fsend, frecv, sib)
            swap.start()
            loc.wait()
            swap.wait_send()
            _rcopy(own_ps, out_ref.at[:, 1 - c], fsend, frecv, sib).wait_recv()
            for ii in range(nfor):
                ici_copy(ii, k, 0).wait_send()


def _grad_w_in(kidx, h, dz6, sm2, dcw):
    s_len = h.shape[0]
    tk = min(K_TILE, s_len)
    kt = s_len // tk
    nj, nr, nc = 3, 512, 512
    nblk = N_CHIPS * nj

    def body(k_ref, a_ref, b_ref, sm2_ref, dcw_in, out_ref, pk2_ref, dcwsh_ref, dcb_ref, *scr):
        del k_ref
        b_pk2, b_dcw, r_pk2, r_dcw, bsend, brecv = scr[-6:]
        i = pl.program_id(0)
        kk = pl.program_id(1)
        x, y, c = _pos()
        k = 2 * x + y

        @pl.when(jnp.logical_and(kk == 0, i == 0))
        def _():
            b = 4 * x + 2 * y + c
            b_pk2[0:8, :] = _onehot_rows(sm2_ref[0:1, :], b)
            b_pk2[8:16, :] = _onehot_rows(sm2_ref[1:2, :], b)
            b_pk2[16:17, :] = sm2_ref[2:3, :]
            b_pk2[17:24, :] = jnp.zeros((7, D_MODEL), F32)
            b_dcw[...] = dcw_in[...]

        for e in range(4):
            @pl.when(jnp.logical_and(kk == 0, i == (e * nblk) // 4))
            def _():
                _butterfly_event(e, [b_pk2, b_dcw], [r_pk2, r_dcw], bsend, brecv, _all_reduce_partners())

        @pl.when(jnp.logical_and(kk == 0, i == (3 * nblk) // 4))
        def _():
            pk2_ref[...] = b_pk2[...]
            sel = jnp.zeros((32, 256), F32)
            for kc in range(N_CHIPS):
                sel = jnp.where(k == kc, b_dcw[:, kc * 256:(kc + 1) * 256], sel)
            dcwsh_ref[...] = sel
            dcb_ref[...] = b_dcw[31:32, :]

        _rs_step(i, kk, kt, nj, nr, nc, lambda add: add(_dot_tn(a_ref[...], b_ref[0])), out_ref, scr[:-6])

    def dz_index(i, kk, k_ref):
        m = _rs_block(i, k_ref[0], nj)
        return ((m // 2 + 4) % N_GROUPS, kk, m % 2)

    const2 = lambda i, kk, k_ref: (0, 0)
    small_out = [(24, D_MODEL), (32, 256), (1, D_MODEL)]
    small_buf = [(24, D_MODEL), (32, D_MODEL)]
    grid_spec = pltpu.PrefetchScalarGridSpec(
        num_scalar_prefetch=1,
        grid=(nblk, kt),
        in_specs=[pl.BlockSpec((tk, D_MODEL), lambda i, kk, k_ref: (kk, 0)),
                  pl.BlockSpec((1, tk, 512), dz_index),
                  pl.BlockSpec((8, D_MODEL), const2),
                  pl.BlockSpec((32, D_MODEL), const2)],
        out_specs=(pl.BlockSpec(memory_space=pl.ANY),) + tuple(pl.BlockSpec(s, const2) for s in small_out),
        scratch_shapes=_rs_scratch(nj, nr, nc)
        + [pltpu.VMEM(s, F32) for s in small_buf] + [pltpu.VMEM((3,) + s, F32) for s in small_buf]
        + [pltpu.SemaphoreType.DMA((6,)), pltpu.SemaphoreType.DMA((6,))])
    return pl.pallas_call(
        body, name="grad_w_in",
        grid_spec=grid_spec,
        out_shape=(jax.ShapeDtypeStruct((nj, 2, nr, nc), F32),) + tuple(jax.ShapeDtypeStruct(s, F32) for s in small_out),
        compiler_params=_params(dimension_semantics=("arbitrary", "arbitrary")),
    )(kidx, h, dz6, sm2, dcw)


def _adamw_math(w, g, m, v):
    m = ADAM_B1 * m + (1.0 - ADAM_B1) * g
    v = ADAM_B2 * v + (1.0 - ADAM_B2) * (g * g)
    m_hat = m / (1.0 - ADAM_B1 ** ADAM_STEP)
    v_hat = v / (1.0 - ADAM_B2 ** ADAM_STEP)
    delta = -ADAM_LR * (m_hat / (jnp.sqrt(v_hat) + ADAM_EPS) + ADAM_WD * w)
    return delta, m, v


def _adamw_blocked(w, m, v, g4, name):
    nj, _, nr, nc = g4.shape

    def body(w_ref, m_ref, v_ref, g_ref, go_ref, d_ref, mo_ref, vo_ref):
        g = g_ref[0, 0]
        d, mn, vn = _adamw_math(w_ref[...], g, m_ref[...], v_ref[...])
        go_ref[...] = g
        d_ref[...] = d
        mo_ref[...] = mn
        vo_ref[...] = vn

    blk = pl.BlockSpec((nr, nc), lambda j, hf: (hf, j))
    return pl.pallas_call(
        body, name=name,
        grid=(nj, 2),
        out_shape=tuple(jax.ShapeDtypeStruct(w.shape, F32) for _ in range(4)),
        in_specs=[blk, blk, blk, pl.BlockSpec((1, 1, nr, nc), lambda j, hf: (j, hf, 0, 0))],
        out_specs=(blk, blk, blk, blk),
        compiler_params=_params(dimension_semantics=("arbitrary", "arbitrary")),
    )(w, m, v, g4)


def _adamw_w_ada(cact, pk1, pk2, w, m, v):
    rb = 256

    def body(cact_ref, pk1_ref, pk2_ref, w_ref, m_ref, v_ref, g_ref, d_ref, mo_ref, vo_ref, gb_ref, dmod, sel_scr):
        @pl.when(pl.program_id(0) == 0)
        def _():
            x, y, _ = _pos()
            k = 2 * x + y
            dmod[:, 0:D_MODEL] = pk2_ref[0:8, :]
            dmod[:, D_MODEL:2 * D_MODEL] = pk2_ref[8:16, :]
            dmod[:, 2 * D_MODEL:3 * D_MODEL] = pk1_ref[0:8, :]
            gb_ref[...] = jnp.sum(dmod[...], axis=0, keepdims=True)
            sel = jnp.zeros((8, 768), F32)
            for kk in range(N_CHIPS):
                sel = jnp.where(k == kk, dmod[:, kk * 768:(kk + 1) * 768], sel)
            sel_scr[...] = sel

        g = _dot_tn(cact_ref[...].astype(BF16), sel_scr[...].astype(BF16))
        d, mn, vn = _adamw_math(w_ref[...], g, m_ref[...], v_ref[...])
        g_ref[...] = g
        d_ref[...] = d
        mo_ref[...] = mn
        vo_ref[...] = vn

    blk = pl.BlockSpec((rb, 768), lambda i: (i, 0))
    const2 = lambda i: (0, 0)
    return pl.pallas_call(
        body, name="adamw_w_ada",
        grid=(D_MODEL // rb,),
        out_shape=tuple(jax.ShapeDtypeStruct(w.shape, F32) for _ in range(4)) + (
            jax.ShapeDtypeStruct((1, 3 * D_MODEL), F32),),
        in_specs=[pl.BlockSpec((8, rb), lambda i: (0, i)), pl.BlockSpec((16, D_MODEL), const2),
                  pl.BlockSpec((24, D_MODEL), const2), blk, blk, blk],
        out_specs=(blk, blk, blk, blk, pl.BlockSpec((1, 3 * D_MODEL), const2)),
        scratch_shapes=[pltpu.VMEM((8, 3 * D_MODEL), F32), pltpu.VMEM((8, 768), F32)],
        compiler_params=_params(dimension_semantics=("arbitrary",)),
    )(cact, pk1, pk2, w, m, v)


def _adamw_small(items):
    n = len(items)

    def body(*refs):
        ins, outs = refs[:4 * n], refs[4 * n:]
        for i in range(n):
            w_ref, g_ref, m_ref, v_ref = ins[4 * i:4 * i + 4]
            d, mn, vn = _adamw_math(w_ref[...], g_ref[...], m_ref[...], v_ref[...])
            outs[3 * i][...] = d
            outs[3 * i + 1][...] = mn
            outs[3 * i + 2][...] = vn

    flat = [a for it in items for a in it]
    outs = pl.pallas_call(
        body, name="adamw_small",
        out_shape=tuple(jax.ShapeDtypeStruct(it[0].shape, F32) for it in items for _ in range(3)),
        in_specs=[_vmem()] * (4 * n),
        out_specs=tuple(_vmem() for _ in range(3 * n)),
        compiler_params=_params(),
    )(*flat)
    return [tuple(outs[3 * i:3 * i + 3]) for i in range(n)]


def kernel(x, c, w_ada, b_ada, norm_g, w_in, conv_w, conv_b, conv_ln_g, conv_ln_b, sg_ln_g, sg_ln_b, w_s, b_s, w_out, final_g, loss_target, m_w_ada, m_b_ada, m_norm_g, m_w_in, m_conv_w, m_conv_b, m_conv_ln_g, m_conv_ln_b, m_sg_ln_g, m_sg_ln_b, m_w_s, m_b_s, m_w_out, m_final_g, v_w_ada, v_b_ada, v_norm_g, v_w_in, v_conv_w, v_conv_b, v_conv_ln_g, v_conv_ln_b, v_sg_ln_g, v_sg_ln_b, v_w_s, v_b_s, v_w_out, v_final_g):
    s_len = x.shape[1]
    x2d = x[0]
    tgt = loss_target[0]
    row = lambda a: a.reshape(1, -1)

    kidx = (2 * lax.axis_index("x") + lax.axis_index("y")).astype(jnp.int32).reshape(1)
    cw_sh = jnp.pad(conv_w.reshape(CONV_WIDTH, 256), ((0, 1), (0, 0)))
    h, z6, mod, cact, cw4, w_in12 = _front(kidx, x2d, c, w_ada[0], b_ada, norm_g, w_in[0], cw_sh)
    w12 = w_in12.reshape(12, D_MODEL, 512)
    q, w_out4 = _conv_fwd(z6, cw4, conv_b, w_out[0])
    w_out_full = w_out4.reshape(2 * D_MODEL, D_MODEL)
    ln_rows = (conv_ln_g, conv_ln_b, sg_ln_g, sg_ln_b, row(final_g))
    bs_exp = jnp.repeat(b_s[0].T, CHUNK, axis=1)
    dz6, dq, dx2, ycat, dy, dws, sm1, dbs = _middle(x2d, z6, q, tgt, ln_rows, mod, w_s[0], bs_exp, w_out_full)
    dz6, pk1, dws_r, dbs_r, loss_t, g_w_out4 = _conv_bwd(
        kidx, dq, z6, cw4, dz6, sm1, dws.reshape(D_MODEL, CHUNK), dbs, ycat, dy)
    grad_x, sm2, dcw = _bwd_in(dz6, w12, x2d, dx2, mod, norm_g, dq, z6)
    g_w_in4, pk2, dcw_sh, dcb = _grad_w_in(kidx, h, dz6, sm2, dcw)

    g_w_in, d_w_in, nm_w_in, nv_w_in = _adamw_blocked(w_in[0], m_w_in[0], v_w_in[0], g_w_in4, "adamw_w_in")
    g_w_out, d_w_out, nm_w_out, nv_w_out = _adamw_blocked(w_out[0], m_w_out[0], v_w_out[0], g_w_out4, "adamw_w_out")
    g_w_ada, d_w_ada, nm_w_ada, nv_w_ada, g_b_ada = _adamw_w_ada(cact, pk1, pk2, w_ada[0], m_w_ada[0], v_w_ada[0])

    g_norm_g = pk2[16:17]
    g_cln_g, g_cln_b, g_sln_g, g_sln_b, g_final = (pk1[8 + i:9 + i] for i in range(5))
    loss = loss_t[0, 0]
    g_conv_w = dcw_sh[:CONV_WIDTH]
    g_w_s = dws_r
    g_b_s = dbs_r[:, :HEADS].T
    small = [
        (b_ada, g_b_ada, m_b_ada, v_b_ada),
        (norm_g, g_norm_g, m_norm_g, v_norm_g),
        (conv_w.reshape(CONV_WIDTH, 256), g_conv_w, m_conv_w.reshape(CONV_WIDTH, 256), v_conv_w.reshape(CONV_WIDTH, 256)),
        (conv_b, dcb, m_conv_b, v_conv_b),
        (conv_ln_g, g_cln_g, m_conv_ln_g, v_conv_ln_g),
        (conv_ln_b, g_cln_b, m_conv_ln_b, v_conv_ln_b),
        (sg_ln_g, g_sln_g, m_sg_ln_g, v_sg_ln_g),
        (sg_ln_b, g_sln_b, m_sg_ln_b, v_sg_ln_b),
        (w_s.reshape(D_MODEL, CHUNK), g_w_s, m_w_s.reshape(D_MODEL, CHUNK), v_w_s.reshape(D_MODEL, CHUNK)),
        (b_s[0], g_b_s, m_b_s[0], v_b_s[0]),
        (row(final_g), g_final, row(m_final_g), row(v_final_g)),
    ]
    upd = _adamw_small(small)

    shapes = [w_ada.shape, b_ada.shape, norm_g.shape, w_in.shape, conv_w.shape, conv_b.shape, conv_ln_g.shape,
              conv_ln_b.shape, sg_ln_g.shape, sg_ln_b.shape, w_s.shape, b_s.shape, w_out.shape, final_g.shape]
    grads = [g_w_ada, g_b_ada, g_norm_g, g_w_in, g_conv_w, dcb, g_cln_g, g_cln_b, g_sln_g, g_sln_b, g_w_s, g_b_s,
             g_w_out, g_final]
    big = {0: (d_w_ada, nm_w_ada, nv_w_ada), 3: (d_w_in, nm_w_in, nv_w_in), 12: (d_w_out, nm_w_out, nv_w_out)}
    small_pos = [1, 2, 4, 5, 6, 7, 8, 9, 10, 11, 13]
    trip = [None] * 14
    for i, t in big.items():
        trip[i] = t
    for i, t in zip(small_pos, upd):
        trip[i] = t
    fit = lambda arrs: [a.reshape(s) for a, s in zip(arrs, shapes)]
    return (loss, grad_x.reshape(x.shape), *fit(grads), *fit([t[0] for t in trip]), *fit([t[1] for t in trip]),
            *fit([t[2] for t in trip]))
```

```python
import jax
import jax.numpy as jnp
from jax import lax
from jax.experimental import pallas as pl
from jax.experimental.pallas import tpu as pltpu

F32 = jnp.float32
BF16 = jnp.bfloat16
MESH = pl.DeviceIdType.MESH

D_MODEL = 1024
N_CHIPS = 4
HEADS = 8
CHUNK = 128
CONV_WIDTH = 31
CONV_HALF = CONV_WIDTH // 2
CONV_PAD = 16
EPS = 1e-6
ADAM_LR = 0.001
ADAM_B1 = 0.9
ADAM_B2 = 0.999
ADAM_EPS = 1e-08
ADAM_WD = 0.01
ADAM_STEP = 10

V7X_VMEM_BYTES = 64 * 1024 * 1024
VMEM_LIMIT = V7X_VMEM_BYTES - 8 * 1024 * 1024
ROWS = 16
UNROLL = 8
TOKEN_TILE = 256
FULL_UNROLL = TOKEN_TILE // ROWS
TIME_TILE = 128
K_TILE = 2048

N_GROUPS = 6


def _natural_group(j):
    return (j + 2) % N_GROUPS


def _pos():
    return lax.axis_index("x"), lax.axis_index("y"), lax.axis_index("c")


def _rcopy(src, dst, ssem, rsem, dev):
    return pltpu.make_async_remote_copy(src_ref=src, dst_ref=dst, send_sem=ssem, recv_sem=rsem,
                                        device_id=dev, device_id_type=MESH)


def _vmem():
    return pl.BlockSpec(memory_space=pltpu.VMEM)


def _params(**kw):
    return pltpu.CompilerParams(vmem_limit_bytes=VMEM_LIMIT, **kw)


def _sigmoid(v):
    return 0.5 * jnp.tanh(0.5 * v) + 0.5


def _row_loop(n_rows, body, unroll=1):
    def step(r, carry):
        body(pl.ds(pl.multiple_of(r * ROWS, ROWS), ROWS))
        return carry
    lax.fori_loop(0, n_rows // ROWS, step, 0, unroll=unroll)


def _colsum8(v):
    return v.reshape(v.shape[0] // 8, 8, v.shape[1]).sum(axis=0)


def _mean(v):
    return jnp.mean(v, axis=-1, keepdims=True)


def _dot_nn(a, b):
    return jnp.dot(a, b, preferred_element_type=F32)


def _dot_nt(a, b):
    return lax.dot_general(a, b, (((1,), (1,)), ((), ())), preferred_element_type=F32)


def _dot_tn(a, b):
    return lax.dot_general(a, b, (((0,), (0,)), ((), ())), preferred_element_type=F32)


def _remote_chip(k, r):
    return jnp.bitwise_xor(k, r + 1)


def _front(kidx, x, c, w_ada, b_ada, norm_g, w_in, cw):
    s_len = x.shape[0]
    tmh = min(512, s_len)
    tmz = min(1024, s_len)
    nh = s_len // tmh
    nz = s_len // tmz
    n_steps = nh + 12 * nz

    def remote_block(q):
        return jnp.where(q < 6, q % 2, 2), jnp.where(q < 6, q // 2, q - 6)

    def block_of(i, k):
        r, j = remote_block(jnp.maximum(i - 3, 0))
        return jnp.where(i < 3, 3 * k + i, 3 * _remote_chip(k, r) + j)

    def body(k_ref, x_ref, c_ref, bada_ref, g_ref, cw_ref, wada_hbm, win_hbm,
             h_ref, z_ref, mod_ref, cact_ref, cw4_ref, w12_hbm,
             h_all, wbuf, stage, wada_v, cslab, mslab, cw4_s,
             lsem, csend, crecv, msend, mrecv, wsend, wrecv, isend, irecv, fsend, frecv, osem):
        del k_ref
        t = pl.program_id(0)
        x_, y_, c_ = _pos()
        k = 2 * x_ + y_
        b = 4 * x_ + 2 * y_ + c_
        sib = (x_, y_, 1 - c_)

        def dev_of(r):
            kk = _remote_chip(k, r)
            return (kk // 2, kk % 2, c_)

        def ici(q, kk):
            r, j = remote_block(q)
            return _rcopy(wbuf.at[3 * kk + j, c_], wbuf.at[3 * kk + j, c_], isend.at[q], irecv.at[q], dev_of(r))

        def fwd(q, hf):
            r, j = remote_block(q)
            blk = 3 * _remote_chip(k, r) + j
            return _rcopy(wbuf.at[blk, hf], wbuf.at[blk, hf], fsend.at[q], frecv.at[q], sib)

        def c_copy(q, src):
            d = jnp.bitwise_xor(b, q)
            return _rcopy(cslab.at[src], cslab.at[src], csend.at[q - 1], crecv.at[q - 1], (d // 4, (d // 2) % 2, d % 2))

        def m_copy(r, kk):
            return _rcopy(mslab.at[kk], mslab.at[kk], msend.at[r], mrecv.at[r], dev_of(r))

        def cw_copy(r, kk):
            return _rcopy(cw4_s.at[kk], cw4_s.at[kk], wsend.at[r], wrecv.at[r], dev_of(r))

        def to_hbm(i):
            m = block_of(i, k)
            return pltpu.make_async_copy(wbuf.at[m], w12_hbm.at[m], osem.at[i])

        @pl.when(t == 0)
        def _():
            ld_w = pltpu.make_async_copy(win_hbm, stage, lsem.at[0])
            ld_w.start()
            ld_a = pltpu.make_async_copy(wada_hbm, wada_v, lsem.at[1])
            ld_a.start()
            cslab[b] = jnp.broadcast_to(c_ref[...], (8, D_MODEL))
            for q in range(1, 8):
                c_copy(q, b).start()
            cw4_s[k] = cw_ref[...]
            for r in range(3):
                cw_copy(r, k).start()
            ld_w.wait()
            for j in range(3):
                for hf in range(2):
                    wbuf[3 * k + j, hf] = stage[hf * 512:(hf + 1) * 512, j * 512:(j + 1) * 512].astype(BF16)
            for i in range(3):
                to_hbm(i).start()
            for q in range(1, 8):
                c_copy(q, jnp.bitwise_xor(b, q)).wait_recv()
            row = lax.broadcasted_iota(jnp.int32, (8, D_MODEL), 0)
            call = jnp.zeros((8, D_MODEL), F32)
            for d in range(8):
                call = jnp.where(row == d, cslab[d], call)
            cact = call * _sigmoid(call)
            cact_ref[...] = cact
            ld_a.wait()
            mslab[k] = _dot_nn(cact.astype(BF16), wada_v[...].astype(BF16))
            for r in range(3):
                m_copy(r, k).start()
            for q in range(9):
                ici(q, k).start()
            for r in range(3):
                m_copy(r, _remote_chip(k, r)).wait_recv()
            row8 = lax.broadcasted_iota(jnp.int32, (8, 768), 0)
            for kk in range(N_CHIPS):
                piece = jnp.sum(jnp.where(row8 == b, mslab[kk], 0.0), axis=0, keepdims=True)
                mod_ref[:, kk * 768:(kk + 1) * 768] = piece + bada_ref[:, kk * 768:(kk + 1) * 768]
            for r in range(3):
                cw_copy(r, _remote_chip(k, r)).wait_recv()
            cw4_ref[...] = cw4_s[...]

        @pl.when(t < nh)
        def _():
            shift = mod_ref[:, 0:D_MODEL]
            scale1 = 1.0 + mod_ref[:, D_MODEL:2 * D_MODEL]
            g = g_ref[...]
            base = t * tmh

            def rows_fn(rows):
                xt = x_ref[rows, :]
                r = lax.rsqrt(_mean(xt * xt) + EPS)
                hv = ((xt * r * g) * scale1 + shift).astype(BF16)
                h_ref[rows, :] = hv
                h_all[pl.ds(pl.multiple_of(base + rows.start, ROWS), ROWS), :] = hv
            _row_loop(tmh, rows_fn, unroll=UNROLL)

        @pl.when(t >= nh)
        def _():
            u = t - nh
            i = u // nz
            rt = u % nz
            @pl.when(jnp.logical_and(rt == 0, jnp.logical_and(i >= 2, i <= 10)))
            def _():
                q = i - 2
                r, _ = remote_block(q)
                ici(q, _remote_chip(k, r)).wait_recv()
                fwd(q, c_).start()

            @pl.when(jnp.logical_and(rt == 0, i >= 3))
            def _():
                fwd(i - 3, 1 - c_).wait_recv()
                to_hbm(i).start()
            m = block_of(i, k)
            hb = h_all[pl.ds(pl.multiple_of(rt * tmz, tmz), tmz), :]
            z_ref[0] = _dot_nn(hb, wbuf[m].reshape(D_MODEL, 512))

        @pl.when(t == n_steps - 1)
        def _():
            for q in range(1, 8):
                c_copy(q, b).wait_send()
            for r in range(3):
                m_copy(r, k).wait_send()
                cw_copy(r, k).wait_send()
            for q in range(9):
                ici(q, k).wait_send()
                fwd(q, c_).wait_send()
            for i in range(12):
                to_hbm(i).wait()

    def z_index(t, k_ref):
        u = jnp.maximum(t - nh, 0)
        m = block_of(u // nz, k_ref[0])
        return ((m // 2 + 4) % N_GROUPS, u % nz, m % 2)

    tok = lambda t, k_ref: (jnp.minimum(t, nh - 1), 0)
    const2 = lambda t, k_ref: (0, 0)
    grid_spec = pltpu.PrefetchScalarGridSpec(
        num_scalar_prefetch=1,
        grid=(n_steps,),
        in_specs=[pl.BlockSpec((tmh, D_MODEL), tok),
                  pl.BlockSpec((1, D_MODEL), const2),
                  pl.BlockSpec((1, 3 * D_MODEL), const2),
                  pl.BlockSpec((1, D_MODEL), const2),
                  pl.BlockSpec((32, 256), const2),
                  pl.BlockSpec(memory_space=pl.ANY),
                  pl.BlockSpec(memory_space=pl.ANY)],
        out_specs=(pl.BlockSpec((tmh, D_MODEL), tok),
                   pl.BlockSpec((1, tmz, 512), z_index),
                   pl.BlockSpec((1, 3 * D_MODEL), const2),
                   pl.BlockSpec((8, D_MODEL), const2),
                   pl.BlockSpec((N_CHIPS, 32, 256), lambda t, k_ref: (0, 0, 0)),
                   pl.BlockSpec(memory_space=pl.ANY)),
        scratch_shapes=[pltpu.VMEM((s_len, D_MODEL), BF16),
                        pltpu.VMEM((12, 2, 512, 512), BF16),
                        pltpu.VMEM((D_MODEL, 1536), F32),
                        pltpu.VMEM((D_MODEL, 768), F32),
                        pltpu.VMEM((8, 8, D_MODEL), F32),
                        pltpu.VMEM((N_CHIPS, 8, 768), F32),
                        pltpu.VMEM((N_CHIPS, 32, 256), F32),
                        pltpu.SemaphoreType.DMA((2,)),
                        pltpu.SemaphoreType.DMA((7,)), pltpu.SemaphoreType.DMA((7,)),
                        pltpu.SemaphoreType.DMA((3,)), pltpu.SemaphoreType.DMA((3,)),
                        pltpu.SemaphoreType.DMA((3,)), pltpu.SemaphoreType.DMA((3,)),
                        pltpu.SemaphoreType.DMA((9,)), pltpu.SemaphoreType.DMA((9,)),
                        pltpu.SemaphoreType.DMA((9,)), pltpu.SemaphoreType.DMA((9,)),
                        pltpu.SemaphoreType.DMA((12,))])
    return pl.pallas_call(
        body, name="front",
        grid_spec=grid_spec,
        out_shape=(jax.ShapeDtypeStruct((s_len, D_MODEL), BF16),
                   jax.ShapeDtypeStruct((N_GROUPS, s_len, D_MODEL), F32),
                   jax.ShapeDtypeStruct((1, 3 * D_MODEL), F32),
                   jax.ShapeDtypeStruct((8, D_MODEL), F32),
                   jax.ShapeDtypeStruct((N_CHIPS, 32, 256), F32),
                   jax.ShapeDtypeStruct((12, 2, 512, 512), BF16)),
        compiler_params=_params(dimension_semantics=("arbitrary",)),
    )(kidx, x, c, b_ada, norm_g, cw, w_ada, w_in)


def _all_reduce_partners():
    x, y, c = _pos()
    return [(x, y, 1 - c), (x, 1 - y, c), (1 - x, y, c)]


def _butterfly_event(e, bufs, recvs, ssem, rsem, partners):
    n = len(bufs)

    def copies(s):
        return [_rcopy(buf, recv.at[s], ssem.at[s * n + i], rsem.at[s * n + i], partners[s])
                for i, (buf, recv) in enumerate(zip(bufs, recvs))]

    if e > 0:
        for cp in copies(e - 1):
            cp.wait()
        for buf, recv in zip(bufs, recvs):
            buf[...] = buf[...] + recv[e - 1]
    if e < len(partners):
        for cp in copies(e):
            cp.start()


def _onehot_rows(v, b):
    row = lax.broadcasted_iota(jnp.int32, (8, v.shape[1]), 0)
    return jnp.where(row == b, jnp.broadcast_to(v, (8, v.shape[1])), 0.0)


def _conv_fwd(z6, cw4, conv_b, w_out):
    s_len = z6.shape[1]
    tt = TIME_TILE
    n_blocks = D_MODEL // 128

    def body(z_ref, cw_ref, cb_ref, wout_hbm, q_ref, p_ref, sig_ref, t2_ref, wout4_hbm,
             ppad, stage, wbuf, lsem, isend, irecv, fsend, frecv, osem):
        jb = pl.program_id(0)
        x_, y_, c_ = _pos()
        k = 2 * x_ + y_
        sib = (x_, y_, 1 - c_)

        def ici(r, kk):
            rk = _remote_chip(k, r)
            return _rcopy(wbuf.at[kk, c_], wbuf.at[kk, c_], isend.at[r], irecv.at[r], (rk // 2, rk % 2, c_))

        def fwd(r, hf):
            kk = _remote_chip(k, r)
            return _rcopy(wbuf.at[kk, hf], wbuf.at[kk, hf], fsend.at[r], frecv.at[r], sib)

        @pl.when(jb == 0)
        def _():
            ld = pltpu.make_async_copy(wout_hbm, stage, lsem)
            ld.start()
            ld.wait()
            for hf in range(2):
                wbuf[k, hf] = stage[hf * 256:(hf + 1) * 256, :].astype(BF16)
            for r in range(3):
                ici(r, k).start()

        @pl.when(jb == (5 * n_blocks) // 8)
        def _():
            for r in range(3):
                ici(r, _remote_chip(k, r)).wait_recv()
                fwd(r, c_).start()

        zero = jnp.zeros((CONV_PAD, 128), F32)
        ppad[0:CONV_PAD, :] = zero
        ppad[s_len + CONV_PAD:s_len + 2 * CONV_PAD, :] = zero

        def fill(i, carry):
            t0 = pl.multiple_of(i * tt, tt)
            sig = _sigmoid(z_ref[1, pl.ds(t0, tt), :])
            p = z_ref[0, pl.ds(t0, tt), :] * sig
            ppad[pl.ds(CONV_PAD + t0, tt), :] = p
            p_ref[0, pl.ds(t0, tt), :] = p
            sig_ref[0, pl.ds(t0, tt), :] = sig
            t2_ref[0, pl.ds(t0, tt), :] = p * (1.0 - sig)
            return carry
        lax.fori_loop(0, s_len // tt, fill, 0)
        w = cw_ref[0]
        bias = cb_ref[...]

        def conv(i, carry):
            t0 = pl.multiple_of(i * tt, tt)
            acc = jnp.broadcast_to(bias, (tt, 128))
            for o in range(1, CONV_WIDTH + 1):
                acc = acc + w[o - 1:o, :] * ppad[pl.ds(t0 + o, tt), :]
            q_ref[0, pl.ds(t0, tt), :] = acc
            return carry
        lax.fori_loop(0, s_len // tt, conv, 0)

        @pl.when(jb == n_blocks - 1)
        def _():
            for r in range(3):
                fwd(r, 1 - c_).wait_recv()
            out = pltpu.make_async_copy(wbuf, wout4_hbm, osem)
            out.start()
            for r in range(3):
                ici(r, k).wait_send()
                fwd(r, c_).wait_send()
            out.wait()

    return pl.pallas_call(
        body, name="conv_fwd",
        grid=(n_blocks,),
        out_shape=tuple(jax.ShapeDtypeStruct((n_blocks, s_len, 128), F32) for _ in range(4))
        + (jax.ShapeDtypeStruct((N_CHIPS, 2, 256, D_MODEL), BF16),),
        in_specs=[pl.BlockSpec((2, s_len, 128), lambda j: (2, 0, j)),
                  pl.BlockSpec((1, 32, 128), lambda j: (j // 2, 0, j % 2)),
                  pl.BlockSpec((1, 128), lambda j: (0, j)),
                  pl.BlockSpec(memory_space=pl.ANY)],
        out_specs=tuple(pl.BlockSpec((1, s_len, 128), lambda j: (j, 0, 0)) for _ in range(4))
        + (pl.BlockSpec(memory_space=pl.ANY),),
        scratch_shapes=[pltpu.VMEM((s_len + 2 * CONV_PAD, 128), F32),
                        pltpu.VMEM((512, D_MODEL), F32),
                        pltpu.VMEM((N_CHIPS, 2, 256, D_MODEL), BF16),
                        pltpu.SemaphoreType.DMA,
                        pltpu.SemaphoreType.DMA((3,)), pltpu.SemaphoreType.DMA((3,)),
                        pltpu.SemaphoreType.DMA((3,)), pltpu.SemaphoreType.DMA((3,)),
                        pltpu.SemaphoreType.DMA],
        compiler_params=_params(dimension_semantics=("arbitrary",)),
    )(z6, cw4, conv_b, w_out)


def _middle(x, z6, q, target, ln_rows, mod, w_s, bs_exp, w_out):
    s_len = x.shape[0]
    tm = TOKEN_TILE
    n_steps = s_len // tm
    n_chunks = tm // CHUNK
    inv_d = 1.0 / D_MODEL

    def body(x_ref, z_ref, q_ref, tgt_ref, cg_ref, cb_ref, sg_ref, sb_ref, fg_ref, mod_ref, ws_ref, bs_ref, wout_ref,
             dz_ref, dq_ref, dx2_ref, ycat_ref, dy_ref, dws_ref, sm_ref, dbs_ref,
             vl_scr, vm_scr, y_scr, dycat_scr, dvm_scr, dvl_scr, acc_scr, dbs_acc, keep, rstd_scr):
        i = pl.program_id(0)

        @pl.when(i == 0)
        def _():
            acc_scr[...] = jnp.zeros_like(acc_scr)
            dbs_acc[...] = jnp.zeros_like(dbs_acc)
            dws_ref[...] = jnp.zeros_like(dws_ref)

        cg, cb, sg, sb, fg = cg_ref[...], cb_ref[...], sg_ref[...], sb_ref[...], fg_ref[...]
        gm = mod_ref[:, 2 * D_MODEL:3 * D_MODEL]

        def norm_stats(t):
            c = t - _mean(t)
            rstd = lax.rsqrt(_mean(c * c) + EPS)
            return c * rstd, rstd

        def phase1(rows):
            qhat, rstd_q = norm_stats(jnp.concatenate([q_ref[blk, rows, :] for blk in range(D_MODEL // 128)], axis=1))
            ln = qhat * cg + cb
            gz = z_ref[0, rows, :]
            sig_ln = _sigmoid(ln)
            sig_g = _sigmoid(gz)
            ycat_ref[rows, 0:D_MODEL] = ((ln * sig_ln) * (gz * sig_g)).astype(BF16)
            vhat, rstd_v = norm_stats(z_ref[2, rows, :])
            vl_scr[rows, :] = (vhat * sg + sb).astype(BF16)
            keep[0, rows, :] = qhat
            keep[1, rows, :] = vhat
            keep[2, rows, :] = sig_ln
            keep[3, rows, :] = sig_g
            rstd_scr[0, rows, :] = rstd_q
            rstd_scr[1, rows, :] = rstd_v
        _row_loop(tm, phase1, unroll=FULL_UNROLL)

        for ch in range(n_chunks):
            r0 = ch * CHUNK
            for h in range(HEADS):
                c0 = h * CHUNK
                vm_scr[r0:r0 + CHUNK, c0:c0 + CHUNK] = (
                    _dot_nn(ws_ref[h].astype(BF16), vl_scr[r0:r0 + CHUNK, c0:c0 + CHUNK]) + bs_ref[:, c0:c0 + CHUNK])

        def phase3(rows):
            bg = z_ref[3, rows, :]
            sig_b = _sigmoid(bg)
            keep[4, rows, :] = sig_b
            ycat_ref[rows, D_MODEL:2 * D_MODEL] = (z_ref[1, rows, :] * vm_scr[rows, :] * (bg * sig_b)).astype(BF16)
        _row_loop(tm, phase3, unroll=FULL_UNROLL)

        y_scr[...] = _dot_nn(ycat_ref[...], wout_ref[...])

        def phase5(rows):
            y = y_scr[rows, :]
            x2 = x_ref[rows, :] + gm * y
            r2 = lax.rsqrt(_mean(x2 * x2) + EPS)
            xn2 = x2 * r2
            diff = xn2 * fg - tgt_ref[rows, :]
            acc_scr[6] += _colsum8(diff * diff)
            dout = diff * inv_d
            acc_scr[0] += _colsum8(dout * xn2)
            dxn = dout * fg
            dx2 = r2 * (dxn - xn2 * _mean(dxn * xn2))
            dx2_ref[rows, :] = dx2
            acc_scr[1] += _colsum8(dx2 * y)
            dy_ref[rows, :] = (dx2 * gm).astype(BF16)
        _row_loop(tm, phase5, unroll=FULL_UNROLL)

        dycat_scr[...] = _dot_nt(dy_ref[...], wout_ref[...])

        def phase7(rows):
            dyb = dycat_scr[rows, D_MODEL:2 * D_MODEL]
            u = z_ref[1, rows, :]
            bg = z_ref[3, rows, :]
            vm = vm_scr[rows, :]
            sig = keep[4, rows, :]
            silu = bg * sig
            dz_ref[1, rows, :] = (dyb * vm * silu).astype(BF16)
            dvm = dyb * u * silu
            dz_ref[3, rows, :] = (dyb * u * vm * (sig * (1.0 + bg * (1.0 - sig)))).astype(BF16)
            dvm_scr[rows, :] = dvm.astype(BF16)
            pos = pl.ds(pl.multiple_of(rows.start % CHUNK, ROWS), ROWS)
            dbs_acc[pos, :] += dvm
        _row_loop(tm, phase7, unroll=FULL_UNROLL)

        for ch in range(n_chunks):
            r0 = ch * CHUNK
            for h in range(HEADS):
                c0 = h * CHUNK
                dvm_b = dvm_scr[r0:r0 + CHUNK, c0:c0 + CHUNK]
                dws_ref[h] += _dot_nt(dvm_b, vl_scr[r0:r0 + CHUNK, c0:c0 + CHUNK])
                dvl_scr[r0:r0 + CHUNK, c0:c0 + CHUNK] = _dot_tn(ws_ref[h].astype(BF16), dvm_b)

        def phase9(rows):
            vhat, rstd_v = keep[1, rows, :], rstd_scr[1, rows, :]
            dvl = dvl_scr[rows, :]
            acc_scr[4] += _colsum8(dvl * vhat)
            acc_scr[5] += _colsum8(dvl)
            dvh = dvl * sg
            dz_ref[2, rows, :] = (rstd_v * (dvh - _mean(dvh) - vhat * _mean(dvh * vhat))).astype(BF16)
            qhat, rstd_q = keep[0, rows, :], rstd_scr[0, rows, :]
            ln = qhat * cg + cb
            sig_ln = keep[2, rows, :]
            gz = z_ref[0, rows, :]
            sig_g = keep[3, rows, :]
            dya = dycat_scr[rows, 0:D_MODEL]
            dz_ref[0, rows, :] = (dya * (ln * sig_ln) * (sig_g * (1.0 + gz * (1.0 - sig_g)))).astype(BF16)
            dln = (dya * (gz * sig_g)) * (sig_ln * (1.0 + ln * (1.0 - sig_ln)))
            acc_scr[2] += _colsum8(dln * qhat)
            acc_scr[3] += _colsum8(dln)
            dqh = dln * cg
            dq = rstd_q * (dqh - _mean(dqh) - qhat * _mean(dqh * qhat))
            for blk in range(D_MODEL // 128):
                dq_ref[blk, rows, :] = dq[:, blk * 128:(blk + 1) * 128]
        _row_loop(tm, phase9, unroll=FULL_UNROLL)

        @pl.when(i == n_steps - 1)
        def _():
            for qi in range(8):
                scale = 0.5 * inv_d if qi == 6 else 1.0
                sm_ref[qi:qi + 1, :] = jnp.sum(acc_scr[qi], axis=0, keepdims=True) * scale
            lane = lax.broadcasted_iota(jnp.int32, (CHUNK, CHUNK), 1)
            tile = jnp.zeros((CHUNK, CHUNK), F32)
            for h in range(HEADS):
                col = jnp.sum(dbs_acc[:, h * CHUNK:(h + 1) * CHUNK], axis=1, keepdims=True)
                tile = jnp.where(lane == h, col, tile)
            dbs_ref[...] = tile

    tok = lambda i: (i, 0)
    const2 = lambda i: (0, 0)
    return pl.pallas_call(
        body, name="middle",
        grid=(n_steps,),
        out_shape=(jax.ShapeDtypeStruct((N_GROUPS, s_len, D_MODEL), BF16),
                   jax.ShapeDtypeStruct((D_MODEL // 128, s_len, 128), F32),
                   jax.ShapeDtypeStruct((s_len, D_MODEL), F32),
                   jax.ShapeDtypeStruct((s_len, 2 * D_MODEL), BF16),
                   jax.ShapeDtypeStruct((s_len, D_MODEL), BF16),
                   jax.ShapeDtypeStruct((HEADS, CHUNK, CHUNK), F32),
                   jax.ShapeDtypeStruct((8, D_MODEL), F32),
                   jax.ShapeDtypeStruct((CHUNK, CHUNK), F32)),
        in_specs=[pl.BlockSpec((tm, D_MODEL), tok),
                  pl.BlockSpec((4, tm, D_MODEL), lambda i: (0, i, 0)),
                  pl.BlockSpec((D_MODEL // 128, tm, 128), lambda i: (0, i, 0)),
                  pl.BlockSpec((tm, D_MODEL), tok),
                  *[pl.BlockSpec((1, D_MODEL), const2) for _ in range(5)],
                  pl.BlockSpec((1, 3 * D_MODEL), const2),
                  pl.BlockSpec((HEADS, CHUNK, CHUNK), lambda i: (0, 0, 0)),
                  pl.BlockSpec((CHUNK, D_MODEL), const2),
                  pl.BlockSpec((2 * D_MODEL, D_MODEL), const2, pipeline_mode=pl.Buffered(1))],
        out_specs=(pl.BlockSpec((4, tm, D_MODEL), lambda i: (0, i, 0)),
                   pl.BlockSpec((D_MODEL // 128, tm, 128), lambda i: (0, i, 0)),
                   pl.BlockSpec((tm, D_MODEL), tok),
                   pl.BlockSpec((tm, 2 * D_MODEL), tok),
                   pl.BlockSpec((tm, D_MODEL), tok),
                   pl.BlockSpec((HEADS, CHUNK, CHUNK), lambda i: (0, 0, 0)),
                   pl.BlockSpec((8, D_MODEL), const2),
                   pl.BlockSpec((CHUNK, CHUNK), const2)),
        scratch_shapes=[pltpu.VMEM((tm, D_MODEL), BF16),
                        pltpu.VMEM((tm, D_MODEL), F32),
                        pltpu.VMEM((tm, D_MODEL), F32),
                        pltpu.VMEM((tm, 2 * D_MODEL), F32),
                        pltpu.VMEM((tm, D_MODEL), BF16),
                        pltpu.VMEM((tm, D_MODEL), F32),
                        pltpu.VMEM((8, 8, D_MODEL), F32),
                        pltpu.VMEM((CHUNK, D_MODEL), F32),
                        pltpu.VMEM((5, tm, D_MODEL), F32),
                        pltpu.VMEM((2, tm, 1), F32)],
        compiler_params=_params(dimension_semantics=("arbitrary",)),
    )(x, z6, q, target, *ln_rows, mod, w_s, bs_exp, w_out)


def _conv_bwd(kidx, dq, sig, t2, cw4, dz6, sm1, dws, dbs, ycat, dy):
    s_len = dq.shape[1]
    tt = TIME_TILE
    n_blocks = D_MODEL // 128
    shp = [(16, D_MODEL), (D_MODEL, 128), (128, 128)]
    nj, nr, nc = 1, 256, D_MODEL
    kt = n_blocks // N_CHIPS
    tk = s_len // kt

    def body(k_ref, dq_ref, sig_ref, t2_ref, cw_ref, dz_in, sm1_ref, dws_in, dbs_in, ycat_ref, dy_ref,
             dz_ref, pk1_ref, dws_ref, dbs_ref, loss_ref, gout_ref,
             dqpad, b_pk, b_dws, b_dbs, r_pk, r_dws, r_dbs, ssem, rsem, *rs_scr):
        del k_ref, dz_in
        jb = pl.program_id(0)
        x_, y_, c_ = _pos()

        @pl.when(jb == 0)
        def _():
            b_pk[0:8, :] = _onehot_rows(sm1_ref[1:2, :], 4 * x_ + 2 * y_ + c_)
            for r, src in enumerate([2, 3, 4, 5, 0, 6]):
                b_pk[8 + r:9 + r, :] = sm1_ref[src:src + 1, :]
            b_pk[14:16, :] = jnp.zeros((2, D_MODEL), F32)
            b_dws[...] = dws_in[...]
            b_dbs[...] = dbs_in[...]

        for e in range(4):
            @pl.when(jb == (e * n_blocks) // 4)
            def _():
                _butterfly_event(e, [b_pk, b_dws, b_dbs], [r_pk, r_dws, r_dbs], ssem, rsem, _all_reduce_partners())

        @pl.when(jb == (3 * n_blocks) // 4)
        def _():
            pk1_ref[...] = b_pk[...]
            dws_ref[...] = b_dws[...]
            dbs_ref[...] = b_dbs[...]
            loss_ref[...] = jnp.broadcast_to(jnp.sum(b_pk[13:14, :], axis=1, keepdims=True), (8, 128))

        zero = jnp.zeros((CONV_PAD, 128), F32)
        dqpad[0:CONV_PAD, :] = zero
        dqpad[s_len + CONV_PAD:s_len + 2 * CONV_PAD, :] = zero
        dqpad[CONV_PAD:s_len + CONV_PAD, :] = dq_ref[0]
        w = cw_ref[0]

        def bwd(i, carry):
            t0 = i * tt
            dp = jnp.zeros((tt, 128), F32)
            for o in range(1, CONV_WIDTH + 1):
                dp = dp + w[CONV_WIDTH - o:CONV_WIDTH - o + 1, :] * dqpad[pl.ds(t0 + o, tt), :]
            dz_ref[0, pl.ds(t0, tt), :] = (dp * sig_ref[0, pl.ds(t0, tt), :]).astype(BF16)
            dz_ref[1, pl.ds(t0, tt), :] = (dp * t2_ref[0, pl.ds(t0, tt), :]).astype(BF16)
            return carry

        n_iter = s_len // tt
        n_groups = min(8, n_iter)
        rows = tk // n_groups

        def conv_and_matmul(add):
            for g in range(n_groups):
                for it in range(g * n_iter // n_groups, (g + 1) * n_iter // n_groups):
                    bwd(it, 0)
                add(_dot_tn(ycat_ref[g * rows:(g + 1) * rows, :], dy_ref[g * rows:(g + 1) * rows, :]))
        _rs_step(jb // kt, jb % kt, kt, nj, nr, nc, conv_and_matmul, gout_ref, rs_scr)

    const2 = lambda j, k_ref: (0, 0)
    grid_spec = pltpu.PrefetchScalarGridSpec(
        num_scalar_prefetch=1,
        grid=(n_blocks,),
        in_specs=[*[pl.BlockSpec((1, s_len, 128), lambda j, k_ref: (j, 0, 0)) for _ in range(3)],
                  pl.BlockSpec((1, 32, 128), lambda j, k_ref: (j // 2, 0, j % 2)),
                  pl.BlockSpec(memory_space=pl.ANY),
                  pl.BlockSpec((8, D_MODEL), const2),
                  pl.BlockSpec((D_MODEL, 128), const2),
                  pl.BlockSpec((128, 128), const2),
                  pl.BlockSpec((tk, 512), lambda j, k_ref: (j % kt, _rs_block(j // kt, k_ref[0], nj))),
                  pl.BlockSpec((tk, D_MODEL), lambda j, k_ref: (j % kt, 0))],
        out_specs=(pl.BlockSpec((2, s_len, 128), lambda j, k_ref: (2, 0, j)),)
        + tuple(pl.BlockSpec(s, const2) for s in shp) + (pl.BlockSpec((8, 128), const2),
                                                         pl.BlockSpec(memory_space=pl.ANY)),
        scratch_shapes=[pltpu.VMEM((s_len + 2 * CONV_PAD, 128), F32)]
        + [pltpu.VMEM(s, F32) for s in shp] + [pltpu.VMEM((3,) + s, F32) for s in shp]
        + [pltpu.SemaphoreType.DMA((9,)), pltpu.SemaphoreType.DMA((9,))] + _rs_scratch(nj, nr, nc))
    return pl.pallas_call(
        body, name="conv_bwd",
        grid_spec=grid_spec,
        out_shape=(jax.ShapeDtypeStruct(dz6.shape, BF16),)
        + tuple(jax.ShapeDtypeStruct(s, F32) for s in shp) + (jax.ShapeDtypeStruct((8, 128), F32),
                                                              jax.ShapeDtypeStruct((nj, 2, nr, nc), F32)),
        input_output_aliases={5: 0},
        compiler_params=_params(dimension_semantics=("arbitrary",)),
    )(kidx, dq, sig, t2, cw4, dz6, sm1, dws, dbs, ycat, dy)


def _bwd_in(dz6, w12, x, dx2, mod, norm_g, dq, p):
    s_len = x.shape[0]
    tm = TOKEN_TILE
    n_steps = s_len // tm
    tt = TIME_TILE
    n_cblocks = D_MODEL // 128
    parts = max(1, n_steps // n_cblocks)
    nsub = max(1, n_cblocks // n_steps)
    cw = 128 * nsub
    tiles = (s_len // tt) // parts

    def body(dz_ref, w_ref, x_ref, dx2_ref, mod_ref, g_ref, dq_ref, p_ref, gx_ref, sm_ref, dcw_ref,
             dh_scr, acc_scr, ppad, wacc):
        i = pl.program_id(0)
        part = i % parts

        @pl.when(i == 0)
        def _():
            acc_scr[...] = jnp.zeros_like(acc_scr)

        @pl.when(part == 0)
        def _():
            zero = jnp.zeros((CONV_PAD, 128), F32)
            for sub in range(nsub):
                ppad[sub, 0:CONV_PAD, :] = zero
                ppad[sub, s_len + CONV_PAD:s_len + 2 * CONV_PAD, :] = zero
                ppad[sub, CONV_PAD:s_len + CONV_PAD, :] = p_ref[sub]
            wacc[...] = jnp.zeros_like(wacc)

        def dw_tile(tile):
            t0 = pl.multiple_of((part * tiles + tile) * tt, tt)
            for sub in range(nsub):
                dqt = dq_ref[sub, pl.ds(t0, tt), :]
                for o in range(1, CONV_WIDTH + 1):
                    wacc[sub, o - 1] += _colsum8(dqt * ppad[sub, pl.ds(t0 + o, tt), :])
                wacc[sub, CONV_WIDTH] += _colsum8(dqt)

        dots = [(j, hf) for j in range(N_GROUPS) for hf in range(2)]
        dh = jnp.zeros((tm, D_MODEL), F32)
        for d, (j, hf) in enumerate(dots):
            dh = dh + _dot_nt(dz_ref[j, :, hf * 512:(hf + 1) * 512], w_ref[2 * _natural_group(j) + hf])
            for tile in range(d * tiles // len(dots), (d + 1) * tiles // len(dots)):
                dw_tile(tile)
        dh_scr[...] = dh

        @pl.when(part == parts - 1)
        def _():
            for sub in range(nsub):
                for k in range(32):
                    dcw_ref[k:k + 1, sub * 128:(sub + 1) * 128] = jnp.sum(wacc[sub, k], axis=0, keepdims=True)

        scale1 = 1.0 + mod_ref[:, D_MODEL:2 * D_MODEL]
        g = g_ref[...]

        def rows_fn(rows):
            xt = x_ref[rows, :]
            r = lax.rsqrt(_mean(xt * xt) + EPS)
            xn = xt * r
            dhr = dh_scr[rows, :]
            acc_scr[0] += _colsum8(dhr)
            acc_scr[1] += _colsum8(dhr * (xn * g))
            acc_scr[2] += _colsum8(dhr * scale1 * xn)
            dxn = dhr * (g * scale1)
            gx_ref[rows, :] = dx2_ref[rows, :] + r * (dxn - xn * _mean(dxn * xn))
        _row_loop(tm, rows_fn, unroll=FULL_UNROLL)

        @pl.when(i == n_steps - 1)
        def _():
            for qi in range(8):
                sm_ref[qi:qi + 1, :] = jnp.sum(acc_scr[qi], axis=0, keepdims=True)

    tok = lambda i: (i, 0)
    const2 = lambda i: (0, 0)
    return pl.pallas_call(
        body, name="bwd_in",
        grid=(n_steps,),
        out_shape=(jax.ShapeDtypeStruct((s_len, D_MODEL), F32), jax.ShapeDtypeStruct((8, D_MODEL), F32),
                   jax.ShapeDtypeStruct((32, D_MODEL), F32)),
        in_specs=[pl.BlockSpec((N_GROUPS, tm, D_MODEL), lambda i: (0, i, 0)),
                  pl.BlockSpec((12, D_MODEL, 512), lambda i: (0, 0, 0), pipeline_mode=pl.Buffered(1)),
                  pl.BlockSpec((tm, D_MODEL), tok),
                  pl.BlockSpec((tm, D_MODEL), tok),
                  pl.BlockSpec((1, 3 * D_MODEL), const2),
                  pl.BlockSpec((1, D_MODEL), const2),
                  pl.BlockSpec((nsub, s_len, 128), lambda i: (i // parts, 0, 0)),
                  pl.BlockSpec((nsub, s_len, 128), lambda i: (i // parts, 0, 0))],
        out_specs=(pl.BlockSpec((tm, D_MODEL), tok), pl.BlockSpec((8, D_MODEL), const2),
                   pl.BlockSpec((32, cw), lambda i: (0, i // parts))),
        scratch_shapes=[pltpu.VMEM((tm, D_MODEL), F32), pltpu.VMEM((8, 8, D_MODEL), F32),
                        pltpu.VMEM((nsub, s_len + 2 * CONV_PAD, 128), F32), pltpu.VMEM((nsub, 32, 8, 128), F32)],
        compiler_params=_params(dimension_semantics=("arbitrary",)),
    )(dz6, w12, x, dx2, mod, norm_g, dq, p)


def _rs_shard_block(i, nj):
    return jnp.where(i < 3 * nj, i % 3, 3), jnp.where(i < 3 * nj, i // 3, i - 3 * nj)


def _rs_block(i, k, nj):
    s, j = _rs_shard_block(i, nj)
    return nj * ((k + 1 + s) % N_CHIPS) + j


def _rs_scratch(nj, nr, nc):
    nblk, nfor = N_CHIPS * nj, 3 * nj
    return [pltpu.VMEM((2, 2, nr, nc), F32),
            pltpu.VMEM((nblk, nr, nc), F32),
            pltpu.VMEM((nfor, nr, nc), BF16),
            pltpu.VMEM((nfor, nr, nc), BF16),
            pltpu.VMEM((nj, nr, nc), F32),
            pltpu.SemaphoreType.DMA,
            pltpu.SemaphoreType.DMA((nblk,)), pltpu.SemaphoreType.DMA((nblk,)),
            pltpu.SemaphoreType.DMA((nfor,)), pltpu.SemaphoreType.DMA((nfor,)),
            pltpu.SemaphoreType.DMA, pltpu.SemaphoreType.DMA]


def _rs_step(i, kk, kt, nj, nr, nc, partial_fn, out_ref, scr):
    nblk, nfor = N_CHIPS * nj, 3 * nj
    acc, recv_a, send_b, recv_b, own_ps, lsem, psend, precv, isend, irecv, fsend, frecv = scr
    x, y, c = _pos()
    k = 2 * x + y
    sib = (x, y, 1 - c)
    slot = i % 2

    @pl.when(kk == 0)
    def _():
        acc[slot] = jnp.zeros((2, nr, nc), F32)

    def pair_copy(ii, sl):
        return _rcopy(acc.at[sl, 1 - c], recv_a.at[ii], psend.at[ii], precv.at[ii], sib)

    def pair_sum(ii, sl):
        cp = pair_copy(ii, sl)
        cp.wait_recv()
        cp.wait_send()
        return acc[sl, c] + recv_a[ii]

    def ici_copy(ii, kd, rslot):
        return _rcopy(send_b.at[ii], recv_b.at[rslot], isend.at[ii], irecv.at[rslot], (kd // 2, kd % 2, c))

    def finalize(ii, sl):
        ps = pair_sum(ii, sl)
        s, j = _rs_shard_block(ii, nj)

        @pl.when(s < 3)
        def _():
            send_b[ii] = ps.astype(BF16)
            ici_copy(ii, (k + 1 + s) % N_CHIPS, (2 - s) * nj + j).start()

        @pl.when(s == 3)
        def _():
            own_ps[j] = ps

    @pl.when(jnp.logical_and(kk == kt - 1, i >= 1))
    def _():
        finalize(i - 1, 1 - slot)

    def add_partial(res):
        acc[slot, 0] += res[0:nr]
        acc[slot, 1] += res[nr:2 * nr]
    partial_fn(add_partial)

    @pl.when(kk == kt - 1)
    def _():
        pair_copy(i, slot).start()

        @pl.when(i == nblk - 1)
        def _():
            own_ps[nj - 1] = pair_sum(nblk - 1, (nblk - 1) % 2)
            for r in range(nfor):
                ici_copy(0, k, r).wait_recv()
            for j in range(nj):
                tot = own_ps[j]
                for s in range(3):
                    tot = tot + recv_b[s * nj + j].astype(F32)
                own_ps[j] = tot
            loc = pltpu.make_async_copy(own_ps, out_ref.at[:, c], lsem)
            loc.start()
            swap = _rcopy(own_ps, out_ref.at[:, c], fsend, frecv, sib)
            swap.start()
            loc.wait()
            swap.wait_send()
            _rcopy(own_ps, out_ref.at[:, 1 - c], fsend, frecv, sib).wait_recv()
            for ii in range(nfor):
                ici_copy(ii, k, 0).wait_send()


def _grad_w_in(kidx, h, dz6, sm2, dcw):
    s_len = h.shape[0]
    tk = min(K_TILE, s_len)
    kt = s_len // tk
    nj, nr, nc = 3, 512, 512
    nblk = N_CHIPS * nj

    def body(k_ref, a_ref, b_ref, sm2_ref, dcw_in, out_ref, pk2_ref, dcwsh_ref, dcb_ref, *scr):
        del k_ref
        b_pk2, b_dcw, r_pk2, r_dcw, bsend, brecv = scr[-6:]
        i = pl.program_id(0)
        kk = pl.program_id(1)
        x, y, c = _pos()
        k = 2 * x + y

        @pl.when(jnp.logical_and(kk == 0, i == 0))
        def _():
            b = 4 * x + 2 * y + c
            b_pk2[0:8, :] = _onehot_rows(sm2_ref[0:1, :], b)
            b_pk2[8:16, :] = _onehot_rows(sm2_ref[1:2, :], b)
            b_pk2[16:17, :] = sm2_ref[2:3, :]
            b_pk2[17:24, :] = jnp.zeros((7, D_MODEL), F32)
            b_dcw[...] = dcw_in[...]

        for e in range(4):
            @pl.when(jnp.logical_and(kk == 0, i == (e * nblk) // 4))
            def _():
                _butterfly_event(e, [b_pk2, b_dcw], [r_pk2, r_dcw], bsend, brecv, _all_reduce_partners())

        @pl.when(jnp.logical_and(kk == 0, i == (3 * nblk) // 4))
        def _():
            pk2_ref[...] = b_pk2[...]
            sel = jnp.zeros((32, 256), F32)
            for kc in range(N_CHIPS):
                sel = jnp.where(k == kc, b_dcw[:, kc * 256:(kc + 1) * 256], sel)
            dcwsh_ref[...] = sel
            dcb_ref[...] = b_dcw[31:32, :]

        _rs_step(i, kk, kt, nj, nr, nc, lambda add: add(_dot_tn(a_ref[...], b_ref[0])), out_ref, scr[:-6])

    def dz_index(i, kk, k_ref):
        m = _rs_block(i, k_ref[0], nj)
        return ((m // 2 + 4) % N_GROUPS, kk, m % 2)

    const2 = lambda i, kk, k_ref: (0, 0)
    small_out = [(24, D_MODEL), (32, 256), (1, D_MODEL)]
    small_buf = [(24, D_MODEL), (32, D_MODEL)]
    grid_spec = pltpu.PrefetchScalarGridSpec(
        num_scalar_prefetch=1,
        grid=(nblk, kt),
        in_specs=[pl.BlockSpec((tk, D_MODEL), lambda i, kk, k_ref: (kk, 0)),
                  pl.BlockSpec((1, tk, 512), dz_index),
                  pl.BlockSpec((8, D_MODEL), const2),
                  pl.BlockSpec((32, D_MODEL), const2)],
        out_specs=(pl.BlockSpec(memory_space=pl.ANY),) + tuple(pl.BlockSpec(s, const2) for s in small_out),
        scratch_shapes=_rs_scratch(nj, nr, nc)
        + [pltpu.VMEM(s, F32) for s in small_buf] + [pltpu.VMEM((3,) + s, F32) for s in small_buf]
        + [pltpu.SemaphoreType.DMA((6,)), pltpu.SemaphoreType.DMA((6,))])
    return pl.pallas_call(
        body, name="grad_w_in",
        grid_spec=grid_spec,
        out_shape=(jax.ShapeDtypeStruct((nj, 2, nr, nc), F32),) + tuple(jax.ShapeDtypeStruct(s, F32) for s in small_out),
        compiler_params=_params(dimension_semantics=("arbitrary", "arbitrary")),
    )(kidx, h, dz6, sm2, dcw)


def _adamw_math(w, g, m, v):
    m = ADAM_B1 * m + (1.0 - ADAM_B1) * g
    v = ADAM_B2 * v + (1.0 - ADAM_B2) * (g * g)
    m_hat = m / (1.0 - ADAM_B1 ** ADAM_STEP)
    v_hat = v / (1.0 - ADAM_B2 ** ADAM_STEP)
    delta = -ADAM_LR * (m_hat / (jnp.sqrt(v_hat) + ADAM_EPS) + ADAM_WD * w)
    return delta, m, v


def _adamw_blocked(w, m, v, g4, name):
    nj, _, nr, nc = g4.shape

    def body(w_ref, m_ref, v_ref, g_ref, go_ref, d_ref, mo_ref, vo_ref):
        g = g_ref[0, 0]
        d, mn, vn = _adamw_math(w_ref[...], g, m_ref[...], v_ref[...])
        go_ref[...] = g
        d_ref[...] = d
        mo_ref[...] = mn
        vo_ref[...] = vn

    blk = pl.BlockSpec((nr, nc), lambda j, hf: (hf, j))
    return pl.pallas_call(
        body, name=name,
        grid=(nj, 2),
        out_shape=tuple(jax.ShapeDtypeStruct(w.shape, F32) for _ in range(4)),
        in_specs=[blk, blk, blk, pl.BlockSpec((1, 1, nr, nc), lambda j, hf: (j, hf, 0, 0))],
        out_specs=(blk, blk, blk, blk),
        compiler_params=_params(dimension_semantics=("arbitrary", "arbitrary")),
    )(w, m, v, g4)


def _adamw_w_ada(cact, pk1, pk2, w, m, v):
    rb = 256

    def body(cact_ref, pk1_ref, pk2_ref, w_ref, m_ref, v_ref, g_ref, d_ref, mo_ref, vo_ref, gb_ref, dmod, sel_scr):
        @pl.when(pl.program_id(0) == 0)
        def _():
            x, y, _ = _pos()
            k = 2 * x + y
            dmod[:, 0:D_MODEL] = pk2_ref[0:8, :]
            dmod[:, D_MODEL:2 * D_MODEL] = pk2_ref[8:16, :]
            dmod[:, 2 * D_MODEL:3 * D_MODEL] = pk1_ref[0:8, :]
            gb_ref[...] = jnp.sum(dmod[...], axis=0, keepdims=True)
            sel = jnp.zeros((8, 768), F32)
            for kk in range(N_CHIPS):
                sel = jnp.where(k == kk, dmod[:, kk * 768:(kk + 1) * 768], sel)
            sel_scr[...] = sel

        g = _dot_tn(cact_ref[...].astype(BF16), sel_scr[...].astype(BF16))
        d, mn, vn = _adamw_math(w_ref[...], g, m_ref[...], v_ref[...])
        g_ref[...] = g
        d_ref[...] = d
        mo_ref[...] = mn
        vo_ref[...] = vn

    blk = pl.BlockSpec((rb, 768), lambda i: (i, 0))
    const2 = lambda i: (0, 0)
    return pl.pallas_call(
        body, name="adamw_w_ada",
        grid=(D_MODEL // rb,),
        out_shape=tuple(jax.ShapeDtypeStruct(w.shape, F32) for _ in range(4)) + (
            jax.ShapeDtypeStruct((1, 3 * D_MODEL), F32),),
        in_specs=[pl.BlockSpec((8, rb), lambda i: (0, i)), pl.BlockSpec((16, D_MODEL), const2),
                  pl.BlockSpec((24, D_MODEL), const2), blk, blk, blk],
        out_specs=(blk, blk, blk, blk, pl.BlockSpec((1, 3 * D_MODEL), const2)),
        scratch_shapes=[pltpu.VMEM((8, 3 * D_MODEL), F32), pltpu.VMEM((8, 768), F32)],
        compiler_params=_params(dimension_semantics=("arbitrary",)),
    )(cact, pk1, pk2, w, m, v)


def _adamw_small(items):
    n = len(items)

    def body(*refs):
        ins, outs = refs[:4 * n], refs[4 * n:]
        for i in range(n):
            w_ref, g_ref, m_ref, v_ref = ins[4 * i:4 * i + 4]
            d, mn, vn = _adamw_math(w_ref[...], g_ref[...], m_ref[...], v_ref[...])
            outs[3 * i][...] = d
            outs[3 * i + 1][...] = mn
            outs[3 * i + 2][...] = vn

    flat = [a for it in items for a in it]
    outs = pl.pallas_call(
        body, name="adamw_small",
        out_shape=tuple(jax.ShapeDtypeStruct(it[0].shape, F32) for it in items for _ in range(3)),
        in_specs=[_vmem()] * (4 * n),
        out_specs=tuple(_vmem() for _ in range(3 * n)),
        compiler_params=_params(),
    )(*flat)
    return [tuple(outs[3 * i:3 * i + 3]) for i in range(n)]


def kernel(x, c, w_ada, b_ada, norm_g, w_in, conv_w, conv_b, conv_ln_g, conv_ln_b, sg_ln_g, sg_ln_b, w_s, b_s, w_out, final_g, loss_target, m_w_ada, m_b_ada, m_norm_g, m_w_in, m_conv_w, m_conv_b, m_conv_ln_g, m_conv_ln_b, m_sg_ln_g, m_sg_ln_b, m_w_s, m_b_s, m_w_out, m_final_g, v_w_ada, v_b_ada, v_norm_g, v_w_in, v_conv_w, v_conv_b, v_conv_ln_g, v_conv_ln_b, v_sg_ln_g, v_sg_ln_b, v_w_s, v_b_s, v_w_out, v_final_g):
    s_len = x.shape[1]
    x2d = x[0]
    tgt = loss_target[0]
    row = lambda a: a.reshape(1, -1)

    kidx = (2 * lax.axis_index("x") + lax.axis_index("y")).astype(jnp.int32).reshape(1)
    cw_sh = jnp.pad(conv_w.reshape(CONV_WIDTH, 256), ((0, 1), (0, 0)))
    h, z6, mod, cact, cw4, w_in12 = _front(kidx, x2d, c, w_ada[0], b_ada, norm_g, w_in[0], cw_sh)
    w12 = w_in12.reshape(12, D_MODEL, 512)
    q, p_cm, sig_cm, t2_cm, w_out4 = _conv_fwd(z6, cw4, conv_b, w_out[0])
    w_out_full = w_out4.reshape(2 * D_MODEL, D_MODEL)
    ln_rows = (conv_ln_g, conv_ln_b, sg_ln_g, sg_ln_b, row(final_g))
    bs_exp = jnp.repeat(b_s[0].T, CHUNK, axis=1)
    dz6, dq, dx2, ycat, dy, dws, sm1, dbs = _middle(x2d, z6, q, tgt, ln_rows, mod, w_s[0], bs_exp, w_out_full)
    dz6, pk1, dws_r, dbs_r, loss_t, g_w_out4 = _conv_bwd(
        kidx, dq, sig_cm, t2_cm, cw4, dz6, sm1, dws.reshape(D_MODEL, CHUNK), dbs, ycat, dy)
    grad_x, sm2, dcw = _bwd_in(dz6, w12, x2d, dx2, mod, norm_g, dq, p_cm)
    g_w_in4, pk2, dcw_sh, dcb = _grad_w_in(kidx, h, dz6, sm2, dcw)

    g_w_in, d_w_in, nm_w_in, nv_w_in = _adamw_blocked(w_in[0], m_w_in[0], v_w_in[0], g_w_in4, "adamw_w_in")
    g_w_out, d_w_out, nm_w_out, nv_w_out = _adamw_blocked(w_out[0], m_w_out[0], v_w_out[0], g_w_out4, "adamw_w_out")
    g_w_ada, d_w_ada, nm_w_ada, nv_w_ada, g_b_ada = _adamw_w_ada(cact, pk1, pk2, w_ada[0], m_w_ada[0], v_w_ada[0])

    g_norm_g = pk2[16:17]
    g_cln_g, g_cln_b, g_sln_g, g_sln_b, g_final = (pk1[8 + i:9 + i] for i in range(5))
    loss = loss_t[0, 0]
    g_conv_w = dcw_sh[:CONV_WIDTH]
    g_w_s = dws_r
    g_b_s = dbs_r[:, :HEADS].T
    small = [
        (b_ada, g_b_ada, m_b_ada, v_b_ada),
        (norm_g, g_norm_g, m_norm_g, v_norm_g),
        (conv_w.reshape(CONV_WIDTH, 256), g_conv_w, m_conv_w.reshape(CONV_WIDTH, 256), v_conv_w.reshape(CONV_WIDTH, 256)),
        (conv_b, dcb, m_conv_b, v_conv_b),
        (conv_ln_g, g_cln_g, m_conv_ln_g, v_conv_ln_g),
        (conv_ln_b, g_cln_b, m_conv_ln_b, v_conv_ln_b),
        (sg_ln_g, g_sln_g, m_sg_ln_g, v_sg_ln_g),
        (sg_ln_b, g_sln_b, m_sg_ln_b, v_sg_ln_b),
        (w_s.reshape(D_MODEL, CHUNK), g_w_s, m_w_s.reshape(D_MODEL, CHUNK), v_w_s.reshape(D_MODEL, CHUNK)),
        (b_s[0], g_b_s, m_b_s[0], v_b_s[0]),
        (row(final_g), g_final, row(m_final_g), row(v_final_g)),
    ]
    upd = _adamw_small(small)

    shapes = [w_ada.shape, b_ada.shape, norm_g.shape, w_in.shape, conv_w.shape, conv_b.shape, conv_ln_g.shape,
              conv_ln_b.shape, sg_ln_g.shape, sg_ln_b.shape, w_s.shape, b_s.shape, w_out.shape, final_g.shape]
    grads = [g_w_ada, g_b_ada, g_norm_g, g_w_in, g_conv_w, dcb, g_cln_g, g_cln_b, g_sln_g, g_sln_b, g_w_s, g_b_s,
             g_w_out, g_final]
    big = {0: (d_w_ada, nm_w_ada, nv_w_ada), 3: (d_w_in, nm_w_in, nv_w_in), 12: (d_w_out, nm_w_out, nv_w_out)}
    small_pos = [1, 2, 4, 5, 6, 7, 8, 9, 10, 11, 13]
    trip = [None] * 14
    for i, t in big.items():
        trip[i] = t
    for i, t in zip(small_pos, upd):
        trip[i] = t
    fit = lambda arrs: [a.reshape(s) for a, s in zip(arrs, shapes)]
    return (loss, grad_x.reshape(x.shape), *fit(grads), *fit([t[0] for t in trip]), *fit([t[1] for t in trip]),
            *fit([t[2] for t in trip]))
```

```python
import jax
import jax.numpy as jnp
from jax import lax
from jax.experimental import pallas as pl
from jax.experimental.pallas import tpu as pltpu

F32 = jnp.float32
BF16 = jnp.bfloat16
MESH = pl.DeviceIdType.MESH

D_MODEL = 1024
N_CHIPS = 4
HEADS = 8
CHUNK = 128
CONV_WIDTH = 31
CONV_HALF = CONV_WIDTH // 2
CONV_PAD = 16
EPS = 1e-6
ADAM_LR = 0.001
ADAM_B1 = 0.9
ADAM_B2 = 0.999
ADAM_EPS = 1e-08
ADAM_WD = 0.01
ADAM_STEP = 10

V7X_VMEM_BYTES = 64 * 1024 * 1024
VMEM_LIMIT = V7X_VMEM_BYTES - 8 * 1024 * 1024
ROWS = 16
UNROLL = 8
TOKEN_TILE = 256
FULL_UNROLL = TOKEN_TILE // ROWS
TIME_TILE = 128
K_TILE = 2048

N_GROUPS = 6


def _natural_group(j):
    return (j + 2) % N_GROUPS


def _pos():
    return lax.axis_index("x"), lax.axis_index("y"), lax.axis_index("c")


def _rcopy(src, dst, ssem, rsem, dev):
    return pltpu.make_async_remote_copy(src_ref=src, dst_ref=dst, send_sem=ssem, recv_sem=rsem,
                                        device_id=dev, device_id_type=MESH)


def _vmem():
    return pl.BlockSpec(memory_space=pltpu.VMEM)


def _params(**kw):
    return pltpu.CompilerParams(vmem_limit_bytes=VMEM_LIMIT, **kw)


def _sigmoid(v):
    return 0.5 * jnp.tanh(0.5 * v) + 0.5


def _row_loop(n_rows, body, unroll=1):
    def step(r, carry):
        body(pl.ds(pl.multiple_of(r * ROWS, ROWS), ROWS))
        return carry
    lax.fori_loop(0, n_rows // ROWS, step, 0, unroll=unroll)


def _colsum8(v):
    return v.reshape(v.shape[0] // 8, 8, v.shape[1]).sum(axis=0)


def _mean(v):
    return jnp.mean(v, axis=-1, keepdims=True)


def _dot_nn(a, b):
    return jnp.dot(a, b, preferred_element_type=F32)


def _dot_nt(a, b):
    return lax.dot_general(a, b, (((1,), (1,)), ((), ())), preferred_element_type=F32)


def _dot_tn(a, b):
    return lax.dot_general(a, b, (((0,), (0,)), ((), ())), preferred_element_type=F32)


def _remote_chip(k, r):
    return jnp.bitwise_xor(k, r + 1)


def _front(kidx, x, c, w_ada, b_ada, norm_g, w_in, cw):
    s_len = x.shape[0]
    tmh = min(512, s_len)
    tmz = min(1024, s_len)
    nh = s_len // tmh
    nz = s_len // tmz
    n_steps = nh + 12 * nz

    def remote_block(q):
        return jnp.where(q < 6, q % 2, 2), jnp.where(q < 6, q // 2, q - 6)

    def block_of(i, k):
        r, j = remote_block(jnp.maximum(i - 3, 0))
        return jnp.where(i < 3, 3 * k + i, 3 * _remote_chip(k, r) + j)

    def body(k_ref, x_ref, c_ref, bada_ref, g_ref, cw_ref, wada_hbm, win_hbm,
             h_ref, z_ref, mod_ref, cact_ref, cw4_ref, w12_hbm,
             h_all, wbuf, stage, wada_v, cslab, mslab, cw4_s,
             lsem, csend, crecv, msend, mrecv, wsend, wrecv, isend, irecv, fsend, frecv, osem):
        del k_ref
        t = pl.program_id(0)
        x_, y_, c_ = _pos()
        k = 2 * x_ + y_
        b = 4 * x_ + 2 * y_ + c_
        sib = (x_, y_, 1 - c_)

        def dev_of(r):
            kk = _remote_chip(k, r)
            return (kk // 2, kk % 2, c_)

        def ici(q, kk):
            r, j = remote_block(q)
            return _rcopy(wbuf.at[3 * kk + j, c_], wbuf.at[3 * kk + j, c_], isend.at[q], irecv.at[q], dev_of(r))

        def fwd(q, hf):
            r, j = remote_block(q)
            blk = 3 * _remote_chip(k, r) + j
            return _rcopy(wbuf.at[blk, hf], wbuf.at[blk, hf], fsend.at[q], frecv.at[q], sib)

        def c_copy(q, src):
            d = jnp.bitwise_xor(b, q)
            return _rcopy(cslab.at[src], cslab.at[src], csend.at[q - 1], crecv.at[q - 1], (d // 4, (d // 2) % 2, d % 2))

        def m_copy(r, kk):
            return _rcopy(mslab.at[kk], mslab.at[kk], msend.at[r], mrecv.at[r], dev_of(r))

        def cw_copy(r, kk):
            return _rcopy(cw4_s.at[kk], cw4_s.at[kk], wsend.at[r], wrecv.at[r], dev_of(r))

        def to_hbm(i):
            m = block_of(i, k)
            return pltpu.make_async_copy(wbuf.at[m], w12_hbm.at[m], osem.at[i])

        @pl.when(t == 0)
        def _():
            ld_w = pltpu.make_async_copy(win_hbm, stage, lsem.at[0])
            ld_w.start()
            ld_a = pltpu.make_async_copy(wada_hbm, wada_v, lsem.at[1])
            ld_a.start()
            cslab[b] = jnp.broadcast_to(c_ref[...], (8, D_MODEL))
            for q in range(1, 8):
                c_copy(q, b).start()
            cw4_s[k] = cw_ref[...]
            for r in range(3):
                cw_copy(r, k).start()
            ld_w.wait()
            for j in range(3):
                for hf in range(2):
                    wbuf[3 * k + j, hf] = stage[hf * 512:(hf + 1) * 512, j * 512:(j + 1) * 512].astype(BF16)
            for i in range(3):
                to_hbm(i).start()
            for q in range(1, 8):
                c_copy(q, jnp.bitwise_xor(b, q)).wait_recv()
            row = lax.broadcasted_iota(jnp.int32, (8, D_MODEL), 0)
            call = jnp.zeros((8, D_MODEL), F32)
            for d in range(8):
                call = jnp.where(row == d, cslab[d], call)
            cact = call * _sigmoid(call)
            cact_ref[...] = cact
            ld_a.wait()
            mslab[k] = _dot_nn(cact.astype(BF16), wada_v[...].astype(BF16))
            for r in range(3):
                m_copy(r, k).start()
            for q in range(9):
                ici(q, k).start()
            for r in range(3):
                m_copy(r, _remote_chip(k, r)).wait_recv()
            row8 = lax.broadcasted_iota(jnp.int32, (8, 768), 0)
            for kk in range(N_CHIPS):
                piece = jnp.sum(jnp.where(row8 == b, mslab[kk], 0.0), axis=0, keepdims=True)
                mod_ref[:, kk * 768:(kk + 1) * 768] = piece + bada_ref[:, kk * 768:(kk + 1) * 768]
            for r in range(3):
                cw_copy(r, _remote_chip(k, r)).wait_recv()
            cw4_ref[...] = cw4_s[...]

        @pl.when(t < nh)
        def _():
            shift = mod_ref[:, 0:D_MODEL]
            scale1 = 1.0 + mod_ref[:, D_MODEL:2 * D_MODEL]
            g = g_ref[...]
            base = t * tmh

            def rows_fn(rows):
                xt = x_ref[rows, :]
                r = lax.rsqrt(_mean(xt * xt) + EPS)
                hv = ((xt * r * g) * scale1 + shift).astype(BF16)
                h_ref[rows, :] = hv
                h_all[pl.ds(pl.multiple_of(base + rows.start, ROWS), ROWS), :] = hv
            _row_loop(tmh, rows_fn, unroll=UNROLL)

        @pl.when(t >= nh)
        def _():
            u = t - nh
            i = u // nz
            rt = u % nz
            @pl.when(jnp.logical_and(rt == 0, jnp.logical_and(i >= 2, i <= 10)))
            def _():
                q = i - 2
                r, _ = remote_block(q)
                ici(q, _remote_chip(k, r)).wait_recv()
                fwd(q, c_).start()

            @pl.when(jnp.logical_and(rt == 0, i >= 3))
            def _():
                fwd(i - 3, 1 - c_).wait_recv()
                to_hbm(i).start()
            m = block_of(i, k)
            hb = h_all[pl.ds(pl.multiple_of(rt * tmz, tmz), tmz), :]
            z_ref[0] = _dot_nn(hb, wbuf[m].reshape(D_MODEL, 512))

        @pl.when(t == n_steps - 1)
        def _():
            for q in range(1, 8):
                c_copy(q, b).wait_send()
            for r in range(3):
                m_copy(r, k).wait_send()
                cw_copy(r, k).wait_send()
            for q in range(9):
                ici(q, k).wait_send()
                fwd(q, c_).wait_send()
            for i in range(12):
                to_hbm(i).wait()

    def z_index(t, k_ref):
        u = jnp.maximum(t - nh, 0)
        m = block_of(u // nz, k_ref[0])
        return ((m // 2 + 4) % N_GROUPS, u % nz, m % 2)

    tok = lambda t, k_ref: (jnp.minimum(t, nh - 1), 0)
    const2 = lambda t, k_ref: (0, 0)
    grid_spec = pltpu.PrefetchScalarGridSpec(
        num_scalar_prefetch=1,
        grid=(n_steps,),
        in_specs=[pl.BlockSpec((tmh, D_MODEL), tok),
                  pl.BlockSpec((1, D_MODEL), const2),
                  pl.BlockSpec((1, 3 * D_MODEL), const2),
                  pl.BlockSpec((1, D_MODEL), const2),
                  pl.BlockSpec((32, 256), const2),
                  pl.BlockSpec(memory_space=pl.ANY),
                  pl.BlockSpec(memory_space=pl.ANY)],
        out_specs=(pl.BlockSpec((tmh, D_MODEL), tok),
                   pl.BlockSpec((1, tmz, 512), z_index),
                   pl.BlockSpec((1, 3 * D_MODEL), const2),
                   pl.BlockSpec((8, D_MODEL), const2),
                   pl.BlockSpec((N_CHIPS, 32, 256), lambda t, k_ref: (0, 0, 0)),
                   pl.BlockSpec(memory_space=pl.ANY)),
        scratch_shapes=[pltpu.VMEM((s_len, D_MODEL), BF16),
                        pltpu.VMEM((12, 2, 512, 512), BF16),
                        pltpu.VMEM((D_MODEL, 1536), F32),
                        pltpu.VMEM((D_MODEL, 768), F32),
                        pltpu.VMEM((8, 8, D_MODEL), F32),
                        pltpu.VMEM((N_CHIPS, 8, 768), F32),
                        pltpu.VMEM((N_CHIPS, 32, 256), F32),
                        pltpu.SemaphoreType.DMA((2,)),
                        pltpu.SemaphoreType.DMA((7,)), pltpu.SemaphoreType.DMA((7,)),
                        pltpu.SemaphoreType.DMA((3,)), pltpu.SemaphoreType.DMA((3,)),
                        pltpu.SemaphoreType.DMA((3,)), pltpu.SemaphoreType.DMA((3,)),
                        pltpu.SemaphoreType.DMA((9,)), pltpu.SemaphoreType.DMA((9,)),
                        pltpu.SemaphoreType.DMA((9,)), pltpu.SemaphoreType.DMA((9,)),
                        pltpu.SemaphoreType.DMA((12,))])
    return pl.pallas_call(
        body, name="front",
        grid_spec=grid_spec,
        out_shape=(jax.ShapeDtypeStruct((s_len, D_MODEL), BF16),
                   jax.ShapeDtypeStruct((N_GROUPS, s_len, D_MODEL), F32),
                   jax.ShapeDtypeStruct((1, 3 * D_MODEL), F32),
                   jax.ShapeDtypeStruct((8, D_MODEL), F32),
                   jax.ShapeDtypeStruct((N_CHIPS, 32, 256), F32),
                   jax.ShapeDtypeStruct((12, 2, 512, 512), BF16)),
        compiler_params=_params(dimension_semantics=("arbitrary",)),
    )(kidx, x, c, b_ada, norm_g, cw, w_ada, w_in)


def _all_reduce_partners():
    x, y, c = _pos()
    return [(x, y, 1 - c), (x, 1 - y, c), (1 - x, y, c)]


def _butterfly_event(e, bufs, recvs, ssem, rsem, partners):
    n = len(bufs)

    def copies(s):
        return [_rcopy(buf, recv.at[s], ssem.at[s * n + i], rsem.at[s * n + i], partners[s])
                for i, (buf, recv) in enumerate(zip(bufs, recvs))]

    if e > 0:
        for cp in copies(e - 1):
            cp.wait()
        for buf, recv in zip(bufs, recvs):
            buf[...] = buf[...] + recv[e - 1]
    if e < len(partners):
        for cp in copies(e):
            cp.start()


def _onehot_rows(v, b):
    row = lax.broadcasted_iota(jnp.int32, (8, v.shape[1]), 0)
    return jnp.where(row == b, jnp.broadcast_to(v, (8, v.shape[1])), 0.0)


def _conv_fwd(z6, cw4, conv_b, w_out):
    s_len = z6.shape[1]
    tt = TIME_TILE
    n_blocks = D_MODEL // 128

    def body(z_ref, cw_ref, cb_ref, wout_hbm, q_ref, p_ref, sig_ref, t2_ref, wout4_hbm,
             ppad, stage, wbuf, lsem, isend, irecv, fsend, frecv, osem):
        jb = pl.program_id(0)
        x_, y_, c_ = _pos()
        k = 2 * x_ + y_
        sib = (x_, y_, 1 - c_)

        def ici(r, kk):
            rk = _remote_chip(k, r)
            return _rcopy(wbuf.at[kk, c_], wbuf.at[kk, c_], isend.at[r], irecv.at[r], (rk // 2, rk % 2, c_))

        def fwd(r, hf):
            kk = _remote_chip(k, r)
            return _rcopy(wbuf.at[kk, hf], wbuf.at[kk, hf], fsend.at[r], frecv.at[r], sib)

        @pl.when(jb == 0)
        def _():
            ld = pltpu.make_async_copy(wout_hbm, stage, lsem)
            ld.start()
            ld.wait()
            for hf in range(2):
                wbuf[k, hf] = stage[hf * 256:(hf + 1) * 256, :].astype(BF16)
            for r in range(3):
                ici(r, k).start()

        @pl.when(jb == (5 * n_blocks) // 8)
        def _():
            for r in range(3):
                ici(r, _remote_chip(k, r)).wait_recv()
                fwd(r, c_).start()

        zero = jnp.zeros((CONV_PAD, 128), F32)
        ppad[0:CONV_PAD, :] = zero
        ppad[s_len + CONV_PAD:s_len + 2 * CONV_PAD, :] = zero

        def fill(i, carry):
            t0 = pl.multiple_of(i * tt, tt)
            sig = _sigmoid(z_ref[1, pl.ds(t0, tt), :])
            p = z_ref[0, pl.ds(t0, tt), :] * sig
            ppad[pl.ds(CONV_PAD + t0, tt), :] = p
            p_ref[0, pl.ds(t0, tt), :] = p
            sig_ref[0, pl.ds(t0, tt), :] = sig
            t2_ref[0, pl.ds(t0, tt), :] = p * (1.0 - sig)
            return carry
        lax.fori_loop(0, s_len // tt, fill, 0)
        w = cw_ref[0]
        bias = cb_ref[...]

        def conv(i, carry):
            t0 = pl.multiple_of(i * tt, tt)
            acc = jnp.broadcast_to(bias, (tt, 128))
            for o in range(1, CONV_WIDTH + 1):
                acc = acc + w[o - 1:o, :] * ppad[pl.ds(t0 + o, tt), :]
            q_ref[0, pl.ds(t0, tt), :] = acc
            return carry
        lax.fori_loop(0, s_len // tt, conv, 0)

        @pl.when(jb == n_blocks - 1)
        def _():
            for r in range(3):
                fwd(r, 1 - c_).wait_recv()
            out = pltpu.make_async_copy(wbuf, wout4_hbm, osem)
            out.start()
            for r in range(3):
                ici(r, k).wait_send()
                fwd(r, c_).wait_send()
            out.wait()

    return pl.pallas_call(
        body, name="conv_fwd",
        grid=(n_blocks,),
        out_shape=tuple(jax.ShapeDtypeStruct((n_blocks, s_len, 128), F32) for _ in range(4))
        + (jax.ShapeDtypeStruct((N_CHIPS, 2, 256, D_MODEL), BF16),),
        in_specs=[pl.BlockSpec((2, s_len, 128), lambda j: (2, 0, j)),
                  pl.BlockSpec((1, 32, 128), lambda j: (j // 2, 0, j % 2)),
                  pl.BlockSpec((1, 128), lambda j: (0, j)),
                  pl.BlockSpec(memory_space=pl.ANY)],
        out_specs=tuple(pl.BlockSpec((1, s_len, 128), lambda j: (j, 0, 0)) for _ in range(4))
        + (pl.BlockSpec(memory_space=pl.ANY),),
        scratch_shapes=[pltpu.VMEM((s_len + 2 * CONV_PAD, 128), F32),
                        pltpu.VMEM((512, D_MODEL), F32),
                        pltpu.VMEM((N_CHIPS, 2, 256, D_MODEL), BF16),
                        pltpu.SemaphoreType.DMA,
                        pltpu.SemaphoreType.DMA((3,)), pltpu.SemaphoreType.DMA((3,)),
                        pltpu.SemaphoreType.DMA((3,)), pltpu.SemaphoreType.DMA((3,)),
                        pltpu.SemaphoreType.DMA],
        compiler_params=_params(dimension_semantics=("arbitrary",)),
    )(z6, cw4, conv_b, w_out)


def _middle(x, z6, q, target, ln_rows, mod, w_s, bs_exp, w_out):
    s_len = x.shape[0]
    tm = TOKEN_TILE
    n_steps = s_len // tm
    n_chunks = tm // CHUNK
    inv_d = 1.0 / D_MODEL

    def body(x_ref, z_ref, q_ref, tgt_ref, cg_ref, cb_ref, sg_ref, sb_ref, fg_ref, mod_ref, ws_ref, bs_ref, wout_ref,
             dz_ref, dq_ref, dx2_ref, ycat_ref, dy_ref, dws_ref, sm_ref, dbs_ref,
             vl_scr, vm_scr, y_scr, dycat_scr, dvm_scr, dvl_scr, acc_scr, dbs_acc, keep, rstd_scr):
        i = pl.program_id(0)

        @pl.when(i == 0)
        def _():
            acc_scr[...] = jnp.zeros_like(acc_scr)
            dbs_acc[...] = jnp.zeros_like(dbs_acc)
            dws_ref[...] = jnp.zeros_like(dws_ref)

        cg, cb, sg, sb, fg = cg_ref[...], cb_ref[...], sg_ref[...], sb_ref[...], fg_ref[...]
        gm = mod_ref[:, 2 * D_MODEL:3 * D_MODEL]

        def norm_stats(t):
            c = t - _mean(t)
            rstd = lax.rsqrt(_mean(c * c) + EPS)
            return c * rstd, rstd

        def phase1(rows):
            qhat, rstd_q = norm_stats(jnp.concatenate([q_ref[blk, rows, :] for blk in range(D_MODEL // 128)], axis=1))
            ln = qhat * cg + cb
            gz = z_ref[0, rows, :]
            sig_ln = _sigmoid(ln)
            sig_g = _sigmoid(gz)
            ycat_ref[rows, 0:D_MODEL] = ((ln * sig_ln) * (gz * sig_g)).astype(BF16)
            vhat, rstd_v = norm_stats(z_ref[2, rows, :])
            vl_scr[rows, :] = (vhat * sg + sb).astype(BF16)
            keep[0, rows, :] = qhat
            keep[1, rows, :] = vhat
            keep[2, rows, :] = sig_ln
            keep[3, rows, :] = sig_g
            rstd_scr[0, rows, :] = rstd_q
            rstd_scr[1, rows, :] = rstd_v
        _row_loop(tm, phase1, unroll=FULL_UNROLL)

        for ch in range(n_chunks):
            r0 = ch * CHUNK
            for h in range(HEADS):
                c0 = h * CHUNK
                vm_scr[r0:r0 + CHUNK, c0:c0 + CHUNK] = (
                    _dot_nn(ws_ref[h].astype(BF16), vl_scr[r0:r0 + CHUNK, c0:c0 + CHUNK]) + bs_ref[:, c0:c0 + CHUNK])

        def phase3(rows):
            bg = z_ref[3, rows, :]
            sig_b = _sigmoid(bg)
            keep[4, rows, :] = sig_b
            ycat_ref[rows, D_MODEL:2 * D_MODEL] = (z_ref[1, rows, :] * vm_scr[rows, :] * (bg * sig_b)).astype(BF16)
        _row_loop(tm, phase3, unroll=FULL_UNROLL)

        y_scr[...] = _dot_nn(ycat_ref[...], wout_ref[...])

        def phase5(rows):
            y = y_scr[rows, :]
            x2 = x_ref[rows, :] + gm * y
            r2 = lax.rsqrt(_mean(x2 * x2) + EPS)
            xn2 = x2 * r2
            diff = xn2 * fg - tgt_ref[rows, :]
            acc_scr[6] += _colsum8(diff * diff)
            dout = diff * inv_d
            acc_scr[0] += _colsum8(dout * xn2)
            dxn = dout * fg
            dx2 = r2 * (dxn - xn2 * _mean(dxn * xn2))
            dx2_ref[rows, :] = dx2
            acc_scr[1] += _colsum8(dx2 * y)
            dy_ref[rows, :] = (dx2 * gm).astype(BF16)
        _row_loop(tm, phase5, unroll=FULL_UNROLL)

        dycat_scr[...] = _dot_nt(dy_ref[...], wout_ref[...])

        def phase7(rows):
            dyb = dycat_scr[rows, D_MODEL:2 * D_MODEL]
            u = z_ref[1, rows, :]
            bg = z_ref[3, rows, :]
            vm = vm_scr[rows, :]
            sig = keep[4, rows, :]
            silu = bg * sig
            dz_ref[1, rows, :] = (dyb * vm * silu).astype(BF16)
            dvm = dyb * u * silu
            dz_ref[3, rows, :] = (dyb * u * vm * (sig * (1.0 + bg * (1.0 - sig)))).astype(BF16)
            dvm_scr[rows, :] = dvm.astype(BF16)
            pos = pl.ds(pl.multiple_of(rows.start % CHUNK, ROWS), ROWS)
            dbs_acc[pos, :] += dvm
        _row_loop(tm, phase7, unroll=FULL_UNROLL)

        for ch in range(n_chunks):
            r0 = ch * CHUNK
            for h in range(HEADS):
                c0 = h * CHUNK
                dvm_b = dvm_scr[r0:r0 + CHUNK, c0:c0 + CHUNK]
                dws_ref[h] += _dot_nt(dvm_b, vl_scr[r0:r0 + CHUNK, c0:c0 + CHUNK])
                dvl_scr[r0:r0 + CHUNK, c0:c0 + CHUNK] = _dot_tn(ws_ref[h].astype(BF16), dvm_b)

        def phase9(rows):
            vhat, rstd_v = keep[1, rows, :], rstd_scr[1, rows, :]
            dvl = dvl_scr[rows, :]
            acc_scr[4] += _colsum8(dvl * vhat)
            acc_scr[5] += _colsum8(dvl)
            dvh = dvl * sg
            dz_ref[2, rows, :] = (rstd_v * (dvh - _mean(dvh) - vhat * _mean(dvh * vhat))).astype(BF16)
            qhat, rstd_q = keep[0, rows, :], rstd_scr[0, rows, :]
            ln = qhat * cg + cb
            sig_ln = keep[2, rows, :]
            gz = z_ref[0, rows, :]
            sig_g = keep[3, rows, :]
            dya = dycat_scr[rows, 0:D_MODEL]
            dz_ref[0, rows, :] = (dya * (ln * sig_ln) * (sig_g * (1.0 + gz * (1.0 - sig_g)))).astype(BF16)
            dln = (dya * (gz * sig_g)) * (sig_ln * (1.0 + ln * (1.0 - sig_ln)))
            acc_scr[2] += _colsum8(dln * qhat)
            acc_scr[3] += _colsum8(dln)
            dqh = dln * cg
            dq = rstd_q * (dqh - _mean(dqh) - qhat * _mean(dqh * qhat))
            for blk in range(D_MODEL // 128):
                dq_ref[blk, rows, :] = dq[:, blk * 128:(blk + 1) * 128]
        _row_loop(tm, phase9, unroll=FULL_UNROLL)

        @pl.when(i == n_steps - 1)
        def _():
            for qi in range(8):
                scale = 0.5 * inv_d if qi == 6 else 1.0
                sm_ref[qi:qi + 1, :] = jnp.sum(acc_scr[qi], axis=0, keepdims=True) * scale
            lane = lax.broadcasted_iota(jnp.int32, (CHUNK, CHUNK), 1)
            tile = jnp.zeros((CHUNK, CHUNK), F32)
            for h in range(HEADS):
                col = jnp.sum(dbs_acc[:, h * CHUNK:(h + 1) * CHUNK], axis=1, keepdims=True)
                tile = jnp.where(lane == h, col, tile)
            dbs_ref[...] = tile

    tok = lambda i: (i, 0)
    const2 = lambda i: (0, 0)
    return pl.pallas_call(
        body, name="middle",
        grid=(n_steps,),
        out_shape=(jax.ShapeDtypeStruct((4, s_len, D_MODEL), BF16),
                   jax.ShapeDtypeStruct((D_MODEL // 128, s_len, 128), F32),
                   jax.ShapeDtypeStruct((s_len, D_MODEL), F32),
                   jax.ShapeDtypeStruct((s_len, 2 * D_MODEL), BF16),
                   jax.ShapeDtypeStruct((s_len, D_MODEL), BF16),
                   jax.ShapeDtypeStruct((HEADS, CHUNK, CHUNK), F32),
                   jax.ShapeDtypeStruct((8, D_MODEL), F32),
                   jax.ShapeDtypeStruct((CHUNK, CHUNK), F32)),
        in_specs=[pl.BlockSpec((tm, D_MODEL), tok),
                  pl.BlockSpec((4, tm, D_MODEL), lambda i: (0, i, 0)),
                  pl.BlockSpec((D_MODEL // 128, tm, 128), lambda i: (0, i, 0)),
                  pl.BlockSpec((tm, D_MODEL), tok),
                  *[pl.BlockSpec((1, D_MODEL), const2) for _ in range(5)],
                  pl.BlockSpec((1, 3 * D_MODEL), const2),
                  pl.BlockSpec((HEADS, CHUNK, CHUNK), lambda i: (0, 0, 0)),
                  pl.BlockSpec((CHUNK, D_MODEL), const2),
                  pl.BlockSpec((2 * D_MODEL, D_MODEL), const2, pipeline_mode=pl.Buffered(1))],
        out_specs=(pl.BlockSpec((4, tm, D_MODEL), lambda i: (0, i, 0)),
                   pl.BlockSpec((D_MODEL // 128, tm, 128), lambda i: (0, i, 0)),
                   pl.BlockSpec((tm, D_MODEL), tok),
                   pl.BlockSpec((tm, 2 * D_MODEL), tok),
                   pl.BlockSpec((tm, D_MODEL), tok),
                   pl.BlockSpec((HEADS, CHUNK, CHUNK), lambda i: (0, 0, 0)),
                   pl.BlockSpec((8, D_MODEL), const2),
                   pl.BlockSpec((CHUNK, CHUNK), const2)),
        scratch_shapes=[pltpu.VMEM((tm, D_MODEL), BF16),
                        pltpu.VMEM((tm, D_MODEL), F32),
                        pltpu.VMEM((tm, D_MODEL), F32),
                        pltpu.VMEM((tm, 2 * D_MODEL), F32),
                        pltpu.VMEM((tm, D_MODEL), BF16),
                        pltpu.VMEM((tm, D_MODEL), F32),
                        pltpu.VMEM((8, 8, D_MODEL), F32),
                        pltpu.VMEM((CHUNK, D_MODEL), F32),
                        pltpu.VMEM((5, tm, D_MODEL), F32),
                        pltpu.VMEM((2, tm, 1), F32)],
        compiler_params=_params(dimension_semantics=("arbitrary",)),
    )(x, z6, q, target, *ln_rows, mod, w_s, bs_exp, w_out)


def _conv_bwd(kidx, dq, sig, t2, cw4, sm1, dws, dbs, ycat, dy):
    s_len = dq.shape[1]
    tt = TIME_TILE
    n_blocks = D_MODEL // 128
    shp = [(16, D_MODEL), (D_MODEL, 128), (128, 128)]
    nj, nr, nc = 1, 256, D_MODEL
    kt = n_blocks // N_CHIPS
    tk = s_len // kt

    def body(k_ref, dq_ref, sig_ref, t2_ref, cw_ref, sm1_ref, dws_in, dbs_in, ycat_ref, dy_ref,
             dz_ref, pk1_ref, dws_ref, dbs_ref, loss_ref, gout_ref,
             dqpad, b_pk, b_dws, b_dbs, r_pk, r_dws, r_dbs, ssem, rsem, *rs_scr):
        del k_ref
        jb = pl.program_id(0)
        x_, y_, c_ = _pos()

        @pl.when(jb == 0)
        def _():
            b_pk[0:8, :] = _onehot_rows(sm1_ref[1:2, :], 4 * x_ + 2 * y_ + c_)
            for r, src in enumerate([2, 3, 4, 5, 0, 6]):
                b_pk[8 + r:9 + r, :] = sm1_ref[src:src + 1, :]
            b_pk[14:16, :] = jnp.zeros((2, D_MODEL), F32)
            b_dws[...] = dws_in[...]
            b_dbs[...] = dbs_in[...]

        for e in range(4):
            @pl.when(jb == (e * n_blocks) // 4)
            def _():
                _butterfly_event(e, [b_pk, b_dws, b_dbs], [r_pk, r_dws, r_dbs], ssem, rsem, _all_reduce_partners())

        @pl.when(jb == (3 * n_blocks) // 4)
        def _():
            pk1_ref[...] = b_pk[...]
            dws_ref[...] = b_dws[...]
            dbs_ref[...] = b_dbs[...]
            loss_ref[...] = jnp.broadcast_to(jnp.sum(b_pk[13:14, :], axis=1, keepdims=True), (8, 128))

        zero = jnp.zeros((CONV_PAD, 128), F32)
        dqpad[0:CONV_PAD, :] = zero
        dqpad[s_len + CONV_PAD:s_len + 2 * CONV_PAD, :] = zero
        dqpad[CONV_PAD:s_len + CONV_PAD, :] = dq_ref[0]
        w = cw_ref[0]

        def bwd(i, carry):
            t0 = i * tt
            dp = jnp.zeros((tt, 128), F32)
            for o in range(1, CONV_WIDTH + 1):
                dp = dp + w[CONV_WIDTH - o:CONV_WIDTH - o + 1, :] * dqpad[pl.ds(t0 + o, tt), :]
            dz_ref[0, 0, pl.ds(t0, tt), :] = (dp * sig_ref[0, pl.ds(t0, tt), :]).astype(BF16)
            dz_ref[0, 1, pl.ds(t0, tt), :] = (dp * t2_ref[0, pl.ds(t0, tt), :]).astype(BF16)
            return carry

        n_iter = s_len // tt
        n_groups = min(8, n_iter)
        rows = tk // n_groups

        def conv_and_matmul(add):
            for g in range(n_groups):
                for it in range(g * n_iter // n_groups, (g + 1) * n_iter // n_groups):
                    bwd(it, 0)
                add(_dot_tn(ycat_ref[g * rows:(g + 1) * rows, :], dy_ref[g * rows:(g + 1) * rows, :]))
        _rs_step(jb // kt, jb % kt, kt, nj, nr, nc, conv_and_matmul, gout_ref, rs_scr)

    const2 = lambda j, k_ref: (0, 0)
    grid_spec = pltpu.PrefetchScalarGridSpec(
        num_scalar_prefetch=1,
        grid=(n_blocks,),
        in_specs=[*[pl.BlockSpec((1, s_len, 128), lambda j, k_ref: (j, 0, 0)) for _ in range(3)],
                  pl.BlockSpec((1, 32, 128), lambda j, k_ref: (j // 2, 0, j % 2)),
                  pl.BlockSpec((8, D_MODEL), const2),
                  pl.BlockSpec((D_MODEL, 128), const2),
                  pl.BlockSpec((128, 128), const2),
                  pl.BlockSpec((tk, 512), lambda j, k_ref: (j % kt, _rs_block(j // kt, k_ref[0], nj))),
                  pl.BlockSpec((tk, D_MODEL), lambda j, k_ref: (j % kt, 0))],
        out_specs=(pl.BlockSpec((1, 2, s_len, 128), lambda j, k_ref: (j, 0, 0, 0)),)
        + tuple(pl.BlockSpec(s, const2) for s in shp) + (pl.BlockSpec((8, 128), const2),
                                                         pl.BlockSpec(memory_space=pl.ANY)),
        scratch_shapes=[pltpu.VMEM((s_len + 2 * CONV_PAD, 128), F32)]
        + [pltpu.VMEM(s, F32) for s in shp] + [pltpu.VMEM((3,) + s, F32) for s in shp]
        + [pltpu.SemaphoreType.DMA((9,)), pltpu.SemaphoreType.DMA((9,))] + _rs_scratch(nj, nr, nc))
    return pl.pallas_call(
        body, name="conv_bwd",
        grid_spec=grid_spec,
        out_shape=(jax.ShapeDtypeStruct((n_blocks, 2, s_len, 128), BF16),)
        + tuple(jax.ShapeDtypeStruct(s, F32) for s in shp) + (jax.ShapeDtypeStruct((8, 128), F32),
                                                              jax.ShapeDtypeStruct((nj, 2, nr, nc), F32)),
        compiler_params=_params(dimension_semantics=("arbitrary",)),
    )(kidx, dq, sig, t2, cw4, sm1, dws, dbs, ycat, dy)


def _bwd_in(dz4, dzc, w12, x, dx2, mod, norm_g, dq, p):
    s_len = x.shape[0]
    tm = TOKEN_TILE
    n_steps = s_len // tm
    tt = TIME_TILE
    n_cblocks = D_MODEL // 128
    parts = max(1, n_steps // n_cblocks)
    nsub = max(1, n_cblocks // n_steps)
    cw = 128 * nsub
    tiles = (s_len // tt) // parts

    def body(dz_ref, dzc_ref, w_ref, x_ref, dx2_ref, mod_ref, g_ref, dq_ref, p_ref, gx_ref, sm_ref, dcw_ref,
             dh_scr, acc_scr, ppad, wacc):
        i = pl.program_id(0)
        part = i % parts

        @pl.when(i == 0)
        def _():
            acc_scr[...] = jnp.zeros_like(acc_scr)

        @pl.when(part == 0)
        def _():
            zero = jnp.zeros((CONV_PAD, 128), F32)
            for sub in range(nsub):
                ppad[sub, 0:CONV_PAD, :] = zero
                ppad[sub, s_len + CONV_PAD:s_len + 2 * CONV_PAD, :] = zero
                ppad[sub, CONV_PAD:s_len + CONV_PAD, :] = p_ref[sub]
            wacc[...] = jnp.zeros_like(wacc)

        def dw_tile(tile):
            t0 = pl.multiple_of((part * tiles + tile) * tt, tt)
            for sub in range(nsub):
                dqt = dq_ref[sub, pl.ds(t0, tt), :]
                for o in range(1, CONV_WIDTH + 1):
                    wacc[sub, o - 1] += _colsum8(dqt * ppad[sub, pl.ds(t0 + o, tt), :])
                wacc[sub, CONV_WIDTH] += _colsum8(dqt)

        def dz_cols(j, hf):
            if j < 4:
                return dz_ref[j, :, hf * 512:(hf + 1) * 512]
            return jnp.concatenate([dzc_ref[4 * hf + blk, j - 4] for blk in range(4)], axis=1)

        dots = [(j, hf) for j in range(N_GROUPS) for hf in range(2)]
        dh = jnp.zeros((tm, D_MODEL), F32)
        for d, (j, hf) in enumerate(dots):
            dh = dh + _dot_nt(dz_cols(j, hf), w_ref[2 * _natural_group(j) + hf])
            for tile in range(d * tiles // len(dots), (d + 1) * tiles // len(dots)):
                dw_tile(tile)
        dh_scr[...] = dh

        @pl.when(part == parts - 1)
        def _():
            for sub in range(nsub):
                for k in range(32):
                    dcw_ref[k:k + 1, sub * 128:(sub + 1) * 128] = jnp.sum(wacc[sub, k], axis=0, keepdims=True)

        scale1 = 1.0 + mod_ref[:, D_MODEL:2 * D_MODEL]
        g = g_ref[...]

        def rows_fn(rows):
            xt = x_ref[rows, :]
            r = lax.rsqrt(_mean(xt * xt) + EPS)
            xn = xt * r
            dhr = dh_scr[rows, :]
            acc_scr[0] += _colsum8(dhr)
            acc_scr[1] += _colsum8(dhr * (xn * g))
            acc_scr[2] += _colsum8(dhr * scale1 * xn)
            dxn = dhr * (g * scale1)
            gx_ref[rows, :] = dx2_ref[rows, :] + r * (dxn - xn * _mean(dxn * xn))
        _row_loop(tm, rows_fn, unroll=FULL_UNROLL)

        @pl.when(i == n_steps - 1)
        def _():
            for qi in range(8):
                sm_ref[qi:qi + 1, :] = jnp.sum(acc_scr[qi], axis=0, keepdims=True)

    tok = lambda i: (i, 0)
    const2 = lambda i: (0, 0)
    return pl.pallas_call(
        body, name="bwd_in",
        grid=(n_steps,),
        out_shape=(jax.ShapeDtypeStruct((s_len, D_MODEL), F32), jax.ShapeDtypeStruct((8, D_MODEL), F32),
                   jax.ShapeDtypeStruct((32, D_MODEL), F32)),
        in_specs=[pl.BlockSpec((4, tm, D_MODEL), lambda i: (0, i, 0)),
                  pl.BlockSpec((n_cblocks, 2, tm, 128), lambda i: (0, 0, i, 0)),
                  pl.BlockSpec((12, D_MODEL, 512), lambda i: (0, 0, 0), pipeline_mode=pl.Buffered(1)),
                  pl.BlockSpec((tm, D_MODEL), tok),
                  pl.BlockSpec((tm, D_MODEL), tok),
                  pl.BlockSpec((1, 3 * D_MODEL), const2),
                  pl.BlockSpec((1, D_MODEL), const2),
                  pl.BlockSpec((nsub, s_len, 128), lambda i: (i // parts, 0, 0)),
                  pl.BlockSpec((nsub, s_len, 128), lambda i: (i // parts, 0, 0))],
        out_specs=(pl.BlockSpec((tm, D_MODEL), tok), pl.BlockSpec((8, D_MODEL), const2),
                   pl.BlockSpec((32, cw), lambda i: (0, i // parts))),
        scratch_shapes=[pltpu.VMEM((tm, D_MODEL), F32), pltpu.VMEM((8, 8, D_MODEL), F32),
                        pltpu.VMEM((nsub, s_len + 2 * CONV_PAD, 128), F32), pltpu.VMEM((nsub, 32, 8, 128), F32)],
        compiler_params=_params(dimension_semantics=("arbitrary",)),
    )(dz4, dzc, w12, x, dx2, mod, norm_g, dq, p)


def _rs_shard_block(i, nj):
    return jnp.where(i < 3 * nj, i % 3, 3), jnp.where(i < 3 * nj, i // 3, i - 3 * nj)


def _rs_block(i, k, nj):
    s, j = _rs_shard_block(i, nj)
    return nj * ((k + 1 + s) % N_CHIPS) + j


def _rs_scratch(nj, nr, nc):
    nblk, nfor = N_CHIPS * nj, 3 * nj
    return [pltpu.VMEM((2, 2, nr, nc), F32),
            pltpu.VMEM((nblk, nr, nc), F32),
            pltpu.VMEM((nfor, nr, nc), BF16),
            pltpu.VMEM((nfor, nr, nc), BF16),
            pltpu.VMEM((nj, nr, nc), F32),
            pltpu.SemaphoreType.DMA,
            pltpu.SemaphoreType.DMA((nblk,)), pltpu.SemaphoreType.DMA((nblk,)),
            pltpu.SemaphoreType.DMA((nfor,)), pltpu.SemaphoreType.DMA((nfor,)),
            pltpu.SemaphoreType.DMA, pltpu.SemaphoreType.DMA]


def _rs_step(i, kk, kt, nj, nr, nc, partial_fn, out_ref, scr):
    nblk, nfor = N_CHIPS * nj, 3 * nj
    acc, recv_a, send_b, recv_b, own_ps, lsem, psend, precv, isend, irecv, fsend, frecv = scr
    x, y, c = _pos()
    k = 2 * x + y
    sib = (x, y, 1 - c)
    slot = i % 2

    @pl.when(kk == 0)
    def _():
        acc[slot] = jnp.zeros((2, nr, nc), F32)

    def pair_copy(ii, sl):
        return _rcopy(acc.at[sl, 1 - c], recv_a.at[ii], psend.at[ii], precv.at[ii], sib)

    def pair_sum(ii, sl):
        cp = pair_copy(ii, sl)
        cp.wait_recv()
        cp.wait_send()
        return acc[sl, c] + recv_a[ii]

    def ici_copy(ii, kd, rslot):
        return _rcopy(send_b.at[ii], recv_b.at[rslot], isend.at[ii], irecv.at[rslot], (kd // 2, kd % 2, c))

    def finalize(ii, sl):
        ps = pair_sum(ii, sl)
        s, j = _rs_shard_block(ii, nj)

        @pl.when(s < 3)
        def _():
            send_b[ii] = ps.astype(BF16)
            ici_copy(ii, (k + 1 + s) % N_CHIPS, (2 - s) * nj + j).start()

        @pl.when(s == 3)
        def _():
            own_ps[j] = ps

    @pl.when(jnp.logical_and(kk == kt - 1, i >= 1))
    def _():
        finalize(i - 1, 1 - slot)

    def add_partial(res):
        acc[slot, 0] += res[0:nr]
        acc[slot, 1] += res[nr:2 * nr]
    partial_fn(add_partial)

    @pl.when(kk == kt - 1)
    def _():
        pair_copy(i, slot).start()

        @pl.when(i == nblk - 1)
        def _():
            own_ps[nj - 1] = pair_sum(nblk - 1, (nblk - 1) % 2)
            for r in range(nfor):
                ici_copy(0, k, r).wait_recv()
            for j in range(nj):
                tot = own_ps[j]
                for s in range(3):
                    tot = tot + recv_b[s * nj + j].astype(F32)
                own_ps[j] = tot
            loc = pltpu.make_async_copy(own_ps, out_ref.at[:, c], lsem)
            loc.start()
            swap = _rcopy(own_ps, out_ref.at[:, c], fsend, frecv, sib)
            swap.start()
            loc.wait()
            swap.wait_send()
            _rcopy(own_ps, out_ref.at[:, 1 - c], fsend, frecv, sib).wait_recv()
            for ii in range(nfor):
                ici_copy(ii, k, 0).wait_send()


def _grad_w_in(kidx, h, dz4, dzc, sm2, dcw):
    s_len = h.shape[0]
    tk = min(K_TILE, s_len)
    kt = s_len // tk
    nj, nr, nc = 3, 512, 512
    nblk = N_CHIPS * nj

    def body(k_ref, a_ref, b_ref, bc_ref, sm2_ref, dcw_in, out_ref, pk2_ref, dcwsh_ref, dcb_ref, *scr):
        del k_ref
        b_pk2, b_dcw, r_pk2, r_dcw, bsend, brecv = scr[-6:]
        i = pl.program_id(0)
        kk = pl.program_id(1)
        x, y, c = _pos()
        k = 2 * x + y
        m = _rs_block(i, k, nj)

        @pl.when(jnp.logical_and(kk == 0, i == 0))
        def _():
            b = 4 * x + 2 * y + c
            b_pk2[0:8, :] = _onehot_rows(sm2_ref[0:1, :], b)
            b_pk2[8:16, :] = _onehot_rows(sm2_ref[1:2, :], b)
            b_pk2[16:17, :] = sm2_ref[2:3, :]
            b_pk2[17:24, :] = jnp.zeros((7, D_MODEL), F32)
            b_dcw[...] = dcw_in[...]

        for e in range(4):
            @pl.when(jnp.logical_and(kk == 0, i == (e * nblk) // 4))
            def _():
                _butterfly_event(e, [b_pk2, b_dcw], [r_pk2, r_dcw], bsend, brecv, _all_reduce_partners())

        @pl.when(jnp.logical_and(kk == 0, i == (3 * nblk) // 4))
        def _():
            pk2_ref[...] = b_pk2[...]
            sel = jnp.zeros((32, 256), F32)
            for kc in range(N_CHIPS):
                sel = jnp.where(k == kc, b_dcw[:, kc * 256:(kc + 1) * 256], sel)
            dcwsh_ref[...] = sel
            dcb_ref[...] = b_dcw[31:32, :]

        def partial(add):
            @pl.when(m >= 4)
            def _():
                add(_dot_tn(a_ref[...], b_ref[0]))

            @pl.when(m < 4)
            def _():
                add(_dot_tn(a_ref[...], jnp.concatenate([bc_ref[blk, 0] for blk in range(4)], axis=1)))
        _rs_step(i, kk, kt, nj, nr, nc, partial, out_ref, scr[:-6])

    def dz4_index(i, kk, k_ref):
        m = _rs_block(i, k_ref[0], nj)
        use = m >= 4
        return (jnp.where(use, m // 2 - 2, 0), jnp.where(use, kk, 0), jnp.where(use, m % 2, 0))

    def dzc_index(i, kk, k_ref):
        m = _rs_block(i, k_ref[0], nj)
        use = m < 4
        return (jnp.where(use, m % 2, 0), jnp.where(use, m // 2, 0), jnp.where(use, kk, 0), 0)

    const2 = lambda i, kk, k_ref: (0, 0)
    small_out = [(24, D_MODEL), (32, 256), (1, D_MODEL)]
    small_buf = [(24, D_MODEL), (32, D_MODEL)]
    grid_spec = pltpu.PrefetchScalarGridSpec(
        num_scalar_prefetch=1,
        grid=(nblk, kt),
        in_specs=[pl.BlockSpec((tk, D_MODEL), lambda i, kk, k_ref: (kk, 0)),
                  pl.BlockSpec((1, tk, 512), dz4_index),
                  pl.BlockSpec((4, 1, tk, 128), dzc_index),
                  pl.BlockSpec((8, D_MODEL), const2),
                  pl.BlockSpec((32, D_MODEL), const2)],
        out_specs=(pl.BlockSpec(memory_space=pl.ANY),) + tuple(pl.BlockSpec(s, const2) for s in small_out),
        scratch_shapes=_rs_scratch(nj, nr, nc)
        + [pltpu.VMEM(s, F32) for s in small_buf] + [pltpu.VMEM((3,) + s, F32) for s in small_buf]
        + [pltpu.SemaphoreType.DMA((6,)), pltpu.SemaphoreType.DMA((6,))])
    return pl.pallas_call(
        body, name="grad_w_in",
        grid_spec=grid_spec,
        out_shape=(jax.ShapeDtypeStruct((nj, 2, nr, nc), F32),) + tuple(jax.ShapeDtypeStruct(s, F32) for s in small_out),
        compiler_params=_params(dimension_semantics=("arbitrary", "arbitrary")),
    )(kidx, h, dz4, dzc, sm2, dcw)


def _adamw_math(w, g, m, v):
    m = ADAM_B1 * m + (1.0 - ADAM_B1) * g
    v = ADAM_B2 * v + (1.0 - ADAM_B2) * (g * g)
    m_hat = m / (1.0 - ADAM_B1 ** ADAM_STEP)
    v_hat = v / (1.0 - ADAM_B2 ** ADAM_STEP)
    delta = -ADAM_LR * (m_hat / (jnp.sqrt(v_hat) + ADAM_EPS) + ADAM_WD * w)
    return delta, m, v


def _adamw_blocked(w, m, v, g4, name):
    nj, _, nr, nc = g4.shape

    def body(w_ref, m_ref, v_ref, g_ref, go_ref, d_ref, mo_ref, vo_ref):
        g = g_ref[0, 0]
        d, mn, vn = _adamw_math(w_ref[...], g, m_ref[...], v_ref[...])
        go_ref[...] = g
        d_ref[...] = d
        mo_ref[...] = mn
        vo_ref[...] = vn

    blk = pl.BlockSpec((nr, nc), lambda j, hf: (hf, j))
    return pl.pallas_call(
        body, name=name,
        grid=(nj, 2),
        out_shape=tuple(jax.ShapeDtypeStruct(w.shape, F32) for _ in range(4)),
        in_specs=[blk, blk, blk, pl.BlockSpec((1, 1, nr, nc), lambda j, hf: (j, hf, 0, 0))],
        out_specs=(blk, blk, blk, blk),
        compiler_params=_params(dimension_semantics=("arbitrary", "arbitrary")),
    )(w, m, v, g4)


def _adamw_w_ada(cact, pk1, pk2, w, m, v):
    rb = 256

    def body(cact_ref, pk1_ref, pk2_ref, w_ref, m_ref, v_ref, g_ref, d_ref, mo_ref, vo_ref, gb_ref, dmod, sel_scr):
        @pl.when(pl.program_id(0) == 0)
        def _():
            x, y, _ = _pos()
            k = 2 * x + y
            dmod[:, 0:D_MODEL] = pk2_ref[0:8, :]
            dmod[:, D_MODEL:2 * D_MODEL] = pk2_ref[8:16, :]
            dmod[:, 2 * D_MODEL:3 * D_MODEL] = pk1_ref[0:8, :]
            gb_ref[...] = jnp.sum(dmod[...], axis=0, keepdims=True)
            sel = jnp.zeros((8, 768), F32)
            for kk in range(N_CHIPS):
                sel = jnp.where(k == kk, dmod[:, kk * 768:(kk + 1) * 768], sel)
            sel_scr[...] = sel

        g = _dot_tn(cact_ref[...].astype(BF16), sel_scr[...].astype(BF16))
        d, mn, vn = _adamw_math(w_ref[...], g, m_ref[...], v_ref[...])
        g_ref[...] = g
        d_ref[...] = d
        mo_ref[...] = mn
        vo_ref[...] = vn

    blk = pl.BlockSpec((rb, 768), lambda i: (i, 0))
    const2 = lambda i: (0, 0)
    return pl.pallas_call(
        body, name="adamw_w_ada",
        grid=(D_MODEL // rb,),
        out_shape=tuple(jax.ShapeDtypeStruct(w.shape, F32) for _ in range(4)) + (
            jax.ShapeDtypeStruct((1, 3 * D_MODEL), F32),),
        in_specs=[pl.BlockSpec((8, rb), lambda i: (0, i)), pl.BlockSpec((16, D_MODEL), const2),
                  pl.BlockSpec((24, D_MODEL), const2), blk, blk, blk],
        out_specs=(blk, blk, blk, blk, pl.BlockSpec((1, 3 * D_MODEL), const2)),
        scratch_shapes=[pltpu.VMEM((8, 3 * D_MODEL), F32), pltpu.VMEM((8, 768), F32)],
        compiler_params=_params(dimension_semantics=("arbitrary",)),
    )(cact, pk1, pk2, w, m, v)


def _adamw_small(items):
    n = len(items)

    def body(*refs):
        ins, outs = refs[:4 * n], refs[4 * n:]
        for i in range(n):
            w_ref, g_ref, m_ref, v_ref = ins[4 * i:4 * i + 4]
            d, mn, vn = _adamw_math(w_ref[...], g_ref[...], m_ref[...], v_ref[...])
            outs[3 * i][...] = d
            outs[3 * i + 1][...] = mn
            outs[3 * i + 2][...] = vn

    flat = [a for it in items for a in it]
    outs = pl.pallas_call(
        body, name="adamw_small",
        out_shape=tuple(jax.ShapeDtypeStruct(it[0].shape, F32) for it in items for _ in range(3)),
        in_specs=[_vmem()] * (4 * n),
        out_specs=tuple(_vmem() for _ in range(3 * n)),
        compiler_params=_params(),
    )(*flat)
    return [tuple(outs[3 * i:3 * i + 3]) for i in range(n)]


def kernel(x, c, w_ada, b_ada, norm_g, w_in, conv_w, conv_b, conv_ln_g, conv_ln_b, sg_ln_g, sg_ln_b, w_s, b_s, w_out, final_g, loss_target, m_w_ada, m_b_ada, m_norm_g, m_w_in, m_conv_w, m_conv_b, m_conv_ln_g, m_conv_ln_b, m_sg_ln_g, m_sg_ln_b, m_w_s, m_b_s, m_w_out, m_final_g, v_w_ada, v_b_ada, v_norm_g, v_w_in, v_conv_w, v_conv_b, v_conv_ln_g, v_conv_ln_b, v_sg_ln_g, v_sg_ln_b, v_w_s, v_b_s, v_w_out, v_final_g):
    s_len = x.shape[1]
    x2d = x[0]
    tgt = loss_target[0]
    row = lambda a: a.reshape(1, -1)

    kidx = (2 * lax.axis_index("x") + lax.axis_index("y")).astype(jnp.int32).reshape(1)
    cw_sh = jnp.pad(conv_w.reshape(CONV_WIDTH, 256), ((0, 1), (0, 0)))
    h, z6, mod, cact, cw4, w_in12 = _front(kidx, x2d, c, w_ada[0], b_ada, norm_g, w_in[0], cw_sh)
    w12 = w_in12.reshape(12, D_MODEL, 512)
    q, p_cm, sig_cm, t2_cm, w_out4 = _conv_fwd(z6, cw4, conv_b, w_out[0])
    w_out_full = w_out4.reshape(2 * D_MODEL, D_MODEL)
    ln_rows = (conv_ln_g, conv_ln_b, sg_ln_g, sg_ln_b, row(final_g))
    bs_exp = jnp.repeat(b_s[0].T, CHUNK, axis=1)
    dz4, dq, dx2, ycat, dy, dws, sm1, dbs = _middle(x2d, z6, q, tgt, ln_rows, mod, w_s[0], bs_exp, w_out_full)
    dzc, pk1, dws_r, dbs_r, loss_t, g_w_out4 = _conv_bwd(
        kidx, dq, sig_cm, t2_cm, cw4, sm1, dws.reshape(D_MODEL, CHUNK), dbs, ycat, dy)
    grad_x, sm2, dcw = _bwd_in(dz4, dzc, w12, x2d, dx2, mod, norm_g, dq, p_cm)
    g_w_in4, pk2, dcw_sh, dcb = _grad_w_in(kidx, h, dz4, dzc, sm2, dcw)

    g_w_in, d_w_in, nm_w_in, nv_w_in = _adamw_blocked(w_in[0], m_w_in[0], v_w_in[0], g_w_in4, "adamw_w_in")
    g_w_out, d_w_out, nm_w_out, nv_w_out = _adamw_blocked(w_out[0], m_w_out[0], v_w_out[0], g_w_out4, "adamw_w_out")
    g_w_ada, d_w_ada, nm_w_ada, nv_w_ada, g_b_ada = _adamw_w_ada(cact, pk1, pk2, w_ada[0], m_w_ada[0], v_w_ada[0])

    g_norm_g = pk2[16:17]
    g_cln_g, g_cln_b, g_sln_g, g_sln_b, g_final = (pk1[8 + i:9 + i] for i in range(5))
    loss = loss_t[0, 0]
    g_conv_w = dcw_sh[:CONV_WIDTH]
    g_w_s = dws_r
    g_b_s = dbs_r[:, :HEADS].T
    small = [
        (b_ada, g_b_ada, m_b_ada, v_b_ada),
        (norm_g, g_norm_g, m_norm_g, v_norm_g),
        (conv_w.reshape(CONV_WIDTH, 256), g_conv_w, m_conv_w.reshape(CONV_WIDTH, 256), v_conv_w.reshape(CONV_WIDTH, 256)),
        (conv_b, dcb, m_conv_b, v_conv_b),
        (conv_ln_g, g_cln_g, m_conv_ln_g, v_conv_ln_g),
        (conv_ln_b, g_cln_b, m_conv_ln_b, v_conv_ln_b),
        (sg_ln_g, g_sln_g, m_sg_ln_g, v_sg_ln_g),
        (sg_ln_b, g_sln_b, m_sg_ln_b, v_sg_ln_b),
        (w_s.reshape(D_MODEL, CHUNK), g_w_s, m_w_s.reshape(D_MODEL, CHUNK), v_w_s.reshape(D_MODEL, CHUNK)),
        (b_s[0], g_b_s, m_b_s[0], v_b_s[0]),
        (row(final_g), g_final, row(m_final_g), row(v_final_g)),
    ]
    upd = _adamw_small(small)

    shapes = [w_ada.shape, b_ada.shape, norm_g.shape, w_in.shape, conv_w.shape, conv_b.shape, conv_ln_g.shape,
              conv_ln_b.shape, sg_ln_g.shape, sg_ln_b.shape, w_s.shape, b_s.shape, w_out.shape, final_g.shape]
    grads = [g_w_ada, g_b_ada, g_norm_g, g_w_in, g_conv_w, dcb, g_cln_g, g_cln_b, g_sln_g, g_sln_b, g_w_s, g_b_s,
             g_w_out, g_final]
    big = {0: (d_w_ada, nm_w_ada, nv_w_ada), 3: (d_w_in, nm_w_in, nv_w_in), 12: (d_w_out, nm_w_out, nv_w_out)}
    small_pos = [1, 2, 4, 5, 6, 7, 8, 9, 10, 11, 13]
    trip = [None] * 14
    for i, t in big.items():
        trip[i] = t
    for i, t in zip(small_pos, upd):
        trip[i] = t
    fit = lambda arrs: [a.reshape(s) for a, s in zip(arrs, shapes)]
    return (loss, grad_x.reshape(x.shape), *fit(grads), *fit([t[0] for t in trip]), *fit([t[1] for t in trip]),
            *fit([t[2] for t in trip]))
```

```python
import jax
import jax.numpy as jnp
from jax import lax
from jax.experimental import pallas as pl
from jax.experimental.pallas import tpu as pltpu

F32 = jnp.float32
BF16 = jnp.bfloat16
MESH = pl.DeviceIdType.MESH

D_MODEL = 1024
N_CHIPS = 4
HEADS = 8
CHUNK = 128
CONV_WIDTH = 31
CONV_HALF = CONV_WIDTH // 2
CONV_PAD = 16
EPS = 1e-6
ADAM_LR = 0.001
ADAM_B1 = 0.9
ADAM_B2 = 0.999
ADAM_EPS = 1e-08
ADAM_WD = 0.01
ADAM_STEP = 10

V7X_VMEM_BYTES = 64 * 1024 * 1024
VMEM_LIMIT = V7X_VMEM_BYTES - 8 * 1024 * 1024
ROWS = 16
UNROLL = 8
TOKEN_TILE = 256
FULL_UNROLL = TOKEN_TILE // ROWS
TIME_TILE = 128
K_TILE = 2048

N_GROUPS = 6


def _natural_group(j):
    return (j + 2) % N_GROUPS


def _pos():
    return lax.axis_index("x"), lax.axis_index("y"), lax.axis_index("c")


def _rcopy(src, dst, ssem, rsem, dev):
    return pltpu.make_async_remote_copy(src_ref=src, dst_ref=dst, send_sem=ssem, recv_sem=rsem,
                                        device_id=dev, device_id_type=MESH)


def _vmem():
    return pl.BlockSpec(memory_space=pltpu.VMEM)


def _params(**kw):
    return pltpu.CompilerParams(vmem_limit_bytes=VMEM_LIMIT, **kw)


def _sigmoid(v):
    return 0.5 * jnp.tanh(0.5 * v) + 0.5


def _row_loop(n_rows, body, unroll=1):
    def step(r, carry):
        body(pl.ds(pl.multiple_of(r * ROWS, ROWS), ROWS))
        return carry
    lax.fori_loop(0, n_rows // ROWS, step, 0, unroll=unroll)


def _colsum8(v):
    return v.reshape(v.shape[0] // 8, 8, v.shape[1]).sum(axis=0)


def _mean(v):
    return jnp.mean(v, axis=-1, keepdims=True)


def _dot_nn(a, b):
    return jnp.dot(a, b, preferred_element_type=F32)


def _dot_nt(a, b):
    return lax.dot_general(a, b, (((1,), (1,)), ((), ())), preferred_element_type=F32)


def _dot_tn(a, b):
    return lax.dot_general(a, b, (((0,), (0,)), ((), ())), preferred_element_type=F32)


def _remote_chip(k, r):
    return jnp.bitwise_xor(k, r + 1)


def _front(kidx, x, c, w_ada, b_ada, norm_g, w_in, cw):
    s_len = x.shape[0]
    tmh = min(512, s_len)
    tmz = min(1024, s_len)
    nh = s_len // tmh
    nz = s_len // tmz
    n_steps = nh + 12 * nz

    def remote_block(q):
        return jnp.where(q < 6, q % 2, 2), jnp.where(q < 6, q // 2, q - 6)

    def block_of(i, k):
        r, j = remote_block(jnp.maximum(i - 3, 0))
        return jnp.where(i < 3, 3 * k + i, 3 * _remote_chip(k, r) + j)

    def body(k_ref, x_ref, c_ref, bada_ref, g_ref, cw_ref, wada_hbm, win_hbm,
             h_ref, z_ref, mod_ref, cact_ref, cw4_ref, w12_hbm,
             h_all, wbuf, stage, wada_v, cslab, mslab, cw4_s,
             lsem, csend, crecv, msend, mrecv, wsend, wrecv, isend, irecv, fsend, frecv, osem):
        del k_ref
        t = pl.program_id(0)
        x_, y_, c_ = _pos()
        k = 2 * x_ + y_
        b = 4 * x_ + 2 * y_ + c_
        sib = (x_, y_, 1 - c_)

        def dev_of(r):
            kk = _remote_chip(k, r)
            return (kk // 2, kk % 2, c_)

        def ici(q, kk):
            r, j = remote_block(q)
            return _rcopy(wbuf.at[3 * kk + j, c_], wbuf.at[3 * kk + j, c_], isend.at[q], irecv.at[q], dev_of(r))

        def fwd(q, hf):
            r, j = remote_block(q)
            blk = 3 * _remote_chip(k, r) + j
            return _rcopy(wbuf.at[blk, hf], wbuf.at[blk, hf], fsend.at[q], frecv.at[q], sib)

        def c_copy(q, src):
            d = jnp.bitwise_xor(b, q)
            return _rcopy(cslab.at[src], cslab.at[src], csend.at[q - 1], crecv.at[q - 1], (d // 4, (d // 2) % 2, d % 2))

        def m_copy(r, kk):
            return _rcopy(mslab.at[kk], mslab.at[kk], msend.at[r], mrecv.at[r], dev_of(r))

        def cw_copy(r, kk):
            return _rcopy(cw4_s.at[kk], cw4_s.at[kk], wsend.at[r], wrecv.at[r], dev_of(r))

        def to_hbm(i):
            m = block_of(i, k)
            return pltpu.make_async_copy(wbuf.at[m], w12_hbm.at[m], osem.at[i])

        @pl.when(t == 0)
        def _():
            ld_w = pltpu.make_async_copy(win_hbm, stage, lsem.at[0])
            ld_w.start()
            ld_a = pltpu.make_async_copy(wada_hbm, wada_v, lsem.at[1])
            ld_a.start()
            cslab[b] = jnp.broadcast_to(c_ref[...], (8, D_MODEL))
            for q in range(1, 8):
                c_copy(q, b).start()
            cw4_s[k] = cw_ref[...]
            for r in range(3):
                cw_copy(r, k).start()
            ld_w.wait()
            for j in range(3):
                for hf in range(2):
                    wbuf[3 * k + j, hf] = stage[hf * 512:(hf + 1) * 512, j * 512:(j + 1) * 512].astype(BF16)
            for i in range(3):
                to_hbm(i).start()
            for q in range(1, 8):
                c_copy(q, jnp.bitwise_xor(b, q)).wait_recv()
            row = lax.broadcasted_iota(jnp.int32, (8, D_MODEL), 0)
            call = jnp.zeros((8, D_MODEL), F32)
            for d in range(8):
                call = jnp.where(row == d, cslab[d], call)
            cact = call * _sigmoid(call)
            cact_ref[...] = cact
            ld_a.wait()
            mslab[k] = _dot_nn(cact.astype(BF16), wada_v[...].astype(BF16))
            for r in range(3):
                m_copy(r, k).start()
            for q in range(9):
                ici(q, k).start()
            for r in range(3):
                m_copy(r, _remote_chip(k, r)).wait_recv()
            row8 = lax.broadcasted_iota(jnp.int32, (8, 768), 0)
            for kk in range(N_CHIPS):
                piece = jnp.sum(jnp.where(row8 == b, mslab[kk], 0.0), axis=0, keepdims=True)
                mod_ref[:, kk * 768:(kk + 1) * 768] = piece + bada_ref[:, kk * 768:(kk + 1) * 768]
            for r in range(3):
                cw_copy(r, _remote_chip(k, r)).wait_recv()
            cw4_ref[...] = cw4_s[...]

        @pl.when(t < nh)
        def _():
            shift = mod_ref[:, 0:D_MODEL]
            scale1 = 1.0 + mod_ref[:, D_MODEL:2 * D_MODEL]
            g = g_ref[...]
            base = t * tmh

            def rows_fn(rows):
                xt = x_ref[rows, :]
                r = lax.rsqrt(_mean(xt * xt) + EPS)
                hv = ((xt * r * g) * scale1 + shift).astype(BF16)
                h_ref[rows, :] = hv
                h_all[pl.ds(pl.multiple_of(base + rows.start, ROWS), ROWS), :] = hv
            _row_loop(tmh, rows_fn, unroll=UNROLL)

        @pl.when(t >= nh)
        def _():
            u = t - nh
            i = u // nz
            rt = u % nz
            @pl.when(jnp.logical_and(rt == 0, jnp.logical_and(i >= 2, i <= 10)))
            def _():
                q = i - 2
                r, _ = remote_block(q)
                ici(q, _remote_chip(k, r)).wait_recv()
                fwd(q, c_).start()

            @pl.when(jnp.logical_and(rt == 0, i >= 3))
            def _():
                fwd(i - 3, 1 - c_).wait_recv()
                to_hbm(i).start()
            m = block_of(i, k)
            hb = h_all[pl.ds(pl.multiple_of(rt * tmz, tmz), tmz), :]
            z_ref[0] = _dot_nn(hb, wbuf[m].reshape(D_MODEL, 512))

        @pl.when(t == n_steps - 1)
        def _():
            for q in range(1, 8):
                c_copy(q, b).wait_send()
            for r in range(3):
                m_copy(r, k).wait_send()
                cw_copy(r, k).wait_send()
            for q in range(9):
                ici(q, k).wait_send()
                fwd(q, c_).wait_send()
            for i in range(12):
                to_hbm(i).wait()

    def z_index(t, k_ref):
        u = jnp.maximum(t - nh, 0)
        m = block_of(u // nz, k_ref[0])
        return ((m // 2 + 4) % N_GROUPS, u % nz, m % 2)

    tok = lambda t, k_ref: (jnp.minimum(t, nh - 1), 0)
    const2 = lambda t, k_ref: (0, 0)
    grid_spec = pltpu.PrefetchScalarGridSpec(
        num_scalar_prefetch=1,
        grid=(n_steps,),
        in_specs=[pl.BlockSpec((tmh, D_MODEL), tok),
                  pl.BlockSpec((1, D_MODEL), const2),
                  pl.BlockSpec((1, 3 * D_MODEL), const2),
                  pl.BlockSpec((1, D_MODEL), const2),
                  pl.BlockSpec((32, 256), const2),
                  pl.BlockSpec(memory_space=pl.ANY),
                  pl.BlockSpec(memory_space=pl.ANY)],
        out_specs=(pl.BlockSpec((tmh, D_MODEL), tok),
                   pl.BlockSpec((1, tmz, 512), z_index),
                   pl.BlockSpec((1, 3 * D_MODEL), const2),
                   pl.BlockSpec((8, D_MODEL), const2),
                   pl.BlockSpec((N_CHIPS, 32, 256), lambda t, k_ref: (0, 0, 0)),
                   pl.BlockSpec(memory_space=pl.ANY)),
        scratch_shapes=[pltpu.VMEM((s_len, D_MODEL), BF16),
                        pltpu.VMEM((12, 2, 512, 512), BF16),
                        pltpu.VMEM((D_MODEL, 1536), F32),
                        pltpu.VMEM((D_MODEL, 768), F32),
                        pltpu.VMEM((8, 8, D_MODEL), F32),
                        pltpu.VMEM((N_CHIPS, 8, 768), F32),
                        pltpu.VMEM((N_CHIPS, 32, 256), F32),
                        pltpu.SemaphoreType.DMA((2,)),
                        pltpu.SemaphoreType.DMA((7,)), pltpu.SemaphoreType.DMA((7,)),
                        pltpu.SemaphoreType.DMA((3,)), pltpu.SemaphoreType.DMA((3,)),
                        pltpu.SemaphoreType.DMA((3,)), pltpu.SemaphoreType.DMA((3,)),
                        pltpu.SemaphoreType.DMA((9,)), pltpu.SemaphoreType.DMA((9,)),
                        pltpu.SemaphoreType.DMA((9,)), pltpu.SemaphoreType.DMA((9,)),
                        pltpu.SemaphoreType.DMA((12,))])
    return pl.pallas_call(
        body, name="front",
        grid_spec=grid_spec,
        out_shape=(jax.ShapeDtypeStruct((s_len, D_MODEL), BF16),
                   jax.ShapeDtypeStruct((N_GROUPS, s_len, D_MODEL), F32),
                   jax.ShapeDtypeStruct((1, 3 * D_MODEL), F32),
                   jax.ShapeDtypeStruct((8, D_MODEL), F32),
                   jax.ShapeDtypeStruct((N_CHIPS, 32, 256), F32),
                   jax.ShapeDtypeStruct((12, 2, 512, 512), BF16)),
        compiler_params=_params(dimension_semantics=("arbitrary",)),
    )(kidx, x, c, b_ada, norm_g, cw, w_ada, w_in)


def _all_reduce_partners():
    x, y, c = _pos()
    return [(x, y, 1 - c), (x, 1 - y, c), (1 - x, y, c)]


def _butterfly_event(e, bufs, recvs, ssem, rsem, partners, wires=None):
    n = len(bufs)
    wires = wires or [None] * n

    def copies(s):
        return [_rcopy(buf if wire is None else wire, recv.at[s], ssem.at[s * n + i], rsem.at[s * n + i], partners[s])
                for i, (buf, recv, wire) in enumerate(zip(bufs, recvs, wires))]

    if e > 0:
        for cp in copies(e - 1):
            cp.wait()
        for buf, recv, wire in zip(bufs, recvs, wires):
            mine = buf[...] if wire is None else wire[...].astype(F32)
            buf[...] = mine + recv[e - 1].astype(F32)
    if e < len(partners):
        for buf, wire in zip(bufs, wires):
            if wire is not None:
                wire[...] = buf[...].astype(BF16)
        for cp in copies(e):
            cp.start()


def _onehot_rows(v, b):
    row = lax.broadcasted_iota(jnp.int32, (8, v.shape[1]), 0)
    return jnp.where(row == b, jnp.broadcast_to(v, (8, v.shape[1])), 0.0)


def _conv_fwd(z6, cw4, conv_b, w_out):
    s_len = z6.shape[1]
    tt = TIME_TILE
    n_blocks = D_MODEL // 128

    def body(z_ref, cw_ref, cb_ref, wout_hbm, q_ref, p_ref, sig_ref, t2_ref, wout4_hbm,
             ppad, stage, wbuf, lsem, isend, irecv, fsend, frecv, osem):
        jb = pl.program_id(0)
        x_, y_, c_ = _pos()
        k = 2 * x_ + y_
        sib = (x_, y_, 1 - c_)

        def ici(r, kk):
            rk = _remote_chip(k, r)
            return _rcopy(wbuf.at[kk, c_], wbuf.at[kk, c_], isend.at[r], irecv.at[r], (rk // 2, rk % 2, c_))

        def fwd(r, hf):
            kk = _remote_chip(k, r)
            return _rcopy(wbuf.at[kk, hf], wbuf.at[kk, hf], fsend.at[r], frecv.at[r], sib)

        @pl.when(jb == 0)
        def _():
            ld = pltpu.make_async_copy(wout_hbm, stage, lsem)
            ld.start()
            ld.wait()
            for hf in range(2):
                wbuf[k, hf] = stage[hf * 256:(hf + 1) * 256, :].astype(BF16)
            for r in range(3):
                ici(r, k).start()

        @pl.when(jb == (5 * n_blocks) // 8)
        def _():
            for r in range(3):
                ici(r, _remote_chip(k, r)).wait_recv()
                fwd(r, c_).start()

        zero = jnp.zeros((CONV_PAD, 128), F32)
        ppad[0:CONV_PAD, :] = zero
        ppad[s_len + CONV_PAD:s_len + 2 * CONV_PAD, :] = zero

        def fill(i, carry):
            t0 = pl.multiple_of(i * tt, tt)
            sig = _sigmoid(z_ref[1, pl.ds(t0, tt), :])
            p = z_ref[0, pl.ds(t0, tt), :] * sig
            ppad[pl.ds(CONV_PAD + t0, tt), :] = p
            p_ref[0, pl.ds(t0, tt), :] = p
            sig_ref[0, pl.ds(t0, tt), :] = sig
            t2_ref[0, pl.ds(t0, tt), :] = p * (1.0 - sig)
            return carry
        lax.fori_loop(0, s_len // tt, fill, 0)
        w = cw_ref[0]
        bias = cb_ref[...]

        def conv(i, carry):
            t0 = pl.multiple_of(i * tt, tt)
            acc = jnp.broadcast_to(bias, (tt, 128))
            for o in range(1, CONV_WIDTH + 1):
                acc = acc + w[o - 1:o, :] * ppad[pl.ds(t0 + o, tt), :]
            q_ref[0, pl.ds(t0, tt), :] = acc
            return carry
        lax.fori_loop(0, s_len // tt, conv, 0)

        @pl.when(jb == n_blocks - 1)
        def _():
            for r in range(3):
                fwd(r, 1 - c_).wait_recv()
            out = pltpu.make_async_copy(wbuf, wout4_hbm, osem)
            out.start()
            for r in range(3):
                ici(r, k).wait_send()
                fwd(r, c_).wait_send()
            out.wait()

    return pl.pallas_call(
        body, name="conv_fwd",
        grid=(n_blocks,),
        out_shape=tuple(jax.ShapeDtypeStruct((n_blocks, s_len, 128), F32) for _ in range(4))
        + (jax.ShapeDtypeStruct((N_CHIPS, 2, 256, D_MODEL), BF16),),
        in_specs=[pl.BlockSpec((2, s_len, 128), lambda j: (2, 0, j)),
                  pl.BlockSpec((1, 32, 128), lambda j: (j // 2, 0, j % 2)),
                  pl.BlockSpec((1, 128), lambda j: (0, j)),
                  pl.BlockSpec(memory_space=pl.ANY)],
        out_specs=tuple(pl.BlockSpec((1, s_len, 128), lambda j: (j, 0, 0)) for _ in range(4))
        + (pl.BlockSpec(memory_space=pl.ANY),),
        scratch_shapes=[pltpu.VMEM((s_len + 2 * CONV_PAD, 128), F32),
                        pltpu.VMEM((512, D_MODEL), F32),
                        pltpu.VMEM((N_CHIPS, 2, 256, D_MODEL), BF16),
                        pltpu.SemaphoreType.DMA,
                        pltpu.SemaphoreType.DMA((3,)), pltpu.SemaphoreType.DMA((3,)),
                        pltpu.SemaphoreType.DMA((3,)), pltpu.SemaphoreType.DMA((3,)),
                        pltpu.SemaphoreType.DMA],
        compiler_params=_params(dimension_semantics=("arbitrary",)),
    )(z6, cw4, conv_b, w_out)


def _middle(x, z6, q, target, ln_rows, mod, w_s, bs_exp, w_out):
    s_len = x.shape[0]
    tm = TOKEN_TILE
    n_steps = s_len // tm
    n_chunks = tm // CHUNK
    inv_d = 1.0 / D_MODEL

    def body(x_ref, z_ref, q_ref, tgt_ref, cg_ref, cb_ref, sg_ref, sb_ref, fg_ref, mod_ref, ws_ref, bs_ref, wout_ref,
             dz_ref, dq_ref, dx2_ref, ycat_ref, dy_ref, dws_ref, sm_ref, dbs_ref,
             vl_scr, vm_scr, y_scr, dycat_scr, dvm_scr, dvl_scr, acc_scr, dbs_acc, keep, rstd_scr):
        i = pl.program_id(0)

        @pl.when(i == 0)
        def _():
            acc_scr[...] = jnp.zeros_like(acc_scr)
            dbs_acc[...] = jnp.zeros_like(dbs_acc)
            dws_ref[...] = jnp.zeros_like(dws_ref)

        cg, cb, sg, sb, fg = cg_ref[...], cb_ref[...], sg_ref[...], sb_ref[...], fg_ref[...]
        gm = mod_ref[:, 2 * D_MODEL:3 * D_MODEL]

        def norm_stats(t):
            c = t - _mean(t)
            rstd = lax.rsqrt(_mean(c * c) + EPS)
            return c * rstd, rstd

        def phase1(rows):
            qhat, rstd_q = norm_stats(jnp.concatenate([q_ref[blk, rows, :] for blk in range(D_MODEL // 128)], axis=1))
            ln = qhat * cg + cb
            gz = z_ref[0, rows, :]
            sig_ln = _sigmoid(ln)
            sig_g = _sigmoid(gz)
            ycat_ref[rows, 0:D_MODEL] = ((ln * sig_ln) * (gz * sig_g)).astype(BF16)
            vhat, rstd_v = norm_stats(z_ref[2, rows, :])
            vl_scr[rows, :] = (vhat * sg + sb).astype(BF16)
            keep[0, rows, :] = qhat
            keep[1, rows, :] = vhat
            keep[2, rows, :] = sig_ln
            keep[3, rows, :] = sig_g
            rstd_scr[0, rows, :] = rstd_q
            rstd_scr[1, rows, :] = rstd_v
        _row_loop(tm, phase1, unroll=FULL_UNROLL)

        for ch in range(n_chunks):
            r0 = ch * CHUNK
            for h in range(HEADS):
                c0 = h * CHUNK
                vm_scr[r0:r0 + CHUNK, c0:c0 + CHUNK] = (
                    _dot_nn(ws_ref[h].astype(BF16), vl_scr[r0:r0 + CHUNK, c0:c0 + CHUNK]) + bs_ref[:, c0:c0 + CHUNK])

        def phase3(rows):
            bg = z_ref[3, rows, :]
            sig_b = _sigmoid(bg)
            keep[4, rows, :] = sig_b
            ycat_ref[rows, D_MODEL:2 * D_MODEL] = (z_ref[1, rows, :] * vm_scr[rows, :] * (bg * sig_b)).astype(BF16)
        _row_loop(tm, phase3, unroll=FULL_UNROLL)

        y_scr[...] = _dot_nn(ycat_ref[...], wout_ref[...])

        def phase5(rows):
            y = y_scr[rows, :]
            x2 = x_ref[rows, :] + gm * y
            r2 = lax.rsqrt(_mean(x2 * x2) + EPS)
            xn2 = x2 * r2
            diff = xn2 * fg - tgt_ref[rows, :]
            acc_scr[6] += _colsum8(diff * diff)
            dout = diff * inv_d
            acc_scr[0] += _colsum8(dout * xn2)
            dxn = dout * fg
            dx2 = r2 * (dxn - xn2 * _mean(dxn * xn2))
            dx2_ref[rows, :] = dx2
            acc_scr[1] += _colsum8(dx2 * y)
            dy_ref[rows, :] = (dx2 * gm).astype(BF16)
        _row_loop(tm, phase5, unroll=FULL_UNROLL)

        dycat_scr[...] = _dot_nt(dy_ref[...], wout_ref[...])

        def phase7(rows):
            dyb = dycat_scr[rows, D_MODEL:2 * D_MODEL]
            u = z_ref[1, rows, :]
            bg = z_ref[3, rows, :]
            vm = vm_scr[rows, :]
            sig = keep[4, rows, :]
            silu = bg * sig
            dz_ref[1, rows, :] = (dyb * vm * silu).astype(BF16)
            dvm = dyb * u * silu
            dz_ref[3, rows, :] = (dyb * u * vm * (sig * (1.0 + bg * (1.0 - sig)))).astype(BF16)
            dvm_scr[rows, :] = dvm.astype(BF16)
            pos = pl.ds(pl.multiple_of(rows.start % CHUNK, ROWS), ROWS)
            dbs_acc[pos, :] += dvm
        _row_loop(tm, phase7, unroll=FULL_UNROLL)

        for ch in range(n_chunks):
            r0 = ch * CHUNK
            for h in range(HEADS):
                c0 = h * CHUNK
                dvm_b = dvm_scr[r0:r0 + CHUNK, c0:c0 + CHUNK]
                dws_ref[h] += _dot_nt(dvm_b, vl_scr[r0:r0 + CHUNK, c0:c0 + CHUNK])
                dvl_scr[r0:r0 + CHUNK, c0:c0 + CHUNK] = _dot_tn(ws_ref[h].astype(BF16), dvm_b)

        def phase9(rows):
            vhat, rstd_v = keep[1, rows, :], rstd_scr[1, rows, :]
            dvl = dvl_scr[rows, :]
            acc_scr[4] += _colsum8(dvl * vhat)
            acc_scr[5] += _colsum8(dvl)
            dvh = dvl * sg
            dz_ref[2, rows, :] = (rstd_v * (dvh - _mean(dvh) - vhat * _mean(dvh * vhat))).astype(BF16)
            qhat, rstd_q = keep[0, rows, :], rstd_scr[0, rows, :]
            ln = qhat * cg + cb
            sig_ln = keep[2, rows, :]
            gz = z_ref[0, rows, :]
            sig_g = keep[3, rows, :]
            dya = dycat_scr[rows, 0:D_MODEL]
            dz_ref[0, rows, :] = (dya * (ln * sig_ln) * (sig_g * (1.0 + gz * (1.0 - sig_g)))).astype(BF16)
            dln = (dya * (gz * sig_g)) * (sig_ln * (1.0 + ln * (1.0 - sig_ln)))
            acc_scr[2] += _colsum8(dln * qhat)
            acc_scr[3] += _colsum8(dln)
            dqh = dln * cg
            dq = rstd_q * (dqh - _mean(dqh) - qhat * _mean(dqh * qhat))
            for blk in range(D_MODEL // 128):
                dq_ref[blk, rows, :] = dq[:, blk * 128:(blk + 1) * 128]
        _row_loop(tm, phase9, unroll=FULL_UNROLL)

        @pl.when(i == n_steps - 1)
        def _():
            for qi in range(8):
                scale = 0.5 * inv_d if qi == 6 else 1.0
                sm_ref[qi:qi + 1, :] = jnp.sum(acc_scr[qi], axis=0, keepdims=True) * scale
            lane = lax.broadcasted_iota(jnp.int32, (CHUNK, CHUNK), 1)
            tile = jnp.zeros((CHUNK, CHUNK), F32)
            for h in range(HEADS):
                col = jnp.sum(dbs_acc[:, h * CHUNK:(h + 1) * CHUNK], axis=1, keepdims=True)
                tile = jnp.where(lane == h, col, tile)
            dbs_ref[...] = tile

    tok = lambda i: (i, 0)
    const2 = lambda i: (0, 0)
    return pl.pallas_call(
        body, name="middle",
        grid=(n_steps,),
        out_shape=(jax.ShapeDtypeStruct((4, s_len, D_MODEL), BF16),
                   jax.ShapeDtypeStruct((D_MODEL // 128, s_len, 128), F32),
                   jax.ShapeDtypeStruct((s_len, D_MODEL), F32),
                   jax.ShapeDtypeStruct((s_len, 2 * D_MODEL), BF16),
                   jax.ShapeDtypeStruct((s_len, D_MODEL), BF16),
                   jax.ShapeDtypeStruct((HEADS, CHUNK, CHUNK), F32),
                   jax.ShapeDtypeStruct((8, D_MODEL), F32),
                   jax.ShapeDtypeStruct((CHUNK, CHUNK), F32)),
        in_specs=[pl.BlockSpec((tm, D_MODEL), tok),
                  pl.BlockSpec((4, tm, D_MODEL), lambda i: (0, i, 0)),
                  pl.BlockSpec((D_MODEL // 128, tm, 128), lambda i: (0, i, 0)),
                  pl.BlockSpec((tm, D_MODEL), tok),
                  *[pl.BlockSpec((1, D_MODEL), const2) for _ in range(5)],
                  pl.BlockSpec((1, 3 * D_MODEL), const2),
                  pl.BlockSpec((HEADS, CHUNK, CHUNK), lambda i: (0, 0, 0)),
                  pl.BlockSpec((CHUNK, D_MODEL), const2),
                  pl.BlockSpec((2 * D_MODEL, D_MODEL), const2, pipeline_mode=pl.Buffered(1))],
        out_specs=(pl.BlockSpec((4, tm, D_MODEL), lambda i: (0, i, 0)),
                   pl.BlockSpec((D_MODEL // 128, tm, 128), lambda i: (0, i, 0)),
                   pl.BlockSpec((tm, D_MODEL), tok),
                   pl.BlockSpec((tm, 2 * D_MODEL), tok),
                   pl.BlockSpec((tm, D_MODEL), tok),
                   pl.BlockSpec((HEADS, CHUNK, CHUNK), lambda i: (0, 0, 0)),
                   pl.BlockSpec((8, D_MODEL), const2),
                   pl.BlockSpec((CHUNK, CHUNK), const2)),
        scratch_shapes=[pltpu.VMEM((tm, D_MODEL), BF16),
                        pltpu.VMEM((tm, D_MODEL), F32),
                        pltpu.VMEM((tm, D_MODEL), F32),
                        pltpu.VMEM((tm, 2 * D_MODEL), F32),
                        pltpu.VMEM((tm, D_MODEL), BF16),
                        pltpu.VMEM((tm, D_MODEL), F32),
                        pltpu.VMEM((8, 8, D_MODEL), F32),
                        pltpu.VMEM((CHUNK, D_MODEL), F32),
                        pltpu.VMEM((5, tm, D_MODEL), F32),
                        pltpu.VMEM((2, tm, 1), F32)],
        compiler_params=_params(dimension_semantics=("arbitrary",)),
    )(x, z6, q, target, *ln_rows, mod, w_s, bs_exp, w_out)


def _conv_bwd(kidx, dq, sig, t2, cw4, sm1, dws, dbs, ycat, dy):
    s_len = dq.shape[1]
    tt = TIME_TILE
    n_blocks = D_MODEL // 128
    shp = [(16, D_MODEL), (D_MODEL, 128), (128, 128)]
    nj, nr, nc = 1, 256, D_MODEL
    kt = n_blocks // N_CHIPS
    tk = s_len // kt

    def body(k_ref, dq_ref, sig_ref, t2_ref, cw_ref, sm1_ref, dws_in, dbs_in, ycat_ref, dy_ref,
             dz_ref, pk1_ref, dws_ref, dbs_ref, loss_ref, gout_ref,
             dqpad, b_pk, b_dws, b_dbs, r_pk, r_dws, r_dbs, w_dws, ssem, rsem, *rs_scr):
        del k_ref
        jb = pl.program_id(0)
        x_, y_, c_ = _pos()

        @pl.when(jb == 0)
        def _():
            b_pk[0:8, :] = _onehot_rows(sm1_ref[1:2, :], 4 * x_ + 2 * y_ + c_)
            for r, src in enumerate([2, 3, 4, 5, 0, 6]):
                b_pk[8 + r:9 + r, :] = sm1_ref[src:src + 1, :]
            b_pk[14:16, :] = jnp.zeros((2, D_MODEL), F32)
            b_dws[...] = dws_in[...]
            b_dbs[...] = dbs_in[...]

        for e, step in enumerate([0, 0, (3 * n_blocks) // 8, (6 * n_blocks) // 8]):
            @pl.when(jb == step)
            def _():
                _butterfly_event(e, [b_pk, b_dws, b_dbs], [r_pk, r_dws, r_dbs], ssem, rsem, _all_reduce_partners(),
                                 wires=[None, w_dws, None])

        @pl.when(jb == (6 * n_blocks) // 8)
        def _():
            pk1_ref[...] = b_pk[...]
            dws_ref[...] = b_dws[...]
            dbs_ref[...] = b_dbs[...]
            loss_ref[...] = jnp.broadcast_to(jnp.sum(b_pk[13:14, :], axis=1, keepdims=True), (8, 128))

        zero = jnp.zeros((CONV_PAD, 128), F32)
        dqpad[0:CONV_PAD, :] = zero
        dqpad[s_len + CONV_PAD:s_len + 2 * CONV_PAD, :] = zero
        dqpad[CONV_PAD:s_len + CONV_PAD, :] = dq_ref[0]
        w = cw_ref[0]

        def bwd(i, carry):
            t0 = i * tt
            dp = jnp.zeros((tt, 128), F32)
            for o in range(1, CONV_WIDTH + 1):
                dp = dp + w[CONV_WIDTH - o:CONV_WIDTH - o + 1, :] * dqpad[pl.ds(t0 + o, tt), :]
            dz_ref[0, 0, pl.ds(t0, tt), :] = (dp * sig_ref[0, pl.ds(t0, tt), :]).astype(BF16)
            dz_ref[0, 1, pl.ds(t0, tt), :] = (dp * t2_ref[0, pl.ds(t0, tt), :]).astype(BF16)
            return carry

        n_iter = s_len // tt
        n_groups = min(8, n_iter)
        rows = tk // n_groups

        def conv_and_matmul(add):
            for g in range(n_groups):
                for it in range(g * n_iter // n_groups, (g + 1) * n_iter // n_groups):
                    bwd(it, 0)
                add(_dot_tn(ycat_ref[g * rows:(g + 1) * rows, :], dy_ref[g * rows:(g + 1) * rows, :]))
        _rs_step(jb // kt, jb % kt, kt, nj, nr, nc, conv_and_matmul, gout_ref, rs_scr)

    const2 = lambda j, k_ref: (0, 0)
    grid_spec = pltpu.PrefetchScalarGridSpec(
        num_scalar_prefetch=1,
        grid=(n_blocks,),
        in_specs=[*[pl.BlockSpec((1, s_len, 128), lambda j, k_ref: (j, 0, 0)) for _ in range(3)],
                  pl.BlockSpec((1, 32, 128), lambda j, k_ref: (j // 2, 0, j % 2)),
                  pl.BlockSpec((8, D_MODEL), const2),
                  pl.BlockSpec((D_MODEL, 128), const2),
                  pl.BlockSpec((128, 128), const2),
                  pl.BlockSpec((tk, 512), lambda j, k_ref: (j % kt, _rs_block(j // kt, k_ref[0], nj))),
                  pl.BlockSpec((tk, D_MODEL), lambda j, k_ref: (j % kt, 0))],
        out_specs=(pl.BlockSpec((1, 2, s_len, 128), lambda j, k_ref: (j, 0, 0, 0)),)
        + tuple(pl.BlockSpec(s, const2) for s in shp) + (pl.BlockSpec((8, 128), const2),
                                                         pl.BlockSpec(memory_space=pl.ANY)),
        scratch_shapes=[pltpu.VMEM((s_len + 2 * CONV_PAD, 128), F32)]
        + [pltpu.VMEM(s, F32) for s in shp]
        + [pltpu.VMEM((3,) + shp[0], F32), pltpu.VMEM((3,) + shp[1], BF16), pltpu.VMEM((3,) + shp[2], F32),
           pltpu.VMEM(shp[1], BF16)]
        + [pltpu.SemaphoreType.DMA((9,)), pltpu.SemaphoreType.DMA((9,))] + _rs_scratch(nj, nr, nc))
    return pl.pallas_call(
        body, name="conv_bwd",
        grid_spec=grid_spec,
        out_shape=(jax.ShapeDtypeStruct((n_blocks, 2, s_len, 128), BF16),)
        + tuple(jax.ShapeDtypeStruct(s, F32) for s in shp) + (jax.ShapeDtypeStruct((8, 128), F32),
                                                              jax.ShapeDtypeStruct((nj, 2, nr, nc), F32)),
        compiler_params=_params(dimension_semantics=("arbitrary",)),
    )(kidx, dq, sig, t2, cw4, sm1, dws, dbs, ycat, dy)


def _bwd_in(dz4, dzc, w12, x, dx2, mod, norm_g, dq, p):
    s_len = x.shape[0]
    tm = TOKEN_TILE
    n_steps = s_len // tm
    tt = TIME_TILE
    n_cblocks = D_MODEL // 128
    parts = max(1, n_steps // n_cblocks)
    nsub = max(1, n_cblocks // n_steps)
    cw = 128 * nsub
    tiles = (s_len // tt) // parts

    def body(dz_ref, dzc_ref, w_ref, x_ref, dx2_ref, mod_ref, g_ref, dq_ref, p_ref, gx_ref, sm_ref, dcw_ref,
             dh_scr, acc_scr, ppad, wacc):
        i = pl.program_id(0)
        part = i % parts

        @pl.when(i == 0)
        def _():
            acc_scr[...] = jnp.zeros_like(acc_scr)

        @pl.when(part == 0)
        def _():
            zero = jnp.zeros((CONV_PAD, 128), F32)
            for sub in range(nsub):
                ppad[sub, 0:CONV_PAD, :] = zero
                ppad[sub, s_len + CONV_PAD:s_len + 2 * CONV_PAD, :] = zero
                ppad[sub, CONV_PAD:s_len + CONV_PAD, :] = p_ref[sub]
            wacc[...] = jnp.zeros_like(wacc)

        def dw_tile(tile):
            t0 = pl.multiple_of((part * tiles + tile) * tt, tt)
            for sub in range(nsub):
                dqt = dq_ref[sub, pl.ds(t0, tt), :]
                for o in range(1, CONV_WIDTH + 1):
                    wacc[sub, o - 1] += _colsum8(dqt * ppad[sub, pl.ds(t0 + o, tt), :])
                wacc[sub, CONV_WIDTH] += _colsum8(dqt)

        def dz_cols(j, hf):
            if j < 4:
                return dz_ref[j, :, hf * 512:(hf + 1) * 512]
            return jnp.concatenate([dzc_ref[4 * hf + blk, j - 4] for blk in range(4)], axis=1)

        dots = [(j, hf) for j in range(N_GROUPS) for hf in range(2)]
        dh = jnp.zeros((tm, D_MODEL), F32)
        for d, (j, hf) in enumerate(dots):
            dh = dh + _dot_nt(dz_cols(j, hf), w_ref[2 * _natural_group(j) + hf])
            for tile in range(d * tiles // len(dots), (d + 1) * tiles // len(dots)):
                dw_tile(tile)
        dh_scr[...] = dh

        @pl.when(part == parts - 1)
        def _():
            for sub in range(nsub):
                for k in range(32):
                    dcw_ref[k:k + 1, sub * 128:(sub + 1) * 128] = jnp.sum(wacc[sub, k], axis=0, keepdims=True)

        scale1 = 1.0 + mod_ref[:, D_MODEL:2 * D_MODEL]
        g = g_ref[...]

        def rows_fn(rows):
            xt = x_ref[rows, :]
            r = lax.rsqrt(_mean(xt * xt) + EPS)
            xn = xt * r
            dhr = dh_scr[rows, :]
            acc_scr[0] += _colsum8(dhr)
            acc_scr[1] += _colsum8(dhr * (xn * g))
            acc_scr[2] += _colsum8(dhr * scale1 * xn)
            dxn = dhr * (g * scale1)
            gx_ref[rows, :] = dx2_ref[rows, :] + r * (dxn - xn * _mean(dxn * xn))
        _row_loop(tm, rows_fn, unroll=FULL_UNROLL)

        @pl.when(i == n_steps - 1)
        def _():
            for qi in range(8):
                sm_ref[qi:qi + 1, :] = jnp.sum(acc_scr[qi], axis=0, keepdims=True)

    tok = lambda i: (i, 0)
    const2 = lambda i: (0, 0)
    return pl.pallas_call(
        body, name="bwd_in",
        grid=(n_steps,),
        out_shape=(jax.ShapeDtypeStruct((s_len, D_MODEL), F32), jax.ShapeDtypeStruct((8, D_MODEL), F32),
                   jax.ShapeDtypeStruct((32, D_MODEL), F32)),
        in_specs=[pl.BlockSpec((4, tm, D_MODEL), lambda i: (0, i, 0)),
                  pl.BlockSpec((n_cblocks, 2, tm, 128), lambda i: (0, 0, i, 0)),
                  pl.BlockSpec((12, D_MODEL, 512), lambda i: (0, 0, 0), pipeline_mode=pl.Buffered(1)),
                  pl.BlockSpec((tm, D_MODEL), tok),
                  pl.BlockSpec((tm, D_MODEL), tok),
                  pl.BlockSpec((1, 3 * D_MODEL), const2),
                  pl.BlockSpec((1, D_MODEL), const2),
                  pl.BlockSpec((nsub, s_len, 128), lambda i: (i // parts, 0, 0)),
                  pl.BlockSpec((nsub, s_len, 128), lambda i: (i // parts, 0, 0))],
        out_specs=(pl.BlockSpec((tm, D_MODEL), tok), pl.BlockSpec((8, D_MODEL), const2),
                   pl.BlockSpec((32, cw), lambda i: (0, i // parts))),
        scratch_shapes=[pltpu.VMEM((tm, D_MODEL), F32), pltpu.VMEM((8, 8, D_MODEL), F32),
                        pltpu.VMEM((nsub, s_len + 2 * CONV_PAD, 128), F32), pltpu.VMEM((nsub, 32, 8, 128), F32)],
        compiler_params=_params(dimension_semantics=("arbitrary",)),
    )(dz4, dzc, w12, x, dx2, mod, norm_g, dq, p)


def _rs_shard_block(i, nj):
    return jnp.where(i < 3 * nj, i % 3, 3), jnp.where(i < 3 * nj, i // 3, i - 3 * nj)


def _rs_block(i, k, nj):
    s, j = _rs_shard_block(i, nj)
    return nj * ((k + 1 + s) % N_CHIPS) + j


def _rs_scratch(nj, nr, nc):
    nblk, nfor = N_CHIPS * nj, 3 * nj
    return [pltpu.VMEM((2, 2, nr, nc), F32),
            pltpu.VMEM((nblk, nr, nc), F32),
            pltpu.VMEM((nfor, nr, nc), BF16),
            pltpu.VMEM((nfor, nr, nc), BF16),
            pltpu.VMEM((nj, nr, nc), F32),
            pltpu.SemaphoreType.DMA,
            pltpu.SemaphoreType.DMA((nblk,)), pltpu.SemaphoreType.DMA((nblk,)),
            pltpu.SemaphoreType.DMA((nfor,)), pltpu.SemaphoreType.DMA((nfor,)),
            pltpu.SemaphoreType.DMA, pltpu.SemaphoreType.DMA]


def _rs_step(i, kk, kt, nj, nr, nc, partial_fn, out_ref, scr):
    nblk, nfor = N_CHIPS * nj, 3 * nj
    acc, recv_a, send_b, recv_b, own_ps, lsem, psend, precv, isend, irecv, fsend, frecv = scr
    x, y, c = _pos()
    k = 2 * x + y
    sib = (x, y, 1 - c)
    slot = i % 2

    @pl.when(kk == 0)
    def _():
        acc[slot] = jnp.zeros((2, nr, nc), F32)

    def pair_copy(ii, sl):
        return _rcopy(acc.at[sl, 1 - c], recv_a.at[ii], psend.at[ii], precv.at[ii], sib)

    def pair_sum(ii, sl):
        cp = pair_copy(ii, sl)
        cp.wait_recv()
        cp.wait_send()
        return acc[sl, c] + recv_a[ii]

    def ici_copy(ii, kd, rslot):
        return _rcopy(send_b.at[ii], recv_b.at[rslot], isend.at[ii], irecv.at[rslot], (kd // 2, kd % 2, c))

    def finalize(ii, sl):
        ps = pair_sum(ii, sl)
        s, j = _rs_shard_block(ii, nj)

        @pl.when(s < 3)
        def _():
            send_b[ii] = ps.astype(BF16)
            ici_copy(ii, (k + 1 + s) % N_CHIPS, (2 - s) * nj + j).start()

        @pl.when(s == 3)
        def _():
            own_ps[j] = ps

    @pl.when(jnp.logical_and(kk == kt - 1, i >= 1))
    def _():
        finalize(i - 1, 1 - slot)

    def add_partial(res):
        acc[slot, 0] += res[0:nr]
        acc[slot, 1] += res[nr:2 * nr]
    partial_fn(add_partial)

    @pl.when(kk == kt - 1)
    def _():
        pair_copy(i, slot).start()

        @pl.when(i == nblk - 1)
        def _():
            own_ps[nj - 1] = pair_sum(nblk - 1, (nblk - 1) % 2)
            for r in range(nfor):
                ici_copy(0, k, r).wait_recv()
            for j in range(nj):
                tot = own_ps[j]
                for s in range(3):
                    tot = tot + recv_b[s * nj + j].astype(F32)
                own_ps[j] = tot
            loc = pltpu.make_async_copy(own_ps, out_ref.at[:, c], lsem)
            loc.start()
            swap = _rcopy(own_ps, out_ref.at[:, c], fsend, frecv, sib)
            swap.start()
            loc.wait()
            swap.wait_send()
            _rcopy(own_ps, out_ref.at[:, 1 - c], fsend, frecv, sib).wait_recv()
            for ii in range(nfor):
                ici_copy(ii, k, 0).wait_send()


def _grad_w_in(kidx, h, dz4, dzc, sm2, dcw):
    s_len = h.shape[0]
    tk = min(K_TILE, s_len)
    kt = s_len // tk
    nj, nr, nc = 3, 512, 512
    nblk = N_CHIPS * nj

    def body(k_ref, a_ref, b_ref, bc_ref, sm2_ref, dcw_in, out_ref, pk2_ref, dcwsh_ref, dcb_ref, *scr):
        del k_ref
        b_pk2, b_dcw, r_pk2, r_dcw, bsend, brecv = scr[-6:]
        i = pl.program_id(0)
        kk = pl.program_id(1)
        x, y, c = _pos()
        k = 2 * x + y
        m = _rs_block(i, k, nj)

        @pl.when(jnp.logical_and(kk == 0, i == 0))
        def _():
            b = 4 * x + 2 * y + c
            b_pk2[0:8, :] = _onehot_rows(sm2_ref[0:1, :], b)
            b_pk2[8:16, :] = _onehot_rows(sm2_ref[1:2, :], b)
            b_pk2[16:17, :] = sm2_ref[2:3, :]
            b_pk2[17:24, :] = jnp.zeros((7, D_MODEL), F32)
            b_dcw[...] = dcw_in[...]

        for e in range(4):
            @pl.when(jnp.logical_and(kk == 0, i == (e * nblk) // 4))
            def _():
                _butterfly_event(e, [b_pk2, b_dcw], [r_pk2, r_dcw], bsend, brecv, _all_reduce_partners())

        @pl.when(jnp.logical_and(kk == 0, i == (3 * nblk) // 4))
        def _():
            pk2_ref[...] = b_pk2[...]
            sel = jnp.zeros((32, 256), F32)
            for kc in range(N_CHIPS):
                sel = jnp.where(k == kc, b_dcw[:, kc * 256:(kc + 1) * 256], sel)
            dcwsh_ref[...] = sel
            dcb_ref[...] = b_dcw[31:32, :]

        def partial(add):
            @pl.when(m >= 4)
            def _():
                add(_dot_tn(a_ref[...], b_ref[0]))

            @pl.when(m < 4)
            def _():
                add(_dot_tn(a_ref[...], jnp.concatenate([bc_ref[blk, 0] for blk in range(4)], axis=1)))
        _rs_step(i, kk, kt, nj, nr, nc, partial, out_ref, scr[:-6])

    def dz4_index(i, kk, k_ref):
        m = _rs_block(i, k_ref[0], nj)
        use = m >= 4
        return (jnp.where(use, m // 2 - 2, 0), jnp.where(use, kk, 0), jnp.where(use, m % 2, 0))

    def dzc_index(i, kk, k_ref):
        m = _rs_block(i, k_ref[0], nj)
        use = m < 4
        return (jnp.where(use, m % 2, 0), jnp.where(use, m // 2, 0), jnp.where(use, kk, 0), 0)

    const2 = lambda i, kk, k_ref: (0, 0)
    small_out = [(24, D_MODEL), (32, 256), (1, D_MODEL)]
    small_buf = [(24, D_MODEL), (32, D_MODEL)]
    grid_spec = pltpu.PrefetchScalarGridSpec(
        num_scalar_prefetch=1,
        grid=(nblk, kt),
        in_specs=[pl.BlockSpec((tk, D_MODEL), lambda i, kk, k_ref: (kk, 0)),
                  pl.BlockSpec((1, tk, 512), dz4_index),
                  pl.BlockSpec((4, 1, tk, 128), dzc_index),
                  pl.BlockSpec((8, D_MODEL), const2),
                  pl.BlockSpec((32, D_MODEL), const2)],
        out_specs=(pl.BlockSpec(memory_space=pl.ANY),) + tuple(pl.BlockSpec(s, const2) for s in small_out),
        scratch_shapes=_rs_scratch(nj, nr, nc)
        + [pltpu.VMEM(s, F32) for s in small_buf] + [pltpu.VMEM((3,) + s, F32) for s in small_buf]
        + [pltpu.SemaphoreType.DMA((6,)), pltpu.SemaphoreType.DMA((6,))])
    return pl.pallas_call(
        body, name="grad_w_in",
        grid_spec=grid_spec,
        out_shape=(jax.ShapeDtypeStruct((nj, 2, nr, nc), F32),) + tuple(jax.ShapeDtypeStruct(s, F32) for s in small_out),
        compiler_params=_params(dimension_semantics=("arbitrary", "arbitrary")),
    )(kidx, h, dz4, dzc, sm2, dcw)


def _adamw_math(w, g, m, v):
    m = ADAM_B1 * m + (1.0 - ADAM_B1) * g
    v = ADAM_B2 * v + (1.0 - ADAM_B2) * (g * g)
    m_hat = m / (1.0 - ADAM_B1 ** ADAM_STEP)
    v_hat = v / (1.0 - ADAM_B2 ** ADAM_STEP)
    delta = -ADAM_LR * (m_hat / (jnp.sqrt(v_hat) + ADAM_EPS) + ADAM_WD * w)
    return delta, m, v


def _adamw_blocked(w, m, v, g4, name):
    nj, _, nr, nc = g4.shape

    def body(w_ref, m_ref, v_ref, g_ref, go_ref, d_ref, mo_ref, vo_ref):
        g = g_ref[0, 0]
        d, mn, vn = _adamw_math(w_ref[...], g, m_ref[...], v_ref[...])
        go_ref[...] = g
        d_ref[...] = d
        mo_ref[...] = mn
        vo_ref[...] = vn

    blk = pl.BlockSpec((nr, nc), lambda j, hf: (hf, j))
    return pl.pallas_call(
        body, name=name,
        grid=(nj, 2),
        out_shape=tuple(jax.ShapeDtypeStruct(w.shape, F32) for _ in range(4)),
        in_specs=[blk, blk, blk, pl.BlockSpec((1, 1, nr, nc), lambda j, hf: (j, hf, 0, 0))],
        out_specs=(blk, blk, blk, blk),
        compiler_params=_params(dimension_semantics=("arbitrary", "arbitrary")),
    )(w, m, v, g4)


def _adamw_w_ada(cact, pk1, pk2, w, m, v):
    rb = 256

    def body(cact_ref, pk1_ref, pk2_ref, w_ref, m_ref, v_ref, g_ref, d_ref, mo_ref, vo_ref, gb_ref, dmod, sel_scr):
        @pl.when(pl.program_id(0) == 0)
        def _():
            x, y, _ = _pos()
            k = 2 * x + y
            dmod[:, 0:D_MODEL] = pk2_ref[0:8, :]
            dmod[:, D_MODEL:2 * D_MODEL] = pk2_ref[8:16, :]
            dmod[:, 2 * D_MODEL:3 * D_MODEL] = pk1_ref[0:8, :]
            gb_ref[...] = jnp.sum(dmod[...], axis=0, keepdims=True)
            sel = jnp.zeros((8, 768), F32)
            for kk in range(N_CHIPS):
                sel = jnp.where(k == kk, dmod[:, kk * 768:(kk + 1) * 768], sel)
            sel_scr[...] = sel

        g = _dot_tn(cact_ref[...].astype(BF16), sel_scr[...].astype(BF16))
        d, mn, vn = _adamw_math(w_ref[...], g, m_ref[...], v_ref[...])
        g_ref[...] = g
        d_ref[...] = d
        mo_ref[...] = mn
        vo_ref[...] = vn

    blk = pl.BlockSpec((rb, 768), lambda i: (i, 0))
    const2 = lambda i: (0, 0)
    return pl.pallas_call(
        body, name="adamw_w_ada",
        grid=(D_MODEL // rb,),
        out_shape=tuple(jax.ShapeDtypeStruct(w.shape, F32) for _ in range(4)) + (
            jax.ShapeDtypeStruct((1, 3 * D_MODEL), F32),),
        in_specs=[pl.BlockSpec((8, rb), lambda i: (0, i)), pl.BlockSpec((16, D_MODEL), const2),
                  pl.BlockSpec((24, D_MODEL), const2), blk, blk, blk],
        out_specs=(blk, blk, blk, blk, pl.BlockSpec((1, 3 * D_MODEL), const2)),
        scratch_shapes=[pltpu.VMEM((8, 3 * D_MODEL), F32), pltpu.VMEM((8, 768), F32)],
        compiler_params=_params(dimension_semantics=("arbitrary",)),
    )(cact, pk1, pk2, w, m, v)


def _adamw_small(items):
    n = len(items)

    def body(*refs):
        ins, outs = refs[:4 * n], refs[4 * n:]
        for i in range(n):
            w_ref, g_ref, m_ref, v_ref = ins[4 * i:4 * i + 4]
            d, mn, vn = _adamw_math(w_ref[...], g_ref[...], m_ref[...], v_ref[...])
            outs[3 * i][...] = d
            outs[3 * i + 1][...] = mn
            outs[3 * i + 2][...] = vn

    flat = [a for it in items for a in it]
    outs = pl.pallas_call(
        body, name="adamw_small",
        out_shape=tuple(jax.ShapeDtypeStruct(it[0].shape, F32) for it in items for _ in range(3)),
        in_specs=[_vmem()] * (4 * n),
        out_specs=tuple(_vmem() for _ in range(3 * n)),
        compiler_params=_params(),
    )(*flat)
    return [tuple(outs[3 * i:3 * i + 3]) for i in range(n)]


def kernel(x, c, w_ada, b_ada, norm_g, w_in, conv_w, conv_b, conv_ln_g, conv_ln_b, sg_ln_g, sg_ln_b, w_s, b_s, w_out, final_g, loss_target, m_w_ada, m_b_ada, m_norm_g, m_w_in, m_conv_w, m_conv_b, m_conv_ln_g, m_conv_ln_b, m_sg_ln_g, m_sg_ln_b, m_w_s, m_b_s, m_w_out, m_final_g, v_w_ada, v_b_ada, v_norm_g, v_w_in, v_conv_w, v_conv_b, v_conv_ln_g, v_conv_ln_b, v_sg_ln_g, v_sg_ln_b, v_w_s, v_b_s, v_w_out, v_final_g):
    s_len = x.shape[1]
    x2d = x[0]
    tgt = loss_target[0]
    row = lambda a: a.reshape(1, -1)

    kidx = (2 * lax.axis_index("x") + lax.axis_index("y")).astype(jnp.int32).reshape(1)
    cw_sh = jnp.pad(conv_w.reshape(CONV_WIDTH, 256), ((0, 1), (0, 0)))
    h, z6, mod, cact, cw4, w_in12 = _front(kidx, x2d, c, w_ada[0], b_ada, norm_g, w_in[0], cw_sh)
    w12 = w_in12.reshape(12, D_MODEL, 512)
    q, p_cm, sig_cm, t2_cm, w_out4 = _conv_fwd(z6, cw4, conv_b, w_out[0])
    w_out_full = w_out4.reshape(2 * D_MODEL, D_MODEL)
    ln_rows = (conv_ln_g, conv_ln_b, sg_ln_g, sg_ln_b, row(final_g))
    bs_exp = jnp.repeat(b_s[0].T, CHUNK, axis=1)
    dz4, dq, dx2, ycat, dy, dws, sm1, dbs = _middle(x2d, z6, q, tgt, ln_rows, mod, w_s[0], bs_exp, w_out_full)
    dzc, pk1, dws_r, dbs_r, loss_t, g_w_out4 = _conv_bwd(
        kidx, dq, sig_cm, t2_cm, cw4, sm1, dws.reshape(D_MODEL, CHUNK), dbs, ycat, dy)
    grad_x, sm2, dcw = _bwd_in(dz4, dzc, w12, x2d, dx2, mod, norm_g, dq, p_cm)
    g_w_in4, pk2, dcw_sh, dcb = _grad_w_in(kidx, h, dz4, dzc, sm2, dcw)

    g_w_in, d_w_in, nm_w_in, nv_w_in = _adamw_blocked(w_in[0], m_w_in[0], v_w_in[0], g_w_in4, "adamw_w_in")
    g_w_out, d_w_out, nm_w_out, nv_w_out = _adamw_blocked(w_out[0], m_w_out[0], v_w_out[0], g_w_out4, "adamw_w_out")
    g_w_ada, d_w_ada, nm_w_ada, nv_w_ada, g_b_ada = _adamw_w_ada(cact, pk1, pk2, w_ada[0], m_w_ada[0], v_w_ada[0])

    g_norm_g = pk2[16:17]
    g_cln_g, g_cln_b, g_sln_g, g_sln_b, g_final = (pk1[8 + i:9 + i] for i in range(5))
    loss = loss_t[0, 0]
    g_conv_w = dcw_sh[:CONV_WIDTH]
    g_w_s = dws_r
    g_b_s = dbs_r[:, :HEADS].T
    small = [
        (b_ada, g_b_ada, m_b_ada, v_b_ada),
        (norm_g, g_norm_g, m_norm_g, v_norm_g),
        (conv_w.reshape(CONV_WIDTH, 256), g_conv_w, m_conv_w.reshape(CONV_WIDTH, 256), v_conv_w.reshape(CONV_WIDTH, 256)),
        (conv_b, dcb, m_conv_b, v_conv_b),
        (conv_ln_g, g_cln_g, m_conv_ln_g, v_conv_ln_g),
        (conv_ln_b, g_cln_b, m_conv_ln_b, v_conv_ln_b),
        (sg_ln_g, g_sln_g, m_sg_ln_g, v_sg_ln_g),
        (sg_ln_b, g_sln_b, m_sg_ln_b, v_sg_ln_b),
        (w_s.reshape(D_MODEL, CHUNK), g_w_s, m_w_s.reshape(D_MODEL, CHUNK), v_w_s.reshape(D_MODEL, CHUNK)),
        (b_s[0], g_b_s, m_b_s[0], v_b_s[0]),
        (row(final_g), g_final, row(m_final_g), row(v_final_g)),
    ]
    upd = _adamw_small(small)

    shapes = [w_ada.shape, b_ada.shape, norm_g.shape, w_in.shape, conv_w.shape, conv_b.shape, conv_ln_g.shape,
              conv_ln_b.shape, sg_ln_g.shape, sg_ln_b.shape, w_s.shape, b_s.shape, w_out.shape, final_g.shape]
    grads = [g_w_ada, g_b_ada, g_norm_g, g_w_in, g_conv_w, dcb, g_cln_g, g_cln_b, g_sln_g, g_sln_b, g_w_s, g_b_s,
             g_w_out, g_final]
    big = {0: (d_w_ada, nm_w_ada, nv_w_ada), 3: (d_w_in, nm_w_in, nv_w_in), 12: (d_w_out, nm_w_out, nv_w_out)}
    small_pos = [1, 2, 4, 5, 6, 7, 8, 9, 10, 11, 13]
    trip = [None] * 14
    for i, t in big.items():
        trip[i] = t
    for i, t in zip(small_pos, upd):
        trip[i] = t
    fit = lambda arrs: [a.reshape(s) for a, s in zip(arrs, shapes)]
    return (loss, grad_x.reshape(x.shape), *fit(grads), *fit([t[0] for t in trip]), *fit([t[1] for t in trip]),
            *fit([t[2] for t in trip]))
```

```python
import jax
import jax.numpy as jnp
from jax import lax
from jax.experimental import pallas as pl
from jax.experimental.pallas import tpu as pltpu

F32 = jnp.float32
BF16 = jnp.bfloat16
MESH = pl.DeviceIdType.MESH

D_MODEL = 1024
N_CHIPS = 4
HEADS = 8
CHUNK = 128
CONV_WIDTH = 31
CONV_HALF = CONV_WIDTH // 2
CONV_PAD = 16
EPS = 1e-6
ADAM_LR = 0.001
ADAM_B1 = 0.9
ADAM_B2 = 0.999
ADAM_EPS = 1e-08
ADAM_WD = 0.01
ADAM_STEP = 10

V7X_VMEM_BYTES = 64 * 1024 * 1024
VMEM_LIMIT = V7X_VMEM_BYTES - 8 * 1024 * 1024
ROWS = 16
UNROLL = 8
TOKEN_TILE = 256
FULL_UNROLL = TOKEN_TILE // ROWS
TIME_TILE = 128
K_TILE = 2048

N_GROUPS = 6


def _natural_group(j):
    return (j + 2) % N_GROUPS


def _pos():
    return lax.axis_index("x"), lax.axis_index("y"), lax.axis_index("c")


def _rcopy(src, dst, ssem, rsem, dev):
    return pltpu.make_async_remote_copy(src_ref=src, dst_ref=dst, send_sem=ssem, recv_sem=rsem,
                                        device_id=dev, device_id_type=MESH)


def _vmem():
    return pl.BlockSpec(memory_space=pltpu.VMEM)


def _params(**kw):
    return pltpu.CompilerParams(vmem_limit_bytes=VMEM_LIMIT, **kw)


def _sigmoid(v):
    return 0.5 * jnp.tanh(0.5 * v) + 0.5


def _row_loop(n_rows, body, unroll=1):
    def step(r, carry):
        body(pl.ds(pl.multiple_of(r * ROWS, ROWS), ROWS))
        return carry
    lax.fori_loop(0, n_rows // ROWS, step, 0, unroll=unroll)


def _colsum8(v):
    return v.reshape(v.shape[0] // 8, 8, v.shape[1]).sum(axis=0)


def _mean(v):
    return jnp.mean(v, axis=-1, keepdims=True)


def _dot_nn(a, b):
    return jnp.dot(a, b, preferred_element_type=F32)


def _dot_nt(a, b):
    return lax.dot_general(a, b, (((1,), (1,)), ((), ())), preferred_element_type=F32)


def _dot_tn(a, b):
    return lax.dot_general(a, b, (((0,), (0,)), ((), ())), preferred_element_type=F32)


def _remote_chip(k, r):
    return jnp.bitwise_xor(k, r + 1)


def _front(kidx, x, c, w_ada, b_ada, norm_g, w_in, cw):
    s_len = x.shape[0]
    tmh = min(512, s_len)
    tmz = min(1024, s_len)
    nh = s_len // tmh
    nz = s_len // tmz
    n_steps = nh + 12 * nz

    def remote_block(q):
        return jnp.where(q < 6, q % 2, 2), jnp.where(q < 6, q // 2, q - 6)

    def block_of(i, k):
        r, j = remote_block(jnp.maximum(i - 3, 0))
        return jnp.where(i < 3, 3 * k + i, 3 * _remote_chip(k, r) + j)

    def body(k_ref, x_ref, c_ref, bada_ref, g_ref, cw_ref, wada_hbm, win_hbm,
             h_ref, z_ref, mod_ref, cact_ref, cw4_ref, w12_hbm,
             h_all, wbuf, stage, wada_v, cslab, mslab, cw4_s,
             lsem, csend, crecv, msend, mrecv, wsend, wrecv, isend, irecv, fsend, frecv, osem):
        del k_ref
        t = pl.program_id(0)
        x_, y_, c_ = _pos()
        k = 2 * x_ + y_
        b = 4 * x_ + 2 * y_ + c_
        sib = (x_, y_, 1 - c_)

        def dev_of(r):
            kk = _remote_chip(k, r)
            return (kk // 2, kk % 2, c_)

        def ici(q, kk):
            r, j = remote_block(q)
            return _rcopy(wbuf.at[3 * kk + j, c_], wbuf.at[3 * kk + j, c_], isend.at[q], irecv.at[q], dev_of(r))

        def fwd(q, hf):
            r, j = remote_block(q)
            blk = 3 * _remote_chip(k, r) + j
            return _rcopy(wbuf.at[blk, hf], wbuf.at[blk, hf], fsend.at[q], frecv.at[q], sib)

        def c_copy(q, src):
            d = jnp.bitwise_xor(b, q)
            return _rcopy(cslab.at[src], cslab.at[src], csend.at[q - 1], crecv.at[q - 1], (d // 4, (d // 2) % 2, d % 2))

        def m_copy(r, kk):
            return _rcopy(mslab.at[kk], mslab.at[kk], msend.at[r], mrecv.at[r], dev_of(r))

        def cw_copy(r, kk):
            return _rcopy(cw4_s.at[kk], cw4_s.at[kk], wsend.at[r], wrecv.at[r], dev_of(r))

        def to_hbm(i):
            m = block_of(i, k)
            return pltpu.make_async_copy(wbuf.at[m], w12_hbm.at[m], osem.at[i])

        @pl.when(t == 0)
        def _():
            ld_w = pltpu.make_async_copy(win_hbm, stage, lsem.at[0])
            ld_w.start()
            ld_a = pltpu.make_async_copy(wada_hbm, wada_v, lsem.at[1])
            ld_a.start()
            cslab[b] = jnp.broadcast_to(c_ref[...], (8, D_MODEL))
            for q in range(1, 8):
                c_copy(q, b).start()
            cw4_s[k] = cw_ref[...]
            for r in range(3):
                cw_copy(r, k).start()
            ld_w.wait()
            for j in range(3):
                for hf in range(2):
                    wbuf[3 * k + j, hf] = stage[hf * 512:(hf + 1) * 512, j * 512:(j + 1) * 512].astype(BF16)
            for i in range(3):
                to_hbm(i).start()
            for q in range(1, 8):
                c_copy(q, jnp.bitwise_xor(b, q)).wait_recv()
            row = lax.broadcasted_iota(jnp.int32, (8, D_MODEL), 0)
            call = jnp.zeros((8, D_MODEL), F32)
            for d in range(8):
                call = jnp.where(row == d, cslab[d], call)
            cact = call * _sigmoid(call)
            cact_ref[...] = cact
            ld_a.wait()
            mslab[k] = _dot_nn(cact.astype(BF16), wada_v[...].astype(BF16))
            for r in range(3):
                m_copy(r, k).start()
            for q in range(9):
                ici(q, k).start()
            for r in range(3):
                m_copy(r, _remote_chip(k, r)).wait_recv()
            row8 = lax.broadcasted_iota(jnp.int32, (8, 768), 0)
            for kk in range(N_CHIPS):
                piece = jnp.sum(jnp.where(row8 == b, mslab[kk], 0.0), axis=0, keepdims=True)
                mod_ref[:, kk * 768:(kk + 1) * 768] = piece + bada_ref[:, kk * 768:(kk + 1) * 768]
            for r in range(3):
                cw_copy(r, _remote_chip(k, r)).wait_recv()
            cw4_ref[...] = cw4_s[...]

        @pl.when(t < nh)
        def _():
            shift = mod_ref[:, 0:D_MODEL]
            scale1 = 1.0 + mod_ref[:, D_MODEL:2 * D_MODEL]
            g = g_ref[...]
            base = t * tmh

            def rows_fn(rows):
                xt = x_ref[rows, :]
                r = lax.rsqrt(_mean(xt * xt) + EPS)
                hv = ((xt * r * g) * scale1 + shift).astype(BF16)
                h_ref[rows, :] = hv
                h_all[pl.ds(pl.multiple_of(base + rows.start, ROWS), ROWS), :] = hv
            _row_loop(tmh, rows_fn, unroll=UNROLL)

        @pl.when(t >= nh)
        def _():
            u = t - nh
            i = u // nz
            rt = u % nz
            @pl.when(jnp.logical_and(rt == 0, jnp.logical_and(i >= 2, i <= 10)))
            def _():
                q = i - 2
                r, _ = remote_block(q)
                ici(q, _remote_chip(k, r)).wait_recv()
                fwd(q, c_).start()

            @pl.when(jnp.logical_and(rt == 0, i >= 3))
            def _():
                fwd(i - 3, 1 - c_).wait_recv()
                to_hbm(i).start()
            m = block_of(i, k)
            hb = h_all[pl.ds(pl.multiple_of(rt * tmz, tmz), tmz), :]
            z_ref[0] = _dot_nn(hb, wbuf[m].reshape(D_MODEL, 512))

        @pl.when(t == n_steps - 1)
        def _():
            for q in range(1, 8):
                c_copy(q, b).wait_send()
            for r in range(3):
                m_copy(r, k).wait_send()
                cw_copy(r, k).wait_send()
            for q in range(9):
                ici(q, k).wait_send()
                fwd(q, c_).wait_send()
            for i in range(12):
                to_hbm(i).wait()

    def z_index(t, k_ref):
        u = jnp.maximum(t - nh, 0)
        m = block_of(u // nz, k_ref[0])
        return ((m // 2 + 4) % N_GROUPS, u % nz, m % 2)

    tok = lambda t, k_ref: (jnp.minimum(t, nh - 1), 0)
    const2 = lambda t, k_ref: (0, 0)
    grid_spec = pltpu.PrefetchScalarGridSpec(
        num_scalar_prefetch=1,
        grid=(n_steps,),
        in_specs=[pl.BlockSpec((tmh, D_MODEL), tok),
                  pl.BlockSpec((1, D_MODEL), const2),
                  pl.BlockSpec((1, 3 * D_MODEL), const2),
                  pl.BlockSpec((1, D_MODEL), const2),
                  pl.BlockSpec((32, 256), const2),
                  pl.BlockSpec(memory_space=pl.ANY),
                  pl.BlockSpec(memory_space=pl.ANY)],
        out_specs=(pl.BlockSpec((tmh, D_MODEL), tok),
                   pl.BlockSpec((1, tmz, 512), z_index),
                   pl.BlockSpec((1, 3 * D_MODEL), const2),
                   pl.BlockSpec((8, D_MODEL), const2),
                   pl.BlockSpec((N_CHIPS, 32, 256), lambda t, k_ref: (0, 0, 0)),
                   pl.BlockSpec(memory_space=pl.ANY)),
        scratch_shapes=[pltpu.VMEM((s_len, D_MODEL), BF16),
                        pltpu.VMEM((12, 2, 512, 512), BF16),
                        pltpu.VMEM((D_MODEL, 1536), F32),
                        pltpu.VMEM((D_MODEL, 768), F32),
                        pltpu.VMEM((8, 8, D_MODEL), F32),
                        pltpu.VMEM((N_CHIPS, 8, 768), F32),
                        pltpu.VMEM((N_CHIPS, 32, 256), F32),
                        pltpu.SemaphoreType.DMA((2,)),
                        pltpu.SemaphoreType.DMA((7,)), pltpu.SemaphoreType.DMA((7,)),
                        pltpu.SemaphoreType.DMA((3,)), pltpu.SemaphoreType.DMA((3,)),
                        pltpu.SemaphoreType.DMA((3,)), pltpu.SemaphoreType.DMA((3,)),
                        pltpu.SemaphoreType.DMA((9,)), pltpu.SemaphoreType.DMA((9,)),
                        pltpu.SemaphoreType.DMA((9,)), pltpu.SemaphoreType.DMA((9,)),
                        pltpu.SemaphoreType.DMA((12,))])
    return pl.pallas_call(
        body, name="front",
        grid_spec=grid_spec,
        out_shape=(jax.ShapeDtypeStruct((s_len, D_MODEL), BF16),
                   jax.ShapeDtypeStruct((N_GROUPS, s_len, D_MODEL), F32),
                   jax.ShapeDtypeStruct((1, 3 * D_MODEL), F32),
                   jax.ShapeDtypeStruct((8, D_MODEL), F32),
                   jax.ShapeDtypeStruct((N_CHIPS, 32, 256), F32),
                   jax.ShapeDtypeStruct((12, 2, 512, 512), BF16)),
        compiler_params=_params(dimension_semantics=("arbitrary",)),
    )(kidx, x, c, b_ada, norm_g, cw, w_ada, w_in)


def _all_reduce_partners():
    x, y, c = _pos()
    return [(x, y, 1 - c), (x, 1 - y, c), (1 - x, y, c)]


def _butterfly_event(e, bufs, recvs, ssem, rsem, partners, wires=None):
    n = len(bufs)
    wires = wires or [None] * n

    def copies(s):
        return [_rcopy(buf if wire is None else wire, recv.at[s], ssem.at[s * n + i], rsem.at[s * n + i], partners[s])
                for i, (buf, recv, wire) in enumerate(zip(bufs, recvs, wires))]

    if e > 0:
        for cp in copies(e - 1):
            cp.wait()
        for buf, recv, wire in zip(bufs, recvs, wires):
            mine = buf[...] if wire is None else wire[...].astype(F32)
            buf[...] = mine + recv[e - 1].astype(F32)
    if e < len(partners):
        for buf, wire in zip(bufs, wires):
            if wire is not None:
                wire[...] = buf[...].astype(BF16)
        for cp in copies(e):
            cp.start()


def _onehot_rows(v, b):
    row = lax.broadcasted_iota(jnp.int32, (8, v.shape[1]), 0)
    return jnp.where(row == b, jnp.broadcast_to(v, (8, v.shape[1])), 0.0)


def _conv_fwd(z6, cw4, conv_b, w_out):
    s_len = z6.shape[1]
    tt = TIME_TILE
    n_blocks = D_MODEL // 128

    def body(z_ref, cw_ref, cb_ref, wout_hbm, q_ref, p_ref, sig_ref, t2_ref, wout4_hbm,
             ppad, stage, wbuf, lsem, isend, irecv, fsend, frecv, osem):
        jb = pl.program_id(0)
        x_, y_, c_ = _pos()
        k = 2 * x_ + y_
        sib = (x_, y_, 1 - c_)

        def ici(r, kk):
            rk = _remote_chip(k, r)
            return _rcopy(wbuf.at[kk, c_], wbuf.at[kk, c_], isend.at[r], irecv.at[r], (rk // 2, rk % 2, c_))

        def fwd(r, hf):
            kk = _remote_chip(k, r)
            return _rcopy(wbuf.at[kk, hf], wbuf.at[kk, hf], fsend.at[r], frecv.at[r], sib)

        @pl.when(jb == 0)
        def _():
            ld = pltpu.make_async_copy(wout_hbm, stage, lsem)
            ld.start()
            ld.wait()
            for hf in range(2):
                wbuf[k, hf] = stage[hf * 256:(hf + 1) * 256, :].astype(BF16)
            for r in range(3):
                ici(r, k).start()

        @pl.when(jb == (5 * n_blocks) // 8)
        def _():
            for r in range(3):
                ici(r, _remote_chip(k, r)).wait_recv()
                fwd(r, c_).start()

        zero = jnp.zeros((CONV_PAD, 128), F32)
        ppad[0:CONV_PAD, :] = zero
        ppad[s_len + CONV_PAD:s_len + 2 * CONV_PAD, :] = zero

        def fill(i, carry):
            t0 = pl.multiple_of(i * tt, tt)
            sig = _sigmoid(z_ref[1, pl.ds(t0, tt), :])
            p = z_ref[0, pl.ds(t0, tt), :] * sig
            ppad[pl.ds(CONV_PAD + t0, tt), :] = p
            p_ref[0, pl.ds(t0, tt), :] = p
            sig_ref[0, pl.ds(t0, tt), :] = sig
            t2_ref[0, pl.ds(t0, tt), :] = p * (1.0 - sig)
            return carry
        lax.fori_loop(0, s_len // tt, fill, 0)
        w = cw_ref[0]
        bias = cb_ref[...]

        def conv(i, carry):
            t0 = pl.multiple_of(i * tt, tt)
            acc = jnp.broadcast_to(bias, (tt, 128))
            for o in range(1, CONV_WIDTH + 1):
                acc = acc + w[o - 1:o, :] * ppad[pl.ds(t0 + o, tt), :]
            q_ref[0, pl.ds(t0, tt), :] = acc
            return carry
        lax.fori_loop(0, s_len // tt, conv, 0)

        @pl.when(jb == n_blocks - 1)
        def _():
            for r in range(3):
                fwd(r, 1 - c_).wait_recv()
            out = pltpu.make_async_copy(wbuf, wout4_hbm, osem)
            out.start()
            for r in range(3):
                ici(r, k).wait_send()
                fwd(r, c_).wait_send()
            out.wait()

    return pl.pallas_call(
        body, name="conv_fwd",
        grid=(n_blocks,),
        out_shape=tuple(jax.ShapeDtypeStruct((n_blocks, s_len, 128), F32) for _ in range(4))
        + (jax.ShapeDtypeStruct((N_CHIPS, 2, 256, D_MODEL), BF16),),
        in_specs=[pl.BlockSpec((2, s_len, 128), lambda j: (2, 0, j)),
                  pl.BlockSpec((1, 32, 128), lambda j: (j // 2, 0, j % 2)),
                  pl.BlockSpec((1, 128), lambda j: (0, j)),
                  pl.BlockSpec(memory_space=pl.ANY)],
        out_specs=tuple(pl.BlockSpec((1, s_len, 128), lambda j: (j, 0, 0)) for _ in range(4))
        + (pl.BlockSpec(memory_space=pl.ANY),),
        scratch_shapes=[pltpu.VMEM((s_len + 2 * CONV_PAD, 128), F32),
                        pltpu.VMEM((512, D_MODEL), F32),
                        pltpu.VMEM((N_CHIPS, 2, 256, D_MODEL), BF16),
                        pltpu.SemaphoreType.DMA,
                        pltpu.SemaphoreType.DMA((3,)), pltpu.SemaphoreType.DMA((3,)),
                        pltpu.SemaphoreType.DMA((3,)), pltpu.SemaphoreType.DMA((3,)),
                        pltpu.SemaphoreType.DMA],
        compiler_params=_params(dimension_semantics=("arbitrary",)),
    )(z6, cw4, conv_b, w_out)


def _middle(x, z6, q, target, ln_rows, mod, w_s, bs_exp, w_out):
    s_len = x.shape[0]
    tm = TOKEN_TILE
    n_steps = s_len // tm
    n_chunks = tm // CHUNK
    inv_d = 1.0 / D_MODEL

    def body(x_ref, z_ref, q_ref, tgt_ref, cg_ref, cb_ref, sg_ref, sb_ref, fg_ref, mod_ref, ws_ref, bs_ref, wout_ref,
             dz_ref, dq_ref, dx2_ref, ycat_ref, dy_ref, dws_ref, sm_ref, dbs_ref,
             vl_scr, vm_scr, y_scr, dycat_scr, dvm_scr, dvl_scr, acc_scr, dbs_acc, keep, rstd_scr):
        i = pl.program_id(0)

        @pl.when(i == 0)
        def _():
            acc_scr[...] = jnp.zeros_like(acc_scr)
            dbs_acc[...] = jnp.zeros_like(dbs_acc)
            dws_ref[...] = jnp.zeros_like(dws_ref)

        cg, cb, sg, sb, fg = cg_ref[...], cb_ref[...], sg_ref[...], sb_ref[...], fg_ref[...]
        gm = mod_ref[:, 2 * D_MODEL:3 * D_MODEL]

        def norm_stats(t):
            c = t - _mean(t)
            rstd = lax.rsqrt(_mean(c * c) + EPS)
            return c * rstd, rstd

        def phase1(rows):
            qhat, rstd_q = norm_stats(jnp.concatenate([q_ref[blk, rows, :] for blk in range(D_MODEL // 128)], axis=1))
            ln = qhat * cg + cb
            gz = z_ref[0, rows, :]
            sig_ln = _sigmoid(ln)
            sig_g = _sigmoid(gz)
            ycat_ref[rows, 0:D_MODEL] = ((ln * sig_ln) * (gz * sig_g)).astype(BF16)
            vhat, rstd_v = norm_stats(z_ref[2, rows, :])
            vl_scr[rows, :] = (vhat * sg + sb).astype(BF16)
            keep[0, rows, :] = qhat
            keep[1, rows, :] = vhat
            keep[2, rows, :] = sig_ln
            keep[3, rows, :] = sig_g
            rstd_scr[0, rows, :] = rstd_q
            rstd_scr[1, rows, :] = rstd_v
        _row_loop(tm, phase1, unroll=FULL_UNROLL)

        for ch in range(n_chunks):
            r0 = ch * CHUNK
            for h in range(HEADS):
                c0 = h * CHUNK
                vm_scr[r0:r0 + CHUNK, c0:c0 + CHUNK] = (
                    _dot_nn(ws_ref[h].astype(BF16), vl_scr[r0:r0 + CHUNK, c0:c0 + CHUNK]) + bs_ref[:, c0:c0 + CHUNK])

        def phase3(rows):
            bg = z_ref[3, rows, :]
            sig_b = _sigmoid(bg)
            keep[4, rows, :] = sig_b
            ycat_ref[rows, D_MODEL:2 * D_MODEL] = (z_ref[1, rows, :] * vm_scr[rows, :] * (bg * sig_b)).astype(BF16)
        _row_loop(tm, phase3, unroll=FULL_UNROLL)

        y_scr[...] = _dot_nn(ycat_ref[...], wout_ref[...])

        def phase5(rows):
            y = y_scr[rows, :]
            x2 = x_ref[rows, :] + gm * y
            r2 = lax.rsqrt(_mean(x2 * x2) + EPS)
            xn2 = x2 * r2
            diff = xn2 * fg - tgt_ref[rows, :]
            acc_scr[6] += _colsum8(diff * diff)
            dout = diff * inv_d
            acc_scr[0] += _colsum8(dout * xn2)
            dxn = dout * fg
            dx2 = r2 * (dxn - xn2 * _mean(dxn * xn2))
            dx2_ref[rows, :] = dx2
            acc_scr[1] += _colsum8(dx2 * y)
            dy_ref[rows, :] = (dx2 * gm).astype(BF16)
        _row_loop(tm, phase5, unroll=FULL_UNROLL)

        dycat_scr[...] = _dot_nt(dy_ref[...], wout_ref[...])

        def phase7(rows):
            dyb = dycat_scr[rows, D_MODEL:2 * D_MODEL]
            u = z_ref[1, rows, :]
            bg = z_ref[3, rows, :]
            vm = vm_scr[rows, :]
            sig = keep[4, rows, :]
            silu = bg * sig
            dz_ref[1, rows, :] = (dyb * vm * silu).astype(BF16)
            dvm = dyb * u * silu
            dz_ref[3, rows, :] = (dyb * u * vm * (sig * (1.0 + bg * (1.0 - sig)))).astype(BF16)
            dvm_scr[rows, :] = dvm.astype(BF16)
            pos = pl.ds(pl.multiple_of(rows.start % CHUNK, ROWS), ROWS)
            dbs_acc[pos, :] += dvm
        _row_loop(tm, phase7, unroll=FULL_UNROLL)

        for ch in range(n_chunks):
            r0 = ch * CHUNK
            for h in range(HEADS):
                c0 = h * CHUNK
                dvm_b = dvm_scr[r0:r0 + CHUNK, c0:c0 + CHUNK]
                dws_ref[h] += _dot_nt(dvm_b, vl_scr[r0:r0 + CHUNK, c0:c0 + CHUNK])
                dvl_scr[r0:r0 + CHUNK, c0:c0 + CHUNK] = _dot_tn(ws_ref[h].astype(BF16), dvm_b)

        def phase9(rows):
            vhat, rstd_v = keep[1, rows, :], rstd_scr[1, rows, :]
            dvl = dvl_scr[rows, :]
            acc_scr[4] += _colsum8(dvl * vhat)
            acc_scr[5] += _colsum8(dvl)
            dvh = dvl * sg
            dz_ref[2, rows, :] = (rstd_v * (dvh - _mean(dvh) - vhat * _mean(dvh * vhat))).astype(BF16)
            qhat, rstd_q = keep[0, rows, :], rstd_scr[0, rows, :]
            ln = qhat * cg + cb
            sig_ln = keep[2, rows, :]
            gz = z_ref[0, rows, :]
            sig_g = keep[3, rows, :]
            dya = dycat_scr[rows, 0:D_MODEL]
            dz_ref[0, rows, :] = (dya * (ln * sig_ln) * (sig_g * (1.0 + gz * (1.0 - sig_g)))).astype(BF16)
            dln = (dya * (gz * sig_g)) * (sig_ln * (1.0 + ln * (1.0 - sig_ln)))
            acc_scr[2] += _colsum8(dln * qhat)
            acc_scr[3] += _colsum8(dln)
            dqh = dln * cg
            dq = rstd_q * (dqh - _mean(dqh) - qhat * _mean(dqh * qhat))
            for blk in range(D_MODEL // 128):
                dq_ref[blk, rows, :] = dq[:, blk * 128:(blk + 1) * 128]
        _row_loop(tm, phase9, unroll=FULL_UNROLL)

        @pl.when(i == n_steps - 1)
        def _():
            for qi in range(8):
                scale = 0.5 * inv_d if qi == 6 else 1.0
                sm_ref[qi:qi + 1, :] = jnp.sum(acc_scr[qi], axis=0, keepdims=True) * scale
            lane = lax.broadcasted_iota(jnp.int32, (CHUNK, CHUNK), 1)
            tile = jnp.zeros((CHUNK, CHUNK), F32)
            for h in range(HEADS):
                col = jnp.sum(dbs_acc[:, h * CHUNK:(h + 1) * CHUNK], axis=1, keepdims=True)
                tile = jnp.where(lane == h, col, tile)
            dbs_ref[...] = tile

    tok = lambda i: (i, 0)
    const2 = lambda i: (0, 0)
    return pl.pallas_call(
        body, name="middle",
        grid=(n_steps,),
        out_shape=(jax.ShapeDtypeStruct((4, s_len, D_MODEL), BF16),
                   jax.ShapeDtypeStruct((D_MODEL // 128, s_len, 128), F32),
                   jax.ShapeDtypeStruct((s_len, D_MODEL), F32),
                   jax.ShapeDtypeStruct((s_len, 2 * D_MODEL), BF16),
                   jax.ShapeDtypeStruct((s_len, D_MODEL), BF16),
                   jax.ShapeDtypeStruct((HEADS, CHUNK, CHUNK), F32),
                   jax.ShapeDtypeStruct((8, D_MODEL), F32),
                   jax.ShapeDtypeStruct((CHUNK, CHUNK), F32)),
        in_specs=[pl.BlockSpec((tm, D_MODEL), tok),
                  pl.BlockSpec((4, tm, D_MODEL), lambda i: (0, i, 0)),
                  pl.BlockSpec((D_MODEL // 128, tm, 128), lambda i: (0, i, 0)),
                  pl.BlockSpec((tm, D_MODEL), tok),
                  *[pl.BlockSpec((1, D_MODEL), const2) for _ in range(5)],
                  pl.BlockSpec((1, 3 * D_MODEL), const2),
                  pl.BlockSpec((HEADS, CHUNK, CHUNK), lambda i: (0, 0, 0)),
                  pl.BlockSpec((CHUNK, D_MODEL), const2),
                  pl.BlockSpec((2 * D_MODEL, D_MODEL), const2, pipeline_mode=pl.Buffered(1))],
        out_specs=(pl.BlockSpec((4, tm, D_MODEL), lambda i: (0, i, 0)),
                   pl.BlockSpec((D_MODEL // 128, tm, 128), lambda i: (0, i, 0)),
                   pl.BlockSpec((tm, D_MODEL), tok),
                   pl.BlockSpec((tm, 2 * D_MODEL), tok),
                   pl.BlockSpec((tm, D_MODEL), tok),
                   pl.BlockSpec((HEADS, CHUNK, CHUNK), lambda i: (0, 0, 0)),
                   pl.BlockSpec((8, D_MODEL), const2),
                   pl.BlockSpec((CHUNK, CHUNK), const2)),
        scratch_shapes=[pltpu.VMEM((tm, D_MODEL), BF16),
                        pltpu.VMEM((tm, D_MODEL), F32),
                        pltpu.VMEM((tm, D_MODEL), F32),
                        pltpu.VMEM((tm, 2 * D_MODEL), F32),
                        pltpu.VMEM((tm, D_MODEL), BF16),
                        pltpu.VMEM((tm, D_MODEL), F32),
                        pltpu.VMEM((8, 8, D_MODEL), F32),
                        pltpu.VMEM((CHUNK, D_MODEL), F32),
                        pltpu.VMEM((5, tm, D_MODEL), F32),
                        pltpu.VMEM((2, tm, 1), F32)],
        compiler_params=_params(dimension_semantics=("arbitrary",)),
    )(x, z6, q, target, *ln_rows, mod, w_s, bs_exp, w_out)


def _conv_bwd(kidx, dq, sig, t2, cw4, sm1, dws, dbs, ycat, dy):
    s_len = dq.shape[1]
    tt = TIME_TILE
    n_blocks = D_MODEL // 128
    shp = [(16, D_MODEL), (D_MODEL, 128), (128, 128)]
    nj, nr, nc = 1, 256, D_MODEL
    kt = n_blocks // N_CHIPS
    tk = s_len // kt

    def body(k_ref, dq_ref, sig_ref, t2_ref, cw_ref, sm1_ref, dws_in, dbs_in, ycat_ref, dy_ref,
             dz_ref, pk1_ref, dws_ref, dbs_ref, loss_ref, gout_ref,
             dqpad, b_pk, b_dws, b_dbs, r_pk, r_dws, r_dbs, w_dws, ssem, rsem, *rs_scr):
        del k_ref
        jb = pl.program_id(0)
        x_, y_, c_ = _pos()

        @pl.when(jb == 0)
        def _():
            b_pk[0:8, :] = _onehot_rows(sm1_ref[1:2, :], 4 * x_ + 2 * y_ + c_)
            for r, src in enumerate([2, 3, 4, 5, 0, 6]):
                b_pk[8 + r:9 + r, :] = sm1_ref[src:src + 1, :]
            b_pk[14:16, :] = jnp.zeros((2, D_MODEL), F32)
            b_dws[...] = dws_in[...]
            b_dbs[...] = dbs_in[...]

        for e, step in enumerate([0, 0, (3 * n_blocks) // 8, (6 * n_blocks) // 8]):
            @pl.when(jb == step)
            def _():
                _butterfly_event(e, [b_pk, b_dws, b_dbs], [r_pk, r_dws, r_dbs], ssem, rsem, _all_reduce_partners(),
                                 wires=[None, w_dws, None])

        @pl.when(jb == (6 * n_blocks) // 8)
        def _():
            pk1_ref[...] = b_pk[...]
            dws_ref[...] = b_dws[...]
            dbs_ref[...] = b_dbs[...]
            loss_ref[...] = jnp.broadcast_to(jnp.sum(b_pk[13:14, :], axis=1, keepdims=True), (8, 128))

        zero = jnp.zeros((CONV_PAD, 128), F32)
        dqpad[0:CONV_PAD, :] = zero
        dqpad[s_len + CONV_PAD:s_len + 2 * CONV_PAD, :] = zero
        dqpad[CONV_PAD:s_len + CONV_PAD, :] = dq_ref[0]
        w = cw_ref[0]

        def bwd(i, carry):
            t0 = i * tt
            dp = jnp.zeros((tt, 128), F32)
            for o in range(1, CONV_WIDTH + 1):
                dp = dp + w[CONV_WIDTH - o:CONV_WIDTH - o + 1, :] * dqpad[pl.ds(t0 + o, tt), :]
            dz_ref[0, 0, pl.ds(t0, tt), :] = (dp * sig_ref[0, pl.ds(t0, tt), :]).astype(BF16)
            dz_ref[0, 1, pl.ds(t0, tt), :] = (dp * t2_ref[0, pl.ds(t0, tt), :]).astype(BF16)
            return carry

        n_iter = s_len // tt
        n_groups = min(8, n_iter)
        rows = tk // n_groups

        def conv_and_matmul(add):
            for g in range(n_groups):
                for it in range(g * n_iter // n_groups, (g + 1) * n_iter // n_groups):
                    bwd(it, 0)
                add(_dot_tn(ycat_ref[g * rows:(g + 1) * rows, :], dy_ref[g * rows:(g + 1) * rows, :]))
        _rs_step(jb // kt, jb % kt, kt, nj, nr, nc, conv_and_matmul, gout_ref, rs_scr)

    const2 = lambda j, k_ref: (0, 0)
    grid_spec = pltpu.PrefetchScalarGridSpec(
        num_scalar_prefetch=1,
        grid=(n_blocks,),
        in_specs=[*[pl.BlockSpec((1, s_len, 128), lambda j, k_ref: (j, 0, 0)) for _ in range(3)],
                  pl.BlockSpec((1, 32, 128), lambda j, k_ref: (j // 2, 0, j % 2)),
                  pl.BlockSpec((8, D_MODEL), const2),
                  pl.BlockSpec((D_MODEL, 128), const2),
                  pl.BlockSpec((128, 128), const2),
                  pl.BlockSpec((tk, 512), lambda j, k_ref: (j % kt, _rs_block(j // kt, k_ref[0], nj))),
                  pl.BlockSpec((tk, D_MODEL), lambda j, k_ref: (j % kt, 0))],
        out_specs=(pl.BlockSpec((1, 2, s_len, 128), lambda j, k_ref: (j, 0, 0, 0)),)
        + tuple(pl.BlockSpec(s, const2) for s in shp) + (pl.BlockSpec((8, 128), const2),
                                                         pl.BlockSpec(memory_space=pl.ANY)),
        scratch_shapes=[pltpu.VMEM((s_len + 2 * CONV_PAD, 128), F32)]
        + [pltpu.VMEM(s, F32) for s in shp]
        + [pltpu.VMEM((3,) + shp[0], F32), pltpu.VMEM((3,) + shp[1], BF16), pltpu.VMEM((3,) + shp[2], F32),
           pltpu.VMEM(shp[1], BF16)]
        + [pltpu.SemaphoreType.DMA((9,)), pltpu.SemaphoreType.DMA((9,))] + _rs_scratch(nj, nr, nc))
    return pl.pallas_call(
        body, name="conv_bwd",
        grid_spec=grid_spec,
        out_shape=(jax.ShapeDtypeStruct((n_blocks, 2, s_len, 128), BF16),)
        + tuple(jax.ShapeDtypeStruct(s, F32) for s in shp) + (jax.ShapeDtypeStruct((8, 128), F32),
                                                              jax.ShapeDtypeStruct((nj, 2, nr, nc), F32)),
        compiler_params=_params(dimension_semantics=("arbitrary",)),
    )(kidx, dq, sig, t2, cw4, sm1, dws, dbs, ycat, dy)


def _bwd_in(dz4, dzc, w12, x, dx2, mod, norm_g, dq, p):
    s_len = x.shape[0]
    tm = TOKEN_TILE
    n_steps = s_len // tm
    tt = TIME_TILE
    n_cblocks = D_MODEL // 128
    parts = max(1, n_steps // n_cblocks)
    nsub = max(1, n_cblocks // n_steps)
    cw = 128 * nsub
    tiles = (s_len // tt) // parts

    def body(dz_ref, dzc_ref, w_ref, x_ref, dx2_ref, mod_ref, g_ref, dq_ref, p_ref, gx_ref, sm_ref, dcw_ref,
             dh_scr, acc_scr, ppad, wacc):
        i = pl.program_id(0)
        part = i % parts

        @pl.when(i == 0)
        def _():
            acc_scr[...] = jnp.zeros_like(acc_scr)

        @pl.when(part == 0)
        def _():
            zero = jnp.zeros((CONV_PAD, 128), F32)
            for sub in range(nsub):
                ppad[sub, 0:CONV_PAD, :] = zero
                ppad[sub, s_len + CONV_PAD:s_len + 2 * CONV_PAD, :] = zero
                ppad[sub, CONV_PAD:s_len + CONV_PAD, :] = p_ref[sub]
            wacc[...] = jnp.zeros_like(wacc)

        def dw_tile(tile):
            t0 = pl.multiple_of((part * tiles + tile) * tt, tt)
            for sub in range(nsub):
                dqt = dq_ref[sub, pl.ds(t0, tt), :]
                for o in range(1, CONV_WIDTH + 1):
                    wacc[sub, o - 1] += _colsum8(dqt * ppad[sub, pl.ds(t0 + o, tt), :])
                wacc[sub, CONV_WIDTH] += _colsum8(dqt)

        def dz_cols(j, hf):
            if j < 4:
                return dz_ref[j, :, hf * 512:(hf + 1) * 512]
            return jnp.concatenate([dzc_ref[4 * hf + blk, j - 4] for blk in range(4)], axis=1)

        dots = [(j, hf) for j in range(N_GROUPS) for hf in range(2)]
        dh = jnp.zeros((tm, D_MODEL), F32)
        for d, (j, hf) in enumerate(dots):
            dh = dh + _dot_nt(dz_cols(j, hf), w_ref[2 * _natural_group(j) + hf])
            for tile in range(d * tiles // len(dots), (d + 1) * tiles // len(dots)):
                dw_tile(tile)
        dh_scr[...] = dh

        @pl.when(part == parts - 1)
        def _():
            for sub in range(nsub):
                for k in range(32):
                    dcw_ref[k:k + 1, sub * 128:(sub + 1) * 128] = jnp.sum(wacc[sub, k], axis=0, keepdims=True)

        scale1 = 1.0 + mod_ref[:, D_MODEL:2 * D_MODEL]
        g = g_ref[...]

        def rows_fn(rows):
            xt = x_ref[rows, :]
            r = lax.rsqrt(_mean(xt * xt) + EPS)
            xn = xt * r
            dhr = dh_scr[rows, :]
            acc_scr[0] += _colsum8(dhr)
            acc_scr[1] += _colsum8(dhr * (xn * g))
            acc_scr[2] += _colsum8(dhr * scale1 * xn)
            dxn = dhr * (g * scale1)
            gx_ref[rows, :] = dx2_ref[rows, :] + r * (dxn - xn * _mean(dxn * xn))
        _row_loop(tm, rows_fn, unroll=FULL_UNROLL)

        @pl.when(i == n_steps - 1)
        def _():
            for qi in range(8):
                sm_ref[qi:qi + 1, :] = jnp.sum(acc_scr[qi], axis=0, keepdims=True)

    tok = lambda i: (i, 0)
    const2 = lambda i: (0, 0)
    return pl.pallas_call(
        body, name="bwd_in",
        grid=(n_steps,),
        out_shape=(jax.ShapeDtypeStruct((s_len, D_MODEL), F32), jax.ShapeDtypeStruct((8, D_MODEL), F32),
                   jax.ShapeDtypeStruct((32, D_MODEL), F32)),
        in_specs=[pl.BlockSpec((4, tm, D_MODEL), lambda i: (0, i, 0)),
                  pl.BlockSpec((n_cblocks, 2, tm, 128), lambda i: (0, 0, i, 0)),
                  pl.BlockSpec((12, D_MODEL, 512), lambda i: (0, 0, 0), pipeline_mode=pl.Buffered(1)),
                  pl.BlockSpec((tm, D_MODEL), tok),
                  pl.BlockSpec((tm, D_MODEL), tok),
                  pl.BlockSpec((1, 3 * D_MODEL), const2),
                  pl.BlockSpec((1, D_MODEL), const2),
                  pl.BlockSpec((nsub, s_len, 128), lambda i: (i // parts, 0, 0)),
                  pl.BlockSpec((nsub, s_len, 128), lambda i: (i // parts, 0, 0))],
        out_specs=(pl.BlockSpec((tm, D_MODEL), tok), pl.BlockSpec((8, D_MODEL), const2),
                   pl.BlockSpec((32, cw), lambda i: (0, i // parts))),
        scratch_shapes=[pltpu.VMEM((tm, D_MODEL), F32), pltpu.VMEM((8, 8, D_MODEL), F32),
                        pltpu.VMEM((nsub, s_len + 2 * CONV_PAD, 128), F32), pltpu.VMEM((nsub, 32, 8, 128), F32)],
        compiler_params=_params(dimension_semantics=("arbitrary",)),
    )(dz4, dzc, w12, x, dx2, mod, norm_g, dq, p)


def _rs_shard_block(i, nj):
    g = jnp.where(i < 2 * nj, i % 2, jnp.where(i < 3 * nj, 2, 3))
    j = jnp.where(i < 2 * nj, i // 2, jnp.where(i < 3 * nj, i - 2 * nj, i - 3 * nj))
    return g, j


def _rs_block(i, k, nj):
    g, j = _rs_shard_block(i, nj)
    return nj * jnp.bitwise_xor(k, 3 - g) + j


def _rs_scratch(nj, nr, nc):
    nblk = N_CHIPS * nj
    return [pltpu.VMEM((2, 2, nr, nc), F32),
            pltpu.VMEM((nblk, nr, nc), F32),
            pltpu.VMEM((2 * nj, nr, nc), BF16),
            pltpu.VMEM((2 * nj, nr, nc), BF16),
            pltpu.VMEM((nj, nr, nc), BF16),
            pltpu.VMEM((nj, nr, nc), BF16),
            pltpu.VMEM((nj, nr, nc), F32),
            pltpu.SemaphoreType.DMA,
            pltpu.SemaphoreType.DMA((nblk,)), pltpu.SemaphoreType.DMA((nblk,)),
            pltpu.SemaphoreType.DMA((2 * nj,)), pltpu.SemaphoreType.DMA((2 * nj,)),
            pltpu.SemaphoreType.DMA((nj,)), pltpu.SemaphoreType.DMA((nj,)),
            pltpu.SemaphoreType.DMA, pltpu.SemaphoreType.DMA]


def _rs_step(i, kk, kt, nj, nr, nc, partial_fn, out_ref, scr):
    nblk = N_CHIPS * nj
    (acc, recv_a, send1, recv1, send2, recv2, own_ps,
     lsem, psend, precv, s1send, s1recv, s2send, s2recv, fsend, frecv) = scr
    x, y, c = _pos()
    sib = (x, y, 1 - c)
    first = (jnp.bitwise_xor(x, 1 - c), jnp.bitwise_xor(y, c), c)
    second = (jnp.bitwise_xor(x, c), jnp.bitwise_xor(y, 1 - c), c)
    slot = i % 2

    @pl.when(kk == 0)
    def _():
        acc[slot] = jnp.zeros((2, nr, nc), F32)

    def pair_copy(ii, sl):
        return _rcopy(acc.at[sl, 1 - c], recv_a.at[ii], psend.at[ii], precv.at[ii], sib)

    def pair_sum(ii, sl):
        cp = pair_copy(ii, sl)
        cp.wait_recv()
        cp.wait_send()
        return acc[sl, c] + recv_a[ii]

    def stage1_copy(src, dst):
        return _rcopy(send1.at[src], recv1.at[dst], s1send.at[src], s1recv.at[dst], first)

    def stage2_copy(j):
        return _rcopy(send2.at[j], recv2.at[j], s2send.at[j], s2recv.at[j], second)

    def finalize(ii, sl):
        ps = pair_sum(ii, sl)
        g, j = _rs_shard_block(ii, nj)

        @pl.when(g == 0)
        def _():
            send1[j] = ps.astype(BF16)
            stage1_copy(j, nj + j).start()

        @pl.when(g == 1 + c)
        def _():
            send1[nj + j] = ps.astype(BF16)
            stage1_copy(nj + j, j).start()

        @pl.when(g == 2 - c)
        def _():
            stage1_copy(0, nj + j).wait_recv()
            send2[j] = (ps + recv1[nj + j].astype(F32)).astype(BF16)
            stage2_copy(j).start()

        @pl.when(g == 3)
        def _():
            own_ps[j] = ps

    @pl.when(jnp.logical_and(kk == kt - 1, i >= 1))
    def _():
        finalize(i - 1, 1 - slot)

    def add_partial(res):
        acc[slot, 0] += res[0:nr]
        acc[slot, 1] += res[nr:2 * nr]
    partial_fn(add_partial)

    @pl.when(kk == kt - 1)
    def _():
        pair_copy(i, slot).start()

        @pl.when(i == nblk - 1)
        def _():
            own_ps[nj - 1] = pair_sum(nblk - 1, (nblk - 1) % 2)
            for j in range(nj):
                stage1_copy(0, j).wait_recv()
                stage2_copy(j).wait_recv()
                own_ps[j] = (own_ps[j] + recv1[j].astype(F32)) + recv2[j].astype(F32)
            loc = pltpu.make_async_copy(own_ps, out_ref.at[:, c], lsem)
            loc.start()
            swap = _rcopy(own_ps, out_ref.at[:, c], fsend, frecv, sib)
            swap.start()
            loc.wait()
            swap.wait_send()
            _rcopy(own_ps, out_ref.at[:, 1 - c], fsend, frecv, sib).wait_recv()
            for s in range(2 * nj):
                stage1_copy(s, 0).wait_send()
            for j in range(nj):
                stage2_copy(j).wait_send()


def _grad_w_in(kidx, h, dz4, dzc, sm2, dcw):
    s_len = h.shape[0]
    tk = min(K_TILE, s_len)
    kt = s_len // tk
    nj, nr, nc = 3, 512, 512
    nblk = N_CHIPS * nj

    def body(k_ref, a_ref, b_ref, bc_ref, sm2_ref, dcw_in, out_ref, pk2_ref, dcwsh_ref, dcb_ref, *scr):
        del k_ref
        b_pk2, b_dcw, r_pk2, r_dcw, bsend, brecv = scr[-6:]
        i = pl.program_id(0)
        kk = pl.program_id(1)
        x, y, c = _pos()
        k = 2 * x + y
        m = _rs_block(i, k, nj)

        @pl.when(jnp.logical_and(kk == 0, i == 0))
        def _():
            b = 4 * x + 2 * y + c
            b_pk2[0:8, :] = _onehot_rows(sm2_ref[0:1, :], b)
            b_pk2[8:16, :] = _onehot_rows(sm2_ref[1:2, :], b)
            b_pk2[16:17, :] = sm2_ref[2:3, :]
            b_pk2[17:24, :] = jnp.zeros((7, D_MODEL), F32)
            b_dcw[...] = dcw_in[...]

        for e in range(4):
            @pl.when(jnp.logical_and(kk == 0, i == (e * nblk) // 4))
            def _():
                _butterfly_event(e, [b_pk2, b_dcw], [r_pk2, r_dcw], bsend, brecv, _all_reduce_partners())

        @pl.when(jnp.logical_and(kk == 0, i == (3 * nblk) // 4))
        def _():
            pk2_ref[...] = b_pk2[...]
            sel = jnp.zeros((32, 256), F32)
            for kc in range(N_CHIPS):
                sel = jnp.where(k == kc, b_dcw[:, kc * 256:(kc + 1) * 256], sel)
            dcwsh_ref[...] = sel
            dcb_ref[...] = b_dcw[31:32, :]

        def partial(add):
            @pl.when(m >= 4)
            def _():
                add(_dot_tn(a_ref[...], b_ref[0]))

            @pl.when(m < 4)
            def _():
                add(_dot_tn(a_ref[...], jnp.concatenate([bc_ref[blk, 0] for blk in range(4)], axis=1)))
        _rs_step(i, kk, kt, nj, nr, nc, partial, out_ref, scr[:-6])

    def dz4_index(i, kk, k_ref):
        m = _rs_block(i, k_ref[0], nj)
        use = m >= 4
        return (jnp.where(use, m // 2 - 2, 0), jnp.where(use, kk, 0), jnp.where(use, m % 2, 0))

    def dzc_index(i, kk, k_ref):
        m = _rs_block(i, k_ref[0], nj)
        use = m < 4
        return (jnp.where(use, m % 2, 0), jnp.where(use, m // 2, 0), jnp.where(use, kk, 0), 0)

    const2 = lambda i, kk, k_ref: (0, 0)
    small_out = [(24, D_MODEL), (32, 256), (1, D_MODEL)]
    small_buf = [(24, D_MODEL), (32, D_MODEL)]
    grid_spec = pltpu.PrefetchScalarGridSpec(
        num_scalar_prefetch=1,
        grid=(nblk, kt),
        in_specs=[pl.BlockSpec((tk, D_MODEL), lambda i, kk, k_ref: (kk, 0)),
                  pl.BlockSpec((1, tk, 512), dz4_index),
                  pl.BlockSpec((4, 1, tk, 128), dzc_index),
                  pl.BlockSpec((8, D_MODEL), const2),
                  pl.BlockSpec((32, D_MODEL), const2)],
        out_specs=(pl.BlockSpec(memory_space=pl.ANY),) + tuple(pl.BlockSpec(s, const2) for s in small_out),
        scratch_shapes=_rs_scratch(nj, nr, nc)
        + [pltpu.VMEM(s, F32) for s in small_buf] + [pltpu.VMEM((3,) + s, F32) for s in small_buf]
        + [pltpu.SemaphoreType.DMA((6,)), pltpu.SemaphoreType.DMA((6,))])
    return pl.pallas_call(
        body, name="grad_w_in",
        grid_spec=grid_spec,
        out_shape=(jax.ShapeDtypeStruct((nj, 2, nr, nc), F32),) + tuple(jax.ShapeDtypeStruct(s, F32) for s in small_out),
        compiler_params=_params(dimension_semantics=("arbitrary", "arbitrary")),
    )(kidx, h, dz4, dzc, sm2, dcw)


def _adamw_math(w, g, m, v):
    m = ADAM_B1 * m + (1.0 - ADAM_B1) * g
    v = ADAM_B2 * v + (1.0 - ADAM_B2) * (g * g)
    m_hat = m / (1.0 - ADAM_B1 ** ADAM_STEP)
    v_hat = v / (1.0 - ADAM_B2 ** ADAM_STEP)
    delta = -ADAM_LR * (m_hat / (jnp.sqrt(v_hat) + ADAM_EPS) + ADAM_WD * w)
    return delta, m, v


def _adamw_blocked(w, m, v, g4, name):
    nj, _, nr, nc = g4.shape

    def body(w_ref, m_ref, v_ref, g_ref, go_ref, d_ref, mo_ref, vo_ref):
        g = g_ref[0, 0]
        d, mn, vn = _adamw_math(w_ref[...], g, m_ref[...], v_ref[...])
        go_ref[...] = g
        d_ref[...] = d
        mo_ref[...] = mn
        vo_ref[...] = vn

    blk = pl.BlockSpec((nr, nc), lambda j, hf: (hf, j))
    return pl.pallas_call(
        body, name=name,
        grid=(nj, 2),
        out_shape=tuple(jax.ShapeDtypeStruct(w.shape, F32) for _ in range(4)),
        in_specs=[blk, blk, blk, pl.BlockSpec((1, 1, nr, nc), lambda j, hf: (j, hf, 0, 0))],
        out_specs=(blk, blk, blk, blk),
        compiler_params=_params(dimension_semantics=("arbitrary", "arbitrary")),
    )(w, m, v, g4)


def _adamw_w_ada(cact, pk1, pk2, w, m, v):
    rb = 256

    def body(cact_ref, pk1_ref, pk2_ref, w_ref, m_ref, v_ref, g_ref, d_ref, mo_ref, vo_ref, gb_ref, dmod, sel_scr):
        @pl.when(pl.program_id(0) == 0)
        def _():
            x, y, _ = _pos()
            k = 2 * x + y
            dmod[:, 0:D_MODEL] = pk2_ref[0:8, :]
            dmod[:, D_MODEL:2 * D_MODEL] = pk2_ref[8:16, :]
            dmod[:, 2 * D_MODEL:3 * D_MODEL] = pk1_ref[0:8, :]
            gb_ref[...] = jnp.sum(dmod[...], axis=0, keepdims=True)
            sel = jnp.zeros((8, 768), F32)
            for kk in range(N_CHIPS):
                sel = jnp.where(k == kk, dmod[:, kk * 768:(kk + 1) * 768], sel)
            sel_scr[...] = sel

        g = _dot_tn(cact_ref[...].astype(BF16), sel_scr[...].astype(BF16))
        d, mn, vn = _adamw_math(w_ref[...], g, m_ref[...], v_ref[...])
        g_ref[...] = g
        d_ref[...] = d
        mo_ref[...] = mn
        vo_ref[...] = vn

    blk = pl.BlockSpec((rb, 768), lambda i: (i, 0))
    const2 = lambda i: (0, 0)
    return pl.pallas_call(
        body, name="adamw_w_ada",
        grid=(D_MODEL // rb,),
        out_shape=tuple(jax.ShapeDtypeStruct(w.shape, F32) for _ in range(4)) + (
            jax.ShapeDtypeStruct((1, 3 * D_MODEL), F32),),
        in_specs=[pl.BlockSpec((8, rb), lambda i: (0, i)), pl.BlockSpec((16, D_MODEL), const2),
                  pl.BlockSpec((24, D_MODEL), const2), blk, blk, blk],
        out_specs=(blk, blk, blk, blk, pl.BlockSpec((1, 3 * D_MODEL), const2)),
        scratch_shapes=[pltpu.VMEM((8, 3 * D_MODEL), F32), pltpu.VMEM((8, 768), F32)],
        compiler_params=_params(dimension_semantics=("arbitrary",)),
    )(cact, pk1, pk2, w, m, v)


def _adamw_small(items):
    n = len(items)

    def body(*refs):
        ins, outs = refs[:4 * n], refs[4 * n:]
        for i in range(n):
            w_ref, g_ref, m_ref, v_ref = ins[4 * i:4 * i + 4]
            d, mn, vn = _adamw_math(w_ref[...], g_ref[...], m_ref[...], v_ref[...])
            outs[3 * i][...] = d
            outs[3 * i + 1][...] = mn
            outs[3 * i + 2][...] = vn

    flat = [a for it in items for a in it]
    outs = pl.pallas_call(
        body, name="adamw_small",
        out_shape=tuple(jax.ShapeDtypeStruct(it[0].shape, F32) for it in items for _ in range(3)),
        in_specs=[_vmem()] * (4 * n),
        out_specs=tuple(_vmem() for _ in range(3 * n)),
        compiler_params=_params(),
    )(*flat)
    return [tuple(outs[3 * i:3 * i + 3]) for i in range(n)]


def kernel(x, c, w_ada, b_ada, norm_g, w_in, conv_w, conv_b, conv_ln_g, conv_ln_b, sg_ln_g, sg_ln_b, w_s, b_s, w_out, final_g, loss_target, m_w_ada, m_b_ada, m_norm_g, m_w_in, m_conv_w, m_conv_b, m_conv_ln_g, m_conv_ln_b, m_sg_ln_g, m_sg_ln_b, m_w_s, m_b_s, m_w_out, m_final_g, v_w_ada, v_b_ada, v_norm_g, v_w_in, v_conv_w, v_conv_b, v_conv_ln_g, v_conv_ln_b, v_sg_ln_g, v_sg_ln_b, v_w_s, v_b_s, v_w_out, v_final_g):
    s_len = x.shape[1]
    x2d = x[0]
    tgt = loss_target[0]
    row = lambda a: a.reshape(1, -1)

    kidx = (2 * lax.axis_index("x") + lax.axis_index("y")).astype(jnp.int32).reshape(1)
    cw_sh = jnp.pad(conv_w.reshape(CONV_WIDTH, 256), ((0, 1), (0, 0)))
    h, z6, mod, cact, cw4, w_in12 = _front(kidx, x2d, c, w_ada[0], b_ada, norm_g, w_in[0], cw_sh)
    w12 = w_in12.reshape(12, D_MODEL, 512)
    q, p_cm, sig_cm, t2_cm, w_out4 = _conv_fwd(z6, cw4, conv_b, w_out[0])
    w_out_full = w_out4.reshape(2 * D_MODEL, D_MODEL)
    ln_rows = (conv_ln_g, conv_ln_b, sg_ln_g, sg_ln_b, row(final_g))
    bs_exp = jnp.repeat(b_s[0].T, CHUNK, axis=1)
    dz4, dq, dx2, ycat, dy, dws, sm1, dbs = _middle(x2d, z6, q, tgt, ln_rows, mod, w_s[0], bs_exp, w_out_full)
    dzc, pk1, dws_r, dbs_r, loss_t, g_w_out4 = _conv_bwd(
        kidx, dq, sig_cm, t2_cm, cw4, sm1, dws.reshape(D_MODEL, CHUNK), dbs, ycat, dy)
    grad_x, sm2, dcw = _bwd_in(dz4, dzc, w12, x2d, dx2, mod, norm_g, dq, p_cm)
    g_w_in4, pk2, dcw_sh, dcb = _grad_w_in(kidx, h, dz4, dzc, sm2, dcw)

    g_w_in, d_w_in, nm_w_in, nv_w_in = _adamw_blocked(w_in[0], m_w_in[0], v_w_in[0], g_w_in4, "adamw_w_in")
    g_w_out, d_w_out, nm_w_out, nv_w_out = _adamw_blocked(w_out[0], m_w_out[0], v_w_out[0], g_w_out4, "adamw_w_out")
    g_w_ada, d_w_ada, nm_w_ada, nv_w_ada, g_b_ada = _adamw_w_ada(cact, pk1, pk2, w_ada[0], m_w_ada[0], v_w_ada[0])

    g_norm_g = pk2[16:17]
    g_cln_g, g_cln_b, g_sln_g, g_sln_b, g_final = (pk1[8 + i:9 + i] for i in range(5))
    loss = loss_t[0, 0]
    g_conv_w = dcw_sh[:CONV_WIDTH]
    g_w_s = dws_r
    g_b_s = dbs_r[:, :HEADS].T
    small = [
        (b_ada, g_b_ada, m_b_ada, v_b_ada),
        (norm_g, g_norm_g, m_norm_g, v_norm_g),
        (conv_w.reshape(CONV_WIDTH, 256), g_conv_w, m_conv_w.reshape(CONV_WIDTH, 256), v_conv_w.reshape(CONV_WIDTH, 256)),
        (conv_b, dcb, m_conv_b, v_conv_b),
        (conv_ln_g, g_cln_g, m_conv_ln_g, v_conv_ln_g),
        (conv_ln_b, g_cln_b, m_conv_ln_b, v_conv_ln_b),
        (sg_ln_g, g_sln_g, m_sg_ln_g, v_sg_ln_g),
        (sg_ln_b, g_sln_b, m_sg_ln_b, v_sg_ln_b),
        (w_s.reshape(D_MODEL, CHUNK), g_w_s, m_w_s.reshape(D_MODEL, CHUNK), v_w_s.reshape(D_MODEL, CHUNK)),
        (b_s[0], g_b_s, m_b_s[0], v_b_s[0]),
        (row(final_g), g_final, row(m_final_g), row(v_final_g)),
    ]
    upd = _adamw_small(small)

    shapes = [w_ada.shape, b_ada.shape, norm_g.shape, w_in.shape, conv_w.shape, conv_b.shape, conv_ln_g.shape,
              conv_ln_b.shape, sg_ln_g.shape, sg_ln_b.shape, w_s.shape, b_s.shape, w_out.shape, final_g.shape]
    grads = [g_w_ada, g_b_ada, g_norm_g, g_w_in, g_conv_w, dcb, g_cln_g, g_cln_b, g_sln_g, g_sln_b, g_w_s, g_b_s,
             g_w_out, g_final]
    big = {0: (d_w_ada, nm_w_ada, nv_w_ada), 3: (d_w_in, nm_w_in, nv_w_in), 12: (d_w_out, nm_w_out, nv_w_out)}
    small_pos = [1, 2, 4, 5, 6, 7, 8, 9, 10, 11, 13]
    trip = [None] * 14
    for i, t in big.items():
        trip[i] = t
    for i, t in zip(small_pos, upd):
        trip[i] = t
    fit = lambda arrs: [a.reshape(s) for a, s in zip(arrs, shapes)]
    return (loss, grad_x.reshape(x.shape), *fit(grads), *fit([t[0] for t in trip]), *fit([t[1] for t in trip]),
            *fit([t[2] for t in trip]))
```

```python
import jax
import jax.numpy as jnp
from jax import lax
from jax.experimental import pallas as pl
from jax.experimental.pallas import tpu as pltpu

F32 = jnp.float32
BF16 = jnp.bfloat16
MESH = pl.DeviceIdType.MESH

D_MODEL = 1024
N_CHIPS = 4
HEADS = 8
CHUNK = 128
CONV_WIDTH = 31
CONV_HALF = CONV_WIDTH // 2
CONV_PAD = 16
EPS = 1e-6
ADAM_LR = 0.001
ADAM_B1 = 0.9
ADAM_B2 = 0.999
ADAM_EPS = 1e-08
ADAM_WD = 0.01
ADAM_STEP = 10

V7X_VMEM_BYTES = 64 * 1024 * 1024
VMEM_LIMIT = V7X_VMEM_BYTES - 8 * 1024 * 1024
ROWS = 16
UNROLL = 8
TOKEN_TILE = 256
FULL_UNROLL = TOKEN_TILE // ROWS
TIME_TILE = 128
K_TILE = 2048

N_GROUPS = 6


def _natural_group(j):
    return (j + 2) % N_GROUPS


def _pos():
    return lax.axis_index("x"), lax.axis_index("y"), lax.axis_index("c")


def _rcopy(src, dst, ssem, rsem, dev):
    return pltpu.make_async_remote_copy(src_ref=src, dst_ref=dst, send_sem=ssem, recv_sem=rsem,
                                        device_id=dev, device_id_type=MESH)


def _vmem():
    return pl.BlockSpec(memory_space=pltpu.VMEM)


def _params(**kw):
    return pltpu.CompilerParams(vmem_limit_bytes=VMEM_LIMIT, **kw)


def _sigmoid(v):
    return 0.5 * jnp.tanh(0.5 * v) + 0.5


def _row_loop(n_rows, body, unroll=1):
    def step(r, carry):
        body(pl.ds(pl.multiple_of(r * ROWS, ROWS), ROWS))
        return carry
    lax.fori_loop(0, n_rows // ROWS, step, 0, unroll=unroll)


def _colsum8(v):
    return v.reshape(v.shape[0] // 8, 8, v.shape[1]).sum(axis=0)


def _mean(v):
    return jnp.mean(v, axis=-1, keepdims=True)


def _dot_nn(a, b):
    return jnp.dot(a, b, preferred_element_type=F32)


def _dot_nt(a, b):
    return lax.dot_general(a, b, (((1,), (1,)), ((), ())), preferred_element_type=F32)


def _dot_tn(a, b):
    return lax.dot_general(a, b, (((0,), (0,)), ((), ())), preferred_element_type=F32)


def _remote_chip(k, r):
    return jnp.bitwise_xor(k, r + 1)


def _front(kidx, x, c, w_ada, b_ada, norm_g, w_in, cw):
    s_len = x.shape[0]
    tmh = min(512, s_len)
    tmz = min(1024, s_len)
    nh = s_len // tmh
    nz = s_len // tmz
    n_steps = nh + 12 * nz

    def remote_block(q):
        return jnp.where(q < 6, q % 2, 2), jnp.where(q < 6, q // 2, q - 6)

    def block_of(i, k):
        r, j = remote_block(jnp.maximum(i - 3, 0))
        return jnp.where(i < 3, 3 * k + i, 3 * _remote_chip(k, r) + j)

    def body(k_ref, x_ref, c_ref, bada_ref, g_ref, cw_ref, wada_hbm, win_hbm,
             h_ref, z_ref, mod_ref, cact_ref, cw4_ref, w12_hbm,
             h_all, wbuf, stage, wada_v, cslab, mslab, cw4_s,
             lsem, csend, crecv, msend, mrecv, wsend, wrecv, isend, irecv, fsend, frecv, osem):
        del k_ref
        t = pl.program_id(0)
        x_, y_, c_ = _pos()
        k = 2 * x_ + y_
        b = 4 * x_ + 2 * y_ + c_
        sib = (x_, y_, 1 - c_)

        def dev_of(r):
            kk = _remote_chip(k, r)
            return (kk // 2, kk % 2, c_)

        def ici(q, kk):
            r, j = remote_block(q)
            return _rcopy(wbuf.at[3 * kk + j, c_], wbuf.at[3 * kk + j, c_], isend.at[q], irecv.at[q], dev_of(r))

        def fwd(q, hf):
            r, j = remote_block(q)
            blk = 3 * _remote_chip(k, r) + j
            return _rcopy(wbuf.at[blk, hf], wbuf.at[blk, hf], fsend.at[q], frecv.at[q], sib)

        def c_copy(q, src):
            d = jnp.bitwise_xor(b, q)
            return _rcopy(cslab.at[src], cslab.at[src], csend.at[q - 1], crecv.at[q - 1], (d // 4, (d // 2) % 2, d % 2))

        def m_copy(r, kk):
            return _rcopy(mslab.at[kk], mslab.at[kk], msend.at[r], mrecv.at[r], dev_of(r))

        def cw_copy(r, kk):
            return _rcopy(cw4_s.at[kk], cw4_s.at[kk], wsend.at[r], wrecv.at[r], dev_of(r))

        def to_hbm(i):
            m = block_of(i, k)
            return pltpu.make_async_copy(wbuf.at[m], w12_hbm.at[m], osem.at[i])

        @pl.when(t == 0)
        def _():
            ld_w = pltpu.make_async_copy(win_hbm, stage, lsem.at[0])
            ld_w.start()
            ld_a = pltpu.make_async_copy(wada_hbm, wada_v, lsem.at[1])
            ld_a.start()
            cslab[b] = jnp.broadcast_to(c_ref[...], (8, D_MODEL))
            for q in range(1, 8):
                c_copy(q, b).start()
            cw4_s[k] = cw_ref[...]
            for r in range(3):
                cw_copy(r, k).start()
            ld_w.wait()
            for j in range(3):
                for hf in range(2):
                    wbuf[3 * k + j, hf] = stage[hf * 512:(hf + 1) * 512, j * 512:(j + 1) * 512].astype(BF16)
            for i in range(3):
                to_hbm(i).start()
            for q in range(1, 8):
                c_copy(q, jnp.bitwise_xor(b, q)).wait_recv()
            row = lax.broadcasted_iota(jnp.int32, (8, D_MODEL), 0)
            call = jnp.zeros((8, D_MODEL), F32)
            for d in range(8):
                call = jnp.where(row == d, cslab[d], call)
            cact = call * _sigmoid(call)
            cact_ref[...] = cact
            ld_a.wait()
            mslab[k] = _dot_nn(cact.astype(BF16), wada_v[...].astype(BF16))
            for r in range(3):
                m_copy(r, k).start()
            for q in range(9):
                ici(q, k).start()
            for r in range(3):
                m_copy(r, _remote_chip(k, r)).wait_recv()
            row8 = lax.broadcasted_iota(jnp.int32, (8, 768), 0)
            for kk in range(N_CHIPS):
                piece = jnp.sum(jnp.where(row8 == b, mslab[kk], 0.0), axis=0, keepdims=True)
                mod_ref[:, kk * 768:(kk + 1) * 768] = piece + bada_ref[:, kk * 768:(kk + 1) * 768]
            for r in range(3):
                cw_copy(r, _remote_chip(k, r)).wait_recv()
            cw4_ref[...] = cw4_s[...]

        @pl.when(t < nh)
        def _():
            shift = mod_ref[:, 0:D_MODEL]
            scale1 = 1.0 + mod_ref[:, D_MODEL:2 * D_MODEL]
            g = g_ref[...]
            base = t * tmh

            def rows_fn(rows):
                xt = x_ref[rows, :]
                r = lax.rsqrt(_mean(xt * xt) + EPS)
                hv = ((xt * r * g) * scale1 + shift).astype(BF16)
                h_ref[rows, :] = hv
                h_all[pl.ds(pl.multiple_of(base + rows.start, ROWS), ROWS), :] = hv
            _row_loop(tmh, rows_fn, unroll=UNROLL)

        @pl.when(t >= nh)
        def _():
            u = t - nh
            i = u // nz
            rt = u % nz
            @pl.when(jnp.logical_and(rt == 0, jnp.logical_and(i >= 2, i <= 10)))
            def _():
                q = i - 2
                r, _ = remote_block(q)
                ici(q, _remote_chip(k, r)).wait_recv()
                fwd(q, c_).start()

            @pl.when(jnp.logical_and(rt == 0, i >= 3))
            def _():
                fwd(i - 3, 1 - c_).wait_recv()
                to_hbm(i).start()
            m = block_of(i, k)
            hb = h_all[pl.ds(pl.multiple_of(rt * tmz, tmz), tmz), :]
            z_ref[0] = _dot_nn(hb, wbuf[m].reshape(D_MODEL, 512))

        @pl.when(t == n_steps - 1)
        def _():
            for q in range(1, 8):
                c_copy(q, b).wait_send()
            for r in range(3):
                m_copy(r, k).wait_send()
                cw_copy(r, k).wait_send()
            for q in range(9):
                ici(q, k).wait_send()
                fwd(q, c_).wait_send()
            for i in range(12):
                to_hbm(i).wait()

    def z_index(t, k_ref):
        u = jnp.maximum(t - nh, 0)
        m = block_of(u // nz, k_ref[0])
        return ((m // 2 + 4) % N_GROUPS, u % nz, m % 2)

    tok = lambda t, k_ref: (jnp.minimum(t, nh - 1), 0)
    const2 = lambda t, k_ref: (0, 0)
    grid_spec = pltpu.PrefetchScalarGridSpec(
        num_scalar_prefetch=1,
        grid=(n_steps,),
        in_specs=[pl.BlockSpec((tmh, D_MODEL), tok),
                  pl.BlockSpec((1, D_MODEL), const2),
                  pl.BlockSpec((1, 3 * D_MODEL), const2),
                  pl.BlockSpec((1, D_MODEL), const2),
                  pl.BlockSpec((32, 256), const2),
                  pl.BlockSpec(memory_space=pl.ANY),
                  pl.BlockSpec(memory_space=pl.ANY)],
        out_specs=(pl.BlockSpec((tmh, D_MODEL), tok),
                   pl.BlockSpec((1, tmz, 512), z_index),
                   pl.BlockSpec((1, 3 * D_MODEL), const2),
                   pl.BlockSpec((8, D_MODEL), const2),
                   pl.BlockSpec((N_CHIPS, 32, 256), lambda t, k_ref: (0, 0, 0)),
                   pl.BlockSpec(memory_space=pl.ANY)),
        scratch_shapes=[pltpu.VMEM((s_len, D_MODEL), BF16),
                        pltpu.VMEM((12, 2, 512, 512), BF16),
                        pltpu.VMEM((D_MODEL, 1536), F32),
                        pltpu.VMEM((D_MODEL, 768), F32),
                        pltpu.VMEM((8, 8, D_MODEL), F32),
                        pltpu.VMEM((N_CHIPS, 8, 768), F32),
                        pltpu.VMEM((N_CHIPS, 32, 256), F32),
                        pltpu.SemaphoreType.DMA((2,)),
                        pltpu.SemaphoreType.DMA((7,)), pltpu.SemaphoreType.DMA((7,)),
                        pltpu.SemaphoreType.DMA((3,)), pltpu.SemaphoreType.DMA((3,)),
                        pltpu.SemaphoreType.DMA((3,)), pltpu.SemaphoreType.DMA((3,)),
                        pltpu.SemaphoreType.DMA((9,)), pltpu.SemaphoreType.DMA((9,)),
                        pltpu.SemaphoreType.DMA((9,)), pltpu.SemaphoreType.DMA((9,)),
                        pltpu.SemaphoreType.DMA((12,))])
    return pl.pallas_call(
        body, name="front",
        grid_spec=grid_spec,
        out_shape=(jax.ShapeDtypeStruct((s_len, D_MODEL), BF16),
                   jax.ShapeDtypeStruct((N_GROUPS, s_len, D_MODEL), F32),
                   jax.ShapeDtypeStruct((1, 3 * D_MODEL), F32),
                   jax.ShapeDtypeStruct((8, D_MODEL), F32),
                   jax.ShapeDtypeStruct((N_CHIPS, 32, 256), F32),
                   jax.ShapeDtypeStruct((12, 2, 512, 512), BF16)),
        compiler_params=_params(dimension_semantics=("arbitrary",)),
    )(kidx, x, c, b_ada, norm_g, cw, w_ada, w_in)


def _all_reduce_partners():
    x, y, c = _pos()
    return [(x, y, 1 - c), (x, 1 - y, c), (1 - x, y, c)]


def _butterfly_event(e, bufs, recvs, ssem, rsem, partners, wires=None):
    n = len(bufs)
    wires = wires or [None] * n

    def copies(s):
        return [_rcopy(buf if wire is None else wire, recv.at[s], ssem.at[s * n + i], rsem.at[s * n + i], partners[s])
                for i, (buf, recv, wire) in enumerate(zip(bufs, recvs, wires))]

    if e > 0:
        for cp in copies(e - 1):
            cp.wait()
        for buf, recv, wire in zip(bufs, recvs, wires):
            mine = buf[...] if wire is None else wire[...].astype(F32)
            buf[...] = mine + recv[e - 1].astype(F32)
    if e < len(partners):
        for buf, wire in zip(bufs, wires):
            if wire is not None:
                wire[...] = buf[...].astype(BF16)
        for cp in copies(e):
            cp.start()


def _onehot_rows(v, b):
    row = lax.broadcasted_iota(jnp.int32, (8, v.shape[1]), 0)
    return jnp.where(row == b, jnp.broadcast_to(v, (8, v.shape[1])), 0.0)


def _conv_fwd(z6, cw4, conv_b, w_out):
    s_len = z6.shape[1]
    tt = TIME_TILE
    n_blocks = D_MODEL // 128

    def body(z_ref, cw_ref, cb_ref, wout_hbm, q_ref, p_ref, sig_ref, t2_ref, wout4_hbm,
             ppad, stage, wbuf, lsem, isend, irecv, fsend, frecv, osem):
        jb = pl.program_id(0)
        x_, y_, c_ = _pos()
        k = 2 * x_ + y_
        sib = (x_, y_, 1 - c_)

        def ici(r, kk):
            rk = _remote_chip(k, r)
            return _rcopy(wbuf.at[kk, c_], wbuf.at[kk, c_], isend.at[r], irecv.at[r], (rk // 2, rk % 2, c_))

        def fwd(r, hf):
            kk = _remote_chip(k, r)
            return _rcopy(wbuf.at[kk, hf], wbuf.at[kk, hf], fsend.at[r], frecv.at[r], sib)

        @pl.when(jb == 0)
        def _():
            ld = pltpu.make_async_copy(wout_hbm, stage, lsem)
            ld.start()
            ld.wait()
            for hf in range(2):
                wbuf[k, hf] = stage[hf * 256:(hf + 1) * 256, :].astype(BF16)
            for r in range(3):
                ici(r, k).start()

        @pl.when(jb == (5 * n_blocks) // 8)
        def _():
            for r in range(3):
                ici(r, _remote_chip(k, r)).wait_recv()
                fwd(r, c_).start()

        zero = jnp.zeros((CONV_PAD, 128), F32)
        ppad[0:CONV_PAD, :] = zero
        ppad[s_len + CONV_PAD:s_len + 2 * CONV_PAD, :] = zero

        def fill(i, carry):
            t0 = pl.multiple_of(i * tt, tt)
            sig = _sigmoid(z_ref[1, pl.ds(t0, tt), :])
            p = z_ref[0, pl.ds(t0, tt), :] * sig
            ppad[pl.ds(CONV_PAD + t0, tt), :] = p
            p_ref[0, pl.ds(t0, tt), :] = p
            sig_ref[0, pl.ds(t0, tt), :] = sig
            t2_ref[0, pl.ds(t0, tt), :] = p * (1.0 - sig)
            return carry
        lax.fori_loop(0, s_len // tt, fill, 0)
        w = cw_ref[0]
        bias = cb_ref[...]

        def conv(i, carry):
            t0 = pl.multiple_of(i * tt, tt)
            acc = jnp.broadcast_to(bias, (tt, 128))
            for o in range(1, CONV_WIDTH + 1):
                acc = acc + w[o - 1:o, :] * ppad[pl.ds(t0 + o, tt), :]
            q_ref[0, pl.ds(t0, tt), :] = acc
            return carry
        lax.fori_loop(0, s_len // tt, conv, 0)

        @pl.when(jb == n_blocks - 1)
        def _():
            for r in range(3):
                fwd(r, 1 - c_).wait_recv()
            out = pltpu.make_async_copy(wbuf, wout4_hbm, osem)
            out.start()
            for r in range(3):
                ici(r, k).wait_send()
                fwd(r, c_).wait_send()
            out.wait()

    return pl.pallas_call(
        body, name="conv_fwd",
        grid=(n_blocks,),
        out_shape=tuple(jax.ShapeDtypeStruct((n_blocks, s_len, 128), F32) for _ in range(4))
        + (jax.ShapeDtypeStruct((N_CHIPS, 2, 256, D_MODEL), BF16),),
        in_specs=[pl.BlockSpec((2, s_len, 128), lambda j: (2, 0, j)),
                  pl.BlockSpec((1, 32, 128), lambda j: (j // 2, 0, j % 2)),
                  pl.BlockSpec((1, 128), lambda j: (0, j)),
                  pl.BlockSpec(memory_space=pl.ANY)],
        out_specs=tuple(pl.BlockSpec((1, s_len, 128), lambda j: (j, 0, 0)) for _ in range(4))
        + (pl.BlockSpec(memory_space=pl.ANY),),
        scratch_shapes=[pltpu.VMEM((s_len + 2 * CONV_PAD, 128), F32),
                        pltpu.VMEM((512, D_MODEL), F32),
                        pltpu.VMEM((N_CHIPS, 2, 256, D_MODEL), BF16),
                        pltpu.SemaphoreType.DMA,
                        pltpu.SemaphoreType.DMA((3,)), pltpu.SemaphoreType.DMA((3,)),
                        pltpu.SemaphoreType.DMA((3,)), pltpu.SemaphoreType.DMA((3,)),
                        pltpu.SemaphoreType.DMA],
        compiler_params=_params(dimension_semantics=("arbitrary",)),
    )(z6, cw4, conv_b, w_out)


def _middle(x, z6, q, target, ln_rows, mod, w_s, bs_exp, w_out):
    s_len = x.shape[0]
    tm = TOKEN_TILE
    n_steps = s_len // tm
    n_chunks = tm // CHUNK
    inv_d = 1.0 / D_MODEL

    def body(x_ref, z_ref, q_ref, tgt_ref, cg_ref, cb_ref, sg_ref, sb_ref, fg_ref, mod_ref, ws_ref, bs_ref, wout_ref,
             dz_ref, dq_ref, dx2_ref, ycat_ref, dy_ref, dws_ref, sm_ref, dbs_ref,
             vl_scr, vm_scr, y_scr, dycat_scr, dvm_scr, dvl_scr, acc_scr, dbs_acc, keep, rstd_scr):
        i = pl.program_id(0)

        @pl.when(i == 0)
        def _():
            acc_scr[...] = jnp.zeros_like(acc_scr)
            dbs_acc[...] = jnp.zeros_like(dbs_acc)
            dws_ref[...] = jnp.zeros_like(dws_ref)

        cg, cb, sg, sb, fg = cg_ref[...], cb_ref[...], sg_ref[...], sb_ref[...], fg_ref[...]
        gm = mod_ref[:, 2 * D_MODEL:3 * D_MODEL]

        def norm_stats(t):
            c = t - _mean(t)
            rstd = lax.rsqrt(_mean(c * c) + EPS)
            return c * rstd, rstd

        def phase1(rows):
            qhat, rstd_q = norm_stats(jnp.concatenate([q_ref[blk, rows, :] for blk in range(D_MODEL // 128)], axis=1))
            ln = qhat * cg + cb
            gz = z_ref[0, rows, :]
            sig_ln = _sigmoid(ln)
            sig_g = _sigmoid(gz)
            ycat_ref[rows, 0:D_MODEL] = ((ln * sig_ln) * (gz * sig_g)).astype(BF16)
            vhat, rstd_v = norm_stats(z_ref[2, rows, :])
            vl_scr[rows, :] = (vhat * sg + sb).astype(BF16)
            keep[0, rows, :] = qhat
            keep[1, rows, :] = vhat
            keep[2, rows, :] = sig_ln
            keep[3, rows, :] = sig_g
            rstd_scr[0, rows, :] = rstd_q
            rstd_scr[1, rows, :] = rstd_v
        _row_loop(tm, phase1, unroll=FULL_UNROLL)

        for ch in range(n_chunks):
            r0 = ch * CHUNK
            for h in range(HEADS):
                c0 = h * CHUNK
                vm_scr[r0:r0 + CHUNK, c0:c0 + CHUNK] = (
                    _dot_nn(ws_ref[h].astype(BF16), vl_scr[r0:r0 + CHUNK, c0:c0 + CHUNK]) + bs_ref[:, c0:c0 + CHUNK])

        def phase3(rows):
            bg = z_ref[3, rows, :]
            sig_b = _sigmoid(bg)
            keep[4, rows, :] = sig_b
            ycat_ref[rows, D_MODEL:2 * D_MODEL] = (z_ref[1, rows, :] * vm_scr[rows, :] * (bg * sig_b)).astype(BF16)
        _row_loop(tm, phase3, unroll=FULL_UNROLL)

        y_scr[...] = _dot_nn(ycat_ref[...], wout_ref[...])

        def phase5(rows):
            y = y_scr[rows, :]
            x2 = x_ref[rows, :] + gm * y
            r2 = lax.rsqrt(_mean(x2 * x2) + EPS)
            xn2 = x2 * r2
            diff = xn2 * fg - tgt_ref[rows, :]
            acc_scr[6] += _colsum8(diff * diff)
            dout = diff * inv_d
            acc_scr[0] += _colsum8(dout * xn2)
            dxn = dout * fg
            dx2 = r2 * (dxn - xn2 * _mean(dxn * xn2))
            dx2_ref[rows, :] = dx2
            acc_scr[1] += _colsum8(dx2 * y)
            dy_ref[rows, :] = (dx2 * gm).astype(BF16)
        _row_loop(tm, phase5, unroll=FULL_UNROLL)

        dycat_scr[...] = _dot_nt(dy_ref[...], wout_ref[...])

        def phase7(rows):
            dyb = dycat_scr[rows, D_MODEL:2 * D_MODEL]
            u = z_ref[1, rows, :]
            bg = z_ref[3, rows, :]
            vm = vm_scr[rows, :]
            sig = keep[4, rows, :]
            silu = bg * sig
            dz_ref[1, rows, :] = (dyb * vm * silu).astype(BF16)
            dvm = dyb * u * silu
            dz_ref[3, rows, :] = (dyb * u * vm * (sig * (1.0 + bg * (1.0 - sig)))).astype(BF16)
            dvm_scr[rows, :] = dvm.astype(BF16)
            pos = pl.ds(pl.multiple_of(rows.start % CHUNK, ROWS), ROWS)
            dbs_acc[pos, :] += dvm
        _row_loop(tm, phase7, unroll=FULL_UNROLL)

        for ch in range(n_chunks):
            r0 = ch * CHUNK
            for h in range(HEADS):
                c0 = h * CHUNK
                dvm_b = dvm_scr[r0:r0 + CHUNK, c0:c0 + CHUNK]
                dws_ref[h] += _dot_nt(dvm_b, vl_scr[r0:r0 + CHUNK, c0:c0 + CHUNK])
                dvl_scr[r0:r0 + CHUNK, c0:c0 + CHUNK] = _dot_tn(ws_ref[h].astype(BF16), dvm_b)

        def phase9(rows):
            vhat, rstd_v = keep[1, rows, :], rstd_scr[1, rows, :]
            dvl = dvl_scr[rows, :]
            acc_scr[4] += _colsum8(dvl * vhat)
            acc_scr[5] += _colsum8(dvl)
            dvh = dvl * sg
            dz_ref[2, rows, :] = (rstd_v * (dvh - _mean(dvh) - vhat * _mean(dvh * vhat))).astype(BF16)
            qhat, rstd_q = keep[0, rows, :], rstd_scr[0, rows, :]
            ln = qhat * cg + cb
            sig_ln = keep[2, rows, :]
            gz = z_ref[0, rows, :]
            sig_g = keep[3, rows, :]
            dya = dycat_scr[rows, 0:D_MODEL]
            dz_ref[0, rows, :] = (dya * (ln * sig_ln) * (sig_g * (1.0 + gz * (1.0 - sig_g)))).astype(BF16)
            dln = (dya * (gz * sig_g)) * (sig_ln * (1.0 + ln * (1.0 - sig_ln)))
            acc_scr[2] += _colsum8(dln * qhat)
            acc_scr[3] += _colsum8(dln)
            dqh = dln * cg
            dq = rstd_q * (dqh - _mean(dqh) - qhat * _mean(dqh * qhat))
            for blk in range(D_MODEL // 128):
                dq_ref[blk, rows, :] = dq[:, blk * 128:(blk + 1) * 128]
        _row_loop(tm, phase9, unroll=FULL_UNROLL)

        @pl.when(i == n_steps - 1)
        def _():
            for qi in range(8):
                scale = 0.5 * inv_d if qi == 6 else 1.0
                sm_ref[qi:qi + 1, :] = jnp.sum(acc_scr[qi], axis=0, keepdims=True) * scale
            lane = lax.broadcasted_iota(jnp.int32, (CHUNK, CHUNK), 1)
            tile = jnp.zeros((CHUNK, CHUNK), F32)
            for h in range(HEADS):
                col = jnp.sum(dbs_acc[:, h * CHUNK:(h + 1) * CHUNK], axis=1, keepdims=True)
                tile = jnp.where(lane == h, col, tile)
            dbs_ref[...] = tile

    tok = lambda i: (i, 0)
    const2 = lambda i: (0, 0)
    return pl.pallas_call(
        body, name="middle",
        grid=(n_steps,),
        out_shape=(jax.ShapeDtypeStruct((4, s_len, D_MODEL), BF16),
                   jax.ShapeDtypeStruct((D_MODEL // 128, s_len, 128), F32),
                   jax.ShapeDtypeStruct((s_len, D_MODEL), F32),
                   jax.ShapeDtypeStruct((s_len, 2 * D_MODEL), BF16),
                   jax.ShapeDtypeStruct((s_len, D_MODEL), BF16),
                   jax.ShapeDtypeStruct((HEADS, CHUNK, CHUNK), F32),
                   jax.ShapeDtypeStruct((8, D_MODEL), F32),
                   jax.ShapeDtypeStruct((CHUNK, CHUNK), F32)),
        in_specs=[pl.BlockSpec((tm, D_MODEL), tok),
                  pl.BlockSpec((4, tm, D_MODEL), lambda i: (0, i, 0)),
                  pl.BlockSpec((D_MODEL // 128, tm, 128), lambda i: (0, i, 0)),
                  pl.BlockSpec((tm, D_MODEL), tok),
                  *[pl.BlockSpec((1, D_MODEL), const2) for _ in range(5)],
                  pl.BlockSpec((1, 3 * D_MODEL), const2),
                  pl.BlockSpec((HEADS, CHUNK, CHUNK), lambda i: (0, 0, 0)),
                  pl.BlockSpec((CHUNK, D_MODEL), const2),
                  pl.BlockSpec((2 * D_MODEL, D_MODEL), const2, pipeline_mode=pl.Buffered(1))],
        out_specs=(pl.BlockSpec((4, tm, D_MODEL), lambda i: (0, i, 0)),
                   pl.BlockSpec((D_MODEL // 128, tm, 128), lambda i: (0, i, 0)),
                   pl.BlockSpec((tm, D_MODEL), tok),
                   pl.BlockSpec((tm, 2 * D_MODEL), tok),
                   pl.BlockSpec((tm, D_MODEL), tok),
                   pl.BlockSpec((HEADS, CHUNK, CHUNK), lambda i: (0, 0, 0)),
                   pl.BlockSpec((8, D_MODEL), const2),
                   pl.BlockSpec((CHUNK, CHUNK), const2)),
        scratch_shapes=[pltpu.VMEM((tm, D_MODEL), BF16),
                        pltpu.VMEM((tm, D_MODEL), F32),
                        pltpu.VMEM((tm, D_MODEL), F32),
                        pltpu.VMEM((tm, 2 * D_MODEL), F32),
                        pltpu.VMEM((tm, D_MODEL), BF16),
                        pltpu.VMEM((tm, D_MODEL), F32),
                        pltpu.VMEM((8, 8, D_MODEL), F32),
                        pltpu.VMEM((CHUNK, D_MODEL), F32),
                        pltpu.VMEM((5, tm, D_MODEL), F32),
                        pltpu.VMEM((2, tm, 1), F32)],
        compiler_params=_params(dimension_semantics=("arbitrary",)),
    )(x, z6, q, target, *ln_rows, mod, w_s, bs_exp, w_out)


def _conv_bwd(kidx, dq, sig, t2, cw4, sm1, dws, dbs, ycat, dy):
    s_len = dq.shape[1]
    tt = TIME_TILE
    n_blocks = D_MODEL // 128
    shp = [(16, D_MODEL), (D_MODEL, 128), (128, 128)]
    nj, nr, nc = 1, 256, D_MODEL
    kt = n_blocks // N_CHIPS
    tk = s_len // kt

    def body(k_ref, dq_ref, sig_ref, t2_ref, cw_ref, sm1_ref, dws_in, dbs_in, ycat_ref, dy_ref,
             dz_ref, pk1_ref, dws_ref, dbs_ref, loss_ref, gout_ref,
             dqpad, b_pk, b_dws, b_dbs, r_pk, r_dws, r_dbs, w_dws, ssem, rsem, *rs_scr):
        del k_ref
        jb = pl.program_id(0)
        x_, y_, c_ = _pos()

        @pl.when(jb == 0)
        def _():
            b_pk[0:8, :] = _onehot_rows(sm1_ref[1:2, :], 4 * x_ + 2 * y_ + c_)
            for r, src in enumerate([2, 3, 4, 5, 0, 6]):
                b_pk[8 + r:9 + r, :] = sm1_ref[src:src + 1, :]
            b_pk[14:16, :] = jnp.zeros((2, D_MODEL), F32)
            b_dws[...] = dws_in[...]
            b_dbs[...] = dbs_in[...]

        for e, step in enumerate([0, 0, (3 * n_blocks) // 8, (6 * n_blocks) // 8]):
            @pl.when(jb == step)
            def _():
                _butterfly_event(e, [b_pk, b_dws, b_dbs], [r_pk, r_dws, r_dbs], ssem, rsem, _all_reduce_partners(),
                                 wires=[None, w_dws, None])

        @pl.when(jb == (6 * n_blocks) // 8)
        def _():
            pk1_ref[...] = b_pk[...]
            dws_ref[...] = b_dws[...]
            dbs_ref[...] = b_dbs[...]
            loss_ref[...] = jnp.broadcast_to(jnp.sum(b_pk[13:14, :], axis=1, keepdims=True), (8, 128))

        zero = jnp.zeros((CONV_PAD, 128), F32)
        dqpad[0:CONV_PAD, :] = zero
        dqpad[s_len + CONV_PAD:s_len + 2 * CONV_PAD, :] = zero
        dqpad[CONV_PAD:s_len + CONV_PAD, :] = dq_ref[0]
        w = cw_ref[0]

        def bwd(i, carry):
            t0 = i * tt
            dp = jnp.zeros((tt, 128), F32)
            for o in range(1, CONV_WIDTH + 1):
                dp = dp + w[CONV_WIDTH - o:CONV_WIDTH - o + 1, :] * dqpad[pl.ds(t0 + o, tt), :]
            dz_ref[0, 0, pl.ds(t0, tt), :] = (dp * sig_ref[0, pl.ds(t0, tt), :]).astype(BF16)
            dz_ref[0, 1, pl.ds(t0, tt), :] = (dp * t2_ref[0, pl.ds(t0, tt), :]).astype(BF16)
            return carry

        n_iter = s_len // tt
        n_groups = min(8, n_iter)
        rows = tk // n_groups

        def conv_and_matmul(add):
            for g in range(n_groups):
                for it in range(g * n_iter // n_groups, (g + 1) * n_iter // n_groups):
                    bwd(it, 0)
                k0 = pl.multiple_of((jb % kt) * tk + g * rows, rows)
                add(_dot_tn(ycat_ref[g * rows:(g + 1) * rows, :], dy_ref[pl.ds(k0, rows), :]))
        _rs_step(jb // kt, jb % kt, kt, nj, nr, nc, conv_and_matmul, gout_ref, rs_scr)

    const2 = lambda j, k_ref: (0, 0)
    grid_spec = pltpu.PrefetchScalarGridSpec(
        num_scalar_prefetch=1,
        grid=(n_blocks,),
        in_specs=[*[pl.BlockSpec((1, s_len, 128), lambda j, k_ref: (j, 0, 0)) for _ in range(3)],
                  pl.BlockSpec((1, 32, 128), lambda j, k_ref: (j // 2, 0, j % 2)),
                  pl.BlockSpec((8, D_MODEL), const2),
                  pl.BlockSpec((D_MODEL, 128), const2),
                  pl.BlockSpec((128, 128), const2),
                  pl.BlockSpec((tk, 512), lambda j, k_ref: (j % kt, _rs_block(j // kt, k_ref[0], nj))),
                  pl.BlockSpec((s_len, D_MODEL), const2, pipeline_mode=pl.Buffered(1))],
        out_specs=(pl.BlockSpec((1, 2, s_len, 128), lambda j, k_ref: (j, 0, 0, 0)),)
        + tuple(pl.BlockSpec(s, const2) for s in shp) + (pl.BlockSpec((8, 128), const2),
                                                         pl.BlockSpec(memory_space=pl.ANY)),
        scratch_shapes=[pltpu.VMEM((s_len + 2 * CONV_PAD, 128), F32)]
        + [pltpu.VMEM(s, F32) for s in shp]
        + [pltpu.VMEM((3,) + shp[0], F32), pltpu.VMEM((3,) + shp[1], BF16), pltpu.VMEM((3,) + shp[2], F32),
           pltpu.VMEM(shp[1], BF16)]
        + [pltpu.SemaphoreType.DMA((9,)), pltpu.SemaphoreType.DMA((9,))] + _rs_scratch(nj, nr, nc))
    return pl.pallas_call(
        body, name="conv_bwd",
        grid_spec=grid_spec,
        out_shape=(jax.ShapeDtypeStruct((n_blocks, 2, s_len, 128), BF16),)
        + tuple(jax.ShapeDtypeStruct(s, F32) for s in shp) + (jax.ShapeDtypeStruct((8, 128), F32),
                                                              jax.ShapeDtypeStruct((nj, 2, nr, nc), F32)),
        compiler_params=_params(dimension_semantics=("arbitrary",)),
    )(kidx, dq, sig, t2, cw4, sm1, dws, dbs, ycat, dy)


def _bwd_in(dz4, dzc, w12, x, dx2, mod, norm_g, dq, p):
    s_len = x.shape[0]
    tm = TOKEN_TILE
    n_steps = s_len // tm
    tt = TIME_TILE
    n_cblocks = D_MODEL // 128
    parts = max(1, n_steps // n_cblocks)
    nsub = max(1, n_cblocks // n_steps)
    cw = 128 * nsub
    tiles = (s_len // tt) // parts

    def body(dz_ref, dzc_ref, w_ref, x_ref, dx2_ref, mod_ref, g_ref, dq_ref, p_ref, gx_ref, sm_ref, dcw_ref,
             dh_scr, acc_scr, ppad, wacc):
        i = pl.program_id(0)
        part = i % parts

        @pl.when(i == 0)
        def _():
            acc_scr[...] = jnp.zeros_like(acc_scr)

        @pl.when(part == 0)
        def _():
            zero = jnp.zeros((CONV_PAD, 128), F32)
            for sub in range(nsub):
                ppad[sub, 0:CONV_PAD, :] = zero
                ppad[sub, s_len + CONV_PAD:s_len + 2 * CONV_PAD, :] = zero
                ppad[sub, CONV_PAD:s_len + CONV_PAD, :] = p_ref[sub]
            wacc[...] = jnp.zeros_like(wacc)

        def dw_tile(tile):
            t0 = pl.multiple_of((part * tiles + tile) * tt, tt)
            for sub in range(nsub):
                dqt = dq_ref[sub, pl.ds(t0, tt), :]
                for o in range(1, CONV_WIDTH + 1):
                    wacc[sub, o - 1] += _colsum8(dqt * ppad[sub, pl.ds(t0 + o, tt), :])
                wacc[sub, CONV_WIDTH] += _colsum8(dqt)

        def dz_cols(j, hf):
            if j < 4:
                return dz_ref[j, :, hf * 512:(hf + 1) * 512]
            return jnp.concatenate([dzc_ref[4 * hf + blk, j - 4] for blk in range(4)], axis=1)

        dots = [(j, hf) for j in range(N_GROUPS) for hf in range(2)]
        dh = jnp.zeros((tm, D_MODEL), F32)
        for d, (j, hf) in enumerate(dots):
            dh = dh + _dot_nt(dz_cols(j, hf), w_ref[2 * _natural_group(j) + hf])
            for tile in range(d * tiles // len(dots), (d + 1) * tiles // len(dots)):
                dw_tile(tile)
        dh_scr[...] = dh

        @pl.when(part == parts - 1)
        def _():
            for sub in range(nsub):
                for k in range(32):
                    dcw_ref[k:k + 1, sub * 128:(sub + 1) * 128] = jnp.sum(wacc[sub, k], axis=0, keepdims=True)

        scale1 = 1.0 + mod_ref[:, D_MODEL:2 * D_MODEL]
        g = g_ref[...]

        def rows_fn(rows):
            xt = x_ref[rows, :]
            r = lax.rsqrt(_mean(xt * xt) + EPS)
            xn = xt * r
            dhr = dh_scr[rows, :]
            acc_scr[0] += _colsum8(dhr)
            acc_scr[1] += _colsum8(dhr * (xn * g))
            acc_scr[2] += _colsum8(dhr * scale1 * xn)
            dxn = dhr * (g * scale1)
            gx_ref[rows, :] = dx2_ref[rows, :] + r * (dxn - xn * _mean(dxn * xn))
        _row_loop(tm, rows_fn, unroll=FULL_UNROLL)

        @pl.when(i == n_steps - 1)
        def _():
            for qi in range(8):
                sm_ref[qi:qi + 1, :] = jnp.sum(acc_scr[qi], axis=0, keepdims=True)

    tok = lambda i: (i, 0)
    const2 = lambda i: (0, 0)
    return pl.pallas_call(
        body, name="bwd_in",
        grid=(n_steps,),
        out_shape=(jax.ShapeDtypeStruct((s_len, D_MODEL), F32), jax.ShapeDtypeStruct((8, D_MODEL), F32),
                   jax.ShapeDtypeStruct((32, D_MODEL), F32)),
        in_specs=[pl.BlockSpec((4, tm, D_MODEL), lambda i: (0, i, 0)),
                  pl.BlockSpec((n_cblocks, 2, tm, 128), lambda i: (0, 0, i, 0)),
                  pl.BlockSpec((12, D_MODEL, 512), lambda i: (0, 0, 0), pipeline_mode=pl.Buffered(1)),
                  pl.BlockSpec((tm, D_MODEL), tok),
                  pl.BlockSpec((tm, D_MODEL), tok),
                  pl.BlockSpec((1, 3 * D_MODEL), const2),
                  pl.BlockSpec((1, D_MODEL), const2),
                  pl.BlockSpec((nsub, s_len, 128), lambda i: (i // parts, 0, 0)),
                  pl.BlockSpec((nsub, s_len, 128), lambda i: (i // parts, 0, 0))],
        out_specs=(pl.BlockSpec((tm, D_MODEL), tok), pl.BlockSpec((8, D_MODEL), const2),
                   pl.BlockSpec((32, cw), lambda i: (0, i // parts))),
        scratch_shapes=[pltpu.VMEM((tm, D_MODEL), F32), pltpu.VMEM((8, 8, D_MODEL), F32),
                        pltpu.VMEM((nsub, s_len + 2 * CONV_PAD, 128), F32), pltpu.VMEM((nsub, 32, 8, 128), F32)],
        compiler_params=_params(dimension_semantics=("arbitrary",)),
    )(dz4, dzc, w12, x, dx2, mod, norm_g, dq, p)


def _rs_shard_block(i, nj):
    g = jnp.where(i < 2 * nj, i % 2, jnp.where(i < 3 * nj, 2, 3))
    j = jnp.where(i < 2 * nj, i // 2, jnp.where(i < 3 * nj, i - 2 * nj, i - 3 * nj))
    return g, j


def _rs_block(i, k, nj):
    g, j = _rs_shard_block(i, nj)
    return nj * jnp.bitwise_xor(k, 3 - g) + j


def _rs_scratch(nj, nr, nc):
    nblk = N_CHIPS * nj
    return [pltpu.VMEM((2, 2, nr, nc), F32),
            pltpu.VMEM((nblk, nr, nc), F32),
            pltpu.VMEM((2 * nj, nr, nc), BF16),
            pltpu.VMEM((2 * nj, nr, nc), BF16),
            pltpu.VMEM((nj, nr, nc), BF16),
            pltpu.VMEM((nj, nr, nc), BF16),
            pltpu.VMEM((nj, nr, nc), F32),
            pltpu.SemaphoreType.DMA,
            pltpu.SemaphoreType.DMA((nblk,)), pltpu.SemaphoreType.DMA((nblk,)),
            pltpu.SemaphoreType.DMA((2 * nj,)), pltpu.SemaphoreType.DMA((2 * nj,)),
            pltpu.SemaphoreType.DMA((nj,)), pltpu.SemaphoreType.DMA((nj,)),
            pltpu.SemaphoreType.DMA, pltpu.SemaphoreType.DMA]


def _rs_step(i, kk, kt, nj, nr, nc, partial_fn, out_ref, scr):
    nblk = N_CHIPS * nj
    (acc, recv_a, send1, recv1, send2, recv2, own_ps,
     lsem, psend, precv, s1send, s1recv, s2send, s2recv, fsend, frecv) = scr
    x, y, c = _pos()
    sib = (x, y, 1 - c)
    first = (jnp.bitwise_xor(x, 1 - c), jnp.bitwise_xor(y, c), c)
    second = (jnp.bitwise_xor(x, c), jnp.bitwise_xor(y, 1 - c), c)
    slot = i % 2

    @pl.when(kk == 0)
    def _():
        acc[slot] = jnp.zeros((2, nr, nc), F32)

    def pair_copy(ii, sl):
        return _rcopy(acc.at[sl, 1 - c], recv_a.at[ii], psend.at[ii], precv.at[ii], sib)

    def pair_sum(ii, sl):
        cp = pair_copy(ii, sl)
        cp.wait_recv()
        cp.wait_send()
        return acc[sl, c] + recv_a[ii]

    def stage1_copy(src, dst):
        return _rcopy(send1.at[src], recv1.at[dst], s1send.at[src], s1recv.at[dst], first)

    def stage2_copy(j):
        return _rcopy(send2.at[j], recv2.at[j], s2send.at[j], s2recv.at[j], second)

    def finalize(ii, sl):
        ps = pair_sum(ii, sl)
        g, j = _rs_shard_block(ii, nj)

        @pl.when(g == 0)
        def _():
            send1[j] = ps.astype(BF16)
            stage1_copy(j, nj + j).start()

        @pl.when(g == 1 + c)
        def _():
            send1[nj + j] = ps.astype(BF16)
            stage1_copy(nj + j, j).start()

        @pl.when(g == 2 - c)
        def _():
            stage1_copy(0, nj + j).wait_recv()
            send2[j] = (ps + recv1[nj + j].astype(F32)).astype(BF16)
            stage2_copy(j).start()

        @pl.when(g == 3)
        def _():
            own_ps[j] = ps

    @pl.when(jnp.logical_and(kk == kt - 1, i >= 1))
    def _():
        finalize(i - 1, 1 - slot)

    def add_partial(res):
        acc[slot, 0] += res[0:nr]
        acc[slot, 1] += res[nr:2 * nr]
    partial_fn(add_partial)

    @pl.when(kk == kt - 1)
    def _():
        pair_copy(i, slot).start()

        @pl.when(i == nblk - 1)
        def _():
            own_ps[nj - 1] = pair_sum(nblk - 1, (nblk - 1) % 2)
            for j in range(nj):
                stage1_copy(0, j).wait_recv()
                stage2_copy(j).wait_recv()
                own_ps[j] = (own_ps[j] + recv1[j].astype(F32)) + recv2[j].astype(F32)
            loc = pltpu.make_async_copy(own_ps, out_ref.at[:, c], lsem)
            loc.start()
            swap = _rcopy(own_ps, out_ref.at[:, c], fsend, frecv, sib)
            swap.start()
            loc.wait()
            swap.wait_send()
            _rcopy(own_ps, out_ref.at[:, 1 - c], fsend, frecv, sib).wait_recv()
            for s in range(2 * nj):
                stage1_copy(s, 0).wait_send()
            for j in range(nj):
                stage2_copy(j).wait_send()


def _grad_w_in(kidx, h, dz4, dzc, sm2, dcw):
    s_len = h.shape[0]
    tk = min(K_TILE, s_len)
    kt = s_len // tk
    nj, nr, nc = 3, 512, 512
    nblk = N_CHIPS * nj

    def body(k_ref, a_ref, b_ref, bc_ref, sm2_ref, dcw_in, out_ref, pk2_ref, dcwsh_ref, dcb_ref, *scr):
        del k_ref
        b_pk2, b_dcw, r_pk2, r_dcw, bsend, brecv = scr[-6:]
        i = pl.program_id(0)
        kk = pl.program_id(1)
        x, y, c = _pos()
        k = 2 * x + y
        m = _rs_block(i, k, nj)

        @pl.when(jnp.logical_and(kk == 0, i == 0))
        def _():
            b = 4 * x + 2 * y + c
            b_pk2[0:8, :] = _onehot_rows(sm2_ref[0:1, :], b)
            b_pk2[8:16, :] = _onehot_rows(sm2_ref[1:2, :], b)
            b_pk2[16:17, :] = sm2_ref[2:3, :]
            b_pk2[17:24, :] = jnp.zeros((7, D_MODEL), F32)
            b_dcw[...] = dcw_in[...]

        for e in range(4):
            @pl.when(jnp.logical_and(kk == 0, i == (e * nblk) // 4))
            def _():
                _butterfly_event(e, [b_pk2, b_dcw], [r_pk2, r_dcw], bsend, brecv, _all_reduce_partners())

        @pl.when(jnp.logical_and(kk == 0, i == (3 * nblk) // 4))
        def _():
            pk2_ref[...] = b_pk2[...]
            sel = jnp.zeros((32, 256), F32)
            for kc in range(N_CHIPS):
                sel = jnp.where(k == kc, b_dcw[:, kc * 256:(kc + 1) * 256], sel)
            dcwsh_ref[...] = sel
            dcb_ref[...] = b_dcw[31:32, :]

        def partial(add):
            rows = pl.ds(pl.multiple_of(kk * tk, tk), tk)

            @pl.when(m >= 4)
            def _():
                add(_dot_tn(a_ref[rows, :], b_ref[0]))

            @pl.when(m < 4)
            def _():
                add(_dot_tn(a_ref[rows, :], jnp.concatenate([bc_ref[blk, 0] for blk in range(4)], axis=1)))
        _rs_step(i, kk, kt, nj, nr, nc, partial, out_ref, scr[:-6])

    def dz4_index(i, kk, k_ref):
        m = _rs_block(i, k_ref[0], nj)
        use = m >= 4
        return (jnp.where(use, m // 2 - 2, 0), jnp.where(use, kk, 0), jnp.where(use, m % 2, 0))

    def dzc_index(i, kk, k_ref):
        m = _rs_block(i, k_ref[0], nj)
        use = m < 4
        return (jnp.where(use, m % 2, 0), jnp.where(use, m // 2, 0), jnp.where(use, kk, 0), 0)

    const2 = lambda i, kk, k_ref: (0, 0)
    small_out = [(24, D_MODEL), (32, 256), (1, D_MODEL)]
    small_buf = [(24, D_MODEL), (32, D_MODEL)]
    grid_spec = pltpu.PrefetchScalarGridSpec(
        num_scalar_prefetch=1,
        grid=(nblk, kt),
        in_specs=[pl.BlockSpec((s_len, D_MODEL), const2, pipeline_mode=pl.Buffered(1)),
                  pl.BlockSpec((1, tk, 512), dz4_index),
                  pl.BlockSpec((4, 1, tk, 128), dzc_index),
                  pl.BlockSpec((8, D_MODEL), const2),
                  pl.BlockSpec((32, D_MODEL), const2)],
        out_specs=(pl.BlockSpec(memory_space=pl.ANY),) + tuple(pl.BlockSpec(s, const2) for s in small_out),
        scratch_shapes=_rs_scratch(nj, nr, nc)
        + [pltpu.VMEM(s, F32) for s in small_buf] + [pltpu.VMEM((3,) + s, F32) for s in small_buf]
        + [pltpu.SemaphoreType.DMA((6,)), pltpu.SemaphoreType.DMA((6,))])
    return pl.pallas_call(
        body, name="grad_w_in",
        grid_spec=grid_spec,
        out_shape=(jax.ShapeDtypeStruct((nj, 2, nr, nc), F32),) + tuple(jax.ShapeDtypeStruct(s, F32) for s in small_out),
        compiler_params=_params(dimension_semantics=("arbitrary", "arbitrary")),
    )(kidx, h, dz4, dzc, sm2, dcw)


def _adamw_math(w, g, m, v):
    m = ADAM_B1 * m + (1.0 - ADAM_B1) * g
    v = ADAM_B2 * v + (1.0 - ADAM_B2) * (g * g)
    m_hat = m / (1.0 - ADAM_B1 ** ADAM_STEP)
    v_hat = v / (1.0 - ADAM_B2 ** ADAM_STEP)
    delta = -ADAM_LR * (m_hat / (jnp.sqrt(v_hat) + ADAM_EPS) + ADAM_WD * w)
    return delta, m, v


def _adamw_blocked(w, m, v, g4, name):
    nj, _, nr, nc = g4.shape

    def body(w_ref, m_ref, v_ref, g_ref, go_ref, d_ref, mo_ref, vo_ref):
        g = g_ref[0, 0]
        d, mn, vn = _adamw_math(w_ref[...], g, m_ref[...], v_ref[...])
        go_ref[...] = g
        d_ref[...] = d
        mo_ref[...] = mn
        vo_ref[...] = vn

    blk = pl.BlockSpec((nr, nc), lambda j, hf: (hf, j))
    return pl.pallas_call(
        body, name=name,
        grid=(nj, 2),
        out_shape=tuple(jax.ShapeDtypeStruct(w.shape, F32) for _ in range(4)),
        in_specs=[blk, blk, blk, pl.BlockSpec((1, 1, nr, nc), lambda j, hf: (j, hf, 0, 0))],
        out_specs=(blk, blk, blk, blk),
        compiler_params=_params(dimension_semantics=("arbitrary", "arbitrary")),
    )(w, m, v, g4)


def _adamw_w_ada(cact, pk1, pk2, w, m, v):
    rb = 256

    def body(cact_ref, pk1_ref, pk2_ref, w_ref, m_ref, v_ref, g_ref, d_ref, mo_ref, vo_ref, gb_ref, dmod, sel_scr):
        @pl.when(pl.program_id(0) == 0)
        def _():
            x, y, _ = _pos()
            k = 2 * x + y
            dmod[:, 0:D_MODEL] = pk2_ref[0:8, :]
            dmod[:, D_MODEL:2 * D_MODEL] = pk2_ref[8:16, :]
            dmod[:, 2 * D_MODEL:3 * D_MODEL] = pk1_ref[0:8, :]
            gb_ref[...] = jnp.sum(dmod[...], axis=0, keepdims=True)
            sel = jnp.zeros((8, 768), F32)
            for kk in range(N_CHIPS):
                sel = jnp.where(k == kk, dmod[:, kk * 768:(kk + 1) * 768], sel)
            sel_scr[...] = sel

        g = _dot_tn(cact_ref[...].astype(BF16), sel_scr[...].astype(BF16))
        d, mn, vn = _adamw_math(w_ref[...], g, m_ref[...], v_ref[...])
        g_ref[...] = g
        d_ref[...] = d
        mo_ref[...] = mn
        vo_ref[...] = vn

    blk = pl.BlockSpec((rb, 768), lambda i: (i, 0))
    const2 = lambda i: (0, 0)
    return pl.pallas_call(
        body, name="adamw_w_ada",
        grid=(D_MODEL // rb,),
        out_shape=tuple(jax.ShapeDtypeStruct(w.shape, F32) for _ in range(4)) + (
            jax.ShapeDtypeStruct((1, 3 * D_MODEL), F32),),
        in_specs=[pl.BlockSpec((8, rb), lambda i: (0, i)), pl.BlockSpec((16, D_MODEL), const2),
                  pl.BlockSpec((24, D_MODEL), const2), blk, blk, blk],
        out_specs=(blk, blk, blk, blk, pl.BlockSpec((1, 3 * D_MODEL), const2)),
        scratch_shapes=[pltpu.VMEM((8, 3 * D_MODEL), F32), pltpu.VMEM((8, 768), F32)],
        compiler_params=_params(dimension_semantics=("arbitrary",)),
    )(cact, pk1, pk2, w, m, v)


def _adamw_small(items):
    n = len(items)

    def body(*refs):
        ins, outs = refs[:4 * n], refs[4 * n:]
        for i in range(n):
            w_ref, g_ref, m_ref, v_ref = ins[4 * i:4 * i + 4]
            d, mn, vn = _adamw_math(w_ref[...], g_ref[...], m_ref[...], v_ref[...])
            outs[3 * i][...] = d
            outs[3 * i + 1][...] = mn
            outs[3 * i + 2][...] = vn

    flat = [a for it in items for a in it]
    outs = pl.pallas_call(
        body, name="adamw_small",
        out_shape=tuple(jax.ShapeDtypeStruct(it[0].shape, F32) for it in items for _ in range(3)),
        in_specs=[_vmem()] * (4 * n),
        out_specs=tuple(_vmem() for _ in range(3 * n)),
        compiler_params=_params(),
    )(*flat)
    return [tuple(outs[3 * i:3 * i + 3]) for i in range(n)]


def kernel(x, c, w_ada, b_ada, norm_g, w_in, conv_w, conv_b, conv_ln_g, conv_ln_b, sg_ln_g, sg_ln_b, w_s, b_s, w_out, final_g, loss_target, m_w_ada, m_b_ada, m_norm_g, m_w_in, m_conv_w, m_conv_b, m_conv_ln_g, m_conv_ln_b, m_sg_ln_g, m_sg_ln_b, m_w_s, m_b_s, m_w_out, m_final_g, v_w_ada, v_b_ada, v_norm_g, v_w_in, v_conv_w, v_conv_b, v_conv_ln_g, v_conv_ln_b, v_sg_ln_g, v_sg_ln_b, v_w_s, v_b_s, v_w_out, v_final_g):
    s_len = x.shape[1]
    x2d = x[0]
    tgt = loss_target[0]
    row = lambda a: a.reshape(1, -1)

    kidx = (2 * lax.axis_index("x") + lax.axis_index("y")).astype(jnp.int32).reshape(1)
    cw_sh = jnp.pad(conv_w.reshape(CONV_WIDTH, 256), ((0, 1), (0, 0)))
    h, z6, mod, cact, cw4, w_in12 = _front(kidx, x2d, c, w_ada[0], b_ada, norm_g, w_in[0], cw_sh)
    w12 = w_in12.reshape(12, D_MODEL, 512)
    q, p_cm, sig_cm, t2_cm, w_out4 = _conv_fwd(z6, cw4, conv_b, w_out[0])
    w_out_full = w_out4.reshape(2 * D_MODEL, D_MODEL)
    ln_rows = (conv_ln_g, conv_ln_b, sg_ln_g, sg_ln_b, row(final_g))
    bs_exp = jnp.repeat(b_s[0].T, CHUNK, axis=1)
    dz4, dq, dx2, ycat, dy, dws, sm1, dbs = _middle(x2d, z6, q, tgt, ln_rows, mod, w_s[0], bs_exp, w_out_full)
    dzc, pk1, dws_r, dbs_r, loss_t, g_w_out4 = _conv_bwd(
        kidx, dq, sig_cm, t2_cm, cw4, sm1, dws.reshape(D_MODEL, CHUNK), dbs, ycat, dy)
    grad_x, sm2, dcw = _bwd_in(dz4, dzc, w12, x2d, dx2, mod, norm_g, dq, p_cm)
    g_w_in4, pk2, dcw_sh, dcb = _grad_w_in(kidx, h, dz4, dzc, sm2, dcw)

    g_w_in, d_w_in, nm_w_in, nv_w_in = _adamw_blocked(w_in[0], m_w_in[0], v_w_in[0], g_w_in4, "adamw_w_in")
    g_w_out, d_w_out, nm_w_out, nv_w_out = _adamw_blocked(w_out[0], m_w_out[0], v_w_out[0], g_w_out4, "adamw_w_out")
    g_w_ada, d_w_ada, nm_w_ada, nv_w_ada, g_b_ada = _adamw_w_ada(cact, pk1, pk2, w_ada[0], m_w_ada[0], v_w_ada[0])

    g_norm_g = pk2[16:17]
    g_cln_g, g_cln_b, g_sln_g, g_sln_b, g_final = (pk1[8 + i:9 + i] for i in range(5))
    loss = loss_t[0, 0]
    g_conv_w = dcw_sh[:CONV_WIDTH]
    g_w_s = dws_r
    g_b_s = dbs_r[:, :HEADS].T
    small = [
        (b_ada, g_b_ada, m_b_ada, v_b_ada),
        (norm_g, g_norm_g, m_norm_g, v_norm_g),
        (conv_w.reshape(CONV_WIDTH, 256), g_conv_w, m_conv_w.reshape(CONV_WIDTH, 256), v_conv_w.reshape(CONV_WIDTH, 256)),
        (conv_b, dcb, m_conv_b, v_conv_b),
        (conv_ln_g, g_cln_g, m_conv_ln_g, v_conv_ln_g),
        (conv_ln_b, g_cln_b, m_conv_ln_b, v_conv_ln_b),
        (sg_ln_g, g_sln_g, m_sg_ln_g, v_sg_ln_g),
        (sg_ln_b, g_sln_b, m_sg_ln_b, v_sg_ln_b),
        (w_s.reshape(D_MODEL, CHUNK), g_w_s, m_w_s.reshape(D_MODEL, CHUNK), v_w_s.reshape(D_MODEL, CHUNK)),
        (b_s[0], g_b_s, m_b_s[0], v_b_s[0]),
        (row(final_g), g_final, row(m_final_g), row(v_final_g)),
    ]
    upd = _adamw_small(small)

    shapes = [w_ada.shape, b_ada.shape, norm_g.shape, w_in.shape, conv_w.shape, conv_b.shape, conv_ln_g.shape,
              conv_ln_b.shape, sg_ln_g.shape, sg_ln_b.shape, w_s.shape, b_s.shape, w_out.shape, final_g.shape]
    grads = [g_w_ada, g_b_ada, g_norm_g, g_w_in, g_conv_w, dcb, g_cln_g, g_cln_b, g_sln_g, g_sln_b, g_w_s, g_b_s,
             g_w_out, g_final]
    big = {0: (d_w_ada, nm_w_ada, nv_w_ada), 3: (d_w_in, nm_w_in, nv_w_in), 12: (d_w_out, nm_w_out, nv_w_out)}
    small_pos = [1, 2, 4, 5, 6, 7, 8, 9, 10, 11, 13]
    trip = [None] * 14
    for i, t in big.items():
        trip[i] = t
    for i, t in zip(small_pos, upd):
        trip[i] = t
    fit = lambda arrs: [a.reshape(s) for a, s in zip(arrs, shapes)]
    return (loss, grad_x.reshape(x.shape), *fit(grads), *fit([t[0] for t in trip]), *fit([t[1] for t in trip]),
            *fit([t[2] for t in trip]))
```

```python
import jax
import jax.numpy as jnp
from jax import lax
from jax.experimental import pallas as pl
from jax.experimental.pallas import tpu as pltpu

F32 = jnp.float32
BF16 = jnp.bfloat16
MESH = pl.DeviceIdType.MESH

D_MODEL = 1024
N_CHIPS = 4
HEADS = 8
CHUNK = 128
CONV_WIDTH = 31
CONV_HALF = CONV_WIDTH // 2
CONV_PAD = 16
EPS = 1e-6
ADAM_LR = 0.001
ADAM_B1 = 0.9
ADAM_B2 = 0.999
ADAM_EPS = 1e-08
ADAM_WD = 0.01
ADAM_STEP = 10

V7X_VMEM_BYTES = 64 * 1024 * 1024
VMEM_LIMIT = V7X_VMEM_BYTES - 8 * 1024 * 1024
ROWS = 16
UNROLL = 8
TOKEN_TILE = 256
FULL_UNROLL = TOKEN_TILE // ROWS
TIME_TILE = 128
K_TILE = 4096

N_GROUPS = 6


def _natural_group(j):
    return (j + 2) % N_GROUPS


def _pos():
    return lax.axis_index("x"), lax.axis_index("y"), lax.axis_index("c")


def _rcopy(src, dst, ssem, rsem, dev):
    return pltpu.make_async_remote_copy(src_ref=src, dst_ref=dst, send_sem=ssem, recv_sem=rsem,
                                        device_id=dev, device_id_type=MESH)


def _vmem():
    return pl.BlockSpec(memory_space=pltpu.VMEM)


def _params(**kw):
    return pltpu.CompilerParams(vmem_limit_bytes=VMEM_LIMIT, **kw)


def _sigmoid(v):
    return 0.5 * jnp.tanh(0.5 * v) + 0.5


def _row_loop(n_rows, body, unroll=1):
    def step(r, carry):
        body(pl.ds(pl.multiple_of(r * ROWS, ROWS), ROWS))
        return carry
    lax.fori_loop(0, n_rows // ROWS, step, 0, unroll=unroll)


def _colsum8(v):
    return v.reshape(v.shape[0] // 8, 8, v.shape[1]).sum(axis=0)


def _mean(v):
    return jnp.mean(v, axis=-1, keepdims=True)


def _dot_nn(a, b):
    return jnp.dot(a, b, preferred_element_type=F32)


def _dot_nt(a, b):
    return lax.dot_general(a, b, (((1,), (1,)), ((), ())), preferred_element_type=F32)


def _dot_tn(a, b):
    return lax.dot_general(a, b, (((0,), (0,)), ((), ())), preferred_element_type=F32)


def _remote_chip(k, r):
    return jnp.bitwise_xor(k, r + 1)


def _front(kidx, x, c, w_ada, b_ada, norm_g, w_in, cw):
    s_len = x.shape[0]
    tmh = min(512, s_len)
    tmz = min(2048, s_len)
    nh = s_len // tmh
    nz = s_len // tmz
    n_steps = nh + 12 * nz

    def remote_block(q):
        return jnp.where(q < 6, q % 2, 2), jnp.where(q < 6, q // 2, q - 6)

    def block_of(i, k):
        r, j = remote_block(jnp.maximum(i - 3, 0))
        return jnp.where(i < 3, 3 * k + i, 3 * _remote_chip(k, r) + j)

    def body(k_ref, x_ref, c_ref, bada_ref, g_ref, cw_ref, wada_hbm, win_hbm,
             h_ref, z_ref, mod_ref, cact_ref, cw4_ref, w12_hbm,
             h_all, wbuf, stage, wada_v, cslab, mslab, cw4_s,
             lsem, csend, crecv, msend, mrecv, wsend, wrecv, isend, irecv, fsend, frecv, osem):
        del k_ref
        t = pl.program_id(0)
        x_, y_, c_ = _pos()
        k = 2 * x_ + y_
        b = 4 * x_ + 2 * y_ + c_
        sib = (x_, y_, 1 - c_)

        def dev_of(r):
            kk = _remote_chip(k, r)
            return (kk // 2, kk % 2, c_)

        def ici(q, kk):
            r, j = remote_block(q)
            return _rcopy(wbuf.at[3 * kk + j, c_], wbuf.at[3 * kk + j, c_], isend.at[q], irecv.at[q], dev_of(r))

        def fwd(q, hf):
            r, j = remote_block(q)
            blk = 3 * _remote_chip(k, r) + j
            return _rcopy(wbuf.at[blk, hf], wbuf.at[blk, hf], fsend.at[q], frecv.at[q], sib)

        def c_copy(q, src):
            d = jnp.bitwise_xor(b, q)
            return _rcopy(cslab.at[src], cslab.at[src], csend.at[q - 1], crecv.at[q - 1], (d // 4, (d // 2) % 2, d % 2))

        def m_copy(r, kk):
            return _rcopy(mslab.at[kk], mslab.at[kk], msend.at[r], mrecv.at[r], dev_of(r))

        def cw_copy(r, kk):
            return _rcopy(cw4_s.at[kk], cw4_s.at[kk], wsend.at[r], wrecv.at[r], dev_of(r))

        def to_hbm(i):
            m = block_of(i, k)
            return pltpu.make_async_copy(wbuf.at[m], w12_hbm.at[m], osem.at[i])

        @pl.when(t == 0)
        def _():
            ld_w = pltpu.make_async_copy(win_hbm, stage, lsem.at[0])
            ld_w.start()
            ld_a = pltpu.make_async_copy(wada_hbm, wada_v, lsem.at[1])
            ld_a.start()
            cslab[b] = jnp.broadcast_to(c_ref[...], (8, D_MODEL))
            for q in range(1, 8):
                c_copy(q, b).start()
            cw4_s[k] = cw_ref[...]
            for r in range(3):
                cw_copy(r, k).start()
            ld_w.wait()
            for j in range(3):
                for hf in range(2):
                    wbuf[3 * k + j, hf] = stage[hf * 512:(hf + 1) * 512, j * 512:(j + 1) * 512].astype(BF16)
            for i in range(3):
                to_hbm(i).start()
            for q in range(1, 8):
                c_copy(q, jnp.bitwise_xor(b, q)).wait_recv()
            row = lax.broadcasted_iota(jnp.int32, (8, D_MODEL), 0)
            call = jnp.zeros((8, D_MODEL), F32)
            for d in range(8):
                call = jnp.where(row == d, cslab[d], call)
            cact = call * _sigmoid(call)
            cact_ref[...] = cact
            ld_a.wait()
            mslab[k] = _dot_nn(cact.astype(BF16), wada_v[...].astype(BF16))
            for r in range(3):
                m_copy(r, k).start()
            for q in range(9):
                ici(q, k).start()
            for r in range(3):
                m_copy(r, _remote_chip(k, r)).wait_recv()
            row8 = lax.broadcasted_iota(jnp.int32, (8, 768), 0)
            for kk in range(N_CHIPS):
                piece = jnp.sum(jnp.where(row8 == b, mslab[kk], 0.0), axis=0, keepdims=True)
                mod_ref[:, kk * 768:(kk + 1) * 768] = piece + bada_ref[:, kk * 768:(kk + 1) * 768]
            for r in range(3):
                cw_copy(r, _remote_chip(k, r)).wait_recv()
            cw4_ref[...] = cw4_s[...]

        @pl.when(t < nh)
        def _():
            shift = mod_ref[:, 0:D_MODEL]
            scale1 = 1.0 + mod_ref[:, D_MODEL:2 * D_MODEL]
            g = g_ref[...]
            base = t * tmh

            def rows_fn(rows):
                xt = x_ref[rows, :]
                r = lax.rsqrt(_mean(xt * xt) + EPS)
                hv = ((xt * r * g) * scale1 + shift).astype(BF16)
                h_ref[rows, :] = hv
                h_all[pl.ds(pl.multiple_of(base + rows.start, ROWS), ROWS), :] = hv
            _row_loop(tmh, rows_fn, unroll=UNROLL)

        @pl.when(t >= nh)
        def _():
            u = t - nh
            i = u // nz
            rt = u % nz
            @pl.when(jnp.logical_and(rt == 0, jnp.logical_and(i >= 2, i <= 10)))
            def _():
                q = i - 2
                r, _ = remote_block(q)
                ici(q, _remote_chip(k, r)).wait_recv()
                fwd(q, c_).start()

            @pl.when(jnp.logical_and(rt == 0, i >= 3))
            def _():
                fwd(i - 3, 1 - c_).wait_recv()
                to_hbm(i).start()
            m = block_of(i, k)
            hb = h_all[pl.ds(pl.multiple_of(rt * tmz, tmz), tmz), :]
            z_ref[0] = _dot_nn(hb, wbuf[m].reshape(D_MODEL, 512))

        @pl.when(t == n_steps - 1)
        def _():
            for q in range(1, 8):
                c_copy(q, b).wait_send()
            for r in range(3):
                m_copy(r, k).wait_send()
                cw_copy(r, k).wait_send()
            for q in range(9):
                ici(q, k).wait_send()
                fwd(q, c_).wait_send()
            for i in range(12):
                to_hbm(i).wait()

    def z_index(t, k_ref):
        u = jnp.maximum(t - nh, 0)
        m = block_of(u // nz, k_ref[0])
        return ((m // 2 + 4) % N_GROUPS, u % nz, m % 2)

    tok = lambda t, k_ref: (jnp.minimum(t, nh - 1), 0)
    const2 = lambda t, k_ref: (0, 0)
    grid_spec = pltpu.PrefetchScalarGridSpec(
        num_scalar_prefetch=1,
        grid=(n_steps,),
        in_specs=[pl.BlockSpec((tmh, D_MODEL), tok),
                  pl.BlockSpec((1, D_MODEL), const2),
                  pl.BlockSpec((1, 3 * D_MODEL), const2),
                  pl.BlockSpec((1, D_MODEL), const2),
                  pl.BlockSpec((32, 256), const2),
                  pl.BlockSpec(memory_space=pl.ANY),
                  pl.BlockSpec(memory_space=pl.ANY)],
        out_specs=(pl.BlockSpec((tmh, D_MODEL), tok),
                   pl.BlockSpec((1, tmz, 512), z_index),
                   pl.BlockSpec((1, 3 * D_MODEL), const2),
                   pl.BlockSpec((8, D_MODEL), const2),
                   pl.BlockSpec((N_CHIPS, 32, 256), lambda t, k_ref: (0, 0, 0)),
                   pl.BlockSpec(memory_space=pl.ANY)),
        scratch_shapes=[pltpu.VMEM((s_len, D_MODEL), BF16),
                        pltpu.VMEM((12, 2, 512, 512), BF16),
                        pltpu.VMEM((D_MODEL, 1536), F32),
                        pltpu.VMEM((D_MODEL, 768), F32),
                        pltpu.VMEM((8, 8, D_MODEL), F32),
                        pltpu.VMEM((N_CHIPS, 8, 768), F32),
                        pltpu.VMEM((N_CHIPS, 32, 256), F32),
                        pltpu.SemaphoreType.DMA((2,)),
                        pltpu.SemaphoreType.DMA((7,)), pltpu.SemaphoreType.DMA((7,)),
                        pltpu.SemaphoreType.DMA((3,)), pltpu.SemaphoreType.DMA((3,)),
                        pltpu.SemaphoreType.DMA((3,)), pltpu.SemaphoreType.DMA((3,)),
                        pltpu.SemaphoreType.DMA((9,)), pltpu.SemaphoreType.DMA((9,)),
                        pltpu.SemaphoreType.DMA((9,)), pltpu.SemaphoreType.DMA((9,)),
                        pltpu.SemaphoreType.DMA((12,))])
    return pl.pallas_call(
        body, name="front",
        grid_spec=grid_spec,
        out_shape=(jax.ShapeDtypeStruct((s_len, D_MODEL), BF16),
                   jax.ShapeDtypeStruct((N_GROUPS, s_len, D_MODEL), F32),
                   jax.ShapeDtypeStruct((1, 3 * D_MODEL), F32),
                   jax.ShapeDtypeStruct((8, D_MODEL), F32),
                   jax.ShapeDtypeStruct((N_CHIPS, 32, 256), F32),
                   jax.ShapeDtypeStruct((12, 2, 512, 512), BF16)),
        compiler_params=_params(dimension_semantics=("arbitrary",)),
    )(kidx, x, c, b_ada, norm_g, cw, w_ada, w_in)


def _all_reduce_partners():
    x, y, c = _pos()
    return [(x, y, 1 - c), (x, 1 - y, c), (1 - x, y, c)]


def _butterfly_event(e, bufs, recvs, ssem, rsem, partners, wires=None):
    n = len(bufs)
    wires = wires or [None] * n

    def copies(s):
        return [_rcopy(buf if wire is None else wire, recv.at[s], ssem.at[s * n + i], rsem.at[s * n + i], partners[s])
                for i, (buf, recv, wire) in enumerate(zip(bufs, recvs, wires))]

    if e > 0:
        for cp in copies(e - 1):
            cp.wait()
        for buf, recv, wire in zip(bufs, recvs, wires):
            mine = buf[...] if wire is None else wire[...].astype(F32)
            buf[...] = mine + recv[e - 1].astype(F32)
    if e < len(partners):
        for buf, wire in zip(bufs, wires):
            if wire is not None:
                wire[...] = buf[...].astype(BF16)
        for cp in copies(e):
            cp.start()


def _onehot_rows(v, b):
    row = lax.broadcasted_iota(jnp.int32, (8, v.shape[1]), 0)
    return jnp.where(row == b, jnp.broadcast_to(v, (8, v.shape[1])), 0.0)


def _conv_fwd(z6, cw4, conv_b, w_out):
    s_len = z6.shape[1]
    tt = TIME_TILE
    n_blocks = D_MODEL // 128

    def body(z_ref, cw_ref, cb_ref, wout_hbm, q_ref, p_ref, sig_ref, t2_ref, wout4_hbm,
             ppad, stage, wbuf, lsem, isend, irecv, fsend, frecv, osem):
        jb = pl.program_id(0)
        x_, y_, c_ = _pos()
        k = 2 * x_ + y_
        sib = (x_, y_, 1 - c_)

        def ici(r, kk):
            rk = _remote_chip(k, r)
            return _rcopy(wbuf.at[kk, c_], wbuf.at[kk, c_], isend.at[r], irecv.at[r], (rk // 2, rk % 2, c_))

        def fwd(r, hf):
            kk = _remote_chip(k, r)
            return _rcopy(wbuf.at[kk, hf], wbuf.at[kk, hf], fsend.at[r], frecv.at[r], sib)

        @pl.when(jb == 0)
        def _():
            ld = pltpu.make_async_copy(wout_hbm, stage, lsem)
            ld.start()
            ld.wait()
            for hf in range(2):
                wbuf[k, hf] = stage[hf * 256:(hf + 1) * 256, :].astype(BF16)
            for r in range(3):
                ici(r, k).start()

        @pl.when(jb == (5 * n_blocks) // 8)
        def _():
            for r in range(3):
                ici(r, _remote_chip(k, r)).wait_recv()
                fwd(r, c_).start()

        zero = jnp.zeros((CONV_PAD, 128), F32)
        ppad[0:CONV_PAD, :] = zero
        ppad[s_len + CONV_PAD:s_len + 2 * CONV_PAD, :] = zero

        def fill(i, carry):
            t0 = pl.multiple_of(i * tt, tt)
            sig = _sigmoid(z_ref[1, pl.ds(t0, tt), :])
            p = z_ref[0, pl.ds(t0, tt), :] * sig
            ppad[pl.ds(CONV_PAD + t0, tt), :] = p
            p_ref[0, pl.ds(t0, tt), :] = p
            sig_ref[0, pl.ds(t0, tt), :] = sig
            t2_ref[0, pl.ds(t0, tt), :] = p * (1.0 - sig)
            return carry
        lax.fori_loop(0, s_len // tt, fill, 0)
        w = cw_ref[0]
        bias = cb_ref[...]

        def conv(i, carry):
            t0 = pl.multiple_of(i * tt, tt)
            acc = jnp.broadcast_to(bias, (tt, 128))
            for o in range(1, CONV_WIDTH + 1):
                acc = acc + w[o - 1:o, :] * ppad[pl.ds(t0 + o, tt), :]
            q_ref[0, pl.ds(t0, tt), :] = acc
            return carry
        lax.fori_loop(0, s_len // tt, conv, 0)

        @pl.when(jb == n_blocks - 1)
        def _():
            for r in range(3):
                fwd(r, 1 - c_).wait_recv()
            out = pltpu.make_async_copy(wbuf, wout4_hbm, osem)
            out.start()
            for r in range(3):
                ici(r, k).wait_send()
                fwd(r, c_).wait_send()
            out.wait()

    return pl.pallas_call(
        body, name="conv_fwd",
        grid=(n_blocks,),
        out_shape=tuple(jax.ShapeDtypeStruct((n_blocks, s_len, 128), F32) for _ in range(4))
        + (jax.ShapeDtypeStruct((N_CHIPS, 2, 256, D_MODEL), BF16),),
        in_specs=[pl.BlockSpec((2, s_len, 128), lambda j: (2, 0, j)),
                  pl.BlockSpec((1, 32, 128), lambda j: (j // 2, 0, j % 2)),
                  pl.BlockSpec((1, 128), lambda j: (0, j)),
                  pl.BlockSpec(memory_space=pl.ANY)],
        out_specs=tuple(pl.BlockSpec((1, s_len, 128), lambda j: (j, 0, 0)) for _ in range(4))
        + (pl.BlockSpec(memory_space=pl.ANY),),
        scratch_shapes=[pltpu.VMEM((s_len + 2 * CONV_PAD, 128), F32),
                        pltpu.VMEM((512, D_MODEL), F32),
                        pltpu.VMEM((N_CHIPS, 2, 256, D_MODEL), BF16),
                        pltpu.SemaphoreType.DMA,
                        pltpu.SemaphoreType.DMA((3,)), pltpu.SemaphoreType.DMA((3,)),
                        pltpu.SemaphoreType.DMA((3,)), pltpu.SemaphoreType.DMA((3,)),
                        pltpu.SemaphoreType.DMA],
        compiler_params=_params(dimension_semantics=("arbitrary",)),
    )(z6, cw4, conv_b, w_out)


def _middle(x, z6, q, target, ln_rows, mod, w_s, bs_exp, w_out):
    s_len = x.shape[0]
    tm = TOKEN_TILE
    n_steps = s_len // tm
    n_chunks = tm // CHUNK
    inv_d = 1.0 / D_MODEL

    def body(x_ref, z_ref, q_ref, tgt_ref, cg_ref, cb_ref, sg_ref, sb_ref, fg_ref, mod_ref, ws_ref, bs_ref, wout_ref,
             dz_ref, dq_ref, dx2_ref, ycat_ref, dy_ref, dws_ref, sm_ref, dbs_ref,
             vl_scr, vm_scr, y_scr, dycat_scr, dvm_scr, dvl_scr, acc_scr, dbs_acc, keep, rstd_scr):
        i = pl.program_id(0)

        @pl.when(i == 0)
        def _():
            acc_scr[...] = jnp.zeros_like(acc_scr)
            dbs_acc[...] = jnp.zeros_like(dbs_acc)
            dws_ref[...] = jnp.zeros_like(dws_ref)

        cg, cb, sg, sb, fg = cg_ref[...], cb_ref[...], sg_ref[...], sb_ref[...], fg_ref[...]
        gm = mod_ref[:, 2 * D_MODEL:3 * D_MODEL]

        def norm_stats(t):
            c = t - _mean(t)
            rstd = lax.rsqrt(_mean(c * c) + EPS)
            return c * rstd, rstd

        def phase1(rows):
            qhat, rstd_q = norm_stats(jnp.concatenate([q_ref[blk, rows, :] for blk in range(D_MODEL // 128)], axis=1))
            ln = qhat * cg + cb
            gz = z_ref[0, rows, :]
            sig_ln = _sigmoid(ln)
            sig_g = _sigmoid(gz)
            ycat_ref[rows, 0:D_MODEL] = ((ln * sig_ln) * (gz * sig_g)).astype(BF16)
            vhat, rstd_v = norm_stats(z_ref[2, rows, :])
            vl_scr[rows, :] = (vhat * sg + sb).astype(BF16)
            keep[0, rows, :] = qhat
            keep[1, rows, :] = vhat
            keep[2, rows, :] = sig_ln
            keep[3, rows, :] = sig_g
            rstd_scr[0, rows, :] = rstd_q
            rstd_scr[1, rows, :] = rstd_v
        _row_loop(tm, phase1, unroll=FULL_UNROLL)

        for ch in range(n_chunks):
            r0 = ch * CHUNK
            for h in range(HEADS):
                c0 = h * CHUNK
                vm_scr[r0:r0 + CHUNK, c0:c0 + CHUNK] = (
                    _dot_nn(ws_ref[h].astype(BF16), vl_scr[r0:r0 + CHUNK, c0:c0 + CHUNK]) + bs_ref[:, c0:c0 + CHUNK])

        def phase3(rows):
            bg = z_ref[3, rows, :]
            sig_b = _sigmoid(bg)
            keep[4, rows, :] = sig_b
            ycat_ref[rows, D_MODEL:2 * D_MODEL] = (z_ref[1, rows, :] * vm_scr[rows, :] * (bg * sig_b)).astype(BF16)
        _row_loop(tm, phase3, unroll=FULL_UNROLL)

        y_scr[...] = _dot_nn(ycat_ref[...], wout_ref[...])

        def phase5(rows):
            y = y_scr[rows, :]
            x2 = x_ref[rows, :] + gm * y
            r2 = lax.rsqrt(_mean(x2 * x2) + EPS)
            xn2 = x2 * r2
            diff = xn2 * fg - tgt_ref[rows, :]
            acc_scr[6] += _colsum8(diff * diff)
            dout = diff * inv_d
            acc_scr[0] += _colsum8(dout * xn2)
            dxn = dout * fg
            dx2 = r2 * (dxn - xn2 * _mean(dxn * xn2))
            dx2_ref[rows, :] = dx2
            acc_scr[1] += _colsum8(dx2 * y)
            dy_ref[rows, :] = (dx2 * gm).astype(BF16)
        _row_loop(tm, phase5, unroll=FULL_UNROLL)

        dycat_scr[...] = _dot_nt(dy_ref[...], wout_ref[...])

        def phase7(rows):
            dyb = dycat_scr[rows, D_MODEL:2 * D_MODEL]
            u = z_ref[1, rows, :]
            bg = z_ref[3, rows, :]
            vm = vm_scr[rows, :]
            sig = keep[4, rows, :]
            silu = bg * sig
            dz_ref[1, rows, :] = (dyb * vm * silu).astype(BF16)
            dvm = dyb * u * silu
            dz_ref[3, rows, :] = (dyb * u * vm * (sig * (1.0 + bg * (1.0 - sig)))).astype(BF16)
            dvm_scr[rows, :] = dvm.astype(BF16)
            pos = pl.ds(pl.multiple_of(rows.start % CHUNK, ROWS), ROWS)
            dbs_acc[pos, :] += dvm
        _row_loop(tm, phase7, unroll=FULL_UNROLL)

        for ch in range(n_chunks):
            r0 = ch * CHUNK
            for h in range(HEADS):
                c0 = h * CHUNK
                dvm_b = dvm_scr[r0:r0 + CHUNK, c0:c0 + CHUNK]
                dws_ref[h] += _dot_nt(dvm_b, vl_scr[r0:r0 + CHUNK, c0:c0 + CHUNK])
                dvl_scr[r0:r0 + CHUNK, c0:c0 + CHUNK] = _dot_tn(ws_ref[h].astype(BF16), dvm_b)

        def phase9(rows):
            vhat, rstd_v = keep[1, rows, :], rstd_scr[1, rows, :]
            dvl = dvl_scr[rows, :]
            acc_scr[4] += _colsum8(dvl * vhat)
            acc_scr[5] += _colsum8(dvl)
            dvh = dvl * sg
            dz_ref[2, rows, :] = (rstd_v * (dvh - _mean(dvh) - vhat * _mean(dvh * vhat))).astype(BF16)
            qhat, rstd_q = keep[0, rows, :], rstd_scr[0, rows, :]
            ln = qhat * cg + cb
            sig_ln = keep[2, rows, :]
            gz = z_ref[0, rows, :]
            sig_g = keep[3, rows, :]
            dya = dycat_scr[rows, 0:D_MODEL]
            dz_ref[0, rows, :] = (dya * (ln * sig_ln) * (sig_g * (1.0 + gz * (1.0 - sig_g)))).astype(BF16)
            dln = (dya * (gz * sig_g)) * (sig_ln * (1.0 + ln * (1.0 - sig_ln)))
            acc_scr[2] += _colsum8(dln * qhat)
            acc_scr[3] += _colsum8(dln)
            dqh = dln * cg
            dq = rstd_q * (dqh - _mean(dqh) - qhat * _mean(dqh * qhat))
            for blk in range(D_MODEL // 128):
                dq_ref[blk, rows, :] = dq[:, blk * 128:(blk + 1) * 128]
        _row_loop(tm, phase9, unroll=FULL_UNROLL)

        @pl.when(i == n_steps - 1)
        def _():
            for qi in range(8):
                scale = 0.5 * inv_d if qi == 6 else 1.0
                sm_ref[qi:qi + 1, :] = jnp.sum(acc_scr[qi], axis=0, keepdims=True) * scale
            lane = lax.broadcasted_iota(jnp.int32, (CHUNK, CHUNK), 1)
            tile = jnp.zeros((CHUNK, CHUNK), F32)
            for h in range(HEADS):
                col = jnp.sum(dbs_acc[:, h * CHUNK:(h + 1) * CHUNK], axis=1, keepdims=True)
                tile = jnp.where(lane == h, col, tile)
            dbs_ref[...] = tile

    tok = lambda i: (i, 0)
    const2 = lambda i: (0, 0)
    return pl.pallas_call(
        body, name="middle",
        grid=(n_steps,),
        out_shape=(jax.ShapeDtypeStruct((4, s_len, D_MODEL), BF16),
                   jax.ShapeDtypeStruct((D_MODEL // 128, s_len, 128), F32),
                   jax.ShapeDtypeStruct((s_len, D_MODEL), F32),
                   jax.ShapeDtypeStruct((s_len, 2 * D_MODEL), BF16),
                   jax.ShapeDtypeStruct((s_len, D_MODEL), BF16),
                   jax.ShapeDtypeStruct((HEADS, CHUNK, CHUNK), F32),
                   jax.ShapeDtypeStruct((8, D_MODEL), F32),
                   jax.ShapeDtypeStruct((CHUNK, CHUNK), F32)),
        in_specs=[pl.BlockSpec((tm, D_MODEL), tok),
                  pl.BlockSpec((4, tm, D_MODEL), lambda i: (0, i, 0)),
                  pl.BlockSpec((D_MODEL // 128, tm, 128), lambda i: (0, i, 0)),
                  pl.BlockSpec((tm, D_MODEL), tok),
                  *[pl.BlockSpec((1, D_MODEL), const2) for _ in range(5)],
                  pl.BlockSpec((1, 3 * D_MODEL), const2),
                  pl.BlockSpec((HEADS, CHUNK, CHUNK), lambda i: (0, 0, 0)),
                  pl.BlockSpec((CHUNK, D_MODEL), const2),
                  pl.BlockSpec((2 * D_MODEL, D_MODEL), const2, pipeline_mode=pl.Buffered(1))],
        out_specs=(pl.BlockSpec((4, tm, D_MODEL), lambda i: (0, i, 0)),
                   pl.BlockSpec((D_MODEL // 128, tm, 128), lambda i: (0, i, 0)),
                   pl.BlockSpec((tm, D_MODEL), tok),
                   pl.BlockSpec((tm, 2 * D_MODEL), tok),
                   pl.BlockSpec((tm, D_MODEL), tok),
                   pl.BlockSpec((HEADS, CHUNK, CHUNK), lambda i: (0, 0, 0)),
                   pl.BlockSpec((8, D_MODEL), const2),
                   pl.BlockSpec((CHUNK, CHUNK), const2)),
        scratch_shapes=[pltpu.VMEM((tm, D_MODEL), BF16),
                        pltpu.VMEM((tm, D_MODEL), F32),
                        pltpu.VMEM((tm, D_MODEL), F32),
                        pltpu.VMEM((tm, 2 * D_MODEL), F32),
                        pltpu.VMEM((tm, D_MODEL), BF16),
                        pltpu.VMEM((tm, D_MODEL), F32),
                        pltpu.VMEM((8, 8, D_MODEL), F32),
                        pltpu.VMEM((CHUNK, D_MODEL), F32),
                        pltpu.VMEM((5, tm, D_MODEL), F32),
                        pltpu.VMEM((2, tm, 1), F32)],
        compiler_params=_params(dimension_semantics=("arbitrary",)),
    )(x, z6, q, target, *ln_rows, mod, w_s, bs_exp, w_out)


def _conv_bwd(kidx, dq, sig, t2, cw4, sm1, dws, dbs, ycat, dy):
    s_len = dq.shape[1]
    tt = TIME_TILE
    n_blocks = D_MODEL // 128
    shp = [(16, D_MODEL), (D_MODEL, 128), (128, 128)]
    nj, nr, nc = 1, 256, D_MODEL
    kt = n_blocks // N_CHIPS
    tk = s_len // kt

    def body(k_ref, dq_ref, sig_ref, t2_ref, cw_ref, sm1_ref, dws_in, dbs_in, ycat_ref, dy_ref,
             dz_ref, pk1_ref, dws_ref, dbs_ref, loss_ref, gout_ref,
             dqpad, b_pk, b_dws, b_dbs, r_pk, r_dws, r_dbs, w_dws, ssem, rsem, *rs_scr):
        del k_ref
        jb = pl.program_id(0)
        x_, y_, c_ = _pos()

        @pl.when(jb == 0)
        def _():
            b_pk[0:8, :] = _onehot_rows(sm1_ref[1:2, :], 4 * x_ + 2 * y_ + c_)
            for r, src in enumerate([2, 3, 4, 5, 0, 6]):
                b_pk[8 + r:9 + r, :] = sm1_ref[src:src + 1, :]
            b_pk[14:16, :] = jnp.zeros((2, D_MODEL), F32)
            b_dws[...] = dws_in[...]
            b_dbs[...] = dbs_in[...]

        for e, step in enumerate([0, 0, (3 * n_blocks) // 8, (6 * n_blocks) // 8]):
            @pl.when(jb == step)
            def _():
                _butterfly_event(e, [b_pk, b_dws, b_dbs], [r_pk, r_dws, r_dbs], ssem, rsem, _all_reduce_partners(),
                                 wires=[None, w_dws, None])

        @pl.when(jb == (6 * n_blocks) // 8)
        def _():
            pk1_ref[...] = b_pk[...]
            dws_ref[...] = b_dws[...]
            dbs_ref[...] = b_dbs[...]
            loss_ref[...] = jnp.broadcast_to(jnp.sum(b_pk[13:14, :], axis=1, keepdims=True), (8, 128))

        zero = jnp.zeros((CONV_PAD, 128), F32)
        dqpad[0:CONV_PAD, :] = zero
        dqpad[s_len + CONV_PAD:s_len + 2 * CONV_PAD, :] = zero
        dqpad[CONV_PAD:s_len + CONV_PAD, :] = dq_ref[0]
        w = cw_ref[0]

        def bwd(i, carry):
            t0 = i * tt
            dp = jnp.zeros((tt, 128), F32)
            for o in range(1, CONV_WIDTH + 1):
                dp = dp + w[CONV_WIDTH - o:CONV_WIDTH - o + 1, :] * dqpad[pl.ds(t0 + o, tt), :]
            dz_ref[0, 0, pl.ds(t0, tt), :] = (dp * sig_ref[0, pl.ds(t0, tt), :]).astype(BF16)
            dz_ref[0, 1, pl.ds(t0, tt), :] = (dp * t2_ref[0, pl.ds(t0, tt), :]).astype(BF16)
            return carry

        n_iter = s_len // tt
        n_groups = min(8, n_iter)
        rows = tk // n_groups

        def conv_and_matmul(add):
            for g in range(n_groups):
                for it in range(g * n_iter // n_groups, (g + 1) * n_iter // n_groups):
                    bwd(it, 0)
                k0 = pl.multiple_of((jb % kt) * tk + g * rows, rows)
                add(_dot_tn(ycat_ref[g * rows:(g + 1) * rows, :], dy_ref[pl.ds(k0, rows), :]))
        _rs_step(jb // kt, jb % kt, kt, nj, nr, nc, conv_and_matmul, gout_ref, rs_scr)

    const2 = lambda j, k_ref: (0, 0)
    grid_spec = pltpu.PrefetchScalarGridSpec(
        num_scalar_prefetch=1,
        grid=(n_blocks,),
        in_specs=[*[pl.BlockSpec((1, s_len, 128), lambda j, k_ref: (j, 0, 0)) for _ in range(3)],
                  pl.BlockSpec((1, 32, 128), lambda j, k_ref: (j // 2, 0, j % 2)),
                  pl.BlockSpec((8, D_MODEL), const2),
                  pl.BlockSpec((D_MODEL, 128), const2),
                  pl.BlockSpec((128, 128), const2),
                  pl.BlockSpec((tk, 512), lambda j, k_ref: (j % kt, _rs_block(j // kt, k_ref[0], nj))),
                  pl.BlockSpec((s_len, D_MODEL), const2, pipeline_mode=pl.Buffered(1))],
        out_specs=(pl.BlockSpec((1, 2, s_len, 128), lambda j, k_ref: (j, 0, 0, 0)),)
        + tuple(pl.BlockSpec(s, const2) for s in shp) + (pl.BlockSpec((8, 128), const2),
                                                         pl.BlockSpec(memory_space=pl.ANY)),
        scratch_shapes=[pltpu.VMEM((s_len + 2 * CONV_PAD, 128), F32)]
        + [pltpu.VMEM(s, F32) for s in shp]
        + [pltpu.VMEM((3,) + shp[0], F32), pltpu.VMEM((3,) + shp[1], BF16), pltpu.VMEM((3,) + shp[2], F32),
           pltpu.VMEM(shp[1], BF16)]
        + [pltpu.SemaphoreType.DMA((9,)), pltpu.SemaphoreType.DMA((9,))] + _rs_scratch(nj, nr, nc))
    return pl.pallas_call(
        body, name="conv_bwd",
        grid_spec=grid_spec,
        out_shape=(jax.ShapeDtypeStruct((n_blocks, 2, s_len, 128), BF16),)
        + tuple(jax.ShapeDtypeStruct(s, F32) for s in shp) + (jax.ShapeDtypeStruct((8, 128), F32),
                                                              jax.ShapeDtypeStruct((nj, 2, nr, nc), F32)),
        compiler_params=_params(dimension_semantics=("arbitrary",)),
    )(kidx, dq, sig, t2, cw4, sm1, dws, dbs, ycat, dy)


def _bwd_in(dz4, dzc, w12, x, dx2, mod, norm_g, dq, p):
    s_len = x.shape[0]
    tm = TOKEN_TILE
    n_steps = s_len // tm
    tt = TIME_TILE
    n_cblocks = D_MODEL // 128
    parts = max(1, n_steps // n_cblocks)
    nsub = max(1, n_cblocks // n_steps)
    cw = 128 * nsub
    tiles = (s_len // tt) // parts

    def body(dz_ref, dzc_ref, w_ref, x_ref, dx2_ref, mod_ref, g_ref, dq_ref, p_ref, gx_ref, sm_ref, dcw_ref,
             dh_scr, acc_scr, ppad, wacc):
        i = pl.program_id(0)
        part = i % parts

        @pl.when(i == 0)
        def _():
            acc_scr[...] = jnp.zeros_like(acc_scr)

        @pl.when(part == 0)
        def _():
            zero = jnp.zeros((CONV_PAD, 128), F32)
            for sub in range(nsub):
                ppad[sub, 0:CONV_PAD, :] = zero
                ppad[sub, s_len + CONV_PAD:s_len + 2 * CONV_PAD, :] = zero
                ppad[sub, CONV_PAD:s_len + CONV_PAD, :] = p_ref[sub]
            wacc[...] = jnp.zeros_like(wacc)

        def dw_tile(tile):
            t0 = pl.multiple_of((part * tiles + tile) * tt, tt)
            for sub in range(nsub):
                dqt = dq_ref[sub, pl.ds(t0, tt), :]
                for o in range(1, CONV_WIDTH + 1):
                    wacc[sub, o - 1] += _colsum8(dqt * ppad[sub, pl.ds(t0 + o, tt), :])
                wacc[sub, CONV_WIDTH] += _colsum8(dqt)

        def dz_cols(j, hf):
            if j < 4:
                return dz_ref[j, :, hf * 512:(hf + 1) * 512]
            return jnp.concatenate([dzc_ref[4 * hf + blk, j - 4] for blk in range(4)], axis=1)

        dots = [(j, hf) for j in range(N_GROUPS) for hf in range(2)]
        dh = jnp.zeros((tm, D_MODEL), F32)
        for d, (j, hf) in enumerate(dots):
            dh = dh + _dot_nt(dz_cols(j, hf), w_ref[2 * _natural_group(j) + hf])
            for tile in range(d * tiles // len(dots), (d + 1) * tiles // len(dots)):
                dw_tile(tile)
        dh_scr[...] = dh

        @pl.when(part == parts - 1)
        def _():
            for sub in range(nsub):
                for k in range(32):
                    dcw_ref[k:k + 1, sub * 128:(sub + 1) * 128] = jnp.sum(wacc[sub, k], axis=0, keepdims=True)

        scale1 = 1.0 + mod_ref[:, D_MODEL:2 * D_MODEL]
        g = g_ref[...]

        def rows_fn(rows):
            xt = x_ref[rows, :]
            r = lax.rsqrt(_mean(xt * xt) + EPS)
            xn = xt * r
            dhr = dh_scr[rows, :]
            acc_scr[0] += _colsum8(dhr)
            acc_scr[1] += _colsum8(dhr * (xn * g))
            acc_scr[2] += _colsum8(dhr * scale1 * xn)
            dxn = dhr * (g * scale1)
            gx_ref[rows, :] = dx2_ref[rows, :] + r * (dxn - xn * _mean(dxn * xn))
        _row_loop(tm, rows_fn, unroll=FULL_UNROLL)

        @pl.when(i == n_steps - 1)
        def _():
            for qi in range(8):
                sm_ref[qi:qi + 1, :] = jnp.sum(acc_scr[qi], axis=0, keepdims=True)

    tok = lambda i: (i, 0)
    const2 = lambda i: (0, 0)
    return pl.pallas_call(
        body, name="bwd_in",
        grid=(n_steps,),
        out_shape=(jax.ShapeDtypeStruct((s_len, D_MODEL), F32), jax.ShapeDtypeStruct((8, D_MODEL), F32),
                   jax.ShapeDtypeStruct((32, D_MODEL), F32)),
        in_specs=[pl.BlockSpec((4, tm, D_MODEL), lambda i: (0, i, 0)),
                  pl.BlockSpec((n_cblocks, 2, tm, 128), lambda i: (0, 0, i, 0)),
                  pl.BlockSpec((12, D_MODEL, 512), lambda i: (0, 0, 0), pipeline_mode=pl.Buffered(1)),
                  pl.BlockSpec((tm, D_MODEL), tok),
                  pl.BlockSpec((tm, D_MODEL), tok),
                  pl.BlockSpec((1, 3 * D_MODEL), const2),
                  pl.BlockSpec((1, D_MODEL), const2),
                  pl.BlockSpec((nsub, s_len, 128), lambda i: (i // parts, 0, 0)),
                  pl.BlockSpec((nsub, s_len, 128), lambda i: (i // parts, 0, 0))],
        out_specs=(pl.BlockSpec((tm, D_MODEL), tok), pl.BlockSpec((8, D_MODEL), const2),
                   pl.BlockSpec((32, cw), lambda i: (0, i // parts))),
        scratch_shapes=[pltpu.VMEM((tm, D_MODEL), F32), pltpu.VMEM((8, 8, D_MODEL), F32),
                        pltpu.VMEM((nsub, s_len + 2 * CONV_PAD, 128), F32), pltpu.VMEM((nsub, 32, 8, 128), F32)],
        compiler_params=_params(dimension_semantics=("arbitrary",)),
    )(dz4, dzc, w12, x, dx2, mod, norm_g, dq, p)


def _rs_shard_block(i, nj):
    g = jnp.where(i < 2 * nj, i % 2, jnp.where(i < 3 * nj, 2, 3))
    j = jnp.where(i < 2 * nj, i // 2, jnp.where(i < 3 * nj, i - 2 * nj, i - 3 * nj))
    return g, j


def _rs_block(i, k, nj):
    g, j = _rs_shard_block(i, nj)
    return nj * jnp.bitwise_xor(k, 3 - g) + j


def _rs_scratch(nj, nr, nc):
    nblk = N_CHIPS * nj
    return [pltpu.VMEM((2, 2, nr, nc), F32),
            pltpu.VMEM((nblk, nr, nc), F32),
            pltpu.VMEM((2 * nj, nr, nc), BF16),
            pltpu.VMEM((2 * nj, nr, nc), BF16),
            pltpu.VMEM((nj, nr, nc), BF16),
            pltpu.VMEM((nj, nr, nc), BF16),
            pltpu.VMEM((nj, nr, nc), F32),
            pltpu.SemaphoreType.DMA,
            pltpu.SemaphoreType.DMA((nblk,)), pltpu.SemaphoreType.DMA((nblk,)),
            pltpu.SemaphoreType.DMA((2 * nj,)), pltpu.SemaphoreType.DMA((2 * nj,)),
            pltpu.SemaphoreType.DMA((nj,)), pltpu.SemaphoreType.DMA((nj,)),
            pltpu.SemaphoreType.DMA, pltpu.SemaphoreType.DMA]


def _rs_step(i, kk, kt, nj, nr, nc, partial_fn, out_ref, scr):
    nblk = N_CHIPS * nj
    (acc, recv_a, send1, recv1, send2, recv2, own_ps,
     lsem, psend, precv, s1send, s1recv, s2send, s2recv, fsend, frecv) = scr
    x, y, c = _pos()
    sib = (x, y, 1 - c)
    first = (jnp.bitwise_xor(x, 1 - c), jnp.bitwise_xor(y, c), c)
    second = (jnp.bitwise_xor(x, c), jnp.bitwise_xor(y, 1 - c), c)
    slot = i % 2

    @pl.when(kk == 0)
    def _():
        acc[slot] = jnp.zeros((2, nr, nc), F32)

    def pair_copy(ii, sl):
        return _rcopy(acc.at[sl, 1 - c], recv_a.at[ii], psend.at[ii], precv.at[ii], sib)

    def pair_sum(ii, sl):
        cp = pair_copy(ii, sl)
        cp.wait_recv()
        cp.wait_send()
        return acc[sl, c] + recv_a[ii]

    def stage1_copy(src, dst):
        return _rcopy(send1.at[src], recv1.at[dst], s1send.at[src], s1recv.at[dst], first)

    def stage2_copy(j):
        return _rcopy(send2.at[j], recv2.at[j], s2send.at[j], s2recv.at[j], second)

    def finalize(ii, sl):
        ps = pair_sum(ii, sl)
        g, j = _rs_shard_block(ii, nj)

        @pl.when(g == 0)
        def _():
            send1[j] = ps.astype(BF16)
            stage1_copy(j, nj + j).start()

        @pl.when(g == 1 + c)
        def _():
            send1[nj + j] = ps.astype(BF16)
            stage1_copy(nj + j, j).start()

        @pl.when(g == 2 - c)
        def _():
            stage1_copy(0, nj + j).wait_recv()
            send2[j] = (ps + recv1[nj + j].astype(F32)).astype(BF16)
            stage2_copy(j).start()

        @pl.when(g == 3)
        def _():
            own_ps[j] = ps

    @pl.when(jnp.logical_and(kk == kt - 1, i >= 1))
    def _():
        finalize(i - 1, 1 - slot)

    def add_partial(res):
        acc[slot, 0] += res[0:nr]
        acc[slot, 1] += res[nr:2 * nr]
    partial_fn(add_partial)

    @pl.when(kk == kt - 1)
    def _():
        pair_copy(i, slot).start()

        @pl.when(i == nblk - 1)
        def _():
            own_ps[nj - 1] = pair_sum(nblk - 1, (nblk - 1) % 2)
            for j in range(nj):
                stage1_copy(0, j).wait_recv()
                stage2_copy(j).wait_recv()
                own_ps[j] = (own_ps[j] + recv1[j].astype(F32)) + recv2[j].astype(F32)
            loc = pltpu.make_async_copy(own_ps, out_ref.at[:, c], lsem)
            loc.start()
            swap = _rcopy(own_ps, out_ref.at[:, c], fsend, frecv, sib)
            swap.start()
            loc.wait()
            swap.wait_send()
            _rcopy(own_ps, out_ref.at[:, 1 - c], fsend, frecv, sib).wait_recv()
            for s in range(2 * nj):
                stage1_copy(s, 0).wait_send()
            for j in range(nj):
                stage2_copy(j).wait_send()


def _grad_w_in(kidx, h, dz4, dzc, sm2, dcw):
    s_len = h.shape[0]
    tk = min(K_TILE, s_len)
    kt = s_len // tk
    nj, nr, nc = 3, 512, 512
    nblk = N_CHIPS * nj

    def body(k_ref, a_ref, b_ref, bc_ref, sm2_ref, dcw_in, out_ref, pk2_ref, dcwsh_ref, dcb_ref, *scr):
        del k_ref
        b_pk2, b_dcw, r_pk2, r_dcw, bsend, brecv = scr[-6:]
        i = pl.program_id(0)
        kk = pl.program_id(1)
        x, y, c = _pos()
        k = 2 * x + y
        m = _rs_block(i, k, nj)

        @pl.when(jnp.logical_and(kk == 0, i == 0))
        def _():
            b = 4 * x + 2 * y + c
            b_pk2[0:8, :] = _onehot_rows(sm2_ref[0:1, :], b)
            b_pk2[8:16, :] = _onehot_rows(sm2_ref[1:2, :], b)
            b_pk2[16:17, :] = sm2_ref[2:3, :]
            b_pk2[17:24, :] = jnp.zeros((7, D_MODEL), F32)
            b_dcw[...] = dcw_in[...]

        for e in range(4):
            @pl.when(jnp.logical_and(kk == 0, i == (e * nblk) // 4))
            def _():
                _butterfly_event(e, [b_pk2, b_dcw], [r_pk2, r_dcw], bsend, brecv, _all_reduce_partners())

        @pl.when(jnp.logical_and(kk == 0, i == (3 * nblk) // 4))
        def _():
            pk2_ref[...] = b_pk2[...]
            sel = jnp.zeros((32, 256), F32)
            for kc in range(N_CHIPS):
                sel = jnp.where(k == kc, b_dcw[:, kc * 256:(kc + 1) * 256], sel)
            dcwsh_ref[...] = sel
            dcb_ref[...] = b_dcw[31:32, :]

        def partial(add):
            rows = pl.ds(pl.multiple_of(kk * tk, tk), tk)

            @pl.when(m >= 4)
            def _():
                add(_dot_tn(a_ref[rows, :], b_ref[0]))

            @pl.when(m < 4)
            def _():
                add(_dot_tn(a_ref[rows, :], jnp.concatenate([bc_ref[blk, 0] for blk in range(4)], axis=1)))
        _rs_step(i, kk, kt, nj, nr, nc, partial, out_ref, scr[:-6])

    def dz4_index(i, kk, k_ref):
        m = _rs_block(i, k_ref[0], nj)
        use = m >= 4
        return (jnp.where(use, m // 2 - 2, 0), jnp.where(use, kk, 0), jnp.where(use, m % 2, 0))

    def dzc_index(i, kk, k_ref):
        m = _rs_block(i, k_ref[0], nj)
        use = m < 4
        return (jnp.where(use, m % 2, 0), jnp.where(use, m // 2, 0), jnp.where(use, kk, 0), 0)

    const2 = lambda i, kk, k_ref: (0, 0)
    small_out = [(24, D_MODEL), (32, 256), (1, D_MODEL)]
    small_buf = [(24, D_MODEL), (32, D_MODEL)]
    grid_spec = pltpu.PrefetchScalarGridSpec(
        num_scalar_prefetch=1,
        grid=(nblk, kt),
        in_specs=[pl.BlockSpec((s_len, D_MODEL), const2, pipeline_mode=pl.Buffered(1)),
                  pl.BlockSpec((1, tk, 512), dz4_index),
                  pl.BlockSpec((4, 1, tk, 128), dzc_index),
                  pl.BlockSpec((8, D_MODEL), const2),
                  pl.BlockSpec((32, D_MODEL), const2)],
        out_specs=(pl.BlockSpec(memory_space=pl.ANY),) + tuple(pl.BlockSpec(s, const2) for s in small_out),
        scratch_shapes=_rs_scratch(nj, nr, nc)
        + [pltpu.VMEM(s, F32) for s in small_buf] + [pltpu.VMEM((3,) + s, F32) for s in small_buf]
        + [pltpu.SemaphoreType.DMA((6,)), pltpu.SemaphoreType.DMA((6,))])
    return pl.pallas_call(
        body, name="grad_w_in",
        grid_spec=grid_spec,
        out_shape=(jax.ShapeDtypeStruct((nj, 2, nr, nc), F32),) + tuple(jax.ShapeDtypeStruct(s, F32) for s in small_out),
        compiler_params=_params(dimension_semantics=("arbitrary", "arbitrary")),
    )(kidx, h, dz4, dzc, sm2, dcw)


def _adamw_math(w, g, m, v):
    m = ADAM_B1 * m + (1.0 - ADAM_B1) * g
    v = ADAM_B2 * v + (1.0 - ADAM_B2) * (g * g)
    m_hat = m / (1.0 - ADAM_B1 ** ADAM_STEP)
    v_hat = v / (1.0 - ADAM_B2 ** ADAM_STEP)
    delta = -ADAM_LR * (m_hat / (jnp.sqrt(v_hat) + ADAM_EPS) + ADAM_WD * w)
    return delta, m, v


def _adamw_blocked(w, m, v, g4, name):
    nj, _, nr, nc = g4.shape

    def body(w_ref, m_ref, v_ref, g_ref, go_ref, d_ref, mo_ref, vo_ref):
        g = g_ref[0, 0]
        d, mn, vn = _adamw_math(w_ref[...], g, m_ref[...], v_ref[...])
        go_ref[...] = g
        d_ref[...] = d
        mo_ref[...] = mn
        vo_ref[...] = vn

    blk = pl.BlockSpec((nr, nc), lambda j, hf: (hf, j))
    return pl.pallas_call(
        body, name=name,
        grid=(nj, 2),
        out_shape=tuple(jax.ShapeDtypeStruct(w.shape, F32) for _ in range(4)),
        in_specs=[blk, blk, blk, pl.BlockSpec((1, 1, nr, nc), lambda j, hf: (j, hf, 0, 0))],
        out_specs=(blk, blk, blk, blk),
        compiler_params=_params(dimension_semantics=("arbitrary", "arbitrary")),
    )(w, m, v, g4)


def _adamw_w_ada(cact, pk1, pk2, w, m, v):
    rb = 256

    def body(cact_ref, pk1_ref, pk2_ref, w_ref, m_ref, v_ref, g_ref, d_ref, mo_ref, vo_ref, gb_ref, dmod, sel_scr):
        @pl.when(pl.program_id(0) == 0)
        def _():
            x, y, _ = _pos()
            k = 2 * x + y
            dmod[:, 0:D_MODEL] = pk2_ref[0:8, :]
            dmod[:, D_MODEL:2 * D_MODEL] = pk2_ref[8:16, :]
            dmod[:, 2 * D_MODEL:3 * D_MODEL] = pk1_ref[0:8, :]
            gb_ref[...] = jnp.sum(dmod[...], axis=0, keepdims=True)
            sel = jnp.zeros((8, 768), F32)
            for kk in range(N_CHIPS):
                sel = jnp.where(k == kk, dmod[:, kk * 768:(kk + 1) * 768], sel)
            sel_scr[...] = sel

        g = _dot_tn(cact_ref[...].astype(BF16), sel_scr[...].astype(BF16))
        d, mn, vn = _adamw_math(w_ref[...], g, m_ref[...], v_ref[...])
        g_ref[...] = g
        d_ref[...] = d
        mo_ref[...] = mn
        vo_ref[...] = vn

    blk = pl.BlockSpec((rb, 768), lambda i: (i, 0))
    const2 = lambda i: (0, 0)
    return pl.pallas_call(
        body, name="adamw_w_ada",
        grid=(D_MODEL // rb,),
        out_shape=tuple(jax.ShapeDtypeStruct(w.shape, F32) for _ in range(4)) + (
            jax.ShapeDtypeStruct((1, 3 * D_MODEL), F32),),
        in_specs=[pl.BlockSpec((8, rb), lambda i: (0, i)), pl.BlockSpec((16, D_MODEL), const2),
                  pl.BlockSpec((24, D_MODEL), const2), blk, blk, blk],
        out_specs=(blk, blk, blk, blk, pl.BlockSpec((1, 3 * D_MODEL), const2)),
        scratch_shapes=[pltpu.VMEM((8, 3 * D_MODEL), F32), pltpu.VMEM((8, 768), F32)],
        compiler_params=_params(dimension_semantics=("arbitrary",)),
    )(cact, pk1, pk2, w, m, v)


def _adamw_small(items):
    n = len(items)

    def body(*refs):
        ins, outs = refs[:4 * n], refs[4 * n:]
        for i in range(n):
            w_ref, g_ref, m_ref, v_ref = ins[4 * i:4 * i + 4]
            d, mn, vn = _adamw_math(w_ref[...], g_ref[...], m_ref[...], v_ref[...])
            outs[3 * i][...] = d
            outs[3 * i + 1][...] = mn
            outs[3 * i + 2][...] = vn

    flat = [a for it in items for a in it]
    outs = pl.pallas_call(
        body, name="adamw_small",
        out_shape=tuple(jax.ShapeDtypeStruct(it[0].shape, F32) for it in items for _ in range(3)),
        in_specs=[_vmem()] * (4 * n),
        out_specs=tuple(_vmem() for _ in range(3 * n)),
        compiler_params=_params(),
    )(*flat)
    return [tuple(outs[3 * i:3 * i + 3]) for i in range(n)]


def kernel(x, c, w_ada, b_ada, norm_g, w_in, conv_w, conv_b, conv_ln_g, conv_ln_b, sg_ln_g, sg_ln_b, w_s, b_s, w_out, final_g, loss_target, m_w_ada, m_b_ada, m_norm_g, m_w_in, m_conv_w, m_conv_b, m_conv_ln_g, m_conv_ln_b, m_sg_ln_g, m_sg_ln_b, m_w_s, m_b_s, m_w_out, m_final_g, v_w_ada, v_b_ada, v_norm_g, v_w_in, v_conv_w, v_conv_b, v_conv_ln_g, v_conv_ln_b, v_sg_ln_g, v_sg_ln_b, v_w_s, v_b_s, v_w_out, v_final_g):
    s_len = x.shape[1]
    x2d = x[0]
    tgt = loss_target[0]
    row = lambda a: a.reshape(1, -1)

    kidx = (2 * lax.axis_index("x") + lax.axis_index("y")).astype(jnp.int32).reshape(1)
    cw_sh = jnp.pad(conv_w.reshape(CONV_WIDTH, 256), ((0, 1), (0, 0)))
    h, z6, mod, cact, cw4, w_in12 = _front(kidx, x2d, c, w_ada[0], b_ada, norm_g, w_in[0], cw_sh)
    w12 = w_in12.reshape(12, D_MODEL, 512)
    q, p_cm, sig_cm, t2_cm, w_out4 = _conv_fwd(z6, cw4, conv_b, w_out[0])
    w_out_full = w_out4.reshape(2 * D_MODEL, D_MODEL)
    ln_rows = (conv_ln_g, conv_ln_b, sg_ln_g, sg_ln_b, row(final_g))
    bs_exp = jnp.repeat(b_s[0].T, CHUNK, axis=1)
    dz4, dq, dx2, ycat, dy, dws, sm1, dbs = _middle(x2d, z6, q, tgt, ln_rows, mod, w_s[0], bs_exp, w_out_full)
    dzc, pk1, dws_r, dbs_r, loss_t, g_w_out4 = _conv_bwd(
        kidx, dq, sig_cm, t2_cm, cw4, sm1, dws.reshape(D_MODEL, CHUNK), dbs, ycat, dy)
    grad_x, sm2, dcw = _bwd_in(dz4, dzc, w12, x2d, dx2, mod, norm_g, dq, p_cm)
    g_w_in4, pk2, dcw_sh, dcb = _grad_w_in(kidx, h, dz4, dzc, sm2, dcw)

    g_w_in, d_w_in, nm_w_in, nv_w_in = _adamw_blocked(w_in[0], m_w_in[0], v_w_in[0], g_w_in4, "adamw_w_in")
    g_w_out, d_w_out, nm_w_out, nv_w_out = _adamw_blocked(w_out[0], m_w_out[0], v_w_out[0], g_w_out4, "adamw_w_out")
    g_w_ada, d_w_ada, nm_w_ada, nv_w_ada, g_b_ada = _adamw_w_ada(cact, pk1, pk2, w_ada[0], m_w_ada[0], v_w_ada[0])

    g_norm_g = pk2[16:17]
    g_cln_g, g_cln_b, g_sln_g, g_sln_b, g_final = (pk1[8 + i:9 + i] for i in range(5))
    loss = loss_t[0, 0]
    g_conv_w = dcw_sh[:CONV_WIDTH]
    g_w_s = dws_r
    g_b_s = dbs_r[:, :HEADS].T
    small = [
        (b_ada, g_b_ada, m_b_ada, v_b_ada),
        (norm_g, g_norm_g, m_norm_g, v_norm_g),
        (conv_w.reshape(CONV_WIDTH, 256), g_conv_w, m_conv_w.reshape(CONV_WIDTH, 256), v_conv_w.reshape(CONV_WIDTH, 256)),
        (conv_b, dcb, m_conv_b, v_conv_b),
        (conv_ln_g, g_cln_g, m_conv_ln_g, v_conv_ln_g),
        (conv_ln_b, g_cln_b, m_conv_ln_b, v_conv_ln_b),
        (sg_ln_g, g_sln_g, m_sg_ln_g, v_sg_ln_g),
        (sg_ln_b, g_sln_b, m_sg_ln_b, v_sg_ln_b),
        (w_s.reshape(D_MODEL, CHUNK), g_w_s, m_w_s.reshape(D_MODEL, CHUNK), v_w_s.reshape(D_MODEL, CHUNK)),
        (b_s[0], g_b_s, m_b_s[0], v_b_s[0]),
        (row(final_g), g_final, row(m_final_g), row(v_final_g)),
    ]
    upd = _adamw_small(small)

    shapes = [w_ada.shape, b_ada.shape, norm_g.shape, w_in.shape, conv_w.shape, conv_b.shape, conv_ln_g.shape,
              conv_ln_b.shape, sg_ln_g.shape, sg_ln_b.shape, w_s.shape, b_s.shape, w_out.shape, final_g.shape]
    grads = [g_w_ada, g_b_ada, g_norm_g, g_w_in, g_conv_w, dcb, g_cln_g, g_cln_b, g_sln_g, g_sln_b, g_w_s, g_b_s,
             g_w_out, g_final]
    big = {0: (d_w_ada, nm_w_ada, nv_w_ada), 3: (d_w_in, nm_w_in, nv_w_in), 12: (d_w_out, nm_w_out, nv_w_out)}
    small_pos = [1, 2, 4, 5, 6, 7, 8, 9, 10, 11, 13]
    trip = [None] * 14
    for i, t in big.items():
        trip[i] = t
    for i, t in zip(small_pos, upd):
        trip[i] = t
    fit = lambda arrs: [a.reshape(s) for a, s in zip(arrs, shapes)]
    return (loss, grad_x.reshape(x.shape), *fit(grads), *fit([t[0] for t in trip]), *fit([t[1] for t in trip]),
            *fit([t[2] for t in trip]))
```

```python
import jax
import jax.numpy as jnp
from jax import lax
from jax.experimental import pallas as pl
from jax.experimental.pallas import tpu as pltpu

F32 = jnp.float32
BF16 = jnp.bfloat16
MESH = pl.DeviceIdType.MESH

D_MODEL = 1024
N_CHIPS = 4
HEADS = 8
CHUNK = 128
CONV_WIDTH = 31
CONV_HALF = CONV_WIDTH // 2
CONV_PAD = 16
EPS = 1e-6
ADAM_LR = 0.001
ADAM_B1 = 0.9
ADAM_B2 = 0.999
ADAM_EPS = 1e-08
ADAM_WD = 0.01
ADAM_STEP = 10

V7X_VMEM_BYTES = 64 * 1024 * 1024
VMEM_LIMIT = V7X_VMEM_BYTES - 8 * 1024 * 1024
ROWS = 16
UNROLL = 8
TOKEN_TILE = 256
FULL_UNROLL = TOKEN_TILE // ROWS
TIME_TILE = 128
K_TILE = 2048

N_GROUPS = 6


def _natural_group(j):
    return (j + 2) % N_GROUPS


def _pos():
    return lax.axis_index("x"), lax.axis_index("y"), lax.axis_index("c")


def _rcopy(src, dst, ssem, rsem, dev):
    return pltpu.make_async_remote_copy(src_ref=src, dst_ref=dst, send_sem=ssem, recv_sem=rsem,
                                        device_id=dev, device_id_type=MESH)


def _vmem():
    return pl.BlockSpec(memory_space=pltpu.VMEM)


def _params(**kw):
    return pltpu.CompilerParams(vmem_limit_bytes=VMEM_LIMIT, **kw)


def _sigmoid(v):
    return 0.5 * jnp.tanh(0.5 * v) + 0.5


def _row_loop(n_rows, body, unroll=1):
    def step(r, carry):
        body(pl.ds(pl.multiple_of(r * ROWS, ROWS), ROWS))
        return carry
    lax.fori_loop(0, n_rows // ROWS, step, 0, unroll=unroll)


def _colsum8(v):
    return v.reshape(v.shape[0] // 8, 8, v.shape[1]).sum(axis=0)


def _mean(v):
    return jnp.mean(v, axis=-1, keepdims=True)


def _dot_nn(a, b):
    return jnp.dot(a, b, preferred_element_type=F32)


def _dot_nt(a, b):
    return lax.dot_general(a, b, (((1,), (1,)), ((), ())), preferred_element_type=F32)


def _dot_tn(a, b):
    return lax.dot_general(a, b, (((0,), (0,)), ((), ())), preferred_element_type=F32)


def _remote_chip(k, r):
    return jnp.bitwise_xor(k, r + 1)


def _front(kidx, x, c, w_ada, b_ada, norm_g, w_in, cw):
    s_len = x.shape[0]
    tmh = min(512, s_len)
    tmz = min(2048, s_len)
    nh = s_len // tmh
    nz = s_len // tmz
    n_steps = nh + 12 * nz

    def remote_block(q):
        return jnp.where(q < 6, q % 2, 2), jnp.where(q < 6, q // 2, q - 6)

    def block_of(i, k):
        r, j = remote_block(jnp.maximum(i - 3, 0))
        return jnp.where(i < 3, 3 * k + i, 3 * _remote_chip(k, r) + j)

    def body(k_ref, x_ref, c_ref, bada_ref, g_ref, cw_ref, wada_hbm, win_hbm,
             h_ref, z_ref, mod_ref, cact_ref, cw4_ref, w12_hbm,
             h_all, wbuf, stage, wada_v, cslab, mslab, cw4_s,
             lsem, csend, crecv, msend, mrecv, wsend, wrecv, isend, irecv, fsend, frecv, osem):
        del k_ref
        t = pl.program_id(0)
        x_, y_, c_ = _pos()
        k = 2 * x_ + y_
        b = 4 * x_ + 2 * y_ + c_
        sib = (x_, y_, 1 - c_)

        def dev_of(r):
            kk = _remote_chip(k, r)
            return (kk // 2, kk % 2, c_)

        def ici(q, kk):
            r, j = remote_block(q)
            return _rcopy(wbuf.at[3 * kk + j, c_], wbuf.at[3 * kk + j, c_], isend.at[q], irecv.at[q], dev_of(r))

        def fwd(q, hf):
            r, j = remote_block(q)
            blk = 3 * _remote_chip(k, r) + j
            return _rcopy(wbuf.at[blk, hf], wbuf.at[blk, hf], fsend.at[q], frecv.at[q], sib)

        def c_copy(q, src):
            d = jnp.bitwise_xor(b, q)
            return _rcopy(cslab.at[src], cslab.at[src], csend.at[q - 1], crecv.at[q - 1], (d // 4, (d // 2) % 2, d % 2))

        def m_copy(r, kk):
            return _rcopy(mslab.at[kk], mslab.at[kk], msend.at[r], mrecv.at[r], dev_of(r))

        def cw_copy(r, kk):
            return _rcopy(cw4_s.at[kk], cw4_s.at[kk], wsend.at[r], wrecv.at[r], dev_of(r))

        def to_hbm(i):
            m = block_of(i, k)
            return pltpu.make_async_copy(wbuf.at[m], w12_hbm.at[m], osem.at[i])

        @pl.when(t == 0)
        def _():
            ld_w = pltpu.make_async_copy(win_hbm, stage, lsem.at[0])
            ld_w.start()
            ld_a = pltpu.make_async_copy(wada_hbm, wada_v, lsem.at[1])
            ld_a.start()
            cslab[b] = jnp.broadcast_to(c_ref[...], (8, D_MODEL))
            for q in range(1, 8):
                c_copy(q, b).start()
            cw4_s[k] = cw_ref[...]
            for r in range(3):
                cw_copy(r, k).start()
            ld_w.wait()
            for j in range(3):
                for hf in range(2):
                    wbuf[3 * k + j, hf] = stage[hf * 512:(hf + 1) * 512, j * 512:(j + 1) * 512].astype(BF16)
            for i in range(3):
                to_hbm(i).start()
            for q in range(1, 8):
                c_copy(q, jnp.bitwise_xor(b, q)).wait_recv()
            row = lax.broadcasted_iota(jnp.int32, (8, D_MODEL), 0)
            call = jnp.zeros((8, D_MODEL), F32)
            for d in range(8):
                call = jnp.where(row == d, cslab[d], call)
            cact = call * _sigmoid(call)
            cact_ref[...] = cact
            ld_a.wait()
            mslab[k] = _dot_nn(cact.astype(BF16), wada_v[...].astype(BF16))
            for r in range(3):
                m_copy(r, k).start()
            for q in range(9):
                ici(q, k).start()
            for r in range(3):
                m_copy(r, _remote_chip(k, r)).wait_recv()
            row8 = lax.broadcasted_iota(jnp.int32, (8, 768), 0)
            for kk in range(N_CHIPS):
                piece = jnp.sum(jnp.where(row8 == b, mslab[kk], 0.0), axis=0, keepdims=True)
                mod_ref[:, kk * 768:(kk + 1) * 768] = piece + bada_ref[:, kk * 768:(kk + 1) * 768]
            for r in range(3):
                cw_copy(r, _remote_chip(k, r)).wait_recv()
            cw4_ref[...] = cw4_s[...]

        @pl.when(t < nh)
        def _():
            shift = mod_ref[:, 0:D_MODEL]
            scale1 = 1.0 + mod_ref[:, D_MODEL:2 * D_MODEL]
            g = g_ref[...]
            base = t * tmh

            def rows_fn(rows):
                xt = x_ref[rows, :]
                r = lax.rsqrt(_mean(xt * xt) + EPS)
                hv = ((xt * r * g) * scale1 + shift).astype(BF16)
                h_ref[rows, :] = hv
                h_all[pl.ds(pl.multiple_of(base + rows.start, ROWS), ROWS), :] = hv
            _row_loop(tmh, rows_fn, unroll=UNROLL)

        @pl.when(t >= nh)
        def _():
            u = t - nh
            i = u // nz
            rt = u % nz
            @pl.when(jnp.logical_and(rt == 0, jnp.logical_and(i >= 2, i <= 10)))
            def _():
                q = i - 2
                r, _ = remote_block(q)
                ici(q, _remote_chip(k, r)).wait_recv()
                fwd(q, c_).start()

            @pl.when(jnp.logical_and(rt == 0, i >= 3))
            def _():
                fwd(i - 3, 1 - c_).wait_recv()
                to_hbm(i).start()
            m = block_of(i, k)
            hb = h_all[pl.ds(pl.multiple_of(rt * tmz, tmz), tmz), :]
            z_ref[0] = _dot_nn(hb, wbuf[m].reshape(D_MODEL, 512))

        @pl.when(t == n_steps - 1)
        def _():
            for q in range(1, 8):
                c_copy(q, b).wait_send()
            for r in range(3):
                m_copy(r, k).wait_send()
                cw_copy(r, k).wait_send()
            for q in range(9):
                ici(q, k).wait_send()
                fwd(q, c_).wait_send()
            for i in range(12):
                to_hbm(i).wait()

    def z_index(t, k_ref):
        u = jnp.maximum(t - nh, 0)
        m = block_of(u // nz, k_ref[0])
        return ((m // 2 + 4) % N_GROUPS, u % nz, m % 2)

    tok = lambda t, k_ref: (jnp.minimum(t, nh - 1), 0)
    const2 = lambda t, k_ref: (0, 0)
    grid_spec = pltpu.PrefetchScalarGridSpec(
        num_scalar_prefetch=1,
        grid=(n_steps,),
        in_specs=[pl.BlockSpec((tmh, D_MODEL), tok),
                  pl.BlockSpec((1, D_MODEL), const2),
                  pl.BlockSpec((1, 3 * D_MODEL), const2),
                  pl.BlockSpec((1, D_MODEL), const2),
                  pl.BlockSpec((32, 256), const2),
                  pl.BlockSpec(memory_space=pl.ANY),
                  pl.BlockSpec(memory_space=pl.ANY)],
        out_specs=(pl.BlockSpec((tmh, D_MODEL), tok),
                   pl.BlockSpec((1, tmz, 512), z_index),
                   pl.BlockSpec((1, 3 * D_MODEL), const2),
                   pl.BlockSpec((8, D_MODEL), const2),
                   pl.BlockSpec((N_CHIPS, 32, 256), lambda t, k_ref: (0, 0, 0)),
                   pl.BlockSpec(memory_space=pl.ANY)),
        scratch_shapes=[pltpu.VMEM((s_len, D_MODEL), BF16),
                        pltpu.VMEM((12, 2, 512, 512), BF16),
                        pltpu.VMEM((D_MODEL, 1536), F32),
                        pltpu.VMEM((D_MODEL, 768), F32),
                        pltpu.VMEM((8, 8, D_MODEL), F32),
                        pltpu.VMEM((N_CHIPS, 8, 768), F32),
                        pltpu.VMEM((N_CHIPS, 32, 256), F32),
                        pltpu.SemaphoreType.DMA((2,)),
                        pltpu.SemaphoreType.DMA((7,)), pltpu.SemaphoreType.DMA((7,)),
                        pltpu.SemaphoreType.DMA((3,)), pltpu.SemaphoreType.DMA((3,)),
                        pltpu.SemaphoreType.DMA((3,)), pltpu.SemaphoreType.DMA((3,)),
                        pltpu.SemaphoreType.DMA((9,)), pltpu.SemaphoreType.DMA((9,)),
                        pltpu.SemaphoreType.DMA((9,)), pltpu.SemaphoreType.DMA((9,)),
                        pltpu.SemaphoreType.DMA((12,))])
    return pl.pallas_call(
        body, name="front",
        grid_spec=grid_spec,
        out_shape=(jax.ShapeDtypeStruct((s_len, D_MODEL), BF16),
                   jax.ShapeDtypeStruct((N_GROUPS, s_len, D_MODEL), F32),
                   jax.ShapeDtypeStruct((1, 3 * D_MODEL), F32),
                   jax.ShapeDtypeStruct((8, D_MODEL), F32),
                   jax.ShapeDtypeStruct((N_CHIPS, 32, 256), F32),
                   jax.ShapeDtypeStruct((12, 2, 512, 512), BF16)),
        compiler_params=_params(dimension_semantics=("arbitrary",)),
    )(kidx, x, c, b_ada, norm_g, cw, w_ada, w_in)


def _all_reduce_partners():
    x, y, c = _pos()
    return [(x, y, 1 - c), (x, 1 - y, c), (1 - x, y, c)]


def _butterfly_event(e, bufs, recvs, ssem, rsem, partners, wires=None):
    n = len(bufs)
    wires = wires or [None] * n

    def copies(s):
        return [_rcopy(buf if wire is None else wire, recv.at[s], ssem.at[s * n + i], rsem.at[s * n + i], partners[s])
                for i, (buf, recv, wire) in enumerate(zip(bufs, recvs, wires))]

    if e > 0:
        for cp in copies(e - 1):
            cp.wait()
        for buf, recv, wire in zip(bufs, recvs, wires):
            mine = buf[...] if wire is None else wire[...].astype(F32)
            buf[...] = mine + recv[e - 1].astype(F32)
    if e < len(partners):
        for buf, wire in zip(bufs, wires):
            if wire is not None:
                wire[...] = buf[...].astype(BF16)
        for cp in copies(e):
            cp.start()


def _onehot_rows(v, b):
    row = lax.broadcasted_iota(jnp.int32, (8, v.shape[1]), 0)
    return jnp.where(row == b, jnp.broadcast_to(v, (8, v.shape[1])), 0.0)


def _conv_fwd(z6, cw4, conv_b, w_out):
    s_len = z6.shape[1]
    tt = TIME_TILE
    n_blocks = D_MODEL // 128

    def body(z_ref, cw_ref, cb_ref, wout_hbm, q_ref, p_ref, sig_ref, t2_ref, wout4_hbm,
             ppad, stage, wbuf, lsem, isend, irecv, fsend, frecv, osem):
        jb = pl.program_id(0)
        x_, y_, c_ = _pos()
        k = 2 * x_ + y_
        sib = (x_, y_, 1 - c_)

        def ici(r, kk):
            rk = _remote_chip(k, r)
            return _rcopy(wbuf.at[kk, c_], wbuf.at[kk, c_], isend.at[r], irecv.at[r], (rk // 2, rk % 2, c_))

        def fwd(r, hf):
            kk = _remote_chip(k, r)
            return _rcopy(wbuf.at[kk, hf], wbuf.at[kk, hf], fsend.at[r], frecv.at[r], sib)

        @pl.when(jb == 0)
        def _():
            ld = pltpu.make_async_copy(wout_hbm, stage, lsem)
            ld.start()
            ld.wait()
            for hf in range(2):
                wbuf[k, hf] = stage[hf * 256:(hf + 1) * 256, :].astype(BF16)
            for r in range(3):
                ici(r, k).start()

        @pl.when(jb == (5 * n_blocks) // 8)
        def _():
            for r in range(3):
                ici(r, _remote_chip(k, r)).wait_recv()
                fwd(r, c_).start()

        zero = jnp.zeros((CONV_PAD, 128), F32)
        ppad[0:CONV_PAD, :] = zero
        ppad[s_len + CONV_PAD:s_len + 2 * CONV_PAD, :] = zero

        def fill(i, carry):
            t0 = pl.multiple_of(i * tt, tt)
            sig = _sigmoid(z_ref[1, pl.ds(t0, tt), :])
            p = z_ref[0, pl.ds(t0, tt), :] * sig
            ppad[pl.ds(CONV_PAD + t0, tt), :] = p
            p_ref[0, pl.ds(t0, tt), :] = p
            sig_ref[0, pl.ds(t0, tt), :] = sig
            t2_ref[0, pl.ds(t0, tt), :] = p * (1.0 - sig)
            return carry
        lax.fori_loop(0, s_len // tt, fill, 0)
        w = cw_ref[0]
        bias = cb_ref[...]

        def conv(i, carry):
            t0 = pl.multiple_of(i * tt, tt)
            acc = jnp.broadcast_to(bias, (tt, 128))
            for o in range(1, CONV_WIDTH + 1):
                acc = acc + w[o - 1:o, :] * ppad[pl.ds(t0 + o, tt), :]
            q_ref[0, pl.ds(t0, tt), :] = acc
            return carry
        lax.fori_loop(0, s_len // tt, conv, 0)

        @pl.when(jb == n_blocks - 1)
        def _():
            for r in range(3):
                fwd(r, 1 - c_).wait_recv()
            out = pltpu.make_async_copy(wbuf, wout4_hbm, osem)
            out.start()
            for r in range(3):
                ici(r, k).wait_send()
                fwd(r, c_).wait_send()
            out.wait()

    return pl.pallas_call(
        body, name="conv_fwd",
        grid=(n_blocks,),
        out_shape=tuple(jax.ShapeDtypeStruct((n_blocks, s_len, 128), F32) for _ in range(4))
        + (jax.ShapeDtypeStruct((N_CHIPS, 2, 256, D_MODEL), BF16),),
        in_specs=[pl.BlockSpec((2, s_len, 128), lambda j: (2, 0, j)),
                  pl.BlockSpec((1, 32, 128), lambda j: (j // 2, 0, j % 2)),
                  pl.BlockSpec((1, 128), lambda j: (0, j)),
                  pl.BlockSpec(memory_space=pl.ANY)],
        out_specs=tuple(pl.BlockSpec((1, s_len, 128), lambda j: (j, 0, 0)) for _ in range(4))
        + (pl.BlockSpec(memory_space=pl.ANY),),
        scratch_shapes=[pltpu.VMEM((s_len + 2 * CONV_PAD, 128), F32),
                        pltpu.VMEM((512, D_MODEL), F32),
                        pltpu.VMEM((N_CHIPS, 2, 256, D_MODEL), BF16),
                        pltpu.SemaphoreType.DMA,
                        pltpu.SemaphoreType.DMA((3,)), pltpu.SemaphoreType.DMA((3,)),
                        pltpu.SemaphoreType.DMA((3,)), pltpu.SemaphoreType.DMA((3,)),
                        pltpu.SemaphoreType.DMA],
        compiler_params=_params(dimension_semantics=("arbitrary",)),
    )(z6, cw4, conv_b, w_out)


def _middle(x, z6, q, target, ln_rows, mod, w_s, bs_exp, w_out):
    s_len = x.shape[0]
    tm = TOKEN_TILE
    n_steps = s_len // tm
    n_chunks = tm // CHUNK
    inv_d = 1.0 / D_MODEL

    def body(x_ref, z_ref, q_ref, tgt_ref, cg_ref, cb_ref, sg_ref, sb_ref, fg_ref, mod_ref, ws_ref, bs_ref, wout_ref,
             dz_ref, dq_ref, dx2_ref, ycat_ref, dy_ref, dws_ref, sm_ref, dbs_ref,
             vl_scr, vm_scr, y_scr, dycat_scr, dvm_scr, dvl_scr, acc_scr, dbs_acc, keep, rstd_scr):
        i = pl.program_id(0)

        @pl.when(i == 0)
        def _():
            acc_scr[...] = jnp.zeros_like(acc_scr)
            dbs_acc[...] = jnp.zeros_like(dbs_acc)
            dws_ref[...] = jnp.zeros_like(dws_ref)

        cg, cb, sg, sb, fg = cg_ref[...], cb_ref[...], sg_ref[...], sb_ref[...], fg_ref[...]
        gm = mod_ref[:, 2 * D_MODEL:3 * D_MODEL]

        def norm_stats(t):
            c = t - _mean(t)
            rstd = lax.rsqrt(_mean(c * c) + EPS)
            return c * rstd, rstd

        def phase1(rows):
            qhat, rstd_q = norm_stats(jnp.concatenate([q_ref[blk, rows, :] for blk in range(D_MODEL // 128)], axis=1))
            ln = qhat * cg + cb
            gz = z_ref[0, rows, :]
            sig_ln = _sigmoid(ln)
            sig_g = _sigmoid(gz)
            ycat_ref[rows, 0:D_MODEL] = ((ln * sig_ln) * (gz * sig_g)).astype(BF16)
            vhat, rstd_v = norm_stats(z_ref[2, rows, :])
            vl_scr[rows, :] = (vhat * sg + sb).astype(BF16)
            keep[0, rows, :] = qhat
            keep[1, rows, :] = vhat
            keep[2, rows, :] = sig_ln
            keep[3, rows, :] = sig_g
            rstd_scr[0, rows, :] = rstd_q
            rstd_scr[1, rows, :] = rstd_v
        _row_loop(tm, phase1, unroll=FULL_UNROLL)

        for ch in range(n_chunks):
            r0 = ch * CHUNK
            for h in range(HEADS):
                c0 = h * CHUNK
                vm_scr[r0:r0 + CHUNK, c0:c0 + CHUNK] = (
                    _dot_nn(ws_ref[h].astype(BF16), vl_scr[r0:r0 + CHUNK, c0:c0 + CHUNK]) + bs_ref[:, c0:c0 + CHUNK])

        def phase3(rows):
            bg = z_ref[3, rows, :]
            sig_b = _sigmoid(bg)
            keep[4, rows, :] = sig_b
            ycat_ref[rows, D_MODEL:2 * D_MODEL] = (z_ref[1, rows, :] * vm_scr[rows, :] * (bg * sig_b)).astype(BF16)
        _row_loop(tm, phase3, unroll=FULL_UNROLL)

        y_scr[...] = _dot_nn(ycat_ref[...], wout_ref[...])

        def phase5(rows):
            y = y_scr[rows, :]
            x2 = x_ref[rows, :] + gm * y
            r2 = lax.rsqrt(_mean(x2 * x2) + EPS)
            xn2 = x2 * r2
            diff = xn2 * fg - tgt_ref[rows, :]
            acc_scr[6] += _colsum8(diff * diff)
            dout = diff * inv_d
            acc_scr[0] += _colsum8(dout * xn2)
            dxn = dout * fg
            dx2 = r2 * (dxn - xn2 * _mean(dxn * xn2))
            dx2_ref[rows, :] = dx2
            acc_scr[1] += _colsum8(dx2 * y)
            dy_ref[rows, :] = (dx2 * gm).astype(BF16)
        _row_loop(tm, phase5, unroll=FULL_UNROLL)

        dycat_scr[...] = _dot_nt(dy_ref[...], wout_ref[...])

        def phase7(rows):
            dyb = dycat_scr[rows, D_MODEL:2 * D_MODEL]
            u = z_ref[1, rows, :]
            bg = z_ref[3, rows, :]
            vm = vm_scr[rows, :]
            sig = keep[4, rows, :]
            silu = bg * sig
            dz_ref[1, rows, :] = (dyb * vm * silu).astype(BF16)
            dvm = dyb * u * silu
            dz_ref[3, rows, :] = (dyb * u * vm * (sig * (1.0 + bg * (1.0 - sig)))).astype(BF16)
            dvm_scr[rows, :] = dvm.astype(BF16)
            pos = pl.ds(pl.multiple_of(rows.start % CHUNK, ROWS), ROWS)
            dbs_acc[pos, :] += dvm
        _row_loop(tm, phase7, unroll=FULL_UNROLL)

        for ch in range(n_chunks):
            r0 = ch * CHUNK
            for h in range(HEADS):
                c0 = h * CHUNK
                dvm_b = dvm_scr[r0:r0 + CHUNK, c0:c0 + CHUNK]
                dws_ref[h] += _dot_nt(dvm_b, vl_scr[r0:r0 + CHUNK, c0:c0 + CHUNK])
                dvl_scr[r0:r0 + CHUNK, c0:c0 + CHUNK] = _dot_tn(ws_ref[h].astype(BF16), dvm_b)

        def phase9(rows):
            vhat, rstd_v = keep[1, rows, :], rstd_scr[1, rows, :]
            dvl = dvl_scr[rows, :]
            acc_scr[4] += _colsum8(dvl * vhat)
            acc_scr[5] += _colsum8(dvl)
            dvh = dvl * sg
            dz_ref[2, rows, :] = (rstd_v * (dvh - _mean(dvh) - vhat * _mean(dvh * vhat))).astype(BF16)
            qhat, rstd_q = keep[0, rows, :], rstd_scr[0, rows, :]
            ln = qhat * cg + cb
            sig_ln = keep[2, rows, :]
            gz = z_ref[0, rows, :]
            sig_g = keep[3, rows, :]
            dya = dycat_scr[rows, 0:D_MODEL]
            dz_ref[0, rows, :] = (dya * (ln * sig_ln) * (sig_g * (1.0 + gz * (1.0 - sig_g)))).astype(BF16)
            dln = (dya * (gz * sig_g)) * (sig_ln * (1.0 + ln * (1.0 - sig_ln)))
            acc_scr[2] += _colsum8(dln * qhat)
            acc_scr[3] += _colsum8(dln)
            dqh = dln * cg
            dq = rstd_q * (dqh - _mean(dqh) - qhat * _mean(dqh * qhat))
            for blk in range(D_MODEL // 128):
                dq_ref[blk, rows, :] = dq[:, blk * 128:(blk + 1) * 128]
        _row_loop(tm, phase9, unroll=FULL_UNROLL)

        @pl.when(i == n_steps - 1)
        def _():
            for qi in range(8):
                scale = 0.5 * inv_d if qi == 6 else 1.0
                sm_ref[qi:qi + 1, :] = jnp.sum(acc_scr[qi], axis=0, keepdims=True) * scale
            lane = lax.broadcasted_iota(jnp.int32, (CHUNK, CHUNK), 1)
            tile = jnp.zeros((CHUNK, CHUNK), F32)
            for h in range(HEADS):
                col = jnp.sum(dbs_acc[:, h * CHUNK:(h + 1) * CHUNK], axis=1, keepdims=True)
                tile = jnp.where(lane == h, col, tile)
            dbs_ref[...] = tile

    tok = lambda i: (i, 0)
    const2 = lambda i: (0, 0)
    return pl.pallas_call(
        body, name="middle",
        grid=(n_steps,),
        out_shape=(jax.ShapeDtypeStruct((4, s_len, D_MODEL), BF16),
                   jax.ShapeDtypeStruct((D_MODEL // 128, s_len, 128), F32),
                   jax.ShapeDtypeStruct((s_len, D_MODEL), F32),
                   jax.ShapeDtypeStruct((s_len, 2 * D_MODEL), BF16),
                   jax.ShapeDtypeStruct((s_len, D_MODEL), BF16),
                   jax.ShapeDtypeStruct((HEADS, CHUNK, CHUNK), F32),
                   jax.ShapeDtypeStruct((8, D_MODEL), F32),
                   jax.ShapeDtypeStruct((CHUNK, CHUNK), F32)),
        in_specs=[pl.BlockSpec((tm, D_MODEL), tok),
                  pl.BlockSpec((4, tm, D_MODEL), lambda i: (0, i, 0)),
                  pl.BlockSpec((D_MODEL // 128, tm, 128), lambda i: (0, i, 0)),
                  pl.BlockSpec((tm, D_MODEL), tok),
                  *[pl.BlockSpec((1, D_MODEL), const2) for _ in range(5)],
                  pl.BlockSpec((1, 3 * D_MODEL), const2),
                  pl.BlockSpec((HEADS, CHUNK, CHUNK), lambda i: (0, 0, 0)),
                  pl.BlockSpec((CHUNK, D_MODEL), const2),
                  pl.BlockSpec((2 * D_MODEL, D_MODEL), const2, pipeline_mode=pl.Buffered(1))],
        out_specs=(pl.BlockSpec((4, tm, D_MODEL), lambda i: (0, i, 0)),
                   pl.BlockSpec((D_MODEL // 128, tm, 128), lambda i: (0, i, 0)),
                   pl.BlockSpec((tm, D_MODEL), tok),
                   pl.BlockSpec((tm, 2 * D_MODEL), tok),
                   pl.BlockSpec((tm, D_MODEL), tok),
                   pl.BlockSpec((HEADS, CHUNK, CHUNK), lambda i: (0, 0, 0)),
                   pl.BlockSpec((8, D_MODEL), const2),
                   pl.BlockSpec((CHUNK, CHUNK), const2)),
        scratch_shapes=[pltpu.VMEM((tm, D_MODEL), BF16),
                        pltpu.VMEM((tm, D_MODEL), F32),
                        pltpu.VMEM((tm, D_MODEL), F32),
                        pltpu.VMEM((tm, 2 * D_MODEL), F32),
                        pltpu.VMEM((tm, D_MODEL), BF16),
                        pltpu.VMEM((tm, D_MODEL), F32),
                        pltpu.VMEM((8, 8, D_MODEL), F32),
                        pltpu.VMEM((CHUNK, D_MODEL), F32),
                        pltpu.VMEM((5, tm, D_MODEL), F32),
                        pltpu.VMEM((2, tm, 1), F32)],
        compiler_params=_params(dimension_semantics=("arbitrary",)),
    )(x, z6, q, target, *ln_rows, mod, w_s, bs_exp, w_out)


def _conv_bwd(kidx, dq, sig, t2, cw4, sm1, dws, dbs, ycat, dy):
    s_len = dq.shape[1]
    tt = TIME_TILE
    n_blocks = D_MODEL // 128
    shp = [(16, D_MODEL), (D_MODEL, 128), (128, 128)]
    nj, nr, nc = 1, 256, D_MODEL
    kt = n_blocks // N_CHIPS
    tk = s_len // kt

    def body(k_ref, dq_ref, sig_ref, t2_ref, cw_ref, sm1_ref, dws_in, dbs_in, ycat_ref, dy_ref,
             dz_ref, pk1_ref, dws_ref, dbs_ref, loss_ref, gout_ref,
             dqpad, b_pk, b_dws, b_dbs, r_pk, r_dws, r_dbs, w_dws, ssem, rsem, *rs_scr):
        del k_ref
        jb = pl.program_id(0)
        x_, y_, c_ = _pos()

        @pl.when(jb == 0)
        def _():
            b_pk[0:8, :] = _onehot_rows(sm1_ref[1:2, :], 4 * x_ + 2 * y_ + c_)
            for r, src in enumerate([2, 3, 4, 5, 0, 6]):
                b_pk[8 + r:9 + r, :] = sm1_ref[src:src + 1, :]
            b_pk[14:16, :] = jnp.zeros((2, D_MODEL), F32)
            b_dws[...] = dws_in[...]
            b_dbs[...] = dbs_in[...]

        for e, step in enumerate([0, 0, (3 * n_blocks) // 8, (6 * n_blocks) // 8]):
            @pl.when(jb == step)
            def _():
                _butterfly_event(e, [b_pk, b_dws, b_dbs], [r_pk, r_dws, r_dbs], ssem, rsem, _all_reduce_partners(),
                                 wires=[None, w_dws, None])

        @pl.when(jb == (6 * n_blocks) // 8)
        def _():
            pk1_ref[...] = b_pk[...]
            dws_ref[...] = b_dws[...]
            dbs_ref[...] = b_dbs[...]
            loss_ref[...] = jnp.broadcast_to(jnp.sum(b_pk[13:14, :], axis=1, keepdims=True), (8, 128))

        zero = jnp.zeros((CONV_PAD, 128), F32)
        dqpad[0:CONV_PAD, :] = zero
        dqpad[s_len + CONV_PAD:s_len + 2 * CONV_PAD, :] = zero
        dqpad[CONV_PAD:s_len + CONV_PAD, :] = dq_ref[0]
        w = cw_ref[0]

        def bwd(i, carry):
            t0 = i * tt
            dp = jnp.zeros((tt, 128), F32)
            for o in range(1, CONV_WIDTH + 1):
                dp = dp + w[CONV_WIDTH - o:CONV_WIDTH - o + 1, :] * dqpad[pl.ds(t0 + o, tt), :]
            dz_ref[0, 0, pl.ds(t0, tt), :] = (dp * sig_ref[0, pl.ds(t0, tt), :]).astype(BF16)
            dz_ref[0, 1, pl.ds(t0, tt), :] = (dp * t2_ref[0, pl.ds(t0, tt), :]).astype(BF16)
            return carry

        n_iter = s_len // tt
        n_groups = min(8, n_iter)
        rows = tk // n_groups

        def conv_and_matmul(add):
            for g in range(n_groups):
                for it in range(g * n_iter // n_groups, (g + 1) * n_iter // n_groups):
                    bwd(it, 0)
                k0 = pl.multiple_of((jb % kt) * tk + g * rows, rows)
                add(_dot_tn(ycat_ref[g * rows:(g + 1) * rows, :], dy_ref[pl.ds(k0, rows), :]))
        _rs_step(jb // kt, jb % kt, kt, nj, nr, nc, conv_and_matmul, gout_ref, rs_scr)

    const2 = lambda j, k_ref: (0, 0)
    grid_spec = pltpu.PrefetchScalarGridSpec(
        num_scalar_prefetch=1,
        grid=(n_blocks,),
        in_specs=[*[pl.BlockSpec((1, s_len, 128), lambda j, k_ref: (j, 0, 0)) for _ in range(3)],
                  pl.BlockSpec((1, 32, 128), lambda j, k_ref: (j // 2, 0, j % 2)),
                  pl.BlockSpec((8, D_MODEL), const2),
                  pl.BlockSpec((D_MODEL, 128), const2),
                  pl.BlockSpec((128, 128), const2),
                  pl.BlockSpec((tk, 512), lambda j, k_ref: (j % kt, _rs_block(j // kt, k_ref[0], nj))),
                  pl.BlockSpec((s_len, D_MODEL), const2, pipeline_mode=pl.Buffered(1))],
        out_specs=(pl.BlockSpec((1, 2, s_len, 128), lambda j, k_ref: (j, 0, 0, 0)),)
        + tuple(pl.BlockSpec(s, const2) for s in shp) + (pl.BlockSpec((8, 128), const2),
                                                         pl.BlockSpec(memory_space=pl.ANY)),
        scratch_shapes=[pltpu.VMEM((s_len + 2 * CONV_PAD, 128), F32)]
        + [pltpu.VMEM(s, F32) for s in shp]
        + [pltpu.VMEM((3,) + shp[0], F32), pltpu.VMEM((3,) + shp[1], BF16), pltpu.VMEM((3,) + shp[2], F32),
           pltpu.VMEM(shp[1], BF16)]
        + [pltpu.SemaphoreType.DMA((9,)), pltpu.SemaphoreType.DMA((9,))] + _rs_scratch(nj, nr, nc))
    return pl.pallas_call(
        body, name="conv_bwd",
        grid_spec=grid_spec,
        out_shape=(jax.ShapeDtypeStruct((n_blocks, 2, s_len, 128), BF16),)
        + tuple(jax.ShapeDtypeStruct(s, F32) for s in shp) + (jax.ShapeDtypeStruct((8, 128), F32),
                                                              jax.ShapeDtypeStruct((nj, 2, nr, nc), F32)),
        compiler_params=_params(dimension_semantics=("arbitrary",)),
    )(kidx, dq, sig, t2, cw4, sm1, dws, dbs, ycat, dy)


def _bwd_in(dz4, dzc, w12, x, dx2, mod, norm_g, dq, p):
    s_len = x.shape[0]
    tm = TOKEN_TILE
    n_steps = s_len // tm
    tt = TIME_TILE
    n_cblocks = D_MODEL // 128
    parts = max(1, n_steps // n_cblocks)
    nsub = max(1, n_cblocks // n_steps)
    cw = 128 * nsub
    tiles = (s_len // tt) // parts

    def body(dz_ref, dzc_ref, w_ref, x_ref, dx2_ref, mod_ref, g_ref, dq_ref, p_ref, gx_ref, sm_ref, dcw_ref,
             dh_scr, acc_scr, ppad, wacc):
        i = pl.program_id(0)
        part = i % parts

        @pl.when(i == 0)
        def _():
            acc_scr[...] = jnp.zeros_like(acc_scr)

        @pl.when(part == 0)
        def _():
            zero = jnp.zeros((CONV_PAD, 128), F32)
            for sub in range(nsub):
                ppad[sub, 0:CONV_PAD, :] = zero
                ppad[sub, s_len + CONV_PAD:s_len + 2 * CONV_PAD, :] = zero
                ppad[sub, CONV_PAD:s_len + CONV_PAD, :] = p_ref[sub]
            wacc[...] = jnp.zeros_like(wacc)

        def dw_tile(tile):
            t0 = pl.multiple_of((part * tiles + tile) * tt, tt)
            for sub in range(nsub):
                dqt = dq_ref[sub, pl.ds(t0, tt), :]
                for o in range(1, CONV_WIDTH + 1):
                    wacc[sub, o - 1] += _colsum8(dqt * ppad[sub, pl.ds(t0 + o, tt), :])
                wacc[sub, CONV_WIDTH] += _colsum8(dqt)

        def dz_cols(j, hf):
            if j < 4:
                return dz_ref[j, :, hf * 512:(hf + 1) * 512]
            return jnp.concatenate([dzc_ref[4 * hf + blk, j - 4] for blk in range(4)], axis=1)

        dots = [(j, hf) for j in range(N_GROUPS) for hf in range(2)]
        dh = jnp.zeros((tm, D_MODEL), F32)
        for d, (j, hf) in enumerate(dots):
            dh = dh + _dot_nt(dz_cols(j, hf), w_ref[2 * _natural_group(j) + hf])
            for tile in range(d * tiles // len(dots), (d + 1) * tiles // len(dots)):
                dw_tile(tile)
        dh_scr[...] = dh

        @pl.when(part == parts - 1)
        def _():
            for sub in range(nsub):
                for k in range(32):
                    dcw_ref[k:k + 1, sub * 128:(sub + 1) * 128] = jnp.sum(wacc[sub, k], axis=0, keepdims=True)

        scale1 = 1.0 + mod_ref[:, D_MODEL:2 * D_MODEL]
        g = g_ref[...]

        def rows_fn(rows):
            xt = x_ref[rows, :]
            r = lax.rsqrt(_mean(xt * xt) + EPS)
            xn = xt * r
            dhr = dh_scr[rows, :]
            acc_scr[0] += _colsum8(dhr)
            acc_scr[1] += _colsum8(dhr * (xn * g))
            acc_scr[2] += _colsum8(dhr * scale1 * xn)
            dxn = dhr * (g * scale1)
            gx_ref[rows, :] = dx2_ref[rows, :] + r * (dxn - xn * _mean(dxn * xn))
        _row_loop(tm, rows_fn, unroll=FULL_UNROLL)

        @pl.when(i == n_steps - 1)
        def _():
            for qi in range(8):
                sm_ref[qi:qi + 1, :] = jnp.sum(acc_scr[qi], axis=0, keepdims=True)

    tok = lambda i: (i, 0)
    const2 = lambda i: (0, 0)
    return pl.pallas_call(
        body, name="bwd_in",
        grid=(n_steps,),
        out_shape=(jax.ShapeDtypeStruct((s_len, D_MODEL), F32), jax.ShapeDtypeStruct((8, D_MODEL), F32),
                   jax.ShapeDtypeStruct((32, D_MODEL), F32)),
        in_specs=[pl.BlockSpec((4, tm, D_MODEL), lambda i: (0, i, 0)),
                  pl.BlockSpec((n_cblocks, 2, tm, 128), lambda i: (0, 0, i, 0)),
                  pl.BlockSpec((12, D_MODEL, 512), lambda i: (0, 0, 0), pipeline_mode=pl.Buffered(1)),
                  pl.BlockSpec((tm, D_MODEL), tok),
                  pl.BlockSpec((tm, D_MODEL), tok),
                  pl.BlockSpec((1, 3 * D_MODEL), const2),
                  pl.BlockSpec((1, D_MODEL), const2),
                  pl.BlockSpec((nsub, s_len, 128), lambda i: (i // parts, 0, 0)),
                  pl.BlockSpec((nsub, s_len, 128), lambda i: (i // parts, 0, 0))],
        out_specs=(pl.BlockSpec((tm, D_MODEL), tok), pl.BlockSpec((8, D_MODEL), const2),
                   pl.BlockSpec((32, cw), lambda i: (0, i // parts))),
        scratch_shapes=[pltpu.VMEM((tm, D_MODEL), F32), pltpu.VMEM((8, 8, D_MODEL), F32),
                        pltpu.VMEM((nsub, s_len + 2 * CONV_PAD, 128), F32), pltpu.VMEM((nsub, 32, 8, 128), F32)],
        compiler_params=_params(dimension_semantics=("arbitrary",)),
    )(dz4, dzc, w12, x, dx2, mod, norm_g, dq, p)


def _rs_shard_block(i, nj):
    g = jnp.where(i < 2 * nj, i % 2, jnp.where(i < 3 * nj, 2, 3))
    j = jnp.where(i < 2 * nj, i // 2, jnp.where(i < 3 * nj, i - 2 * nj, i - 3 * nj))
    return g, j


def _rs_block(i, k, nj):
    g, j = _rs_shard_block(i, nj)
    return nj * jnp.bitwise_xor(k, 3 - g) + j


def _rs_scratch(nj, nr, nc):
    nblk = N_CHIPS * nj
    return [pltpu.VMEM((2, 2, nr, nc), F32),
            pltpu.VMEM((nblk, nr, nc), F32),
            pltpu.VMEM((2 * nj, nr, nc), BF16),
            pltpu.VMEM((2 * nj, nr, nc), BF16),
            pltpu.VMEM((nj, nr, nc), BF16),
            pltpu.VMEM((nj, nr, nc), BF16),
            pltpu.VMEM((nj, nr, nc), F32),
            pltpu.SemaphoreType.DMA,
            pltpu.SemaphoreType.DMA((nblk,)), pltpu.SemaphoreType.DMA((nblk,)),
            pltpu.SemaphoreType.DMA((2 * nj,)), pltpu.SemaphoreType.DMA((2 * nj,)),
            pltpu.SemaphoreType.DMA((nj,)), pltpu.SemaphoreType.DMA((nj,)),
            pltpu.SemaphoreType.DMA, pltpu.SemaphoreType.DMA]


def _rs_step(i, kk, kt, nj, nr, nc, partial_fn, out_ref, scr):
    nblk = N_CHIPS * nj
    (acc, recv_a, send1, recv1, send2, recv2, own_ps,
     lsem, psend, precv, s1send, s1recv, s2send, s2recv, fsend, frecv) = scr
    x, y, c = _pos()
    sib = (x, y, 1 - c)
    first = (jnp.bitwise_xor(x, 1 - c), jnp.bitwise_xor(y, c), c)
    second = (jnp.bitwise_xor(x, c), jnp.bitwise_xor(y, 1 - c), c)
    slot = i % 2

    @pl.when(kk == 0)
    def _():
        acc[slot] = jnp.zeros((2, nr, nc), F32)

    def pair_copy(ii, sl):
        return _rcopy(acc.at[sl, 1 - c], recv_a.at[ii], psend.at[ii], precv.at[ii], sib)

    def pair_sum(ii, sl):
        cp = pair_copy(ii, sl)
        cp.wait_recv()
        cp.wait_send()
        return acc[sl, c] + recv_a[ii]

    def stage1_copy(src, dst):
        return _rcopy(send1.at[src], recv1.at[dst], s1send.at[src], s1recv.at[dst], first)

    def stage2_copy(j):
        return _rcopy(send2.at[j], recv2.at[j], s2send.at[j], s2recv.at[j], second)

    def finalize(ii, sl):
        ps = pair_sum(ii, sl)
        g, j = _rs_shard_block(ii, nj)

        @pl.when(g == 0)
        def _():
            send1[j] = ps.astype(BF16)
            stage1_copy(j, nj + j).start()

        @pl.when(g == 1 + c)
        def _():
            send1[nj + j] = ps.astype(BF16)
            stage1_copy(nj + j, j).start()

        @pl.when(g == 2 - c)
        def _():
            stage1_copy(0, nj + j).wait_recv()
            send2[j] = (ps + recv1[nj + j].astype(F32)).astype(BF16)
            stage2_copy(j).start()

        @pl.when(g == 3)
        def _():
            own_ps[j] = ps

    @pl.when(jnp.logical_and(kk == kt - 1, i >= 1))
    def _():
        finalize(i - 1, 1 - slot)

    def add_partial(res):
        acc[slot, 0] += res[0:nr]
        acc[slot, 1] += res[nr:2 * nr]
    partial_fn(add_partial)

    @pl.when(kk == kt - 1)
    def _():
        pair_copy(i, slot).start()

        @pl.when(i == nblk - 1)
        def _():
            own_ps[nj - 1] = pair_sum(nblk - 1, (nblk - 1) % 2)
            for j in range(nj):
                stage1_copy(0, j).wait_recv()
                stage2_copy(j).wait_recv()
                own_ps[j] = (own_ps[j] + recv1[j].astype(F32)) + recv2[j].astype(F32)
            loc = pltpu.make_async_copy(own_ps, out_ref.at[:, c], lsem)
            loc.start()
            swap = _rcopy(own_ps, out_ref.at[:, c], fsend, frecv, sib)
            swap.start()
            loc.wait()
            swap.wait_send()
            _rcopy(own_ps, out_ref.at[:, 1 - c], fsend, frecv, sib).wait_recv()
            for s in range(2 * nj):
                stage1_copy(s, 0).wait_send()
            for j in range(nj):
                stage2_copy(j).wait_send()


def _grad_w_in(kidx, h, dz4, dzc, sm2, dcw):
    s_len = h.shape[0]
    tk = min(K_TILE, s_len)
    kt = s_len // tk
    nj, nr, nc = 3, 512, 512
    nblk = N_CHIPS * nj

    def body(k_ref, a_ref, b_ref, bc_ref, sm2_ref, dcw_in, out_ref, pk2_ref, dcwsh_ref, dcb_ref, *scr):
        del k_ref
        b_pk2, b_dcw, r_pk2, r_dcw, bsend, brecv = scr[-6:]
        i = pl.program_id(0)
        kk = pl.program_id(1)
        x, y, c = _pos()
        k = 2 * x + y
        m = _rs_block(i, k, nj)

        @pl.when(jnp.logical_and(kk == 0, i == 0))
        def _():
            b = 4 * x + 2 * y + c
            b_pk2[0:8, :] = _onehot_rows(sm2_ref[0:1, :], b)
            b_pk2[8:16, :] = _onehot_rows(sm2_ref[1:2, :], b)
            b_pk2[16:17, :] = sm2_ref[2:3, :]
            b_pk2[17:24, :] = jnp.zeros((7, D_MODEL), F32)
            b_dcw[...] = dcw_in[...]

        for e in range(4):
            @pl.when(jnp.logical_and(kk == 0, i == (e * nblk) // 4))
            def _():
                _butterfly_event(e, [b_pk2, b_dcw], [r_pk2, r_dcw], bsend, brecv, _all_reduce_partners())

        @pl.when(jnp.logical_and(kk == 0, i == (3 * nblk) // 4))
        def _():
            pk2_ref[...] = b_pk2[...]
            sel = jnp.zeros((32, 256), F32)
            for kc in range(N_CHIPS):
                sel = jnp.where(k == kc, b_dcw[:, kc * 256:(kc + 1) * 256], sel)
            dcwsh_ref[...] = sel
            dcb_ref[...] = b_dcw[31:32, :]

        def partial(add):
            rows = pl.ds(pl.multiple_of(kk * tk, tk), tk)

            @pl.when(m >= 4)
            def _():
                add(_dot_tn(a_ref[rows, :], b_ref[0]))

            @pl.when(m < 4)
            def _():
                add(_dot_tn(a_ref[rows, :], jnp.concatenate([bc_ref[blk, 0] for blk in range(4)], axis=1)))
        _rs_step(i, kk, kt, nj, nr, nc, partial, out_ref, scr[:-6])

    def dz4_index(i, kk, k_ref):
        m = _rs_block(i, k_ref[0], nj)
        use = m >= 4
        return (jnp.where(use, m // 2 - 2, 0), jnp.where(use, kk, 0), jnp.where(use, m % 2, 0))

    def dzc_index(i, kk, k_ref):
        m = _rs_block(i, k_ref[0], nj)
        use = m < 4
        return (jnp.where(use, m % 2, 0), jnp.where(use, m // 2, 0), jnp.where(use, kk, 0), 0)

    const2 = lambda i, kk, k_ref: (0, 0)
    small_out = [(24, D_MODEL), (32, 256), (1, D_MODEL)]
    small_buf = [(24, D_MODEL), (32, D_MODEL)]
    grid_spec = pltpu.PrefetchScalarGridSpec(
        num_scalar_prefetch=1,
        grid=(nblk, kt),
        in_specs=[pl.BlockSpec((s_len, D_MODEL), const2, pipeline_mode=pl.Buffered(1)),
                  pl.BlockSpec((1, tk, 512), dz4_index),
                  pl.BlockSpec((4, 1, tk, 128), dzc_index),
                  pl.BlockSpec((8, D_MODEL), const2),
                  pl.BlockSpec((32, D_MODEL), const2)],
        out_specs=(pl.BlockSpec(memory_space=pl.ANY),) + tuple(pl.BlockSpec(s, const2) for s in small_out),
        scratch_shapes=_rs_scratch(nj, nr, nc)
        + [pltpu.VMEM(s, F32) for s in small_buf] + [pltpu.VMEM((3,) + s, F32) for s in small_buf]
        + [pltpu.SemaphoreType.DMA((6,)), pltpu.SemaphoreType.DMA((6,))])
    return pl.pallas_call(
        body, name="grad_w_in",
        grid_spec=grid_spec,
        out_shape=(jax.ShapeDtypeStruct((nj, 2, nr, nc), F32),) + tuple(jax.ShapeDtypeStruct(s, F32) for s in small_out),
        compiler_params=_params(dimension_semantics=("arbitrary", "arbitrary")),
    )(kidx, h, dz4, dzc, sm2, dcw)


def _adamw_math(w, g, m, v):
    m = ADAM_B1 * m + (1.0 - ADAM_B1) * g
    v = ADAM_B2 * v + (1.0 - ADAM_B2) * (g * g)
    m_hat = m / (1.0 - ADAM_B1 ** ADAM_STEP)
    v_hat = v / (1.0 - ADAM_B2 ** ADAM_STEP)
    delta = -ADAM_LR * (m_hat / (jnp.sqrt(v_hat) + ADAM_EPS) + ADAM_WD * w)
    return delta, m, v


def _adamw_blocked(w, m, v, g4, name):
    nj, _, nr, nc = g4.shape

    def body(w_ref, m_ref, v_ref, g_ref, go_ref, d_ref, mo_ref, vo_ref):
        g = g_ref[0, 0]
        d, mn, vn = _adamw_math(w_ref[...], g, m_ref[...], v_ref[...])
        go_ref[...] = g
        d_ref[...] = d
        mo_ref[...] = mn
        vo_ref[...] = vn

    blk = pl.BlockSpec((nr, nc), lambda j, hf: (hf, j))
    return pl.pallas_call(
        body, name=name,
        grid=(nj, 2),
        out_shape=tuple(jax.ShapeDtypeStruct(w.shape, F32) for _ in range(4)),
        in_specs=[blk, blk, blk, pl.BlockSpec((1, 1, nr, nc), lambda j, hf: (j, hf, 0, 0))],
        out_specs=(blk, blk, blk, blk),
        compiler_params=_params(dimension_semantics=("arbitrary", "arbitrary")),
    )(w, m, v, g4)


def _adamw_w_ada(cact, pk1, pk2, w, m, v):
    rb = 256

    def body(cact_ref, pk1_ref, pk2_ref, w_ref, m_ref, v_ref, g_ref, d_ref, mo_ref, vo_ref, gb_ref, dmod, sel_scr):
        @pl.when(pl.program_id(0) == 0)
        def _():
            x, y, _ = _pos()
            k = 2 * x + y
            dmod[:, 0:D_MODEL] = pk2_ref[0:8, :]
            dmod[:, D_MODEL:2 * D_MODEL] = pk2_ref[8:16, :]
            dmod[:, 2 * D_MODEL:3 * D_MODEL] = pk1_ref[0:8, :]
            gb_ref[...] = jnp.sum(dmod[...], axis=0, keepdims=True)
            sel = jnp.zeros((8, 768), F32)
            for kk in range(N_CHIPS):
                sel = jnp.where(k == kk, dmod[:, kk * 768:(kk + 1) * 768], sel)
            sel_scr[...] = sel

        g = _dot_tn(cact_ref[...].astype(BF16), sel_scr[...].astype(BF16))
        d, mn, vn = _adamw_math(w_ref[...], g, m_ref[...], v_ref[...])
        g_ref[...] = g
        d_ref[...] = d
        mo_ref[...] = mn
        vo_ref[...] = vn

    blk = pl.BlockSpec((rb, 768), lambda i: (i, 0))
    const2 = lambda i: (0, 0)
    return pl.pallas_call(
        body, name="adamw_w_ada",
        grid=(D_MODEL // rb,),
        out_shape=tuple(jax.ShapeDtypeStruct(w.shape, F32) for _ in range(4)) + (
            jax.ShapeDtypeStruct((1, 3 * D_MODEL), F32),),
        in_specs=[pl.BlockSpec((8, rb), lambda i: (0, i)), pl.BlockSpec((16, D_MODEL), const2),
                  pl.BlockSpec((24, D_MODEL), const2), blk, blk, blk],
        out_specs=(blk, blk, blk, blk, pl.BlockSpec((1, 3 * D_MODEL), const2)),
        scratch_shapes=[pltpu.VMEM((8, 3 * D_MODEL), F32), pltpu.VMEM((8, 768), F32)],
        compiler_params=_params(dimension_semantics=("arbitrary",)),
    )(cact, pk1, pk2, w, m, v)


def _adamw_small(items):
    n = len(items)

    def body(*refs):
        ins, outs = refs[:4 * n], refs[4 * n:]
        for i in range(n):
            w_ref, g_ref, m_ref, v_ref = ins[4 * i:4 * i + 4]
            d, mn, vn = _adamw_math(w_ref[...], g_ref[...], m_ref[...], v_ref[...])
            outs[3 * i][...] = d
            outs[3 * i + 1][...] = mn
            outs[3 * i + 2][...] = vn

    flat = [a for it in items for a in it]
    outs = pl.pallas_call(
        body, name="adamw_small",
        out_shape=tuple(jax.ShapeDtypeStruct(it[0].shape, F32) for it in items for _ in range(3)),
        in_specs=[_vmem()] * (4 * n),
        out_specs=tuple(_vmem() for _ in range(3 * n)),
        compiler_params=_params(),
    )(*flat)
    return [tuple(outs[3 * i:3 * i + 3]) for i in range(n)]


def kernel(x, c, w_ada, b_ada, norm_g, w_in, conv_w, conv_b, conv_ln_g, conv_ln_b, sg_ln_g, sg_ln_b, w_s, b_s, w_out, final_g, loss_target, m_w_ada, m_b_ada, m_norm_g, m_w_in, m_conv_w, m_conv_b, m_conv_ln_g, m_conv_ln_b, m_sg_ln_g, m_sg_ln_b, m_w_s, m_b_s, m_w_out, m_final_g, v_w_ada, v_b_ada, v_norm_g, v_w_in, v_conv_w, v_conv_b, v_conv_ln_g, v_conv_ln_b, v_sg_ln_g, v_sg_ln_b, v_w_s, v_b_s, v_w_out, v_final_g):
    s_len = x.shape[1]
    x2d = x[0]
    tgt = loss_target[0]
    row = lambda a: a.reshape(1, -1)

    kidx = (2 * lax.axis_index("x") + lax.axis_index("y")).astype(jnp.int32).reshape(1)
    cw_sh = jnp.pad(conv_w.reshape(CONV_WIDTH, 256), ((0, 1), (0, 0)))
    h, z6, mod, cact, cw4, w_in12 = _front(kidx, x2d, c, w_ada[0], b_ada, norm_g, w_in[0], cw_sh)
    w12 = w_in12.reshape(12, D_MODEL, 512)
    q, p_cm, sig_cm, t2_cm, w_out4 = _conv_fwd(z6, cw4, conv_b, w_out[0])
    w_out_full = w_out4.reshape(2 * D_MODEL, D_MODEL)
    ln_rows = (conv_ln_g, conv_ln_b, sg_ln_g, sg_ln_b, row(final_g))
    bs_exp = jnp.repeat(b_s[0].T, CHUNK, axis=1)
    dz4, dq, dx2, ycat, dy, dws, sm1, dbs = _middle(x2d, z6, q, tgt, ln_rows, mod, w_s[0], bs_exp, w_out_full)
    dzc, pk1, dws_r, dbs_r, loss_t, g_w_out4 = _conv_bwd(
        kidx, dq, sig_cm, t2_cm, cw4, sm1, dws.reshape(D_MODEL, CHUNK), dbs, ycat, dy)
    grad_x, sm2, dcw = _bwd_in(dz4, dzc, w12, x2d, dx2, mod, norm_g, dq, p_cm)
    g_w_in4, pk2, dcw_sh, dcb = _grad_w_in(kidx, h, dz4, dzc, sm2, dcw)

    g_w_in, d_w_in, nm_w_in, nv_w_in = _adamw_blocked(w_in[0], m_w_in[0], v_w_in[0], g_w_in4, "adamw_w_in")
    g_w_out, d_w_out, nm_w_out, nv_w_out = _adamw_blocked(w_out[0], m_w_out[0], v_w_out[0], g_w_out4, "adamw_w_out")
    g_w_ada, d_w_ada, nm_w_ada, nv_w_ada, g_b_ada = _adamw_w_ada(cact, pk1, pk2, w_ada[0], m_w_ada[0], v_w_ada[0])

    g_norm_g = pk2[16:17]
    g_cln_g, g_cln_b, g_sln_g, g_sln_b, g_final = (pk1[8 + i:9 + i] for i in range(5))
    loss = loss_t[0, 0]
    g_conv_w = dcw_sh[:CONV_WIDTH]
    g_w_s = dws_r
    g_b_s = dbs_r[:, :HEADS].T
    small = [
        (b_ada, g_b_ada, m_b_ada, v_b_ada),
        (norm_g, g_norm_g, m_norm_g, v_norm_g),
        (conv_w.reshape(CONV_WIDTH, 256), g_conv_w, m_conv_w.reshape(CONV_WIDTH, 256), v_conv_w.reshape(CONV_WIDTH, 256)),
        (conv_b, dcb, m_conv_b, v_conv_b),
        (conv_ln_g, g_cln_g, m_conv_ln_g, v_conv_ln_g),
        (conv_ln_b, g_cln_b, m_conv_ln_b, v_conv_ln_b),
        (sg_ln_g, g_sln_g, m_sg_ln_g, v_sg_ln_g),
        (sg_ln_b, g_sln_b, m_sg_ln_b, v_sg_ln_b),
        (w_s.reshape(D_MODEL, CHUNK), g_w_s, m_w_s.reshape(D_MODEL, CHUNK), v_w_s.reshape(D_MODEL, CHUNK)),
        (b_s[0], g_b_s, m_b_s[0], v_b_s[0]),
        (row(final_g), g_final, row(m_final_g), row(v_final_g)),
    ]
    upd = _adamw_small(small)

    shapes = [w_ada.shape, b_ada.shape, norm_g.shape, w_in.shape, conv_w.shape, conv_b.shape, conv_ln_g.shape,
              conv_ln_b.shape, sg_ln_g.shape, sg_ln_b.shape, w_s.shape, b_s.shape, w_out.shape, final_g.shape]
    grads = [g_w_ada, g_b_ada, g_norm_g, g_w_in, g_conv_w, dcb, g_cln_g, g_cln_b, g_sln_g, g_sln_b, g_w_s, g_b_s,
             g_w_out, g_final]
    big = {0: (d_w_ada, nm_w_ada, nv_w_ada), 3: (d_w_in, nm_w_in, nv_w_in), 12: (d_w_out, nm_w_out, nv_w_out)}
    small_pos = [1, 2, 4, 5, 6, 7, 8, 9, 10, 11, 13]
    trip = [None] * 14
    for i, t in big.items():
        trip[i] = t
    for i, t in zip(small_pos, upd):
        trip[i] = t
    fit = lambda arrs: [a.reshape(s) for a, s in zip(arrs, shapes)]
    return (loss, grad_x.reshape(x.shape), *fit(grads), *fit([t[0] for t in trip]), *fit([t[1] for t in trip]),
            *fit([t[2] for t in trip]))
```

```python
import jax
import jax.numpy as jnp
from jax import lax
from jax.experimental import pallas as pl
from jax.experimental.pallas import tpu as pltpu

F32 = jnp.float32
BF16 = jnp.bfloat16
MESH = pl.DeviceIdType.MESH

D_MODEL = 1024
N_CHIPS = 4
HEADS = 8
CHUNK = 128
CONV_WIDTH = 31
CONV_HALF = CONV_WIDTH // 2
CONV_PAD = 16
EPS = 1e-6
ADAM_LR = 0.001
ADAM_B1 = 0.9
ADAM_B2 = 0.999
ADAM_EPS = 1e-08
ADAM_WD = 0.01
ADAM_STEP = 10

V7X_VMEM_BYTES = 64 * 1024 * 1024
VMEM_LIMIT = V7X_VMEM_BYTES - 8 * 1024 * 1024
ROWS = 16
UNROLL = 8
TOKEN_TILE = 256
FULL_UNROLL = TOKEN_TILE // ROWS
TIME_TILE = 128
K_TILE = 2048

N_GROUPS = 6


def _natural_group(j):
    return (j + 2) % N_GROUPS


def _pos():
    return lax.axis_index("x"), lax.axis_index("y"), lax.axis_index("c")


def _rcopy(src, dst, ssem, rsem, dev):
    return pltpu.make_async_remote_copy(src_ref=src, dst_ref=dst, send_sem=ssem, recv_sem=rsem,
                                        device_id=dev, device_id_type=MESH)


def _vmem():
    return pl.BlockSpec(memory_space=pltpu.VMEM)


def _params(**kw):
    return pltpu.CompilerParams(vmem_limit_bytes=VMEM_LIMIT, **kw)


def _sigmoid(v):
    return 0.5 * jnp.tanh(0.5 * v) + 0.5


def _row_loop(n_rows, body, unroll=1):
    def step(r, carry):
        body(pl.ds(pl.multiple_of(r * ROWS, ROWS), ROWS))
        return carry
    lax.fori_loop(0, n_rows // ROWS, step, 0, unroll=unroll)


def _colsum8(v):
    return v.reshape(v.shape[0] // 8, 8, v.shape[1]).sum(axis=0)


def _mean(v):
    return jnp.mean(v, axis=-1, keepdims=True)


def _dot_nn(a, b):
    return jnp.dot(a, b, preferred_element_type=F32)


def _dot_nt(a, b):
    return lax.dot_general(a, b, (((1,), (1,)), ((), ())), preferred_element_type=F32)


def _dot_tn(a, b):
    return lax.dot_general(a, b, (((0,), (0,)), ((), ())), preferred_element_type=F32)


def _remote_chip(k, r):
    return jnp.bitwise_xor(k, r + 1)


def _front(kidx, x, c, w_ada, b_ada, norm_g, w_in, cw):
    s_len = x.shape[0]
    tmh = min(512, s_len)
    tmz = min(2048, s_len)
    nh = s_len // tmh
    nz = s_len // tmz
    n_steps = nh + 12 * nz

    def remote_block(q):
        return jnp.where(q < 6, q % 2, 2), jnp.where(q < 6, q // 2, q - 6)

    def block_of(i, k):
        r, j = remote_block(jnp.maximum(i - 3, 0))
        return jnp.where(i < 3, 3 * k + i, 3 * _remote_chip(k, r) + j)

    def body(k_ref, x_ref, c_ref, bada_ref, g_ref, cw_ref, wada_hbm, win_hbm,
             h_ref, z_ref, mod_ref, cact_ref, cw4_ref, w12_hbm,
             h_all, wbuf, stage, wada_v, cslab, mslab, cw4_s,
             lsem, csend, crecv, msend, mrecv, wsend, wrecv, isend, irecv, fsend, frecv, osem):
        del k_ref
        t = pl.program_id(0)
        x_, y_, c_ = _pos()
        k = 2 * x_ + y_
        b = 4 * x_ + 2 * y_ + c_
        sib = (x_, y_, 1 - c_)

        def dev_of(r):
            kk = _remote_chip(k, r)
            return (kk // 2, kk % 2, c_)

        def ici(q, kk):
            r, j = remote_block(q)
            return _rcopy(wbuf.at[3 * kk + j, c_], wbuf.at[3 * kk + j, c_], isend.at[q], irecv.at[q], dev_of(r))

        def fwd(q, hf):
            r, j = remote_block(q)
            blk = 3 * _remote_chip(k, r) + j
            return _rcopy(wbuf.at[blk, hf], wbuf.at[blk, hf], fsend.at[q], frecv.at[q], sib)

        def c_copy(q, src):
            d = jnp.bitwise_xor(b, q)
            return _rcopy(cslab.at[src], cslab.at[src], csend.at[q - 1], crecv.at[q - 1], (d // 4, (d // 2) % 2, d % 2))

        def m_copy(r, kk):
            return _rcopy(mslab.at[kk], mslab.at[kk], msend.at[r], mrecv.at[r], dev_of(r))

        def cw_copy(r, kk):
            return _rcopy(cw4_s.at[kk], cw4_s.at[kk], wsend.at[r], wrecv.at[r], dev_of(r))

        def to_hbm(i):
            m = block_of(i, k)
            return pltpu.make_async_copy(wbuf.at[m], w12_hbm.at[m], osem.at[i])

        @pl.when(t == 0)
        def _():
            ld_w = pltpu.make_async_copy(win_hbm, stage, lsem.at[0])
            ld_w.start()
            ld_a = pltpu.make_async_copy(wada_hbm, wada_v, lsem.at[1])
            ld_a.start()
            cslab[b] = jnp.broadcast_to(c_ref[...], (8, D_MODEL))
            for q in range(1, 8):
                c_copy(q, b).start()
            cw4_s[k] = cw_ref[...]
            for r in range(3):
                cw_copy(r, k).start()
            ld_w.wait()
            for j in range(3):
                for hf in range(2):
                    wbuf[3 * k + j, hf] = stage[hf * 512:(hf + 1) * 512, j * 512:(j + 1) * 512].astype(BF16)
            for i in range(3):
                to_hbm(i).start()
            for q in range(1, 8):
                c_copy(q, jnp.bitwise_xor(b, q)).wait_recv()
            row = lax.broadcasted_iota(jnp.int32, (8, D_MODEL), 0)
            call = jnp.zeros((8, D_MODEL), F32)
            for d in range(8):
                call = jnp.where(row == d, cslab[d], call)
            cact = call * _sigmoid(call)
            cact_ref[...] = cact
            ld_a.wait()
            mslab[k] = _dot_nn(cact.astype(BF16), wada_v[...].astype(BF16))
            for r in range(3):
                m_copy(r, k).start()
            for q in range(9):
                ici(q, k).start()
            for r in range(3):
                m_copy(r, _remote_chip(k, r)).wait_recv()
            row8 = lax.broadcasted_iota(jnp.int32, (8, 768), 0)
            for kk in range(N_CHIPS):
                piece = jnp.sum(jnp.where(row8 == b, mslab[kk], 0.0), axis=0, keepdims=True)
                mod_ref[:, kk * 768:(kk + 1) * 768] = piece + bada_ref[:, kk * 768:(kk + 1) * 768]
            for r in range(3):
                cw_copy(r, _remote_chip(k, r)).wait_recv()
            cw4_ref[...] = cw4_s[...]

        @pl.when(t < nh)
        def _():
            shift = mod_ref[:, 0:D_MODEL]
            scale1 = 1.0 + mod_ref[:, D_MODEL:2 * D_MODEL]
            g = g_ref[...]
            base = t * tmh

            def rows_fn(rows):
                xt = x_ref[rows, :]
                r = lax.rsqrt(_mean(xt * xt) + EPS)
                hv = ((xt * r * g) * scale1 + shift).astype(BF16)
                h_ref[rows, :] = hv
                h_all[pl.ds(pl.multiple_of(base + rows.start, ROWS), ROWS), :] = hv
            _row_loop(tmh, rows_fn, unroll=UNROLL)

        @pl.when(t >= nh)
        def _():
            u = t - nh
            i = u // nz
            rt = u % nz
            @pl.when(jnp.logical_and(rt == 0, jnp.logical_and(i >= 2, i <= 10)))
            def _():
                q = i - 2
                r, _ = remote_block(q)
                ici(q, _remote_chip(k, r)).wait_recv()
                fwd(q, c_).start()

            @pl.when(jnp.logical_and(rt == 0, i >= 3))
            def _():
                fwd(i - 3, 1 - c_).wait_recv()
                to_hbm(i).start()
            m = block_of(i, k)
            hb = h_all[pl.ds(pl.multiple_of(rt * tmz, tmz), tmz), :]
            z_ref[0] = _dot_nn(hb, wbuf[m].reshape(D_MODEL, 512))

        @pl.when(t == n_steps - 1)
        def _():
            for q in range(1, 8):
                c_copy(q, b).wait_send()
            for r in range(3):
                m_copy(r, k).wait_send()
                cw_copy(r, k).wait_send()
            for q in range(9):
                ici(q, k).wait_send()
                fwd(q, c_).wait_send()
            for i in range(12):
                to_hbm(i).wait()

    def z_index(t, k_ref):
        u = jnp.maximum(t - nh, 0)
        m = block_of(u // nz, k_ref[0])
        return ((m // 2 + 4) % N_GROUPS, u % nz, m % 2)

    tok = lambda t, k_ref: (jnp.minimum(t, nh - 1), 0)
    const2 = lambda t, k_ref: (0, 0)
    grid_spec = pltpu.PrefetchScalarGridSpec(
        num_scalar_prefetch=1,
        grid=(n_steps,),
        in_specs=[pl.BlockSpec((tmh, D_MODEL), tok),
                  pl.BlockSpec((1, D_MODEL), const2),
                  pl.BlockSpec((1, 3 * D_MODEL), const2),
                  pl.BlockSpec((1, D_MODEL), const2),
                  pl.BlockSpec((32, 256), const2),
                  pl.BlockSpec(memory_space=pl.ANY),
                  pl.BlockSpec(memory_space=pl.ANY)],
        out_specs=(pl.BlockSpec((tmh, D_MODEL), tok),
                   pl.BlockSpec((1, tmz, 512), z_index),
                   pl.BlockSpec((1, 3 * D_MODEL), const2),
                   pl.BlockSpec((8, D_MODEL), const2),
                   pl.BlockSpec((N_CHIPS, 32, 256), lambda t, k_ref: (0, 0, 0)),
                   pl.BlockSpec(memory_space=pl.ANY)),
        scratch_shapes=[pltpu.VMEM((s_len, D_MODEL), BF16),
                        pltpu.VMEM((12, 2, 512, 512), BF16),
                        pltpu.VMEM((D_MODEL, 1536), F32),
                        pltpu.VMEM((D_MODEL, 768), F32),
                        pltpu.VMEM((8, 8, D_MODEL), F32),
                        pltpu.VMEM((N_CHIPS, 8, 768), F32),
                        pltpu.VMEM((N_CHIPS, 32, 256), F32),
                        pltpu.SemaphoreType.DMA((2,)),
                        pltpu.SemaphoreType.DMA((7,)), pltpu.SemaphoreType.DMA((7,)),
                        pltpu.SemaphoreType.DMA((3,)), pltpu.SemaphoreType.DMA((3,)),
                        pltpu.SemaphoreType.DMA((3,)), pltpu.SemaphoreType.DMA((3,)),
                        pltpu.SemaphoreType.DMA((9,)), pltpu.SemaphoreType.DMA((9,)),
                        pltpu.SemaphoreType.DMA((9,)), pltpu.SemaphoreType.DMA((9,)),
                        pltpu.SemaphoreType.DMA((12,))])
    return pl.pallas_call(
        body, name="front",
        grid_spec=grid_spec,
        out_shape=(jax.ShapeDtypeStruct((s_len, D_MODEL), BF16),
                   jax.ShapeDtypeStruct((N_GROUPS, s_len, D_MODEL), F32),
                   jax.ShapeDtypeStruct((1, 3 * D_MODEL), F32),
                   jax.ShapeDtypeStruct((8, D_MODEL), F32),
                   jax.ShapeDtypeStruct((N_CHIPS, 32, 256), F32),
                   jax.ShapeDtypeStruct((12, 2, 512, 512), BF16)),
        compiler_params=_params(dimension_semantics=("arbitrary",)),
    )(kidx, x, c, b_ada, norm_g, cw, w_ada, w_in)


def _all_reduce_partners():
    x, y, c = _pos()
    return [(x, y, 1 - c), (x, 1 - y, c), (1 - x, y, c)]


def _butterfly_event(e, bufs, recvs, ssem, rsem, partners, wires=None):
    n = len(bufs)
    wires = wires or [None] * n

    def copies(s):
        return [_rcopy(buf if wire is None else wire, recv.at[s], ssem.at[s * n + i], rsem.at[s * n + i], partners[s])
                for i, (buf, recv, wire) in enumerate(zip(bufs, recvs, wires))]

    if e > 0:
        for cp in copies(e - 1):
            cp.wait()
        for buf, recv, wire in zip(bufs, recvs, wires):
            mine = buf[...] if wire is None else wire[...].astype(F32)
            buf[...] = mine + recv[e - 1].astype(F32)
    if e < len(partners):
        for buf, wire in zip(bufs, wires):
            if wire is not None:
                wire[...] = buf[...].astype(BF16)
        for cp in copies(e):
            cp.start()


def _onehot_rows(v, b):
    row = lax.broadcasted_iota(jnp.int32, (8, v.shape[1]), 0)
    return jnp.where(row == b, jnp.broadcast_to(v, (8, v.shape[1])), 0.0)


def _conv_fwd(z6, cw4, conv_b, w_out):
    s_len = z6.shape[1]
    tt = TIME_TILE
    n_blocks = D_MODEL // 128

    def body(z_ref, cw_ref, cb_ref, wout_hbm, q_ref, p_ref, sig_ref, t2_ref, wout4_hbm,
             ppad, stage, wbuf, lsem, isend, irecv, fsend, frecv, osem):
        jb = pl.program_id(0)
        x_, y_, c_ = _pos()
        k = 2 * x_ + y_
        sib = (x_, y_, 1 - c_)

        def ici(r, kk):
            rk = _remote_chip(k, r)
            return _rcopy(wbuf.at[kk, c_], wbuf.at[kk, c_], isend.at[r], irecv.at[r], (rk // 2, rk % 2, c_))

        def fwd(r, hf):
            kk = _remote_chip(k, r)
            return _rcopy(wbuf.at[kk, hf], wbuf.at[kk, hf], fsend.at[r], frecv.at[r], sib)

        @pl.when(jb == 0)
        def _():
            ld = pltpu.make_async_copy(wout_hbm, stage, lsem)
            ld.start()
            ld.wait()
            for hf in range(2):
                wbuf[k, hf] = stage[hf * 256:(hf + 1) * 256, :].astype(BF16)
            for r in range(3):
                ici(r, k).start()

        @pl.when(jb == (5 * n_blocks) // 8)
        def _():
            for r in range(3):
                ici(r, _remote_chip(k, r)).wait_recv()
                fwd(r, c_).start()

        zero = jnp.zeros((CONV_PAD, 128), F32)
        ppad[0:CONV_PAD, :] = zero
        ppad[s_len + CONV_PAD:s_len + 2 * CONV_PAD, :] = zero

        def fill(i, carry):
            t0 = pl.multiple_of(i * tt, tt)
            sig = _sigmoid(z_ref[1, pl.ds(t0, tt), :])
            p = z_ref[0, pl.ds(t0, tt), :] * sig
            ppad[pl.ds(CONV_PAD + t0, tt), :] = p
            p_ref[0, pl.ds(t0, tt), :] = p
            sig_ref[0, pl.ds(t0, tt), :] = sig
            t2_ref[0, pl.ds(t0, tt), :] = p * (1.0 - sig)
            return carry
        lax.fori_loop(0, s_len // tt, fill, 0)
        w = cw_ref[0]
        bias = cb_ref[...]

        def conv(i, carry):
            t0 = pl.multiple_of(i * tt, tt)
            acc = jnp.broadcast_to(bias, (tt, 128))
            for o in range(1, CONV_WIDTH + 1):
                acc = acc + w[o - 1:o, :] * ppad[pl.ds(t0 + o, tt), :]
            q_ref[0, pl.ds(t0, tt), :] = acc
            return carry
        lax.fori_loop(0, s_len // tt, conv, 0)

        @pl.when(jb == n_blocks - 1)
        def _():
            for r in range(3):
                fwd(r, 1 - c_).wait_recv()
            out = pltpu.make_async_copy(wbuf, wout4_hbm, osem)
            out.start()
            for r in range(3):
                ici(r, k).wait_send()
                fwd(r, c_).wait_send()
            out.wait()

    return pl.pallas_call(
        body, name="conv_fwd",
        grid=(n_blocks,),
        out_shape=tuple(jax.ShapeDtypeStruct((n_blocks, s_len, 128), F32) for _ in range(4))
        + (jax.ShapeDtypeStruct((N_CHIPS, 2, 256, D_MODEL), BF16),),
        in_specs=[pl.BlockSpec((2, s_len, 128), lambda j: (2, 0, j)),
                  pl.BlockSpec((1, 32, 128), lambda j: (j // 2, 0, j % 2)),
                  pl.BlockSpec((1, 128), lambda j: (0, j)),
                  pl.BlockSpec(memory_space=pl.ANY)],
        out_specs=tuple(pl.BlockSpec((1, s_len, 128), lambda j: (j, 0, 0)) for _ in range(4))
        + (pl.BlockSpec(memory_space=pl.ANY),),
        scratch_shapes=[pltpu.VMEM((s_len + 2 * CONV_PAD, 128), F32),
                        pltpu.VMEM((512, D_MODEL), F32),
                        pltpu.VMEM((N_CHIPS, 2, 256, D_MODEL), BF16),
                        pltpu.SemaphoreType.DMA,
                        pltpu.SemaphoreType.DMA((3,)), pltpu.SemaphoreType.DMA((3,)),
                        pltpu.SemaphoreType.DMA((3,)), pltpu.SemaphoreType.DMA((3,)),
                        pltpu.SemaphoreType.DMA],
        compiler_params=_params(dimension_semantics=("arbitrary",)),
    )(z6, cw4, conv_b, w_out)


def _middle(x, z6, q, target, ln_rows, mod, w_s, bs_exp, w_out):
    s_len = x.shape[0]
    tm = TOKEN_TILE
    n_steps = s_len // tm
    n_chunks = tm // CHUNK
    inv_d = 1.0 / D_MODEL

    def body(x_ref, z_ref, q_ref, tgt_ref, cg_ref, cb_ref, sg_ref, sb_ref, fg_ref, mod_ref, ws_ref, bs_ref, wout_ref,
             dz_ref, dq_ref, dx2_ref, ycat_ref, dy_ref, dws_ref, sm_ref, dbs_ref,
             vl_scr, vm_scr, y_scr, dycat_scr, dvm_scr, dvl_scr, acc_scr, dbs_acc, keep, rstd_scr):
        i = pl.program_id(0)

        @pl.when(i == 0)
        def _():
            acc_scr[...] = jnp.zeros_like(acc_scr)
            dbs_acc[...] = jnp.zeros_like(dbs_acc)
            dws_ref[...] = jnp.zeros_like(dws_ref)

        cg, cb, sg, sb, fg = cg_ref[...], cb_ref[...], sg_ref[...], sb_ref[...], fg_ref[...]
        gm = mod_ref[:, 2 * D_MODEL:3 * D_MODEL]

        def norm_stats(t):
            c = t - _mean(t)
            rstd = lax.rsqrt(_mean(c * c) + EPS)
            return c * rstd, rstd

        def phase1(rows):
            qhat, rstd_q = norm_stats(jnp.concatenate([q_ref[blk, rows, :] for blk in range(D_MODEL // 128)], axis=1))
            ln = qhat * cg + cb
            gz = z_ref[0, rows, :]
            sig_ln = _sigmoid(ln)
            sig_g = _sigmoid(gz)
            ycat_ref[rows, 0:D_MODEL] = ((ln * sig_ln) * (gz * sig_g)).astype(BF16)
            vhat, rstd_v = norm_stats(z_ref[2, rows, :])
            vl_scr[rows, :] = (vhat * sg + sb).astype(BF16)
            keep[0, rows, :] = qhat
            keep[1, rows, :] = vhat
            keep[2, rows, :] = sig_ln
            keep[3, rows, :] = sig_g
            rstd_scr[0, rows, :] = rstd_q
            rstd_scr[1, rows, :] = rstd_v
        _row_loop(tm, phase1, unroll=FULL_UNROLL)

        for ch in range(n_chunks):
            r0 = ch * CHUNK
            for h in range(HEADS):
                c0 = h * CHUNK
                vm_scr[r0:r0 + CHUNK, c0:c0 + CHUNK] = (
                    _dot_nn(ws_ref[h].astype(BF16), vl_scr[r0:r0 + CHUNK, c0:c0 + CHUNK]) + bs_ref[:, c0:c0 + CHUNK])

        def phase3(rows):
            bg = z_ref[3, rows, :]
            sig_b = _sigmoid(bg)
            keep[4, rows, :] = sig_b
            ycat_ref[rows, D_MODEL:2 * D_MODEL] = (z_ref[1, rows, :] * vm_scr[rows, :] * (bg * sig_b)).astype(BF16)
        _row_loop(tm, phase3, unroll=FULL_UNROLL)

        y_scr[...] = _dot_nn(ycat_ref[...], wout_ref[...])

        def phase5(rows):
            y = y_scr[rows, :]
            x2 = x_ref[rows, :] + gm * y
            r2 = lax.rsqrt(_mean(x2 * x2) + EPS)
            xn2 = x2 * r2
            diff = xn2 * fg - tgt_ref[rows, :]
            acc_scr[6] += _colsum8(diff * diff)
            dout = diff * inv_d
            acc_scr[0] += _colsum8(dout * xn2)
            dxn = dout * fg
            dx2 = r2 * (dxn - xn2 * _mean(dxn * xn2))
            dx2_ref[rows, :] = dx2
            acc_scr[1] += _colsum8(dx2 * y)
            dy_ref[rows, :] = (dx2 * gm).astype(BF16)
        _row_loop(tm, phase5, unroll=FULL_UNROLL)

        dycat_scr[...] = _dot_nt(dy_ref[...], wout_ref[...])

        def phase7(rows):
            dyb = dycat_scr[rows, D_MODEL:2 * D_MODEL]
            u = z_ref[1, rows, :]
            bg = z_ref[3, rows, :]
            vm = vm_scr[rows, :]
            sig = keep[4, rows, :]
            silu = bg * sig
            dz_ref[1, rows, :] = (dyb * vm * silu).astype(BF16)
            dvm = dyb * u * silu
            dz_ref[3, rows, :] = (dyb * u * vm * (sig * (1.0 + bg * (1.0 - sig)))).astype(BF16)
            dvm_scr[rows, :] = dvm.astype(BF16)
            pos = pl.ds(pl.multiple_of(rows.start % CHUNK, ROWS), ROWS)
            dbs_acc[pos, :] += dvm
        _row_loop(tm, phase7, unroll=FULL_UNROLL)

        for ch in range(n_chunks):
            r0 = ch * CHUNK
            for h in range(HEADS):
                c0 = h * CHUNK
                dvm_b = dvm_scr[r0:r0 + CHUNK, c0:c0 + CHUNK]
                dws_ref[h] += _dot_nt(dvm_b, vl_scr[r0:r0 + CHUNK, c0:c0 + CHUNK])
                dvl_scr[r0:r0 + CHUNK, c0:c0 + CHUNK] = _dot_tn(ws_ref[h].astype(BF16), dvm_b)

        def phase9(rows):
            vhat, rstd_v = keep[1, rows, :], rstd_scr[1, rows, :]
            dvl = dvl_scr[rows, :]
            acc_scr[4] += _colsum8(dvl * vhat)
            acc_scr[5] += _colsum8(dvl)
            dvh = dvl * sg
            dz_ref[2, rows, :] = (rstd_v * (dvh - _mean(dvh) - vhat * _mean(dvh * vhat))).astype(BF16)
            qhat, rstd_q = keep[0, rows, :], rstd_scr[0, rows, :]
            ln = qhat * cg + cb
            sig_ln = keep[2, rows, :]
            gz = z_ref[0, rows, :]
            sig_g = keep[3, rows, :]
            dya = dycat_scr[rows, 0:D_MODEL]
            dz_ref[0, rows, :] = (dya * (ln * sig_ln) * (sig_g * (1.0 + gz * (1.0 - sig_g)))).astype(BF16)
            dln = (dya * (gz * sig_g)) * (sig_ln * (1.0 + ln * (1.0 - sig_ln)))
            acc_scr[2] += _colsum8(dln * qhat)
            acc_scr[3] += _colsum8(dln)
            dqh = dln * cg
            dq = rstd_q * (dqh - _mean(dqh) - qhat * _mean(dqh * qhat))
            for blk in range(D_MODEL // 128):
                dq_ref[blk, rows, :] = dq[:, blk * 128:(blk + 1) * 128]
        _row_loop(tm, phase9, unroll=FULL_UNROLL)

        @pl.when(i == n_steps - 1)
        def _():
            for qi in range(8):
                scale = 0.5 * inv_d if qi == 6 else 1.0
                sm_ref[qi:qi + 1, :] = jnp.sum(acc_scr[qi], axis=0, keepdims=True) * scale
            lane = lax.broadcasted_iota(jnp.int32, (CHUNK, CHUNK), 1)
            tile = jnp.zeros((CHUNK, CHUNK), F32)
            for h in range(HEADS):
                col = jnp.sum(dbs_acc[:, h * CHUNK:(h + 1) * CHUNK], axis=1, keepdims=True)
                tile = jnp.where(lane == h, col, tile)
            dbs_ref[...] = tile

    tok = lambda i: (i, 0)
    const2 = lambda i: (0, 0)
    return pl.pallas_call(
        body, name="middle",
        grid=(n_steps,),
        out_shape=(jax.ShapeDtypeStruct((4, s_len, D_MODEL), BF16),
                   jax.ShapeDtypeStruct((D_MODEL // 128, s_len, 128), F32),
                   jax.ShapeDtypeStruct((s_len, D_MODEL), F32),
                   jax.ShapeDtypeStruct((s_len, 2 * D_MODEL), BF16),
                   jax.ShapeDtypeStruct((s_len, D_MODEL), BF16),
                   jax.ShapeDtypeStruct((HEADS, CHUNK, CHUNK), F32),
                   jax.ShapeDtypeStruct((8, D_MODEL), F32),
                   jax.ShapeDtypeStruct((CHUNK, CHUNK), F32)),
        in_specs=[pl.BlockSpec((tm, D_MODEL), tok),
                  pl.BlockSpec((4, tm, D_MODEL), lambda i: (0, i, 0)),
                  pl.BlockSpec((D_MODEL // 128, tm, 128), lambda i: (0, i, 0)),
                  pl.BlockSpec((tm, D_MODEL), tok),
                  *[pl.BlockSpec((1, D_MODEL), const2) for _ in range(5)],
                  pl.BlockSpec((1, 3 * D_MODEL), const2),
                  pl.BlockSpec((HEADS, CHUNK, CHUNK), lambda i: (0, 0, 0)),
                  pl.BlockSpec((CHUNK, D_MODEL), const2),
                  pl.BlockSpec((2 * D_MODEL, D_MODEL), const2, pipeline_mode=pl.Buffered(1))],
        out_specs=(pl.BlockSpec((4, tm, D_MODEL), lambda i: (0, i, 0)),
                   pl.BlockSpec((D_MODEL // 128, tm, 128), lambda i: (0, i, 0)),
                   pl.BlockSpec((tm, D_MODEL), tok),
                   pl.BlockSpec((tm, 2 * D_MODEL), tok),
                   pl.BlockSpec((tm, D_MODEL), tok),
                   pl.BlockSpec((HEADS, CHUNK, CHUNK), lambda i: (0, 0, 0)),
                   pl.BlockSpec((8, D_MODEL), const2),
                   pl.BlockSpec((CHUNK, CHUNK), const2)),
        scratch_shapes=[pltpu.VMEM((tm, D_MODEL), BF16),
                        pltpu.VMEM((tm, D_MODEL), F32),
                        pltpu.VMEM((tm, D_MODEL), F32),
                        pltpu.VMEM((tm, 2 * D_MODEL), F32),
                        pltpu.VMEM((tm, D_MODEL), BF16),
                        pltpu.VMEM((tm, D_MODEL), F32),
                        pltpu.VMEM((8, 8, D_MODEL), F32),
                        pltpu.VMEM((CHUNK, D_MODEL), F32),
                        pltpu.VMEM((5, tm, D_MODEL), F32),
                        pltpu.VMEM((2, tm, 1), F32)],
        compiler_params=_params(dimension_semantics=("arbitrary",)),
    )(x, z6, q, target, *ln_rows, mod, w_s, bs_exp, w_out)


def _conv_bwd(kidx, dq, sig, t2, cw4, sm1, dws, dbs, ycat, dy):
    s_len = dq.shape[1]
    tt = TIME_TILE
    n_blocks = D_MODEL // 128
    shp = [(16, D_MODEL), (D_MODEL, 128), (128, 128)]
    nj, nr, nc = 1, 256, D_MODEL
    kt = n_blocks // N_CHIPS
    tk = s_len // kt

    def body(k_ref, dq_ref, sig_ref, t2_ref, cw_ref, sm1_ref, dws_in, dbs_in, ycat_ref, dy_ref,
             dz_ref, pk1_ref, dws_ref, dbs_ref, loss_ref, gout_ref,
             dqpad, b_pk, b_dws, b_dbs, r_pk, r_dws, r_dbs, w_dws, ssem, rsem, *rs_scr):
        del k_ref
        jb = pl.program_id(0)
        x_, y_, c_ = _pos()

        @pl.when(jb == 0)
        def _():
            b_pk[0:8, :] = _onehot_rows(sm1_ref[1:2, :], 4 * x_ + 2 * y_ + c_)
            for r, src in enumerate([2, 3, 4, 5, 0, 6]):
                b_pk[8 + r:9 + r, :] = sm1_ref[src:src + 1, :]
            b_pk[14:16, :] = jnp.zeros((2, D_MODEL), F32)
            b_dws[...] = dws_in[...]
            b_dbs[...] = dbs_in[...]

        for e, step in enumerate([0, 0, (3 * n_blocks) // 8, (6 * n_blocks) // 8]):
            @pl.when(jb == step)
            def _():
                _butterfly_event(e, [b_pk, b_dws, b_dbs], [r_pk, r_dws, r_dbs], ssem, rsem, _all_reduce_partners(),
                                 wires=[None, w_dws, None])

        @pl.when(jb == (6 * n_blocks) // 8)
        def _():
            pk1_ref[...] = b_pk[...]
            dws_ref[...] = b_dws[...]
            dbs_ref[...] = b_dbs[...]
            loss_ref[...] = jnp.broadcast_to(jnp.sum(b_pk[13:14, :], axis=1, keepdims=True), (8, 128))

        zero = jnp.zeros((CONV_PAD, 128), F32)
        dqpad[0:CONV_PAD, :] = zero
        dqpad[s_len + CONV_PAD:s_len + 2 * CONV_PAD, :] = zero
        dqpad[CONV_PAD:s_len + CONV_PAD, :] = dq_ref[0]
        w = cw_ref[0]

        def bwd(i, carry):
            t0 = i * tt
            dp = jnp.zeros((tt, 128), F32)
            for o in range(1, CONV_WIDTH + 1):
                dp = dp + w[CONV_WIDTH - o:CONV_WIDTH - o + 1, :] * dqpad[pl.ds(t0 + o, tt), :]
            dz_ref[0, 0, pl.ds(t0, tt), :] = (dp * sig_ref[0, pl.ds(t0, tt), :]).astype(BF16)
            dz_ref[0, 1, pl.ds(t0, tt), :] = (dp * t2_ref[0, pl.ds(t0, tt), :]).astype(BF16)
            return carry

        n_iter = s_len // tt
        n_groups = min(8, n_iter)
        rows = tk // n_groups

        def conv_and_matmul(add):
            for g in range(n_groups):
                for it in range(g * n_iter // n_groups, (g + 1) * n_iter // n_groups):
                    bwd(it, 0)
                k0 = pl.multiple_of((jb % kt) * tk + g * rows, rows)
                add(_dot_tn(ycat_ref[g * rows:(g + 1) * rows, :], dy_ref[pl.ds(k0, rows), :]))
        _rs_step(jb // kt, jb % kt, kt, nj, nr, nc, conv_and_matmul, gout_ref, rs_scr)

    const2 = lambda j, k_ref: (0, 0)
    grid_spec = pltpu.PrefetchScalarGridSpec(
        num_scalar_prefetch=1,
        grid=(n_blocks,),
        in_specs=[*[pl.BlockSpec((1, s_len, 128), lambda j, k_ref: (j, 0, 0)) for _ in range(3)],
                  pl.BlockSpec((1, 32, 128), lambda j, k_ref: (j // 2, 0, j % 2)),
                  pl.BlockSpec((8, D_MODEL), const2),
                  pl.BlockSpec((D_MODEL, 128), const2),
                  pl.BlockSpec((128, 128), const2),
                  pl.BlockSpec((tk, 512), lambda j, k_ref: (j % kt, _rs_block(j // kt, k_ref[0], nj))),
                  pl.BlockSpec((s_len, D_MODEL), const2, pipeline_mode=pl.Buffered(1))],
        out_specs=(pl.BlockSpec((1, 2, s_len, 128), lambda j, k_ref: (j, 0, 0, 0)),)
        + tuple(pl.BlockSpec(s, const2) for s in shp) + (pl.BlockSpec((8, 128), const2),
                                                         pl.BlockSpec(memory_space=pl.ANY)),
        scratch_shapes=[pltpu.VMEM((s_len + 2 * CONV_PAD, 128), F32)]
        + [pltpu.VMEM(s, F32) for s in shp]
        + [pltpu.VMEM((3,) + shp[0], F32), pltpu.VMEM((3,) + shp[1], BF16), pltpu.VMEM((3,) + shp[2], F32),
           pltpu.VMEM(shp[1], BF16)]
        + [pltpu.SemaphoreType.DMA((9,)), pltpu.SemaphoreType.DMA((9,))] + _rs_scratch(nj, nr, nc))
    return pl.pallas_call(
        body, name="conv_bwd",
        grid_spec=grid_spec,
        out_shape=(jax.ShapeDtypeStruct((n_blocks, 2, s_len, 128), BF16),)
        + tuple(jax.ShapeDtypeStruct(s, F32) for s in shp) + (jax.ShapeDtypeStruct((8, 128), F32),
                                                              jax.ShapeDtypeStruct((nj, 2, nr, nc), F32)),
        compiler_params=_params(dimension_semantics=("arbitrary",)),
    )(kidx, dq, sig, t2, cw4, sm1, dws, dbs, ycat, dy)


def _bwd_in(dz4, dzc, w12, x, dx2, mod, norm_g, dq, p):
    s_len = x.shape[0]
    tm = TOKEN_TILE
    n_steps = s_len // tm
    tt = TIME_TILE
    n_cblocks = D_MODEL // 128
    parts = max(1, n_steps // n_cblocks)
    nsub = max(1, n_cblocks // n_steps)
    cw = 128 * nsub
    tiles = (s_len // tt) // parts

    def body(dz_ref, dzc_ref, w_ref, x_ref, dx2_ref, mod_ref, g_ref, dq_ref, p_ref, gx_ref, sm_ref, dcw_ref,
             dh_scr, acc_scr, ppad, wacc):
        i = pl.program_id(0)
        part = i % parts

        @pl.when(i == 0)
        def _():
            acc_scr[...] = jnp.zeros_like(acc_scr)

        @pl.when(part == 0)
        def _():
            zero = jnp.zeros((CONV_PAD, 128), F32)
            for sub in range(nsub):
                ppad[sub, 0:CONV_PAD, :] = zero
                ppad[sub, s_len + CONV_PAD:s_len + 2 * CONV_PAD, :] = zero
                ppad[sub, CONV_PAD:s_len + CONV_PAD, :] = p_ref[sub]
            wacc[...] = jnp.zeros_like(wacc)

        def dw_tile(tile):
            t0 = pl.multiple_of((part * tiles + tile) * tt, tt)
            for sub in range(nsub):
                dqt = dq_ref[sub, pl.ds(t0, tt), :]
                for o in range(1, CONV_WIDTH + 1):
                    wacc[sub, o - 1] += _colsum8(dqt * ppad[sub, pl.ds(t0 + o, tt), :])
                wacc[sub, CONV_WIDTH] += _colsum8(dqt)

        def dz_cols(j, hf):
            if j < 4:
                return dz_ref[j, :, hf * 512:(hf + 1) * 512]
            return jnp.concatenate([dzc_ref[4 * hf + blk, j - 4] for blk in range(4)], axis=1)

        dots = [(j, hf) for j in range(N_GROUPS) for hf in range(2)]
        dh = jnp.zeros((tm, D_MODEL), F32)
        for d, (j, hf) in enumerate(dots):
            dh = dh + _dot_nt(dz_cols(j, hf), w_ref[2 * _natural_group(j) + hf])
            for tile in range(d * tiles // len(dots), (d + 1) * tiles // len(dots)):
                dw_tile(tile)
        dh_scr[...] = dh

        @pl.when(part == parts - 1)
        def _():
            for sub in range(nsub):
                for k in range(32):
                    dcw_ref[k:k + 1, sub * 128:(sub + 1) * 128] = jnp.sum(wacc[sub, k], axis=0, keepdims=True)

        scale1 = 1.0 + mod_ref[:, D_MODEL:2 * D_MODEL]
        g = g_ref[...]

        def rows_fn(rows):
            xt = x_ref[rows, :]
            r = lax.rsqrt(_mean(xt * xt) + EPS)
            xn = xt * r
            dhr = dh_scr[rows, :]
            acc_scr[0] += _colsum8(dhr)
            acc_scr[1] += _colsum8(dhr * (xn * g))
            acc_scr[2] += _colsum8(dhr * scale1 * xn)
            dxn = dhr * (g * scale1)
            gx_ref[rows, :] = dx2_ref[rows, :] + r * (dxn - xn * _mean(dxn * xn))
        _row_loop(tm, rows_fn, unroll=FULL_UNROLL)

        @pl.when(i == n_steps - 1)
        def _():
            for qi in range(8):
                sm_ref[qi:qi + 1, :] = jnp.sum(acc_scr[qi], axis=0, keepdims=True)

    tok = lambda i: (i, 0)
    const2 = lambda i: (0, 0)
    return pl.pallas_call(
        body, name="bwd_in",
        grid=(n_steps,),
        out_shape=(jax.ShapeDtypeStruct((s_len, D_MODEL), F32), jax.ShapeDtypeStruct((8, D_MODEL), F32),
                   jax.ShapeDtypeStruct((32, D_MODEL), F32)),
        in_specs=[pl.BlockSpec((4, tm, D_MODEL), lambda i: (0, i, 0)),
                  pl.BlockSpec((n_cblocks, 2, tm, 128), lambda i: (0, 0, i, 0)),
                  pl.BlockSpec((12, D_MODEL, 512), lambda i: (0, 0, 0), pipeline_mode=pl.Buffered(1)),
                  pl.BlockSpec((tm, D_MODEL), tok),
                  pl.BlockSpec((tm, D_MODEL), tok),
                  pl.BlockSpec((1, 3 * D_MODEL), const2),
                  pl.BlockSpec((1, D_MODEL), const2),
                  pl.BlockSpec((nsub, s_len, 128), lambda i: (i // parts, 0, 0)),
                  pl.BlockSpec((nsub, s_len, 128), lambda i: (i // parts, 0, 0))],
        out_specs=(pl.BlockSpec((tm, D_MODEL), tok), pl.BlockSpec((8, D_MODEL), const2),
                   pl.BlockSpec((32, cw), lambda i: (0, i // parts))),
        scratch_shapes=[pltpu.VMEM((tm, D_MODEL), F32), pltpu.VMEM((8, 8, D_MODEL), F32),
                        pltpu.VMEM((nsub, s_len + 2 * CONV_PAD, 128), F32), pltpu.VMEM((nsub, 32, 8, 128), F32)],
        compiler_params=_params(dimension_semantics=("arbitrary",)),
    )(dz4, dzc, w12, x, dx2, mod, norm_g, dq, p)


def _rs_shard_block(i, nj):
    g = jnp.where(i < 2 * nj, i % 2, jnp.where(i < 3 * nj, 2, 3))
    j = jnp.where(i < 2 * nj, i // 2, jnp.where(i < 3 * nj, i - 2 * nj, i - 3 * nj))
    return g, j


def _rs_block(i, k, nj):
    g, j = _rs_shard_block(i, nj)
    return nj * jnp.bitwise_xor(k, 3 - g) + j


def _rs_scratch(nj, nr, nc):
    nblk = N_CHIPS * nj
    return [pltpu.VMEM((2, 2, nr, nc), F32),
            pltpu.VMEM((nblk, nr, nc), F32),
            pltpu.VMEM((2 * nj, nr, nc), BF16),
            pltpu.VMEM((2 * nj, nr, nc), BF16),
            pltpu.VMEM((nj, nr, nc), BF16),
            pltpu.VMEM((nj, nr, nc), BF16),
            pltpu.VMEM((nj, nr, nc), F32),
            pltpu.SemaphoreType.DMA,
            pltpu.SemaphoreType.DMA((nblk,)), pltpu.SemaphoreType.DMA((nblk,)),
            pltpu.SemaphoreType.DMA((2 * nj,)), pltpu.SemaphoreType.DMA((2 * nj,)),
            pltpu.SemaphoreType.DMA((nj,)), pltpu.SemaphoreType.DMA((nj,)),
            pltpu.SemaphoreType.DMA, pltpu.SemaphoreType.DMA]


def _rs_step(i, kk, kt, nj, nr, nc, partial_fn, out_ref, scr):
    nblk = N_CHIPS * nj
    (acc, recv_a, send1, recv1, send2, recv2, own_ps,
     lsem, psend, precv, s1send, s1recv, s2send, s2recv, fsend, frecv) = scr
    x, y, c = _pos()
    sib = (x, y, 1 - c)
    first = (jnp.bitwise_xor(x, 1 - c), jnp.bitwise_xor(y, c), c)
    second = (jnp.bitwise_xor(x, c), jnp.bitwise_xor(y, 1 - c), c)
    slot = i % 2

    @pl.when(kk == 0)
    def _():
        acc[slot] = jnp.zeros((2, nr, nc), F32)

    def pair_copy(ii, sl):
        return _rcopy(acc.at[sl, 1 - c], recv_a.at[ii], psend.at[ii], precv.at[ii], sib)

    def pair_sum(ii, sl):
        cp = pair_copy(ii, sl)
        cp.wait_recv()
        cp.wait_send()
        return acc[sl, c] + recv_a[ii]

    def stage1_copy(src, dst):
        return _rcopy(send1.at[src], recv1.at[dst], s1send.at[src], s1recv.at[dst], first)

    def stage2_copy(j):
        return _rcopy(send2.at[j], recv2.at[j], s2send.at[j], s2recv.at[j], second)

    def finalize(ii, sl):
        ps = pair_sum(ii, sl)
        g, j = _rs_shard_block(ii, nj)

        @pl.when(g == 0)
        def _():
            send1[j] = ps.astype(BF16)
            stage1_copy(j, nj + j).start()

        @pl.when(g == 1 + c)
        def _():
            send1[nj + j] = ps.astype(BF16)
            stage1_copy(nj + j, j).start()

        @pl.when(g == 2 - c)
        def _():
            stage1_copy(0, nj + j).wait_recv()
            send2[j] = (ps + recv1[nj + j].astype(F32)).astype(BF16)
            stage2_copy(j).start()

        @pl.when(g == 3)
        def _():
            own_ps[j] = ps

    @pl.when(jnp.logical_and(kk == kt - 1, i >= 1))
    def _():
        finalize(i - 1, 1 - slot)

    def add_partial(res):
        acc[slot, 0] += res[0:nr]
        acc[slot, 1] += res[nr:2 * nr]
    partial_fn(add_partial)

    @pl.when(kk == kt - 1)
    def _():
        pair_copy(i, slot).start()

        @pl.when(i == nblk - 1)
        def _():
            own_ps[nj - 1] = pair_sum(nblk - 1, (nblk - 1) % 2)
            for j in range(nj):
                stage1_copy(0, j).wait_recv()
                stage2_copy(j).wait_recv()
                own_ps[j] = (own_ps[j] + recv1[j].astype(F32)) + recv2[j].astype(F32)
            loc = pltpu.make_async_copy(own_ps, out_ref.at[:, c], lsem)
            loc.start()
            swap = _rcopy(own_ps, out_ref.at[:, c], fsend, frecv, sib)
            swap.start()
            loc.wait()
            swap.wait_send()
            _rcopy(own_ps, out_ref.at[:, 1 - c], fsend, frecv, sib).wait_recv()
            for s in range(2 * nj):
                stage1_copy(s, 0).wait_send()
            for j in range(nj):
                stage2_copy(j).wait_send()


def _grad_w_in(kidx, h, dz4, dzc, sm2, dcw):
    s_len = h.shape[0]
    tk = min(K_TILE, s_len)
    kt = s_len // tk
    nj, nr, nc = 3, 512, 512
    nblk = N_CHIPS * nj

    def body(k_ref, a_ref, b_ref, bc_ref, sm2_ref, dcw_in, out_ref, pk2_ref, dcwsh_ref, dcb_ref, *scr):
        del k_ref
        b_pk2, b_dcw, r_pk2, r_dcw, bsend, brecv = scr[-6:]
        i = pl.program_id(0)
        kk = pl.program_id(1)
        x, y, c = _pos()
        k = 2 * x + y
        m = _rs_block(i, k, nj)

        @pl.when(jnp.logical_and(kk == 0, i == 0))
        def _():
            b = 4 * x + 2 * y + c
            b_pk2[0:8, :] = _onehot_rows(sm2_ref[0:1, :], b)
            b_pk2[8:16, :] = _onehot_rows(sm2_ref[1:2, :], b)
            b_pk2[16:17, :] = sm2_ref[2:3, :]
            b_pk2[17:24, :] = jnp.zeros((7, D_MODEL), F32)
            b_dcw[...] = dcw_in[...]

        for e in range(4):
            @pl.when(jnp.logical_and(kk == 0, i == (e * nblk) // 4))
            def _():
                _butterfly_event(e, [b_pk2, b_dcw], [r_pk2, r_dcw], bsend, brecv, _all_reduce_partners())

        @pl.when(jnp.logical_and(kk == 0, i == (3 * nblk) // 4))
        def _():
            pk2_ref[...] = b_pk2[...]
            sel = jnp.zeros((32, 256), F32)
            for kc in range(N_CHIPS):
                sel = jnp.where(k == kc, b_dcw[:, kc * 256:(kc + 1) * 256], sel)
            dcwsh_ref[...] = sel
            dcb_ref[...] = b_dcw[31:32, :]

        def partial(add):
            rows = pl.ds(pl.multiple_of(kk * tk, tk), tk)

            @pl.when(m >= 4)
            def _():
                add(_dot_tn(a_ref[rows, :], b_ref[0]))

            @pl.when(m < 4)
            def _():
                add(_dot_tn(a_ref[rows, :], jnp.concatenate([bc_ref[blk, 0] for blk in range(4)], axis=1)))
        _rs_step(i, kk, kt, nj, nr, nc, partial, out_ref, scr[:-6])

    def dz4_index(i, kk, k_ref):
        m = _rs_block(i, k_ref[0], nj)
        use = m >= 4
        return (jnp.where(use, m // 2 - 2, 0), jnp.where(use, kk, 0), jnp.where(use, m % 2, 0))

    def dzc_index(i, kk, k_ref):
        m = _rs_block(i, k_ref[0], nj)
        use = m < 4
        return (jnp.where(use, m % 2, 0), jnp.where(use, m // 2, 0), jnp.where(use, kk, 0), 0)

    const2 = lambda i, kk, k_ref: (0, 0)
    small_out = [(24, D_MODEL), (32, 256), (1, D_MODEL)]
    small_buf = [(24, D_MODEL), (32, D_MODEL)]
    grid_spec = pltpu.PrefetchScalarGridSpec(
        num_scalar_prefetch=1,
        grid=(nblk, kt),
        in_specs=[pl.BlockSpec((s_len, D_MODEL), const2, pipeline_mode=pl.Buffered(1)),
                  pl.BlockSpec((1, tk, 512), dz4_index),
                  pl.BlockSpec((4, 1, tk, 128), dzc_index),
                  pl.BlockSpec((8, D_MODEL), const2),
                  pl.BlockSpec((32, D_MODEL), const2)],
        out_specs=(pl.BlockSpec(memory_space=pl.ANY),) + tuple(pl.BlockSpec(s, const2) for s in small_out),
        scratch_shapes=_rs_scratch(nj, nr, nc)
        + [pltpu.VMEM(s, F32) for s in small_buf] + [pltpu.VMEM((3,) + s, F32) for s in small_buf]
        + [pltpu.SemaphoreType.DMA((6,)), pltpu.SemaphoreType.DMA((6,))])
    return pl.pallas_call(
        body, name="grad_w_in",
        grid_spec=grid_spec,
        out_shape=(jax.ShapeDtypeStruct((nj, 2, nr, nc), F32),) + tuple(jax.ShapeDtypeStruct(s, F32) for s in small_out),
        compiler_params=_params(dimension_semantics=("arbitrary", "arbitrary")),
    )(kidx, h, dz4, dzc, sm2, dcw)


def _adamw_math(w, g, m, v):
    m = ADAM_B1 * m + (1.0 - ADAM_B1) * g
    v = ADAM_B2 * v + (1.0 - ADAM_B2) * (g * g)
    m_hat = m / (1.0 - ADAM_B1 ** ADAM_STEP)
    v_hat = v / (1.0 - ADAM_B2 ** ADAM_STEP)
    delta = -ADAM_LR * (m_hat / (jnp.sqrt(v_hat) + ADAM_EPS) + ADAM_WD * w)
    return delta, m, v


def _adamw_blocked(w, m, v, g4, name):
    nj, _, nr, nc = g4.shape

    def body(w_ref, m_ref, v_ref, g_ref, go_ref, d_ref, mo_ref, vo_ref):
        g = g_ref[0, 0]
        d, mn, vn = _adamw_math(w_ref[...], g, m_ref[...], v_ref[...])
        go_ref[...] = g
        d_ref[...] = d
        mo_ref[...] = mn
        vo_ref[...] = vn

    blk = pl.BlockSpec((nr, nc), lambda j, hf: (hf, j))
    return pl.pallas_call(
        body, name=name,
        grid=(nj, 2),
        out_shape=tuple(jax.ShapeDtypeStruct(w.shape, F32) for _ in range(4)),
        in_specs=[blk, blk, blk, pl.BlockSpec((1, 1, nr, nc), lambda j, hf: (j, hf, 0, 0))],
        out_specs=(blk, blk, blk, blk),
        compiler_params=_params(dimension_semantics=("arbitrary", "arbitrary")),
    )(w, m, v, g4)


def _adamw_w_ada(cact, pk1, pk2, w, m, v):
    rb = 256

    def body(cact_ref, pk1_ref, pk2_ref, w_ref, m_ref, v_ref, g_ref, d_ref, mo_ref, vo_ref, gb_ref, dmod, sel_scr):
        @pl.when(pl.program_id(0) == 0)
        def _():
            x, y, _ = _pos()
            k = 2 * x + y
            dmod[:, 0:D_MODEL] = pk2_ref[0:8, :]
            dmod[:, D_MODEL:2 * D_MODEL] = pk2_ref[8:16, :]
            dmod[:, 2 * D_MODEL:3 * D_MODEL] = pk1_ref[0:8, :]
            gb_ref[...] = jnp.sum(dmod[...], axis=0, keepdims=True)
            sel = jnp.zeros((8, 768), F32)
            for kk in range(N_CHIPS):
                sel = jnp.where(k == kk, dmod[:, kk * 768:(kk + 1) * 768], sel)
            sel_scr[...] = sel

        g = _dot_tn(cact_ref[...].astype(BF16), sel_scr[...].astype(BF16))
        d, mn, vn = _adamw_math(w_ref[...], g, m_ref[...], v_ref[...])
        g_ref[...] = g
        d_ref[...] = d
        mo_ref[...] = mn
        vo_ref[...] = vn

    blk = pl.BlockSpec((rb, 768), lambda i: (i, 0))
    const2 = lambda i: (0, 0)
    return pl.pallas_call(
        body, name="adamw_w_ada",
        grid=(D_MODEL // rb,),
        out_shape=tuple(jax.ShapeDtypeStruct(w.shape, F32) for _ in range(4)) + (
            jax.ShapeDtypeStruct((1, 3 * D_MODEL), F32),),
        in_specs=[pl.BlockSpec((8, rb), lambda i: (0, i)), pl.BlockSpec((16, D_MODEL), const2),
                  pl.BlockSpec((24, D_MODEL), const2), blk, blk, blk],
        out_specs=(blk, blk, blk, blk, pl.BlockSpec((1, 3 * D_MODEL), const2)),
        scratch_shapes=[pltpu.VMEM((8, 3 * D_MODEL), F32), pltpu.VMEM((8, 768), F32)],
        compiler_params=_params(dimension_semantics=("arbitrary",)),
    )(cact, pk1, pk2, w, m, v)


def _adamw_small(items, conv):
    n = len(items)

    def body(*refs):
        ins, cw_ref, cg_ref, cm_ref, cv_ref = refs[:4 * n], *refs[4 * n:4 * n + 4]
        outs, (cgo_ref, cd_ref, cmo_ref, cvo_ref) = refs[4 * n + 4:7 * n + 4], refs[7 * n + 4:]
        for i in range(n):
            w_ref, g_ref, m_ref, v_ref = ins[4 * i:4 * i + 4]
            d, mn, vn = _adamw_math(w_ref[...], g_ref[...], m_ref[...], v_ref[...])
            outs[3 * i][...] = d
            outs[3 * i + 1][...] = mn
            outs[3 * i + 2][...] = vn
        for r in range(CONV_WIDTH):
            g = cg_ref[r:r + 1, :]
            d, mn, vn = _adamw_math(cw_ref[0, r], g, cm_ref[0, r], cv_ref[0, r])
            cgo_ref[0, r] = g
            cd_ref[0, r] = d
            cmo_ref[0, r] = mn
            cvo_ref[0, r] = vn

    flat = [a for it in items for a in it] + list(conv)
    outs = pl.pallas_call(
        body, name="adamw_small",
        out_shape=tuple(jax.ShapeDtypeStruct(it[0].shape, F32) for it in items for _ in range(3))
        + tuple(jax.ShapeDtypeStruct(conv[0].shape, F32) for _ in range(4)),
        in_specs=[_vmem()] * (4 * n + 4),
        out_specs=tuple(_vmem() for _ in range(3 * n + 4)),
        compiler_params=_params(),
    )(*flat)
    return [tuple(outs[3 * i:3 * i + 3]) for i in range(n)], tuple(outs[3 * n:])


def kernel(x, c, w_ada, b_ada, norm_g, w_in, conv_w, conv_b, conv_ln_g, conv_ln_b, sg_ln_g, sg_ln_b, w_s, b_s, w_out, final_g, loss_target, m_w_ada, m_b_ada, m_norm_g, m_w_in, m_conv_w, m_conv_b, m_conv_ln_g, m_conv_ln_b, m_sg_ln_g, m_sg_ln_b, m_w_s, m_b_s, m_w_out, m_final_g, v_w_ada, v_b_ada, v_norm_g, v_w_in, v_conv_w, v_conv_b, v_conv_ln_g, v_conv_ln_b, v_sg_ln_g, v_sg_ln_b, v_w_s, v_b_s, v_w_out, v_final_g):
    s_len = x.shape[1]
    x2d = x[0]
    tgt = loss_target[0]
    row = lambda a: a.reshape(1, -1)

    kidx = (2 * lax.axis_index("x") + lax.axis_index("y")).astype(jnp.int32).reshape(1)
    cw_sh = jnp.pad(conv_w.reshape(CONV_WIDTH, 256), ((0, 1), (0, 0)))
    h, z6, mod, cact, cw4, w_in12 = _front(kidx, x2d, c, w_ada[0], b_ada, norm_g, w_in[0], cw_sh)
    w12 = w_in12.reshape(12, D_MODEL, 512)
    q, p_cm, sig_cm, t2_cm, w_out4 = _conv_fwd(z6, cw4, conv_b, w_out[0])
    w_out_full = w_out4.reshape(2 * D_MODEL, D_MODEL)
    ln_rows = (conv_ln_g, conv_ln_b, sg_ln_g, sg_ln_b, row(final_g))
    bs_exp = jnp.repeat(b_s[0].T, CHUNK, axis=1)
    dz4, dq, dx2, ycat, dy, dws, sm1, dbs = _middle(x2d, z6, q, tgt, ln_rows, mod, w_s[0], bs_exp, w_out_full)
    dzc, pk1, dws_r, dbs_r, loss_t, g_w_out4 = _conv_bwd(
        kidx, dq, sig_cm, t2_cm, cw4, sm1, dws.reshape(D_MODEL, CHUNK), dbs, ycat, dy)
    grad_x, sm2, dcw = _bwd_in(dz4, dzc, w12, x2d, dx2, mod, norm_g, dq, p_cm)
    g_w_in4, pk2, dcw_sh, dcb = _grad_w_in(kidx, h, dz4, dzc, sm2, dcw)

    g_w_in, d_w_in, nm_w_in, nv_w_in = _adamw_blocked(w_in[0], m_w_in[0], v_w_in[0], g_w_in4, "adamw_w_in")
    g_w_out, d_w_out, nm_w_out, nv_w_out = _adamw_blocked(w_out[0], m_w_out[0], v_w_out[0], g_w_out4, "adamw_w_out")
    g_w_ada, d_w_ada, nm_w_ada, nv_w_ada, g_b_ada = _adamw_w_ada(cact, pk1, pk2, w_ada[0], m_w_ada[0], v_w_ada[0])

    g_norm_g = pk2[16:17]
    g_cln_g, g_cln_b, g_sln_g, g_sln_b, g_final = (pk1[8 + i:9 + i] for i in range(5))
    loss = loss_t[0, 0]
    g_w_s = dws_r
    g_b_s = dbs_r[:, :HEADS].T
    small = [
        (b_ada, g_b_ada, m_b_ada, v_b_ada),
        (norm_g, g_norm_g, m_norm_g, v_norm_g),
        (conv_b, dcb, m_conv_b, v_conv_b),
        (conv_ln_g, g_cln_g, m_conv_ln_g, v_conv_ln_g),
        (conv_ln_b, g_cln_b, m_conv_ln_b, v_conv_ln_b),
        (sg_ln_g, g_sln_g, m_sg_ln_g, v_sg_ln_g),
        (sg_ln_b, g_sln_b, m_sg_ln_b, v_sg_ln_b),
        (w_s.reshape(D_MODEL, CHUNK), g_w_s, m_w_s.reshape(D_MODEL, CHUNK), v_w_s.reshape(D_MODEL, CHUNK)),
        (b_s[0], g_b_s, m_b_s[0], v_b_s[0]),
        (row(final_g), g_final, row(m_final_g), row(v_final_g)),
    ]
    upd, (g_conv_w, d_conv_w, nm_conv_w, nv_conv_w) = _adamw_small(small, (conv_w, dcw_sh, m_conv_w, v_conv_w))

    shapes = [w_ada.shape, b_ada.shape, norm_g.shape, w_in.shape, conv_w.shape, conv_b.shape, conv_ln_g.shape,
              conv_ln_b.shape, sg_ln_g.shape, sg_ln_b.shape, w_s.shape, b_s.shape, w_out.shape, final_g.shape]
    grads = [g_w_ada, g_b_ada, g_norm_g, g_w_in, g_conv_w, dcb, g_cln_g, g_cln_b, g_sln_g, g_sln_b, g_w_s, g_b_s,
             g_w_out, g_final]
    big = {0: (d_w_ada, nm_w_ada, nv_w_ada), 3: (d_w_in, nm_w_in, nv_w_in), 4: (d_conv_w, nm_conv_w, nv_conv_w),
           12: (d_w_out, nm_w_out, nv_w_out)}
    small_pos = [1, 2, 5, 6, 7, 8, 9, 10, 11, 13]
    trip = [None] * 14
    for i, t in big.items():
        trip[i] = t
    for i, t in zip(small_pos, upd):
        trip[i] = t
    fit = lambda arrs: [a.reshape(s) for a, s in zip(arrs, shapes)]
    return (loss, grad_x.reshape(x.shape), *fit(grads), *fit([t[0] for t in trip]), *fit([t[1] for t in trip]),
            *fit([t[2] for t in trip]))
```

```python
import jax
import jax.numpy as jnp
from jax import lax
from jax.experimental import pallas as pl
from jax.experimental.pallas import tpu as pltpu

F32 = jnp.float32
BF16 = jnp.bfloat16
MESH = pl.DeviceIdType.MESH

D_MODEL = 1024
N_CHIPS = 4
HEADS = 8
CHUNK = 128
CONV_WIDTH = 31
CONV_HALF = CONV_WIDTH // 2
CONV_PAD = 16
EPS = 1e-6
ADAM_LR = 0.001
ADAM_B1 = 0.9
ADAM_B2 = 0.999
ADAM_EPS = 1e-08
ADAM_WD = 0.01
ADAM_STEP = 10

V7X_VMEM_BYTES = 64 * 1024 * 1024
VMEM_LIMIT = V7X_VMEM_BYTES - 8 * 1024 * 1024
ROWS = 16
UNROLL = 8
TOKEN_TILE = 256
FULL_UNROLL = TOKEN_TILE // ROWS
TIME_TILE = 128
K_TILE = 2048

N_GROUPS = 6


def _natural_group(j):
    return (j + 2) % N_GROUPS


def _pos():
    return lax.axis_index("x"), lax.axis_index("y"), lax.axis_index("c")


def _rcopy(src, dst, ssem, rsem, dev):
    return pltpu.make_async_remote_copy(src_ref=src, dst_ref=dst, send_sem=ssem, recv_sem=rsem,
                                        device_id=dev, device_id_type=MESH)


def _vmem():
    return pl.BlockSpec(memory_space=pltpu.VMEM)


def _params(**kw):
    return pltpu.CompilerParams(vmem_limit_bytes=VMEM_LIMIT, **kw)


def _sigmoid(v):
    return 0.5 * jnp.tanh(0.5 * v) + 0.5


def _row_loop(n_rows, body, unroll=1):
    def step(r, carry):
        body(pl.ds(pl.multiple_of(r * ROWS, ROWS), ROWS))
        return carry
    lax.fori_loop(0, n_rows // ROWS, step, 0, unroll=unroll)


def _colsum8(v):
    return v.reshape(v.shape[0] // 8, 8, v.shape[1]).sum(axis=0)


def _mean(v):
    return jnp.mean(v, axis=-1, keepdims=True)


def _dot_nn(a, b):
    return jnp.dot(a, b, preferred_element_type=F32)


def _dot_nt(a, b):
    return lax.dot_general(a, b, (((1,), (1,)), ((), ())), preferred_element_type=F32)


def _dot_tn(a, b):
    return lax.dot_general(a, b, (((0,), (0,)), ((), ())), preferred_element_type=F32)


def _remote_chip(k, r):
    return jnp.bitwise_xor(k, r + 1)


def _front(kidx, x, c, w_ada, b_ada, norm_g, w_in, cw):
    s_len = x.shape[0]
    tmh = min(512, s_len)
    tmz = min(2048, s_len)
    nh = s_len // tmh
    nz = s_len // tmz
    n_steps = nh + 12 * nz

    def remote_block(q):
        return jnp.where(q < 6, q % 2, 2), jnp.where(q < 6, q // 2, q - 6)

    def block_of(i, k):
        r, j = remote_block(jnp.maximum(i - 3, 0))
        return jnp.where(i < 3, 3 * k + i, 3 * _remote_chip(k, r) + j)

    def body(k_ref, x_ref, c_ref, bada_ref, g_ref, cw_ref, wada_hbm, win_hbm,
             h_ref, z_ref, mod_ref, cact_ref, cw4_ref, w12_hbm,
             h_all, wbuf, stage, wada_v, cslab, mslab, cw4_s,
             lsem, csend, crecv, msend, mrecv, wsend, wrecv, isend, irecv, fsend, frecv, osem):
        del k_ref
        t = pl.program_id(0)
        x_, y_, c_ = _pos()
        k = 2 * x_ + y_
        b = 4 * x_ + 2 * y_ + c_
        sib = (x_, y_, 1 - c_)

        def dev_of(r):
            kk = _remote_chip(k, r)
            return (kk // 2, kk % 2, c_)

        def ici(q, kk):
            r, j = remote_block(q)
            return _rcopy(wbuf.at[3 * kk + j, c_], wbuf.at[3 * kk + j, c_], isend.at[q], irecv.at[q], dev_of(r))

        def fwd(q, hf):
            r, j = remote_block(q)
            blk = 3 * _remote_chip(k, r) + j
            return _rcopy(wbuf.at[blk, hf], wbuf.at[blk, hf], fsend.at[q], frecv.at[q], sib)

        def c_copy(q, src):
            d = jnp.bitwise_xor(b, q)
            return _rcopy(cslab.at[src], cslab.at[src], csend.at[q - 1], crecv.at[q - 1], (d // 4, (d // 2) % 2, d % 2))

        def m_copy(r, kk):
            return _rcopy(mslab.at[kk], mslab.at[kk], msend.at[r], mrecv.at[r], dev_of(r))

        def cw_copy(r, kk):
            return _rcopy(cw4_s.at[kk], cw4_s.at[kk], wsend.at[r], wrecv.at[r], dev_of(r))

        def to_hbm(i):
            m = block_of(i, k)
            return pltpu.make_async_copy(wbuf.at[m], w12_hbm.at[m], osem.at[i])

        @pl.when(t == 0)
        def _():
            ld_w = pltpu.make_async_copy(win_hbm, stage, lsem.at[0])
            ld_w.start()
            ld_a = pltpu.make_async_copy(wada_hbm, wada_v, lsem.at[1])
            ld_a.start()
            cslab[b] = jnp.broadcast_to(c_ref[...], (8, D_MODEL))
            for q in range(1, 8):
                c_copy(q, b).start()
            cw4_s[k] = cw_ref[...]
            for r in range(3):
                cw_copy(r, k).start()
            ld_w.wait()
            for j in range(3):
                for hf in range(2):
                    wbuf[3 * k + j, hf] = stage[hf * 512:(hf + 1) * 512, j * 512:(j + 1) * 512].astype(BF16)
            for i in range(3):
                to_hbm(i).start()
            for q in range(1, 8):
                c_copy(q, jnp.bitwise_xor(b, q)).wait_recv()
            row = lax.broadcasted_iota(jnp.int32, (8, D_MODEL), 0)
            call = jnp.zeros((8, D_MODEL), F32)
            for d in range(8):
                call = jnp.where(row == d, cslab[d], call)
            cact = call * _sigmoid(call)
            cact_ref[...] = cact
            ld_a.wait()
            mslab[k] = _dot_nn(cact.astype(BF16), wada_v[...].astype(BF16))
            for r in range(3):
                m_copy(r, k).start()
            for q in range(9):
                ici(q, k).start()
            for r in range(3):
                m_copy(r, _remote_chip(k, r)).wait_recv()
            row8 = lax.broadcasted_iota(jnp.int32, (8, 768), 0)
            for kk in range(N_CHIPS):
                piece = jnp.sum(jnp.where(row8 == b, mslab[kk], 0.0), axis=0, keepdims=True)
                mod_ref[:, kk * 768:(kk + 1) * 768] = piece + bada_ref[:, kk * 768:(kk + 1) * 768]
            for r in range(3):
                cw_copy(r, _remote_chip(k, r)).wait_recv()
            cw4_ref[...] = cw4_s[...]

        @pl.when(t < nh)
        def _():
            shift = mod_ref[:, 0:D_MODEL]
            scale1 = 1.0 + mod_ref[:, D_MODEL:2 * D_MODEL]
            g = g_ref[...]
            base = t * tmh

            def rows_fn(rows):
                xt = x_ref[rows, :]
                r = lax.rsqrt(_mean(xt * xt) + EPS)
                hv = ((xt * r * g) * scale1 + shift).astype(BF16)
                h_ref[rows, :] = hv
                h_all[pl.ds(pl.multiple_of(base + rows.start, ROWS), ROWS), :] = hv
            _row_loop(tmh, rows_fn, unroll=UNROLL)

        @pl.when(t >= nh)
        def _():
            u = t - nh
            i = u // nz
            rt = u % nz
            @pl.when(jnp.logical_and(rt == 0, jnp.logical_and(i >= 2, i <= 10)))
            def _():
                q = i - 2
                r, _ = remote_block(q)
                ici(q, _remote_chip(k, r)).wait_recv()
                fwd(q, c_).start()

            @pl.when(jnp.logical_and(rt == 0, i >= 3))
            def _():
                fwd(i - 3, 1 - c_).wait_recv()
                to_hbm(i).start()
            m = block_of(i, k)
            hb = h_all[pl.ds(pl.multiple_of(rt * tmz, tmz), tmz), :]
            z_ref[0] = _dot_nn(hb, wbuf[m].reshape(D_MODEL, 512))

        @pl.when(t == n_steps - 1)
        def _():
            for q in range(1, 8):
                c_copy(q, b).wait_send()
            for r in range(3):
                m_copy(r, k).wait_send()
                cw_copy(r, k).wait_send()
            for q in range(9):
                ici(q, k).wait_send()
                fwd(q, c_).wait_send()
            for i in range(12):
                to_hbm(i).wait()

    def z_index(t, k_ref):
        u = jnp.maximum(t - nh, 0)
        m = block_of(u // nz, k_ref[0])
        return ((m // 2 + 4) % N_GROUPS, u % nz, m % 2)

    tok = lambda t, k_ref: (jnp.minimum(t, nh - 1), 0)
    const2 = lambda t, k_ref: (0, 0)
    grid_spec = pltpu.PrefetchScalarGridSpec(
        num_scalar_prefetch=1,
        grid=(n_steps,),
        in_specs=[pl.BlockSpec((tmh, D_MODEL), tok),
                  pl.BlockSpec((1, D_MODEL), const2),
                  pl.BlockSpec((1, 3 * D_MODEL), const2),
                  pl.BlockSpec((1, D_MODEL), const2),
                  pl.BlockSpec((32, 256), const2),
                  pl.BlockSpec(memory_space=pl.ANY),
                  pl.BlockSpec(memory_space=pl.ANY)],
        out_specs=(pl.BlockSpec((tmh, D_MODEL), tok),
                   pl.BlockSpec((1, tmz, 512), z_index),
                   pl.BlockSpec((1, 3 * D_MODEL), const2),
                   pl.BlockSpec((8, D_MODEL), const2),
                   pl.BlockSpec((N_CHIPS, 32, 256), lambda t, k_ref: (0, 0, 0)),
                   pl.BlockSpec(memory_space=pl.ANY)),
        scratch_shapes=[pltpu.VMEM((s_len, D_MODEL), BF16),
                        pltpu.VMEM((12, 2, 512, 512), BF16),
                        pltpu.VMEM((D_MODEL, 1536), F32),
                        pltpu.VMEM((D_MODEL, 768), F32),
                        pltpu.VMEM((8, 8, D_MODEL), F32),
                        pltpu.VMEM((N_CHIPS, 8, 768), F32),
                        pltpu.VMEM((N_CHIPS, 32, 256), F32),
                        pltpu.SemaphoreType.DMA((2,)),
                        pltpu.SemaphoreType.DMA((7,)), pltpu.SemaphoreType.DMA((7,)),
                        pltpu.SemaphoreType.DMA((3,)), pltpu.SemaphoreType.DMA((3,)),
                        pltpu.SemaphoreType.DMA((3,)), pltpu.SemaphoreType.DMA((3,)),
                        pltpu.SemaphoreType.DMA((9,)), pltpu.SemaphoreType.DMA((9,)),
                        pltpu.SemaphoreType.DMA((9,)), pltpu.SemaphoreType.DMA((9,)),
                        pltpu.SemaphoreType.DMA((12,))])
    return pl.pallas_call(
        body, name="front",
        grid_spec=grid_spec,
        out_shape=(jax.ShapeDtypeStruct((s_len, D_MODEL), BF16),
                   jax.ShapeDtypeStruct((N_GROUPS, s_len, D_MODEL), F32),
                   jax.ShapeDtypeStruct((1, 3 * D_MODEL), F32),
                   jax.ShapeDtypeStruct((8, D_MODEL), F32),
                   jax.ShapeDtypeStruct((N_CHIPS, 32, 256), F32),
                   jax.ShapeDtypeStruct((12, 2, 512, 512), BF16)),
        compiler_params=_params(dimension_semantics=("arbitrary",)),
    )(kidx, x, c, b_ada, norm_g, cw, w_ada, w_in)


def _all_reduce_partners():
    x, y, c = _pos()
    return [(x, y, 1 - c), (x, 1 - y, c), (1 - x, y, c)]


def _butterfly_event(e, bufs, recvs, ssem, rsem, partners, wires=None):
    n = len(bufs)
    wires = wires or [None] * n

    def copies(s):
        return [_rcopy(buf if wire is None else wire, recv.at[s], ssem.at[s * n + i], rsem.at[s * n + i], partners[s])
                for i, (buf, recv, wire) in enumerate(zip(bufs, recvs, wires))]

    if e > 0:
        for cp in copies(e - 1):
            cp.wait()
        for buf, recv, wire in zip(bufs, recvs, wires):
            mine = buf[...] if wire is None else wire[...].astype(F32)
            buf[...] = mine + recv[e - 1].astype(F32)
    if e < len(partners):
        for buf, wire in zip(bufs, wires):
            if wire is not None:
                wire[...] = buf[...].astype(BF16)
        for cp in copies(e):
            cp.start()


def _onehot_rows(v, b):
    row = lax.broadcasted_iota(jnp.int32, (8, v.shape[1]), 0)
    return jnp.where(row == b, jnp.broadcast_to(v, (8, v.shape[1])), 0.0)


def _conv_fwd(z6, cw4, conv_b, w_out):
    s_len = z6.shape[1]
    tt = TIME_TILE
    n_blocks = D_MODEL // 128

    def body(z_ref, cw_ref, cb_ref, wout_hbm, q_ref, p_ref, sig_ref, t2_ref, wout4_hbm,
             ppad, stage, wbuf, lsem, isend, irecv, fsend, frecv, osem):
        jb = pl.program_id(0)
        x_, y_, c_ = _pos()
        k = 2 * x_ + y_
        sib = (x_, y_, 1 - c_)

        def ici(r, kk):
            rk = _remote_chip(k, r)
            return _rcopy(wbuf.at[kk, c_], wbuf.at[kk, c_], isend.at[r], irecv.at[r], (rk // 2, rk % 2, c_))

        def fwd(r, hf):
            kk = _remote_chip(k, r)
            return _rcopy(wbuf.at[kk, hf], wbuf.at[kk, hf], fsend.at[r], frecv.at[r], sib)

        @pl.when(jb == 0)
        def _():
            ld = pltpu.make_async_copy(wout_hbm, stage, lsem)
            ld.start()
            ld.wait()
            for hf in range(2):
                wbuf[k, hf] = stage[hf * 256:(hf + 1) * 256, :].astype(BF16)
            for r in range(3):
                ici(r, k).start()

        @pl.when(jb == (5 * n_blocks) // 8)
        def _():
            for r in range(3):
                ici(r, _remote_chip(k, r)).wait_recv()
                fwd(r, c_).start()

        zero = jnp.zeros((CONV_PAD, 128), F32)
        ppad[0:CONV_PAD, :] = zero
        ppad[s_len + CONV_PAD:s_len + 2 * CONV_PAD, :] = zero

        def fill(i, carry):
            t0 = pl.multiple_of(i * tt, tt)
            sig = _sigmoid(z_ref[1, pl.ds(t0, tt), :])
            p = z_ref[0, pl.ds(t0, tt), :] * sig
            ppad[pl.ds(CONV_PAD + t0, tt), :] = p
            p_ref[0, pl.ds(t0, tt), :] = p
            sig_ref[0, pl.ds(t0, tt), :] = sig
            t2_ref[0, pl.ds(t0, tt), :] = p * (1.0 - sig)
            return carry
        lax.fori_loop(0, s_len // tt, fill, 0)
        w = cw_ref[0]
        bias = cb_ref[...]

        def conv(i, carry):
            t0 = pl.multiple_of(i * tt, tt)
            acc = jnp.broadcast_to(bias, (tt, 128))
            for o in range(1, CONV_WIDTH + 1):
                acc = acc + w[o - 1:o, :] * ppad[pl.ds(t0 + o, tt), :]
            q_ref[0, pl.ds(t0, tt), :] = acc
            return carry
        lax.fori_loop(0, s_len // tt, conv, 0)

        @pl.when(jb == n_blocks - 1)
        def _():
            for r in range(3):
                fwd(r, 1 - c_).wait_recv()
            out = pltpu.make_async_copy(wbuf, wout4_hbm, osem)
            out.start()
            for r in range(3):
                ici(r, k).wait_send()
                fwd(r, c_).wait_send()
            out.wait()

    return pl.pallas_call(
        body, name="conv_fwd",
        grid=(n_blocks,),
        out_shape=tuple(jax.ShapeDtypeStruct((n_blocks, s_len, 128), F32) for _ in range(4))
        + (jax.ShapeDtypeStruct((N_CHIPS, 2, 256, D_MODEL), BF16),),
        in_specs=[pl.BlockSpec((2, s_len, 128), lambda j: (2, 0, j)),
                  pl.BlockSpec((1, 32, 128), lambda j: (j // 2, 0, j % 2)),
                  pl.BlockSpec((1, 128), lambda j: (0, j)),
                  pl.BlockSpec(memory_space=pl.ANY)],
        out_specs=tuple(pl.BlockSpec((1, s_len, 128), lambda j: (j, 0, 0)) for _ in range(4))
        + (pl.BlockSpec(memory_space=pl.ANY),),
        scratch_shapes=[pltpu.VMEM((s_len + 2 * CONV_PAD, 128), F32),
                        pltpu.VMEM((512, D_MODEL), F32),
                        pltpu.VMEM((N_CHIPS, 2, 256, D_MODEL), BF16),
                        pltpu.SemaphoreType.DMA,
                        pltpu.SemaphoreType.DMA((3,)), pltpu.SemaphoreType.DMA((3,)),
                        pltpu.SemaphoreType.DMA((3,)), pltpu.SemaphoreType.DMA((3,)),
                        pltpu.SemaphoreType.DMA],
        compiler_params=_params(dimension_semantics=("arbitrary",)),
    )(z6, cw4, conv_b, w_out)


def _middle(x, z6, q, target, ln_rows, mod, w_s, bs_exp, w_out):
    s_len = x.shape[0]
    tm = TOKEN_TILE
    n_steps = s_len // tm
    n_chunks = tm // CHUNK
    inv_d = 1.0 / D_MODEL

    def body(x_ref, z_ref, q_ref, tgt_ref, cg_ref, cb_ref, sg_ref, sb_ref, fg_ref, mod_ref, ws_ref, bs_ref, wout_ref,
             dz_ref, dq_ref, dx2_ref, ycat_ref, dy_ref, dws_ref, sm_ref, dbs_ref,
             vl_scr, vm_scr, y_scr, dycat_scr, dvm_scr, dvl_scr, acc_scr, dbs_acc, keep, rstd_scr):
        i = pl.program_id(0)

        @pl.when(i == 0)
        def _():
            acc_scr[...] = jnp.zeros_like(acc_scr)
            dbs_acc[...] = jnp.zeros_like(dbs_acc)
            dws_ref[...] = jnp.zeros_like(dws_ref)

        cg, cb, sg, sb, fg = cg_ref[...], cb_ref[...], sg_ref[...], sb_ref[...], fg_ref[...]
        gm = mod_ref[:, 2 * D_MODEL:3 * D_MODEL]

        def norm_stats(t):
            c = t - _mean(t)
            rstd = lax.rsqrt(_mean(c * c) + EPS)
            return c * rstd, rstd

        def phase1(rows):
            qhat, rstd_q = norm_stats(jnp.concatenate([q_ref[blk, rows, :] for blk in range(D_MODEL // 128)], axis=1))
            ln = qhat * cg + cb
            gz = z_ref[0, rows, :]
            sig_ln = _sigmoid(ln)
            sig_g = _sigmoid(gz)
            ycat_ref[rows, 0:D_MODEL] = ((ln * sig_ln) * (gz * sig_g)).astype(BF16)
            vhat, rstd_v = norm_stats(z_ref[2, rows, :])
            vl_scr[rows, :] = (vhat * sg + sb).astype(BF16)
            keep[0, rows, :] = qhat
            keep[1, rows, :] = vhat
            keep[2, rows, :] = sig_ln
            keep[3, rows, :] = sig_g
            rstd_scr[0, rows, :] = rstd_q
            rstd_scr[1, rows, :] = rstd_v
        _row_loop(tm, phase1, unroll=FULL_UNROLL)

        for ch in range(n_chunks):
            r0 = ch * CHUNK
            for h in range(HEADS):
                c0 = h * CHUNK
                vm_scr[r0:r0 + CHUNK, c0:c0 + CHUNK] = (
                    _dot_nn(ws_ref[h].astype(BF16), vl_scr[r0:r0 + CHUNK, c0:c0 + CHUNK]) + bs_ref[:, c0:c0 + CHUNK])

        def phase3(rows):
            bg = z_ref[3, rows, :]
            sig_b = _sigmoid(bg)
            keep[4, rows, :] = sig_b
            ycat_ref[rows, D_MODEL:2 * D_MODEL] = (z_ref[1, rows, :] * vm_scr[rows, :] * (bg * sig_b)).astype(BF16)
        _row_loop(tm, phase3, unroll=FULL_UNROLL)

        y_scr[...] = _dot_nn(ycat_ref[...], wout_ref[...])

        def phase5(rows):
            y = y_scr[rows, :]
            x2 = x_ref[rows, :] + gm * y
            r2 = lax.rsqrt(_mean(x2 * x2) + EPS)
            xn2 = x2 * r2
            diff = xn2 * fg - tgt_ref[rows, :]
            acc_scr[6] += _colsum8(diff * diff)
            dout = diff * inv_d
            acc_scr[0] += _colsum8(dout * xn2)
            dxn = dout * fg
            dx2 = r2 * (dxn - xn2 * _mean(dxn * xn2))
            dx2_ref[rows, :] = dx2
            acc_scr[1] += _colsum8(dx2 * y)
            dy_ref[rows, :] = (dx2 * gm).astype(BF16)
        _row_loop(tm, phase5, unroll=FULL_UNROLL)

        dycat_scr[...] = _dot_nt(dy_ref[...], wout_ref[...])

        def phase7(rows):
            dyb = dycat_scr[rows, D_MODEL:2 * D_MODEL]
            u = z_ref[1, rows, :]
            bg = z_ref[3, rows, :]
            vm = vm_scr[rows, :]
            sig = keep[4, rows, :]
            silu = bg * sig
            dz_ref[1, rows, :] = (dyb * vm * silu).astype(BF16)
            dvm = dyb * u * silu
            dz_ref[3, rows, :] = (dyb * u * vm * (sig * (1.0 + bg * (1.0 - sig)))).astype(BF16)
            dvm_scr[rows, :] = dvm.astype(BF16)
            pos = pl.ds(pl.multiple_of(rows.start % CHUNK, ROWS), ROWS)
            dbs_acc[pos, :] += dvm
        _row_loop(tm, phase7, unroll=FULL_UNROLL)

        for ch in range(n_chunks):
            r0 = ch * CHUNK
            for h in range(HEADS):
                c0 = h * CHUNK
                dvm_b = dvm_scr[r0:r0 + CHUNK, c0:c0 + CHUNK]
                dws_ref[h] += _dot_nt(dvm_b, vl_scr[r0:r0 + CHUNK, c0:c0 + CHUNK])
                dvl_scr[r0:r0 + CHUNK, c0:c0 + CHUNK] = _dot_tn(ws_ref[h].astype(BF16), dvm_b)

        def phase9(rows):
            vhat, rstd_v = keep[1, rows, :], rstd_scr[1, rows, :]
            dvl = dvl_scr[rows, :]
            acc_scr[4] += _colsum8(dvl * vhat)
            acc_scr[5] += _colsum8(dvl)
            dvh = dvl * sg
            dz_ref[2, rows, :] = (rstd_v * (dvh - _mean(dvh) - vhat * _mean(dvh * vhat))).astype(BF16)
            qhat, rstd_q = keep[0, rows, :], rstd_scr[0, rows, :]
            ln = qhat * cg + cb
            sig_ln = keep[2, rows, :]
            gz = z_ref[0, rows, :]
            sig_g = keep[3, rows, :]
            dya = dycat_scr[rows, 0:D_MODEL]
            dz_ref[0, rows, :] = (dya * (ln * sig_ln) * (sig_g * (1.0 + gz * (1.0 - sig_g)))).astype(BF16)
            dln = (dya * (gz * sig_g)) * (sig_ln * (1.0 + ln * (1.0 - sig_ln)))
            acc_scr[2] += _colsum8(dln * qhat)
            acc_scr[3] += _colsum8(dln)
            dqh = dln * cg
            dq = rstd_q * (dqh - _mean(dqh) - qhat * _mean(dqh * qhat))
            for blk in range(D_MODEL // 128):
                dq_ref[blk, rows, :] = dq[:, blk * 128:(blk + 1) * 128]
        _row_loop(tm, phase9, unroll=FULL_UNROLL)

        @pl.when(i == n_steps - 1)
        def _():
            for qi in range(8):
                scale = 0.5 * inv_d if qi == 6 else 1.0
                sm_ref[qi:qi + 1, :] = jnp.sum(acc_scr[qi], axis=0, keepdims=True) * scale
            lane = lax.broadcasted_iota(jnp.int32, (CHUNK, CHUNK), 1)
            tile = jnp.zeros((CHUNK, CHUNK), F32)
            for h in range(HEADS):
                col = jnp.sum(dbs_acc[:, h * CHUNK:(h + 1) * CHUNK], axis=1, keepdims=True)
                tile = jnp.where(lane == h, col, tile)
            dbs_ref[...] = tile

    tok = lambda i: (i, 0)
    const2 = lambda i: (0, 0)
    return pl.pallas_call(
        body, name="middle",
        grid=(n_steps,),
        out_shape=(jax.ShapeDtypeStruct((4, s_len, D_MODEL), BF16),
                   jax.ShapeDtypeStruct((D_MODEL // 128, s_len, 128), F32),
                   jax.ShapeDtypeStruct((s_len, D_MODEL), F32),
                   jax.ShapeDtypeStruct((s_len, 2 * D_MODEL), BF16),
                   jax.ShapeDtypeStruct((s_len, D_MODEL), BF16),
                   jax.ShapeDtypeStruct((HEADS, CHUNK, CHUNK), F32),
                   jax.ShapeDtypeStruct((8, D_MODEL), F32),
                   jax.ShapeDtypeStruct((CHUNK, CHUNK), F32)),
        in_specs=[pl.BlockSpec((tm, D_MODEL), tok),
                  pl.BlockSpec((4, tm, D_MODEL), lambda i: (0, i, 0)),
                  pl.BlockSpec((D_MODEL // 128, tm, 128), lambda i: (0, i, 0)),
                  pl.BlockSpec((tm, D_MODEL), tok),
                  *[pl.BlockSpec((1, D_MODEL), const2) for _ in range(5)],
                  pl.BlockSpec((1, 3 * D_MODEL), const2),
                  pl.BlockSpec((HEADS, CHUNK, CHUNK), lambda i: (0, 0, 0)),
                  pl.BlockSpec((CHUNK, D_MODEL), const2),
                  pl.BlockSpec((2 * D_MODEL, D_MODEL), const2, pipeline_mode=pl.Buffered(1))],
        out_specs=(pl.BlockSpec((4, tm, D_MODEL), lambda i: (0, i, 0)),
                   pl.BlockSpec((D_MODEL // 128, tm, 128), lambda i: (0, i, 0)),
                   pl.BlockSpec((tm, D_MODEL), tok),
                   pl.BlockSpec((tm, 2 * D_MODEL), tok),
                   pl.BlockSpec((tm, D_MODEL), tok),
                   pl.BlockSpec((HEADS, CHUNK, CHUNK), lambda i: (0, 0, 0)),
                   pl.BlockSpec((8, D_MODEL), const2),
                   pl.BlockSpec((CHUNK, CHUNK), const2)),
        scratch_shapes=[pltpu.VMEM((tm, D_MODEL), BF16),
                        pltpu.VMEM((tm, D_MODEL), F32),
                        pltpu.VMEM((tm, D_MODEL), F32),
                        pltpu.VMEM((tm, 2 * D_MODEL), F32),
                        pltpu.VMEM((tm, D_MODEL), BF16),
                        pltpu.VMEM((tm, D_MODEL), F32),
                        pltpu.VMEM((8, 8, D_MODEL), F32),
                        pltpu.VMEM((CHUNK, D_MODEL), F32),
                        pltpu.VMEM((5, tm, D_MODEL), F32),
                        pltpu.VMEM((2, tm, 1), F32)],
        compiler_params=_params(dimension_semantics=("arbitrary",)),
    )(x, z6, q, target, *ln_rows, mod, w_s, bs_exp, w_out)


def _conv_bwd(kidx, dq, sig, t2, cw4, sm1, dws, dbs, ycat, dy):
    s_len = dq.shape[1]
    tt = TIME_TILE
    n_blocks = D_MODEL // 128
    shp = [(16, D_MODEL), (D_MODEL, 128), (128, 128)]
    nj, nr, nc = 1, 256, D_MODEL
    kt = n_blocks // N_CHIPS
    tk = s_len // kt

    def body(k_ref, dq_ref, sig_ref, t2_ref, cw_ref, sm1_ref, dws_in, dbs_in, ycat_ref, dy_ref,
             dz_ref, pk1_ref, dws_ref, dbs_ref, loss_ref, gout_ref,
             dqpad, b_pk, b_dws, b_dbs, r_pk, r_dws, r_dbs, w_dws, ssem, rsem, *rs_scr):
        del k_ref
        jb = pl.program_id(0)
        x_, y_, c_ = _pos()

        @pl.when(jb == 0)
        def _():
            b_pk[0:8, :] = _onehot_rows(sm1_ref[1:2, :], 4 * x_ + 2 * y_ + c_)
            for r, src in enumerate([2, 3, 4, 5, 0, 6]):
                b_pk[8 + r:9 + r, :] = sm1_ref[src:src + 1, :]
            b_pk[14:16, :] = jnp.zeros((2, D_MODEL), F32)
            b_dws[...] = dws_in[...]
            b_dbs[...] = dbs_in[...]

        for e, step in enumerate([0, 0, (3 * n_blocks) // 8, (6 * n_blocks) // 8]):
            @pl.when(jb == step)
            def _():
                _butterfly_event(e, [b_pk, b_dws, b_dbs], [r_pk, r_dws, r_dbs], ssem, rsem, _all_reduce_partners(),
                                 wires=[None, w_dws, None])

        @pl.when(jb == (6 * n_blocks) // 8)
        def _():
            pk1_ref[...] = b_pk[...]
            dws_ref[...] = b_dws[...]
            dbs_ref[...] = b_dbs[...]
            loss_ref[...] = jnp.sum(b_pk[13:14, :], axis=1, keepdims=True)

        zero = jnp.zeros((CONV_PAD, 128), F32)
        dqpad[0:CONV_PAD, :] = zero
        dqpad[s_len + CONV_PAD:s_len + 2 * CONV_PAD, :] = zero
        dqpad[CONV_PAD:s_len + CONV_PAD, :] = dq_ref[0]
        w = cw_ref[0]

        def bwd(i, carry):
            t0 = i * tt
            dp = jnp.zeros((tt, 128), F32)
            for o in range(1, CONV_WIDTH + 1):
                dp = dp + w[CONV_WIDTH - o:CONV_WIDTH - o + 1, :] * dqpad[pl.ds(t0 + o, tt), :]
            dz_ref[0, 0, pl.ds(t0, tt), :] = (dp * sig_ref[0, pl.ds(t0, tt), :]).astype(BF16)
            dz_ref[0, 1, pl.ds(t0, tt), :] = (dp * t2_ref[0, pl.ds(t0, tt), :]).astype(BF16)
            return carry

        n_iter = s_len // tt
        n_groups = min(8, n_iter)
        rows = tk // n_groups

        def conv_and_matmul(add):
            for g in range(n_groups):
                for it in range(g * n_iter // n_groups, (g + 1) * n_iter // n_groups):
                    bwd(it, 0)
                k0 = pl.multiple_of((jb % kt) * tk + g * rows, rows)
                add(_dot_tn(ycat_ref[g * rows:(g + 1) * rows, :], dy_ref[pl.ds(k0, rows), :]))
        _rs_step(jb // kt, jb % kt, kt, nj, nr, nc, conv_and_matmul, gout_ref, rs_scr)

    const2 = lambda j, k_ref: (0, 0)
    grid_spec = pltpu.PrefetchScalarGridSpec(
        num_scalar_prefetch=1,
        grid=(n_blocks,),
        in_specs=[*[pl.BlockSpec((1, s_len, 128), lambda j, k_ref: (j, 0, 0)) for _ in range(3)],
                  pl.BlockSpec((1, 32, 128), lambda j, k_ref: (j // 2, 0, j % 2)),
                  pl.BlockSpec((8, D_MODEL), const2),
                  pl.BlockSpec((D_MODEL, 128), const2),
                  pl.BlockSpec((128, 128), const2),
                  pl.BlockSpec((tk, 512), lambda j, k_ref: (j % kt, _rs_block(j // kt, k_ref[0], nj))),
                  pl.BlockSpec((s_len, D_MODEL), const2, pipeline_mode=pl.Buffered(1))],
        out_specs=(pl.BlockSpec((1, 2, s_len, 128), lambda j, k_ref: (j, 0, 0, 0)),)
        + tuple(pl.BlockSpec(s, const2) for s in shp) + (pl.BlockSpec((1, 1), const2),
                                                         pl.BlockSpec(memory_space=pl.ANY)),
        scratch_shapes=[pltpu.VMEM((s_len + 2 * CONV_PAD, 128), F32)]
        + [pltpu.VMEM(s, F32) for s in shp]
        + [pltpu.VMEM((3,) + shp[0], F32), pltpu.VMEM((3,) + shp[1], BF16), pltpu.VMEM((3,) + shp[2], F32),
           pltpu.VMEM(shp[1], BF16)]
        + [pltpu.SemaphoreType.DMA((9,)), pltpu.SemaphoreType.DMA((9,))] + _rs_scratch(nj, nr, nc))
    return pl.pallas_call(
        body, name="conv_bwd",
        grid_spec=grid_spec,
        out_shape=(jax.ShapeDtypeStruct((n_blocks, 2, s_len, 128), BF16),)
        + tuple(jax.ShapeDtypeStruct(s, F32) for s in shp) + (jax.ShapeDtypeStruct((1, 1), F32),
                                                              jax.ShapeDtypeStruct((nj, 2, nr, nc), F32)),
        compiler_params=_params(dimension_semantics=("arbitrary",)),
    )(kidx, dq, sig, t2, cw4, sm1, dws, dbs, ycat, dy)


def _bwd_in(dz4, dzc, w12, x, dx2, mod, norm_g, dq, p):
    s_len = x.shape[0]
    tm = TOKEN_TILE
    n_steps = s_len // tm
    tt = TIME_TILE
    n_cblocks = D_MODEL // 128
    parts = max(1, n_steps // n_cblocks)
    nsub = max(1, n_cblocks // n_steps)
    cw = 128 * nsub
    tiles = (s_len // tt) // parts

    def body(dz_ref, dzc_ref, w_ref, x_ref, dx2_ref, mod_ref, g_ref, dq_ref, p_ref, gx_ref, sm_ref, dcw_ref,
             dh_scr, acc_scr, ppad, wacc):
        i = pl.program_id(0)
        part = i % parts

        @pl.when(i == 0)
        def _():
            acc_scr[...] = jnp.zeros_like(acc_scr)

        @pl.when(part == 0)
        def _():
            zero = jnp.zeros((CONV_PAD, 128), F32)
            for sub in range(nsub):
                ppad[sub, 0:CONV_PAD, :] = zero
                ppad[sub, s_len + CONV_PAD:s_len + 2 * CONV_PAD, :] = zero
                ppad[sub, CONV_PAD:s_len + CONV_PAD, :] = p_ref[sub]
            wacc[...] = jnp.zeros_like(wacc)

        def dw_tile(tile):
            t0 = pl.multiple_of((part * tiles + tile) * tt, tt)
            for sub in range(nsub):
                dqt = dq_ref[sub, pl.ds(t0, tt), :]
                for o in range(1, CONV_WIDTH + 1):
                    wacc[sub, o - 1] += _colsum8(dqt * ppad[sub, pl.ds(t0 + o, tt), :])
                wacc[sub, CONV_WIDTH] += _colsum8(dqt)

        def dz_cols(j, hf):
            if j < 4:
                return dz_ref[j, :, hf * 512:(hf + 1) * 512]
            return jnp.concatenate([dzc_ref[4 * hf + blk, j - 4] for blk in range(4)], axis=1)

        dots = [(j, hf) for j in range(N_GROUPS) for hf in range(2)]
        dh = jnp.zeros((tm, D_MODEL), F32)
        for d, (j, hf) in enumerate(dots):
            dh = dh + _dot_nt(dz_cols(j, hf), w_ref[2 * _natural_group(j) + hf])
            for tile in range(d * tiles // len(dots), (d + 1) * tiles // len(dots)):
                dw_tile(tile)
        dh_scr[...] = dh

        @pl.when(part == parts - 1)
        def _():
            for sub in range(nsub):
                for k in range(32):
                    dcw_ref[k:k + 1, sub * 128:(sub + 1) * 128] = jnp.sum(wacc[sub, k], axis=0, keepdims=True)

        scale1 = 1.0 + mod_ref[:, D_MODEL:2 * D_MODEL]
        g = g_ref[...]

        def rows_fn(rows):
            xt = x_ref[rows, :]
            r = lax.rsqrt(_mean(xt * xt) + EPS)
            xn = xt * r
            dhr = dh_scr[rows, :]
            acc_scr[0] += _colsum8(dhr)
            acc_scr[1] += _colsum8(dhr * (xn * g))
            acc_scr[2] += _colsum8(dhr * scale1 * xn)
            dxn = dhr * (g * scale1)
            gx_ref[rows, :] = dx2_ref[rows, :] + r * (dxn - xn * _mean(dxn * xn))
        _row_loop(tm, rows_fn, unroll=FULL_UNROLL)

        @pl.when(i == n_steps - 1)
        def _():
            for qi in range(8):
                sm_ref[qi:qi + 1, :] = jnp.sum(acc_scr[qi], axis=0, keepdims=True)

    tok = lambda i: (i, 0)
    const2 = lambda i: (0, 0)
    return pl.pallas_call(
        body, name="bwd_in",
        grid=(n_steps,),
        out_shape=(jax.ShapeDtypeStruct((s_len, D_MODEL), F32), jax.ShapeDtypeStruct((8, D_MODEL), F32),
                   jax.ShapeDtypeStruct((32, D_MODEL), F32)),
        in_specs=[pl.BlockSpec((4, tm, D_MODEL), lambda i: (0, i, 0)),
                  pl.BlockSpec((n_cblocks, 2, tm, 128), lambda i: (0, 0, i, 0)),
                  pl.BlockSpec((12, D_MODEL, 512), lambda i: (0, 0, 0), pipeline_mode=pl.Buffered(1)),
                  pl.BlockSpec((tm, D_MODEL), tok),
                  pl.BlockSpec((tm, D_MODEL), tok),
                  pl.BlockSpec((1, 3 * D_MODEL), const2),
                  pl.BlockSpec((1, D_MODEL), const2),
                  pl.BlockSpec((nsub, s_len, 128), lambda i: (i // parts, 0, 0)),
                  pl.BlockSpec((nsub, s_len, 128), lambda i: (i // parts, 0, 0))],
        out_specs=(pl.BlockSpec((tm, D_MODEL), tok), pl.BlockSpec((8, D_MODEL), const2),
                   pl.BlockSpec((32, cw), lambda i: (0, i // parts))),
        scratch_shapes=[pltpu.VMEM((tm, D_MODEL), F32), pltpu.VMEM((8, 8, D_MODEL), F32),
                        pltpu.VMEM((nsub, s_len + 2 * CONV_PAD, 128), F32), pltpu.VMEM((nsub, 32, 8, 128), F32)],
        compiler_params=_params(dimension_semantics=("arbitrary",)),
    )(dz4, dzc, w12, x, dx2, mod, norm_g, dq, p)


def _rs_shard_block(i, nj):
    g = jnp.where(i < 2 * nj, i % 2, jnp.where(i < 3 * nj, 2, 3))
    j = jnp.where(i < 2 * nj, i // 2, jnp.where(i < 3 * nj, i - 2 * nj, i - 3 * nj))
    return g, j


def _rs_block(i, k, nj):
    g, j = _rs_shard_block(i, nj)
    return nj * jnp.bitwise_xor(k, 3 - g) + j


def _rs_scratch(nj, nr, nc):
    nblk = N_CHIPS * nj
    return [pltpu.VMEM((2, 2, nr, nc), F32),
            pltpu.VMEM((nblk, nr, nc), F32),
            pltpu.VMEM((2 * nj, nr, nc), BF16),
            pltpu.VMEM((2 * nj, nr, nc), BF16),
            pltpu.VMEM((nj, nr, nc), BF16),
            pltpu.VMEM((nj, nr, nc), BF16),
            pltpu.VMEM((nj, nr, nc), F32),
            pltpu.SemaphoreType.DMA,
            pltpu.SemaphoreType.DMA((nblk,)), pltpu.SemaphoreType.DMA((nblk,)),
            pltpu.SemaphoreType.DMA((2 * nj,)), pltpu.SemaphoreType.DMA((2 * nj,)),
            pltpu.SemaphoreType.DMA((nj,)), pltpu.SemaphoreType.DMA((nj,)),
            pltpu.SemaphoreType.DMA, pltpu.SemaphoreType.DMA]


def _rs_step(i, kk, kt, nj, nr, nc, partial_fn, out_ref, scr):
    nblk = N_CHIPS * nj
    (acc, recv_a, send1, recv1, send2, recv2, own_ps,
     lsem, psend, precv, s1send, s1recv, s2send, s2recv, fsend, frecv) = scr
    x, y, c = _pos()
    sib = (x, y, 1 - c)
    first = (jnp.bitwise_xor(x, 1 - c), jnp.bitwise_xor(y, c), c)
    second = (jnp.bitwise_xor(x, c), jnp.bitwise_xor(y, 1 - c), c)
    slot = i % 2

    @pl.when(kk == 0)
    def _():
        acc[slot] = jnp.zeros((2, nr, nc), F32)

    def pair_copy(ii, sl):
        return _rcopy(acc.at[sl, 1 - c], recv_a.at[ii], psend.at[ii], precv.at[ii], sib)

    def pair_sum(ii, sl):
        cp = pair_copy(ii, sl)
        cp.wait_recv()
        cp.wait_send()
        return acc[sl, c] + recv_a[ii]

    def stage1_copy(src, dst):
        return _rcopy(send1.at[src], recv1.at[dst], s1send.at[src], s1recv.at[dst], first)

    def stage2_copy(j):
        return _rcopy(send2.at[j], recv2.at[j], s2send.at[j], s2recv.at[j], second)

    def finalize(ii, sl):
        ps = pair_sum(ii, sl)
        g, j = _rs_shard_block(ii, nj)

        @pl.when(g == 0)
        def _():
            send1[j] = ps.astype(BF16)
            stage1_copy(j, nj + j).start()

        @pl.when(g == 1 + c)
        def _():
            send1[nj + j] = ps.astype(BF16)
            stage1_copy(nj + j, j).start()

        @pl.when(g == 2 - c)
        def _():
            stage1_copy(0, nj + j).wait_recv()
            send2[j] = (ps + recv1[nj + j].astype(F32)).astype(BF16)
            stage2_copy(j).start()

        @pl.when(g == 3)
        def _():
            own_ps[j] = ps

    @pl.when(jnp.logical_and(kk == kt - 1, i >= 1))
    def _():
        finalize(i - 1, 1 - slot)

    def add_partial(res):
        acc[slot, 0] += res[0:nr]
        acc[slot, 1] += res[nr:2 * nr]
    partial_fn(add_partial)

    @pl.when(kk == kt - 1)
    def _():
        pair_copy(i, slot).start()

        @pl.when(i == nblk - 1)
        def _():
            own_ps[nj - 1] = pair_sum(nblk - 1, (nblk - 1) % 2)
            for j in range(nj):
                stage1_copy(0, j).wait_recv()
                stage2_copy(j).wait_recv()
                own_ps[j] = (own_ps[j] + recv1[j].astype(F32)) + recv2[j].astype(F32)
            loc = pltpu.make_async_copy(own_ps, out_ref.at[:, c], lsem)
            loc.start()
            swap = _rcopy(own_ps, out_ref.at[:, c], fsend, frecv, sib)
            swap.start()
            loc.wait()
            swap.wait_send()
            _rcopy(own_ps, out_ref.at[:, 1 - c], fsend, frecv, sib).wait_recv()
            for s in range(2 * nj):
                stage1_copy(s, 0).wait_send()
            for j in range(nj):
                stage2_copy(j).wait_send()


def _grad_w_in(kidx, h, dz4, dzc, sm2, dcw):
    s_len = h.shape[0]
    tk = min(K_TILE, s_len)
    kt = s_len // tk
    nj, nr, nc = 3, 512, 512
    nblk = N_CHIPS * nj

    def body(k_ref, a_ref, b_ref, bc_ref, sm2_ref, dcw_in, out_ref, pk2_ref, dcwsh_ref, dcb_ref, *scr):
        del k_ref
        b_pk2, b_dcw, r_pk2, r_dcw, bsend, brecv = scr[-6:]
        i = pl.program_id(0)
        kk = pl.program_id(1)
        x, y, c = _pos()
        k = 2 * x + y
        m = _rs_block(i, k, nj)

        @pl.when(jnp.logical_and(kk == 0, i == 0))
        def _():
            b = 4 * x + 2 * y + c
            b_pk2[0:8, :] = _onehot_rows(sm2_ref[0:1, :], b)
            b_pk2[8:16, :] = _onehot_rows(sm2_ref[1:2, :], b)
            b_pk2[16:17, :] = sm2_ref[2:3, :]
            b_pk2[17:24, :] = jnp.zeros((7, D_MODEL), F32)
            b_dcw[...] = dcw_in[...]

        for e in range(4):
            @pl.when(jnp.logical_and(kk == 0, i == (e * nblk) // 4))
            def _():
                _butterfly_event(e, [b_pk2, b_dcw], [r_pk2, r_dcw], bsend, brecv, _all_reduce_partners())

        @pl.when(jnp.logical_and(kk == 0, i == (3 * nblk) // 4))
        def _():
            pk2_ref[...] = b_pk2[...]
            sel = jnp.zeros((32, 256), F32)
            for kc in range(N_CHIPS):
                sel = jnp.where(k == kc, b_dcw[:, kc * 256:(kc + 1) * 256], sel)
            dcwsh_ref[...] = sel
            dcb_ref[...] = b_dcw[31:32, :]

        def partial(add):
            rows = pl.ds(pl.multiple_of(kk * tk, tk), tk)

            @pl.when(m >= 4)
            def _():
                add(_dot_tn(a_ref[rows, :], b_ref[0]))

            @pl.when(m < 4)
            def _():
                add(_dot_tn(a_ref[rows, :], jnp.concatenate([bc_ref[blk, 0] for blk in range(4)], axis=1)))
        _rs_step(i, kk, kt, nj, nr, nc, partial, out_ref, scr[:-6])

    def dz4_index(i, kk, k_ref):
        m = _rs_block(i, k_ref[0], nj)
        use = m >= 4
        return (jnp.where(use, m // 2 - 2, 0), jnp.where(use, kk, 0), jnp.where(use, m % 2, 0))

    def dzc_index(i, kk, k_ref):
        m = _rs_block(i, k_ref[0], nj)
        use = m < 4
        return (jnp.where(use, m % 2, 0), jnp.where(use, m // 2, 0), jnp.where(use, kk, 0), 0)

    const2 = lambda i, kk, k_ref: (0, 0)
    small_out = [(24, D_MODEL), (32, 256), (1, D_MODEL)]
    small_buf = [(24, D_MODEL), (32, D_MODEL)]
    grid_spec = pltpu.PrefetchScalarGridSpec(
        num_scalar_prefetch=1,
        grid=(nblk, kt),
        in_specs=[pl.BlockSpec((s_len, D_MODEL), const2, pipeline_mode=pl.Buffered(1)),
                  pl.BlockSpec((1, tk, 512), dz4_index),
                  pl.BlockSpec((4, 1, tk, 128), dzc_index),
                  pl.BlockSpec((8, D_MODEL), const2),
                  pl.BlockSpec((32, D_MODEL), const2)],
        out_specs=(pl.BlockSpec(memory_space=pl.ANY),) + tuple(pl.BlockSpec(s, const2) for s in small_out),
        scratch_shapes=_rs_scratch(nj, nr, nc)
        + [pltpu.VMEM(s, F32) for s in small_buf] + [pltpu.VMEM((3,) + s, F32) for s in small_buf]
        + [pltpu.SemaphoreType.DMA((6,)), pltpu.SemaphoreType.DMA((6,))])
    return pl.pallas_call(
        body, name="grad_w_in",
        grid_spec=grid_spec,
        out_shape=(jax.ShapeDtypeStruct((nj, 2, nr, nc), F32),) + tuple(jax.ShapeDtypeStruct(s, F32) for s in small_out),
        compiler_params=_params(dimension_semantics=("arbitrary", "arbitrary")),
    )(kidx, h, dz4, dzc, sm2, dcw)


def _adamw_math(w, g, m, v):
    m = ADAM_B1 * m + (1.0 - ADAM_B1) * g
    v = ADAM_B2 * v + (1.0 - ADAM_B2) * (g * g)
    m_hat = m / (1.0 - ADAM_B1 ** ADAM_STEP)
    v_hat = v / (1.0 - ADAM_B2 ** ADAM_STEP)
    delta = -ADAM_LR * (m_hat / (jnp.sqrt(v_hat) + ADAM_EPS) + ADAM_WD * w)
    return delta, m, v


def _adamw_blocked(w, m, v, g4, name):
    nj, _, nr, nc = g4.shape

    def body(w_ref, m_ref, v_ref, g_ref, go_ref, d_ref, mo_ref, vo_ref):
        g = g_ref[0, 0]
        d, mn, vn = _adamw_math(w_ref[...], g, m_ref[...], v_ref[...])
        go_ref[...] = g
        d_ref[...] = d
        mo_ref[...] = mn
        vo_ref[...] = vn

    blk = pl.BlockSpec((nr, nc), lambda j, hf: (hf, j))
    return pl.pallas_call(
        body, name=name,
        grid=(nj, 2),
        out_shape=tuple(jax.ShapeDtypeStruct(w.shape, F32) for _ in range(4)),
        in_specs=[blk, blk, blk, pl.BlockSpec((1, 1, nr, nc), lambda j, hf: (j, hf, 0, 0))],
        out_specs=(blk, blk, blk, blk),
        compiler_params=_params(dimension_semantics=("arbitrary", "arbitrary")),
    )(w, m, v, g4)


def _adamw_w_ada(cact, pk1, pk2, w, m, v):
    rb = 256

    def body(cact_ref, pk1_ref, pk2_ref, w_ref, m_ref, v_ref, g_ref, d_ref, mo_ref, vo_ref, gb_ref, dmod, sel_scr):
        @pl.when(pl.program_id(0) == 0)
        def _():
            x, y, _ = _pos()
            k = 2 * x + y
            dmod[:, 0:D_MODEL] = pk2_ref[0:8, :]
            dmod[:, D_MODEL:2 * D_MODEL] = pk2_ref[8:16, :]
            dmod[:, 2 * D_MODEL:3 * D_MODEL] = pk1_ref[0:8, :]
            gb_ref[...] = jnp.sum(dmod[...], axis=0, keepdims=True)
            sel = jnp.zeros((8, 768), F32)
            for kk in range(N_CHIPS):
                sel = jnp.where(k == kk, dmod[:, kk * 768:(kk + 1) * 768], sel)
            sel_scr[...] = sel

        g = _dot_tn(cact_ref[...].astype(BF16), sel_scr[...].astype(BF16))
        d, mn, vn = _adamw_math(w_ref[...], g, m_ref[...], v_ref[...])
        g_ref[...] = g
        d_ref[...] = d
        mo_ref[...] = mn
        vo_ref[...] = vn

    blk = pl.BlockSpec((rb, 768), lambda i: (i, 0))
    const2 = lambda i: (0, 0)
    return pl.pallas_call(
        body, name="adamw_w_ada",
        grid=(D_MODEL // rb,),
        out_shape=tuple(jax.ShapeDtypeStruct(w.shape, F32) for _ in range(4)) + (
            jax.ShapeDtypeStruct((1, 3 * D_MODEL), F32),),
        in_specs=[pl.BlockSpec((8, rb), lambda i: (0, i)), pl.BlockSpec((16, D_MODEL), const2),
                  pl.BlockSpec((24, D_MODEL), const2), blk, blk, blk],
        out_specs=(blk, blk, blk, blk, pl.BlockSpec((1, 3 * D_MODEL), const2)),
        scratch_shapes=[pltpu.VMEM((8, 3 * D_MODEL), F32), pltpu.VMEM((8, 768), F32)],
        compiler_params=_params(dimension_semantics=("arbitrary",)),
    )(cact, pk1, pk2, w, m, v)


def _adamw_small(items, conv, packs, rows):
    n, nr = len(items), len(rows)

    def body(*refs):
        refs = list(refs)
        take = lambda k: [refs.pop(0) for _ in range(k)]
        ins, (cw_ref, cg_ref, cm_ref, cv_ref), pk_refs, row_ins = take(4 * n), take(4), take(len(packs)), take(3 * nr)
        outs, (cgo_ref, cd_ref, cmo_ref, cvo_ref), row_outs = take(3 * n), take(4), take(4 * nr)
        for i in range(n):
            w_ref, g_ref, m_ref, v_ref = ins[4 * i:4 * i + 4]
            d, mn, vn = _adamw_math(w_ref[...], g_ref[...], m_ref[...], v_ref[...])
            outs[3 * i][...] = d
            outs[3 * i + 1][...] = mn
            outs[3 * i + 2][...] = vn
        for r in range(CONV_WIDTH):
            g = cg_ref[r:r + 1, :]
            d, mn, vn = _adamw_math(cw_ref[0, r], g, cm_ref[0, r], cv_ref[0, r])
            cgo_ref[0, r] = g
            cd_ref[0, r] = d
            cmo_ref[0, r] = mn
            cvo_ref[0, r] = vn
        for i, (_, _, _, p, r) in enumerate(rows):
            w_ref, m_ref, v_ref = row_ins[3 * i:3 * i + 3]
            g = pk_refs[p][r:r + 1, :]
            d, mn, vn = _adamw_math(w_ref[...], g, m_ref[...], v_ref[...])
            row_outs[4 * i][...] = g
            row_outs[4 * i + 1][...] = d
            row_outs[4 * i + 2][...] = mn
            row_outs[4 * i + 3][...] = vn

    flat = [a for it in items for a in it] + list(conv) + list(packs) + [a for it in rows for a in it[:3]]
    n_out = 3 * n + 4 + 4 * nr
    outs = pl.pallas_call(
        body, name="adamw_small",
        out_shape=tuple(jax.ShapeDtypeStruct(it[0].shape, F32) for it in items for _ in range(3))
        + tuple(jax.ShapeDtypeStruct(conv[0].shape, F32) for _ in range(4))
        + tuple(jax.ShapeDtypeStruct(it[0].shape, F32) for it in rows for _ in range(4)),
        in_specs=[_vmem()] * len(flat),
        out_specs=tuple(_vmem() for _ in range(n_out)),
        compiler_params=_params(),
    )(*flat)
    return ([tuple(outs[3 * i:3 * i + 3]) for i in range(n)], tuple(outs[3 * n:3 * n + 4]),
            [tuple(outs[3 * n + 4 + 4 * i:3 * n + 8 + 4 * i]) for i in range(nr)])


def kernel(x, c, w_ada, b_ada, norm_g, w_in, conv_w, conv_b, conv_ln_g, conv_ln_b, sg_ln_g, sg_ln_b, w_s, b_s, w_out, final_g, loss_target, m_w_ada, m_b_ada, m_norm_g, m_w_in, m_conv_w, m_conv_b, m_conv_ln_g, m_conv_ln_b, m_sg_ln_g, m_sg_ln_b, m_w_s, m_b_s, m_w_out, m_final_g, v_w_ada, v_b_ada, v_norm_g, v_w_in, v_conv_w, v_conv_b, v_conv_ln_g, v_conv_ln_b, v_sg_ln_g, v_sg_ln_b, v_w_s, v_b_s, v_w_out, v_final_g):
    s_len = x.shape[1]
    x2d = x[0]
    tgt = loss_target[0]
    row = lambda a: a.reshape(1, -1)

    kidx = (2 * lax.axis_index("x") + lax.axis_index("y")).astype(jnp.int32).reshape(1)
    cw_sh = jnp.pad(conv_w.reshape(CONV_WIDTH, 256), ((0, 1), (0, 0)))
    h, z6, mod, cact, cw4, w_in12 = _front(kidx, x2d, c, w_ada[0], b_ada, norm_g, w_in[0], cw_sh)
    w12 = w_in12.reshape(12, D_MODEL, 512)
    q, p_cm, sig_cm, t2_cm, w_out4 = _conv_fwd(z6, cw4, conv_b, w_out[0])
    w_out_full = w_out4.reshape(2 * D_MODEL, D_MODEL)
    ln_rows = (conv_ln_g, conv_ln_b, sg_ln_g, sg_ln_b, row(final_g))
    bs_exp = jnp.repeat(b_s[0].T, CHUNK, axis=1)
    dz4, dq, dx2, ycat, dy, dws, sm1, dbs = _middle(x2d, z6, q, tgt, ln_rows, mod, w_s[0], bs_exp, w_out_full)
    dzc, pk1, dws_r, dbs_r, loss_t, g_w_out4 = _conv_bwd(
        kidx, dq, sig_cm, t2_cm, cw4, sm1, dws.reshape(D_MODEL, CHUNK), dbs, ycat, dy)
    grad_x, sm2, dcw = _bwd_in(dz4, dzc, w12, x2d, dx2, mod, norm_g, dq, p_cm)
    g_w_in4, pk2, dcw_sh, dcb = _grad_w_in(kidx, h, dz4, dzc, sm2, dcw)

    g_w_in, d_w_in, nm_w_in, nv_w_in = _adamw_blocked(w_in[0], m_w_in[0], v_w_in[0], g_w_in4, "adamw_w_in")
    g_w_out, d_w_out, nm_w_out, nv_w_out = _adamw_blocked(w_out[0], m_w_out[0], v_w_out[0], g_w_out4, "adamw_w_out")
    g_w_ada, d_w_ada, nm_w_ada, nv_w_ada, g_b_ada = _adamw_w_ada(cact, pk1, pk2, w_ada[0], m_w_ada[0], v_w_ada[0])

    loss = loss_t.reshape(())
    g_w_s = dws_r
    g_b_s = dbs_r[:, :HEADS].T
    small = [
        (b_ada, g_b_ada, m_b_ada, v_b_ada),
        (conv_b, dcb, m_conv_b, v_conv_b),
        (w_s.reshape(D_MODEL, CHUNK), g_w_s, m_w_s.reshape(D_MODEL, CHUNK), v_w_s.reshape(D_MODEL, CHUNK)),
        (b_s[0], g_b_s, m_b_s[0], v_b_s[0]),
    ]
    rows = [
        (norm_g, m_norm_g, v_norm_g, 1, 16),
        (conv_ln_g, m_conv_ln_g, v_conv_ln_g, 0, 8),
        (conv_ln_b, m_conv_ln_b, v_conv_ln_b, 0, 9),
        (sg_ln_g, m_sg_ln_g, v_sg_ln_g, 0, 10),
        (sg_ln_b, m_sg_ln_b, v_sg_ln_b, 0, 11),
        (row(final_g), row(m_final_g), row(v_final_g), 0, 12),
    ]
    upd, (g_conv_w, d_conv_w, nm_conv_w, nv_conv_w), row_upd = _adamw_small(
        small, (conv_w, dcw_sh, m_conv_w, v_conv_w), (pk1, pk2), rows)
    g_norm_g, g_cln_g, g_cln_b, g_sln_g, g_sln_b, g_final = (t[0] for t in row_upd)

    shapes = [w_ada.shape, b_ada.shape, norm_g.shape, w_in.shape, conv_w.shape, conv_b.shape, conv_ln_g.shape,
              conv_ln_b.shape, sg_ln_g.shape, sg_ln_b.shape, w_s.shape, b_s.shape, w_out.shape, final_g.shape]
    grads = [g_w_ada, g_b_ada, g_norm_g, g_w_in, g_conv_w, dcb, g_cln_g, g_cln_b, g_sln_g, g_sln_b, g_w_s, g_b_s,
             g_w_out, g_final]
    big = {0: (d_w_ada, nm_w_ada, nv_w_ada), 3: (d_w_in, nm_w_in, nv_w_in), 4: (d_conv_w, nm_conv_w, nv_conv_w),
           12: (d_w_out, nm_w_out, nv_w_out)}
    trip = [None] * 14
    for i, t in big.items():
        trip[i] = t
    for i, t in zip([1, 5, 10, 11], upd):
        trip[i] = t
    for i, t in zip([2, 6, 7, 8, 9, 13], row_upd):
        trip[i] = t[1:]
    fit = lambda arrs: [a.reshape(s) for a, s in zip(arrs, shapes)]
    return (loss, grad_x.reshape(x.shape), *fit(grads), *fit([t[0] for t in trip]), *fit([t[1] for t in trip]),
            *fit([t[2] for t in trip]))
```

```python
import jax
import jax.numpy as jnp
from jax import lax
from jax.experimental import pallas as pl
from jax.experimental.pallas import tpu as pltpu

F32 = jnp.float32
BF16 = jnp.bfloat16
MESH = pl.DeviceIdType.MESH

D_MODEL = 1024
N_CHIPS = 4
HEADS = 8
CHUNK = 128
CONV_WIDTH = 31
CONV_HALF = CONV_WIDTH // 2
CONV_PAD = 16
EPS = 1e-6
ADAM_LR = 0.001
ADAM_B1 = 0.9
ADAM_B2 = 0.999
ADAM_EPS = 1e-08
ADAM_WD = 0.01
ADAM_STEP = 10

V7X_VMEM_BYTES = 64 * 1024 * 1024
VMEM_LIMIT = V7X_VMEM_BYTES - 8 * 1024 * 1024
ROWS = 16
UNROLL = 8
TOKEN_TILE = 256
FULL_UNROLL = TOKEN_TILE // ROWS
TIME_TILE = 128
K_TILE = 2048

N_GROUPS = 6


def _natural_group(j):
    return (j + 2) % N_GROUPS


def _pos():
    return lax.axis_index("x"), lax.axis_index("y"), lax.axis_index("c")


def _rcopy(src, dst, ssem, rsem, dev):
    return pltpu.make_async_remote_copy(src_ref=src, dst_ref=dst, send_sem=ssem, recv_sem=rsem,
                                        device_id=dev, device_id_type=MESH)


def _vmem():
    return pl.BlockSpec(memory_space=pltpu.VMEM)


def _params(**kw):
    return pltpu.CompilerParams(vmem_limit_bytes=VMEM_LIMIT, **kw)


def _sigmoid(v):
    return 0.5 * jnp.tanh(0.5 * v) + 0.5


def _row_loop(n_rows, body, unroll=1):
    def step(r, carry):
        body(pl.ds(pl.multiple_of(r * ROWS, ROWS), ROWS))
        return carry
    lax.fori_loop(0, n_rows // ROWS, step, 0, unroll=unroll)


def _colsum8(v):
    return v.reshape(v.shape[0] // 8, 8, v.shape[1]).sum(axis=0)


def _mean(v):
    return jnp.mean(v, axis=-1, keepdims=True)


def _dot_nn(a, b):
    return jnp.dot(a, b, preferred_element_type=F32)


def _dot_nt(a, b):
    return lax.dot_general(a, b, (((1,), (1,)), ((), ())), preferred_element_type=F32)


def _dot_tn(a, b):
    return lax.dot_general(a, b, (((0,), (0,)), ((), ())), preferred_element_type=F32)


def _remote_chip(k, r):
    return jnp.bitwise_xor(k, r + 1)


def _front(kidx, x, c, w_ada, b_ada, norm_g, w_in, cw):
    s_len = x.shape[0]
    tmh = min(512, s_len)
    tmz = min(2048, s_len)
    nh = s_len // tmh
    nz = s_len // tmz
    n_steps = nh + 12 * nz

    def remote_block(q):
        return jnp.where(q < 6, q % 2, 2), jnp.where(q < 6, q // 2, q - 6)

    def block_of(i, k):
        r, j = remote_block(jnp.maximum(i - 3, 0))
        return jnp.where(i < 3, 3 * k + i, 3 * _remote_chip(k, r) + j)

    def body(k_ref, x_ref, c_ref, bada_ref, g_ref, cw_ref, wada_hbm, win_hbm,
             h_ref, z_ref, mod_ref, cact_ref, cw4_ref, w12_hbm,
             h_all, wbuf, stage, wada_v, cslab, mslab, cw4_s,
             lsem, csend, crecv, msend, mrecv, wsend, wrecv, isend, irecv, fsend, frecv, osem):
        del k_ref
        t = pl.program_id(0)
        x_, y_, c_ = _pos()
        k = 2 * x_ + y_
        b = 4 * x_ + 2 * y_ + c_
        sib = (x_, y_, 1 - c_)

        def dev_of(r):
            kk = _remote_chip(k, r)
            return (kk // 2, kk % 2, c_)

        def ici(q, kk):
            r, j = remote_block(q)
            return _rcopy(wbuf.at[3 * kk + j, c_], wbuf.at[3 * kk + j, c_], isend.at[q], irecv.at[q], dev_of(r))

        def fwd(q, hf):
            r, j = remote_block(q)
            blk = 3 * _remote_chip(k, r) + j
            return _rcopy(wbuf.at[blk, hf], wbuf.at[blk, hf], fsend.at[q], frecv.at[q], sib)

        def c_copy(q, src):
            d = jnp.bitwise_xor(b, q)
            return _rcopy(cslab.at[src], cslab.at[src], csend.at[q - 1], crecv.at[q - 1], (d // 4, (d // 2) % 2, d % 2))

        def m_copy(r, kk):
            return _rcopy(mslab.at[kk], mslab.at[kk], msend.at[r], mrecv.at[r], dev_of(r))

        def cw_copy(r, kk):
            return _rcopy(cw4_s.at[kk], cw4_s.at[kk], wsend.at[r], wrecv.at[r], dev_of(r))

        def to_hbm(i):
            m = block_of(i, k)
            return pltpu.make_async_copy(wbuf.at[m], w12_hbm.at[m], osem.at[i])

        @pl.when(t == 0)
        def _():
            ld_w = pltpu.make_async_copy(win_hbm, stage, lsem.at[0])
            ld_w.start()
            ld_a = pltpu.make_async_copy(wada_hbm, wada_v, lsem.at[1])
            ld_a.start()
            cslab[b] = jnp.broadcast_to(c_ref[...], (8, D_MODEL))
            for q in range(1, 8):
                c_copy(q, b).start()
            cw4_s[k] = cw_ref[...]
            for r in range(3):
                cw_copy(r, k).start()
            ld_w.wait()
            for j in range(3):
                for hf in range(2):
                    wbuf[3 * k + j, hf] = stage[hf * 512:(hf + 1) * 512, j * 512:(j + 1) * 512].astype(BF16)
            for i in range(3):
                to_hbm(i).start()
            for q in range(1, 8):
                c_copy(q, jnp.bitwise_xor(b, q)).wait_recv()
            row = lax.broadcasted_iota(jnp.int32, (8, D_MODEL), 0)
            call = jnp.zeros((8, D_MODEL), F32)
            for d in range(8):
                call = jnp.where(row == d, cslab[d], call)
            cact = call * _sigmoid(call)
            cact_ref[...] = cact
            ld_a.wait()
            mslab[k] = _dot_nn(cact.astype(BF16), wada_v[...].astype(BF16))
            for r in range(3):
                m_copy(r, k).start()
            for q in range(9):
                ici(q, k).start()
            for r in range(3):
                m_copy(r, _remote_chip(k, r)).wait_recv()
            row8 = lax.broadcasted_iota(jnp.int32, (8, 768), 0)
            for kk in range(N_CHIPS):
                piece = jnp.sum(jnp.where(row8 == b, mslab[kk], 0.0), axis=0, keepdims=True)
                mod_ref[:, kk * 768:(kk + 1) * 768] = piece + bada_ref[:, kk * 768:(kk + 1) * 768]
            for r in range(3):
                cw_copy(r, _remote_chip(k, r)).wait_recv()
            cw4_ref[...] = cw4_s[...]

        @pl.when(t < nh)
        def _():
            shift = mod_ref[:, 0:D_MODEL]
            scale1 = 1.0 + mod_ref[:, D_MODEL:2 * D_MODEL]
            g = g_ref[...]
            base = t * tmh

            def rows_fn(rows):
                xt = x_ref[rows, :]
                r = lax.rsqrt(_mean(xt * xt) + EPS)
                hv = ((xt * r * g) * scale1 + shift).astype(BF16)
                h_ref[rows, :] = hv
                h_all[pl.ds(pl.multiple_of(base + rows.start, ROWS), ROWS), :] = hv
            _row_loop(tmh, rows_fn, unroll=UNROLL)

        @pl.when(t >= nh)
        def _():
            u = t - nh
            i = u // nz
            rt = u % nz
            @pl.when(jnp.logical_and(rt == 0, jnp.logical_and(i >= 2, i <= 10)))
            def _():
                q = i - 2
                r, _ = remote_block(q)
                ici(q, _remote_chip(k, r)).wait_recv()
                fwd(q, c_).start()

            @pl.when(jnp.logical_and(rt == 0, i >= 3))
            def _():
                fwd(i - 3, 1 - c_).wait_recv()
                to_hbm(i).start()
            m = block_of(i, k)
            hb = h_all[pl.ds(pl.multiple_of(rt * tmz, tmz), tmz), :]
            z_ref[0] = _dot_nn(hb, wbuf[m].reshape(D_MODEL, 512))

        @pl.when(t == n_steps - 1)
        def _():
            for q in range(1, 8):
                c_copy(q, b).wait_send()
            for r in range(3):
                m_copy(r, k).wait_send()
                cw_copy(r, k).wait_send()
            for q in range(9):
                ici(q, k).wait_send()
                fwd(q, c_).wait_send()
            for i in range(12):
                to_hbm(i).wait()

    def z_index(t, k_ref):
        u = jnp.maximum(t - nh, 0)
        m = block_of(u // nz, k_ref[0])
        return ((m // 2 + 4) % N_GROUPS, u % nz, m % 2)

    tok = lambda t, k_ref: (jnp.minimum(t, nh - 1), 0)
    const2 = lambda t, k_ref: (0, 0)
    grid_spec = pltpu.PrefetchScalarGridSpec(
        num_scalar_prefetch=1,
        grid=(n_steps,),
        in_specs=[pl.BlockSpec((tmh, D_MODEL), tok),
                  pl.BlockSpec((1, D_MODEL), const2),
                  pl.BlockSpec((1, 3 * D_MODEL), const2),
                  pl.BlockSpec((1, D_MODEL), const2),
                  pl.BlockSpec((32, 256), const2),
                  pl.BlockSpec(memory_space=pl.ANY),
                  pl.BlockSpec(memory_space=pl.ANY)],
        out_specs=(pl.BlockSpec((tmh, D_MODEL), tok),
                   pl.BlockSpec((1, tmz, 512), z_index),
                   pl.BlockSpec((1, 3 * D_MODEL), const2),
                   pl.BlockSpec((8, D_MODEL), const2),
                   pl.BlockSpec((N_CHIPS, 32, 256), lambda t, k_ref: (0, 0, 0)),
                   pl.BlockSpec(memory_space=pl.ANY)),
        scratch_shapes=[pltpu.VMEM((s_len, D_MODEL), BF16),
                        pltpu.VMEM((12, 2, 512, 512), BF16),
                        pltpu.VMEM((D_MODEL, 1536), F32),
                        pltpu.VMEM((D_MODEL, 768), F32),
                        pltpu.VMEM((8, 8, D_MODEL), F32),
                        pltpu.VMEM((N_CHIPS, 8, 768), F32),
                        pltpu.VMEM((N_CHIPS, 32, 256), F32),
                        pltpu.SemaphoreType.DMA((2,)),
                        pltpu.SemaphoreType.DMA((7,)), pltpu.SemaphoreType.DMA((7,)),
                        pltpu.SemaphoreType.DMA((3,)), pltpu.SemaphoreType.DMA((3,)),
                        pltpu.SemaphoreType.DMA((3,)), pltpu.SemaphoreType.DMA((3,)),
                        pltpu.SemaphoreType.DMA((9,)), pltpu.SemaphoreType.DMA((9,)),
                        pltpu.SemaphoreType.DMA((9,)), pltpu.SemaphoreType.DMA((9,)),
                        pltpu.SemaphoreType.DMA((12,))])
    return pl.pallas_call(
        body, name="front",
        grid_spec=grid_spec,
        out_shape=(jax.ShapeDtypeStruct((s_len, D_MODEL), BF16),
                   jax.ShapeDtypeStruct((N_GROUPS, s_len, D_MODEL), F32),
                   jax.ShapeDtypeStruct((1, 3 * D_MODEL), F32),
                   jax.ShapeDtypeStruct((8, D_MODEL), F32),
                   jax.ShapeDtypeStruct((N_CHIPS, 32, 256), F32),
                   jax.ShapeDtypeStruct((12, 2, 512, 512), BF16)),
        compiler_params=_params(dimension_semantics=("arbitrary",)),
    )(kidx, x, c, b_ada, norm_g, cw, w_ada, w_in)


def _all_reduce_partners():
    x, y, c = _pos()
    return [(x, y, 1 - c), (x, 1 - y, c), (1 - x, y, c)]


def _butterfly_event(e, bufs, recvs, ssem, rsem, partners, wires=None):
    n = len(bufs)
    wires = wires or [None] * n

    def copies(s):
        return [_rcopy(buf if wire is None else wire, recv.at[s], ssem.at[s * n + i], rsem.at[s * n + i], partners[s])
                for i, (buf, recv, wire) in enumerate(zip(bufs, recvs, wires))]

    if e > 0:
        for cp in copies(e - 1):
            cp.wait()
        for buf, recv, wire in zip(bufs, recvs, wires):
            mine = buf[...] if wire is None else wire[...].astype(F32)
            buf[...] = mine + recv[e - 1].astype(F32)
    if e < len(partners):
        for buf, wire in zip(bufs, wires):
            if wire is not None:
                wire[...] = buf[...].astype(BF16)
        for cp in copies(e):
            cp.start()


def _onehot_rows(v, b):
    row = lax.broadcasted_iota(jnp.int32, (8, v.shape[1]), 0)
    return jnp.where(row == b, jnp.broadcast_to(v, (8, v.shape[1])), 0.0)


def _conv_fwd(z6, cw4, conv_b, w_out):
    s_len = z6.shape[1]
    tt = TIME_TILE
    n_blocks = D_MODEL // 128

    def body(z_ref, cw_ref, cb_ref, wout_hbm, q_ref, p_ref, sig_ref, t2_ref, wout4_hbm,
             ppad, stage, wbuf, lsem, isend, irecv, fsend, frecv, osem):
        jb = pl.program_id(0)
        x_, y_, c_ = _pos()
        k = 2 * x_ + y_
        sib = (x_, y_, 1 - c_)

        def ici(r, kk):
            rk = _remote_chip(k, r)
            return _rcopy(wbuf.at[kk, c_], wbuf.at[kk, c_], isend.at[r], irecv.at[r], (rk // 2, rk % 2, c_))

        def fwd(r, hf):
            kk = _remote_chip(k, r)
            return _rcopy(wbuf.at[kk, hf], wbuf.at[kk, hf], fsend.at[r], frecv.at[r], sib)

        @pl.when(jb == 0)
        def _():
            ld = pltpu.make_async_copy(wout_hbm, stage, lsem)
            ld.start()
            ld.wait()
            for hf in range(2):
                wbuf[k, hf] = stage[hf * 256:(hf + 1) * 256, :].astype(BF16)
            for r in range(3):
                ici(r, k).start()

        @pl.when(jb == (5 * n_blocks) // 8)
        def _():
            for r in range(3):
                ici(r, _remote_chip(k, r)).wait_recv()
                fwd(r, c_).start()

        zero = jnp.zeros((CONV_PAD, 128), F32)
        ppad[0:CONV_PAD, :] = zero
        ppad[s_len + CONV_PAD:s_len + 2 * CONV_PAD, :] = zero

        def fill(i, carry):
            t0 = pl.multiple_of(i * tt, tt)
            sig = _sigmoid(z_ref[1, pl.ds(t0, tt), :])
            p = z_ref[0, pl.ds(t0, tt), :] * sig
            ppad[pl.ds(CONV_PAD + t0, tt), :] = p
            p_ref[0, pl.ds(t0, tt), :] = p
            sig_ref[0, pl.ds(t0, tt), :] = sig
            t2_ref[0, pl.ds(t0, tt), :] = p * (1.0 - sig)
            return carry
        lax.fori_loop(0, s_len // tt, fill, 0)
        w = cw_ref[0]
        bias = cb_ref[...]

        def conv(i, carry):
            t0 = pl.multiple_of(i * tt, tt)
            acc = jnp.broadcast_to(bias, (tt, 128))
            for o in range(1, CONV_WIDTH + 1):
                acc = acc + w[o - 1:o, :] * ppad[pl.ds(t0 + o, tt), :]
            q_ref[0, pl.ds(t0, tt), :] = acc
            return carry
        lax.fori_loop(0, s_len // tt, conv, 0)

        @pl.when(jb == n_blocks - 1)
        def _():
            for r in range(3):
                fwd(r, 1 - c_).wait_recv()
            out = pltpu.make_async_copy(wbuf, wout4_hbm, osem)
            out.start()
            for r in range(3):
                ici(r, k).wait_send()
                fwd(r, c_).wait_send()
            out.wait()

    return pl.pallas_call(
        body, name="conv_fwd",
        grid=(n_blocks,),
        out_shape=tuple(jax.ShapeDtypeStruct((n_blocks, s_len, 128), F32) for _ in range(4))
        + (jax.ShapeDtypeStruct((N_CHIPS, 2, 256, D_MODEL), BF16),),
        in_specs=[pl.BlockSpec((2, s_len, 128), lambda j: (2, 0, j)),
                  pl.BlockSpec((1, 32, 128), lambda j: (j // 2, 0, j % 2)),
                  pl.BlockSpec((1, 128), lambda j: (0, j)),
                  pl.BlockSpec(memory_space=pl.ANY)],
        out_specs=tuple(pl.BlockSpec((1, s_len, 128), lambda j: (j, 0, 0)) for _ in range(4))
        + (pl.BlockSpec(memory_space=pl.ANY),),
        scratch_shapes=[pltpu.VMEM((s_len + 2 * CONV_PAD, 128), F32),
                        pltpu.VMEM((512, D_MODEL), F32),
                        pltpu.VMEM((N_CHIPS, 2, 256, D_MODEL), BF16),
                        pltpu.SemaphoreType.DMA,
                        pltpu.SemaphoreType.DMA((3,)), pltpu.SemaphoreType.DMA((3,)),
                        pltpu.SemaphoreType.DMA((3,)), pltpu.SemaphoreType.DMA((3,)),
                        pltpu.SemaphoreType.DMA],
        compiler_params=_params(dimension_semantics=("arbitrary",)),
    )(z6, cw4, conv_b, w_out)


def _middle(x, z6, q, target, ln_rows, mod, w_s, bs_exp, w_out):
    s_len = x.shape[0]
    tm = TOKEN_TILE
    n_steps = s_len // tm
    n_chunks = tm // CHUNK
    inv_d = 1.0 / D_MODEL

    def body(x_ref, z_ref, q_ref, tgt_ref, cg_ref, cb_ref, sg_ref, sb_ref, fg_ref, mod_ref, ws_ref, bs_ref, wout_ref,
             dz_ref, dq_ref, dx2_ref, ycat_ref, dy_ref, dws_ref, sm_ref, dbs_ref,
             vl_scr, vm_scr, y_scr, dycat_scr, dvm_scr, dvl_scr, acc_scr, dbs_acc, keep, rstd_scr):
        i = pl.program_id(0)

        @pl.when(i == 0)
        def _():
            acc_scr[...] = jnp.zeros_like(acc_scr)
            dbs_acc[...] = jnp.zeros_like(dbs_acc)
            dws_ref[...] = jnp.zeros_like(dws_ref)

        cg, cb, sg, sb, fg = cg_ref[...], cb_ref[...], sg_ref[...], sb_ref[...], fg_ref[...]
        gm = mod_ref[:, 2 * D_MODEL:3 * D_MODEL]

        def norm_stats(t):
            c = t - _mean(t)
            rstd = lax.rsqrt(_mean(c * c) + EPS)
            return c * rstd, rstd

        def phase1(rows):
            qhat, rstd_q = norm_stats(jnp.concatenate([q_ref[blk, rows, :] for blk in range(D_MODEL // 128)], axis=1))
            ln = qhat * cg + cb
            gz = z_ref[0, rows, :]
            sig_ln = _sigmoid(ln)
            sig_g = _sigmoid(gz)
            ycat_ref[rows, 0:D_MODEL] = ((ln * sig_ln) * (gz * sig_g)).astype(BF16)
            vhat, rstd_v = norm_stats(z_ref[2, rows, :])
            vl_scr[rows, :] = (vhat * sg + sb).astype(BF16)
            keep[0, rows, :] = qhat
            keep[1, rows, :] = vhat
            keep[2, rows, :] = sig_ln
            keep[3, rows, :] = sig_g
            rstd_scr[0, rows, :] = rstd_q
            rstd_scr[1, rows, :] = rstd_v
        _row_loop(tm, phase1, unroll=FULL_UNROLL)

        for ch in range(n_chunks):
            r0 = ch * CHUNK
            for h in range(HEADS):
                c0 = h * CHUNK
                vm_scr[r0:r0 + CHUNK, c0:c0 + CHUNK] = (
                    _dot_nn(ws_ref[h].astype(BF16), vl_scr[r0:r0 + CHUNK, c0:c0 + CHUNK]) + bs_ref[:, c0:c0 + CHUNK])

        def phase3(rows):
            bg = z_ref[3, rows, :]
            sig_b = _sigmoid(bg)
            keep[4, rows, :] = sig_b
            ycat_ref[rows, D_MODEL:2 * D_MODEL] = (z_ref[1, rows, :] * vm_scr[rows, :] * (bg * sig_b)).astype(BF16)
        _row_loop(tm, phase3, unroll=FULL_UNROLL)

        y_scr[...] = _dot_nn(ycat_ref[...], wout_ref[...])

        def phase5(rows):
            y = y_scr[rows, :]
            x2 = x_ref[rows, :] + gm * y
            r2 = lax.rsqrt(_mean(x2 * x2) + EPS)
            xn2 = x2 * r2
            diff = xn2 * fg - tgt_ref[rows, :]
            acc_scr[6] += _colsum8(diff * diff)
            dout = diff * inv_d
            acc_scr[0] += _colsum8(dout * xn2)
            dxn = dout * fg
            dx2 = r2 * (dxn - xn2 * _mean(dxn * xn2))
            dx2_ref[rows, :] = dx2
            acc_scr[1] += _colsum8(dx2 * y)
            dy_ref[rows, :] = (dx2 * gm).astype(BF16)
        _row_loop(tm, phase5, unroll=FULL_UNROLL)

        dycat_scr[...] = _dot_nt(dy_ref[...], wout_ref[...])

        def phase7(rows):
            dyb = dycat_scr[rows, D_MODEL:2 * D_MODEL]
            u = z_ref[1, rows, :]
            bg = z_ref[3, rows, :]
            vm = vm_scr[rows, :]
            sig = keep[4, rows, :]
            silu = bg * sig
            dz_ref[1, rows, :] = (dyb * vm * silu).astype(BF16)
            dvm = dyb * u * silu
            dz_ref[3, rows, :] = (dyb * u * vm * (sig * (1.0 + bg * (1.0 - sig)))).astype(BF16)
            dvm_scr[rows, :] = dvm.astype(BF16)
            pos = pl.ds(pl.multiple_of(rows.start % CHUNK, ROWS), ROWS)
            dbs_acc[pos, :] += dvm
        _row_loop(tm, phase7, unroll=FULL_UNROLL)

        for ch in range(n_chunks):
            r0 = ch * CHUNK
            for h in range(HEADS):
                c0 = h * CHUNK
                dvm_b = dvm_scr[r0:r0 + CHUNK, c0:c0 + CHUNK]
                dws_ref[h] += _dot_nt(dvm_b, vl_scr[r0:r0 + CHUNK, c0:c0 + CHUNK])
                dvl_scr[r0:r0 + CHUNK, c0:c0 + CHUNK] = _dot_tn(ws_ref[h].astype(BF16), dvm_b)

        def phase9(rows):
            vhat, rstd_v = keep[1, rows, :], rstd_scr[1, rows, :]
            dvl = dvl_scr[rows, :]
            acc_scr[4] += _colsum8(dvl * vhat)
            acc_scr[5] += _colsum8(dvl)
            dvh = dvl * sg
            dz_ref[2, rows, :] = (rstd_v * (dvh - _mean(dvh) - vhat * _mean(dvh * vhat))).astype(BF16)
            qhat, rstd_q = keep[0, rows, :], rstd_scr[0, rows, :]
            ln = qhat * cg + cb
            sig_ln = keep[2, rows, :]
            gz = z_ref[0, rows, :]
            sig_g = keep[3, rows, :]
            dya = dycat_scr[rows, 0:D_MODEL]
            dz_ref[0, rows, :] = (dya * (ln * sig_ln) * (sig_g * (1.0 + gz * (1.0 - sig_g)))).astype(BF16)
            dln = (dya * (gz * sig_g)) * (sig_ln * (1.0 + ln * (1.0 - sig_ln)))
            acc_scr[2] += _colsum8(dln * qhat)
            acc_scr[3] += _colsum8(dln)
            dqh = dln * cg
            dq = rstd_q * (dqh - _mean(dqh) - qhat * _mean(dqh * qhat))
            for blk in range(D_MODEL // 128):
                dq_ref[blk, rows, :] = dq[:, blk * 128:(blk + 1) * 128]
        _row_loop(tm, phase9, unroll=FULL_UNROLL)

        @pl.when(i == n_steps - 1)
        def _():
            for qi in range(8):
                scale = 0.5 * inv_d if qi == 6 else 1.0
                sm_ref[qi:qi + 1, :] = jnp.sum(acc_scr[qi], axis=0, keepdims=True) * scale
            lane = lax.broadcasted_iota(jnp.int32, (CHUNK, CHUNK), 1)
            tile = jnp.zeros((CHUNK, CHUNK), F32)
            for h in range(HEADS):
                col = jnp.sum(dbs_acc[:, h * CHUNK:(h + 1) * CHUNK], axis=1, keepdims=True)
                tile = jnp.where(lane == h, col, tile)
            dbs_ref[...] = tile

    tok = lambda i: (i, 0)
    const2 = lambda i: (0, 0)
    return pl.pallas_call(
        body, name="middle",
        grid=(n_steps,),
        out_shape=(jax.ShapeDtypeStruct((4, s_len, D_MODEL), BF16),
                   jax.ShapeDtypeStruct((D_MODEL // 128, s_len, 128), F32),
                   jax.ShapeDtypeStruct((s_len, D_MODEL), F32),
                   jax.ShapeDtypeStruct((s_len, 2 * D_MODEL), BF16),
                   jax.ShapeDtypeStruct((s_len, D_MODEL), BF16),
                   jax.ShapeDtypeStruct((HEADS, CHUNK, CHUNK), F32),
                   jax.ShapeDtypeStruct((8, D_MODEL), F32),
                   jax.ShapeDtypeStruct((CHUNK, CHUNK), F32)),
        in_specs=[pl.BlockSpec((tm, D_MODEL), tok),
                  pl.BlockSpec((4, tm, D_MODEL), lambda i: (0, i, 0)),
                  pl.BlockSpec((D_MODEL // 128, tm, 128), lambda i: (0, i, 0)),
                  pl.BlockSpec((tm, D_MODEL), tok),
                  *[pl.BlockSpec((1, D_MODEL), const2) for _ in range(5)],
                  pl.BlockSpec((1, 3 * D_MODEL), const2),
                  pl.BlockSpec((HEADS, CHUNK, CHUNK), lambda i: (0, 0, 0)),
                  pl.BlockSpec((CHUNK, D_MODEL), const2),
                  pl.BlockSpec((2 * D_MODEL, D_MODEL), const2, pipeline_mode=pl.Buffered(1))],
        out_specs=(pl.BlockSpec((4, tm, D_MODEL), lambda i: (0, i, 0)),
                   pl.BlockSpec((D_MODEL // 128, tm, 128), lambda i: (0, i, 0)),
                   pl.BlockSpec((tm, D_MODEL), tok),
                   pl.BlockSpec((tm, 2 * D_MODEL), tok),
                   pl.BlockSpec((tm, D_MODEL), tok),
                   pl.BlockSpec((HEADS, CHUNK, CHUNK), lambda i: (0, 0, 0)),
                   pl.BlockSpec((8, D_MODEL), const2),
                   pl.BlockSpec((CHUNK, CHUNK), const2)),
        scratch_shapes=[pltpu.VMEM((tm, D_MODEL), BF16),
                        pltpu.VMEM((tm, D_MODEL), F32),
                        pltpu.VMEM((tm, D_MODEL), F32),
                        pltpu.VMEM((tm, 2 * D_MODEL), F32),
                        pltpu.VMEM((tm, D_MODEL), BF16),
                        pltpu.VMEM((tm, D_MODEL), F32),
                        pltpu.VMEM((8, 8, D_MODEL), F32),
                        pltpu.VMEM((CHUNK, D_MODEL), F32),
                        pltpu.VMEM((5, tm, D_MODEL), F32),
                        pltpu.VMEM((2, tm, 1), F32)],
        compiler_params=_params(dimension_semantics=("arbitrary",)),
    )(x, z6, q, target, *ln_rows, mod, w_s, bs_exp, w_out)


def _conv_bwd(kidx, dq, sig, t2, cw4, sm1, dws, dbs, ycat, dy):
    s_len = dq.shape[1]
    tt = TIME_TILE
    n_blocks = D_MODEL // 128
    shp = [(16, D_MODEL), (D_MODEL, 128), (128, 128)]
    nj, nr, nc = 1, 256, D_MODEL
    kt = n_blocks // N_CHIPS
    tk = s_len // kt

    def body(k_ref, dq_ref, sig_ref, t2_ref, cw_ref, sm1_ref, dws_in, dbs_in, ycat_ref, dy_ref,
             dz_ref, pk1_ref, dws_ref, dbs_ref, loss_ref, gout_ref,
             dqpad, b_pk, b_dws, b_dbs, r_pk, r_dws, r_dbs, w_dws, ssem, rsem, *rs_scr):
        del k_ref
        jb = pl.program_id(0)
        x_, y_, c_ = _pos()

        @pl.when(jb == 0)
        def _():
            b_pk[0:8, :] = _onehot_rows(sm1_ref[1:2, :], 4 * x_ + 2 * y_ + c_)
            for r, src in enumerate([2, 3, 4, 5, 0, 6]):
                b_pk[8 + r:9 + r, :] = sm1_ref[src:src + 1, :]
            b_pk[14:16, :] = jnp.zeros((2, D_MODEL), F32)
            b_dws[...] = dws_in[...]
            b_dbs[...] = dbs_in[...]

        for e, step in enumerate([0, 0, (3 * n_blocks) // 8, (6 * n_blocks) // 8]):
            @pl.when(jb == step)
            def _():
                _butterfly_event(e, [b_pk, b_dws, b_dbs], [r_pk, r_dws, r_dbs], ssem, rsem, _all_reduce_partners(),
                                 wires=[None, w_dws, None])

        @pl.when(jb == (6 * n_blocks) // 8)
        def _():
            pk1_ref[...] = b_pk[...]
            dws_ref[...] = b_dws[...]
            dbs_ref[...] = b_dbs[...]
            loss_ref[...] = jnp.sum(b_pk[13:14, :], axis=1, keepdims=True)

        zero = jnp.zeros((CONV_PAD, 128), F32)
        dqpad[0:CONV_PAD, :] = zero
        dqpad[s_len + CONV_PAD:s_len + 2 * CONV_PAD, :] = zero
        dqpad[CONV_PAD:s_len + CONV_PAD, :] = dq_ref[0]
        w = cw_ref[0]

        def bwd(i, carry):
            t0 = i * tt
            dp = jnp.zeros((tt, 128), F32)
            for o in range(1, CONV_WIDTH + 1):
                dp = dp + w[CONV_WIDTH - o:CONV_WIDTH - o + 1, :] * dqpad[pl.ds(t0 + o, tt), :]
            dz_ref[0, 0, pl.ds(t0, tt), :] = (dp * sig_ref[0, pl.ds(t0, tt), :]).astype(BF16)
            dz_ref[0, 1, pl.ds(t0, tt), :] = (dp * t2_ref[0, pl.ds(t0, tt), :]).astype(BF16)
            return carry

        n_iter = s_len // tt
        n_groups = min(8, n_iter)
        rows = tk // n_groups

        def conv_and_matmul(add):
            for g in range(n_groups):
                for it in range(g * n_iter // n_groups, (g + 1) * n_iter // n_groups):
                    bwd(it, 0)
                k0 = pl.multiple_of((jb % kt) * tk + g * rows, rows)
                add(_dot_tn(ycat_ref[g * rows:(g + 1) * rows, :], dy_ref[pl.ds(k0, rows), :]))
        _rs_step(jb // kt, jb % kt, kt, nj, nr, nc, conv_and_matmul, gout_ref, rs_scr)

    const2 = lambda j, k_ref: (0, 0)
    grid_spec = pltpu.PrefetchScalarGridSpec(
        num_scalar_prefetch=1,
        grid=(n_blocks,),
        in_specs=[*[pl.BlockSpec((1, s_len, 128), lambda j, k_ref: (j, 0, 0)) for _ in range(3)],
                  pl.BlockSpec((1, 32, 128), lambda j, k_ref: (j // 2, 0, j % 2)),
                  pl.BlockSpec((8, D_MODEL), const2),
                  pl.BlockSpec((D_MODEL, 128), const2),
                  pl.BlockSpec((128, 128), const2),
                  pl.BlockSpec((tk, 512), lambda j, k_ref: (j % kt, _rs_block(j // kt, k_ref[0], nj))),
                  pl.BlockSpec((s_len, D_MODEL), const2, pipeline_mode=pl.Buffered(1))],
        out_specs=(pl.BlockSpec((1, 2, s_len, 128), lambda j, k_ref: (j, 0, 0, 0)),)
        + tuple(pl.BlockSpec(s, const2) for s in shp) + (pl.BlockSpec((1, 1), const2),
                                                         pl.BlockSpec(memory_space=pl.ANY)),
        scratch_shapes=[pltpu.VMEM((s_len + 2 * CONV_PAD, 128), F32)]
        + [pltpu.VMEM(s, F32) for s in shp]
        + [pltpu.VMEM((3,) + shp[0], F32), pltpu.VMEM((3,) + shp[1], BF16), pltpu.VMEM((3,) + shp[2], F32),
           pltpu.VMEM(shp[1], BF16)]
        + [pltpu.SemaphoreType.DMA((9,)), pltpu.SemaphoreType.DMA((9,))] + _rs_scratch(nj, nr, nc))
    return pl.pallas_call(
        body, name="conv_bwd",
        grid_spec=grid_spec,
        out_shape=(jax.ShapeDtypeStruct((n_blocks, 2, s_len, 128), BF16),)
        + tuple(jax.ShapeDtypeStruct(s, F32) for s in shp) + (jax.ShapeDtypeStruct((1, 1), F32),
                                                              jax.ShapeDtypeStruct((nj, 2, nr, nc), F32)),
        compiler_params=_params(dimension_semantics=("arbitrary",)),
    )(kidx, dq, sig, t2, cw4, sm1, dws, dbs, ycat, dy)


def _bwd_in(dz4, dzc, w12, x, dx2, mod, norm_g, dq, p):
    s_len = x.shape[0]
    tm = TOKEN_TILE
    n_steps = s_len // tm
    tt = TIME_TILE
    n_cblocks = D_MODEL // 128
    parts = max(1, n_steps // n_cblocks)
    nsub = max(1, n_cblocks // n_steps)
    cw = 128 * nsub
    tiles = (s_len // tt) // parts

    def body(dz_ref, dzc_ref, w_ref, x_ref, dx2_ref, mod_ref, g_ref, dq_ref, p_ref, gx_ref, sm_ref, dcw_ref,
             dh_scr, acc_scr, ppad, wacc):
        i = pl.program_id(0)
        part = i % parts

        @pl.when(i == 0)
        def _():
            acc_scr[...] = jnp.zeros_like(acc_scr)

        @pl.when(part == 0)
        def _():
            zero = jnp.zeros((CONV_PAD, 128), F32)
            for sub in range(nsub):
                ppad[sub, 0:CONV_PAD, :] = zero
                ppad[sub, s_len + CONV_PAD:s_len + 2 * CONV_PAD, :] = zero
                ppad[sub, CONV_PAD:s_len + CONV_PAD, :] = p_ref[sub]
            wacc[...] = jnp.zeros_like(wacc)

        def dw_tile(tile):
            t0 = pl.multiple_of((part * tiles + tile) * tt, tt)
            for sub in range(nsub):
                dqt = dq_ref[sub, pl.ds(t0, tt), :]
                for o in range(1, CONV_WIDTH + 1):
                    wacc[sub, o - 1] += _colsum8(dqt * ppad[sub, pl.ds(t0 + o, tt), :])
                wacc[sub, CONV_WIDTH] += _colsum8(dqt)

        def dz_cols(j, hf):
            if j < 4:
                return dz_ref[j, :, hf * 512:(hf + 1) * 512]
            return jnp.concatenate([dzc_ref[4 * hf + blk, j - 4] for blk in range(4)], axis=1)

        dots = [(j, hf) for j in range(N_GROUPS) for hf in range(2)]
        dh = jnp.zeros((tm, D_MODEL), F32)
        for d, (j, hf) in enumerate(dots):
            dh = dh + _dot_nt(dz_cols(j, hf), w_ref[2 * _natural_group(j) + hf])
            for tile in range(d * tiles // len(dots), (d + 1) * tiles // len(dots)):
                dw_tile(tile)
        dh_scr[...] = dh

        @pl.when(part == parts - 1)
        def _():
            for sub in range(nsub):
                for k in range(32):
                    dcw_ref[k:k + 1, sub * 128:(sub + 1) * 128] = jnp.sum(wacc[sub, k], axis=0, keepdims=True)

        scale1 = 1.0 + mod_ref[:, D_MODEL:2 * D_MODEL]
        g = g_ref[...]

        def rows_fn(rows):
            xt = x_ref[rows, :]
            r = lax.rsqrt(_mean(xt * xt) + EPS)
            xn = xt * r
            dhr = dh_scr[rows, :]
            acc_scr[0] += _colsum8(dhr)
            acc_scr[1] += _colsum8(dhr * (xn * g))
            acc_scr[2] += _colsum8(dhr * scale1 * xn)
            dxn = dhr * (g * scale1)
            gx_ref[rows, :] = dx2_ref[rows, :] + r * (dxn - xn * _mean(dxn * xn))
        _row_loop(tm, rows_fn, unroll=FULL_UNROLL)

        @pl.when(i == n_steps - 1)
        def _():
            for qi in range(8):
                sm_ref[qi:qi + 1, :] = jnp.sum(acc_scr[qi], axis=0, keepdims=True)

    tok = lambda i: (i, 0)
    const2 = lambda i: (0, 0)
    return pl.pallas_call(
        body, name="bwd_in",
        grid=(n_steps,),
        out_shape=(jax.ShapeDtypeStruct((s_len, D_MODEL), F32), jax.ShapeDtypeStruct((8, D_MODEL), F32),
                   jax.ShapeDtypeStruct((32, D_MODEL), F32)),
        in_specs=[pl.BlockSpec((4, tm, D_MODEL), lambda i: (0, i, 0)),
                  pl.BlockSpec((n_cblocks, 2, tm, 128), lambda i: (0, 0, i, 0)),
                  pl.BlockSpec((12, D_MODEL, 512), lambda i: (0, 0, 0), pipeline_mode=pl.Buffered(1)),
                  pl.BlockSpec((tm, D_MODEL), tok),
                  pl.BlockSpec((tm, D_MODEL), tok),
                  pl.BlockSpec((1, 3 * D_MODEL), const2),
                  pl.BlockSpec((1, D_MODEL), const2),
                  pl.BlockSpec((nsub, s_len, 128), lambda i: (i // parts, 0, 0)),
                  pl.BlockSpec((nsub, s_len, 128), lambda i: (i // parts, 0, 0))],
        out_specs=(pl.BlockSpec((tm, D_MODEL), tok), pl.BlockSpec((8, D_MODEL), const2),
                   pl.BlockSpec((32, cw), lambda i: (0, i // parts))),
        scratch_shapes=[pltpu.VMEM((tm, D_MODEL), F32), pltpu.VMEM((8, 8, D_MODEL), F32),
                        pltpu.VMEM((nsub, s_len + 2 * CONV_PAD, 128), F32), pltpu.VMEM((nsub, 32, 8, 128), F32)],
        compiler_params=_params(dimension_semantics=("arbitrary",)),
    )(dz4, dzc, w12, x, dx2, mod, norm_g, dq, p)


def _rs_shard_block(i, nj):
    g = jnp.where(i < 2 * nj, i % 2, jnp.where(i < 3 * nj, 2, 3))
    j = jnp.where(i < 2 * nj, i // 2, jnp.where(i < 3 * nj, i - 2 * nj, i - 3 * nj))
    return g, j


def _rs_block(i, k, nj):
    g, j = _rs_shard_block(i, nj)
    return nj * jnp.bitwise_xor(k, 3 - g) + j


def _rs_scratch(nj, nr, nc):
    nblk = N_CHIPS * nj
    return [pltpu.VMEM((2, 2, nr, nc), F32),
            pltpu.VMEM((nblk, nr, nc), F32),
            pltpu.VMEM((2 * nj, nr, nc), BF16),
            pltpu.VMEM((2 * nj, nr, nc), BF16),
            pltpu.VMEM((nj, nr, nc), BF16),
            pltpu.VMEM((nj, nr, nc), BF16),
            pltpu.VMEM((nj, nr, nc), F32),
            pltpu.SemaphoreType.DMA,
            pltpu.SemaphoreType.DMA((nblk,)), pltpu.SemaphoreType.DMA((nblk,)),
            pltpu.SemaphoreType.DMA((2 * nj,)), pltpu.SemaphoreType.DMA((2 * nj,)),
            pltpu.SemaphoreType.DMA((nj,)), pltpu.SemaphoreType.DMA((nj,)),
            pltpu.SemaphoreType.DMA, pltpu.SemaphoreType.DMA]


def _rs_step(i, kk, kt, nj, nr, nc, partial_fn, out_ref, scr):
    nblk = N_CHIPS * nj
    (acc, recv_a, send1, recv1, send2, recv2, own_ps,
     lsem, psend, precv, s1send, s1recv, s2send, s2recv, fsend, frecv) = scr
    x, y, c = _pos()
    sib = (x, y, 1 - c)
    first = (jnp.bitwise_xor(x, 1 - c), jnp.bitwise_xor(y, c), c)
    second = (jnp.bitwise_xor(x, c), jnp.bitwise_xor(y, 1 - c), c)
    slot = i % 2

    @pl.when(kk == 0)
    def _():
        acc[slot] = jnp.zeros((2, nr, nc), F32)

    def pair_copy(ii, sl):
        return _rcopy(acc.at[sl, 1 - c], recv_a.at[ii], psend.at[ii], precv.at[ii], sib)

    def pair_sum(ii, sl):
        cp = pair_copy(ii, sl)
        cp.wait_recv()
        cp.wait_send()
        return acc[sl, c] + recv_a[ii]

    def stage1_copy(src, dst):
        return _rcopy(send1.at[src], recv1.at[dst], s1send.at[src], s1recv.at[dst], first)

    def stage2_copy(j):
        return _rcopy(send2.at[j], recv2.at[j], s2send.at[j], s2recv.at[j], second)

    def finalize(ii, sl):
        ps = pair_sum(ii, sl)
        g, j = _rs_shard_block(ii, nj)

        @pl.when(g == 0)
        def _():
            send1[j] = ps.astype(BF16)
            stage1_copy(j, nj + j).start()

        @pl.when(g == 1 + c)
        def _():
            send1[nj + j] = ps.astype(BF16)
            stage1_copy(nj + j, j).start()

        @pl.when(g == 2 - c)
        def _():
            stage1_copy(0, nj + j).wait_recv()
            send2[j] = (ps + recv1[nj + j].astype(F32)).astype(BF16)
            stage2_copy(j).start()

        @pl.when(g == 3)
        def _():
            own_ps[j] = ps

    @pl.when(jnp.logical_and(kk == kt - 1, i >= 1))
    def _():
        finalize(i - 1, 1 - slot)

    def add_partial(res):
        acc[slot, 0] += res[0:nr]
        acc[slot, 1] += res[nr:2 * nr]
    partial_fn(add_partial)

    @pl.when(kk == kt - 1)
    def _():
        pair_copy(i, slot).start()

        @pl.when(i == nblk - 1)
        def _():
            own_ps[nj - 1] = pair_sum(nblk - 1, (nblk - 1) % 2)
            for j in range(nj):
                stage1_copy(0, j).wait_recv()
                stage2_copy(j).wait_recv()
                own_ps[j] = (own_ps[j] + recv1[j].astype(F32)) + recv2[j].astype(F32)
            loc = pltpu.make_async_copy(own_ps, out_ref.at[:, c], lsem)
            loc.start()
            swap = _rcopy(own_ps, out_ref.at[:, c], fsend, frecv, sib)
            swap.start()
            loc.wait()
            swap.wait_send()
            _rcopy(own_ps, out_ref.at[:, 1 - c], fsend, frecv, sib).wait_recv()
            for s in range(2 * nj):
                stage1_copy(s, 0).wait_send()
            for j in range(nj):
                stage2_copy(j).wait_send()


def _grad_w_in(kidx, h, dz4, dzc, sm2, dcw):
    s_len = h.shape[0]
    tk = min(K_TILE, s_len)
    kt = s_len // tk
    nj, nr, nc = 3, 512, 512
    nblk = N_CHIPS * nj

    def body(k_ref, a_ref, b_ref, bc_ref, sm2_ref, dcw_in, out_ref, pk2_ref, dcwsh_ref, dcb_ref, *scr):
        del k_ref
        b_pk2, b_dcw, r_pk2, r_dcw, bsend, brecv = scr[-6:]
        i = pl.program_id(0)
        kk = pl.program_id(1)
        x, y, c = _pos()
        k = 2 * x + y
        m = _rs_block(i, k, nj)

        @pl.when(jnp.logical_and(kk == 0, i == 0))
        def _():
            b = 4 * x + 2 * y + c
            b_pk2[0:8, :] = _onehot_rows(sm2_ref[0:1, :], b)
            b_pk2[8:16, :] = _onehot_rows(sm2_ref[1:2, :], b)
            b_pk2[16:17, :] = sm2_ref[2:3, :]
            b_pk2[17:24, :] = jnp.zeros((7, D_MODEL), F32)
            b_dcw[...] = dcw_in[...]

        for e in range(4):
            @pl.when(jnp.logical_and(kk == 0, i == (e * nblk) // 4))
            def _():
                _butterfly_event(e, [b_pk2, b_dcw], [r_pk2, r_dcw], bsend, brecv, _all_reduce_partners())

        @pl.when(jnp.logical_and(kk == 0, i == (3 * nblk) // 4))
        def _():
            pk2_ref[...] = b_pk2[...]
            sel = jnp.zeros((32, 256), F32)
            for kc in range(N_CHIPS):
                sel = jnp.where(k == kc, b_dcw[:, kc * 256:(kc + 1) * 256], sel)
            dcwsh_ref[...] = sel
            dcb_ref[...] = b_dcw[31:32, :]

        def partial(add):
            rows = pl.ds(pl.multiple_of(kk * tk, tk), tk)

            @pl.when(m >= 4)
            def _():
                add(_dot_tn(a_ref[rows, :], b_ref[0]))

            @pl.when(m < 4)
            def _():
                add(_dot_tn(a_ref[rows, :], jnp.concatenate([bc_ref[blk, 0] for blk in range(4)], axis=1)))
        _rs_step(i, kk, kt, nj, nr, nc, partial, out_ref, scr[:-6])

    def dz4_index(i, kk, k_ref):
        m = _rs_block(i, k_ref[0], nj)
        use = m >= 4
        return (jnp.where(use, m // 2 - 2, 0), jnp.where(use, kk, 0), jnp.where(use, m % 2, 0))

    def dzc_index(i, kk, k_ref):
        m = _rs_block(i, k_ref[0], nj)
        use = m < 4
        return (jnp.where(use, m % 2, 0), jnp.where(use, m // 2, 0), jnp.where(use, kk, 0), 0)

    const2 = lambda i, kk, k_ref: (0, 0)
    small_out = [(24, D_MODEL), (32, 256), (1, D_MODEL)]
    small_buf = [(24, D_MODEL), (32, D_MODEL)]
    grid_spec = pltpu.PrefetchScalarGridSpec(
        num_scalar_prefetch=1,
        grid=(nblk, kt),
        in_specs=[pl.BlockSpec((s_len, D_MODEL), const2, pipeline_mode=pl.Buffered(1)),
                  pl.BlockSpec((1, tk, 512), dz4_index),
                  pl.BlockSpec((4, 1, tk, 128), dzc_index),
                  pl.BlockSpec((8, D_MODEL), const2),
                  pl.BlockSpec((32, D_MODEL), const2)],
        out_specs=(pl.BlockSpec(memory_space=pl.ANY),) + tuple(pl.BlockSpec(s, const2) for s in small_out),
        scratch_shapes=_rs_scratch(nj, nr, nc)
        + [pltpu.VMEM(s, F32) for s in small_buf] + [pltpu.VMEM((3,) + s, F32) for s in small_buf]
        + [pltpu.SemaphoreType.DMA((6,)), pltpu.SemaphoreType.DMA((6,))])
    return pl.pallas_call(
        body, name="grad_w_in",
        grid_spec=grid_spec,
        out_shape=(jax.ShapeDtypeStruct((nj, 2, nr, nc), F32),) + tuple(jax.ShapeDtypeStruct(s, F32) for s in small_out),
        compiler_params=_params(dimension_semantics=("arbitrary", "arbitrary")),
    )(kidx, h, dz4, dzc, sm2, dcw)


def _adamw_math(w, g, m, v):
    m = ADAM_B1 * m + (1.0 - ADAM_B1) * g
    v = ADAM_B2 * v + (1.0 - ADAM_B2) * (g * g)
    m_hat = m / (1.0 - ADAM_B1 ** ADAM_STEP)
    v_hat = v / (1.0 - ADAM_B2 ** ADAM_STEP)
    delta = -ADAM_LR * (m_hat / (jnp.sqrt(v_hat) + ADAM_EPS) + ADAM_WD * w)
    return delta, m, v


def _adamw_blocked(w, m, v, g4, name):
    nj, _, nr, nc = g4.shape

    def body(w_ref, m_ref, v_ref, g_ref, go_ref, d_ref, mo_ref, vo_ref):
        g = g_ref[0, 0]
        d, mn, vn = _adamw_math(w_ref[...], g, m_ref[...], v_ref[...])
        go_ref[...] = g
        d_ref[...] = d
        mo_ref[...] = mn
        vo_ref[...] = vn

    blk = pl.BlockSpec((nr, nc), lambda j, hf: (hf, j))
    return pl.pallas_call(
        body, name=name,
        grid=(nj, 2),
        out_shape=tuple(jax.ShapeDtypeStruct(w.shape, F32) for _ in range(4)),
        in_specs=[blk, blk, blk, pl.BlockSpec((1, 1, nr, nc), lambda j, hf: (j, hf, 0, 0))],
        out_specs=(blk, blk, blk, blk),
        compiler_params=_params(dimension_semantics=("arbitrary", "arbitrary")),
    )(w, m, v, g4)


def _adamw_w_ada(cact, pk1, pk2, w, m, v):
    rb = 256

    def body(cact_ref, pk1_ref, pk2_ref, w_ref, m_ref, v_ref, g_ref, d_ref, mo_ref, vo_ref, gb_ref, dmod, sel_scr):
        @pl.when(pl.program_id(0) == 0)
        def _():
            x, y, _ = _pos()
            k = 2 * x + y
            dmod[:, 0:D_MODEL] = pk2_ref[0:8, :]
            dmod[:, D_MODEL:2 * D_MODEL] = pk2_ref[8:16, :]
            dmod[:, 2 * D_MODEL:3 * D_MODEL] = pk1_ref[0:8, :]
            gb_ref[...] = jnp.sum(dmod[...], axis=0, keepdims=True)
            sel = jnp.zeros((8, 768), F32)
            for kk in range(N_CHIPS):
                sel = jnp.where(k == kk, dmod[:, kk * 768:(kk + 1) * 768], sel)
            sel_scr[...] = sel

        g = _dot_tn(cact_ref[...].astype(BF16), sel_scr[...].astype(BF16))
        d, mn, vn = _adamw_math(w_ref[...], g, m_ref[...], v_ref[...])
        g_ref[...] = g
        d_ref[...] = d
        mo_ref[...] = mn
        vo_ref[...] = vn

    blk = pl.BlockSpec((rb, 768), lambda i: (i, 0))
    const2 = lambda i: (0, 0)
    return pl.pallas_call(
        body, name="adamw_w_ada",
        grid=(D_MODEL // rb,),
        out_shape=tuple(jax.ShapeDtypeStruct(w.shape, F32) for _ in range(4)) + (
            jax.ShapeDtypeStruct((1, 3 * D_MODEL), F32),),
        in_specs=[pl.BlockSpec((8, rb), lambda i: (0, i)), pl.BlockSpec((16, D_MODEL), const2),
                  pl.BlockSpec((24, D_MODEL), const2), blk, blk, blk],
        out_specs=(blk, blk, blk, blk, pl.BlockSpec((1, 3 * D_MODEL), const2)),
        scratch_shapes=[pltpu.VMEM((8, 3 * D_MODEL), F32), pltpu.VMEM((8, 768), F32)],
        compiler_params=_params(dimension_semantics=("arbitrary",)),
    )(cact, pk1, pk2, w, m, v)


def _adamw_small(items, conv, packs, rows):
    n, nr = len(items), len(rows)

    def body(*refs):
        refs = list(refs)
        take = lambda k: [refs.pop(0) for _ in range(k)]
        ins, (cw_ref, cg_ref, cm_ref, cv_ref), pk_refs, row_ins = take(4 * n), take(4), take(len(packs)), take(3 * nr)
        outs, (cgo_ref, cd_ref, cmo_ref, cvo_ref), row_outs = take(3 * n), take(4), take(4 * nr)
        for i in range(n):
            w_ref, g_ref, m_ref, v_ref = ins[4 * i:4 * i + 4]
            d, mn, vn = _adamw_math(w_ref[...], g_ref[...], m_ref[...], v_ref[...])
            outs[3 * i][...] = d
            outs[3 * i + 1][...] = mn
            outs[3 * i + 2][...] = vn
        for r in range(CONV_WIDTH):
            g = cg_ref[r:r + 1, :]
            d, mn, vn = _adamw_math(cw_ref[0, r], g, cm_ref[0, r], cv_ref[0, r])
            cgo_ref[0, r] = g
            cd_ref[0, r] = d
            cmo_ref[0, r] = mn
            cvo_ref[0, r] = vn
        for i, (_, _, _, p, r) in enumerate(rows):
            w_ref, m_ref, v_ref = row_ins[3 * i:3 * i + 3]
            g = pk_refs[p][r:r + 1, :]
            d, mn, vn = _adamw_math(w_ref[...], g, m_ref[...], v_ref[...])
            row_outs[4 * i][...] = g
            row_outs[4 * i + 1][...] = d
            row_outs[4 * i + 2][...] = mn
            row_outs[4 * i + 3][...] = vn

    flat = [a for it in items for a in it] + list(conv) + list(packs) + [a for it in rows for a in it[:3]]
    n_out = 3 * n + 4 + 4 * nr
    outs = pl.pallas_call(
        body, name="adamw_small",
        out_shape=tuple(jax.ShapeDtypeStruct(it[0].shape, F32) for it in items for _ in range(3))
        + tuple(jax.ShapeDtypeStruct(conv[0].shape, F32) for _ in range(4))
        + tuple(jax.ShapeDtypeStruct(it[0].shape, F32) for it in rows for _ in range(4)),
        in_specs=[_vmem()] * len(flat),
        out_specs=tuple(_vmem() for _ in range(n_out)),
        compiler_params=_params(),
    )(*flat)
    return ([tuple(outs[3 * i:3 * i + 3]) for i in range(n)], tuple(outs[3 * n:3 * n + 4]),
            [tuple(outs[3 * n + 4 + 4 * i:3 * n + 8 + 4 * i]) for i in range(nr)])


def kernel(x, c, w_ada, b_ada, norm_g, w_in, conv_w, conv_b, conv_ln_g, conv_ln_b, sg_ln_g, sg_ln_b, w_s, b_s, w_out, final_g, loss_target, m_w_ada, m_b_ada, m_norm_g, m_w_in, m_conv_w, m_conv_b, m_conv_ln_g, m_conv_ln_b, m_sg_ln_g, m_sg_ln_b, m_w_s, m_b_s, m_w_out, m_final_g, v_w_ada, v_b_ada, v_norm_g, v_w_in, v_conv_w, v_conv_b, v_conv_ln_g, v_conv_ln_b, v_sg_ln_g, v_sg_ln_b, v_w_s, v_b_s, v_w_out, v_final_g):
    s_len = x.shape[1]
    x2d = x[0]
    tgt = loss_target[0]
    row = lambda a: a.reshape(1, -1)

    kidx = (2 * lax.axis_index("x") + lax.axis_index("y")).astype(jnp.int32).reshape(1)
    cw_sh = jnp.pad(conv_w.reshape(CONV_WIDTH, 256), ((0, 1), (0, 0)))
    h, z6, mod, cact, cw4, w_in12 = _front(kidx, x2d, c, w_ada[0], b_ada, norm_g, w_in[0], cw_sh)
    w12 = w_in12.reshape(12, D_MODEL, 512)
    q, p_cm, sig_cm, t2_cm, w_out4 = _conv_fwd(z6, cw4, conv_b, w_out[0])
    w_out_full = w_out4.reshape(2 * D_MODEL, D_MODEL)
    ln_rows = (conv_ln_g, conv_ln_b, sg_ln_g, sg_ln_b, row(final_g))
    bs_exp = jnp.repeat(b_s[0].T, CHUNK, axis=1)
    dz4, dq, dx2, ycat, dy, dws, sm1, dbs = _middle(x2d, z6, q, tgt, ln_rows, mod, w_s[0], bs_exp, w_out_full)
    dzc, pk1, dws_r, dbs_r, loss_t, g_w_out4 = _conv_bwd(
        kidx, dq, sig_cm, t2_cm, cw4, sm1, dws.reshape(D_MODEL, CHUNK), dbs, ycat, dy)
    grad_x, sm2, dcw = _bwd_in(dz4, dzc, w12, x2d, dx2, mod, norm_g, dq, p_cm)
    g_w_in4, pk2, dcw_sh, dcb = _grad_w_in(kidx, h, dz4, dzc, sm2, dcw)

    g_w_in, d_w_in, nm_w_in, nv_w_in = _adamw_blocked(w_in[0], m_w_in[0], v_w_in[0], g_w_in4, "adamw_w_in")
    g_w_out, d_w_out, nm_w_out, nv_w_out = _adamw_blocked(w_out[0], m_w_out[0], v_w_out[0], g_w_out4, "adamw_w_out")
    g_w_ada, d_w_ada, nm_w_ada, nv_w_ada, g_b_ada = _adamw_w_ada(cact, pk1, pk2, w_ada[0], m_w_ada[0], v_w_ada[0])

    loss = loss_t.reshape(())
    g_w_s = dws_r
    g_b_s = dbs_r[:, :HEADS].T
    small = [
        (b_ada, g_b_ada, m_b_ada, v_b_ada),
        (conv_b, dcb, m_conv_b, v_conv_b),
        (w_s[0], g_w_s.reshape(HEADS, CHUNK, CHUNK), m_w_s[0], v_w_s[0]),
        (b_s[0], g_b_s, m_b_s[0], v_b_s[0]),
    ]
    rows = [
        (norm_g, m_norm_g, v_norm_g, 1, 16),
        (conv_ln_g, m_conv_ln_g, v_conv_ln_g, 0, 8),
        (conv_ln_b, m_conv_ln_b, v_conv_ln_b, 0, 9),
        (sg_ln_g, m_sg_ln_g, v_sg_ln_g, 0, 10),
        (sg_ln_b, m_sg_ln_b, v_sg_ln_b, 0, 11),
        (row(final_g), row(m_final_g), row(v_final_g), 0, 12),
    ]
    upd, (g_conv_w, d_conv_w, nm_conv_w, nv_conv_w), row_upd = _adamw_small(
        small, (conv_w, dcw_sh, m_conv_w, v_conv_w), (pk1, pk2), rows)
    g_norm_g, g_cln_g, g_cln_b, g_sln_g, g_sln_b, g_final = (t[0] for t in row_upd)

    shapes = [w_ada.shape, b_ada.shape, norm_g.shape, w_in.shape, conv_w.shape, conv_b.shape, conv_ln_g.shape,
              conv_ln_b.shape, sg_ln_g.shape, sg_ln_b.shape, w_s.shape, b_s.shape, w_out.shape, final_g.shape]
    grads = [g_w_ada, g_b_ada, g_norm_g, g_w_in, g_conv_w, dcb, g_cln_g, g_cln_b, g_sln_g, g_sln_b, g_w_s, g_b_s,
             g_w_out, g_final]
    big = {0: (d_w_ada, nm_w_ada, nv_w_ada), 3: (d_w_in, nm_w_in, nv_w_in), 4: (d_conv_w, nm_conv_w, nv_conv_w),
           12: (d_w_out, nm_w_out, nv_w_out)}
    trip = [None] * 14
    for i, t in big.items():
        trip[i] = t
    for i, t in zip([1, 5, 10, 11], upd):
        trip[i] = t
    for i, t in zip([2, 6, 7, 8, 9, 13], row_upd):
        trip[i] = t[1:]
    fit = lambda arrs: [a.reshape(s) for a, s in zip(arrs, shapes)]
    return (loss, grad_x.reshape(x.shape), *fit(grads), *fit([t[0] for t in trip]), *fit([t[1] for t in trip]),
            *fit([t[2] for t in trip]))
```

```python
import jax
import jax.numpy as jnp
from jax import lax
from jax.experimental import pallas as pl
from jax.experimental.pallas import tpu as pltpu

F32 = jnp.float32
BF16 = jnp.bfloat16
MESH = pl.DeviceIdType.MESH

D_MODEL = 1024
N_CHIPS = 4
HEADS = 8
CHUNK = 128
CONV_WIDTH = 31
CONV_HALF = CONV_WIDTH // 2
CONV_PAD = 16
EPS = 1e-6
ADAM_LR = 0.001
ADAM_B1 = 0.9
ADAM_B2 = 0.999
ADAM_EPS = 1e-08
ADAM_WD = 0.01
ADAM_STEP = 10

V7X_VMEM_BYTES = 64 * 1024 * 1024
VMEM_LIMIT = V7X_VMEM_BYTES - 8 * 1024 * 1024
ROWS = 16
UNROLL = 8
TOKEN_TILE = 256
FULL_UNROLL = TOKEN_TILE // ROWS
TIME_TILE = 128
K_TILE = 2048

N_GROUPS = 6


def _natural_group(j):
    return (j + 2) % N_GROUPS


def _pos():
    return lax.axis_index("x"), lax.axis_index("y"), lax.axis_index("c")


def _rcopy(src, dst, ssem, rsem, dev):
    return pltpu.make_async_remote_copy(src_ref=src, dst_ref=dst, send_sem=ssem, recv_sem=rsem,
                                        device_id=dev, device_id_type=MESH)


def _vmem():
    return pl.BlockSpec(memory_space=pltpu.VMEM)


def _params(**kw):
    return pltpu.CompilerParams(vmem_limit_bytes=VMEM_LIMIT, **kw)


def _sigmoid(v):
    return 0.5 * jnp.tanh(0.5 * v) + 0.5


def _row_loop(n_rows, body, unroll=1):
    def step(r, carry):
        body(pl.ds(pl.multiple_of(r * ROWS, ROWS), ROWS))
        return carry
    lax.fori_loop(0, n_rows // ROWS, step, 0, unroll=unroll)


def _colsum8(v):
    return v.reshape(v.shape[0] // 8, 8, v.shape[1]).sum(axis=0)


def _mean(v):
    return jnp.mean(v, axis=-1, keepdims=True)


def _dot_nn(a, b):
    return jnp.dot(a, b, preferred_element_type=F32)


def _dot_nt(a, b):
    return lax.dot_general(a, b, (((1,), (1,)), ((), ())), preferred_element_type=F32)


def _dot_tn(a, b):
    return lax.dot_general(a, b, (((0,), (0,)), ((), ())), preferred_element_type=F32)


def _remote_chip(k, r):
    return jnp.bitwise_xor(k, r + 1)


def _front(kidx, x, c, w_ada, b_ada, norm_g, w_in, cw):
    s_len = x.shape[0]
    tmh = min(512, s_len)
    tmz = min(2048, s_len)
    nh = s_len // tmh
    nz = s_len // tmz
    n_steps = nh + 12 * nz

    def remote_block(q):
        return jnp.where(q < 6, q % 2, 2), jnp.where(q < 6, q // 2, q - 6)

    def block_of(i, k):
        r, j = remote_block(jnp.maximum(i - 3, 0))
        return jnp.where(i < 3, 3 * k + i, 3 * _remote_chip(k, r) + j)

    def body(k_ref, x_ref, c_ref, bada_ref, g_ref, cw_ref, wada_hbm, win_hbm,
             h_ref, z_ref, mod_ref, cact_ref, cw4_ref, w12_hbm,
             h_all, wbuf, stage, wada_v, cslab, mslab, cw4_s,
             lsem, csend, crecv, msend, mrecv, wsend, wrecv, isend, irecv, fsend, frecv, osem):
        del k_ref
        t = pl.program_id(0)
        x_, y_, c_ = _pos()
        k = 2 * x_ + y_
        b = 4 * x_ + 2 * y_ + c_
        sib = (x_, y_, 1 - c_)

        def dev_of(r):
            kk = _remote_chip(k, r)
            return (kk // 2, kk % 2, c_)

        def ici(q, kk):
            r, j = remote_block(q)
            return _rcopy(wbuf.at[3 * kk + j, c_], wbuf.at[3 * kk + j, c_], isend.at[q], irecv.at[q], dev_of(r))

        def fwd(q, hf):
            r, j = remote_block(q)
            blk = 3 * _remote_chip(k, r) + j
            return _rcopy(wbuf.at[blk, hf], wbuf.at[blk, hf], fsend.at[q], frecv.at[q], sib)

        def c_copy(q, src):
            d = jnp.bitwise_xor(b, q)
            return _rcopy(cslab.at[src], cslab.at[src], csend.at[q - 1], crecv.at[q - 1], (d // 4, (d // 2) % 2, d % 2))

        def m_copy(r, kk):
            return _rcopy(mslab.at[kk], mslab.at[kk], msend.at[r], mrecv.at[r], dev_of(r))

        def cw_copy(r, kk):
            return _rcopy(cw4_s.at[kk], cw4_s.at[kk], wsend.at[r], wrecv.at[r], dev_of(r))

        def to_hbm(i):
            m = block_of(i, k)
            return pltpu.make_async_copy(wbuf.at[m], w12_hbm.at[m], osem.at[i])

        @pl.when(t == 0)
        def _():
            ld_w = pltpu.make_async_copy(win_hbm, stage, lsem.at[0])
            ld_w.start()
            ld_a = pltpu.make_async_copy(wada_hbm, wada_v, lsem.at[1])
            ld_a.start()
            cslab[b] = jnp.broadcast_to(c_ref[...], (8, D_MODEL))
            for q in range(1, 8):
                c_copy(q, b).start()
            cw4_s[k] = cw_ref[...]
            for r in range(3):
                cw_copy(r, k).start()
            ld_w.wait()
            for j in range(3):
                for hf in range(2):
                    wbuf[3 * k + j, hf] = stage[hf * 512:(hf + 1) * 512, j * 512:(j + 1) * 512].astype(BF16)
            for i in range(3):
                to_hbm(i).start()
            for q in range(1, 8):
                c_copy(q, jnp.bitwise_xor(b, q)).wait_recv()
            row = lax.broadcasted_iota(jnp.int32, (8, D_MODEL), 0)
            call = jnp.zeros((8, D_MODEL), F32)
            for d in range(8):
                call = jnp.where(row == d, cslab[d], call)
            cact = call * _sigmoid(call)
            cact_ref[...] = cact
            ld_a.wait()
            mslab[k] = _dot_nn(cact.astype(BF16), wada_v[...].astype(BF16))
            for r in range(3):
                m_copy(r, k).start()
            for q in range(9):
                ici(q, k).start()
            for r in range(3):
                m_copy(r, _remote_chip(k, r)).wait_recv()
            row8 = lax.broadcasted_iota(jnp.int32, (8, 768), 0)
            for kk in range(N_CHIPS):
                piece = jnp.sum(jnp.where(row8 == b, mslab[kk], 0.0), axis=0, keepdims=True)
                mod_ref[:, kk * 768:(kk + 1) * 768] = piece + bada_ref[:, kk * 768:(kk + 1) * 768]
            for r in range(3):
                cw_copy(r, _remote_chip(k, r)).wait_recv()
            cw4_ref[...] = cw4_s[...]

        @pl.when(t < nh)
        def _():
            shift = mod_ref[:, 0:D_MODEL]
            scale1 = 1.0 + mod_ref[:, D_MODEL:2 * D_MODEL]
            g = g_ref[...]
            base = t * tmh

            def rows_fn(rows):
                xt = x_ref[rows, :]
                r = lax.rsqrt(_mean(xt * xt) + EPS)
                hv = ((xt * r * g) * scale1 + shift).astype(BF16)
                h_ref[rows, :] = hv
                h_all[pl.ds(pl.multiple_of(base + rows.start, ROWS), ROWS), :] = hv
            _row_loop(tmh, rows_fn, unroll=UNROLL)

        @pl.when(t >= nh)
        def _():
            u = t - nh
            i = u // nz
            rt = u % nz
            @pl.when(jnp.logical_and(rt == 0, jnp.logical_and(i >= 2, i <= 10)))
            def _():
                q = i - 2
                r, _ = remote_block(q)
                ici(q, _remote_chip(k, r)).wait_recv()
                fwd(q, c_).start()

            @pl.when(jnp.logical_and(rt == 0, i >= 3))
            def _():
                fwd(i - 3, 1 - c_).wait_recv()
                to_hbm(i).start()
            m = block_of(i, k)
            hb = h_all[pl.ds(pl.multiple_of(rt * tmz, tmz), tmz), :]
            z_ref[0] = _dot_nn(hb, wbuf[m].reshape(D_MODEL, 512))

        @pl.when(t == n_steps - 1)
        def _():
            for q in range(1, 8):
                c_copy(q, b).wait_send()
            for r in range(3):
                m_copy(r, k).wait_send()
                cw_copy(r, k).wait_send()
            for q in range(9):
                ici(q, k).wait_send()
                fwd(q, c_).wait_send()
            for i in range(12):
                to_hbm(i).wait()

    def z_index(t, k_ref):
        u = jnp.maximum(t - nh, 0)
        m = block_of(u // nz, k_ref[0])
        return ((m // 2 + 4) % N_GROUPS, u % nz, m % 2)

    tok = lambda t, k_ref: (jnp.minimum(t, nh - 1), 0)
    const2 = lambda t, k_ref: (0, 0)
    grid_spec = pltpu.PrefetchScalarGridSpec(
        num_scalar_prefetch=1,
        grid=(n_steps,),
        in_specs=[pl.BlockSpec((tmh, D_MODEL), tok),
                  pl.BlockSpec((1, D_MODEL), const2),
                  pl.BlockSpec((1, 3 * D_MODEL), const2),
                  pl.BlockSpec((1, D_MODEL), const2),
                  pl.BlockSpec((32, 256), const2),
                  pl.BlockSpec(memory_space=pl.ANY),
                  pl.BlockSpec(memory_space=pl.ANY)],
        out_specs=(pl.BlockSpec((tmh, D_MODEL), tok),
                   pl.BlockSpec((1, tmz, 512), z_index),
                   pl.BlockSpec((1, 3 * D_MODEL), const2),
                   pl.BlockSpec((8, D_MODEL), const2),
                   pl.BlockSpec((N_CHIPS, 32, 256), lambda t, k_ref: (0, 0, 0)),
                   pl.BlockSpec(memory_space=pl.ANY)),
        scratch_shapes=[pltpu.VMEM((s_len, D_MODEL), BF16),
                        pltpu.VMEM((12, 2, 512, 512), BF16),
                        pltpu.VMEM((D_MODEL, 1536), F32),
                        pltpu.VMEM((D_MODEL, 768), F32),
                        pltpu.VMEM((8, 8, D_MODEL), F32),
                        pltpu.VMEM((N_CHIPS, 8, 768), F32),
                        pltpu.VMEM((N_CHIPS, 32, 256), F32),
                        pltpu.SemaphoreType.DMA((2,)),
                        pltpu.SemaphoreType.DMA((7,)), pltpu.SemaphoreType.DMA((7,)),
                        pltpu.SemaphoreType.DMA((3,)), pltpu.SemaphoreType.DMA((3,)),
                        pltpu.SemaphoreType.DMA((3,)), pltpu.SemaphoreType.DMA((3,)),
                        pltpu.SemaphoreType.DMA((9,)), pltpu.SemaphoreType.DMA((9,)),
                        pltpu.SemaphoreType.DMA((9,)), pltpu.SemaphoreType.DMA((9,)),
                        pltpu.SemaphoreType.DMA((12,))])
    return pl.pallas_call(
        body, name="front",
        grid_spec=grid_spec,
        out_shape=(jax.ShapeDtypeStruct((s_len, D_MODEL), BF16),
                   jax.ShapeDtypeStruct((N_GROUPS, s_len, D_MODEL), F32),
                   jax.ShapeDtypeStruct((1, 3 * D_MODEL), F32),
                   jax.ShapeDtypeStruct((8, D_MODEL), F32),
                   jax.ShapeDtypeStruct((N_CHIPS, 32, 256), F32),
                   jax.ShapeDtypeStruct((12, 2, 512, 512), BF16)),
        compiler_params=_params(dimension_semantics=("arbitrary",)),
    )(kidx, x, c, b_ada, norm_g, cw, w_ada, w_in)


def _all_reduce_partners():
    x, y, c = _pos()
    return [(x, y, 1 - c), (x, 1 - y, c), (1 - x, y, c)]


def _butterfly_event(e, bufs, recvs, ssem, rsem, partners, wires=None):
    n = len(bufs)
    wires = wires or [None] * n

    def copies(s):
        return [_rcopy(buf if wire is None else wire, recv.at[s], ssem.at[s * n + i], rsem.at[s * n + i], partners[s])
                for i, (buf, recv, wire) in enumerate(zip(bufs, recvs, wires))]

    if e > 0:
        for cp in copies(e - 1):
            cp.wait()
        for buf, recv, wire in zip(bufs, recvs, wires):
            mine = buf[...] if wire is None else wire[...].astype(F32)
            buf[...] = mine + recv[e - 1].astype(F32)
    if e < len(partners):
        for buf, wire in zip(bufs, wires):
            if wire is not None:
                wire[...] = buf[...].astype(BF16)
        for cp in copies(e):
            cp.start()


def _onehot_rows(v, b):
    row = lax.broadcasted_iota(jnp.int32, (8, v.shape[1]), 0)
    return jnp.where(row == b, jnp.broadcast_to(v, (8, v.shape[1])), 0.0)


def _conv_fwd(z6, cw4, conv_b, w_out):
    s_len = z6.shape[1]
    tt = TIME_TILE
    n_blocks = D_MODEL // 128

    def body(z_ref, cw_ref, cb_ref, wout_hbm, q_ref, p_ref, sig_ref, t2_ref, wout4_hbm,
             ppad, stage, wbuf, lsem, isend, irecv, fsend, frecv, osem):
        jb = pl.program_id(0)
        x_, y_, c_ = _pos()
        k = 2 * x_ + y_
        sib = (x_, y_, 1 - c_)

        def ici(r, kk):
            rk = _remote_chip(k, r)
            return _rcopy(wbuf.at[kk, c_], wbuf.at[kk, c_], isend.at[r], irecv.at[r], (rk // 2, rk % 2, c_))

        def fwd(r, hf):
            kk = _remote_chip(k, r)
            return _rcopy(wbuf.at[kk, hf], wbuf.at[kk, hf], fsend.at[r], frecv.at[r], sib)

        @pl.when(jb == 0)
        def _():
            ld = pltpu.make_async_copy(wout_hbm, stage, lsem)
            ld.start()
            ld.wait()
            for hf in range(2):
                wbuf[k, hf] = stage[hf * 256:(hf + 1) * 256, :].astype(BF16)
            for r in range(3):
                ici(r, k).start()

        @pl.when(jb == (5 * n_blocks) // 8)
        def _():
            for r in range(3):
                ici(r, _remote_chip(k, r)).wait_recv()
                fwd(r, c_).start()

        zero = jnp.zeros((CONV_PAD, 128), F32)
        ppad[0:CONV_PAD, :] = zero
        ppad[s_len + CONV_PAD:s_len + 2 * CONV_PAD, :] = zero

        def fill(i, carry):
            t0 = pl.multiple_of(i * tt, tt)
            sig = _sigmoid(z_ref[1, pl.ds(t0, tt), :])
            p = z_ref[0, pl.ds(t0, tt), :] * sig
            ppad[pl.ds(CONV_PAD + t0, tt), :] = p
            p_ref[0, pl.ds(t0, tt), :] = p
            sig_ref[0, pl.ds(t0, tt), :] = sig
            t2_ref[0, pl.ds(t0, tt), :] = p * (1.0 - sig)
            return carry
        lax.fori_loop(0, s_len // tt, fill, 0)
        w = cw_ref[0]
        bias = cb_ref[...]

        def conv(i, carry):
            t0 = pl.multiple_of(i * tt, tt)
            acc = jnp.broadcast_to(bias, (tt, 128))
            for o in range(1, CONV_WIDTH + 1):
                acc = acc + w[o - 1:o, :] * ppad[pl.ds(t0 + o, tt), :]
            q_ref[0, pl.ds(t0, tt), :] = acc
            return carry
        lax.fori_loop(0, s_len // tt, conv, 0, unroll=2)

        @pl.when(jb == n_blocks - 1)
        def _():
            for r in range(3):
                fwd(r, 1 - c_).wait_recv()
            out = pltpu.make_async_copy(wbuf, wout4_hbm, osem)
            out.start()
            for r in range(3):
                ici(r, k).wait_send()
                fwd(r, c_).wait_send()
            out.wait()

    return pl.pallas_call(
        body, name="conv_fwd",
        grid=(n_blocks,),
        out_shape=tuple(jax.ShapeDtypeStruct((n_blocks, s_len, 128), F32) for _ in range(4))
        + (jax.ShapeDtypeStruct((N_CHIPS, 2, 256, D_MODEL), BF16),),
        in_specs=[pl.BlockSpec((2, s_len, 128), lambda j: (2, 0, j)),
                  pl.BlockSpec((1, 32, 128), lambda j: (j // 2, 0, j % 2)),
                  pl.BlockSpec((1, 128), lambda j: (0, j)),
                  pl.BlockSpec(memory_space=pl.ANY)],
        out_specs=tuple(pl.BlockSpec((1, s_len, 128), lambda j: (j, 0, 0)) for _ in range(4))
        + (pl.BlockSpec(memory_space=pl.ANY),),
        scratch_shapes=[pltpu.VMEM((s_len + 2 * CONV_PAD, 128), F32),
                        pltpu.VMEM((512, D_MODEL), F32),
                        pltpu.VMEM((N_CHIPS, 2, 256, D_MODEL), BF16),
                        pltpu.SemaphoreType.DMA,
                        pltpu.SemaphoreType.DMA((3,)), pltpu.SemaphoreType.DMA((3,)),
                        pltpu.SemaphoreType.DMA((3,)), pltpu.SemaphoreType.DMA((3,)),
                        pltpu.SemaphoreType.DMA],
        compiler_params=_params(dimension_semantics=("arbitrary",)),
    )(z6, cw4, conv_b, w_out)


def _middle(x, z6, q, target, ln_rows, mod, w_s, bs_exp, w_out):
    s_len = x.shape[0]
    tm = TOKEN_TILE
    n_steps = s_len // tm
    n_chunks = tm // CHUNK
    inv_d = 1.0 / D_MODEL

    def body(x_ref, z_ref, q_ref, tgt_ref, cg_ref, cb_ref, sg_ref, sb_ref, fg_ref, mod_ref, ws_ref, bs_ref, wout_ref,
             dz_ref, dq_ref, dx2_ref, ycat_ref, dy_ref, dws_ref, sm_ref, dbs_ref,
             vl_scr, vm_scr, y_scr, dycat_scr, dvm_scr, dvl_scr, acc_scr, dbs_acc, keep, rstd_scr):
        i = pl.program_id(0)

        @pl.when(i == 0)
        def _():
            acc_scr[...] = jnp.zeros_like(acc_scr)
            dbs_acc[...] = jnp.zeros_like(dbs_acc)
            dws_ref[...] = jnp.zeros_like(dws_ref)

        cg, cb, sg, sb, fg = cg_ref[...], cb_ref[...], sg_ref[...], sb_ref[...], fg_ref[...]
        gm = mod_ref[:, 2 * D_MODEL:3 * D_MODEL]

        def norm_stats(t):
            c = t - _mean(t)
            rstd = lax.rsqrt(_mean(c * c) + EPS)
            return c * rstd, rstd

        def phase1(rows):
            qhat, rstd_q = norm_stats(jnp.concatenate([q_ref[blk, rows, :] for blk in range(D_MODEL // 128)], axis=1))
            ln = qhat * cg + cb
            gz = z_ref[0, rows, :]
            sig_ln = _sigmoid(ln)
            sig_g = _sigmoid(gz)
            ycat_ref[rows, 0:D_MODEL] = ((ln * sig_ln) * (gz * sig_g)).astype(BF16)
            vhat, rstd_v = norm_stats(z_ref[2, rows, :])
            vl_scr[rows, :] = (vhat * sg + sb).astype(BF16)
            keep[0, rows, :] = qhat
            keep[1, rows, :] = vhat
            keep[2, rows, :] = sig_ln
            keep[3, rows, :] = sig_g
            rstd_scr[0, rows, :] = rstd_q
            rstd_scr[1, rows, :] = rstd_v
        _row_loop(tm, phase1, unroll=FULL_UNROLL)

        for ch in range(n_chunks):
            r0 = ch * CHUNK
            for h in range(HEADS):
                c0 = h * CHUNK
                vm_scr[r0:r0 + CHUNK, c0:c0 + CHUNK] = (
                    _dot_nn(ws_ref[h].astype(BF16), vl_scr[r0:r0 + CHUNK, c0:c0 + CHUNK]) + bs_ref[:, c0:c0 + CHUNK])

        def phase3(rows):
            bg = z_ref[3, rows, :]
            sig_b = _sigmoid(bg)
            keep[4, rows, :] = sig_b
            ycat_ref[rows, D_MODEL:2 * D_MODEL] = (z_ref[1, rows, :] * vm_scr[rows, :] * (bg * sig_b)).astype(BF16)
        _row_loop(tm, phase3, unroll=FULL_UNROLL)

        y_scr[...] = _dot_nn(ycat_ref[...], wout_ref[...])

        def phase5(rows):
            y = y_scr[rows, :]
            x2 = x_ref[rows, :] + gm * y
            r2 = lax.rsqrt(_mean(x2 * x2) + EPS)
            xn2 = x2 * r2
            diff = xn2 * fg - tgt_ref[rows, :]
            acc_scr[6] += _colsum8(diff * diff)
            dout = diff * inv_d
            acc_scr[0] += _colsum8(dout * xn2)
            dxn = dout * fg
            dx2 = r2 * (dxn - xn2 * _mean(dxn * xn2))
            dx2_ref[rows, :] = dx2
            acc_scr[1] += _colsum8(dx2 * y)
            dy_ref[rows, :] = (dx2 * gm).astype(BF16)
        _row_loop(tm, phase5, unroll=FULL_UNROLL)

        dycat_scr[...] = _dot_nt(dy_ref[...], wout_ref[...])

        def phase7(rows):
            dyb = dycat_scr[rows, D_MODEL:2 * D_MODEL]
            u = z_ref[1, rows, :]
            bg = z_ref[3, rows, :]
            vm = vm_scr[rows, :]
            sig = keep[4, rows, :]
            silu = bg * sig
            dz_ref[1, rows, :] = (dyb * vm * silu).astype(BF16)
            dvm = dyb * u * silu
            dz_ref[3, rows, :] = (dyb * u * vm * (sig * (1.0 + bg * (1.0 - sig)))).astype(BF16)
            dvm_scr[rows, :] = dvm.astype(BF16)
            pos = pl.ds(pl.multiple_of(rows.start % CHUNK, ROWS), ROWS)
            dbs_acc[pos, :] += dvm
        _row_loop(tm, phase7, unroll=FULL_UNROLL)

        for ch in range(n_chunks):
            r0 = ch * CHUNK
            for h in range(HEADS):
                c0 = h * CHUNK
                dvm_b = dvm_scr[r0:r0 + CHUNK, c0:c0 + CHUNK]
                dws_ref[h] += _dot_nt(dvm_b, vl_scr[r0:r0 + CHUNK, c0:c0 + CHUNK])
                dvl_scr[r0:r0 + CHUNK, c0:c0 + CHUNK] = _dot_tn(ws_ref[h].astype(BF16), dvm_b)

        def phase9(rows):
            vhat, rstd_v = keep[1, rows, :], rstd_scr[1, rows, :]
            dvl = dvl_scr[rows, :]
            acc_scr[4] += _colsum8(dvl * vhat)
            acc_scr[5] += _colsum8(dvl)
            dvh = dvl * sg
            dz_ref[2, rows, :] = (rstd_v * (dvh - _mean(dvh) - vhat * _mean(dvh * vhat))).astype(BF16)
            qhat, rstd_q = keep[0, rows, :], rstd_scr[0, rows, :]
            ln = qhat * cg + cb
            sig_ln = keep[2, rows, :]
            gz = z_ref[0, rows, :]
            sig_g = keep[3, rows, :]
            dya = dycat_scr[rows, 0:D_MODEL]
            dz_ref[0, rows, :] = (dya * (ln * sig_ln) * (sig_g * (1.0 + gz * (1.0 - sig_g)))).astype(BF16)
            dln = (dya * (gz * sig_g)) * (sig_ln * (1.0 + ln * (1.0 - sig_ln)))
            acc_scr[2] += _colsum8(dln * qhat)
            acc_scr[3] += _colsum8(dln)
            dqh = dln * cg
            dq = rstd_q * (dqh - _mean(dqh) - qhat * _mean(dqh * qhat))
            for blk in range(D_MODEL // 128):
                dq_ref[blk, rows, :] = dq[:, blk * 128:(blk + 1) * 128]
        _row_loop(tm, phase9, unroll=FULL_UNROLL)

        @pl.when(i == n_steps - 1)
        def _():
            for qi in range(8):
                scale = 0.5 * inv_d if qi == 6 else 1.0
                sm_ref[qi:qi + 1, :] = jnp.sum(acc_scr[qi], axis=0, keepdims=True) * scale
            lane = lax.broadcasted_iota(jnp.int32, (CHUNK, CHUNK), 1)
            tile = jnp.zeros((CHUNK, CHUNK), F32)
            for h in range(HEADS):
                col = jnp.sum(dbs_acc[:, h * CHUNK:(h + 1) * CHUNK], axis=1, keepdims=True)
                tile = jnp.where(lane == h, col, tile)
            dbs_ref[...] = tile

    tok = lambda i: (i, 0)
    const2 = lambda i: (0, 0)
    return pl.pallas_call(
        body, name="middle",
        grid=(n_steps,),
        out_shape=(jax.ShapeDtypeStruct((4, s_len, D_MODEL), BF16),
                   jax.ShapeDtypeStruct((D_MODEL // 128, s_len, 128), F32),
                   jax.ShapeDtypeStruct((s_len, D_MODEL), F32),
                   jax.ShapeDtypeStruct((s_len, 2 * D_MODEL), BF16),
                   jax.ShapeDtypeStruct((s_len, D_MODEL), BF16),
                   jax.ShapeDtypeStruct((HEADS, CHUNK, CHUNK), F32),
                   jax.ShapeDtypeStruct((8, D_MODEL), F32),
                   jax.ShapeDtypeStruct((CHUNK, CHUNK), F32)),
        in_specs=[pl.BlockSpec((tm, D_MODEL), tok),
                  pl.BlockSpec((4, tm, D_MODEL), lambda i: (0, i, 0)),
                  pl.BlockSpec((D_MODEL // 128, tm, 128), lambda i: (0, i, 0)),
                  pl.BlockSpec((tm, D_MODEL), tok),
                  *[pl.BlockSpec((1, D_MODEL), const2) for _ in range(5)],
                  pl.BlockSpec((1, 3 * D_MODEL), const2),
                  pl.BlockSpec((HEADS, CHUNK, CHUNK), lambda i: (0, 0, 0)),
                  pl.BlockSpec((CHUNK, D_MODEL), const2),
                  pl.BlockSpec((2 * D_MODEL, D_MODEL), const2, pipeline_mode=pl.Buffered(1))],
        out_specs=(pl.BlockSpec((4, tm, D_MODEL), lambda i: (0, i, 0)),
                   pl.BlockSpec((D_MODEL // 128, tm, 128), lambda i: (0, i, 0)),
                   pl.BlockSpec((tm, D_MODEL), tok),
                   pl.BlockSpec((tm, 2 * D_MODEL), tok),
                   pl.BlockSpec((tm, D_MODEL), tok),
                   pl.BlockSpec((HEADS, CHUNK, CHUNK), lambda i: (0, 0, 0)),
                   pl.BlockSpec((8, D_MODEL), const2),
                   pl.BlockSpec((CHUNK, CHUNK), const2)),
        scratch_shapes=[pltpu.VMEM((tm, D_MODEL), BF16),
                        pltpu.VMEM((tm, D_MODEL), F32),
                        pltpu.VMEM((tm, D_MODEL), F32),
                        pltpu.VMEM((tm, 2 * D_MODEL), F32),
                        pltpu.VMEM((tm, D_MODEL), BF16),
                        pltpu.VMEM((tm, D_MODEL), F32),
                        pltpu.VMEM((8, 8, D_MODEL), F32),
                        pltpu.VMEM((CHUNK, D_MODEL), F32),
                        pltpu.VMEM((5, tm, D_MODEL), F32),
                        pltpu.VMEM((2, tm, 1), F32)],
        compiler_params=_params(dimension_semantics=("arbitrary",)),
    )(x, z6, q, target, *ln_rows, mod, w_s, bs_exp, w_out)


def _conv_bwd(kidx, dq, sig, t2, cw4, sm1, dws, dbs, ycat, dy):
    s_len = dq.shape[1]
    tt = TIME_TILE
    n_blocks = D_MODEL // 128
    shp = [(16, D_MODEL), (D_MODEL, 128), (128, 128)]
    nj, nr, nc = 1, 256, D_MODEL
    kt = n_blocks // N_CHIPS
    tk = s_len // kt

    def body(k_ref, dq_ref, sig_ref, t2_ref, cw_ref, sm1_ref, dws_in, dbs_in, ycat_ref, dy_ref,
             dz_ref, pk1_ref, dws_ref, dbs_ref, loss_ref, gout_ref,
             dqpad, b_pk, b_dws, b_dbs, r_pk, r_dws, r_dbs, w_dws, ssem, rsem, *rs_scr):
        del k_ref
        jb = pl.program_id(0)
        x_, y_, c_ = _pos()

        @pl.when(jb == 0)
        def _():
            b_pk[0:8, :] = _onehot_rows(sm1_ref[1:2, :], 4 * x_ + 2 * y_ + c_)
            for r, src in enumerate([2, 3, 4, 5, 0, 6]):
                b_pk[8 + r:9 + r, :] = sm1_ref[src:src + 1, :]
            b_pk[14:16, :] = jnp.zeros((2, D_MODEL), F32)
            b_dws[...] = dws_in[...]
            b_dbs[...] = dbs_in[...]

        for e, step in enumerate([0, 0, (3 * n_blocks) // 8, (6 * n_blocks) // 8]):
            @pl.when(jb == step)
            def _():
                _butterfly_event(e, [b_pk, b_dws, b_dbs], [r_pk, r_dws, r_dbs], ssem, rsem, _all_reduce_partners(),
                                 wires=[None, w_dws, None])

        @pl.when(jb == (6 * n_blocks) // 8)
        def _():
            pk1_ref[...] = b_pk[...]
            dws_ref[...] = b_dws[...]
            dbs_ref[...] = b_dbs[...]
            loss_ref[...] = jnp.sum(b_pk[13:14, :], axis=1, keepdims=True)

        zero = jnp.zeros((CONV_PAD, 128), F32)
        dqpad[0:CONV_PAD, :] = zero
        dqpad[s_len + CONV_PAD:s_len + 2 * CONV_PAD, :] = zero
        dqpad[CONV_PAD:s_len + CONV_PAD, :] = dq_ref[0]
        w = cw_ref[0]

        def bwd(i, carry):
            t0 = i * tt
            dp = jnp.zeros((tt, 128), F32)
            for o in range(1, CONV_WIDTH + 1):
                dp = dp + w[CONV_WIDTH - o:CONV_WIDTH - o + 1, :] * dqpad[pl.ds(t0 + o, tt), :]
            dz_ref[0, 0, pl.ds(t0, tt), :] = (dp * sig_ref[0, pl.ds(t0, tt), :]).astype(BF16)
            dz_ref[0, 1, pl.ds(t0, tt), :] = (dp * t2_ref[0, pl.ds(t0, tt), :]).astype(BF16)
            return carry

        n_iter = s_len // tt
        n_groups = min(8, n_iter)
        rows = tk // n_groups

        def conv_and_matmul(add):
            for g in range(n_groups):
                for it in range(g * n_iter // n_groups, (g + 1) * n_iter // n_groups):
                    bwd(it, 0)
                k0 = pl.multiple_of((jb % kt) * tk + g * rows, rows)
                add(_dot_tn(ycat_ref[g * rows:(g + 1) * rows, :], dy_ref[pl.ds(k0, rows), :]))
        _rs_step(jb // kt, jb % kt, kt, nj, nr, nc, conv_and_matmul, gout_ref, rs_scr)

    const2 = lambda j, k_ref: (0, 0)
    grid_spec = pltpu.PrefetchScalarGridSpec(
        num_scalar_prefetch=1,
        grid=(n_blocks,),
        in_specs=[*[pl.BlockSpec((1, s_len, 128), lambda j, k_ref: (j, 0, 0)) for _ in range(3)],
                  pl.BlockSpec((1, 32, 128), lambda j, k_ref: (j // 2, 0, j % 2)),
                  pl.BlockSpec((8, D_MODEL), const2),
                  pl.BlockSpec((D_MODEL, 128), const2),
                  pl.BlockSpec((128, 128), const2),
                  pl.BlockSpec((tk, 512), lambda j, k_ref: (j % kt, _rs_block(j // kt, k_ref[0], nj))),
                  pl.BlockSpec((s_len, D_MODEL), const2, pipeline_mode=pl.Buffered(1))],
        out_specs=(pl.BlockSpec((1, 2, s_len, 128), lambda j, k_ref: (j, 0, 0, 0)),)
        + tuple(pl.BlockSpec(s, const2) for s in shp) + (pl.BlockSpec((1, 1), const2),
                                                         pl.BlockSpec(memory_space=pl.ANY)),
        scratch_shapes=[pltpu.VMEM((s_len + 2 * CONV_PAD, 128), F32)]
        + [pltpu.VMEM(s, F32) for s in shp]
        + [pltpu.VMEM((3,) + shp[0], F32), pltpu.VMEM((3,) + shp[1], BF16), pltpu.VMEM((3,) + shp[2], F32),
           pltpu.VMEM(shp[1], BF16)]
        + [pltpu.SemaphoreType.DMA((9,)), pltpu.SemaphoreType.DMA((9,))] + _rs_scratch(nj, nr, nc))
    return pl.pallas_call(
        body, name="conv_bwd",
        grid_spec=grid_spec,
        out_shape=(jax.ShapeDtypeStruct((n_blocks, 2, s_len, 128), BF16),)
        + tuple(jax.ShapeDtypeStruct(s, F32) for s in shp) + (jax.ShapeDtypeStruct((1, 1), F32),
                                                              jax.ShapeDtypeStruct((nj, 2, nr, nc), F32)),
        compiler_params=_params(dimension_semantics=("arbitrary",)),
    )(kidx, dq, sig, t2, cw4, sm1, dws, dbs, ycat, dy)


def _bwd_in(dz4, dzc, w12, x, dx2, mod, norm_g, dq, p):
    s_len = x.shape[0]
    tm = TOKEN_TILE
    n_steps = s_len // tm
    tt = TIME_TILE
    n_cblocks = D_MODEL // 128
    parts = max(1, n_steps // n_cblocks)
    nsub = max(1, n_cblocks // n_steps)
    cw = 128 * nsub
    tiles = (s_len // tt) // parts

    def body(dz_ref, dzc_ref, w_ref, x_ref, dx2_ref, mod_ref, g_ref, dq_ref, p_ref, gx_ref, sm_ref, dcw_ref,
             dh_scr, acc_scr, ppad, wacc):
        i = pl.program_id(0)
        part = i % parts

        @pl.when(i == 0)
        def _():
            acc_scr[...] = jnp.zeros_like(acc_scr)

        @pl.when(part == 0)
        def _():
            zero = jnp.zeros((CONV_PAD, 128), F32)
            for sub in range(nsub):
                ppad[sub, 0:CONV_PAD, :] = zero
                ppad[sub, s_len + CONV_PAD:s_len + 2 * CONV_PAD, :] = zero
                ppad[sub, CONV_PAD:s_len + CONV_PAD, :] = p_ref[sub]
            wacc[...] = jnp.zeros_like(wacc)

        def dw_tile(tile):
            t0 = pl.multiple_of((part * tiles + tile) * tt, tt)
            for sub in range(nsub):
                dqt = dq_ref[sub, pl.ds(t0, tt), :]
                for o in range(1, CONV_WIDTH + 1):
                    wacc[sub, o - 1] += _colsum8(dqt * ppad[sub, pl.ds(t0 + o, tt), :])
                wacc[sub, CONV_WIDTH] += _colsum8(dqt)

        def dz_cols(j, hf):
            if j < 4:
                return dz_ref[j, :, hf * 512:(hf + 1) * 512]
            return jnp.concatenate([dzc_ref[4 * hf + blk, j - 4] for blk in range(4)], axis=1)

        dots = [(j, hf) for j in range(N_GROUPS) for hf in range(2)]
        dh = jnp.zeros((tm, D_MODEL), F32)
        for d, (j, hf) in enumerate(dots):
            dh = dh + _dot_nt(dz_cols(j, hf), w_ref[2 * _natural_group(j) + hf])
            for tile in range(d * tiles // len(dots), (d + 1) * tiles // len(dots)):
                dw_tile(tile)
        dh_scr[...] = dh

        @pl.when(part == parts - 1)
        def _():
            for sub in range(nsub):
                for k in range(32):
                    dcw_ref[k:k + 1, sub * 128:(sub + 1) * 128] = jnp.sum(wacc[sub, k], axis=0, keepdims=True)

        scale1 = 1.0 + mod_ref[:, D_MODEL:2 * D_MODEL]
        g = g_ref[...]

        def rows_fn(rows):
            xt = x_ref[rows, :]
            r = lax.rsqrt(_mean(xt * xt) + EPS)
            xn = xt * r
            dhr = dh_scr[rows, :]
            acc_scr[0] += _colsum8(dhr)
            acc_scr[1] += _colsum8(dhr * (xn * g))
            acc_scr[2] += _colsum8(dhr * scale1 * xn)
            dxn = dhr * (g * scale1)
            gx_ref[rows, :] = dx2_ref[rows, :] + r * (dxn - xn * _mean(dxn * xn))
        _row_loop(tm, rows_fn, unroll=FULL_UNROLL)

        @pl.when(i == n_steps - 1)
        def _():
            for qi in range(8):
                sm_ref[qi:qi + 1, :] = jnp.sum(acc_scr[qi], axis=0, keepdims=True)

    tok = lambda i: (i, 0)
    const2 = lambda i: (0, 0)
    return pl.pallas_call(
        body, name="bwd_in",
        grid=(n_steps,),
        out_shape=(jax.ShapeDtypeStruct((s_len, D_MODEL), F32), jax.ShapeDtypeStruct((8, D_MODEL), F32),
                   jax.ShapeDtypeStruct((32, D_MODEL), F32)),
        in_specs=[pl.BlockSpec((4, tm, D_MODEL), lambda i: (0, i, 0)),
                  pl.BlockSpec((n_cblocks, 2, tm, 128), lambda i: (0, 0, i, 0)),
                  pl.BlockSpec((12, D_MODEL, 512), lambda i: (0, 0, 0), pipeline_mode=pl.Buffered(1)),
                  pl.BlockSpec((tm, D_MODEL), tok),
                  pl.BlockSpec((tm, D_MODEL), tok),
                  pl.BlockSpec((1, 3 * D_MODEL), const2),
                  pl.BlockSpec((1, D_MODEL), const2),
                  pl.BlockSpec((nsub, s_len, 128), lambda i: (i // parts, 0, 0)),
                  pl.BlockSpec((nsub, s_len, 128), lambda i: (i // parts, 0, 0))],
        out_specs=(pl.BlockSpec((tm, D_MODEL), tok), pl.BlockSpec((8, D_MODEL), const2),
                   pl.BlockSpec((32, cw), lambda i: (0, i // parts))),
        scratch_shapes=[pltpu.VMEM((tm, D_MODEL), F32), pltpu.VMEM((8, 8, D_MODEL), F32),
                        pltpu.VMEM((nsub, s_len + 2 * CONV_PAD, 128), F32), pltpu.VMEM((nsub, 32, 8, 128), F32)],
        compiler_params=_params(dimension_semantics=("arbitrary",)),
    )(dz4, dzc, w12, x, dx2, mod, norm_g, dq, p)


def _rs_shard_block(i, nj):
    g = jnp.where(i < 2 * nj, i % 2, jnp.where(i < 3 * nj, 2, 3))
    j = jnp.where(i < 2 * nj, i // 2, jnp.where(i < 3 * nj, i - 2 * nj, i - 3 * nj))
    return g, j


def _rs_block(i, k, nj):
    g, j = _rs_shard_block(i, nj)
    return nj * jnp.bitwise_xor(k, 3 - g) + j


def _rs_scratch(nj, nr, nc):
    nblk = N_CHIPS * nj
    return [pltpu.VMEM((2, 2, nr, nc), F32),
            pltpu.VMEM((nblk, nr, nc), F32),
            pltpu.VMEM((2 * nj, nr, nc), BF16),
            pltpu.VMEM((2 * nj, nr, nc), BF16),
            pltpu.VMEM((nj, nr, nc), BF16),
            pltpu.VMEM((nj, nr, nc), BF16),
            pltpu.VMEM((nj, nr, nc), F32),
            pltpu.SemaphoreType.DMA,
            pltpu.SemaphoreType.DMA((nblk,)), pltpu.SemaphoreType.DMA((nblk,)),
            pltpu.SemaphoreType.DMA((2 * nj,)), pltpu.SemaphoreType.DMA((2 * nj,)),
            pltpu.SemaphoreType.DMA((nj,)), pltpu.SemaphoreType.DMA((nj,)),
            pltpu.SemaphoreType.DMA, pltpu.SemaphoreType.DMA]


def _rs_step(i, kk, kt, nj, nr, nc, partial_fn, out_ref, scr):
    nblk = N_CHIPS * nj
    (acc, recv_a, send1, recv1, send2, recv2, own_ps,
     lsem, psend, precv, s1send, s1recv, s2send, s2recv, fsend, frecv) = scr
    x, y, c = _pos()
    sib = (x, y, 1 - c)
    first = (jnp.bitwise_xor(x, 1 - c), jnp.bitwise_xor(y, c), c)
    second = (jnp.bitwise_xor(x, c), jnp.bitwise_xor(y, 1 - c), c)
    slot = i % 2

    @pl.when(kk == 0)
    def _():
        acc[slot] = jnp.zeros((2, nr, nc), F32)

    def pair_copy(ii, sl):
        return _rcopy(acc.at[sl, 1 - c], recv_a.at[ii], psend.at[ii], precv.at[ii], sib)

    def pair_sum(ii, sl):
        cp = pair_copy(ii, sl)
        cp.wait_recv()
        cp.wait_send()
        return acc[sl, c] + recv_a[ii]

    def stage1_copy(src, dst):
        return _rcopy(send1.at[src], recv1.at[dst], s1send.at[src], s1recv.at[dst], first)

    def stage2_copy(j):
        return _rcopy(send2.at[j], recv2.at[j], s2send.at[j], s2recv.at[j], second)

    def finalize(ii, sl):
        ps = pair_sum(ii, sl)
        g, j = _rs_shard_block(ii, nj)

        @pl.when(g == 0)
        def _():
            send1[j] = ps.astype(BF16)
            stage1_copy(j, nj + j).start()

        @pl.when(g == 1 + c)
        def _():
            send1[nj + j] = ps.astype(BF16)
            stage1_copy(nj + j, j).start()

        @pl.when(g == 2 - c)
        def _():
            stage1_copy(0, nj + j).wait_recv()
            send2[j] = (ps + recv1[nj + j].astype(F32)).astype(BF16)
            stage2_copy(j).start()

        @pl.when(g == 3)
        def _():
            own_ps[j] = ps

    @pl.when(jnp.logical_and(kk == kt - 1, i >= 1))
    def _():
        finalize(i - 1, 1 - slot)

    def add_partial(res):
        acc[slot, 0] += res[0:nr]
        acc[slot, 1] += res[nr:2 * nr]
    partial_fn(add_partial)

    @pl.when(kk == kt - 1)
    def _():
        pair_copy(i, slot).start()

        @pl.when(i == nblk - 1)
        def _():
            own_ps[nj - 1] = pair_sum(nblk - 1, (nblk - 1) % 2)
            for j in range(nj):
                stage1_copy(0, j).wait_recv()
                stage2_copy(j).wait_recv()
                own_ps[j] = (own_ps[j] + recv1[j].astype(F32)) + recv2[j].astype(F32)
            loc = pltpu.make_async_copy(own_ps, out_ref.at[:, c], lsem)
            loc.start()
            swap = _rcopy(own_ps, out_ref.at[:, c], fsend, frecv, sib)
            swap.start()
            loc.wait()
            swap.wait_send()
            _rcopy(own_ps, out_ref.at[:, 1 - c], fsend, frecv, sib).wait_recv()
            for s in range(2 * nj):
                stage1_copy(s, 0).wait_send()
            for j in range(nj):
                stage2_copy(j).wait_send()


def _grad_w_in(kidx, h, dz4, dzc, sm2, dcw):
    s_len = h.shape[0]
    tk = min(K_TILE, s_len)
    kt = s_len // tk
    nj, nr, nc = 3, 512, 512
    nblk = N_CHIPS * nj

    def body(k_ref, a_ref, b_ref, bc_ref, sm2_ref, dcw_in, out_ref, pk2_ref, dcwsh_ref, dcb_ref, *scr):
        del k_ref
        b_pk2, b_dcw, r_pk2, r_dcw, bsend, brecv = scr[-6:]
        i = pl.program_id(0)
        kk = pl.program_id(1)
        x, y, c = _pos()
        k = 2 * x + y
        m = _rs_block(i, k, nj)

        @pl.when(jnp.logical_and(kk == 0, i == 0))
        def _():
            b = 4 * x + 2 * y + c
            b_pk2[0:8, :] = _onehot_rows(sm2_ref[0:1, :], b)
            b_pk2[8:16, :] = _onehot_rows(sm2_ref[1:2, :], b)
            b_pk2[16:17, :] = sm2_ref[2:3, :]
            b_pk2[17:24, :] = jnp.zeros((7, D_MODEL), F32)
            b_dcw[...] = dcw_in[...]

        for e in range(4):
            @pl.when(jnp.logical_and(kk == 0, i == (e * nblk) // 4))
            def _():
                _butterfly_event(e, [b_pk2, b_dcw], [r_pk2, r_dcw], bsend, brecv, _all_reduce_partners())

        @pl.when(jnp.logical_and(kk == 0, i == (3 * nblk) // 4))
        def _():
            pk2_ref[...] = b_pk2[...]
            sel = jnp.zeros((32, 256), F32)
            for kc in range(N_CHIPS):
                sel = jnp.where(k == kc, b_dcw[:, kc * 256:(kc + 1) * 256], sel)
            dcwsh_ref[...] = sel
            dcb_ref[...] = b_dcw[31:32, :]

        def partial(add):
            rows = pl.ds(pl.multiple_of(kk * tk, tk), tk)

            @pl.when(m >= 4)
            def _():
                add(_dot_tn(a_ref[rows, :], b_ref[0]))

            @pl.when(m < 4)
            def _():
                add(_dot_tn(a_ref[rows, :], jnp.concatenate([bc_ref[blk, 0] for blk in range(4)], axis=1)))
        _rs_step(i, kk, kt, nj, nr, nc, partial, out_ref, scr[:-6])

    def dz4_index(i, kk, k_ref):
        m = _rs_block(i, k_ref[0], nj)
        use = m >= 4
        return (jnp.where(use, m // 2 - 2, 0), jnp.where(use, kk, 0), jnp.where(use, m % 2, 0))

    def dzc_index(i, kk, k_ref):
        m = _rs_block(i, k_ref[0], nj)
        use = m < 4
        return (jnp.where(use, m % 2, 0), jnp.where(use, m // 2, 0), jnp.where(use, kk, 0), 0)

    const2 = lambda i, kk, k_ref: (0, 0)
    small_out = [(24, D_MODEL), (32, 256), (1, D_MODEL)]
    small_buf = [(24, D_MODEL), (32, D_MODEL)]
    grid_spec = pltpu.PrefetchScalarGridSpec(
        num_scalar_prefetch=1,
        grid=(nblk, kt),
        in_specs=[pl.BlockSpec((s_len, D_MODEL), const2, pipeline_mode=pl.Buffered(1)),
                  pl.BlockSpec((1, tk, 512), dz4_index),
                  pl.BlockSpec((4, 1, tk, 128), dzc_index),
                  pl.BlockSpec((8, D_MODEL), const2),
                  pl.BlockSpec((32, D_MODEL), const2)],
        out_specs=(pl.BlockSpec(memory_space=pl.ANY),) + tuple(pl.BlockSpec(s, const2) for s in small_out),
        scratch_shapes=_rs_scratch(nj, nr, nc)
        + [pltpu.VMEM(s, F32) for s in small_buf] + [pltpu.VMEM((3,) + s, F32) for s in small_buf]
        + [pltpu.SemaphoreType.DMA((6,)), pltpu.SemaphoreType.DMA((6,))])
    return pl.pallas_call(
        body, name="grad_w_in",
        grid_spec=grid_spec,
        out_shape=(jax.ShapeDtypeStruct((nj, 2, nr, nc), F32),) + tuple(jax.ShapeDtypeStruct(s, F32) for s in small_out),
        compiler_params=_params(dimension_semantics=("arbitrary", "arbitrary")),
    )(kidx, h, dz4, dzc, sm2, dcw)


def _adamw_math(w, g, m, v):
    m = ADAM_B1 * m + (1.0 - ADAM_B1) * g
    v = ADAM_B2 * v + (1.0 - ADAM_B2) * (g * g)
    m_hat = m / (1.0 - ADAM_B1 ** ADAM_STEP)
    v_hat = v / (1.0 - ADAM_B2 ** ADAM_STEP)
    delta = -ADAM_LR * (m_hat / (jnp.sqrt(v_hat) + ADAM_EPS) + ADAM_WD * w)
    return delta, m, v


def _adamw_blocked(w, m, v, g4, name):
    nj, _, nr, nc = g4.shape

    def body(w_ref, m_ref, v_ref, g_ref, go_ref, d_ref, mo_ref, vo_ref):
        g = g_ref[0, 0]
        d, mn, vn = _adamw_math(w_ref[...], g, m_ref[...], v_ref[...])
        go_ref[...] = g
        d_ref[...] = d
        mo_ref[...] = mn
        vo_ref[...] = vn

    blk = pl.BlockSpec((nr, nc), lambda j, hf: (hf, j))
    return pl.pallas_call(
        body, name=name,
        grid=(nj, 2),
        out_shape=tuple(jax.ShapeDtypeStruct(w.shape, F32) for _ in range(4)),
        in_specs=[blk, blk, blk, pl.BlockSpec((1, 1, nr, nc), lambda j, hf: (j, hf, 0, 0))],
        out_specs=(blk, blk, blk, blk),
        compiler_params=_params(dimension_semantics=("arbitrary", "arbitrary")),
    )(w, m, v, g4)


def _adamw_w_ada(cact, pk1, pk2, w, m, v):
    rb = 256

    def body(cact_ref, pk1_ref, pk2_ref, w_ref, m_ref, v_ref, g_ref, d_ref, mo_ref, vo_ref, gb_ref, dmod, sel_scr):
        @pl.when(pl.program_id(0) == 0)
        def _():
            x, y, _ = _pos()
            k = 2 * x + y
            dmod[:, 0:D_MODEL] = pk2_ref[0:8, :]
            dmod[:, D_MODEL:2 * D_MODEL] = pk2_ref[8:16, :]
            dmod[:, 2 * D_MODEL:3 * D_MODEL] = pk1_ref[0:8, :]
            gb_ref[...] = jnp.sum(dmod[...], axis=0, keepdims=True)
            sel = jnp.zeros((8, 768), F32)
            for kk in range(N_CHIPS):
                sel = jnp.where(k == kk, dmod[:, kk * 768:(kk + 1) * 768], sel)
            sel_scr[...] = sel

        g = _dot_tn(cact_ref[...].astype(BF16), sel_scr[...].astype(BF16))
        d, mn, vn = _adamw_math(w_ref[...], g, m_ref[...], v_ref[...])
        g_ref[...] = g
        d_ref[...] = d
        mo_ref[...] = mn
        vo_ref[...] = vn

    blk = pl.BlockSpec((rb, 768), lambda i: (i, 0))
    const2 = lambda i: (0, 0)
    return pl.pallas_call(
        body, name="adamw_w_ada",
        grid=(D_MODEL // rb,),
        out_shape=tuple(jax.ShapeDtypeStruct(w.shape, F32) for _ in range(4)) + (
            jax.ShapeDtypeStruct((1, 3 * D_MODEL), F32),),
        in_specs=[pl.BlockSpec((8, rb), lambda i: (0, i)), pl.BlockSpec((16, D_MODEL), const2),
                  pl.BlockSpec((24, D_MODEL), const2), blk, blk, blk],
        out_specs=(blk, blk, blk, blk, pl.BlockSpec((1, 3 * D_MODEL), const2)),
        scratch_shapes=[pltpu.VMEM((8, 3 * D_MODEL), F32), pltpu.VMEM((8, 768), F32)],
        compiler_params=_params(dimension_semantics=("arbitrary",)),
    )(cact, pk1, pk2, w, m, v)


def _adamw_small(items, conv, packs, rows):
    n, nr = len(items), len(rows)

    def body(*refs):
        refs = list(refs)
        take = lambda k: [refs.pop(0) for _ in range(k)]
        ins, (cw_ref, cg_ref, cm_ref, cv_ref), pk_refs, row_ins = take(4 * n), take(4), take(len(packs)), take(3 * nr)
        outs, (cgo_ref, cd_ref, cmo_ref, cvo_ref), row_outs = take(3 * n), take(4), take(4 * nr)
        for i in range(n):
            w_ref, g_ref, m_ref, v_ref = ins[4 * i:4 * i + 4]
            d, mn, vn = _adamw_math(w_ref[...], g_ref[...], m_ref[...], v_ref[...])
            outs[3 * i][...] = d
            outs[3 * i + 1][...] = mn
            outs[3 * i + 2][...] = vn
        for r in range(CONV_WIDTH):
            g = cg_ref[r:r + 1, :]
            d, mn, vn = _adamw_math(cw_ref[0, r], g, cm_ref[0, r], cv_ref[0, r])
            cgo_ref[0, r] = g
            cd_ref[0, r] = d
            cmo_ref[0, r] = mn
            cvo_ref[0, r] = vn
        for i, (_, _, _, p, r) in enumerate(rows):
            w_ref, m_ref, v_ref = row_ins[3 * i:3 * i + 3]
            g = pk_refs[p][r:r + 1, :]
            d, mn, vn = _adamw_math(w_ref[...], g, m_ref[...], v_ref[...])
            row_outs[4 * i][...] = g
            row_outs[4 * i + 1][...] = d
            row_outs[4 * i + 2][...] = mn
            row_outs[4 * i + 3][...] = vn

    flat = [a for it in items for a in it] + list(conv) + list(packs) + [a for it in rows for a in it[:3]]
    n_out = 3 * n + 4 + 4 * nr
    outs = pl.pallas_call(
        body, name="adamw_small",
        out_shape=tuple(jax.ShapeDtypeStruct(it[0].shape, F32) for it in items for _ in range(3))
        + tuple(jax.ShapeDtypeStruct(conv[0].shape, F32) for _ in range(4))
        + tuple(jax.ShapeDtypeStruct(it[0].shape, F32) for it in rows for _ in range(4)),
        in_specs=[_vmem()] * len(flat),
        out_specs=tuple(_vmem() for _ in range(n_out)),
        compiler_params=_params(),
    )(*flat)
    return ([tuple(outs[3 * i:3 * i + 3]) for i in range(n)], tuple(outs[3 * n:3 * n + 4]),
            [tuple(outs[3 * n + 4 + 4 * i:3 * n + 8 + 4 * i]) for i in range(nr)])


def kernel(x, c, w_ada, b_ada, norm_g, w_in, conv_w, conv_b, conv_ln_g, conv_ln_b, sg_ln_g, sg_ln_b, w_s, b_s, w_out, final_g, loss_target, m_w_ada, m_b_ada, m_norm_g, m_w_in, m_conv_w, m_conv_b, m_conv_ln_g, m_conv_ln_b, m_sg_ln_g, m_sg_ln_b, m_w_s, m_b_s, m_w_out, m_final_g, v_w_ada, v_b_ada, v_norm_g, v_w_in, v_conv_w, v_conv_b, v_conv_ln_g, v_conv_ln_b, v_sg_ln_g, v_sg_ln_b, v_w_s, v_b_s, v_w_out, v_final_g):
    s_len = x.shape[1]
    x2d = x[0]
    tgt = loss_target[0]
    row = lambda a: a.reshape(1, -1)

    kidx = (2 * lax.axis_index("x") + lax.axis_index("y")).astype(jnp.int32).reshape(1)
    cw_sh = jnp.pad(conv_w.reshape(CONV_WIDTH, 256), ((0, 1), (0, 0)))
    h, z6, mod, cact, cw4, w_in12 = _front(kidx, x2d, c, w_ada[0], b_ada, norm_g, w_in[0], cw_sh)
    w12 = w_in12.reshape(12, D_MODEL, 512)
    q, p_cm, sig_cm, t2_cm, w_out4 = _conv_fwd(z6, cw4, conv_b, w_out[0])
    w_out_full = w_out4.reshape(2 * D_MODEL, D_MODEL)
    ln_rows = (conv_ln_g, conv_ln_b, sg_ln_g, sg_ln_b, row(final_g))
    bs_exp = jnp.repeat(b_s[0].T, CHUNK, axis=1)
    dz4, dq, dx2, ycat, dy, dws, sm1, dbs = _middle(x2d, z6, q, tgt, ln_rows, mod, w_s[0], bs_exp, w_out_full)
    dzc, pk1, dws_r, dbs_r, loss_t, g_w_out4 = _conv_bwd(
        kidx, dq, sig_cm, t2_cm, cw4, sm1, dws.reshape(D_MODEL, CHUNK), dbs, ycat, dy)
    grad_x, sm2, dcw = _bwd_in(dz4, dzc, w12, x2d, dx2, mod, norm_g, dq, p_cm)
    g_w_in4, pk2, dcw_sh, dcb = _grad_w_in(kidx, h, dz4, dzc, sm2, dcw)

    g_w_in, d_w_in, nm_w_in, nv_w_in = _adamw_blocked(w_in[0], m_w_in[0], v_w_in[0], g_w_in4, "adamw_w_in")
    g_w_out, d_w_out, nm_w_out, nv_w_out = _adamw_blocked(w_out[0], m_w_out[0], v_w_out[0], g_w_out4, "adamw_w_out")
    g_w_ada, d_w_ada, nm_w_ada, nv_w_ada, g_b_ada = _adamw_w_ada(cact, pk1, pk2, w_ada[0], m_w_ada[0], v_w_ada[0])

    loss = loss_t.reshape(())
    g_w_s = dws_r
    g_b_s = dbs_r[:, :HEADS].T
    small = [
        (b_ada, g_b_ada, m_b_ada, v_b_ada),
        (conv_b, dcb, m_conv_b, v_conv_b),
        (w_s.reshape(D_MODEL, CHUNK), g_w_s, m_w_s.reshape(D_MODEL, CHUNK), v_w_s.reshape(D_MODEL, CHUNK)),
        (b_s[0], g_b_s, m_b_s[0], v_b_s[0]),
    ]
    rows = [
        (norm_g, m_norm_g, v_norm_g, 1, 16),
        (conv_ln_g, m_conv_ln_g, v_conv_ln_g, 0, 8),
        (conv_ln_b, m_conv_ln_b, v_conv_ln_b, 0, 9),
        (sg_ln_g, m_sg_ln_g, v_sg_ln_g, 0, 10),
        (sg_ln_b, m_sg_ln_b, v_sg_ln_b, 0, 11),
        (row(final_g), row(m_final_g), row(v_final_g), 0, 12),
    ]
    upd, (g_conv_w, d_conv_w, nm_conv_w, nv_conv_w), row_upd = _adamw_small(
        small, (conv_w, dcw_sh, m_conv_w, v_conv_w), (pk1, pk2), rows)
    g_norm_g, g_cln_g, g_cln_b, g_sln_g, g_sln_b, g_final = (t[0] for t in row_upd)

    shapes = [w_ada.shape, b_ada.shape, norm_g.shape, w_in.shape, conv_w.shape, conv_b.shape, conv_ln_g.shape,
              conv_ln_b.shape, sg_ln_g.shape, sg_ln_b.shape, w_s.shape, b_s.shape, w_out.shape, final_g.shape]
    grads = [g_w_ada, g_b_ada, g_norm_g, g_w_in, g_conv_w, dcb, g_cln_g, g_cln_b, g_sln_g, g_sln_b, g_w_s, g_b_s,
             g_w_out, g_final]
    big = {0: (d_w_ada, nm_w_ada, nv_w_ada), 3: (d_w_in, nm_w_in, nv_w_in), 4: (d_conv_w, nm_conv_w, nv_conv_w),
           12: (d_w_out, nm_w_out, nv_w_out)}
    trip = [None] * 14
    for i, t in big.items():
        trip[i] = t
    for i, t in zip([1, 5, 10, 11], upd):
        trip[i] = t
    for i, t in zip([2, 6, 7, 8, 9, 13], row_upd):
        trip[i] = t[1:]
    fit = lambda arrs: [a.reshape(s) for a, s in zip(arrs, shapes)]
    return (loss, grad_x.reshape(x.shape), *fit(grads), *fit([t[0] for t in trip]), *fit([t[1] for t in trip]),
            *fit([t[2] for t in trip]))
```

```python
import jax
import jax.numpy as jnp
from jax import lax
from jax.experimental import pallas as pl
from jax.experimental.pallas import tpu as pltpu

F32 = jnp.float32
BF16 = jnp.bfloat16
MESH = pl.DeviceIdType.MESH

D_MODEL = 1024
N_CHIPS = 4
HEADS = 8
CHUNK = 128
CONV_WIDTH = 31
CONV_HALF = CONV_WIDTH // 2
CONV_PAD = 16
EPS = 1e-6
ADAM_LR = 0.001
ADAM_B1 = 0.9
ADAM_B2 = 0.999
ADAM_EPS = 1e-08
ADAM_WD = 0.01
ADAM_STEP = 10

V7X_VMEM_BYTES = 64 * 1024 * 1024
VMEM_LIMIT = V7X_VMEM_BYTES - 8 * 1024 * 1024
ROWS = 16
UNROLL = 8
TOKEN_TILE = 256
FULL_UNROLL = TOKEN_TILE // ROWS
TIME_TILE = 128
K_TILE = 2048

N_GROUPS = 6


def _natural_group(j):
    return (j + 2) % N_GROUPS


def _pos():
    return lax.axis_index("x"), lax.axis_index("y"), lax.axis_index("c")


def _rcopy(src, dst, ssem, rsem, dev):
    return pltpu.make_async_remote_copy(src_ref=src, dst_ref=dst, send_sem=ssem, recv_sem=rsem,
                                        device_id=dev, device_id_type=MESH)


def _vmem():
    return pl.BlockSpec(memory_space=pltpu.VMEM)


def _params(**kw):
    return pltpu.CompilerParams(vmem_limit_bytes=VMEM_LIMIT, **kw)


def _sigmoid(v):
    return 0.5 * jnp.tanh(0.5 * v) + 0.5


def _row_loop(n_rows, body, unroll=1):
    def step(r, carry):
        body(pl.ds(pl.multiple_of(r * ROWS, ROWS), ROWS))
        return carry
    lax.fori_loop(0, n_rows // ROWS, step, 0, unroll=unroll)


def _colsum8(v):
    return v.reshape(v.shape[0] // 8, 8, v.shape[1]).sum(axis=0)


def _mean(v):
    return jnp.mean(v, axis=-1, keepdims=True)


def _dot_nn(a, b):
    return jnp.dot(a, b, preferred_element_type=F32)


def _dot_nt(a, b):
    return lax.dot_general(a, b, (((1,), (1,)), ((), ())), preferred_element_type=F32)


def _dot_tn(a, b):
    return lax.dot_general(a, b, (((0,), (0,)), ((), ())), preferred_element_type=F32)


def _remote_chip(k, r):
    return jnp.bitwise_xor(k, r + 1)


def _front(kidx, x, c, w_ada, b_ada, norm_g, w_in, cw):
    s_len = x.shape[0]
    tmh = min(512, s_len)
    tmz = min(2048, s_len)
    nh = s_len // tmh
    nz = s_len // tmz
    n_steps = nh + 12 * nz

    def remote_block(q):
        return jnp.where(q < 6, q % 2, 2), jnp.where(q < 6, q // 2, q - 6)

    def block_of(i, k):
        r, j = remote_block(jnp.maximum(i - 3, 0))
        return jnp.where(i < 3, 3 * k + i, 3 * _remote_chip(k, r) + j)

    def body(k_ref, x_ref, c_ref, bada_ref, g_ref, cw_ref, wada_hbm, win_hbm,
             h_ref, z_ref, mod_ref, cact_ref, cw4_ref, w12_hbm,
             h_all, wbuf, stage, wada_v, cslab, mslab, cw4_s,
             lsem, csend, crecv, msend, mrecv, wsend, wrecv, isend, irecv, fsend, frecv, osem):
        del k_ref
        t = pl.program_id(0)
        x_, y_, c_ = _pos()
        k = 2 * x_ + y_
        b = 4 * x_ + 2 * y_ + c_
        sib = (x_, y_, 1 - c_)

        def dev_of(r):
            kk = _remote_chip(k, r)
            return (kk // 2, kk % 2, c_)

        def ici(q, kk):
            r, j = remote_block(q)
            return _rcopy(wbuf.at[3 * kk + j, c_], wbuf.at[3 * kk + j, c_], isend.at[q], irecv.at[q], dev_of(r))

        def fwd(q, hf):
            r, j = remote_block(q)
            blk = 3 * _remote_chip(k, r) + j
            return _rcopy(wbuf.at[blk, hf], wbuf.at[blk, hf], fsend.at[q], frecv.at[q], sib)

        def c_copy(q, src):
            d = jnp.bitwise_xor(b, q)
            return _rcopy(cslab.at[src], cslab.at[src], csend.at[q - 1], crecv.at[q - 1], (d // 4, (d // 2) % 2, d % 2))

        def m_copy(r, kk):
            return _rcopy(mslab.at[kk], mslab.at[kk], msend.at[r], mrecv.at[r], dev_of(r))

        def cw_copy(r, kk):
            return _rcopy(cw4_s.at[kk], cw4_s.at[kk], wsend.at[r], wrecv.at[r], dev_of(r))

        def to_hbm(i):
            m = block_of(i, k)
            return pltpu.make_async_copy(wbuf.at[m], w12_hbm.at[m], osem.at[i])

        @pl.when(t == 0)
        def _():
            ld_w = pltpu.make_async_copy(win_hbm, stage, lsem.at[0])
            ld_w.start()
            ld_a = pltpu.make_async_copy(wada_hbm, wada_v, lsem.at[1])
            ld_a.start()
            cslab[b] = jnp.broadcast_to(c_ref[...], (8, D_MODEL))
            for q in range(1, 8):
                c_copy(q, b).start()
            cw4_s[k] = cw_ref[...]
            for r in range(3):
                cw_copy(r, k).start()
            ld_w.wait()
            for j in range(3):
                for hf in range(2):
                    wbuf[3 * k + j, hf] = stage[hf * 512:(hf + 1) * 512, j * 512:(j + 1) * 512].astype(BF16)
            for i in range(3):
                to_hbm(i).start()
            for q in range(1, 8):
                c_copy(q, jnp.bitwise_xor(b, q)).wait_recv()
            row = lax.broadcasted_iota(jnp.int32, (8, D_MODEL), 0)
            call = jnp.zeros((8, D_MODEL), F32)
            for d in range(8):
                call = jnp.where(row == d, cslab[d], call)
            cact = call * _sigmoid(call)
            cact_ref[...] = cact
            ld_a.wait()
            mslab[k] = _dot_nn(cact.astype(BF16), wada_v[...].astype(BF16))
            for r in range(3):
                m_copy(r, k).start()
            for q in range(9):
                ici(q, k).start()
            for r in range(3):
                m_copy(r, _remote_chip(k, r)).wait_recv()
            row8 = lax.broadcasted_iota(jnp.int32, (8, 768), 0)
            for kk in range(N_CHIPS):
                piece = jnp.sum(jnp.where(row8 == b, mslab[kk], 0.0), axis=0, keepdims=True)
                mod_ref[:, kk * 768:(kk + 1) * 768] = piece + bada_ref[:, kk * 768:(kk + 1) * 768]
            for r in range(3):
                cw_copy(r, _remote_chip(k, r)).wait_recv()
            cw4_ref[...] = cw4_s[...]

        @pl.when(t < nh)
        def _():
            shift = mod_ref[:, 0:D_MODEL]
            scale1 = 1.0 + mod_ref[:, D_MODEL:2 * D_MODEL]
            g = g_ref[...]
            base = t * tmh

            def rows_fn(rows):
                xt = x_ref[rows, :]
                r = lax.rsqrt(_mean(xt * xt) + EPS)
                hv = ((xt * r * g) * scale1 + shift).astype(BF16)
                h_ref[rows, :] = hv
                h_all[pl.ds(pl.multiple_of(base + rows.start, ROWS), ROWS), :] = hv
            _row_loop(tmh, rows_fn, unroll=UNROLL)

        @pl.when(t >= nh)
        def _():
            u = t - nh
            i = u // nz
            rt = u % nz
            @pl.when(jnp.logical_and(rt == 0, jnp.logical_and(i >= 2, i <= 10)))
            def _():
                q = i - 2
                r, _ = remote_block(q)
                ici(q, _remote_chip(k, r)).wait_recv()
                fwd(q, c_).start()

            @pl.when(jnp.logical_and(rt == 0, i >= 3))
            def _():
                fwd(i - 3, 1 - c_).wait_recv()
                to_hbm(i).start()
            m = block_of(i, k)
            hb = h_all[pl.ds(pl.multiple_of(rt * tmz, tmz), tmz), :]
            z_ref[0] = _dot_nn(hb, wbuf[m].reshape(D_MODEL, 512))

        @pl.when(t == n_steps - 1)
        def _():
            for q in range(1, 8):
                c_copy(q, b).wait_send()
            for r in range(3):
                m_copy(r, k).wait_send()
                cw_copy(r, k).wait_send()
            for q in range(9):
                ici(q, k).wait_send()
                fwd(q, c_).wait_send()
            for i in range(12):
                to_hbm(i).wait()

    def z_index(t, k_ref):
        u = jnp.maximum(t - nh, 0)
        m = block_of(u // nz, k_ref[0])
        return ((m // 2 + 4) % N_GROUPS, u % nz, m % 2)

    tok = lambda t, k_ref: (jnp.minimum(t, nh - 1), 0)
    const2 = lambda t, k_ref: (0, 0)
    grid_spec = pltpu.PrefetchScalarGridSpec(
        num_scalar_prefetch=1,
        grid=(n_steps,),
        in_specs=[pl.BlockSpec((tmh, D_MODEL), tok),
                  pl.BlockSpec((1, D_MODEL), const2),
                  pl.BlockSpec((1, 3 * D_MODEL), const2),
                  pl.BlockSpec((1, D_MODEL), const2),
                  pl.BlockSpec((32, 256), const2),
                  pl.BlockSpec(memory_space=pl.ANY),
                  pl.BlockSpec(memory_space=pl.ANY)],
        out_specs=(pl.BlockSpec((tmh, D_MODEL), tok),
                   pl.BlockSpec((1, tmz, 512), z_index),
                   pl.BlockSpec((1, 3 * D_MODEL), const2),
                   pl.BlockSpec((8, D_MODEL), const2),
                   pl.BlockSpec((N_CHIPS, 32, 256), lambda t, k_ref: (0, 0, 0)),
                   pl.BlockSpec(memory_space=pl.ANY)),
        scratch_shapes=[pltpu.VMEM((s_len, D_MODEL), BF16),
                        pltpu.VMEM((12, 2, 512, 512), BF16),
                        pltpu.VMEM((D_MODEL, 1536), F32),
                        pltpu.VMEM((D_MODEL, 768), F32),
                        pltpu.VMEM((8, 8, D_MODEL), F32),
                        pltpu.VMEM((N_CHIPS, 8, 768), F32),
                        pltpu.VMEM((N_CHIPS, 32, 256), F32),
                        pltpu.SemaphoreType.DMA((2,)),
                        pltpu.SemaphoreType.DMA((7,)), pltpu.SemaphoreType.DMA((7,)),
                        pltpu.SemaphoreType.DMA((3,)), pltpu.SemaphoreType.DMA((3,)),
                        pltpu.SemaphoreType.DMA((3,)), pltpu.SemaphoreType.DMA((3,)),
                        pltpu.SemaphoreType.DMA((9,)), pltpu.SemaphoreType.DMA((9,)),
                        pltpu.SemaphoreType.DMA((9,)), pltpu.SemaphoreType.DMA((9,)),
                        pltpu.SemaphoreType.DMA((12,))])
    return pl.pallas_call(
        body, name="front",
        grid_spec=grid_spec,
        out_shape=(jax.ShapeDtypeStruct((s_len, D_MODEL), BF16),
                   jax.ShapeDtypeStruct((N_GROUPS, s_len, D_MODEL), F32),
                   jax.ShapeDtypeStruct((1, 3 * D_MODEL), F32),
                   jax.ShapeDtypeStruct((8, D_MODEL), F32),
                   jax.ShapeDtypeStruct((N_CHIPS, 32, 256), F32),
                   jax.ShapeDtypeStruct((12, 2, 512, 512), BF16)),
        compiler_params=_params(dimension_semantics=("arbitrary",)),
    )(kidx, x, c, b_ada, norm_g, cw, w_ada, w_in)


def _all_reduce_partners():
    x, y, c = _pos()
    return [(x, y, 1 - c), (x, 1 - y, c), (1 - x, y, c)]


def _butterfly_event(e, bufs, recvs, ssem, rsem, partners, wires=None):
    n = len(bufs)
    wires = wires or [None] * n

    def copies(s):
        return [_rcopy(buf if wire is None else wire, recv.at[s], ssem.at[s * n + i], rsem.at[s * n + i], partners[s])
                for i, (buf, recv, wire) in enumerate(zip(bufs, recvs, wires))]

    if e > 0:
        for cp in copies(e - 1):
            cp.wait()
        for buf, recv, wire in zip(bufs, recvs, wires):
            mine = buf[...] if wire is None else wire[...].astype(F32)
            buf[...] = mine + recv[e - 1].astype(F32)
    if e < len(partners):
        for buf, wire in zip(bufs, wires):
            if wire is not None:
                wire[...] = buf[...].astype(BF16)
        for cp in copies(e):
            cp.start()


def _onehot_rows(v, b):
    row = lax.broadcasted_iota(jnp.int32, (8, v.shape[1]), 0)
    return jnp.where(row == b, jnp.broadcast_to(v, (8, v.shape[1])), 0.0)


def _conv_fwd(z6, cw4, conv_b, w_out):
    s_len = z6.shape[1]
    tt = TIME_TILE
    n_blocks = D_MODEL // 128

    def body(z_ref, cw_ref, cb_ref, wout_hbm, q_ref, p_ref, sig_ref, t2_ref, wout4_hbm,
             ppad, stage, wbuf, lsem, isend, irecv, fsend, frecv, osem):
        jb = pl.program_id(0)
        x_, y_, c_ = _pos()
        k = 2 * x_ + y_
        sib = (x_, y_, 1 - c_)

        def ici(r, kk):
            rk = _remote_chip(k, r)
            return _rcopy(wbuf.at[kk, c_], wbuf.at[kk, c_], isend.at[r], irecv.at[r], (rk // 2, rk % 2, c_))

        def fwd(r, hf):
            kk = _remote_chip(k, r)
            return _rcopy(wbuf.at[kk, hf], wbuf.at[kk, hf], fsend.at[r], frecv.at[r], sib)

        @pl.when(jb == 0)
        def _():
            ld = pltpu.make_async_copy(wout_hbm, stage, lsem)
            ld.start()
            ld.wait()
            for hf in range(2):
                wbuf[k, hf] = stage[hf * 256:(hf + 1) * 256, :].astype(BF16)
            for r in range(3):
                ici(r, k).start()

        @pl.when(jb == (5 * n_blocks) // 8)
        def _():
            for r in range(3):
                ici(r, _remote_chip(k, r)).wait_recv()
                fwd(r, c_).start()

        zero = jnp.zeros((CONV_PAD, 128), F32)
        ppad[0:CONV_PAD, :] = zero
        ppad[s_len + CONV_PAD:s_len + 2 * CONV_PAD, :] = zero

        def fill(i, carry):
            t0 = pl.multiple_of(i * tt, tt)
            sig = _sigmoid(z_ref[1, pl.ds(t0, tt), :])
            p = z_ref[0, pl.ds(t0, tt), :] * sig
            ppad[pl.ds(CONV_PAD + t0, tt), :] = p
            p_ref[0, pl.ds(t0, tt), :] = p
            sig_ref[0, pl.ds(t0, tt), :] = sig
            t2_ref[0, pl.ds(t0, tt), :] = p * (1.0 - sig)
            return carry
        lax.fori_loop(0, s_len // tt, fill, 0, unroll=2)
        w = cw_ref[0]
        bias = cb_ref[...]

        def conv(i, carry):
            t0 = pl.multiple_of(i * tt, tt)
            acc = jnp.broadcast_to(bias, (tt, 128))
            for o in range(1, CONV_WIDTH + 1):
                acc = acc + w[o - 1:o, :] * ppad[pl.ds(t0 + o, tt), :]
            q_ref[0, pl.ds(t0, tt), :] = acc
            return carry
        lax.fori_loop(0, s_len // tt, conv, 0, unroll=2)

        @pl.when(jb == n_blocks - 1)
        def _():
            for r in range(3):
                fwd(r, 1 - c_).wait_recv()
            out = pltpu.make_async_copy(wbuf, wout4_hbm, osem)
            out.start()
            for r in range(3):
                ici(r, k).wait_send()
                fwd(r, c_).wait_send()
            out.wait()

    return pl.pallas_call(
        body, name="conv_fwd",
        grid=(n_blocks,),
        out_shape=tuple(jax.ShapeDtypeStruct((n_blocks, s_len, 128), F32) for _ in range(4))
        + (jax.ShapeDtypeStruct((N_CHIPS, 2, 256, D_MODEL), BF16),),
        in_specs=[pl.BlockSpec((2, s_len, 128), lambda j: (2, 0, j)),
                  pl.BlockSpec((1, 32, 128), lambda j: (j // 2, 0, j % 2)),
                  pl.BlockSpec((1, 128), lambda j: (0, j)),
                  pl.BlockSpec(memory_space=pl.ANY)],
        out_specs=tuple(pl.BlockSpec((1, s_len, 128), lambda j: (j, 0, 0)) for _ in range(4))
        + (pl.BlockSpec(memory_space=pl.ANY),),
        scratch_shapes=[pltpu.VMEM((s_len + 2 * CONV_PAD, 128), F32),
                        pltpu.VMEM((512, D_MODEL), F32),
                        pltpu.VMEM((N_CHIPS, 2, 256, D_MODEL), BF16),
                        pltpu.SemaphoreType.DMA,
                        pltpu.SemaphoreType.DMA((3,)), pltpu.SemaphoreType.DMA((3,)),
                        pltpu.SemaphoreType.DMA((3,)), pltpu.SemaphoreType.DMA((3,)),
                        pltpu.SemaphoreType.DMA],
        compiler_params=_params(dimension_semantics=("arbitrary",)),
    )(z6, cw4, conv_b, w_out)


def _middle(x, z6, q, target, ln_rows, mod, w_s, bs_exp, w_out):
    s_len = x.shape[0]
    tm = TOKEN_TILE
    n_steps = s_len // tm
    n_chunks = tm // CHUNK
    inv_d = 1.0 / D_MODEL

    def body(x_ref, z_ref, q_ref, tgt_ref, cg_ref, cb_ref, sg_ref, sb_ref, fg_ref, mod_ref, ws_ref, bs_ref, wout_ref,
             dz_ref, dq_ref, dx2_ref, ycat_ref, dy_ref, dws_ref, sm_ref, dbs_ref,
             vl_scr, vm_scr, y_scr, dycat_scr, dvm_scr, dvl_scr, acc_scr, dbs_acc, keep, rstd_scr):
        i = pl.program_id(0)

        @pl.when(i == 0)
        def _():
            acc_scr[...] = jnp.zeros_like(acc_scr)
            dbs_acc[...] = jnp.zeros_like(dbs_acc)
            dws_ref[...] = jnp.zeros_like(dws_ref)

        cg, cb, sg, sb, fg = cg_ref[...], cb_ref[...], sg_ref[...], sb_ref[...], fg_ref[...]
        gm = mod_ref[:, 2 * D_MODEL:3 * D_MODEL]

        def norm_stats(t):
            c = t - _mean(t)
            rstd = lax.rsqrt(_mean(c * c) + EPS)
            return c * rstd, rstd

        def phase1(rows):
            qhat, rstd_q = norm_stats(jnp.concatenate([q_ref[blk, rows, :] for blk in range(D_MODEL // 128)], axis=1))
            ln = qhat * cg + cb
            gz = z_ref[0, rows, :]
            sig_ln = _sigmoid(ln)
            sig_g = _sigmoid(gz)
            ycat_ref[rows, 0:D_MODEL] = ((ln * sig_ln) * (gz * sig_g)).astype(BF16)
            vhat, rstd_v = norm_stats(z_ref[2, rows, :])
            vl_scr[rows, :] = (vhat * sg + sb).astype(BF16)
            keep[0, rows, :] = qhat
            keep[1, rows, :] = vhat
            keep[2, rows, :] = sig_ln
            keep[3, rows, :] = sig_g
            rstd_scr[0, rows, :] = rstd_q
            rstd_scr[1, rows, :] = rstd_v
        _row_loop(tm, phase1, unroll=FULL_UNROLL)

        for ch in range(n_chunks):
            r0 = ch * CHUNK
            for h in range(HEADS):
                c0 = h * CHUNK
                vm_scr[r0:r0 + CHUNK, c0:c0 + CHUNK] = (
                    _dot_nn(ws_ref[h].astype(BF16), vl_scr[r0:r0 + CHUNK, c0:c0 + CHUNK]) + bs_ref[:, c0:c0 + CHUNK])

        def phase3(rows):
            bg = z_ref[3, rows, :]
            sig_b = _sigmoid(bg)
            keep[4, rows, :] = sig_b
            ycat_ref[rows, D_MODEL:2 * D_MODEL] = (z_ref[1, rows, :] * vm_scr[rows, :] * (bg * sig_b)).astype(BF16)
        _row_loop(tm, phase3, unroll=FULL_UNROLL)

        y_scr[...] = _dot_nn(ycat_ref[...], wout_ref[...])

        def phase5(rows):
            y = y_scr[rows, :]
            x2 = x_ref[rows, :] + gm * y
            r2 = lax.rsqrt(_mean(x2 * x2) + EPS)
            xn2 = x2 * r2
            diff = xn2 * fg - tgt_ref[rows, :]
            acc_scr[6] += _colsum8(diff * diff)
            dout = diff * inv_d
            acc_scr[0] += _colsum8(dout * xn2)
            dxn = dout * fg
            dx2 = r2 * (dxn - xn2 * _mean(dxn * xn2))
            dx2_ref[rows, :] = dx2
            acc_scr[1] += _colsum8(dx2 * y)
            dy_ref[rows, :] = (dx2 * gm).astype(BF16)
        _row_loop(tm, phase5, unroll=FULL_UNROLL)

        dycat_scr[...] = _dot_nt(dy_ref[...], wout_ref[...])

        def phase7(rows):
            dyb = dycat_scr[rows, D_MODEL:2 * D_MODEL]
            u = z_ref[1, rows, :]
            bg = z_ref[3, rows, :]
            vm = vm_scr[rows, :]
            sig = keep[4, rows, :]
            silu = bg * sig
            dz_ref[1, rows, :] = (dyb * vm * silu).astype(BF16)
            dvm = dyb * u * silu
            dz_ref[3, rows, :] = (dyb * u * vm * (sig * (1.0 + bg * (1.0 - sig)))).astype(BF16)
            dvm_scr[rows, :] = dvm.astype(BF16)
            pos = pl.ds(pl.multiple_of(rows.start % CHUNK, ROWS), ROWS)
            dbs_acc[pos, :] += dvm
        _row_loop(tm, phase7, unroll=FULL_UNROLL)

        for ch in range(n_chunks):
            r0 = ch * CHUNK
            for h in range(HEADS):
                c0 = h * CHUNK
                dvm_b = dvm_scr[r0:r0 + CHUNK, c0:c0 + CHUNK]
                dws_ref[h] += _dot_nt(dvm_b, vl_scr[r0:r0 + CHUNK, c0:c0 + CHUNK])
                dvl_scr[r0:r0 + CHUNK, c0:c0 + CHUNK] = _dot_tn(ws_ref[h].astype(BF16), dvm_b)

        def phase9(rows):
            vhat, rstd_v = keep[1, rows, :], rstd_scr[1, rows, :]
            dvl = dvl_scr[rows, :]
            acc_scr[4] += _colsum8(dvl * vhat)
            acc_scr[5] += _colsum8(dvl)
            dvh = dvl * sg
            dz_ref[2, rows, :] = (rstd_v * (dvh - _mean(dvh) - vhat * _mean(dvh * vhat))).astype(BF16)
            qhat, rstd_q = keep[0, rows, :], rstd_scr[0, rows, :]
            ln = qhat * cg + cb
            sig_ln = keep[2, rows, :]
            gz = z_ref[0, rows, :]
            sig_g = keep[3, rows, :]
            dya = dycat_scr[rows, 0:D_MODEL]
            dz_ref[0, rows, :] = (dya * (ln * sig_ln) * (sig_g * (1.0 + gz * (1.0 - sig_g)))).astype(BF16)
            dln = (dya * (gz * sig_g)) * (sig_ln * (1.0 + ln * (1.0 - sig_ln)))
            acc_scr[2] += _colsum8(dln * qhat)
            acc_scr[3] += _colsum8(dln)
            dqh = dln * cg
            dq = rstd_q * (dqh - _mean(dqh) - qhat * _mean(dqh * qhat))
            for blk in range(D_MODEL // 128):
                dq_ref[blk, rows, :] = dq[:, blk * 128:(blk + 1) * 128]
        _row_loop(tm, phase9, unroll=FULL_UNROLL)

        @pl.when(i == n_steps - 1)
        def _():
            for qi in range(8):
                scale = 0.5 * inv_d if qi == 6 else 1.0
                sm_ref[qi:qi + 1, :] = jnp.sum(acc_scr[qi], axis=0, keepdims=True) * scale
            lane = lax.broadcasted_iota(jnp.int32, (CHUNK, CHUNK), 1)
            tile = jnp.zeros((CHUNK, CHUNK), F32)
            for h in range(HEADS):
                col = jnp.sum(dbs_acc[:, h * CHUNK:(h + 1) * CHUNK], axis=1, keepdims=True)
                tile = jnp.where(lane == h, col, tile)
            dbs_ref[...] = tile

    tok = lambda i: (i, 0)
    const2 = lambda i: (0, 0)
    return pl.pallas_call(
        body, name="middle",
        grid=(n_steps,),
        out_shape=(jax.ShapeDtypeStruct((4, s_len, D_MODEL), BF16),
                   jax.ShapeDtypeStruct((D_MODEL // 128, s_len, 128), F32),
                   jax.ShapeDtypeStruct((s_len, D_MODEL), F32),
                   jax.ShapeDtypeStruct((s_len, 2 * D_MODEL), BF16),
                   jax.ShapeDtypeStruct((s_len, D_MODEL), BF16),
                   jax.ShapeDtypeStruct((HEADS, CHUNK, CHUNK), F32),
                   jax.ShapeDtypeStruct((8, D_MODEL), F32),
                   jax.ShapeDtypeStruct((CHUNK, CHUNK), F32)),
        in_specs=[pl.BlockSpec((tm, D_MODEL), tok),
                  pl.BlockSpec((4, tm, D_MODEL), lambda i: (0, i, 0)),
                  pl.BlockSpec((D_MODEL // 128, tm, 128), lambda i: (0, i, 0)),
                  pl.BlockSpec((tm, D_MODEL), tok),
                  *[pl.BlockSpec((1, D_MODEL), const2) for _ in range(5)],
                  pl.BlockSpec((1, 3 * D_MODEL), const2),
                  pl.BlockSpec((HEADS, CHUNK, CHUNK), lambda i: (0, 0, 0)),
                  pl.BlockSpec((CHUNK, D_MODEL), const2),
                  pl.BlockSpec((2 * D_MODEL, D_MODEL), const2, pipeline_mode=pl.Buffered(1))],
        out_specs=(pl.BlockSpec((4, tm, D_MODEL), lambda i: (0, i, 0)),
                   pl.BlockSpec((D_MODEL // 128, tm, 128), lambda i: (0, i, 0)),
                   pl.BlockSpec((tm, D_MODEL), tok),
                   pl.BlockSpec((tm, 2 * D_MODEL), tok),
                   pl.BlockSpec((tm, D_MODEL), tok),
                   pl.BlockSpec((HEADS, CHUNK, CHUNK), lambda i: (0, 0, 0)),
                   pl.BlockSpec((8, D_MODEL), const2),
                   pl.BlockSpec((CHUNK, CHUNK), const2)),
        scratch_shapes=[pltpu.VMEM((tm, D_MODEL), BF16),
                        pltpu.VMEM((tm, D_MODEL), F32),
                        pltpu.VMEM((tm, D_MODEL), F32),
                        pltpu.VMEM((tm, 2 * D_MODEL), F32),
                        pltpu.VMEM((tm, D_MODEL), BF16),
                        pltpu.VMEM((tm, D_MODEL), F32),
                        pltpu.VMEM((8, 8, D_MODEL), F32),
                        pltpu.VMEM((CHUNK, D_MODEL), F32),
                        pltpu.VMEM((5, tm, D_MODEL), F32),
                        pltpu.VMEM((2, tm, 1), F32)],
        compiler_params=_params(dimension_semantics=("arbitrary",)),
    )(x, z6, q, target, *ln_rows, mod, w_s, bs_exp, w_out)


def _conv_bwd(kidx, dq, sig, t2, cw4, sm1, dws, dbs, ycat, dy):
    s_len = dq.shape[1]
    tt = TIME_TILE
    n_blocks = D_MODEL // 128
    shp = [(16, D_MODEL), (D_MODEL, 128), (128, 128)]
    nj, nr, nc = 1, 256, D_MODEL
    kt = n_blocks // N_CHIPS
    tk = s_len // kt

    def body(k_ref, dq_ref, sig_ref, t2_ref, cw_ref, sm1_ref, dws_in, dbs_in, ycat_ref, dy_ref,
             dz_ref, pk1_ref, dws_ref, dbs_ref, loss_ref, gout_ref,
             dqpad, b_pk, b_dws, b_dbs, r_pk, r_dws, r_dbs, w_dws, ssem, rsem, *rs_scr):
        del k_ref
        jb = pl.program_id(0)
        x_, y_, c_ = _pos()

        @pl.when(jb == 0)
        def _():
            b_pk[0:8, :] = _onehot_rows(sm1_ref[1:2, :], 4 * x_ + 2 * y_ + c_)
            for r, src in enumerate([2, 3, 4, 5, 0, 6]):
                b_pk[8 + r:9 + r, :] = sm1_ref[src:src + 1, :]
            b_pk[14:16, :] = jnp.zeros((2, D_MODEL), F32)
            b_dws[...] = dws_in[...]
            b_dbs[...] = dbs_in[...]

        for e, step in enumerate([0, 0, (3 * n_blocks) // 8, (6 * n_blocks) // 8]):
            @pl.when(jb == step)
            def _():
                _butterfly_event(e, [b_pk, b_dws, b_dbs], [r_pk, r_dws, r_dbs], ssem, rsem, _all_reduce_partners(),
                                 wires=[None, w_dws, None])

        @pl.when(jb == (6 * n_blocks) // 8)
        def _():
            pk1_ref[...] = b_pk[...]
            dws_ref[...] = b_dws[...]
            dbs_ref[...] = b_dbs[...]
            loss_ref[...] = jnp.sum(b_pk[13:14, :], axis=1, keepdims=True)

        zero = jnp.zeros((CONV_PAD, 128), F32)
        dqpad[0:CONV_PAD, :] = zero
        dqpad[s_len + CONV_PAD:s_len + 2 * CONV_PAD, :] = zero
        dqpad[CONV_PAD:s_len + CONV_PAD, :] = dq_ref[0]
        w = cw_ref[0]

        def bwd(i, carry):
            t0 = i * tt
            dp = jnp.zeros((tt, 128), F32)
            for o in range(1, CONV_WIDTH + 1):
                dp = dp + w[CONV_WIDTH - o:CONV_WIDTH - o + 1, :] * dqpad[pl.ds(t0 + o, tt), :]
            dz_ref[0, 0, pl.ds(t0, tt), :] = (dp * sig_ref[0, pl.ds(t0, tt), :]).astype(BF16)
            dz_ref[0, 1, pl.ds(t0, tt), :] = (dp * t2_ref[0, pl.ds(t0, tt), :]).astype(BF16)
            return carry

        n_iter = s_len // tt
        n_groups = min(8, n_iter)
        rows = tk // n_groups

        def conv_and_matmul(add):
            for g in range(n_groups):
                for it in range(g * n_iter // n_groups, (g + 1) * n_iter // n_groups):
                    bwd(it, 0)
                k0 = pl.multiple_of((jb % kt) * tk + g * rows, rows)
                add(_dot_tn(ycat_ref[g * rows:(g + 1) * rows, :], dy_ref[pl.ds(k0, rows), :]))
        _rs_step(jb // kt, jb % kt, kt, nj, nr, nc, conv_and_matmul, gout_ref, rs_scr)

    const2 = lambda j, k_ref: (0, 0)
    grid_spec = pltpu.PrefetchScalarGridSpec(
        num_scalar_prefetch=1,
        grid=(n_blocks,),
        in_specs=[*[pl.BlockSpec((1, s_len, 128), lambda j, k_ref: (j, 0, 0)) for _ in range(3)],
                  pl.BlockSpec((1, 32, 128), lambda j, k_ref: (j // 2, 0, j % 2)),
                  pl.BlockSpec((8, D_MODEL), const2),
                  pl.BlockSpec((D_MODEL, 128), const2),
                  pl.BlockSpec((128, 128), const2),
                  pl.BlockSpec((tk, 512), lambda j, k_ref: (j % kt, _rs_block(j // kt, k_ref[0], nj))),
                  pl.BlockSpec((s_len, D_MODEL), const2, pipeline_mode=pl.Buffered(1))],
        out_specs=(pl.BlockSpec((1, 2, s_len, 128), lambda j, k_ref: (j, 0, 0, 0)),)
        + tuple(pl.BlockSpec(s, const2) for s in shp) + (pl.BlockSpec((1, 1), const2),
                                                         pl.BlockSpec(memory_space=pl.ANY)),
        scratch_shapes=[pltpu.VMEM((s_len + 2 * CONV_PAD, 128), F32)]
        + [pltpu.VMEM(s, F32) for s in shp]
        + [pltpu.VMEM((3,) + shp[0], F32), pltpu.VMEM((3,) + shp[1], BF16), pltpu.VMEM((3,) + shp[2], F32),
           pltpu.VMEM(shp[1], BF16)]
        + [pltpu.SemaphoreType.DMA((9,)), pltpu.SemaphoreType.DMA((9,))] + _rs_scratch(nj, nr, nc))
    return pl.pallas_call(
        body, name="conv_bwd",
        grid_spec=grid_spec,
        out_shape=(jax.ShapeDtypeStruct((n_blocks, 2, s_len, 128), BF16),)
        + tuple(jax.ShapeDtypeStruct(s, F32) for s in shp) + (jax.ShapeDtypeStruct((1, 1), F32),
                                                              jax.ShapeDtypeStruct((nj, 2, nr, nc), F32)),
        compiler_params=_params(dimension_semantics=("arbitrary",)),
    )(kidx, dq, sig, t2, cw4, sm1, dws, dbs, ycat, dy)


def _bwd_in(dz4, dzc, w12, x, dx2, mod, norm_g, dq, p):
    s_len = x.shape[0]
    tm = TOKEN_TILE
    n_steps = s_len // tm
    tt = TIME_TILE
    n_cblocks = D_MODEL // 128
    parts = max(1, n_steps // n_cblocks)
    nsub = max(1, n_cblocks // n_steps)
    cw = 128 * nsub
    tiles = (s_len // tt) // parts

    def body(dz_ref, dzc_ref, w_ref, x_ref, dx2_ref, mod_ref, g_ref, dq_ref, p_ref, gx_ref, sm_ref, dcw_ref,
             dh_scr, acc_scr, ppad, wacc):
        i = pl.program_id(0)
        part = i % parts

        @pl.when(i == 0)
        def _():
            acc_scr[...] = jnp.zeros_like(acc_scr)

        @pl.when(part == 0)
        def _():
            zero = jnp.zeros((CONV_PAD, 128), F32)
            for sub in range(nsub):
                ppad[sub, 0:CONV_PAD, :] = zero
                ppad[sub, s_len + CONV_PAD:s_len + 2 * CONV_PAD, :] = zero
                ppad[sub, CONV_PAD:s_len + CONV_PAD, :] = p_ref[sub]
            wacc[...] = jnp.zeros_like(wacc)

        def dw_tile(tile):
            t0 = pl.multiple_of((part * tiles + tile) * tt, tt)
            for sub in range(nsub):
                dqt = dq_ref[sub, pl.ds(t0, tt), :]
                for o in range(1, CONV_WIDTH + 1):
                    wacc[sub, o - 1] += _colsum8(dqt * ppad[sub, pl.ds(t0 + o, tt), :])
                wacc[sub, CONV_WIDTH] += _colsum8(dqt)

        def dz_cols(j, hf):
            if j < 4:
                return dz_ref[j, :, hf * 512:(hf + 1) * 512]
            return jnp.concatenate([dzc_ref[4 * hf + blk, j - 4] for blk in range(4)], axis=1)

        dots = [(j, hf) for j in range(N_GROUPS) for hf in range(2)]
        dh = jnp.zeros((tm, D_MODEL), F32)
        for d, (j, hf) in enumerate(dots):
            dh = dh + _dot_nt(dz_cols(j, hf), w_ref[2 * _natural_group(j) + hf])
            for tile in range(d * tiles // len(dots), (d + 1) * tiles // len(dots)):
                dw_tile(tile)
        dh_scr[...] = dh

        @pl.when(part == parts - 1)
        def _():
            for sub in range(nsub):
                for k in range(32):
                    dcw_ref[k:k + 1, sub * 128:(sub + 1) * 128] = jnp.sum(wacc[sub, k], axis=0, keepdims=True)

        scale1 = 1.0 + mod_ref[:, D_MODEL:2 * D_MODEL]
        g = g_ref[...]

        def rows_fn(rows):
            xt = x_ref[rows, :]
            r = lax.rsqrt(_mean(xt * xt) + EPS)
            xn = xt * r
            dhr = dh_scr[rows, :]
            acc_scr[0] += _colsum8(dhr)
            acc_scr[1] += _colsum8(dhr * (xn * g))
            acc_scr[2] += _colsum8(dhr * scale1 * xn)
            dxn = dhr * (g * scale1)
            gx_ref[rows, :] = dx2_ref[rows, :] + r * (dxn - xn * _mean(dxn * xn))
        _row_loop(tm, rows_fn, unroll=FULL_UNROLL)

        @pl.when(i == n_steps - 1)
        def _():
            for qi in range(8):
                sm_ref[qi:qi + 1, :] = jnp.sum(acc_scr[qi], axis=0, keepdims=True)

    tok = lambda i: (i, 0)
    const2 = lambda i: (0, 0)
    return pl.pallas_call(
        body, name="bwd_in",
        grid=(n_steps,),
        out_shape=(jax.ShapeDtypeStruct((s_len, D_MODEL), F32), jax.ShapeDtypeStruct((8, D_MODEL), F32),
                   jax.ShapeDtypeStruct((32, D_MODEL), F32)),
        in_specs=[pl.BlockSpec((4, tm, D_MODEL), lambda i: (0, i, 0)),
                  pl.BlockSpec((n_cblocks, 2, tm, 128), lambda i: (0, 0, i, 0)),
                  pl.BlockSpec((12, D_MODEL, 512), lambda i: (0, 0, 0), pipeline_mode=pl.Buffered(1)),
                  pl.BlockSpec((tm, D_MODEL), tok),
                  pl.BlockSpec((tm, D_MODEL), tok),
                  pl.BlockSpec((1, 3 * D_MODEL), const2),
                  pl.BlockSpec((1, D_MODEL), const2),
                  pl.BlockSpec((nsub, s_len, 128), lambda i: (i // parts, 0, 0)),
                  pl.BlockSpec((nsub, s_len, 128), lambda i: (i // parts, 0, 0))],
        out_specs=(pl.BlockSpec((tm, D_MODEL), tok), pl.BlockSpec((8, D_MODEL), const2),
                   pl.BlockSpec((32, cw), lambda i: (0, i // parts))),
        scratch_shapes=[pltpu.VMEM((tm, D_MODEL), F32), pltpu.VMEM((8, 8, D_MODEL), F32),
                        pltpu.VMEM((nsub, s_len + 2 * CONV_PAD, 128), F32), pltpu.VMEM((nsub, 32, 8, 128), F32)],
        compiler_params=_params(dimension_semantics=("arbitrary",)),
    )(dz4, dzc, w12, x, dx2, mod, norm_g, dq, p)


def _rs_shard_block(i, nj):
    g = jnp.where(i < 2 * nj, i % 2, jnp.where(i < 3 * nj, 2, 3))
    j = jnp.where(i < 2 * nj, i // 2, jnp.where(i < 3 * nj, i - 2 * nj, i - 3 * nj))
    return g, j


def _rs_block(i, k, nj):
    g, j = _rs_shard_block(i, nj)
    return nj * jnp.bitwise_xor(k, 3 - g) + j


def _rs_scratch(nj, nr, nc):
    nblk = N_CHIPS * nj
    return [pltpu.VMEM((2, 2, nr, nc), F32),
            pltpu.VMEM((nblk, nr, nc), F32),
            pltpu.VMEM((2 * nj, nr, nc), BF16),
            pltpu.VMEM((2 * nj, nr, nc), BF16),
            pltpu.VMEM((nj, nr, nc), BF16),
            pltpu.VMEM((nj, nr, nc), BF16),
            pltpu.VMEM((nj, nr, nc), F32),
            pltpu.SemaphoreType.DMA,
            pltpu.SemaphoreType.DMA((nblk,)), pltpu.SemaphoreType.DMA((nblk,)),
            pltpu.SemaphoreType.DMA((2 * nj,)), pltpu.SemaphoreType.DMA((2 * nj,)),
            pltpu.SemaphoreType.DMA((nj,)), pltpu.SemaphoreType.DMA((nj,)),
            pltpu.SemaphoreType.DMA, pltpu.SemaphoreType.DMA]


def _rs_step(i, kk, kt, nj, nr, nc, partial_fn, out_ref, scr):
    nblk = N_CHIPS * nj
    (acc, recv_a, send1, recv1, send2, recv2, own_ps,
     lsem, psend, precv, s1send, s1recv, s2send, s2recv, fsend, frecv) = scr
    x, y, c = _pos()
    sib = (x, y, 1 - c)
    first = (jnp.bitwise_xor(x, 1 - c), jnp.bitwise_xor(y, c), c)
    second = (jnp.bitwise_xor(x, c), jnp.bitwise_xor(y, 1 - c), c)
    slot = i % 2

    @pl.when(kk == 0)
    def _():
        acc[slot] = jnp.zeros((2, nr, nc), F32)

    def pair_copy(ii, sl):
        return _rcopy(acc.at[sl, 1 - c], recv_a.at[ii], psend.at[ii], precv.at[ii], sib)

    def pair_sum(ii, sl):
        cp = pair_copy(ii, sl)
        cp.wait_recv()
        cp.wait_send()
        return acc[sl, c] + recv_a[ii]

    def stage1_copy(src, dst):
        return _rcopy(send1.at[src], recv1.at[dst], s1send.at[src], s1recv.at[dst], first)

    def stage2_copy(j):
        return _rcopy(send2.at[j], recv2.at[j], s2send.at[j], s2recv.at[j], second)

    def finalize(ii, sl):
        ps = pair_sum(ii, sl)
        g, j = _rs_shard_block(ii, nj)

        @pl.when(g == 0)
        def _():
            send1[j] = ps.astype(BF16)
            stage1_copy(j, nj + j).start()

        @pl.when(g == 1 + c)
        def _():
            send1[nj + j] = ps.astype(BF16)
            stage1_copy(nj + j, j).start()

        @pl.when(g == 2 - c)
        def _():
            stage1_copy(0, nj + j).wait_recv()
            send2[j] = (ps + recv1[nj + j].astype(F32)).astype(BF16)
            stage2_copy(j).start()

        @pl.when(g == 3)
        def _():
            own_ps[j] = ps

    @pl.when(jnp.logical_and(kk == kt - 1, i >= 1))
    def _():
        finalize(i - 1, 1 - slot)

    def add_partial(res):
        acc[slot, 0] += res[0:nr]
        acc[slot, 1] += res[nr:2 * nr]
    partial_fn(add_partial)

    @pl.when(kk == kt - 1)
    def _():
        pair_copy(i, slot).start()

        @pl.when(i == nblk - 1)
        def _():
            own_ps[nj - 1] = pair_sum(nblk - 1, (nblk - 1) % 2)
            for j in range(nj):
                stage1_copy(0, j).wait_recv()
                stage2_copy(j).wait_recv()
                own_ps[j] = (own_ps[j] + recv1[j].astype(F32)) + recv2[j].astype(F32)
            loc = pltpu.make_async_copy(own_ps, out_ref.at[:, c], lsem)
            loc.start()
            swap = _rcopy(own_ps, out_ref.at[:, c], fsend, frecv, sib)
            swap.start()
            loc.wait()
            swap.wait_send()
            _rcopy(own_ps, out_ref.at[:, 1 - c], fsend, frecv, sib).wait_recv()
            for s in range(2 * nj):
                stage1_copy(s, 0).wait_send()
            for j in range(nj):
                stage2_copy(j).wait_send()


def _grad_w_in(kidx, h, dz4, dzc, sm2, dcw):
    s_len = h.shape[0]
    tk = min(K_TILE, s_len)
    kt = s_len // tk
    nj, nr, nc = 3, 512, 512
    nblk = N_CHIPS * nj

    def body(k_ref, a_ref, b_ref, bc_ref, sm2_ref, dcw_in, out_ref, pk2_ref, dcwsh_ref, dcb_ref, *scr):
        del k_ref
        b_pk2, b_dcw, r_pk2, r_dcw, bsend, brecv = scr[-6:]
        i = pl.program_id(0)
        kk = pl.program_id(1)
        x, y, c = _pos()
        k = 2 * x + y
        m = _rs_block(i, k, nj)

        @pl.when(jnp.logical_and(kk == 0, i == 0))
        def _():
            b = 4 * x + 2 * y + c
            b_pk2[0:8, :] = _onehot_rows(sm2_ref[0:1, :], b)
            b_pk2[8:16, :] = _onehot_rows(sm2_ref[1:2, :], b)
            b_pk2[16:17, :] = sm2_ref[2:3, :]
            b_pk2[17:24, :] = jnp.zeros((7, D_MODEL), F32)
            b_dcw[...] = dcw_in[...]

        for e in range(4):
            @pl.when(jnp.logical_and(kk == 0, i == (e * nblk) // 4))
            def _():
                _butterfly_event(e, [b_pk2, b_dcw], [r_pk2, r_dcw], bsend, brecv, _all_reduce_partners())

        @pl.when(jnp.logical_and(kk == 0, i == (3 * nblk) // 4))
        def _():
            pk2_ref[...] = b_pk2[...]
            sel = jnp.zeros((32, 256), F32)
            for kc in range(N_CHIPS):
                sel = jnp.where(k == kc, b_dcw[:, kc * 256:(kc + 1) * 256], sel)
            dcwsh_ref[...] = sel
            dcb_ref[...] = b_dcw[31:32, :]

        def partial(add):
            rows = pl.ds(pl.multiple_of(kk * tk, tk), tk)

            @pl.when(m >= 4)
            def _():
                add(_dot_tn(a_ref[rows, :], b_ref[0]))

            @pl.when(m < 4)
            def _():
                add(_dot_tn(a_ref[rows, :], jnp.concatenate([bc_ref[blk, 0] for blk in range(4)], axis=1)))
        _rs_step(i, kk, kt, nj, nr, nc, partial, out_ref, scr[:-6])

    def dz4_index(i, kk, k_ref):
        m = _rs_block(i, k_ref[0], nj)
        use = m >= 4
        return (jnp.where(use, m // 2 - 2, 0), jnp.where(use, kk, 0), jnp.where(use, m % 2, 0))

    def dzc_index(i, kk, k_ref):
        m = _rs_block(i, k_ref[0], nj)
        use = m < 4
        return (jnp.where(use, m % 2, 0), jnp.where(use, m // 2, 0), jnp.where(use, kk, 0), 0)

    const2 = lambda i, kk, k_ref: (0, 0)
    small_out = [(24, D_MODEL), (32, 256), (1, D_MODEL)]
    small_buf = [(24, D_MODEL), (32, D_MODEL)]
    grid_spec = pltpu.PrefetchScalarGridSpec(
        num_scalar_prefetch=1,
        grid=(nblk, kt),
        in_specs=[pl.BlockSpec((s_len, D_MODEL), const2, pipeline_mode=pl.Buffered(1)),
                  pl.BlockSpec((1, tk, 512), dz4_index),
                  pl.BlockSpec((4, 1, tk, 128), dzc_index),
                  pl.BlockSpec((8, D_MODEL), const2),
                  pl.BlockSpec((32, D_MODEL), const2)],
        out_specs=(pl.BlockSpec(memory_space=pl.ANY),) + tuple(pl.BlockSpec(s, const2) for s in small_out),
        scratch_shapes=_rs_scratch(nj, nr, nc)
        + [pltpu.VMEM(s, F32) for s in small_buf] + [pltpu.VMEM((3,) + s, F32) for s in small_buf]
        + [pltpu.SemaphoreType.DMA((6,)), pltpu.SemaphoreType.DMA((6,))])
    return pl.pallas_call(
        body, name="grad_w_in",
        grid_spec=grid_spec,
        out_shape=(jax.ShapeDtypeStruct((nj, 2, nr, nc), F32),) + tuple(jax.ShapeDtypeStruct(s, F32) for s in small_out),
        compiler_params=_params(dimension_semantics=("arbitrary", "arbitrary")),
    )(kidx, h, dz4, dzc, sm2, dcw)


def _adamw_math(w, g, m, v):
    m = ADAM_B1 * m + (1.0 - ADAM_B1) * g
    v = ADAM_B2 * v + (1.0 - ADAM_B2) * (g * g)
    m_hat = m / (1.0 - ADAM_B1 ** ADAM_STEP)
    v_hat = v / (1.0 - ADAM_B2 ** ADAM_STEP)
    delta = -ADAM_LR * (m_hat / (jnp.sqrt(v_hat) + ADAM_EPS) + ADAM_WD * w)
    return delta, m, v


def _adamw_blocked(w, m, v, g4, name):
    nj, _, nr, nc = g4.shape

    def body(w_ref, m_ref, v_ref, g_ref, go_ref, d_ref, mo_ref, vo_ref):
        g = g_ref[0, 0]
        d, mn, vn = _adamw_math(w_ref[...], g, m_ref[...], v_ref[...])
        go_ref[...] = g
        d_ref[...] = d
        mo_ref[...] = mn
        vo_ref[...] = vn

    blk = pl.BlockSpec((nr, nc), lambda j, hf: (hf, j))
    return pl.pallas_call(
        body, name=name,
        grid=(nj, 2),
        out_shape=tuple(jax.ShapeDtypeStruct(w.shape, F32) for _ in range(4)),
        in_specs=[blk, blk, blk, pl.BlockSpec((1, 1, nr, nc), lambda j, hf: (j, hf, 0, 0))],
        out_specs=(blk, blk, blk, blk),
        compiler_params=_params(dimension_semantics=("arbitrary", "arbitrary")),
    )(w, m, v, g4)


def _adamw_w_ada(cact, pk1, pk2, w, m, v):
    rb = 256

    def body(cact_ref, pk1_ref, pk2_ref, w_ref, m_ref, v_ref, g_ref, d_ref, mo_ref, vo_ref, gb_ref, dmod, sel_scr):
        @pl.when(pl.program_id(0) == 0)
        def _():
            x, y, _ = _pos()
            k = 2 * x + y
            dmod[:, 0:D_MODEL] = pk2_ref[0:8, :]
            dmod[:, D_MODEL:2 * D_MODEL] = pk2_ref[8:16, :]
            dmod[:, 2 * D_MODEL:3 * D_MODEL] = pk1_ref[0:8, :]
            gb_ref[...] = jnp.sum(dmod[...], axis=0, keepdims=True)
            sel = jnp.zeros((8, 768), F32)
            for kk in range(N_CHIPS):
                sel = jnp.where(k == kk, dmod[:, kk * 768:(kk + 1) * 768], sel)
            sel_scr[...] = sel

        g = _dot_tn(cact_ref[...].astype(BF16), sel_scr[...].astype(BF16))
        d, mn, vn = _adamw_math(w_ref[...], g, m_ref[...], v_ref[...])
        g_ref[...] = g
        d_ref[...] = d
        mo_ref[...] = mn
        vo_ref[...] = vn

    blk = pl.BlockSpec((rb, 768), lambda i: (i, 0))
    const2 = lambda i: (0, 0)
    return pl.pallas_call(
        body, name="adamw_w_ada",
        grid=(D_MODEL // rb,),
        out_shape=tuple(jax.ShapeDtypeStruct(w.shape, F32) for _ in range(4)) + (
            jax.ShapeDtypeStruct((1, 3 * D_MODEL), F32),),
        in_specs=[pl.BlockSpec((8, rb), lambda i: (0, i)), pl.BlockSpec((16, D_MODEL), const2),
                  pl.BlockSpec((24, D_MODEL), const2), blk, blk, blk],
        out_specs=(blk, blk, blk, blk, pl.BlockSpec((1, 3 * D_MODEL), const2)),
        scratch_shapes=[pltpu.VMEM((8, 3 * D_MODEL), F32), pltpu.VMEM((8, 768), F32)],
        compiler_params=_params(dimension_semantics=("arbitrary",)),
    )(cact, pk1, pk2, w, m, v)


def _adamw_small(items, conv, packs, rows):
    n, nr = len(items), len(rows)

    def body(*refs):
        refs = list(refs)
        take = lambda k: [refs.pop(0) for _ in range(k)]
        ins, (cw_ref, cg_ref, cm_ref, cv_ref), pk_refs, row_ins = take(4 * n), take(4), take(len(packs)), take(3 * nr)
        outs, (cgo_ref, cd_ref, cmo_ref, cvo_ref), row_outs = take(3 * n), take(4), take(4 * nr)
        for i in range(n):
            w_ref, g_ref, m_ref, v_ref = ins[4 * i:4 * i + 4]
            d, mn, vn = _adamw_math(w_ref[...], g_ref[...], m_ref[...], v_ref[...])
            outs[3 * i][...] = d
            outs[3 * i + 1][...] = mn
            outs[3 * i + 2][...] = vn
        for r in range(CONV_WIDTH):
            g = cg_ref[r:r + 1, :]
            d, mn, vn = _adamw_math(cw_ref[0, r], g, cm_ref[0, r], cv_ref[0, r])
            cgo_ref[0, r] = g
            cd_ref[0, r] = d
            cmo_ref[0, r] = mn
            cvo_ref[0, r] = vn
        for i, (_, _, _, p, r) in enumerate(rows):
            w_ref, m_ref, v_ref = row_ins[3 * i:3 * i + 3]
            g = pk_refs[p][r:r + 1, :]
            d, mn, vn = _adamw_math(w_ref[...], g, m_ref[...], v_ref[...])
            row_outs[4 * i][...] = g
            row_outs[4 * i + 1][...] = d
            row_outs[4 * i + 2][...] = mn
            row_outs[4 * i + 3][...] = vn

    flat = [a for it in items for a in it] + list(conv) + list(packs) + [a for it in rows for a in it[:3]]
    n_out = 3 * n + 4 + 4 * nr
    outs = pl.pallas_call(
        body, name="adamw_small",
        out_shape=tuple(jax.ShapeDtypeStruct(it[0].shape, F32) for it in items for _ in range(3))
        + tuple(jax.ShapeDtypeStruct(conv[0].shape, F32) for _ in range(4))
        + tuple(jax.ShapeDtypeStruct(it[0].shape, F32) for it in rows for _ in range(4)),
        in_specs=[_vmem()] * len(flat),
        out_specs=tuple(_vmem() for _ in range(n_out)),
        compiler_params=_params(),
    )(*flat)
    return ([tuple(outs[3 * i:3 * i + 3]) for i in range(n)], tuple(outs[3 * n:3 * n + 4]),
            [tuple(outs[3 * n + 4 + 4 * i:3 * n + 8 + 4 * i]) for i in range(nr)])


def kernel(x, c, w_ada, b_ada, norm_g, w_in, conv_w, conv_b, conv_ln_g, conv_ln_b, sg_ln_g, sg_ln_b, w_s, b_s, w_out, final_g, loss_target, m_w_ada, m_b_ada, m_norm_g, m_w_in, m_conv_w, m_conv_b, m_conv_ln_g, m_conv_ln_b, m_sg_ln_g, m_sg_ln_b, m_w_s, m_b_s, m_w_out, m_final_g, v_w_ada, v_b_ada, v_norm_g, v_w_in, v_conv_w, v_conv_b, v_conv_ln_g, v_conv_ln_b, v_sg_ln_g, v_sg_ln_b, v_w_s, v_b_s, v_w_out, v_final_g):
    s_len = x.shape[1]
    x2d = x[0]
    tgt = loss_target[0]
    row = lambda a: a.reshape(1, -1)

    kidx = (2 * lax.axis_index("x") + lax.axis_index("y")).astype(jnp.int32).reshape(1)
    cw_sh = jnp.pad(conv_w.reshape(CONV_WIDTH, 256), ((0, 1), (0, 0)))
    h, z6, mod, cact, cw4, w_in12 = _front(kidx, x2d, c, w_ada[0], b_ada, norm_g, w_in[0], cw_sh)
    w12 = w_in12.reshape(12, D_MODEL, 512)
    q, p_cm, sig_cm, t2_cm, w_out4 = _conv_fwd(z6, cw4, conv_b, w_out[0])
    w_out_full = w_out4.reshape(2 * D_MODEL, D_MODEL)
    ln_rows = (conv_ln_g, conv_ln_b, sg_ln_g, sg_ln_b, row(final_g))
    bs_exp = jnp.repeat(b_s[0].T, CHUNK, axis=1)
    dz4, dq, dx2, ycat, dy, dws, sm1, dbs = _middle(x2d, z6, q, tgt, ln_rows, mod, w_s[0], bs_exp, w_out_full)
    dzc, pk1, dws_r, dbs_r, loss_t, g_w_out4 = _conv_bwd(
        kidx, dq, sig_cm, t2_cm, cw4, sm1, dws.reshape(D_MODEL, CHUNK), dbs, ycat, dy)
    grad_x, sm2, dcw = _bwd_in(dz4, dzc, w12, x2d, dx2, mod, norm_g, dq, p_cm)
    g_w_in4, pk2, dcw_sh, dcb = _grad_w_in(kidx, h, dz4, dzc, sm2, dcw)

    g_w_in, d_w_in, nm_w_in, nv_w_in = _adamw_blocked(w_in[0], m_w_in[0], v_w_in[0], g_w_in4, "adamw_w_in")
    g_w_out, d_w_out, nm_w_out, nv_w_out = _adamw_blocked(w_out[0], m_w_out[0], v_w_out[0], g_w_out4, "adamw_w_out")
    g_w_ada, d_w_ada, nm_w_ada, nv_w_ada, g_b_ada = _adamw_w_ada(cact, pk1, pk2, w_ada[0], m_w_ada[0], v_w_ada[0])

    loss = loss_t.reshape(())
    g_w_s = dws_r
    g_b_s = dbs_r[:, :HEADS].T
    small = [
        (b_ada, g_b_ada, m_b_ada, v_b_ada),
        (conv_b, dcb, m_conv_b, v_conv_b),
        (w_s.reshape(D_MODEL, CHUNK), g_w_s, m_w_s.reshape(D_MODEL, CHUNK), v_w_s.reshape(D_MODEL, CHUNK)),
        (b_s[0], g_b_s, m_b_s[0], v_b_s[0]),
    ]
    rows = [
        (norm_g, m_norm_g, v_norm_g, 1, 16),
        (conv_ln_g, m_conv_ln_g, v_conv_ln_g, 0, 8),
        (conv_ln_b, m_conv_ln_b, v_conv_ln_b, 0, 9),
        (sg_ln_g, m_sg_ln_g, v_sg_ln_g, 0, 10),
        (sg_ln_b, m_sg_ln_b, v_sg_ln_b, 0, 11),
        (row(final_g), row(m_final_g), row(v_final_g), 0, 12),
    ]
    upd, (g_conv_w, d_conv_w, nm_conv_w, nv_conv_w), row_upd = _adamw_small(
        small, (conv_w, dcw_sh, m_conv_w, v_conv_w), (pk1, pk2), rows)
    g_norm_g, g_cln_g, g_cln_b, g_sln_g, g_sln_b, g_final = (t[0] for t in row_upd)

    shapes = [w_ada.shape, b_ada.shape, norm_g.shape, w_in.shape, conv_w.shape, conv_b.shape, conv_ln_g.shape,
              conv_ln_b.shape, sg_ln_g.shape, sg_ln_b.shape, w_s.shape, b_s.shape, w_out.shape, final_g.shape]
    grads = [g_w_ada, g_b_ada, g_norm_g, g_w_in, g_conv_w, dcb, g_cln_g, g_cln_b, g_sln_g, g_sln_b, g_w_s, g_b_s,
             g_w_out, g_final]
    big = {0: (d_w_ada, nm_w_ada, nv_w_ada), 3: (d_w_in, nm_w_in, nv_w_in), 4: (d_conv_w, nm_conv_w, nv_conv_w),
           12: (d_w_out, nm_w_out, nv_w_out)}
    trip = [None] * 14
    for i, t in big.items():
        trip[i] = t
    for i, t in zip([1, 5, 10, 11], upd):
        trip[i] = t
    for i, t in zip([2, 6, 7, 8, 9, 13], row_upd):
        trip[i] = t[1:]
    fit = lambda arrs: [a.reshape(s) for a, s in zip(arrs, shapes)]
    return (loss, grad_x.reshape(x.shape), *fit(grads), *fit([t[0] for t in trip]), *fit([t[1] for t in trip]),
            *fit([t[2] for t in trip]))
```
